```python
import jax, jax.numpy as jnp
from jax import lax
import numpy as np

D_MODEL = 1024
BATCH = 16
SEQ = 2048
DEPTH = 1

N_MEM = 256
EPS = 1e-6
POOL_WINDOWS = (2, 4, 8, 16)
N_POOL_GROUPS = len(POOL_WINDOWS)
POOL_WIDTH = D_MODEL // 2
POOL_GC = POOL_WIDTH // N_POOL_GROUPS
FOX_HEADS = 8
FOX_DH = 64
FOX_WIDTH = FOX_HEADS * FOX_DH
Q_BLOCK = 128
GATE_WIDTH = 2 * D_MODEL
IN_SPLITS = (POOL_WIDTH, POOL_WIDTH + FOX_WIDTH, POOL_WIDTH + 2 * FOX_WIDTH,
             POOL_WIDTH + 3 * FOX_WIDTH, POOL_WIDTH + 3 * FOX_WIDTH + FOX_HEADS)
IN_COLS = POOL_WIDTH + 3 * FOX_WIDTH + FOX_HEADS + GATE_WIDTH
X_HEADS = 4
X_DH = 128
X_WIDTH = X_HEADS * X_DH
D_FF = ((8 * D_MODEL // 3 + 255) // 256) * 256

kernel_name = "hybrid_pool_fox_gated_block"


def rmsnorm(x, g):
    xf = x.astype(jnp.float32)
    y = xf * lax.rsqrt(jnp.mean(xf * xf, axis=-1, keepdims=True) + EPS)
    return (y * g.astype(jnp.float32)).astype(x.dtype)


def pool_mixer(u, pool_w, pool_scale):
    B, S, _ = u.shape
    uf = u.astype(jnp.float32)
    cs = jnp.cumsum(uf, axis=1)
    t = jnp.arange(S)
    outs = []
    for g, w in enumerate(POOL_WINDOWS):
        sl = slice(g * POOL_GC, (g + 1) * POOL_GC)
        cs_g = cs[..., sl]
        shifted = jnp.pad(cs_g, ((0, 0), (w, 0), (0, 0)))[:, :S]
        cnt = jnp.minimum(t + 1, w).astype(jnp.float32)[None, :, None]
        outs.append((cs_g - shifted) / cnt - uf[..., sl])
    d = jnp.stack(outs, axis=2).astype(u.dtype)
    y = jnp.einsum('bsgc,gcd->bsgd', d, pool_w).reshape(B, S, POOL_WIDTH)
    return y * pool_scale


def forgetting_attention(q, k, v, f_logit):
    B, S, H, dh = q.shape
    q = q.transpose(0, 2, 1, 3)
    k = k.transpose(0, 2, 1, 3)
    v = v.transpose(0, 2, 1, 3)
    c = jnp.cumsum(jax.nn.log_sigmoid(f_logit.astype(jnp.float32)), axis=1).transpose(0, 2, 1)
    scale = dh ** -0.5
    outs = []
    for i in range(S // Q_BLOCK):
        q0, q1 = i * Q_BLOCK, (i + 1) * Q_BLOCK
        qb = q[:, :, q0:q1]
        kp, vp, cp = k[:, :, :q1], v[:, :, :q1], c[:, :, :q1]
        s = jnp.einsum('bhqd,bhkd->bhqk', qb, kp).astype(jnp.float32) * scale
        s = s + c[:, :, q0:q1, None] - cp[:, :, None, :]
        mask = (q0 + jnp.arange(Q_BLOCK))[:, None] >= jnp.arange(q1)[None, :]
        s = jnp.where(mask[None, None], s, -1e30)
        p = jax.nn.softmax(s, axis=-1).astype(v.dtype)
        outs.append(jnp.einsum('bhqk,bhkd->bhqd', p, vp))
    o = jnp.concatenate(outs, axis=2)
    return o.transpose(0, 2, 1, 3).reshape(B, S, H * dh)


def memory_cross_attention(h, mem_n, w_xq, w_xkv, w_xo):
    B, S, _ = h.shape
    M = mem_n.shape[1]
    q = (h @ w_xq).reshape(B, S, X_HEADS, X_DH)
    kv = (mem_n @ w_xkv).reshape(B, M, 2, X_HEADS, X_DH)
    k, v = kv[:, :, 0], kv[:, :, 1]
    s = jnp.einsum('bshd,bmhd->bhsm', q, k).astype(jnp.float32) * (X_DH ** -0.5)
    p = jax.nn.softmax(s, axis=-1).astype(v.dtype)
    o = jnp.einsum('bhsm,bmhd->bshd', p, v).reshape(B, S, X_WIDTH)
    return o @ w_xo


def _fwd_setup_inputs(seed: int = 0) -> dict:
    key = jax.random.key(seed)
    ks = jax.random.split(key, 24)
    f32 = jnp.float32

    def nrm(k, shape, fan_in):
        return jax.random.normal(k, shape, f32) * (fan_in ** -0.5)

    def gain(k, shape):
        return 1.0 + 0.05 * jax.random.normal(k, shape, f32)

    L = DEPTH
    return {
        "x": jax.random.normal(ks[0], (BATCH, SEQ, D_MODEL), f32),
        "mem": jax.random.normal(ks[1], (BATCH, N_MEM, D_MODEL), f32),
        "norm_mix_g": gain(ks[2], (L, D_MODEL)),
        "w_in": nrm(ks[3], (L, D_MODEL, IN_COLS), D_MODEL),
        "b_forget": 3.0 + 0.5 * jax.random.normal(ks[4], (L, FOX_HEADS), f32),
        "b_gate": 0.02 * jax.random.normal(ks[5], (L, GATE_WIDTH), f32),
        "pool_w": nrm(ks[6], (L, N_POOL_GROUPS, POOL_GC, POOL_GC), POOL_GC),
        "pool_scale": gain(ks[7], (L, POOL_WIDTH)),
        "w_pool_out": nrm(ks[8], (L, POOL_WIDTH, D_MODEL), POOL_WIDTH),
        "w_fox_out": nrm(ks[9], (L, FOX_WIDTH, D_MODEL), FOX_WIDTH),
        "w_out": nrm(ks[10], (L, D_MODEL, D_MODEL), D_MODEL),
        "norm_x_g": gain(ks[11], (L, D_MODEL)),
        "norm_mem_g": gain(ks[12], (L, D_MODEL)),
        "w_xq": nrm(ks[13], (L, D_MODEL, X_WIDTH), D_MODEL),
        "w_xkv": nrm(ks[14], (L, D_MODEL, 2 * X_WIDTH), D_MODEL),
        "w_xo": nrm(ks[15], (L, X_WIDTH, D_MODEL), X_WIDTH),
        "norm_ffn_g": gain(ks[16], (L, D_MODEL)),
        "w_ffn_in": nrm(ks[17], (L, D_MODEL, 2 * D_FF), D_MODEL),
        "w_ffn_out": nrm(ks[18], (L, D_FF, D_MODEL), D_FF),
        "norm_final_g": gain(ks[19], (D_MODEL,)),
    }


def _fwd_reference(x, mem, norm_mix_g, w_in, b_forget, b_gate, pool_w, pool_scale,
              w_pool_out, w_fox_out, w_out, norm_x_g, norm_mem_g, w_xq, w_xkv, w_xo,
              norm_ffn_g, w_ffn_in, w_ffn_out, norm_final_g):
    B, S, D = x.shape
    for l in range(DEPTH):
        h = rmsnorm(x, norm_mix_g[l])
        proj = h @ w_in[l]
        u_pool, q, k, v, f_logit, g_logit = jnp.split(proj, IN_SPLITS, axis=-1)
        y_pool = pool_mixer(u_pool, pool_w[l], pool_scale[l]) @ w_pool_out[l]
        y_fox = forgetting_attention(q.reshape(B, S, FOX_HEADS, FOX_DH),
                                     k.reshape(B, S, FOX_HEADS, FOX_DH),
                                     v.reshape(B, S, FOX_HEADS, FOX_DH),
                                     f_logit + b_forget[l]) @ w_fox_out[l]
        gates = jax.nn.sigmoid(g_logit + b_gate[l])
        g_pool, g_fox = gates[..., :D], gates[..., D:]
        x = x + (g_pool * y_pool + g_fox * y_fox) @ w_out[l]
        hx = rmsnorm(x, norm_x_g[l])
        mem_n = rmsnorm(mem, norm_mem_g[l])
        x = x + memory_cross_attention(hx, mem_n, w_xq[l], w_xkv[l], w_xo[l])
        hf = rmsnorm(x, norm_ffn_g[l])
        gu = hf @ w_ffn_in[l]
        gt, up = gu[..., :D_FF], gu[..., D_FF:]
        x = x + (jax.nn.silu(gt) * up) @ w_ffn_out[l]
    return rmsnorm(x, norm_final_g)


import jax as _jax
import jax.numpy as _jnp

TWIN_FORMAT = 'train_step'
FWD_PARAMS = ['x', 'mem', 'norm_mix_g', 'w_in', 'b_forget', 'b_gate', 'pool_w', 'pool_scale', 'w_pool_out', 'w_fox_out', 'w_out', 'norm_x_g', 'norm_mem_g', 'w_xq', 'w_xkv', 'w_xo', 'norm_ffn_g', 'w_ffn_in', 'w_ffn_out', 'norm_final_g']
TWIN_WEIGHTS = ['norm_mix_g', 'w_in', 'b_forget', 'b_gate', 'pool_w', 'pool_scale', 'w_pool_out', 'w_fox_out', 'w_out', 'norm_x_g', 'norm_mem_g', 'w_xq', 'w_xkv', 'w_xo', 'norm_ffn_g', 'w_ffn_in', 'w_ffn_out', 'norm_final_g']
TWIN_DIFF_INPUT = 'x'
TWIN_INPUTS = ['x', 'mem', 'norm_mix_g', 'w_in', 'b_forget', 'b_gate', 'pool_w', 'pool_scale', 'w_pool_out', 'w_fox_out', 'w_out', 'norm_x_g', 'norm_mem_g', 'w_xq', 'w_xkv', 'w_xo', 'norm_ffn_g', 'w_ffn_in', 'w_ffn_out', 'norm_final_g', 'loss_target', 'm_norm_mix_g', 'm_w_in', 'm_b_forget', 'm_b_gate', 'm_pool_w', 'm_pool_scale', 'm_w_pool_out', 'm_w_fox_out', 'm_w_out', 'm_norm_x_g', 'm_norm_mem_g', 'm_w_xq', 'm_w_xkv', 'm_w_xo', 'm_norm_ffn_g', 'm_w_ffn_in', 'm_w_ffn_out', 'm_norm_final_g', 'v_norm_mix_g', 'v_w_in', 'v_b_forget', 'v_b_gate', 'v_pool_w', 'v_pool_scale', 'v_w_pool_out', 'v_w_fox_out', 'v_w_out', 'v_norm_x_g', 'v_norm_mem_g', 'v_w_xq', 'v_w_xkv', 'v_w_xo', 'v_norm_ffn_g', 'v_w_ffn_in', 'v_w_ffn_out', 'v_norm_final_g']
TWIN_OUTPUTS = ['loss', 'grad_x', 'grad_norm_mix_g', 'grad_w_in', 'grad_b_forget', 'grad_b_gate', 'grad_pool_w', 'grad_pool_scale', 'grad_w_pool_out', 'grad_w_fox_out', 'grad_w_out', 'grad_norm_x_g', 'grad_norm_mem_g', 'grad_w_xq', 'grad_w_xkv', 'grad_w_xo', 'grad_norm_ffn_g', 'grad_w_ffn_in', 'grad_w_ffn_out', 'grad_norm_final_g', 'delta_norm_mix_g', 'delta_w_in', 'delta_b_forget', 'delta_b_gate', 'delta_pool_w', 'delta_pool_scale', 'delta_w_pool_out', 'delta_w_fox_out', 'delta_w_out', 'delta_norm_x_g', 'delta_norm_mem_g', 'delta_w_xq', 'delta_w_xkv', 'delta_w_xo', 'delta_norm_ffn_g', 'delta_w_ffn_in', 'delta_w_ffn_out', 'delta_norm_final_g', 'new_m_norm_mix_g', 'new_m_w_in', 'new_m_b_forget', 'new_m_b_gate', 'new_m_pool_w', 'new_m_pool_scale', 'new_m_w_pool_out', 'new_m_w_fox_out', 'new_m_w_out', 'new_m_norm_x_g', 'new_m_norm_mem_g', 'new_m_w_xq', 'new_m_w_xkv', 'new_m_w_xo', 'new_m_norm_ffn_g', 'new_m_w_ffn_in', 'new_m_w_ffn_out', 'new_m_norm_final_g', 'new_v_norm_mix_g', 'new_v_w_in', 'new_v_b_forget', 'new_v_b_gate', 'new_v_pool_w', 'new_v_pool_scale', 'new_v_w_pool_out', 'new_v_w_fox_out', 'new_v_w_out', 'new_v_norm_x_g', 'new_v_norm_mem_g', 'new_v_w_xq', 'new_v_w_xkv', 'new_v_w_xo', 'new_v_norm_ffn_g', 'new_v_w_ffn_in', 'new_v_w_ffn_out', 'new_v_norm_final_g']
TWIN_LEAF_KINDS = {'loss': 'loss', 'grad_x': 'grad_x', 'grad_norm_mix_g': 'grad_w', 'grad_w_in': 'grad_w', 'grad_b_forget': 'grad_w', 'grad_b_gate': 'grad_w', 'grad_pool_w': 'grad_w', 'grad_pool_scale': 'grad_w', 'grad_w_pool_out': 'grad_w', 'grad_w_fox_out': 'grad_w', 'grad_w_out': 'grad_w', 'grad_norm_x_g': 'grad_w', 'grad_norm_mem_g': 'grad_w', 'grad_w_xq': 'grad_w', 'grad_w_xkv': 'grad_w', 'grad_w_xo': 'grad_w', 'grad_norm_ffn_g': 'grad_w', 'grad_w_ffn_in': 'grad_w', 'grad_w_ffn_out': 'grad_w', 'grad_norm_final_g': 'grad_w', 'delta_norm_mix_g': 'delta_w', 'delta_w_in': 'delta_w', 'delta_b_forget': 'delta_w', 'delta_b_gate': 'delta_w', 'delta_pool_w': 'delta_w', 'delta_pool_scale': 'delta_w', 'delta_w_pool_out': 'delta_w', 'delta_w_fox_out': 'delta_w', 'delta_w_out': 'delta_w', 'delta_norm_x_g': 'delta_w', 'delta_norm_mem_g': 'delta_w', 'delta_w_xq': 'delta_w', 'delta_w_xkv': 'delta_w', 'delta_w_xo': 'delta_w', 'delta_norm_ffn_g': 'delta_w', 'delta_w_ffn_in': 'delta_w', 'delta_w_ffn_out': 'delta_w', 'delta_norm_final_g': 'delta_w', 'new_m_norm_mix_g': 'new_m', 'new_m_w_in': 'new_m', 'new_m_b_forget': 'new_m', 'new_m_b_gate': 'new_m', 'new_m_pool_w': 'new_m', 'new_m_pool_scale': 'new_m', 'new_m_w_pool_out': 'new_m', 'new_m_w_fox_out': 'new_m', 'new_m_w_out': 'new_m', 'new_m_norm_x_g': 'new_m', 'new_m_norm_mem_g': 'new_m', 'new_m_w_xq': 'new_m', 'new_m_w_xkv': 'new_m', 'new_m_w_xo': 'new_m', 'new_m_norm_ffn_g': 'new_m', 'new_m_w_ffn_in': 'new_m', 'new_m_w_ffn_out': 'new_m', 'new_m_norm_final_g': 'new_m', 'new_v_norm_mix_g': 'new_v', 'new_v_w_in': 'new_v', 'new_v_b_forget': 'new_v', 'new_v_b_gate': 'new_v', 'new_v_pool_w': 'new_v', 'new_v_pool_scale': 'new_v', 'new_v_w_pool_out': 'new_v', 'new_v_w_fox_out': 'new_v', 'new_v_w_out': 'new_v', 'new_v_norm_x_g': 'new_v', 'new_v_norm_mem_g': 'new_v', 'new_v_w_xq': 'new_v', 'new_v_w_xkv': 'new_v', 'new_v_w_xo': 'new_v', 'new_v_norm_ffn_g': 'new_v', 'new_v_w_ffn_in': 'new_v', 'new_v_w_ffn_out': 'new_v', 'new_v_norm_final_g': 'new_v'}


def _forward(args):
    return _fwd_reference(*[args[k] for k in FWD_PARAMS])


def _output_shape():
    out = _jax.eval_shape(lambda: _forward(_fwd_setup_inputs(0)))
    return out.shape, out.dtype

N_MICROBATCH = 1
ADAM_LR = 0.001
ADAM_B1 = 0.9
ADAM_B2 = 0.999
ADAM_EPS = 1e-08
ADAM_WD = 0.01
ADAM_STEP = 10
PER_EXAMPLE_BATCH_AXIS = {'x': 0, 'mem': 0, 'loss_target': 0}
SHARED_INPUTS = []
_WEIGHT_DTYPES = {'norm_mix_g': _jnp.float32, 'w_in': _jnp.float32, 'b_forget': _jnp.float32, 'b_gate': _jnp.float32, 'pool_w': _jnp.float32, 'pool_scale': _jnp.float32, 'w_pool_out': _jnp.float32, 'w_fox_out': _jnp.float32, 'w_out': _jnp.float32, 'norm_x_g': _jnp.float32, 'norm_mem_g': _jnp.float32, 'w_xq': _jnp.float32, 'w_xkv': _jnp.float32, 'w_xo': _jnp.float32, 'norm_ffn_g': _jnp.float32, 'w_ffn_in': _jnp.float32, 'w_ffn_out': _jnp.float32, 'norm_final_g': _jnp.float32}
MOMENT_SCALE = {'norm_mix_g': 1.008351e-01, 'w_in': 5.108254e-02, 'b_forget': 1.635198e-01, 'b_gate': 2.473398e-02, 'pool_w': 1.217897e-01, 'pool_scale': 1.202614e-01, 'w_pool_out': 8.545305e-02, 'w_fox_out': 3.171647e-02, 'w_out': 9.054123e-02, 'norm_x_g': 2.051767e-02, 'norm_mem_g': 2.962579e-02, 'w_xq': 2.695120e-02, 'w_xkv': 2.705539e-02, 'w_xo': 1.884497e-02, 'norm_ffn_g': 1.221636e-01, 'w_ffn_in': 5.129650e-02, 'w_ffn_out': 8.432621e-02, 'norm_final_g': 3.200041e+01}


def _to_microbatches(a, axis):
    t = _jnp.moveaxis(a, axis, 0)
    t = t.reshape((N_MICROBATCH, t.shape[0] // N_MICROBATCH) + t.shape[1:])
    return _jnp.moveaxis(t, 1, axis + 1)


def setup_inputs(seed: int = 0) -> dict:
    inp = _fwd_setup_inputs(seed)
    key = _jax.random.fold_in(_jax.random.key(seed), 7919)
    shape, _ = _output_shape()
    out = dict(inp)
    out["loss_target"] = _jax.random.normal(_jax.random.fold_in(key, 0), shape, _jnp.float32)
    for i, name in enumerate(TWIN_WEIGHTS):
        w = inp[name].astype(_jnp.float32)
        if MOMENT_SCALE is None:
            s = _jnp.sqrt(_jnp.mean(_jnp.square(w)) + 1e-30)
        else:
            s = MOMENT_SCALE[name]
        km, kv = _jax.random.split(_jax.random.fold_in(key, i + 1))
        out[name] = w
        out["m_" + name] = s * _jax.random.normal(km, w.shape, _jnp.float32)
        out["v_" + name] = (s * s) * _jax.random.uniform(kv, w.shape, _jnp.float32, 0.5, 1.5)
    if N_MICROBATCH > 1:
        for name, axis in PER_EXAMPLE_BATCH_AXIS.items():
            out[name] = _to_microbatches(out[name], axis)
    return {'x': out['x'], 'mem': out['mem'], 'norm_mix_g': out['norm_mix_g'], 'w_in': out['w_in'], 'b_forget': out['b_forget'], 'b_gate': out['b_gate'], 'pool_w': out['pool_w'], 'pool_scale': out['pool_scale'], 'w_pool_out': out['w_pool_out'], 'w_fox_out': out['w_fox_out'], 'w_out': out['w_out'], 'norm_x_g': out['norm_x_g'], 'norm_mem_g': out['norm_mem_g'], 'w_xq': out['w_xq'], 'w_xkv': out['w_xkv'], 'w_xo': out['w_xo'], 'norm_ffn_g': out['norm_ffn_g'], 'w_ffn_in': out['w_ffn_in'], 'w_ffn_out': out['w_ffn_out'], 'norm_final_g': out['norm_final_g'], 'loss_target': out['loss_target'], 'm_norm_mix_g': out['m_norm_mix_g'], 'm_w_in': out['m_w_in'], 'm_b_forget': out['m_b_forget'], 'm_b_gate': out['m_b_gate'], 'm_pool_w': out['m_pool_w'], 'm_pool_scale': out['m_pool_scale'], 'm_w_pool_out': out['m_w_pool_out'], 'm_w_fox_out': out['m_w_fox_out'], 'm_w_out': out['m_w_out'], 'm_norm_x_g': out['m_norm_x_g'], 'm_norm_mem_g': out['m_norm_mem_g'], 'm_w_xq': out['m_w_xq'], 'm_w_xkv': out['m_w_xkv'], 'm_w_xo': out['m_w_xo'], 'm_norm_ffn_g': out['m_norm_ffn_g'], 'm_w_ffn_in': out['m_w_ffn_in'], 'm_w_ffn_out': out['m_w_ffn_out'], 'm_norm_final_g': out['m_norm_final_g'], 'v_norm_mix_g': out['v_norm_mix_g'], 'v_w_in': out['v_w_in'], 'v_b_forget': out['v_b_forget'], 'v_b_gate': out['v_b_gate'], 'v_pool_w': out['v_pool_w'], 'v_pool_scale': out['v_pool_scale'], 'v_w_pool_out': out['v_w_pool_out'], 'v_w_fox_out': out['v_w_fox_out'], 'v_w_out': out['v_w_out'], 'v_norm_x_g': out['v_norm_x_g'], 'v_norm_mem_g': out['v_norm_mem_g'], 'v_w_xq': out['v_w_xq'], 'v_w_xkv': out['v_w_xkv'], 'v_w_xo': out['v_w_xo'], 'v_norm_ffn_g': out['v_norm_ffn_g'], 'v_w_ffn_in': out['v_w_ffn_in'], 'v_w_ffn_out': out['v_w_ffn_out'], 'v_norm_final_g': out['v_norm_final_g']}


def _loss(weights, diff, rest, loss_target):
    with _jax.named_scope("forward"):
        args = {**rest, TWIN_DIFF_INPUT: diff, **{k: w.astype(_WEIGHT_DTYPES[k]) for k, w in weights.items()}}
        y = _forward(args)
    with _jax.named_scope("loss_head"):
        err = _jnp.square(y.astype(_jnp.float32) - loss_target)
        return 0.5 * _jnp.sum(_jnp.mean(err, axis=-1)) if err.ndim else 0.5 * err


def _adamw(w, g, m, v):
    m = ADAM_B1 * m + (1.0 - ADAM_B1) * g
    v = ADAM_B2 * v + (1.0 - ADAM_B2) * _jnp.square(g)
    m_hat = m / (1.0 - ADAM_B1 ** ADAM_STEP)
    v_hat = v / (1.0 - ADAM_B2 ** ADAM_STEP)
    delta = -ADAM_LR * (m_hat / (_jnp.sqrt(v_hat) + ADAM_EPS) + ADAM_WD * w)
    return delta, m, v


def reference(x, mem, norm_mix_g, w_in, b_forget, b_gate, pool_w, pool_scale, w_pool_out, w_fox_out, w_out, norm_x_g, norm_mem_g, w_xq, w_xkv, w_xo, norm_ffn_g, w_ffn_in, w_ffn_out, norm_final_g, loss_target, m_norm_mix_g, m_w_in, m_b_forget, m_b_gate, m_pool_w, m_pool_scale, m_w_pool_out, m_w_fox_out, m_w_out, m_norm_x_g, m_norm_mem_g, m_w_xq, m_w_xkv, m_w_xo, m_norm_ffn_g, m_w_ffn_in, m_w_ffn_out, m_norm_final_g, v_norm_mix_g, v_w_in, v_b_forget, v_b_gate, v_pool_w, v_pool_scale, v_w_pool_out, v_w_fox_out, v_w_out, v_norm_x_g, v_norm_mem_g, v_w_xq, v_w_xkv, v_w_xo, v_norm_ffn_g, v_w_ffn_in, v_w_ffn_out, v_norm_final_g):
    given = dict(x=x, mem=mem, norm_mix_g=norm_mix_g, w_in=w_in, b_forget=b_forget, b_gate=b_gate, pool_w=pool_w, pool_scale=pool_scale, w_pool_out=w_pool_out, w_fox_out=w_fox_out, w_out=w_out, norm_x_g=norm_x_g, norm_mem_g=norm_mem_g, w_xq=w_xq, w_xkv=w_xkv, w_xo=w_xo, norm_ffn_g=norm_ffn_g, w_ffn_in=w_ffn_in, w_ffn_out=w_ffn_out, norm_final_g=norm_final_g, loss_target=loss_target, m_norm_mix_g=m_norm_mix_g, m_w_in=m_w_in, m_b_forget=m_b_forget, m_b_gate=m_b_gate, m_pool_w=m_pool_w, m_pool_scale=m_pool_scale, m_w_pool_out=m_w_pool_out, m_w_fox_out=m_w_fox_out, m_w_out=m_w_out, m_norm_x_g=m_norm_x_g, m_norm_mem_g=m_norm_mem_g, m_w_xq=m_w_xq, m_w_xkv=m_w_xkv, m_w_xo=m_w_xo, m_norm_ffn_g=m_norm_ffn_g, m_w_ffn_in=m_w_ffn_in, m_w_ffn_out=m_w_ffn_out, m_norm_final_g=m_norm_final_g, v_norm_mix_g=v_norm_mix_g, v_w_in=v_w_in, v_b_forget=v_b_forget, v_b_gate=v_b_gate, v_pool_w=v_pool_w, v_pool_scale=v_pool_scale, v_w_pool_out=v_w_pool_out, v_w_fox_out=v_w_fox_out, v_w_out=v_w_out, v_norm_x_g=v_norm_x_g, v_norm_mem_g=v_norm_mem_g, v_w_xq=v_w_xq, v_w_xkv=v_w_xkv, v_w_xo=v_w_xo, v_norm_ffn_g=v_norm_ffn_g, v_w_ffn_in=v_w_ffn_in, v_w_ffn_out=v_w_ffn_out, v_norm_final_g=v_norm_final_g)
    weights = {n: given[n] for n in TWIN_WEIGHTS}
    shared = {n: given[n] for n in SHARED_INPUTS}
    per_example = {n: given[n] for n in ['x', 'mem']}
    grad_fn = _jax.value_and_grad(_loss, argnums=(0, 1))

    def one_microbatch(ex, loss_target):
        ex = dict(ex)
        diff = ex.pop(TWIN_DIFF_INPUT)
        return grad_fn(weights, diff, {**shared, **ex}, loss_target)

    if N_MICROBATCH == 1:
        loss, (grad_w, grad_x) = one_microbatch(per_example, given["loss_target"])
    else:
        def body(carry, xs):
            loss_sum, grad_sum = carry
            l_k, (gw_k, gx_k) = one_microbatch(xs[0], xs[1])
            with _jax.named_scope("update"):
                return (loss_sum + l_k, _jax.tree.map(_jnp.add, grad_sum, gw_k)), gx_k

        init = (_jnp.zeros((), _jnp.float32), _jax.tree.map(_jnp.zeros_like, weights))
        (loss, grad_w), grad_x = _jax.lax.scan(body, init, (per_example, given["loss_target"]))
    with _jax.named_scope("update"):
        delta_w, new_m, new_v = {}, {}, {}
        for n in TWIN_WEIGHTS:
            delta_w[n], new_m[n], new_v[n] = _adamw(weights[n], grad_w[n], given["m_" + n], given["v_" + n])
    return (loss, grad_x, *[grad_w[n] for n in TWIN_WEIGHTS], *[delta_w[n] for n in TWIN_WEIGHTS],
            *[new_m[n] for n in TWIN_WEIGHTS], *[new_v[n] for n in TWIN_WEIGHTS])
```

```python
import functools
import math

import jax
import jax.numpy as jnp
from jax import lax
from jax.experimental import pallas as pl
from jax.experimental.pallas import tpu as pltpu

F32 = jnp.float32
BF16 = jnp.bfloat16
MESH = pl.DeviceIdType.MESH

D_MODEL = 1024
EPS = 1e-6
POOL_WINDOWS = (2, 4, 8, 16)
POOL_WIDTH = 512
POOL_GC = 128
FOX_HEADS = 8
FOX_DH = 64
FOX_WIDTH = 512
X_HEADS = 4
X_DH = 128
X_WIDTH = 512
D_FF = 2816
IN_COLS = 4104
ADAM_LR = 0.001
ADAM_B1 = 0.9
ADAM_B2 = 0.999
ADAM_EPS = 1e-08
ADAM_WD = 0.01
ADAM_STEP = 10

N_CHIPS = 4
N_DEV = 8
LANES = 128
VMEM_LIMIT_BYTES = 48 * 1024 * 1024
NEG_INF = -1e30
ATT_BLOCK = 256

SHARDED = (
    ("w_in", (1024, IN_COLS), 1),
    ("w_pool_out", (POOL_WIDTH, 1024), 1),
    ("w_fox_out", (FOX_WIDTH, 1024), 1),
    ("w_out", (1024, 1024), 0),
    ("w_xq", (1024, X_WIDTH), 0),
    ("w_xkv", (1024, 2 * X_WIDTH), 0),
    ("w_xo", (X_WIDTH, 1024), 1),
    ("w_ffn_in", (1024, 2 * D_FF), 1),
    ("w_ffn_out", (D_FF, 1024), 0),
)
SMALL = (
    ("norm_mix_g", (1, 1024)),
    ("b_forget", (1, 8)),
    ("b_gate", (1, 2048)),
    ("pool_w", (1, 4, 128, 128)),
    ("pool_scale", (1, 512)),
    ("norm_x_g", (1, 1024)),
    ("norm_mem_g", (1, 1024)),
    ("norm_ffn_g", (1, 1024)),
    ("norm_final_g", (1024,)),
)
WEIGHT_ORDER = ("norm_mix_g", "w_in", "b_forget", "b_gate", "pool_w", "pool_scale", "w_pool_out", "w_fox_out", "w_out",
                "norm_x_g", "norm_mem_g", "w_xq", "w_xkv", "w_xo", "norm_ffn_g", "w_ffn_in", "w_ffn_out", "norm_final_g")


def _shard_shape(shape, axis):
    s = list(shape)
    s[axis] //= N_CHIPS
    return tuple(s)


def _round_up(n, m):
    return (n + m - 1) // m * m


SHARD_ELEMS = sum(math.prod(_shard_shape(s, a)) for _, s, a in SHARDED)
PACK_ROWS = _round_up(-(-SHARD_ELEMS // LANES), 1024)
HALF_ROWS = PACK_ROWS // 2
SMALL_ELEMS = sum(math.prod(s) for _, s in SMALL)
SMALL_ROWS = _round_up(SMALL_ELEMS // LANES + 1, 8)
LOSS_POS = SMALL_ROWS * LANES - 1


def _cparams(sem=None):
    return pltpu.CompilerParams(dimension_semantics=sem, vmem_limit_bytes=VMEM_LIMIT_BYTES)


def _block(dim, pref, unit):
    if dim <= pref:
        return dim
    best = None
    for b in range(unit, pref + 1, unit):
        if dim % b == 0:
            best = b
    assert best is not None, (dim, pref, unit)
    return best


def _pack_rows(flat, rows):
    flat = jnp.pad(flat, (0, rows * LANES - flat.shape[0]))
    return flat.reshape(rows, LANES)


def _pack_local(shards, dtype):
    return _pack_rows(jnp.concatenate([s.reshape(-1).astype(dtype) for s in shards]), PACK_ROWS)


def _unpack_local(packed):
    flat = packed.reshape(-1)
    out, off = [], 0
    for _, shape, axis in SHARDED:
        shp = _shard_shape(shape, axis)
        n = math.prod(shp)
        out.append(flat[off:off + n].reshape(shp))
        off += n
    return out


def _unpack_full(gathered):
    flat = gathered.reshape(N_CHIPS, -1)
    out, off = [], 0
    for _, shape, axis in SHARDED:
        shp = _shard_shape(shape, axis)
        n = math.prod(shp)
        out.append(jnp.concatenate([flat[j, off:off + n].reshape(shp) for j in range(N_CHIPS)], axis=axis))
        off += n
    return out


def _pack_full(fulls):
    slabs = []
    for j in range(N_CHIPS):
        parts = []
        for (_, shape, axis), g in zip(SHARDED, fulls):
            n = shape[axis] // N_CHIPS
            parts.append(lax.slice_in_dim(g, j * n, (j + 1) * n, axis=axis).reshape(-1))
        slabs.append(_pack_rows(jnp.concatenate(parts), PACK_ROWS))
    return jnp.stack(slabs)


def _pack_small(parts, last=None):
    flat = jnp.concatenate([p.reshape(-1).astype(F32) for p in parts])
    flat = jnp.pad(flat, (0, SMALL_ROWS * LANES - flat.shape[0]))
    if last is not None:
        flat = flat.at[LOSS_POS].set(last)
    return flat.reshape(SMALL_ROWS, LANES)


def _unpack_small(packed):
    flat = packed.reshape(-1)
    out, off = [], 0
    for _, shape in SMALL:
        n = math.prod(shape)
        out.append(flat[off:off + n].reshape(shape))
        off += n
    return out


def _my_place():
    return lax.axis_index("x"), lax.axis_index("y"), lax.axis_index("c")


ANY = pl.BlockSpec(memory_space=pl.ANY)


def _gather_weights(packed):
    def body(w_ref, out_ref, send_sems, recv_sems, local_sem):
        x, y, c = _my_place()
        sibling = (x, y, 1 - c)
        chips = [(1 - x, y), (x, 1 - y), (1 - x, 1 - y)]

        def half(chip, core):
            return out_ref.at[2 * chip[0] + chip[1], pl.ds(core * HALF_ROWS, HALF_ROWS), :]

        def copy(k, chip, core, to, src=None):
            return pltpu.make_async_remote_copy(
                src_ref=half(chip, core) if src is None else src, dst_ref=half(chip, core),
                send_sem=send_sems.at[k], recv_sem=recv_sems.at[k], device_id=to, device_id_type=MESH)

        mine = pltpu.make_async_copy(w_ref, out_ref.at[2 * x + y], local_sem)
        mine.start()
        my_half = w_ref.at[pl.ds(c * HALF_ROWS, HALF_ROWS), :]
        first = [copy(j, (x, y), c, (*chip, c), src=my_half) for j, chip in enumerate(chips)]
        for cp in first:
            cp.start()
        passed = [copy(3 + j, chip, c, sibling) for j, chip in enumerate(chips)]
        for j, chip in enumerate(chips):
            copy(j, chip, c, (x, y, c)).wait_recv()
            passed[j].start()
        for j, chip in enumerate(chips):
            copy(3 + j, chip, 1 - c, (x, y, c)).wait_recv()
        for cp in first + passed:
            cp.wait_send()
        mine.wait()

    return pl.pallas_call(
        body, name="gather_weights",
        out_shape=jax.ShapeDtypeStruct((N_CHIPS,) + packed.shape, packed.dtype),
        in_specs=[ANY], out_specs=ANY,
        scratch_shapes=[pltpu.SemaphoreType.DMA((6,)), pltpu.SemaphoreType.DMA((6,)), pltpu.SemaphoreType.DMA],
    )(packed)


def _swap_halves(grads):
    def body(g_ref, out_ref, send_sem, recv_sem):
        x, y, c = _my_place()
        cp = pltpu.make_async_remote_copy(
            src_ref=g_ref.at[:, pl.ds((1 - c) * HALF_ROWS, HALF_ROWS), :], dst_ref=out_ref,
            send_sem=send_sem, recv_sem=recv_sem, device_id=(x, y, 1 - c), device_id_type=MESH)
        cp.start()
        cp.wait()

    return pl.pallas_call(
        body, name="swap_halves",
        out_shape=jax.ShapeDtypeStruct((N_CHIPS, HALF_ROWS, LANES), grads.dtype),
        in_specs=[ANY], out_specs=ANY,
        scratch_shapes=[pltpu.SemaphoreType.DMA, pltpu.SemaphoreType.DMA],
    )(grads)


def _exchange_chips(sums):
    def body(s_ref, out_ref, send_sems, recv_sems, local_sem):
        x, y, c = _my_place()
        me = 2 * x + y
        chips = [(1 - x, y), (x, 1 - y), (1 - x, 1 - y)]
        mine = pltpu.make_async_copy(s_ref.at[me], out_ref.at[me], local_sem)
        mine.start()
        sends = [pltpu.make_async_remote_copy(
            src_ref=s_ref.at[2 * chip[0] + chip[1]], dst_ref=out_ref.at[me],
            send_sem=send_sems.at[j], recv_sem=recv_sems.at[j], device_id=(*chip, c), device_id_type=MESH)
            for j, chip in enumerate(chips)]
        for cp in sends:
            cp.start()
        for j, chip in enumerate(chips):
            slot = out_ref.at[2 * chip[0] + chip[1]]
            pltpu.make_async_remote_copy(
                src_ref=slot, dst_ref=slot, send_sem=send_sems.at[j], recv_sem=recv_sems.at[j],
                device_id=(x, y, c), device_id_type=MESH).wait_recv()
        for cp in sends:
            cp.wait_send()
        mine.wait()

    return pl.pallas_call(
        body, name="exchange_chips",
        out_shape=jax.ShapeDtypeStruct(sums.shape, sums.dtype),
        in_specs=[ANY], out_specs=ANY,
        scratch_shapes=[pltpu.SemaphoreType.DMA((3,)), pltpu.SemaphoreType.DMA((3,)), pltpu.SemaphoreType.DMA],
    )(sums)


def _join_halves(half):
    def body(h_ref, out_ref, send_sem, recv_sem, local_sem):
        x, y, c = _my_place()
        rows = out_ref.at[pl.ds(c * HALF_ROWS, HALF_ROWS), :]
        mine = pltpu.make_async_copy(h_ref, rows, local_sem)
        mine.start()
        cp = pltpu.make_async_remote_copy(
            src_ref=h_ref, dst_ref=rows, send_sem=send_sem, recv_sem=recv_sem,
            device_id=(x, y, 1 - c), device_id_type=MESH)
        cp.start()
        theirs = out_ref.at[pl.ds((1 - c) * HALF_ROWS, HALF_ROWS), :]
        pltpu.make_async_remote_copy(
            src_ref=theirs, dst_ref=theirs, send_sem=send_sem, recv_sem=recv_sem,
            device_id=(x, y, c), device_id_type=MESH).wait_recv()
        cp.wait_send()
        mine.wait()

    return pl.pallas_call(
        body, name="join_halves",
        out_shape=jax.ShapeDtypeStruct((PACK_ROWS, LANES), half.dtype),
        in_specs=[ANY], out_specs=ANY,
        scratch_shapes=[pltpu.SemaphoreType.DMA, pltpu.SemaphoreType.DMA, pltpu.SemaphoreType.DMA],
    )(half)


def _gather_small(block):
    m_per = block.shape[0]

    def body(x_ref, out_ref, send_sems, recv_sems, local_sem):
        x, y, c = _my_place()
        me, sibling = (x, y, c), (x, y, 1 - c)
        chips = [(1 - x, y), (x, 1 - y), (1 - x, 1 - y)]

        def rows(px, py, pc):
            return out_ref.at[pl.ds((4 * px + 2 * py + pc) * m_per, m_per), :]

        def copy(k, blk, to, src=None):
            return pltpu.make_async_remote_copy(
                src_ref=rows(*blk) if src is None else src, dst_ref=rows(*blk),
                send_sem=send_sems.at[k], recv_sem=recv_sems.at[k], device_id=to, device_id_type=MESH)

        mine = pltpu.make_async_copy(x_ref, rows(*me), local_sem)
        mine.start()
        first = [copy(0, me, sibling, src=x_ref)]
        first += [copy(1 + j, me, (*chip, c), src=x_ref) for j, chip in enumerate(chips)]
        for cp in first:
            cp.start()
        passed = [copy(4 + j, (*chip, c), sibling) for j, chip in enumerate(chips)]
        for j, chip in enumerate(chips):
            copy(1 + j, (*chip, c), me).wait_recv()
            passed[j].start()
        copy(0, sibling, me).wait_recv()
        for j, chip in enumerate(chips):
            copy(4 + j, (*chip, 1 - c), me).wait_recv()
        for cp in first + passed:
            cp.wait_send()
        mine.wait()

    return pl.pallas_call(
        body, name="gather_small",
        out_shape=jax.ShapeDtypeStruct((N_DEV * m_per, LANES), block.dtype),
        in_specs=[pl.BlockSpec(memory_space=pltpu.VMEM)],
        out_specs=pl.BlockSpec(memory_space=pltpu.VMEM),
        scratch_shapes=[pltpu.SemaphoreType.DMA((7,)), pltpu.SemaphoreType.DMA((7,)), pltpu.SemaphoreType.DMA],
    )(block)


def _sum_halves(grads, theirs, core):
    br = _block(HALF_ROWS, 2048, 16)
    nb = HALF_ROWS // br

    def body(core_ref, a_ref, b_ref, o_ref):
        o_ref[...] = (a_ref[...] + b_ref[...]).astype(BF16)

    return pl.pallas_call(
        body, name="sum_halves",
        out_shape=jax.ShapeDtypeStruct((N_CHIPS, HALF_ROWS, LANES), BF16),
        grid_spec=pltpu.PrefetchScalarGridSpec(
            num_scalar_prefetch=1, grid=(N_CHIPS, nb),
            in_specs=[pl.BlockSpec((1, br, LANES), lambda j, i, core_ref: (j, core_ref[0] * nb + i, 0)),
                      pl.BlockSpec((1, br, LANES), lambda j, i, core_ref: (j, i, 0))],
            out_specs=pl.BlockSpec((1, br, LANES), lambda j, i, core_ref: (j, i, 0))),
        compiler_params=_cparams(("parallel", "parallel")),
    )(core, grads, theirs)


def _sum_chips(slots):
    br = _block(HALF_ROWS, 2048, 16)

    def body(s_ref, o_ref):
        acc = s_ref[0].astype(F32)
        for k in range(1, N_CHIPS):
            acc = acc + s_ref[k].astype(F32)
        o_ref[...] = acc

    return pl.pallas_call(
        body, name="sum_chips",
        out_shape=jax.ShapeDtypeStruct((HALF_ROWS, LANES), F32),
        grid=(HALF_ROWS // br,),
        in_specs=[pl.BlockSpec((N_CHIPS, br, LANES), lambda i: (0, i, 0))],
        out_specs=pl.BlockSpec((br, LANES), lambda i: (i, 0)),
        compiler_params=_cparams(("parallel",)),
    )(slots)


def _adamw_math(w, g, m, v):
    m = ADAM_B1 * m + (1.0 - ADAM_B1) * g
    v = ADAM_B2 * v + (1.0 - ADAM_B2) * (g * g)
    m_hat = m / (1.0 - ADAM_B1 ** ADAM_STEP)
    v_hat = v / (1.0 - ADAM_B2 ** ADAM_STEP)
    delta = -ADAM_LR * (m_hat / (jnp.sqrt(v_hat) + ADAM_EPS) + ADAM_WD * w)
    return delta, m, v


def _adamw(w, g, m, v, name):
    rows, cols = w.shape
    br = _block(rows, max(8, (1 << 19) // cols // 8 * 8), 8)

    def body(w_ref, g_ref, m_ref, v_ref, d_ref, nm_ref, nv_ref):
        d, nm, nv = _adamw_math(w_ref[...], g_ref[...], m_ref[...], v_ref[...])
        d_ref[...] = d
        nm_ref[...] = nm
        nv_ref[...] = nv

    spec = pl.BlockSpec((br, cols), lambda i: (i, 0))
    shape = jax.ShapeDtypeStruct(w.shape, F32)
    return pl.pallas_call(
        body, name=name, out_shape=(shape, shape, shape), grid=(rows // br,),
        in_specs=[spec] * 4, out_specs=(spec, spec, spec),
        compiler_params=_cparams(("parallel",)),
    )(w, g, m, v)


def _adamw_small(parts, w, m, v):
    def body(p_ref, w_ref, m_ref, v_ref, g_ref, d_ref, nm_ref, nv_ref):
        g = p_ref[0]
        for k in range(1, N_DEV):
            g = g + p_ref[k]
        d, nm, nv = _adamw_math(w_ref[...], g, m_ref[...], v_ref[...])
        g_ref[...] = g
        d_ref[...] = d
        nm_ref[...] = nm
        nv_ref[...] = nv

    shape = jax.ShapeDtypeStruct((SMALL_ROWS, LANES), F32)
    return pl.pallas_call(body, name="adamw_small", out_shape=(shape,) * 4, compiler_params=_cparams())(parts, w, m, v)


def _mm(a, b, *, name, ta=False, tb=False, out_dtype=F32, res=None, bm=512, bn=512, bk=1024):
    if ta:
        kdim, m = a.shape
    else:
        m, kdim = a.shape
    if tb:
        n, kb = b.shape
    else:
        kb, n = b.shape
    assert kdim == kb, (a.shape, b.shape, ta, tb)
    bm = _block(m, bm, LANES if ta else 16)
    bn = _block(n, bn, LANES)
    bk = _block(kdim, bk, LANES)
    nk = kdim // bk
    dims = (((0 if ta else 1,), (1 if tb else 0,)), ((), ()))

    def body(*refs):
        if res is None:
            a_ref, b_ref, o_ref, acc_ref = refs
            r_ref = None
        else:
            a_ref, b_ref, r_ref, o_ref, acc_ref = refs
        k = pl.program_id(2)
        part = lax.dot_general(a_ref[...].astype(BF16), b_ref[...].astype(BF16), dims, preferred_element_type=F32)

        @pl.when(k == 0)
        def _():
            acc_ref[...] = part

        @pl.when(k > 0)
        def _():
            acc_ref[...] += part

        @pl.when(k == nk - 1)
        def _():
            r = acc_ref[...]
            if r_ref is not None:
                r = r + r_ref[...]
            o_ref[...] = r.astype(out_dtype)

    a_spec = pl.BlockSpec((bk, bm), lambda i, j, k: (k, i)) if ta else pl.BlockSpec((bm, bk), lambda i, j, k: (i, k))
    b_spec = pl.BlockSpec((bn, bk), lambda i, j, k: (j, k)) if tb else pl.BlockSpec((bk, bn), lambda i, j, k: (k, j))
    o_spec = pl.BlockSpec((bm, bn), lambda i, j, k: (i, j))
    in_specs = [a_spec, b_spec] + ([o_spec] if res is not None else [])
    args = (a, b) + ((res,) if res is not None else ())
    return pl.pallas_call(
        body, name=name, out_shape=jax.ShapeDtypeStruct((m, n), out_dtype),
        grid=(m // bm, n // bn, nk), in_specs=in_specs, out_specs=o_spec,
        scratch_shapes=[pltpu.VMEM((bm, bn), F32)],
        compiler_params=_cparams(("parallel", "parallel", "arbitrary")),
    )(*args)


def _rms_fwd(x, g, name):
    t, d = x.shape
    bt = _block(t, 512, 16)

    def body(x_ref, g_ref, h_ref):
        xv = x_ref[...]
        r = lax.rsqrt(jnp.mean(xv * xv, axis=-1, keepdims=True) + EPS)
        h_ref[...] = (xv * r * g_ref[...]).astype(BF16)

    return pl.pallas_call(
        body, name=name, out_shape=jax.ShapeDtypeStruct((t, d), BF16), grid=(t // bt,),
        in_specs=[pl.BlockSpec((bt, d), lambda i: (i, 0)), pl.BlockSpec((1, d), lambda i: (0, 0))],
        out_specs=pl.BlockSpec((bt, d), lambda i: (i, 0)),
        compiler_params=_cparams(("parallel",)),
    )(x, g)


def _rms_bwd(dh, x, g, dres, name):
    t, d = x.shape
    bt = _block(t, 256, 8)
    want_dx = dres is not None

    def body(*refs):
        if want_dx:
            dh_ref, x_ref, g_ref, dres_ref, dx_ref, dg_ref = refs
        else:
            dh_ref, x_ref, g_ref, dg_ref = refs
        xv = x_ref[...]
        r = lax.rsqrt(jnp.mean(xv * xv, axis=-1, keepdims=True) + EPS)
        xhat = xv * r
        dhv = dh_ref[...]

        @pl.when(pl.program_id(0) == 0)
        def _():
            dg_ref[...] = jnp.zeros_like(dg_ref)

        dg_ref[...] += jnp.sum(dhv * xhat, axis=0, keepdims=True)
        if want_dx:
            dxhat = dhv * g_ref[...]
            dx_ref[...] = dres_ref[...] + r * (dxhat - xhat * jnp.mean(dxhat * xhat, axis=-1, keepdims=True))

    row = pl.BlockSpec((bt, d), lambda i: (i, 0))
    vec = pl.BlockSpec((1, d), lambda i: (0, 0))
    if want_dx:
        return pl.pallas_call(
            body, name=name, grid=(t // bt,),
            out_shape=(jax.ShapeDtypeStruct((t, d), F32), jax.ShapeDtypeStruct((1, d), F32)),
            in_specs=[row, row, vec, row], out_specs=(row, vec),
            compiler_params=_cparams(("arbitrary",)),
        )(dh, x, g, dres)
    return pl.pallas_call(
        body, name=name, grid=(t // bt,), out_shape=jax.ShapeDtypeStruct((1, d), F32),
        in_specs=[row, row, vec], out_specs=vec,
        compiler_params=_cparams(("arbitrary",)),
    )(dh, x, g)


def _final_loss(x, target, g):
    t, d = x.shape
    bt = _block(t, 256, 8)

    def body(x_ref, t_ref, g_ref, dx_ref, dg_ref, loss_ref):
        xv = x_ref[...]
        gv = g_ref[...]
        r = lax.rsqrt(jnp.mean(xv * xv, axis=-1, keepdims=True) + EPS)
        xhat = xv * r
        err = xhat * gv - t_ref[...]

        @pl.when(pl.program_id(0) == 0)
        def _():
            dg_ref[...] = jnp.zeros_like(dg_ref)
            loss_ref[...] = jnp.zeros_like(loss_ref)

        loss_ref[...] += 0.5 * jnp.sum(jnp.mean(err * err, axis=-1, keepdims=True), axis=0, keepdims=True)
        dy = err * (1.0 / d)
        dg_ref[...] += jnp.sum(dy * xhat, axis=0, keepdims=True)
        dxhat = dy * gv
        dx_ref[...] = r * (dxhat - xhat * jnp.mean(dxhat * xhat, axis=-1, keepdims=True))

    row = pl.BlockSpec((bt, d), lambda i: (i, 0))
    vec = pl.BlockSpec((1, d), lambda i: (0, 0))
    return pl.pallas_call(
        body, name="final_loss", grid=(t // bt,),
        out_shape=(jax.ShapeDtypeStruct((t, d), F32), jax.ShapeDtypeStruct((1, d), F32),
                   jax.ShapeDtypeStruct((1, LANES), F32)),
        in_specs=[row, row, vec], out_specs=(row, vec, pl.BlockSpec((1, LANES), lambda i: (0, 0))),
        compiler_params=_cparams(("arbitrary",)),
    )(x, target, g)


def _shift_down(a, k, row):
    return jnp.where(row >= k, pltpu.roll(a, k, 0), 0.0)


def _shift_up(a, k, row):
    n = a.shape[0]
    return jnp.where(row < n - k, pltpu.roll(a, n - k, 0), 0.0)


def _window_delta(u, w, row):
    s, k = u, 1
    while k < w:
        s = s + _shift_down(s, k, row)
        k *= 2
    cnt = jnp.minimum(row + 1, w).astype(F32)
    return s / cnt - u, cnt


def _pool_fwd(proj, pool_w, pool_scale):
    b, s, _ = proj.shape

    def body(u_ref, pw_ref, sc_ref, y_ref):
        row = lax.broadcasted_iota(jnp.int32, (s, POOL_GC), 0)
        for g, w in enumerate(POOL_WINDOWS):
            cols = slice(g * POOL_GC, (g + 1) * POOL_GC)
            d, _ = _window_delta(u_ref[0, :, cols], w, row)
            z = jnp.dot(d.astype(BF16), pw_ref[g].astype(BF16), preferred_element_type=F32)
            y_ref[0, :, cols] = (z * sc_ref[:, cols]).astype(BF16)

    return pl.pallas_call(
        body, name="pool_fwd", out_shape=jax.ShapeDtypeStruct((b, s, POOL_WIDTH), BF16), grid=(b,),
        in_specs=[pl.BlockSpec((1, s, POOL_WIDTH), lambda i: (i, 0, 0)),
                  pl.BlockSpec((4, POOL_GC, POOL_GC), lambda i: (0, 0, 0)),
                  pl.BlockSpec((1, POOL_WIDTH), lambda i: (0, 0))],
        out_specs=pl.BlockSpec((1, s, POOL_WIDTH), lambda i: (i, 0, 0)),
        compiler_params=_cparams(("parallel",)),
    )(proj, pool_w, pool_scale)


def _pool_bwd(proj, dy, pool_w, pool_scale):
    b, s, _ = proj.shape

    def body(u_ref, dy_ref, pw_ref, sc_ref, du_ref, dpw_ref, dsc_ref):
        @pl.when(pl.program_id(0) == 0)
        def _():
            dpw_ref[...] = jnp.zeros_like(dpw_ref)
            dsc_ref[...] = jnp.zeros_like(dsc_ref)

        row = lax.broadcasted_iota(jnp.int32, (s, POOL_GC), 0)
        for g, w in enumerate(POOL_WINDOWS):
            cols = slice(g * POOL_GC, (g + 1) * POOL_GC)
            d, cnt = _window_delta(u_ref[0, :, cols], w, row)
            db = d.astype(BF16)
            pw = pw_ref[g].astype(BF16)
            z = jnp.dot(db, pw, preferred_element_type=F32)
            dyv = dy_ref[0, :, cols]
            dsc_ref[:, cols] += jnp.sum(dyv * z, axis=0, keepdims=True)
            dz = (dyv * sc_ref[:, cols]).astype(BF16)
            dpw_ref[g] += lax.dot_general(db, dz, (((0,), (0,)), ((), ())), preferred_element_type=F32)
            dd = lax.dot_general(dz, pw, (((1,), (1,)), ((), ())), preferred_element_type=F32)
            acc, k = dd / cnt, 1
            while k < w:
                acc = acc + _shift_up(acc, k, row)
                k *= 2
            du_ref[0, :, cols] = (acc - dd).astype(BF16)

    return pl.pallas_call(
        body, name="pool_bwd", grid=(b,),
        out_shape=(jax.ShapeDtypeStruct((b, s, POOL_WIDTH), BF16), jax.ShapeDtypeStruct((4, POOL_GC, POOL_GC), F32),
                   jax.ShapeDtypeStruct((1, POOL_WIDTH), F32)),
        in_specs=[pl.BlockSpec((1, s, POOL_WIDTH), lambda i: (i, 0, 0)),
                  pl.BlockSpec((1, s, POOL_WIDTH), lambda i: (i, 0, 0)),
                  pl.BlockSpec((4, POOL_GC, POOL_GC), lambda i: (0, 0, 0)),
                  pl.BlockSpec((1, POOL_WIDTH), lambda i: (0, 0))],
        out_specs=(pl.BlockSpec((1, s, POOL_WIDTH), lambda i: (i, 0, 0)),
                   pl.BlockSpec((4, POOL_GC, POOL_GC), lambda i: (0, 0, 0)),
                   pl.BlockSpec((1, POOL_WIDTH), lambda i: (0, 0))),
        compiler_params=_cparams(("arbitrary",)),
    )(proj, dy, pool_w, pool_scale)


def _forget_cumsum(f, bias):
    b, s, c = f.shape

    def body(f_ref, b_ref, c_ref):
        row = lax.broadcasted_iota(jnp.int32, (s, LANES), 0)
        z = f_ref[0] + b_ref[...]
        acc = jnp.minimum(z, 0.0) - jnp.log(1.0 + jnp.exp(-jnp.abs(z)))
        k = 1
        while k < s:
            acc = acc + _shift_down(acc, k, row)
            k *= 2
        c_ref[0] = acc

    return pl.pallas_call(
        body, name="forget_cumsum", out_shape=jax.ShapeDtypeStruct((b, s, c), F32), grid=(b, c // LANES),
        in_specs=[pl.BlockSpec((1, s, LANES), lambda i, j: (i, 0, j)), pl.BlockSpec((1, LANES), lambda i, j: (0, j))],
        out_specs=pl.BlockSpec((1, s, LANES), lambda i, j: (i, 0, j)),
        compiler_params=_cparams(("parallel", "parallel")),
    )(f, bias)


def _forget_bwd(dc, f, bias):
    b, s, _ = f.shape

    def body(dc_ref, f_ref, b_ref, df_ref, db_ref):
        @pl.when(pl.program_id(0) == 0)
        def _():
            db_ref[...] = jnp.zeros_like(db_ref)

        row = lax.broadcasted_iota(jnp.int32, (s, LANES), 0)
        acc, k = dc_ref[0], 1
        while k < s:
            acc = acc + _shift_up(acc, k, row)
            k *= 2
        z = f_ref[0] + b_ref[...]
        df = acc / (1.0 + jnp.exp(z))
        db_ref[...] += jnp.sum(df, axis=0, keepdims=True)
        df_ref[0] = df.astype(BF16)

    blk = pl.BlockSpec((1, s, LANES), lambda i: (i, 0, 0))
    vec = pl.BlockSpec((1, LANES), lambda i: (0, 0))
    return pl.pallas_call(
        body, name="forget_bwd", grid=(b,),
        out_shape=(jax.ShapeDtypeStruct((b, s, LANES), BF16), jax.ShapeDtypeStruct((1, LANES), F32)),
        in_specs=[blk, blk, vec], out_specs=(blk, vec),
        compiler_params=_cparams(("arbitrary",)),
    )(dc, f, bias)


Q_BLK0, K_BLK0, V_BLK0 = 4, 8, 12
PAIRS = FOX_HEADS // 2
FOX_SCALE = FOX_DH ** -0.5


def _head_mask(rows, head):
    return (lax.broadcasted_iota(jnp.int32, (rows, LANES), 1) // FOX_DH) == head


def _fox_scores(qm, kblk, cq, ck, masked, blk):
    s = lax.dot_general(qm, kblk, (((1,), (1,)), ((), ())), preferred_element_type=F32)
    s = s + cq - ck
    if masked:
        keep = lax.broadcasted_iota(jnp.int32, (blk, blk), 0) >= lax.broadcasted_iota(jnp.int32, (blk, blk), 1)
        s = jnp.where(keep, s, NEG_INF)
    return s


def _fox_fwd(proj, c_exp, c_row):
    b, s, _ = proj.shape
    blk = min(ATT_BLOCK, s)
    nq = s // blk

    def body(q_ref, k_ref, v_ref, cc_ref, cr_ref, o_ref, lse_ref):
        qi = pl.program_id(2)
        qv = q_ref[0] * FOX_SCALE
        outs, lses = [], []
        for head in range(2):
            hm = _head_mask(blk, head)
            qm = jnp.where(hm, qv, 0.0).astype(BF16)
            cq = cc_ref[0][:, head * FOX_DH:head * FOX_DH + 1]

            def step(kb, carry, masked):
                m, l, acc = carry
                kblk = k_ref[0, pl.ds(pl.multiple_of(kb * blk, blk), blk), :].astype(BF16)
                vblk = v_ref[0, pl.ds(pl.multiple_of(kb * blk, blk), blk), :].astype(BF16)
                sc = _fox_scores(qm, kblk, cq, cr_ref[0, head, kb], masked, blk)
                m_new = jnp.maximum(m, jnp.max(sc, axis=-1, keepdims=True))
                p = jnp.exp(sc - m_new)
                alpha = jnp.exp(m - m_new)
                l = alpha * l + jnp.sum(p, axis=-1, keepdims=True)
                acc = alpha * acc + jnp.dot(p.astype(BF16), vblk, preferred_element_type=F32)
                return m_new, l, acc

            init = (jnp.full((blk, 1), NEG_INF, F32), jnp.zeros((blk, 1), F32), jnp.zeros((blk, LANES), F32))
            carry = lax.fori_loop(0, qi, functools.partial(step, masked=False), init)
            m, l, acc = step(qi, carry, True)
            outs.append(acc / l)
            lses.append(m + jnp.log(l))
        hm0 = _head_mask(blk, 0)
        o_ref[0] = jnp.where(hm0, outs[0], outs[1])
        lse_ref[0] = jnp.where(hm0, lses[0], lses[1])

    qspec = lambda base: pl.BlockSpec((1, blk, LANES), lambda i, h, q: (i, q, base + h))
    kvspec = lambda base: pl.BlockSpec((1, s, LANES), lambda i, h, q: (i, 0, base + h))
    tile = pl.BlockSpec((1, blk, LANES), lambda i, h, q: (i, q, h))
    shape = jax.ShapeDtypeStruct((b, s, FOX_WIDTH), F32)
    return pl.pallas_call(
        body, name="fox_fwd", out_shape=(shape, shape), grid=(b, PAIRS, nq),
        in_specs=[qspec(Q_BLK0), kvspec(K_BLK0), kvspec(V_BLK0), tile,
                  pl.BlockSpec((1, 2, nq, 1, blk), lambda i, h, q: (i, h, 0, 0, 0))],
        out_specs=(tile, tile),
        compiler_params=_cparams(("parallel", "parallel", "arbitrary")),
    )(proj, proj, proj, c_exp, c_row)


def _fox_bwd_q(proj, c_exp, c_row, lse, o, do):
    b, s, _ = proj.shape
    blk = min(ATT_BLOCK, s)
    nq = s // blk

    def body(q_ref, k_ref, v_ref, cc_ref, cr_ref, lse_ref, o_ref, do_ref, dq_ref, dl_ref, dcq_ref):
        qi = pl.program_id(2)
        qv = q_ref[0] * FOX_SCALE
        dov = do_ref[0]
        prod = dov * o_ref[0]
        dqs, deltas, dcqs = [], [], []
        for head in range(2):
            hm = _head_mask(blk, head)
            qm = jnp.where(hm, qv, 0.0).astype(BF16)
            dom = jnp.where(hm, dov, 0.0).astype(BF16)
            delta = jnp.sum(jnp.where(hm, prod, 0.0), axis=-1, keepdims=True)
            cq = cc_ref[0][:, head * FOX_DH:head * FOX_DH + 1]
            lse_h = lse_ref[0][:, head * FOX_DH:head * FOX_DH + 1]

            def step(kb, carry, masked):
                acc, dcq = carry
                kblk = k_ref[0, pl.ds(pl.multiple_of(kb * blk, blk), blk), :].astype(BF16)
                vblk = v_ref[0, pl.ds(pl.multiple_of(kb * blk, blk), blk), :].astype(BF16)
                sc = _fox_scores(qm, kblk, cq, cr_ref[0, head, kb], masked, blk)
                p = jnp.exp(sc - lse_h)
                dp = lax.dot_general(dom, vblk, (((1,), (1,)), ((), ())), preferred_element_type=F32)
                ds = p * (dp - delta)
                acc = acc + jnp.dot(ds.astype(BF16), kblk, preferred_element_type=F32)
                return acc, dcq + jnp.sum(ds, axis=-1, keepdims=True)

            init = (jnp.zeros((blk, LANES), F32), jnp.zeros((blk, 1), F32))
            acc, dcq = step(qi, lax.fori_loop(0, qi, functools.partial(step, masked=False), init), True)
            dqs.append(acc)
            deltas.append(delta)
            dcqs.append(dcq)
        hm0 = _head_mask(blk, 0)
        dq_ref[0] = (jnp.where(hm0, dqs[0], dqs[1]) * FOX_SCALE).astype(BF16)
        dl_ref[0] = jnp.where(hm0, deltas[0], deltas[1])
        dcq_ref[0] = jnp.where(hm0, dcqs[0], dcqs[1])

    qspec = lambda base: pl.BlockSpec((1, blk, LANES), lambda i, h, q: (i, q, base + h))
    kvspec = lambda base: pl.BlockSpec((1, s, LANES), lambda i, h, q: (i, 0, base + h))
    tile = pl.BlockSpec((1, blk, LANES), lambda i, h, q: (i, q, h))
    return pl.pallas_call(
        body, name="fox_bwd_q", grid=(b, PAIRS, nq),
        out_shape=(jax.ShapeDtypeStruct((b, s, FOX_WIDTH), BF16), jax.ShapeDtypeStruct((b, s, FOX_WIDTH), F32),
                   jax.ShapeDtypeStruct((b, s, FOX_WIDTH), F32)),
        in_specs=[qspec(Q_BLK0), kvspec(K_BLK0), kvspec(V_BLK0), tile,
                  pl.BlockSpec((1, 2, nq, 1, blk), lambda i, h, q: (i, h, 0, 0, 0)), tile, tile, tile],
        out_specs=(tile, tile, tile),
        compiler_params=_cparams(("parallel", "parallel", "arbitrary")),
    )(proj, proj, proj, c_exp, c_row, lse, o, do)


def _fox_bwd_kv(proj, c_exp, c_row, lse, delta, do):
    b, s, _ = proj.shape
    blk = min(ATT_BLOCK, s)
    nq = s // blk

    def body(q_ref, k_ref, v_ref, cc_ref, cr_ref, lse_ref, dl_ref, do_ref, dk_ref, dv_ref, dc_ref):
        ki = pl.program_id(2)
        kblk = k_ref[0].astype(BF16)
        vblk = v_ref[0].astype(BF16)
        dk = jnp.zeros((blk, LANES), F32)
        dv = jnp.zeros((blk, LANES), F32)
        for head in range(2):
            hm = _head_mask(blk, head)
            ck = cr_ref[0, head, 0]
            lane = head * FOX_DH

            def step(qb, carry, masked):
                dk, dv, dc = carry
                rows = pl.ds(pl.multiple_of(qb * blk, blk), blk)
                qm = jnp.where(hm, q_ref[0, rows, :] * FOX_SCALE, 0.0).astype(BF16)
                dom = jnp.where(hm, do_ref[0, rows, :], 0.0).astype(BF16)
                cq = cc_ref[0, rows, :][:, lane:lane + 1]
                lse_h = lse_ref[0, rows, :][:, lane:lane + 1]
                delta = dl_ref[0, rows, :][:, lane:lane + 1]
                sc = _fox_scores(qm, kblk, cq, ck, masked, blk)
                p = jnp.exp(sc - lse_h)
                dv = dv + lax.dot_general(p.astype(BF16), dom, (((0,), (0,)), ((), ())), preferred_element_type=F32)
                dp = lax.dot_general(dom, vblk, (((1,), (1,)), ((), ())), preferred_element_type=F32)
                ds = p * (dp - delta)
                dk = dk + lax.dot_general(ds.astype(BF16), qm, (((0,), (0,)), ((), ())), preferred_element_type=F32)
                return dk, dv, dc - jnp.sum(ds, axis=0, keepdims=True)

            carry = step(ki, (dk, dv, jnp.zeros((1, blk), F32)), True)
            dk, dv, dc = lax.fori_loop(ki + 1, nq, functools.partial(step, masked=False), carry)
            dc_ref[0, head, 0] = dc
        dk_ref[0] = dk.astype(BF16)
        dv_ref[0] = dv.astype(BF16)

    full = lambda base: pl.BlockSpec((1, s, LANES), lambda i, h, k: (i, 0, base + h))
    ktile = lambda base: pl.BlockSpec((1, blk, LANES), lambda i, h, k: (i, k, base + h))
    crow = pl.BlockSpec((1, 2, 1, 1, blk), lambda i, h, k: (i, h, k, 0, 0))
    shape = jax.ShapeDtypeStruct((b, s, FOX_WIDTH), BF16)
    return pl.pallas_call(
        body, name="fox_bwd_kv", grid=(b, PAIRS, nq),
        out_shape=(shape, shape, jax.ShapeDtypeStruct(c_row.shape, F32)),
        in_specs=[full(Q_BLK0), ktile(K_BLK0), ktile(V_BLK0), full(0), crow, full(0), full(0), full(0)],
        out_specs=(ktile(0), ktile(0), crow),
        compiler_params=_cparams(("parallel", "parallel", "arbitrary")),
    )(proj, proj, proj, c_exp, c_row, lse, delta, do)


G_BLK0 = 2


def _sigmoid(z):
    return 1.0 / (1.0 + jnp.exp(-z))


def _mix_fwd(proj, b_gate, y_pool, y_fox):
    t = proj.shape[0]
    bt = _block(t, 256, 16)

    def body(gp_ref, gf_ref, bp_ref, bf_ref, yp_ref, yf_ref, o_ref):
        gp = _sigmoid(gp_ref[...] + bp_ref[...])
        gf = _sigmoid(gf_ref[...] + bf_ref[...])
        o_ref[...] = (gp * yp_ref[...] + gf * yf_ref[...]).astype(BF16)

    col = lambda j: pl.BlockSpec((bt, D_MODEL), lambda i: (i, j))
    vec = lambda j: pl.BlockSpec((1, D_MODEL), lambda i: (0, j))
    return pl.pallas_call(
        body, name="mix_fwd", out_shape=jax.ShapeDtypeStruct((t, D_MODEL), BF16), grid=(t // bt,),
        in_specs=[col(G_BLK0), col(G_BLK0 + 1), vec(0), vec(1), col(0), col(0)], out_specs=col(0),
        compiler_params=_cparams(("parallel",)),
    )(proj, proj, b_gate, b_gate, y_pool, y_fox)


def _mix_bwd(proj, b_gate, y_pool, y_fox, dmix):
    t = proj.shape[0]
    bt = _block(t, 256, 16)

    def body(gp_ref, gf_ref, bp_ref, bf_ref, yp_ref, yf_ref, dm_ref, dyp_ref, dyf_ref, dgl_ref, db_ref):
        @pl.when(pl.program_id(0) == 0)
        def _():
            db_ref[...] = jnp.zeros_like(db_ref)

        dm = dm_ref[...]
        gp = _sigmoid(gp_ref[...] + bp_ref[...])
        gf = _sigmoid(gf_ref[...] + bf_ref[...])
        dyp_ref[...] = (dm * gp).astype(BF16)
        dyf_ref[...] = (dm * gf).astype(BF16)
        dlp = dm * yp_ref[...] * gp * (1.0 - gp)
        dlf = dm * yf_ref[...] * gf * (1.0 - gf)
        dgl_ref[:, :D_MODEL] = dlp.astype(BF16)
        dgl_ref[:, D_MODEL:] = dlf.astype(BF16)
        db_ref[:, :D_MODEL] += jnp.sum(dlp, axis=0, keepdims=True)
        db_ref[:, D_MODEL:] += jnp.sum(dlf, axis=0, keepdims=True)

    col = lambda j: pl.BlockSpec((bt, D_MODEL), lambda i: (i, j))
    vec = lambda j: pl.BlockSpec((1, D_MODEL), lambda i: (0, j))
    wide = pl.BlockSpec((bt, 2 * D_MODEL), lambda i: (i, 0))
    return pl.pallas_call(
        body, name="mix_bwd", grid=(t // bt,),
        out_shape=(jax.ShapeDtypeStruct((t, D_MODEL), BF16), jax.ShapeDtypeStruct((t, D_MODEL), BF16),
                   jax.ShapeDtypeStruct((t, 2 * D_MODEL), BF16), jax.ShapeDtypeStruct((1, 2 * D_MODEL), F32)),
        in_specs=[col(G_BLK0), col(G_BLK0 + 1), vec(0), vec(1), col(0), col(0), col(0)],
        out_specs=(col(0), col(0), wide, pl.BlockSpec((1, 2 * D_MODEL), lambda i: (0, 0))),
        compiler_params=_cparams(("arbitrary",)),
    )(proj, proj, b_gate, b_gate, y_pool, y_fox, dmix)


X_SCALE = X_DH ** -0.5


def _xattn_probs(qh, kh):
    s = lax.dot_general(qh, kh, (((1,), (1,)), ((), ())), preferred_element_type=F32) * X_SCALE
    e = jnp.exp(s - jnp.max(s, axis=-1, keepdims=True))
    return e / jnp.sum(e, axis=-1, keepdims=True)


def _xattn_fwd(q, kv):
    b, s, _ = q.shape
    m = kv.shape[1]
    bq = _block(s, 512, 16)

    def body(q_ref, kv_ref, o_ref):
        for h in range(X_HEADS):
            cols = slice(h * X_DH, (h + 1) * X_DH)
            p = _xattn_probs(q_ref[0, :, cols], kv_ref[0, :, cols])
            vh = kv_ref[0, :, X_WIDTH + h * X_DH:X_WIDTH + (h + 1) * X_DH]
            o_ref[0, :, cols] = jnp.dot(p.astype(BF16), vh, preferred_element_type=F32).astype(BF16)

    return pl.pallas_call(
        body, name="xattn_fwd", out_shape=jax.ShapeDtypeStruct((b, s, X_WIDTH), BF16), grid=(b, s // bq),
        in_specs=[pl.BlockSpec((1, bq, X_WIDTH), lambda i, j: (i, j, 0)),
                  pl.BlockSpec((1, m, 2 * X_WIDTH), lambda i, j: (i, 0, 0))],
        out_specs=pl.BlockSpec((1, bq, X_WIDTH), lambda i, j: (i, j, 0)),
        compiler_params=_cparams(("parallel", "parallel")),
    )(q, kv)


def _xattn_bwd(q, kv, do):
    b, s, _ = q.shape
    m = kv.shape[1]
    bq = _block(s, 512, 16)

    def body(q_ref, kv_ref, do_ref, dq_ref, dkv_ref):
        @pl.when(pl.program_id(1) == 0)
        def _():
            dkv_ref[...] = jnp.zeros_like(dkv_ref)

        for h in range(X_HEADS):
            cols = slice(h * X_DH, (h + 1) * X_DH)
            vcols = slice(X_WIDTH + h * X_DH, X_WIDTH + (h + 1) * X_DH)
            qh, kh, vh, doh = q_ref[0, :, cols], kv_ref[0, :, cols], kv_ref[0, :, vcols], do_ref[0, :, cols]
            p = _xattn_probs(qh, kh)
            dkv_ref[0, :, vcols] += lax.dot_general(p.astype(BF16), doh, (((0,), (0,)), ((), ())),
                                                    preferred_element_type=F32)
            dp = lax.dot_general(doh, vh, (((1,), (1,)), ((), ())), preferred_element_type=F32)
            ds = (p * (dp - jnp.sum(p * dp, axis=-1, keepdims=True)) * X_SCALE).astype(BF16)
            dq_ref[0, :, cols] = jnp.dot(ds, kh, preferred_element_type=F32).astype(BF16)
            dkv_ref[0, :, cols] += lax.dot_general(ds, qh, (((0,), (0,)), ((), ())), preferred_element_type=F32)

    tile = pl.BlockSpec((1, bq, X_WIDTH), lambda i, j: (i, j, 0))
    mem = pl.BlockSpec((1, m, 2 * X_WIDTH), lambda i, j: (i, 0, 0))
    return pl.pallas_call(
        body, name="xattn_bwd", grid=(b, s // bq),
        out_shape=(jax.ShapeDtypeStruct((b, s, X_WIDTH), BF16), jax.ShapeDtypeStruct((b, m, 2 * X_WIDTH), F32)),
        in_specs=[tile, mem, tile], out_specs=(tile, mem),
        compiler_params=_cparams(("parallel", "arbitrary")),
    )(q, kv, do)


def _swiglu_fwd(gu):
    t = gu.shape[0]
    bt = _block(t, 256, 16)

    def body(gt_ref, up_ref, o_ref):
        gt = gt_ref[...]
        o_ref[...] = (gt * _sigmoid(gt) * up_ref[...]).astype(BF16)

    col = lambda j: pl.BlockSpec((bt, D_FF), lambda i: (i, j))
    return pl.pallas_call(
        body, name="swiglu_fwd", out_shape=jax.ShapeDtypeStruct((t, D_FF), BF16), grid=(t // bt,),
        in_specs=[col(0), col(1)], out_specs=col(0),
        compiler_params=_cparams(("parallel",)),
    )(gu, gu)


def _swiglu_bwd(gu, dact):
    t = gu.shape[0]
    bt = _block(t, 256, 16)

    def body(gt_ref, up_ref, da_ref, o_ref):
        gt = gt_ref[...]
        da = da_ref[...]
        sg = _sigmoid(gt)
        silu = gt * sg
        o_ref[:, :D_FF] = (da * up_ref[...] * (sg + silu * (1.0 - sg))).astype(BF16)
        o_ref[:, D_FF:] = (da * silu).astype(BF16)

    col = lambda j: pl.BlockSpec((bt, D_FF), lambda i: (i, j))
    return pl.pallas_call(
        body, name="swiglu_bwd", out_shape=jax.ShapeDtypeStruct((t, 2 * D_FF), BF16), grid=(t // bt,),
        in_specs=[col(0), col(1), col(0)], out_specs=pl.BlockSpec((bt, 2 * D_FF), lambda i: (i, 0)),
        compiler_params=_cparams(("parallel",)),
    )(gu, gu, dact)


def _step(x, mem, loss_target, weights, moments_m, moments_v):
    nb, s, d = x.shape
    n_mem = mem.shape[1]
    t = nb * s
    blk = min(ATT_BLOCK, s)
    x2 = x.reshape(t, d)
    mem2 = mem.reshape(nb * n_mem, d)
    tgt2 = loss_target.reshape(t, d)
    w2 = {n: (weights[n].reshape(weights[n].shape[1:]) if weights[n].ndim == 3 else weights[n]) for n, _, _ in SHARDED}

    full = _unpack_full(_gather_weights(_pack_local([w2[n] for n, _, _ in SHARDED], BF16)))
    w_in, w_pool_out, w_fox_out, w_out, w_xq, w_xkv, w_xo, w_ffn_in, w_ffn_out = full
    w_main = jnp.concatenate([w_in[:, :2048], w_in[:, 2056:]], axis=1)
    w_f = jnp.pad(w_in[:, 2048:2056], ((0, 0), (0, LANES - FOX_HEADS)))

    g_mix, g_x, g_mem, g_ffn = (weights[n] for n in ("norm_mix_g", "norm_x_g", "norm_mem_g", "norm_ffn_g"))
    g_final = weights["norm_final_g"].reshape(1, d)
    pool_w = weights["pool_w"].reshape(4, POOL_GC, POOL_GC)
    pool_scale, b_gate = weights["pool_scale"], weights["b_gate"]
    b_f_pad = jnp.pad(weights["b_forget"], ((0, 0), (0, LANES - FOX_HEADS)))
    b_f_exp = jnp.repeat(weights["b_forget"], FOX_DH, axis=1)

    h = _rms_fwd(x2, g_mix, "norm_mix")
    proj = _mm(h, w_main, name="in_proj")
    f_pad = _mm(h, w_f, name="in_proj_forget")
    proj3 = proj.reshape(nb, s, 4096)
    y = _pool_fwd(proj3, pool_w, pool_scale)
    f_exp = jnp.repeat(f_pad[:, :FOX_HEADS], FOX_DH, axis=1).reshape(nb, s, FOX_WIDTH)
    c_exp = _forget_cumsum(f_exp, b_f_exp)
    c_row = c_exp[:, :, ::FOX_DH].transpose(0, 2, 1).reshape(nb, FOX_HEADS, s // blk, 1, blk)
    o, lse = _fox_fwd(proj3, c_exp, c_row)
    y2, o2 = y.reshape(t, POOL_WIDTH), o.reshape(t, FOX_WIDTH)
    y_pool = _mm(y2, w_pool_out, name="pool_out")
    y_fox = _mm(o2, w_fox_out, name="fox_out")
    mix = _mix_fwd(proj, b_gate, y_pool, y_fox)
    x1 = _mm(mix, w_out, res=x2, name="mix_out")
    hx = _rms_fwd(x1, g_x, "norm_x")
    mem_n = _rms_fwd(mem2, g_mem, "norm_mem")
    qx = _mm(hx, w_xq, out_dtype=BF16, name="x_q")
    kv = _mm(mem_n, w_xkv, out_dtype=BF16, name="x_kv")
    qx3, kv3 = qx.reshape(nb, s, X_WIDTH), kv.reshape(nb, n_mem, 2 * X_WIDTH)
    ox = _xattn_fwd(qx3, kv3).reshape(t, X_WIDTH)
    x2_ = _mm(ox, w_xo, res=x1, name="x_out")
    hf = _rms_fwd(x2_, g_ffn, "norm_ffn")
    gu = _mm(hf, w_ffn_in, name="ffn_in")
    act = _swiglu_fwd(gu)
    x3 = _mm(act, w_ffn_out, res=x2_, bk=D_FF, name="ffn_out")

    dx3, dg_final, loss_part = _final_loss(x3, tgt2, g_final)
    dw_ffn_out = _mm(act, dx3, ta=True, name="d_w_ffn_out")
    dact = _mm(dx3, w_ffn_out, tb=True, name="d_act")
    dgu = _swiglu_bwd(gu, dact)
    dw_ffn_in = _mm(hf, dgu, ta=True, name="d_w_ffn_in")
    dhf = _mm(dgu, w_ffn_in, tb=True, name="d_hf")
    dx2, dg_ffn = _rms_bwd(dhf, x2_, g_ffn, dx3, "norm_ffn_bwd")

    dw_xo = _mm(ox, dx2, ta=True, name="d_w_xo")
    dox = _mm(dx2, w_xo, tb=True, out_dtype=BF16, name="d_ox").reshape(nb, s, X_WIDTH)
    dqx, dkv = _xattn_bwd(qx3, kv3, dox)
    dqx2, dkv2 = dqx.reshape(t, X_WIDTH), dkv.reshape(nb * n_mem, 2 * X_WIDTH)
    dw_xkv = _mm(mem_n, dkv2, ta=True, name="d_w_xkv")
    dmem_n = _mm(dkv2, w_xkv, tb=True, name="d_mem_n")
    dg_mem = _rms_bwd(dmem_n, mem2, g_mem, None, "norm_mem_bwd")
    dw_xq = _mm(hx, dqx2, ta=True, name="d_w_xq")
    dhx = _mm(dqx2, w_xq, tb=True, name="d_hx")
    dx1, dg_x = _rms_bwd(dhx, x1, g_x, dx2, "norm_x_bwd")

    dw_out = _mm(mix, dx1, ta=True, name="d_w_out")
    dmix = _mm(dx1, w_out, tb=True, name="d_mix")
    dyp, dyf, dgl, db_gate = _mix_bwd(proj, b_gate, y_pool, y_fox, dmix)
    dw_pool_out = _mm(y2, dyp, ta=True, name="d_w_pool_out")
    dw_fox_out = _mm(o2, dyf, ta=True, name="d_w_fox_out")
    dy = _mm(dyp, w_pool_out, tb=True, name="d_y").reshape(nb, s, POOL_WIDTH)
    do = _mm(dyf, w_fox_out, tb=True, name="d_o").reshape(nb, s, FOX_WIDTH)
    du, dpool_w, dpool_scale = _pool_bwd(proj3, dy, pool_w, pool_scale)
    dq, delta, dc_q = _fox_bwd_q(proj3, c_exp, c_row, lse, o, do)
    dk, dv, dc_row = _fox_bwd_kv(proj3, c_exp, c_row, lse, delta, do)
    dc = dc_row.reshape(nb, FOX_HEADS, s).transpose(0, 2, 1) + dc_q[:, :, ::FOX_DH]
    dc = jnp.pad(dc, ((0, 0), (0, 0), (0, LANES - FOX_HEADS)))
    df, db_f = _forget_bwd(dc, f_pad.reshape(nb, s, LANES), b_f_pad)
    dproj = jnp.concatenate([du.reshape(t, -1), dq.reshape(t, -1), dk.reshape(t, -1), dv.reshape(t, -1), dgl], axis=1)
    df2 = df.reshape(t, LANES)
    dw_main = _mm(h, dproj, ta=True, name="d_w_main")
    dw_f = _mm(h, df2, ta=True, name="d_w_forget")
    dh = _mm(df2, w_f, tb=True, name="d_h_forget")
    dh = _mm(dproj, w_main, tb=True, res=dh, name="d_h")
    dx, dg_mix = _rms_bwd(dh, x2, g_mix, dx1, "norm_mix_bwd")
    dw_in = jnp.concatenate([dw_main[:, :2048], dw_f[:, :FOX_HEADS], dw_main[:, 2048:]], axis=1)

    grads_full = [dw_in, dw_pool_out, dw_fox_out, dw_out, dw_xq, dw_xkv, dw_xo, dw_ffn_in, dw_ffn_out]
    packed = _pack_full(grads_full)
    core = lax.axis_index("c").astype(jnp.int32).reshape(1)
    chip_sums = _sum_halves(packed, _swap_halves(packed), core)
    reduced = _join_halves(_sum_chips(_exchange_chips(chip_sums)))
    g_shards = _unpack_local(reduced)

    small_grads = {"norm_mix_g": dg_mix, "b_forget": db_f[:, :FOX_HEADS], "b_gate": db_gate, "pool_w": dpool_w,
                   "pool_scale": dpool_scale, "norm_x_g": dg_x, "norm_mem_g": dg_mem, "norm_ffn_g": dg_ffn,
                   "norm_final_g": dg_final}
    parts = _gather_small(_pack_small([small_grads[n] for n, _ in SMALL], last=loss_part[0, 0]))
    sg, sd, sm, sv = _adamw_small(parts.reshape(N_DEV, SMALL_ROWS, LANES),
                                  _pack_small([weights[n] for n, _ in SMALL]),
                                  _pack_small([moments_m[n] for n, _ in SMALL]),
                                  _pack_small([moments_v[n] for n, _ in SMALL]))
    loss = sg.reshape(-1)[LOSS_POS]

    grads, deltas, new_m, new_v = {}, {}, {}, {}
    for (n, _), g_, d_, m_, v_ in zip(SMALL, _unpack_small(sg), _unpack_small(sd), _unpack_small(sm), _unpack_small(sv)):
        grads[n], deltas[n], new_m[n], new_v[n] = g_, d_, m_, v_
    for (n, _, _), g_ in zip(SHARDED, g_shards):
        shp = weights[n].shape
        d_, m_, v_ = _adamw(w2[n], g_, moments_m[n].reshape(g_.shape), moments_v[n].reshape(g_.shape), "adamw_" + n)
        grads[n], deltas[n], new_m[n], new_v[n] = (a.reshape(shp) for a in (g_, d_, m_, v_))
    return loss, dx.reshape(nb, s, d), grads, deltas, new_m, new_v


def kernel(x, mem, norm_mix_g, w_in, b_forget, b_gate, pool_w, pool_scale, w_pool_out, w_fox_out, w_out, norm_x_g, norm_mem_g, w_xq, w_xkv, w_xo, norm_ffn_g, w_ffn_in, w_ffn_out, norm_final_g, loss_target, m_norm_mix_g, m_w_in, m_b_forget, m_b_gate, m_pool_w, m_pool_scale, m_w_pool_out, m_w_fox_out, m_w_out, m_norm_x_g, m_norm_mem_g, m_w_xq, m_w_xkv, m_w_xo, m_norm_ffn_g, m_w_ffn_in, m_w_ffn_out, m_norm_final_g, v_norm_mix_g, v_w_in, v_b_forget, v_b_gate, v_pool_w, v_pool_scale, v_w_pool_out, v_w_fox_out, v_w_out, v_norm_x_g, v_norm_mem_g, v_w_xq, v_w_xkv, v_w_xo, v_norm_ffn_g, v_w_ffn_in, v_w_ffn_out, v_norm_final_g):
    given = dict(locals())
    weights = {n: given[n] for n in WEIGHT_ORDER}
    moments_m = {n: given["m_" + n] for n in WEIGHT_ORDER}
    moments_v = {n: given["v_" + n] for n in WEIGHT_ORDER}
    loss, grad_x, grads, deltas, new_m, new_v = _step(x, mem, loss_target, weights, moments_m, moments_v)
    return (loss, grad_x, *[grads[n] for n in WEIGHT_ORDER], *[deltas[n] for n in WEIGHT_ORDER],
            *[new_m[n] for n in WEIGHT_ORDER], *[new_v[n] for n in WEIGHT_ORDER])
```

```python
import functools
import math

import jax
import jax.numpy as jnp
from jax import lax
from jax.experimental import pallas as pl
from jax.experimental.pallas import tpu as pltpu

F32 = jnp.float32
BF16 = jnp.bfloat16
MESH = pl.DeviceIdType.MESH

D_MODEL = 1024
EPS = 1e-6
POOL_WINDOWS = (2, 4, 8, 16)
POOL_WIDTH = 512
POOL_GC = 128
FOX_HEADS = 8
FOX_DH = 64
FOX_WIDTH = 512
X_HEADS = 4
X_DH = 128
X_WIDTH = 512
D_FF = 2816
IN_COLS = 4104
GATE_WIDTH = 2048
ADAM_LR = 0.001
ADAM_B1 = 0.9
ADAM_B2 = 0.999
ADAM_EPS = 1e-08
ADAM_WD = 0.01
ADAM_STEP = 10

N_CHIPS = 4
N_DEV = 8
LANES = 128
VMEM_LIMIT_BYTES = 56 * 1024 * 1024
NEG_INF = -1e30
ATT_BLOCK = 512

SHARDED = (
    ("w_in", (1024, IN_COLS), 1),
    ("w_pool_out", (POOL_WIDTH, 1024), 1),
    ("w_fox_out", (FOX_WIDTH, 1024), 1),
    ("w_out", (1024, 1024), 0),
    ("w_xq", (1024, X_WIDTH), 0),
    ("w_xkv", (1024, 2 * X_WIDTH), 0),
    ("w_xo", (X_WIDTH, 1024), 1),
    ("w_ffn_in", (1024, 2 * D_FF), 1),
    ("w_ffn_out", (D_FF, 1024), 0),
)
SMALL = (
    ("norm_mix_g", (1, 1024)),
    ("b_forget", (1, 8)),
    ("b_gate", (1, 2048)),
    ("pool_w", (1, 4, 128, 128)),
    ("pool_scale", (1, 512)),
    ("norm_x_g", (1, 1024)),
    ("norm_mem_g", (1, 1024)),
    ("norm_ffn_g", (1, 1024)),
    ("norm_final_g", (1024,)),
)
WEIGHT_ORDER = ("norm_mix_g", "w_in", "b_forget", "b_gate", "pool_w", "pool_scale", "w_pool_out", "w_fox_out", "w_out",
                "norm_x_g", "norm_mem_g", "w_xq", "w_xkv", "w_xo", "norm_ffn_g", "w_ffn_in", "w_ffn_out", "norm_final_g")


def _round_up(n, m):
    return (n + m - 1) // m * m


SMALL_ELEMS = sum(math.prod(s) for _, s in SMALL)
SMALL_ROWS = _round_up(SMALL_ELEMS // LANES + 1, 8)
LOSS_POS = SMALL_ROWS * LANES - 1


def _cparams(sem=None):
    return pltpu.CompilerParams(dimension_semantics=sem, vmem_limit_bytes=VMEM_LIMIT_BYTES)


def _block(dim, pref, unit):
    if dim <= pref:
        return dim
    best = None
    for b in range(unit, pref + 1, unit):
        if dim % b == 0:
            best = b
    assert best is not None, (dim, pref, unit)
    return best


def _rows_block(rows, cols, unit=16, elems=1 << 19):
    return _block(rows, max(unit, elems // cols // unit * unit), unit)


def _pack_small(parts, last=None):
    flat = jnp.concatenate([p.reshape(-1).astype(F32) for p in parts])
    flat = jnp.pad(flat, (0, SMALL_ROWS * LANES - flat.shape[0]))
    if last is not None:
        flat = flat.at[LOSS_POS].set(last)
    return flat.reshape(SMALL_ROWS, LANES)


def _unpack_small(packed):
    flat = packed.reshape(-1)
    out, off = [], 0
    for _, shape in SMALL:
        n = math.prod(shape)
        out.append(flat[off:off + n].reshape(shape))
        off += n
    return out


def _my_place():
    return lax.axis_index("x"), lax.axis_index("y"), lax.axis_index("c")


def _other_chips(x, y):
    return [(1 - x, y), (x, 1 - y), (1 - x, 1 - y)]


def _chip(place):
    return 2 * place[0] + place[1]


ANY = pl.BlockSpec(memory_space=pl.ANY)


def _gather_weights(shards):
    n = len(shards)

    def body(*refs):
        ins, outs = refs[:n], refs[n:2 * n]
        send_sems, recv_sems, local_sems = refs[2 * n:]
        x, y, c = _my_place()
        sibling = (x, y, 1 - c)
        chips = _other_chips(x, y)

        def half(k, chip, core):
            h = ins[k].shape[0] // 2
            return outs[k].at[_chip(chip), pl.ds(core * h, h), :]

        def copy(k, slot, chip, core, to, src=None):
            return pltpu.make_async_remote_copy(
                src_ref=half(k, chip, core) if src is None else src, dst_ref=half(k, chip, core),
                send_sem=send_sems.at[6 * k + slot], recv_sem=recv_sems.at[6 * k + slot],
                device_id=to, device_id_type=MESH)

        mine = [pltpu.make_async_copy(ins[k], outs[k].at[_chip((x, y))], local_sems.at[k]) for k in range(n)]
        for cp in mine:
            cp.start()
        first = []
        for j, chip in enumerate(chips):
            for k in range(n):
                h = ins[k].shape[0] // 2
                first.append(copy(k, j, (x, y), c, (*chip, c), src=ins[k].at[pl.ds(c * h, h), :]))
                first[-1].start()
        passed = []
        for j, chip in enumerate(chips):
            for k in range(n):
                copy(k, j, chip, c, (x, y, c)).wait_recv()
                passed.append(copy(k, 3 + j, chip, c, sibling))
                passed[-1].start()
        for j, chip in enumerate(chips):
            for k in range(n):
                copy(k, 3 + j, chip, 1 - c, (x, y, c)).wait_recv()
        for cp in first + passed:
            cp.wait_send()
        for cp in mine:
            cp.wait()

    return pl.pallas_call(
        body, name="gather_weights",
        out_shape=[jax.ShapeDtypeStruct((N_CHIPS,) + s.shape, s.dtype) for s in shards],
        in_specs=[ANY] * n, out_specs=[ANY] * n,
        scratch_shapes=[pltpu.SemaphoreType.DMA((6 * n,)), pltpu.SemaphoreType.DMA((6 * n,)),
                        pltpu.SemaphoreType.DMA((n,))],
    )(*shards)


def _swap_halves(grads):
    n = len(grads)

    def body(*refs):
        ins, outs = refs[:n], refs[n:2 * n]
        send_sems, recv_sems = refs[2 * n:]
        x, y, c = _my_place()
        copies = []
        for k in range(n):
            h = ins[k].shape[1] // 2
            copies.append(pltpu.make_async_remote_copy(
                src_ref=ins[k].at[:, pl.ds((1 - c) * h, h), :], dst_ref=outs[k],
                send_sem=send_sems.at[k], recv_sem=recv_sems.at[k], device_id=(x, y, 1 - c), device_id_type=MESH))
            copies[-1].start()
        for cp in copies:
            cp.wait()

    return pl.pallas_call(
        body, name="swap_halves",
        out_shape=[jax.ShapeDtypeStruct((N_CHIPS, g.shape[1] // 2, g.shape[2]), g.dtype) for g in grads],
        in_specs=[ANY] * n, out_specs=[ANY] * n,
        scratch_shapes=[pltpu.SemaphoreType.DMA((n,)), pltpu.SemaphoreType.DMA((n,))],
    )(*grads)


def _exchange_chips(sums):
    n = len(sums)

    def body(*refs):
        ins, outs = refs[:n], refs[n:2 * n]
        send_sems, recv_sems, local_sems = refs[2 * n:]
        x, y, c = _my_place()
        me = _chip((x, y))
        chips = _other_chips(x, y)
        mine = [pltpu.make_async_copy(ins[k].at[me], outs[k].at[me], local_sems.at[k]) for k in range(n)]
        for cp in mine:
            cp.start()
        sends = []
        for j, chip in enumerate(chips):
            for k in range(n):
                sends.append(pltpu.make_async_remote_copy(
                    src_ref=ins[k].at[_chip(chip)], dst_ref=outs[k].at[me],
                    send_sem=send_sems.at[3 * k + j], recv_sem=recv_sems.at[3 * k + j],
                    device_id=(*chip, c), device_id_type=MESH))
                sends[-1].start()
        for j, chip in enumerate(chips):
            for k in range(n):
                slot = outs[k].at[_chip(chip)]
                pltpu.make_async_remote_copy(
                    src_ref=slot, dst_ref=slot, send_sem=send_sems.at[3 * k + j], recv_sem=recv_sems.at[3 * k + j],
                    device_id=(x, y, c), device_id_type=MESH).wait_recv()
        for cp in sends:
            cp.wait_send()
        for cp in mine:
            cp.wait()

    return pl.pallas_call(
        body, name="exchange_chips",
        out_shape=[jax.ShapeDtypeStruct(s.shape, s.dtype) for s in sums],
        in_specs=[ANY] * n, out_specs=[ANY] * n,
        scratch_shapes=[pltpu.SemaphoreType.DMA((3 * n,)), pltpu.SemaphoreType.DMA((3 * n,)),
                        pltpu.SemaphoreType.DMA((n,))],
    )(*sums)


def _join_halves(halves):
    n = len(halves)

    def body(*refs):
        ins, outs = refs[:n], refs[n:2 * n]
        send_sems, recv_sems, local_sems = refs[2 * n:]
        x, y, c = _my_place()
        started = []
        for k in range(n):
            h = ins[k].shape[0]
            rows = outs[k].at[pl.ds(c * h, h), :]
            started.append(pltpu.make_async_copy(ins[k], rows, local_sems.at[k]))
            started[-1].start()
            started.append(pltpu.make_async_remote_copy(
                src_ref=ins[k], dst_ref=rows, send_sem=send_sems.at[k], recv_sem=recv_sems.at[k],
                device_id=(x, y, 1 - c), device_id_type=MESH))
            started[-1].start()
        for k in range(n):
            h = ins[k].shape[0]
            theirs = outs[k].at[pl.ds((1 - c) * h, h), :]
            pltpu.make_async_remote_copy(
                src_ref=theirs, dst_ref=theirs, send_sem=send_sems.at[k], recv_sem=recv_sems.at[k],
                device_id=(x, y, c), device_id_type=MESH).wait_recv()
        for k in range(n):
            started[2 * k + 1].wait_send()
            started[2 * k].wait()

    return pl.pallas_call(
        body, name="join_halves",
        out_shape=[jax.ShapeDtypeStruct((2 * h.shape[0], h.shape[1]), h.dtype) for h in halves],
        in_specs=[ANY] * n, out_specs=[ANY] * n,
        scratch_shapes=[pltpu.SemaphoreType.DMA((n,)), pltpu.SemaphoreType.DMA((n,)), pltpu.SemaphoreType.DMA((n,))],
    )(*halves)


def _gather_small(block):
    m_per = block.shape[0]

    def body(x_ref, out_ref, send_sems, recv_sems, local_sem):
        x, y, c = _my_place()
        me, sibling = (x, y, c), (x, y, 1 - c)
        chips = _other_chips(x, y)

        def rows(px, py, pc):
            return out_ref.at[pl.ds((4 * px + 2 * py + pc) * m_per, m_per), :]

        def copy(k, blk, to, src=None):
            return pltpu.make_async_remote_copy(
                src_ref=rows(*blk) if src is None else src, dst_ref=rows(*blk),
                send_sem=send_sems.at[k], recv_sem=recv_sems.at[k], device_id=to, device_id_type=MESH)

        mine = pltpu.make_async_copy(x_ref, rows(*me), local_sem)
        mine.start()
        first = [copy(0, me, sibling, src=x_ref)]
        first += [copy(1 + j, me, (*chip, c), src=x_ref) for j, chip in enumerate(chips)]
        for cp in first:
            cp.start()
        passed = [copy(4 + j, (*chip, c), sibling) for j, chip in enumerate(chips)]
        for j, chip in enumerate(chips):
            copy(1 + j, (*chip, c), me).wait_recv()
            passed[j].start()
        copy(0, sibling, me).wait_recv()
        for j, chip in enumerate(chips):
            copy(4 + j, (*chip, 1 - c), me).wait_recv()
        for cp in first + passed:
            cp.wait_send()
        mine.wait()

    return pl.pallas_call(
        body, name="gather_small",
        out_shape=jax.ShapeDtypeStruct((N_DEV * m_per, LANES), block.dtype),
        in_specs=[pl.BlockSpec(memory_space=pltpu.VMEM)],
        out_specs=pl.BlockSpec(memory_space=pltpu.VMEM),
        scratch_shapes=[pltpu.SemaphoreType.DMA((7,)), pltpu.SemaphoreType.DMA((7,)), pltpu.SemaphoreType.DMA],
    )(block)


def _sum_halves(grads, theirs, core, name):
    _, h, cols = theirs.shape
    br = _rows_block(h, cols)
    nb = h // br

    def body(core_ref, a_ref, b_ref, o_ref):
        o_ref[...] = (a_ref[...] + b_ref[...]).astype(BF16)

    return pl.pallas_call(
        body, name=name,
        out_shape=jax.ShapeDtypeStruct(theirs.shape, BF16),
        grid_spec=pltpu.PrefetchScalarGridSpec(
            num_scalar_prefetch=1, grid=(N_CHIPS, nb),
            in_specs=[pl.BlockSpec((1, br, cols), lambda j, i, core_ref: (j, core_ref[0] * nb + i, 0)),
                      pl.BlockSpec((1, br, cols), lambda j, i, core_ref: (j, i, 0))],
            out_specs=pl.BlockSpec((1, br, cols), lambda j, i, core_ref: (j, i, 0))),
        compiler_params=_cparams(("parallel", "parallel")),
    )(core, grads, theirs)


def _sum_chips(slots, name):
    _, h, cols = slots.shape
    br = _rows_block(h, cols)

    def body(s_ref, o_ref):
        acc = s_ref[0].astype(F32)
        for k in range(1, N_CHIPS):
            acc = acc + s_ref[k].astype(F32)
        o_ref[...] = acc

    return pl.pallas_call(
        body, name=name,
        out_shape=jax.ShapeDtypeStruct((h, cols), F32),
        grid=(h // br,),
        in_specs=[pl.BlockSpec((N_CHIPS, br, cols), lambda i: (0, i, 0))],
        out_specs=pl.BlockSpec((br, cols), lambda i: (i, 0)),
        compiler_params=_cparams(("parallel",)),
    )(slots)


def _adamw_math(w, g, m, v):
    m = ADAM_B1 * m + (1.0 - ADAM_B1) * g
    v = ADAM_B2 * v + (1.0 - ADAM_B2) * (g * g)
    m_hat = m / (1.0 - ADAM_B1 ** ADAM_STEP)
    v_hat = v / (1.0 - ADAM_B2 ** ADAM_STEP)
    delta = -ADAM_LR * (m_hat / (jnp.sqrt(v_hat) + ADAM_EPS) + ADAM_WD * w)
    return delta, m, v


def _adamw(w, g, m, v, name):
    rows, cols = w.shape
    br = _rows_block(rows, cols, unit=8)

    def body(w_ref, g_ref, m_ref, v_ref, d_ref, nm_ref, nv_ref):
        d, nm, nv = _adamw_math(w_ref[...], g_ref[...], m_ref[...], v_ref[...])
        d_ref[...] = d
        nm_ref[...] = nm
        nv_ref[...] = nv

    spec = pl.BlockSpec((br, cols), lambda i: (i, 0))
    shape = jax.ShapeDtypeStruct(w.shape, F32)
    return pl.pallas_call(
        body, name=name, out_shape=(shape, shape, shape), grid=(rows // br,),
        in_specs=[spec] * 4, out_specs=(spec, spec, spec),
        compiler_params=_cparams(("parallel",)),
    )(w, g, m, v)


def _adamw_small(parts, w, m, v):
    def body(p_ref, w_ref, m_ref, v_ref, g_ref, d_ref, nm_ref, nv_ref):
        g = p_ref[0]
        for k in range(1, N_DEV):
            g = g + p_ref[k]
        d, nm, nv = _adamw_math(w_ref[...], g, m_ref[...], v_ref[...])
        g_ref[...] = g
        d_ref[...] = d
        nm_ref[...] = nm
        nv_ref[...] = nv

    shape = jax.ShapeDtypeStruct((SMALL_ROWS, LANES), F32)
    return pl.pallas_call(body, name="adamw_small", out_shape=(shape,) * 4, compiler_params=_cparams())(parts, w, m, v)


def _mm(a, b, *, name, ta=False, out_dtype=F32, res=None, bm=1024, bn=1024, bk=4096, b_stack=False, out_stack=False):
    if ta:
        kdim, m = a.shape
    else:
        m, kdim = a.shape
    if b_stack:
        _, kb, chunk = b.shape
        n = N_CHIPS * chunk
    else:
        kb, n = b.shape
        chunk = n // N_CHIPS if out_stack else n
    assert kdim == kb, (a.shape, b.shape, ta)
    bm = _block(m, bm, LANES if ta else 16)
    bn = _block(chunk, bn, LANES)
    bk = _block(kdim, bk, LANES)
    nk = kdim // bk
    per_chunk = chunk // bn
    dims = (((0 if ta else 1,), (0,)), ((), ()))

    def body(*refs):
        refs = list(refs)
        a_ref, b_ref = refs[:2]
        r_ref = refs[2] if res is not None else None
        o_ref = refs[3] if res is not None else refs[2]
        part = lax.dot_general(a_ref[...].astype(BF16), b_ref[...].astype(BF16), dims, preferred_element_type=F32)

        def finish(r):
            if r_ref is not None:
                r = r + r_ref[...]
            o_ref[...] = r.astype(out_dtype)

        if nk == 1:
            finish(part)
        else:
            acc_ref = refs[-1]
            k = pl.program_id(2)

            @pl.when(k == 0)
            def _():
                acc_ref[...] = part

            @pl.when(k > 0)
            def _():
                acc_ref[...] += part

            @pl.when(k == nk - 1)
            def _():
                finish(acc_ref[...])

    a_spec = pl.BlockSpec((bk, bm), lambda i, j, k: (k, i)) if ta else pl.BlockSpec((bm, bk), lambda i, j, k: (i, k))
    if b_stack:
        b_spec = pl.BlockSpec((None, bk, bn), lambda i, j, k: (j // per_chunk, k, j % per_chunk))
    else:
        b_spec = pl.BlockSpec((bk, bn), lambda i, j, k: (k, j))
    r_spec = pl.BlockSpec((bm, bn), lambda i, j, k: (i, j))
    if out_stack:
        o_spec = pl.BlockSpec((None, bm, bn), lambda i, j, k: (j // per_chunk, i, j % per_chunk))
        o_shape = (N_CHIPS, m, chunk)
    else:
        o_spec, o_shape = r_spec, (m, n)
    in_specs = [a_spec, b_spec] + ([r_spec] if res is not None else [])
    args = (a, b) + ((res,) if res is not None else ())
    return pl.pallas_call(
        body, name=name, out_shape=jax.ShapeDtypeStruct(o_shape, out_dtype),
        grid=(m // bm, n // bn, nk), in_specs=in_specs, out_specs=o_spec,
        scratch_shapes=[pltpu.VMEM((bm, bn), F32)] if nk > 1 else [],
        compiler_params=_cparams(("parallel", "parallel", "arbitrary")),
    )(*args)


def _rms_fwd(x, g, name):
    t, d = x.shape
    bt = _block(t, 512, 16)

    def body(x_ref, g_ref, h_ref):
        xv = x_ref[...]
        r = lax.rsqrt(jnp.mean(xv * xv, axis=-1, keepdims=True) + EPS)
        h_ref[...] = (xv * r * g_ref[...]).astype(BF16)

    return pl.pallas_call(
        body, name=name, out_shape=jax.ShapeDtypeStruct((t, d), BF16), grid=(t // bt,),
        in_specs=[pl.BlockSpec((bt, d), lambda i: (i, 0)), pl.BlockSpec((1, d), lambda i: (0, 0))],
        out_specs=pl.BlockSpec((bt, d), lambda i: (i, 0)),
        compiler_params=_cparams(("parallel",)),
    )(x, g)


def _rms_bwd(dh, x, g, dres, name):
    t, d = x.shape
    bt = _block(t, 256, 16)
    want_dx = dres is not None

    def body(*refs):
        if want_dx:
            dh_ref, x_ref, g_ref, dres_ref, dx_ref, dxb_ref, dg_ref = refs
        else:
            dh_ref, x_ref, g_ref, dg_ref = refs
        xv = x_ref[...]
        r = lax.rsqrt(jnp.mean(xv * xv, axis=-1, keepdims=True) + EPS)
        xhat = xv * r
        dhv = dh_ref[...]

        @pl.when(pl.program_id(0) == 0)
        def _():
            dg_ref[...] = jnp.zeros_like(dg_ref)

        dg_ref[...] += jnp.sum(dhv * xhat, axis=0, keepdims=True)
        if want_dx:
            dxhat = dhv * g_ref[...]
            dx = dres_ref[...] + r * (dxhat - xhat * jnp.mean(dxhat * xhat, axis=-1, keepdims=True))
            dx_ref[...] = dx
            dxb_ref[...] = dx.astype(BF16)

    row = pl.BlockSpec((bt, d), lambda i: (i, 0))
    vec = pl.BlockSpec((1, d), lambda i: (0, 0))
    if want_dx:
        return pl.pallas_call(
            body, name=name, grid=(t // bt,),
            out_shape=(jax.ShapeDtypeStruct((t, d), F32), jax.ShapeDtypeStruct((t, d), BF16),
                       jax.ShapeDtypeStruct((1, d), F32)),
            in_specs=[row, row, vec, row], out_specs=(row, row, vec),
            compiler_params=_cparams(("arbitrary",)),
        )(dh, x, g, dres)
    return pl.pallas_call(
        body, name=name, grid=(t // bt,), out_shape=jax.ShapeDtypeStruct((1, d), F32),
        in_specs=[row, row, vec], out_specs=vec,
        compiler_params=_cparams(("arbitrary",)),
    )(dh, x, g)


def _final_loss(x, target, g):
    t, d = x.shape
    bt = _block(t, 256, 16)

    def body(x_ref, t_ref, g_ref, dx_ref, dxb_ref, dg_ref, loss_ref):
        xv = x_ref[...]
        gv = g_ref[...]
        r = lax.rsqrt(jnp.mean(xv * xv, axis=-1, keepdims=True) + EPS)
        xhat = xv * r
        err = xhat * gv - t_ref[...]

        @pl.when(pl.program_id(0) == 0)
        def _():
            dg_ref[...] = jnp.zeros_like(dg_ref)
            loss_ref[...] = jnp.zeros_like(loss_ref)

        loss_ref[...] += 0.5 * jnp.sum(jnp.mean(err * err, axis=-1, keepdims=True), axis=0, keepdims=True)
        dy = err * (1.0 / d)
        dg_ref[...] += jnp.sum(dy * xhat, axis=0, keepdims=True)
        dxhat = dy * gv
        dx = r * (dxhat - xhat * jnp.mean(dxhat * xhat, axis=-1, keepdims=True))
        dx_ref[...] = dx
        dxb_ref[...] = dx.astype(BF16)

    row = pl.BlockSpec((bt, d), lambda i: (i, 0))
    vec = pl.BlockSpec((1, d), lambda i: (0, 0))
    return pl.pallas_call(
        body, name="final_loss", grid=(t // bt,),
        out_shape=(jax.ShapeDtypeStruct((t, d), F32), jax.ShapeDtypeStruct((t, d), BF16),
                   jax.ShapeDtypeStruct((1, d), F32), jax.ShapeDtypeStruct((1, LANES), F32)),
        in_specs=[row, row, vec], out_specs=(row, row, vec, pl.BlockSpec((1, LANES), lambda i: (0, 0))),
        compiler_params=_cparams(("arbitrary",)),
    )(x, target, g)


GU_COLS = GATE_WIDTH + POOL_WIDTH
U_BLK = GATE_WIDTH // POOL_WIDTH


def _shift_down(a, k, row):
    return jnp.where(row >= k, pltpu.roll(a, k, 0), 0.0)


def _shift_up(a, k, row):
    n = a.shape[0]
    return jnp.where(row < n - k, pltpu.roll(a, n - k, 0), 0.0)


def _window_delta(u, w, row):
    s, k = u, 1
    while k < w:
        s = s + _shift_down(s, k, row)
        k *= 2
    cnt = jnp.minimum(row + 1, w).astype(F32)
    return s / cnt - u, cnt


def _pool_fwd(gu, pool_w, pool_scale):
    b, s, _ = gu.shape

    def body(u_ref, pw_ref, sc_ref, y_ref):
        row = lax.broadcasted_iota(jnp.int32, (s, POOL_GC), 0)
        for g, w in enumerate(POOL_WINDOWS):
            cols = slice(g * POOL_GC, (g + 1) * POOL_GC)
            d, _ = _window_delta(u_ref[0, :, cols], w, row)
            z = jnp.dot(d.astype(BF16), pw_ref[g].astype(BF16), preferred_element_type=F32)
            y_ref[0, :, cols] = (z * sc_ref[:, cols]).astype(BF16)

    return pl.pallas_call(
        body, name="pool_fwd", out_shape=jax.ShapeDtypeStruct((b, s, POOL_WIDTH), BF16), grid=(b,),
        in_specs=[pl.BlockSpec((1, s, POOL_WIDTH), lambda i: (i, 0, U_BLK)),
                  pl.BlockSpec((4, POOL_GC, POOL_GC), lambda i: (0, 0, 0)),
                  pl.BlockSpec((1, POOL_WIDTH), lambda i: (0, 0))],
        out_specs=pl.BlockSpec((1, s, POOL_WIDTH), lambda i: (i, 0, 0)),
        compiler_params=_cparams(("parallel",)),
    )(gu, pool_w, pool_scale)


def _pool_bwd(gu, dy, pool_w, pool_scale, dgu):
    b, s, _ = gu.shape

    def body(u_ref, dy_ref, pw_ref, sc_ref, dgu_in, du_ref, dpw_ref, dsc_ref):
        del dgu_in

        @pl.when(pl.program_id(0) == 0)
        def _():
            dpw_ref[...] = jnp.zeros_like(dpw_ref)
            dsc_ref[...] = jnp.zeros_like(dsc_ref)

        row = lax.broadcasted_iota(jnp.int32, (s, POOL_GC), 0)
        for g, w in enumerate(POOL_WINDOWS):
            cols = slice(g * POOL_GC, (g + 1) * POOL_GC)
            d, cnt = _window_delta(u_ref[0, :, cols], w, row)
            db = d.astype(BF16)
            pw = pw_ref[g].astype(BF16)
            z = jnp.dot(db, pw, preferred_element_type=F32)
            dyv = dy_ref[0, :, cols]
            dsc_ref[:, cols] += jnp.sum(dyv * z, axis=0, keepdims=True)
            dz = (dyv * sc_ref[:, cols]).astype(BF16)
            dpw_ref[g] += lax.dot_general(db, dz, (((0,), (0,)), ((), ())), preferred_element_type=F32)
            dd = lax.dot_general(dz, pw, (((1,), (1,)), ((), ())), preferred_element_type=F32)
            acc, k = dd / cnt, 1
            while k < w:
                acc = acc + _shift_up(acc, k, row)
                k *= 2
            du_ref[0, :, cols] = (acc - dd).astype(BF16)

    return pl.pallas_call(
        body, name="pool_bwd", grid=(b,),
        out_shape=(jax.ShapeDtypeStruct((b, s, GU_COLS), BF16), jax.ShapeDtypeStruct((4, POOL_GC, POOL_GC), F32),
                   jax.ShapeDtypeStruct((1, POOL_WIDTH), F32)),
        in_specs=[pl.BlockSpec((1, s, POOL_WIDTH), lambda i: (i, 0, U_BLK)),
                  pl.BlockSpec((1, s, POOL_WIDTH), lambda i: (i, 0, 0)),
                  pl.BlockSpec((4, POOL_GC, POOL_GC), lambda i: (0, 0, 0)),
                  pl.BlockSpec((1, POOL_WIDTH), lambda i: (0, 0)), ANY],
        out_specs=(pl.BlockSpec((1, s, POOL_WIDTH), lambda i: (i, 0, U_BLK)),
                   pl.BlockSpec((4, POOL_GC, POOL_GC), lambda i: (0, 0, 0)),
                   pl.BlockSpec((1, POOL_WIDTH), lambda i: (0, 0))),
        input_output_aliases={4: 0},
        compiler_params=_cparams(("arbitrary",)),
    )(gu, dy, pool_w, pool_scale, dgu)


def _forget_cumsum(f, bias):
    b, s, c = f.shape

    def body(f_ref, b_ref, c_ref):
        row = lax.broadcasted_iota(jnp.int32, (s, LANES), 0)
        z = f_ref[0] + b_ref[...]
        acc = jnp.minimum(z, 0.0) - jnp.log(1.0 + jnp.exp(-jnp.abs(z)))
        k = 1
        while k < s:
            acc = acc + _shift_down(acc, k, row)
            k *= 2
        c_ref[0] = acc

    return pl.pallas_call(
        body, name="forget_cumsum", out_shape=jax.ShapeDtypeStruct((b, s, c), F32), grid=(b, c // LANES),
        in_specs=[pl.BlockSpec((1, s, LANES), lambda i, j: (i, 0, j)), pl.BlockSpec((1, LANES), lambda i, j: (0, j))],
        out_specs=pl.BlockSpec((1, s, LANES), lambda i, j: (i, 0, j)),
        compiler_params=_cparams(("parallel", "parallel")),
    )(f, bias)


def _forget_bwd(dc, f, bias):
    b, s, _ = f.shape

    def body(dc_ref, f_ref, b_ref, df_ref, db_ref):
        @pl.when(pl.program_id(0) == 0)
        def _():
            db_ref[...] = jnp.zeros_like(db_ref)

        row = lax.broadcasted_iota(jnp.int32, (s, LANES), 0)
        acc, k = dc_ref[0], 1
        while k < s:
            acc = acc + _shift_up(acc, k, row)
            k *= 2
        z = f_ref[0] + b_ref[...]
        df = acc / (1.0 + jnp.exp(z))
        db_ref[...] += jnp.sum(df, axis=0, keepdims=True)
        df_ref[0] = df.astype(BF16)

    blk = pl.BlockSpec((1, s, LANES), lambda i: (i, 0, 0))
    vec = pl.BlockSpec((1, LANES), lambda i: (0, 0))
    return pl.pallas_call(
        body, name="forget_bwd", grid=(b,),
        out_shape=(jax.ShapeDtypeStruct((b, s, LANES), BF16), jax.ShapeDtypeStruct((1, LANES), F32)),
        in_specs=[blk, blk, vec], out_specs=(blk, vec),
        compiler_params=_cparams(("arbitrary",)),
    )(dc, f, bias)


KV_BLK0 = 2
PAIRS = FOX_HEADS // 2
FOX_SCALE = FOX_DH ** -0.5
NT_DIMS = (((1,), (1,)), ((), ()))
TN_DIMS = (((0,), (0,)), ((), ()))


def _stack_heads(v):
    head = lax.broadcasted_iota(jnp.int32, v.shape, 1) // FOX_DH
    zero = jnp.zeros_like(v)
    return jnp.concatenate([jnp.where(head == 0, v, zero), jnp.where(head == 1, v, zero)], axis=0)


def _stack_cols(v):
    return jnp.concatenate([v[:, 0:1], v[:, FOX_DH:FOX_DH + 1]], axis=0)


def _unstack(t, blk):
    head = lax.broadcasted_iota(jnp.int32, (blk, LANES), 1) // FOX_DH
    return jnp.where(head == 0, t[:blk], t[blk:])


def _fox_scores(q_all, kblk, row_bias, cr_ref, kb, masked, blk):
    top = lax.broadcasted_iota(jnp.int32, (2 * blk, 1), 0) < blk
    s = lax.dot_general(q_all, kblk, NT_DIMS, preferred_element_type=F32)
    s = s + (row_bias - jnp.where(top, cr_ref[0, 0, kb], cr_ref[0, 1, kb]))
    if masked:
        r = lax.broadcasted_iota(jnp.int32, (2 * blk, blk), 0)
        keep = jnp.where(r >= blk, r - blk, r) >= lax.broadcasted_iota(jnp.int32, (2 * blk, blk), 1)
        s = jnp.where(keep, s, NEG_INF)
    return s


def _fox_fwd(qkv, c_exp, c_row):
    b, s, _ = qkv.shape
    blk = min(ATT_BLOCK, s)
    nq = s // blk

    def body(q_ref, kv_ref, cc_ref, cr_ref, o_ref, ob_ref, lse_ref):
        qi = pl.program_id(2)
        q_all = _stack_heads(q_ref[0] * FOX_SCALE)
        cq = _stack_cols(cc_ref[0])

        def step(kb, carry, masked):
            m, l, acc = carry
            rows = pl.ds(pl.multiple_of(kb * blk, blk), blk)
            sc = _fox_scores(q_all, kv_ref[0, rows, :LANES], cq, cr_ref, kb, masked, blk)
            m_new = jnp.maximum(m, jnp.max(sc, axis=-1, keepdims=True))
            p = jnp.exp(sc - m_new)
            alpha = jnp.exp(m - m_new)
            l = alpha * l + jnp.sum(p, axis=-1, keepdims=True)
            acc = alpha * acc + jnp.dot(p.astype(BF16), kv_ref[0, rows, LANES:], preferred_element_type=F32)
            return m_new, l, acc

        init = (jnp.full((2 * blk, 1), NEG_INF, F32), jnp.zeros((2 * blk, 1), F32), jnp.zeros((2 * blk, LANES), F32))
        m, l, acc = step(qi, lax.fori_loop(0, qi, functools.partial(step, masked=False), init), True)
        o = _unstack(acc / l, blk)
        o_ref[0] = o
        ob_ref[0] = o.astype(BF16)
        lse_ref[0] = _unstack(jnp.broadcast_to(m + jnp.log(l), (2 * blk, LANES)), blk)

    tile = pl.BlockSpec((1, blk, LANES), lambda i, h, q: (i, q, h))
    kvspec = pl.BlockSpec((1, s, 2 * LANES), lambda i, h, q: (i, 0, KV_BLK0 + h))
    shape = jax.ShapeDtypeStruct((b, s, FOX_WIDTH), F32)
    return pl.pallas_call(
        body, name="fox_fwd", out_shape=(shape, jax.ShapeDtypeStruct((b, s, FOX_WIDTH), BF16), shape),
        grid=(b, PAIRS, nq),
        in_specs=[tile, kvspec, tile, pl.BlockSpec((1, 2, nq, 1, blk), lambda i, h, q: (i, h, 0, 0, 0))],
        out_specs=(tile, tile, tile),
        compiler_params=_cparams(("parallel", "parallel", "arbitrary")),
    )(qkv, qkv, c_exp, c_row)


def _fox_bwd_q(qkv, c_exp, c_row, lse, o, do):
    b, s, _ = qkv.shape
    blk = min(ATT_BLOCK, s)
    nq = s // blk

    def body(q_ref, kv_ref, cc_ref, cr_ref, lse_ref, o_ref, do_ref, dq_ref, dl_ref, dcq_ref):
        qi = pl.program_id(2)
        q_all = _stack_heads(q_ref[0] * FOX_SCALE)
        dov = do_ref[0]
        do_all = _stack_heads(dov.astype(BF16))
        delta = jnp.sum(_stack_heads(dov * o_ref[0]), axis=-1, keepdims=True)
        bias = _stack_cols(cc_ref[0]) - _stack_cols(lse_ref[0])

        def step(kb, carry, masked):
            acc, dcq = carry
            rows = pl.ds(pl.multiple_of(kb * blk, blk), blk)
            kblk = kv_ref[0, rows, :LANES]
            p = jnp.exp(_fox_scores(q_all, kblk, bias, cr_ref, kb, masked, blk))
            dp = lax.dot_general(do_all, kv_ref[0, rows, LANES:], NT_DIMS, preferred_element_type=F32)
            ds = p * (dp - delta)
            acc = acc + jnp.dot(ds.astype(BF16), kblk, preferred_element_type=F32)
            return acc, dcq + jnp.sum(ds, axis=-1, keepdims=True)

        init = (jnp.zeros((2 * blk, LANES), F32), jnp.zeros((2 * blk, 1), F32))
        acc, dcq = step(qi, lax.fori_loop(0, qi, functools.partial(step, masked=False), init), True)
        dq_ref[0] = (_unstack(acc, blk) * FOX_SCALE).astype(BF16)
        dl_ref[0] = _unstack(jnp.broadcast_to(delta, (2 * blk, LANES)), blk)
        dcq_ref[0] = _unstack(jnp.broadcast_to(dcq, (2 * blk, LANES)), blk)

    tile = pl.BlockSpec((1, blk, LANES), lambda i, h, q: (i, q, h))
    kvspec = pl.BlockSpec((1, s, 2 * LANES), lambda i, h, q: (i, 0, KV_BLK0 + h))
    shape = jax.ShapeDtypeStruct((b, s, FOX_WIDTH), F32)
    return pl.pallas_call(
        body, name="fox_bwd_q", grid=(b, PAIRS, nq),
        out_shape=(jax.ShapeDtypeStruct(qkv.shape, BF16), shape, shape),
        in_specs=[tile, kvspec, tile, pl.BlockSpec((1, 2, nq, 1, blk), lambda i, h, q: (i, h, 0, 0, 0)),
                  tile, tile, tile],
        out_specs=(tile, tile, tile),
        compiler_params=_cparams(("parallel", "parallel", "arbitrary")),
    )(qkv, qkv, c_exp, c_row, lse, o, do)


def _fox_bwd_kv(qkv, c_exp, c_row, lse, delta, do, dqkv):
    b, s, _ = qkv.shape
    blk = min(ATT_BLOCK, s)
    nq = s // blk

    def body(q_ref, kv_ref, cc_ref, cr_ref, lse_ref, dl_ref, do_ref, dqkv_in, dkv_ref, dc_ref):
        del dqkv_in
        ki = pl.program_id(2)
        kblk = kv_ref[0, :, :LANES]
        vblk = kv_ref[0, :, LANES:]

        def step(qb, carry, masked):
            dk, dv, dc = carry
            rows = pl.ds(pl.multiple_of(qb * blk, blk), blk)
            q_all = _stack_heads(q_ref[0, rows, :] * FOX_SCALE)
            do_all = _stack_heads(do_ref[0, rows, :].astype(BF16))
            bias = _stack_cols(cc_ref[0, rows, :]) - _stack_cols(lse_ref[0, rows, :])
            delta = _stack_cols(dl_ref[0, rows, :])
            p = jnp.exp(_fox_scores(q_all, kblk, bias, cr_ref, 0, masked, blk))
            dv = dv + lax.dot_general(p.astype(BF16), do_all, TN_DIMS, preferred_element_type=F32)
            dp = lax.dot_general(do_all, vblk, NT_DIMS, preferred_element_type=F32)
            ds = p * (dp - delta)
            dk = dk + lax.dot_general(ds.astype(BF16), q_all, TN_DIMS, preferred_element_type=F32)
            col = jnp.concatenate([jnp.sum(ds[:blk], axis=0, keepdims=True), jnp.sum(ds[blk:], axis=0, keepdims=True)],
                                  axis=0)
            return dk, dv, dc - col

        zero = jnp.zeros((blk, LANES), F32)
        carry = step(ki, (zero, zero, jnp.zeros((2, blk), F32)), True)
        dk, dv, dc = lax.fori_loop(ki + 1, nq, functools.partial(step, masked=False), carry)
        dkv_ref[0, :, :LANES] = dk.astype(BF16)
        dkv_ref[0, :, LANES:] = dv.astype(BF16)
        dc_ref[0, 0, 0] = dc[0:1]
        dc_ref[0, 1, 0] = dc[1:2]

    full = pl.BlockSpec((1, s, LANES), lambda i, h, k: (i, 0, h))
    kvtile = pl.BlockSpec((1, blk, 2 * LANES), lambda i, h, k: (i, k, KV_BLK0 + h))
    crow = pl.BlockSpec((1, 2, 1, 1, blk), lambda i, h, k: (i, h, k, 0, 0))
    return pl.pallas_call(
        body, name="fox_bwd_kv", grid=(b, PAIRS, nq),
        out_shape=(jax.ShapeDtypeStruct(qkv.shape, BF16), jax.ShapeDtypeStruct(c_row.shape, F32)),
        in_specs=[full, kvtile, full, crow, full, full, full, ANY],
        out_specs=(kvtile, crow),
        input_output_aliases={7: 0},
        compiler_params=_cparams(("parallel", "parallel", "arbitrary")),
    )(qkv, qkv, c_exp, c_row, lse, delta, do, dqkv)


def _sigmoid(z):
    return 1.0 / (1.0 + jnp.exp(-z))


def _mix_fwd(gu, b_gate, y_pool, y_fox):
    t = gu.shape[0]
    bt = _block(t, 256, 16)

    def body(gp_ref, gf_ref, bp_ref, bf_ref, yp_ref, yf_ref, o_ref):
        gp = _sigmoid(gp_ref[...] + bp_ref[...])
        gf = _sigmoid(gf_ref[...] + bf_ref[...])
        o_ref[...] = (gp * yp_ref[...] + gf * yf_ref[...]).astype(BF16)

    col = lambda j: pl.BlockSpec((bt, D_MODEL), lambda i: (i, j))
    vec = lambda j: pl.BlockSpec((1, D_MODEL), lambda i: (0, j))
    return pl.pallas_call(
        body, name="mix_fwd", out_shape=jax.ShapeDtypeStruct((t, D_MODEL), BF16), grid=(t // bt,),
        in_specs=[col(0), col(1), vec(0), vec(1), col(0), col(0)], out_specs=col(0),
        compiler_params=_cparams(("parallel",)),
    )(gu, gu, b_gate, b_gate, y_pool, y_fox)


def _mix_bwd(gu, b_gate, y_pool, y_fox, dmix):
    t = gu.shape[0]
    bt = _block(t, 256, 16)

    def body(gp_ref, gf_ref, bp_ref, bf_ref, yp_ref, yf_ref, dm_ref, dyp_ref, dyf_ref, dgl_ref, db_ref):
        @pl.when(pl.program_id(0) == 0)
        def _():
            db_ref[...] = jnp.zeros_like(db_ref)

        dm = dm_ref[...]
        gp = _sigmoid(gp_ref[...] + bp_ref[...])
        gf = _sigmoid(gf_ref[...] + bf_ref[...])
        dyp_ref[...] = (dm * gp).astype(BF16)
        dyf_ref[...] = (dm * gf).astype(BF16)
        dlp = dm * yp_ref[...] * gp * (1.0 - gp)
        dlf = dm * yf_ref[...] * gf * (1.0 - gf)
        dgl_ref[:, :D_MODEL] = dlp.astype(BF16)
        dgl_ref[:, D_MODEL:] = dlf.astype(BF16)
        db_ref[:, :D_MODEL] += jnp.sum(dlp, axis=0, keepdims=True)
        db_ref[:, D_MODEL:] += jnp.sum(dlf, axis=0, keepdims=True)

    col = lambda j: pl.BlockSpec((bt, D_MODEL), lambda i: (i, j))
    vec = lambda j: pl.BlockSpec((1, D_MODEL), lambda i: (0, j))
    wide = pl.BlockSpec((bt, GATE_WIDTH), lambda i: (i, 0))
    return pl.pallas_call(
        body, name="mix_bwd", grid=(t // bt,),
        out_shape=(jax.ShapeDtypeStruct((t, D_MODEL), BF16), jax.ShapeDtypeStruct((t, D_MODEL), BF16),
                   jax.ShapeDtypeStruct((t, GU_COLS), BF16), jax.ShapeDtypeStruct((1, GATE_WIDTH), F32)),
        in_specs=[col(0), col(1), vec(0), vec(1), col(0), col(0), col(0)],
        out_specs=(col(0), col(0), wide, pl.BlockSpec((1, GATE_WIDTH), lambda i: (0, 0))),
        compiler_params=_cparams(("arbitrary",)),
    )(gu, gu, b_gate, b_gate, y_pool, y_fox, dmix)


X_SCALE = X_DH ** -0.5


def _xattn_probs(qh, kh):
    s = lax.dot_general(qh, kh, NT_DIMS, preferred_element_type=F32) * X_SCALE
    e = jnp.exp(s - jnp.max(s, axis=-1, keepdims=True))
    return e / jnp.sum(e, axis=-1, keepdims=True)


def _xattn_fwd(q, kv):
    b, s, _ = q.shape
    m = kv.shape[1]
    bq = _block(s, 512, 16)

    def body(q_ref, kv_ref, o_ref):
        for h in range(X_HEADS):
            cols = slice(h * X_DH, (h + 1) * X_DH)
            p = _xattn_probs(q_ref[0, :, cols], kv_ref[0, :, cols])
            vh = kv_ref[0, :, X_WIDTH + h * X_DH:X_WIDTH + (h + 1) * X_DH]
            o_ref[0, :, cols] = jnp.dot(p.astype(BF16), vh, preferred_element_type=F32).astype(BF16)

    return pl.pallas_call(
        body, name="xattn_fwd", out_shape=jax.ShapeDtypeStruct((b, s, X_WIDTH), BF16), grid=(b, s // bq),
        in_specs=[pl.BlockSpec((1, bq, X_WIDTH), lambda i, j: (i, j, 0)),
                  pl.BlockSpec((1, m, 2 * X_WIDTH), lambda i, j: (i, 0, 0))],
        out_specs=pl.BlockSpec((1, bq, X_WIDTH), lambda i, j: (i, j, 0)),
        compiler_params=_cparams(("parallel", "parallel")),
    )(q, kv)


def _xattn_bwd(q, kv, do):
    b, s, _ = q.shape
    m = kv.shape[1]
    bq = _block(s, 512, 16)

    def body(q_ref, kv_ref, do_ref, dq_ref, dkv_ref):
        @pl.when(pl.program_id(1) == 0)
        def _():
            dkv_ref[...] = jnp.zeros_like(dkv_ref)

        for h in range(X_HEADS):
            cols = slice(h * X_DH, (h + 1) * X_DH)
            vcols = slice(X_WIDTH + h * X_DH, X_WIDTH + (h + 1) * X_DH)
            qh, kh, vh, doh = q_ref[0, :, cols], kv_ref[0, :, cols], kv_ref[0, :, vcols], do_ref[0, :, cols]
            p = _xattn_probs(qh, kh)
            dkv_ref[0, :, vcols] += lax.dot_general(p.astype(BF16), doh, TN_DIMS, preferred_element_type=F32)
            dp = lax.dot_general(doh, vh, NT_DIMS, preferred_element_type=F32)
            ds = (p * (dp - jnp.sum(p * dp, axis=-1, keepdims=True)) * X_SCALE).astype(BF16)
            dq_ref[0, :, cols] = jnp.dot(ds, kh, preferred_element_type=F32).astype(BF16)
            dkv_ref[0, :, cols] += lax.dot_general(ds, qh, TN_DIMS, preferred_element_type=F32)

    tile = pl.BlockSpec((1, bq, X_WIDTH), lambda i, j: (i, j, 0))
    mem = pl.BlockSpec((1, m, 2 * X_WIDTH), lambda i, j: (i, 0, 0))
    return pl.pallas_call(
        body, name="xattn_bwd", grid=(b, s // bq),
        out_shape=(jax.ShapeDtypeStruct((b, s, X_WIDTH), BF16), jax.ShapeDtypeStruct((b, m, 2 * X_WIDTH), F32)),
        in_specs=[tile, mem, tile], out_specs=(tile, mem),
        compiler_params=_cparams(("parallel", "arbitrary")),
    )(q, kv, do)


def _swiglu_fwd(gu):
    t = gu.shape[0]
    bt = _block(t, 256, 16)

    def body(gt_ref, up_ref, o_ref):
        gt = gt_ref[...]
        o_ref[...] = (gt * _sigmoid(gt) * up_ref[...]).astype(BF16)

    col = lambda j: pl.BlockSpec((bt, D_FF), lambda i: (i, j))
    return pl.pallas_call(
        body, name="swiglu_fwd", out_shape=jax.ShapeDtypeStruct((t, D_FF), BF16), grid=(t // bt,),
        in_specs=[col(0), col(1)], out_specs=col(0),
        compiler_params=_cparams(("parallel",)),
    )(gu, gu)


def _swiglu_bwd(gu, dact):
    t = gu.shape[0]
    bt = _block(t, 256, 16)

    def body(gt_ref, up_ref, da_ref, o_ref):
        gt = gt_ref[...]
        da = da_ref[...]
        sg = _sigmoid(gt)
        silu = gt * sg
        o_ref[:, :D_FF] = (da * up_ref[...] * (sg + silu * (1.0 - sg))).astype(BF16)
        o_ref[:, D_FF:] = (da * silu).astype(BF16)

    col = lambda j: pl.BlockSpec((bt, D_FF), lambda i: (i, j))
    return pl.pallas_call(
        body, name="swiglu_bwd", out_shape=jax.ShapeDtypeStruct((t, 2 * D_FF), BF16), grid=(t // bt,),
        in_specs=[col(0), col(1), col(0)], out_specs=pl.BlockSpec((bt, 2 * D_FF), lambda i: (i, 0)),
        compiler_params=_cparams(("parallel",)),
    )(gu, gu, dact)


def _stack_of(w, axis):
    r, c = w.shape
    if axis == 0:
        return w.reshape(N_CHIPS, r // N_CHIPS, c)
    return w.reshape(r, N_CHIPS, c // N_CHIPS).transpose(1, 0, 2)


def _unstack_cols(w3):
    n, r, c = w3.shape
    return w3.transpose(1, 0, 2).reshape(r, n * c)


def _stack_t(w3):
    n, r, c = w3.shape
    return w3.transpose(0, 2, 1).reshape(n * c, r)


def _pair_kv(k, v):
    r = k.shape[0]
    return jnp.stack([k.reshape(r, PAIRS, LANES), v.reshape(r, PAIRS, LANES)], axis=2).reshape(r, 2 * FOX_WIDTH)


def _unpair_kv(kv):
    r = kv.shape[0]
    kv = kv.reshape(r, PAIRS, 2, LANES)
    return kv[:, :, 0, :].reshape(r, FOX_WIDTH), kv[:, :, 1, :].reshape(r, FOX_WIDTH)


def _step(x, mem, loss_target, weights, moments_m, moments_v):
    nb, s, d = x.shape
    n_mem = mem.shape[1]
    t = nb * s
    blk = min(ATT_BLOCK, s)
    x2 = x.reshape(t, d)
    mem2 = mem.reshape(nb * n_mem, d)
    tgt2 = loss_target.reshape(t, d)
    local = {n: weights[n].reshape(weights[n].shape[1:]) for n, _, _ in SHARDED}

    stacks = dict(zip([n for n, _, _ in SHARDED], _gather_weights([local[n].astype(BF16) for n, _, _ in SHARDED])))
    w_in = _unstack_cols(stacks["w_in"])
    w_gu = jnp.concatenate([w_in[:, 2056:], w_in[:, :512]], axis=1)
    w_qkv = jnp.concatenate([w_in[:, 512:1024], _pair_kv(w_in[:, 1024:1536], w_in[:, 1536:2048])], axis=1)
    w_f = jnp.pad(w_in[:, 2048:2056], ((0, 0), (0, LANES - FOX_HEADS)))
    w_pool_out3, w_fox_out3, w_xo3, w_ffn_in3 = (stacks[n] for n in ("w_pool_out", "w_fox_out", "w_xo", "w_ffn_in"))
    w_out, w_xq, w_xkv, w_ffn_out = (stacks[n].reshape(-1, stacks[n].shape[2])
                                     for n in ("w_out", "w_xq", "w_xkv", "w_ffn_out"))

    g_mix, g_x, g_mem, g_ffn = (weights[n] for n in ("norm_mix_g", "norm_x_g", "norm_mem_g", "norm_ffn_g"))
    g_final = weights["norm_final_g"].reshape(1, d)
    pool_w = weights["pool_w"].reshape(4, POOL_GC, POOL_GC)
    pool_scale, b_gate = weights["pool_scale"], weights["b_gate"]
    b_f_pad = jnp.pad(weights["b_forget"], ((0, 0), (0, LANES - FOX_HEADS)))
    b_f_exp = jnp.repeat(weights["b_forget"], FOX_DH, axis=1)

    h = _rms_fwd(x2, g_mix, "norm_mix")
    gu = _mm(h, w_gu, bn=512, name="in_proj_gates_pool")
    qkv = _mm(h, w_qkv, out_dtype=BF16, bn=512, name="in_proj_qkv")
    f_pad = _mm(h, w_f, name="in_proj_forget")
    gu3, qkv3 = gu.reshape(nb, s, GU_COLS), qkv.reshape(nb, s, 3 * FOX_WIDTH)
    y = _pool_fwd(gu3, pool_w, pool_scale)
    f_exp = jnp.repeat(f_pad[:, :FOX_HEADS], FOX_DH, axis=1).reshape(nb, s, FOX_WIDTH)
    c_exp = _forget_cumsum(f_exp, b_f_exp)
    c_row = c_exp[:, :, ::FOX_DH].transpose(0, 2, 1).reshape(nb, FOX_HEADS, s // blk, 1, blk)
    o, o_b, lse = _fox_fwd(qkv3, c_exp, c_row)
    y2, o2 = y.reshape(t, POOL_WIDTH), o_b.reshape(t, FOX_WIDTH)
    y_pool = _mm(y2, w_pool_out3, b_stack=True, name="pool_out")
    y_fox = _mm(o2, w_fox_out3, b_stack=True, name="fox_out")
    mix = _mix_fwd(gu, b_gate, y_pool, y_fox)
    x1 = _mm(mix, w_out, res=x2, name="mix_out")
    hx = _rms_fwd(x1, g_x, "norm_x")
    mem_n = _rms_fwd(mem2, g_mem, "norm_mem")
    qx = _mm(hx, w_xq, out_dtype=BF16, name="x_q")
    kv = _mm(mem_n, w_xkv, out_dtype=BF16, name="x_kv")
    qx3, kv3 = qx.reshape(nb, s, X_WIDTH), kv.reshape(nb, n_mem, 2 * X_WIDTH)
    ox = _xattn_fwd(qx3, kv3).reshape(t, X_WIDTH)
    x2_ = _mm(ox, w_xo3, b_stack=True, res=x1, name="x_out")
    hf = _rms_fwd(x2_, g_ffn, "norm_ffn")
    ffn = _mm(hf, w_ffn_in3, b_stack=True, bm=512, bn=1408, name="ffn_in")
    act = _swiglu_fwd(ffn)
    x3 = _mm(act, w_ffn_out, res=x2_, name="ffn_out")

    dx3, dx3_b, dg_final, loss_part = _final_loss(x3, tgt2, g_final)
    dw_ffn_out = _mm(act, dx3_b, ta=True, bm=1408, bn=512, bk=2048, name="d_w_ffn_out")
    dact = _mm(dx3_b, w_ffn_out.T, bn=1408, name="d_act")
    dffn = _swiglu_bwd(ffn, dact)
    dw_ffn_in = _mm(hf, dffn, ta=True, bm=512, bn=1408, bk=2048, out_stack=True, name="d_w_ffn_in")
    dhf = _mm(dffn, _stack_t(w_ffn_in3), bk=2816, name="d_hf")
    dx2, dx2_b, dg_ffn = _rms_bwd(dhf, x2_, g_ffn, dx3, "norm_ffn_bwd")

    dw_xo = _mm(ox, dx2_b, ta=True, bn=256, out_stack=True, name="d_w_xo")
    dox = _mm(dx2_b, _stack_t(w_xo3), out_dtype=BF16, name="d_ox").reshape(nb, s, X_WIDTH)
    dqx, dkv = _xattn_bwd(qx3, kv3, dox)
    dqx2, dkv2 = dqx.reshape(t, X_WIDTH), dkv.reshape(nb * n_mem, 2 * X_WIDTH)
    dw_xkv = _mm(mem_n, dkv2, ta=True, name="d_w_xkv")
    dmem_n = _mm(dkv2, w_xkv.T, name="d_mem_n")
    dg_mem = _rms_bwd(dmem_n, mem2, g_mem, None, "norm_mem_bwd")
    dw_xq = _mm(hx, dqx2, ta=True, name="d_w_xq")
    dhx = _mm(dqx2, w_xq.T, name="d_hx")
    dx1, dx1_b, dg_x = _rms_bwd(dhx, x1, g_x, dx2, "norm_x_bwd")

    dw_out = _mm(mix, dx1_b, ta=True, name="d_w_out")
    dmix = _mm(dx1_b, w_out.T, name="d_mix")
    dyp, dyf, dgu, db_gate = _mix_bwd(gu, b_gate, y_pool, y_fox, dmix)
    dw_pool_out = _mm(y2, dyp, ta=True, bn=256, out_stack=True, name="d_w_pool_out")
    dw_fox_out = _mm(o2, dyf, ta=True, bn=256, out_stack=True, name="d_w_fox_out")
    dy = _mm(dyp, _stack_t(w_pool_out3), name="d_y").reshape(nb, s, POOL_WIDTH)
    do = _mm(dyf, _stack_t(w_fox_out3), name="d_o").reshape(nb, s, FOX_WIDTH)
    dgu3, dpool_w, dpool_scale = _pool_bwd(gu3, dy, pool_w, pool_scale, dgu.reshape(nb, s, GU_COLS))
    dqkv3, delta, dc_q = _fox_bwd_q(qkv3, c_exp, c_row, lse, o, do)
    dqkv3, dc_row = _fox_bwd_kv(qkv3, c_exp, c_row, lse, delta, do, dqkv3)
    dc = dc_row.reshape(nb, FOX_HEADS, s).transpose(0, 2, 1) + dc_q[:, :, ::FOX_DH]
    dc = jnp.pad(dc, ((0, 0), (0, 0), (0, LANES - FOX_HEADS)))
    df, db_f = _forget_bwd(dc, f_pad.reshape(nb, s, LANES), b_f_pad)
    dgu2, dqkv2, df2 = dgu3.reshape(t, GU_COLS), dqkv3.reshape(t, 3 * FOX_WIDTH), df.reshape(t, LANES)
    dw_gu = _mm(h, dgu2, ta=True, bn=512, name="d_w_gates_pool")
    dw_qkv = _mm(h, dqkv2, ta=True, bn=512, name="d_w_qkv")
    dw_f = _mm(h, df2, ta=True, name="d_w_forget")
    dh = _mm(df2, w_f.T, name="d_h_forget")
    dh = _mm(dqkv2, w_qkv.T, res=dh, name="d_h_qkv")
    dh = _mm(dgu2, w_gu.T, res=dh, name="d_h_gates_pool")
    dx, _, dg_mix = _rms_bwd(dh, x2, g_mix, dx1, "norm_mix_bwd")
    dw_k, dw_v = _unpair_kv(dw_qkv[:, FOX_WIDTH:])
    dw_in = jnp.concatenate([dw_gu[:, GATE_WIDTH:], dw_qkv[:, :FOX_WIDTH], dw_k, dw_v, dw_f[:, :FOX_HEADS],
                             dw_gu[:, :GATE_WIDTH]], axis=1)

    grad_stacks = {"w_in": _stack_of(dw_in, 1), "w_pool_out": dw_pool_out, "w_fox_out": dw_fox_out,
                   "w_out": _stack_of(dw_out, 0), "w_xq": _stack_of(dw_xq, 0), "w_xkv": _stack_of(dw_xkv, 0),
                   "w_xo": dw_xo, "w_ffn_in": dw_ffn_in, "w_ffn_out": _stack_of(dw_ffn_out, 0)}
    names = [n for n, _, _ in SHARDED]
    partial = [grad_stacks[n] for n in names]
    core = lax.axis_index("c").astype(jnp.int32).reshape(1)
    theirs = _swap_halves(partial)
    chip_sums = [_sum_halves(g_, t_, core, "sum_halves_" + n) for n, g_, t_ in zip(names, partial, theirs)]
    slots = _exchange_chips(chip_sums)
    reduced = _join_halves([_sum_chips(s_, "sum_chips_" + n) for n, s_ in zip(names, slots)])

    small_grads = {"norm_mix_g": dg_mix, "b_forget": db_f[:, :FOX_HEADS], "b_gate": db_gate, "pool_w": dpool_w,
                   "pool_scale": dpool_scale, "norm_x_g": dg_x, "norm_mem_g": dg_mem, "norm_ffn_g": dg_ffn,
                   "norm_final_g": dg_final}
    parts = _gather_small(_pack_small([small_grads[n] for n, _ in SMALL], last=loss_part[0, 0]))
    sg, sd, sm, sv = _adamw_small(parts.reshape(N_DEV, SMALL_ROWS, LANES),
                                  _pack_small([weights[n] for n, _ in SMALL]),
                                  _pack_small([moments_m[n] for n, _ in SMALL]),
                                  _pack_small([moments_v[n] for n, _ in SMALL]))
    loss = sg.reshape(-1)[LOSS_POS]

    grads, deltas, new_m, new_v = {}, {}, {}, {}
    for (n, _), g_, d_, m_, v_ in zip(SMALL, _unpack_small(sg), _unpack_small(sd), _unpack_small(sm), _unpack_small(sv)):
        grads[n], deltas[n], new_m[n], new_v[n] = g_, d_, m_, v_
    for n, g_ in zip(names, reduced):
        shp = weights[n].shape
        d_, m_, v_ = _adamw(local[n], g_, moments_m[n].reshape(g_.shape), moments_v[n].reshape(g_.shape), "adamw_" + n)
        grads[n], deltas[n], new_m[n], new_v[n] = (a.reshape(shp) for a in (g_, d_, m_, v_))
    return loss, dx.reshape(nb, s, d), grads, deltas, new_m, new_v


def kernel(x, mem, norm_mix_g, w_in, b_forget, b_gate, pool_w, pool_scale, w_pool_out, w_fox_out, w_out, norm_x_g, norm_mem_g, w_xq, w_xkv, w_xo, norm_ffn_g, w_ffn_in, w_ffn_out, norm_final_g, loss_target, m_norm_mix_g, m_w_in, m_b_forget, m_b_gate, m_pool_w, m_pool_scale, m_w_pool_out, m_w_fox_out, m_w_out, m_norm_x_g, m_norm_mem_g, m_w_xq, m_w_xkv, m_w_xo, m_norm_ffn_g, m_w_ffn_in, m_w_ffn_out, m_norm_final_g, v_norm_mix_g, v_w_in, v_b_forget, v_b_gate, v_pool_w, v_pool_scale, v_w_pool_out, v_w_fox_out, v_w_out, v_norm_x_g, v_norm_mem_g, v_w_xq, v_w_xkv, v_w_xo, v_norm_ffn_g, v_w_ffn_in, v_w_ffn_out, v_norm_final_g):
    given = dict(locals())
    weights = {n: given[n] for n in WEIGHT_ORDER}
    moments_m = {n: given["m_" + n] for n in WEIGHT_ORDER}
    moments_v = {n: given["v_" + n] for n in WEIGHT_ORDER}
    loss, grad_x, grads, deltas, new_m, new_v = _step(x, mem, loss_target, weights, moments_m, moments_v)
    return (loss, grad_x, *[grads[n] for n in WEIGHT_ORDER], *[deltas[n] for n in WEIGHT_ORDER],
            *[new_m[n] for n in WEIGHT_ORDER], *[new_v[n] for n in WEIGHT_ORDER])
```

```python
import functools
import math

import jax
import jax.numpy as jnp
from jax import lax
from jax.experimental import pallas as pl
from jax.experimental.pallas import tpu as pltpu

F32 = jnp.float32
BF16 = jnp.bfloat16
MESH = pl.DeviceIdType.MESH

D_MODEL = 1024
EPS = 1e-6
POOL_WINDOWS = (2, 4, 8, 16)
POOL_WIDTH = 512
POOL_GC = 128
FOX_HEADS = 8
FOX_DH = 64
FOX_WIDTH = 512
X_HEADS = 4
X_DH = 128
X_WIDTH = 512
D_FF = 2816
IN_COLS = 4104
GATE_WIDTH = 2048
ADAM_LR = 0.001
ADAM_B1 = 0.9
ADAM_B2 = 0.999
ADAM_EPS = 1e-08
ADAM_WD = 0.01
ADAM_STEP = 10

N_CHIPS = 4
N_DEV = 8
LANES = 128
VMEM_LIMIT_BYTES = 56 * 1024 * 1024
NEG_INF = -1e30
ATT_BLOCK = 512

SHARDED = (
    ("w_in", (1024, IN_COLS), 1),
    ("w_pool_out", (POOL_WIDTH, 1024), 1),
    ("w_fox_out", (FOX_WIDTH, 1024), 1),
    ("w_out", (1024, 1024), 0),
    ("w_xq", (1024, X_WIDTH), 0),
    ("w_xkv", (1024, 2 * X_WIDTH), 0),
    ("w_xo", (X_WIDTH, 1024), 1),
    ("w_ffn_in", (1024, 2 * D_FF), 1),
    ("w_ffn_out", (D_FF, 1024), 0),
)
SMALL = (
    ("norm_mix_g", (1, 1024)),
    ("b_forget", (1, 8)),
    ("b_gate", (1, 2048)),
    ("pool_w", (1, 4, 128, 128)),
    ("pool_scale", (1, 512)),
    ("norm_x_g", (1, 1024)),
    ("norm_mem_g", (1, 1024)),
    ("norm_ffn_g", (1, 1024)),
    ("norm_final_g", (1024,)),
)
WEIGHT_ORDER = ("norm_mix_g", "w_in", "b_forget", "b_gate", "pool_w", "pool_scale", "w_pool_out", "w_fox_out", "w_out",
                "norm_x_g", "norm_mem_g", "w_xq", "w_xkv", "w_xo", "norm_ffn_g", "w_ffn_in", "w_ffn_out", "norm_final_g")


def _round_up(n, m):
    return (n + m - 1) // m * m


SMALL_ELEMS = sum(math.prod(s) for _, s in SMALL)
SMALL_ROWS = _round_up(SMALL_ELEMS // LANES + 1, 8)
LOSS_POS = SMALL_ROWS * LANES - 1


def _cparams(sem=None):
    return pltpu.CompilerParams(dimension_semantics=sem, vmem_limit_bytes=VMEM_LIMIT_BYTES)


def _block(dim, pref, unit):
    if dim <= pref:
        return dim
    best = None
    for b in range(unit, pref + 1, unit):
        if dim % b == 0:
            best = b
    assert best is not None, (dim, pref, unit)
    return best


def _rows_block(rows, cols, unit=16, elems=1 << 19):
    return _block(rows, max(unit, elems // cols // unit * unit), unit)


def _pack_small(parts, last=None):
    flat = jnp.concatenate([p.reshape(-1).astype(F32) for p in parts])
    flat = jnp.pad(flat, (0, SMALL_ROWS * LANES - flat.shape[0]))
    if last is not None:
        flat = flat.at[LOSS_POS].set(last)
    return flat.reshape(SMALL_ROWS, LANES)


def _unpack_small(packed):
    flat = packed.reshape(-1)
    out, off = [], 0
    for _, shape in SMALL:
        n = math.prod(shape)
        out.append(flat[off:off + n].reshape(shape))
        off += n
    return out


def _my_place():
    return lax.axis_index("x"), lax.axis_index("y"), lax.axis_index("c")


def _other_chips(x, y):
    return [(1 - x, y), (x, 1 - y), (1 - x, 1 - y)]


def _chip(place):
    return 2 * place[0] + place[1]


ANY = pl.BlockSpec(memory_space=pl.ANY)


def _gather_weights(shards):
    n = len(shards)

    def body(*refs):
        ins, outs = refs[:n], refs[n:2 * n]
        send_sems, recv_sems = refs[2 * n:]
        x, y, c = _my_place()
        sibling = (x, y, 1 - c)
        chips = _other_chips(x, y)

        def half(k, chip, core):
            h = ins[k].shape[0] // 2
            return outs[k].at[_chip(chip), pl.ds(core * h, h), :]

        def copy(k, slot, chip, core, to, src=None):
            return pltpu.make_async_remote_copy(
                src_ref=half(k, chip, core) if src is None else src, dst_ref=half(k, chip, core),
                send_sem=send_sems.at[6 * k + slot], recv_sem=recv_sems.at[6 * k + slot],
                device_id=to, device_id_type=MESH)

        first = []
        for j, chip in enumerate(chips):
            for k in range(n):
                h = ins[k].shape[0] // 2
                first.append(copy(k, j, (x, y), c, (*chip, c), src=ins[k].at[pl.ds(c * h, h), :]))
                first[-1].start()
        passed = []
        for j, chip in enumerate(chips):
            for k in range(n):
                copy(k, j, chip, c, (x, y, c)).wait_recv()
                passed.append(copy(k, 3 + j, chip, c, sibling))
                passed[-1].start()
        for j, chip in enumerate(chips):
            for k in range(n):
                copy(k, 3 + j, chip, 1 - c, (x, y, c)).wait_recv()
        for cp in first + passed:
            cp.wait_send()

    return pl.pallas_call(
        body, name="gather_weights",
        out_shape=[jax.ShapeDtypeStruct((N_CHIPS,) + s.shape, s.dtype) for s in shards],
        in_specs=[ANY] * n, out_specs=[ANY] * n,
        scratch_shapes=[pltpu.SemaphoreType.DMA((6 * n,)), pltpu.SemaphoreType.DMA((6 * n,))],
    )(*shards)


def _swap_halves(grads):
    n = len(grads)

    def body(*refs):
        ins, outs = refs[:n], refs[n:2 * n]
        send_sems, recv_sems = refs[2 * n:]
        x, y, c = _my_place()
        copies = []
        for k in range(n):
            h = ins[k].shape[1] // 2
            copies.append(pltpu.make_async_remote_copy(
                src_ref=ins[k].at[:, pl.ds((1 - c) * h, h), :], dst_ref=outs[k],
                send_sem=send_sems.at[k], recv_sem=recv_sems.at[k], device_id=(x, y, 1 - c), device_id_type=MESH))
            copies[-1].start()
        for cp in copies:
            cp.wait()

    return pl.pallas_call(
        body, name="swap_halves",
        out_shape=[jax.ShapeDtypeStruct((N_CHIPS, g.shape[1] // 2, g.shape[2]), g.dtype) for g in grads],
        in_specs=[ANY] * n, out_specs=[ANY] * n,
        scratch_shapes=[pltpu.SemaphoreType.DMA((n,)), pltpu.SemaphoreType.DMA((n,))],
    )(*grads)


def _exchange_chips(sums):
    n = len(sums)

    def body(*refs):
        ins, outs = refs[:n], refs[n:2 * n]
        send_sems, recv_sems = refs[2 * n:]
        x, y, c = _my_place()
        me = _chip((x, y))
        chips = _other_chips(x, y)
        sends = []
        for j, chip in enumerate(chips):
            for k in range(n):
                sends.append(pltpu.make_async_remote_copy(
                    src_ref=ins[k].at[_chip(chip)], dst_ref=outs[k].at[me],
                    send_sem=send_sems.at[3 * k + j], recv_sem=recv_sems.at[3 * k + j],
                    device_id=(*chip, c), device_id_type=MESH))
                sends[-1].start()
        for j, chip in enumerate(chips):
            for k in range(n):
                slot = outs[k].at[_chip(chip)]
                pltpu.make_async_remote_copy(
                    src_ref=slot, dst_ref=slot, send_sem=send_sems.at[3 * k + j], recv_sem=recv_sems.at[3 * k + j],
                    device_id=(x, y, c), device_id_type=MESH).wait_recv()
        for cp in sends:
            cp.wait_send()

    return pl.pallas_call(
        body, name="exchange_chips",
        out_shape=[jax.ShapeDtypeStruct(s.shape, s.dtype) for s in sums],
        in_specs=[ANY] * n, out_specs=[ANY] * n,
        scratch_shapes=[pltpu.SemaphoreType.DMA((3 * n,)), pltpu.SemaphoreType.DMA((3 * n,))],
    )(*sums)


def _join_halves(shards):
    n = len(shards)

    def body(*refs):
        ins, outs = refs[:n], refs[n:2 * n]
        send_sems, recv_sems = refs[2 * n:]
        x, y, c = _my_place()
        sends = []
        for k in range(n):
            h = ins[k].shape[0] // 2
            sends.append(pltpu.make_async_remote_copy(
                src_ref=ins[k].at[pl.ds(c * h, h), :], dst_ref=outs[k].at[pl.ds(c * h, h), :],
                send_sem=send_sems.at[k], recv_sem=recv_sems.at[k], device_id=(x, y, 1 - c), device_id_type=MESH))
            sends[-1].start()
        for k in range(n):
            h = ins[k].shape[0] // 2
            theirs = outs[k].at[pl.ds((1 - c) * h, h), :]
            pltpu.make_async_remote_copy(
                src_ref=theirs, dst_ref=theirs, send_sem=send_sems.at[k], recv_sem=recv_sems.at[k],
                device_id=(x, y, c), device_id_type=MESH).wait_recv()
        for cp in sends:
            cp.wait_send()

    return pl.pallas_call(
        body, name="join_halves",
        out_shape=[jax.ShapeDtypeStruct(s.shape, s.dtype) for s in shards],
        in_specs=[ANY] * n, out_specs=[ANY] * n,
        input_output_aliases={k: k for k in range(n)},
        scratch_shapes=[pltpu.SemaphoreType.DMA((n,)), pltpu.SemaphoreType.DMA((n,))],
    )(*shards)


def _gather_small(block):
    m_per = block.shape[0]

    def body(x_ref, out_ref, send_sems, recv_sems, local_sem):
        x, y, c = _my_place()
        me, sibling = (x, y, c), (x, y, 1 - c)
        chips = _other_chips(x, y)

        def rows(px, py, pc):
            return out_ref.at[pl.ds((4 * px + 2 * py + pc) * m_per, m_per), :]

        def copy(k, blk, to, src=None):
            return pltpu.make_async_remote_copy(
                src_ref=rows(*blk) if src is None else src, dst_ref=rows(*blk),
                send_sem=send_sems.at[k], recv_sem=recv_sems.at[k], device_id=to, device_id_type=MESH)

        mine = pltpu.make_async_copy(x_ref, rows(*me), local_sem)
        mine.start()
        first = [copy(0, me, sibling, src=x_ref)]
        first += [copy(1 + j, me, (*chip, c), src=x_ref) for j, chip in enumerate(chips)]
        for cp in first:
            cp.start()
        passed = [copy(4 + j, (*chip, c), sibling) for j, chip in enumerate(chips)]
        for j, chip in enumerate(chips):
            copy(1 + j, (*chip, c), me).wait_recv()
            passed[j].start()
        copy(0, sibling, me).wait_recv()
        for j, chip in enumerate(chips):
            copy(4 + j, (*chip, 1 - c), me).wait_recv()
        for cp in first + passed:
            cp.wait_send()
        mine.wait()

    return pl.pallas_call(
        body, name="gather_small",
        out_shape=jax.ShapeDtypeStruct((N_DEV * m_per, LANES), block.dtype),
        in_specs=[pl.BlockSpec(memory_space=pltpu.VMEM)],
        out_specs=pl.BlockSpec(memory_space=pltpu.VMEM),
        scratch_shapes=[pltpu.SemaphoreType.DMA((7,)), pltpu.SemaphoreType.DMA((7,)), pltpu.SemaphoreType.DMA],
    )(block)


def _sum_halves(grads, theirs, core, name):
    _, h, cols = theirs.shape
    br = _rows_block(h, cols)
    nb = h // br

    def body(core_ref, a_ref, b_ref, o_ref):
        o_ref[...] = (a_ref[...] + b_ref[...]).astype(BF16)

    return pl.pallas_call(
        body, name=name,
        out_shape=jax.ShapeDtypeStruct(theirs.shape, BF16),
        grid_spec=pltpu.PrefetchScalarGridSpec(
            num_scalar_prefetch=1, grid=(N_CHIPS, nb),
            in_specs=[pl.BlockSpec((1, br, cols), lambda j, i, core_ref: (j, core_ref[0] * nb + i, 0)),
                      pl.BlockSpec((1, br, cols), lambda j, i, core_ref: (j, i, 0))],
            out_specs=pl.BlockSpec((1, br, cols), lambda j, i, core_ref: (j, i, 0))),
        compiler_params=_cparams(("parallel", "parallel")),
    )(core, grads, theirs)


def _sum_chips(slots, sums, place, name):
    _, h, cols = slots.shape
    br = _rows_block(h, cols)
    nb = h // br

    def body(place_ref, s_ref, own_ref, o_ref):
        me = place_ref[1]
        acc = None
        for k in range(N_CHIPS):
            term = jnp.where(me == k, own_ref[k], s_ref[k]).astype(F32)
            acc = term if acc is None else acc + term
        o_ref[...] = acc

    stack = pl.BlockSpec((N_CHIPS, br, cols), lambda i, place_ref: (0, i, 0))
    return pl.pallas_call(
        body, name=name,
        out_shape=jax.ShapeDtypeStruct((2 * h, cols), F32),
        grid_spec=pltpu.PrefetchScalarGridSpec(
            num_scalar_prefetch=1, grid=(nb,), in_specs=[stack, stack],
            out_specs=pl.BlockSpec((br, cols), lambda i, place_ref: (place_ref[0] * nb + i, 0))),
        compiler_params=_cparams(("parallel",)),
    )(place, slots, sums)


def _adamw_math(w, g, m, v):
    m = ADAM_B1 * m + (1.0 - ADAM_B1) * g
    v = ADAM_B2 * v + (1.0 - ADAM_B2) * (g * g)
    m_hat = m / (1.0 - ADAM_B1 ** ADAM_STEP)
    v_hat = v / (1.0 - ADAM_B2 ** ADAM_STEP)
    delta = -ADAM_LR * (m_hat / (jnp.sqrt(v_hat) + ADAM_EPS) + ADAM_WD * w)
    return delta, m, v


def _adamw(w, g, m, v, name):
    rows, cols = w.shape
    br = _rows_block(rows, cols, unit=8)

    def body(w_ref, g_ref, m_ref, v_ref, d_ref, nm_ref, nv_ref):
        d, nm, nv = _adamw_math(w_ref[...], g_ref[...], m_ref[...], v_ref[...])
        d_ref[...] = d
        nm_ref[...] = nm
        nv_ref[...] = nv

    spec = pl.BlockSpec((br, cols), lambda i: (i, 0))
    shape = jax.ShapeDtypeStruct(w.shape, F32)
    return pl.pallas_call(
        body, name=name, out_shape=(shape, shape, shape), grid=(rows // br,),
        in_specs=[spec] * 4, out_specs=(spec, spec, spec),
        compiler_params=_cparams(("parallel",)),
    )(w, g, m, v)


def _adamw_small(parts, w, m, v):
    def body(p_ref, w_ref, m_ref, v_ref, g_ref, d_ref, nm_ref, nv_ref):
        g = p_ref[0]
        for k in range(1, N_DEV):
            g = g + p_ref[k]
        d, nm, nv = _adamw_math(w_ref[...], g, m_ref[...], v_ref[...])
        g_ref[...] = g
        d_ref[...] = d
        nm_ref[...] = nm
        nv_ref[...] = nv

    shape = jax.ShapeDtypeStruct((SMALL_ROWS, LANES), F32)
    return pl.pallas_call(body, name="adamw_small", out_shape=(shape,) * 4, compiler_params=_cparams())(parts, w, m, v)


def _mm(a, b, *, name, ta=False, out_dtype=F32, res=None, bm=1024, bn=1024, bk=4096, b_stack=False, out_stack=False):
    if ta:
        kdim, m = a.shape
    else:
        m, kdim = a.shape
    if b_stack:
        _, kb, chunk = b.shape
        n = N_CHIPS * chunk
    else:
        kb, n = b.shape
        chunk = n // N_CHIPS if out_stack else n
    assert kdim == kb, (a.shape, b.shape, ta)
    bm = _block(m, bm, LANES if ta else 16)
    bn = _block(chunk, bn, LANES)
    bk = _block(kdim, bk, LANES)
    nk = kdim // bk
    per_chunk = chunk // bn
    dims = (((0 if ta else 1,), (0,)), ((), ()))

    def body(*refs):
        refs = list(refs)
        a_ref, b_ref = refs[:2]
        r_ref = refs[2] if res is not None else None
        o_ref = refs[3] if res is not None else refs[2]
        part = lax.dot_general(a_ref[...].astype(BF16), b_ref[...].astype(BF16), dims, preferred_element_type=F32)

        def finish(r):
            if r_ref is not None:
                r = r + r_ref[...]
            o_ref[...] = r.astype(out_dtype)

        if nk == 1:
            finish(part)
        else:
            acc_ref = refs[-1]
            k = pl.program_id(2)

            @pl.when(k == 0)
            def _():
                acc_ref[...] = part

            @pl.when(k > 0)
            def _():
                acc_ref[...] += part

            @pl.when(k == nk - 1)
            def _():
                finish(acc_ref[...])

    a_spec = pl.BlockSpec((bk, bm), lambda i, j, k: (k, i)) if ta else pl.BlockSpec((bm, bk), lambda i, j, k: (i, k))
    if b_stack:
        b_spec = pl.BlockSpec((None, bk, bn), lambda i, j, k: (j // per_chunk, k, j % per_chunk))
    else:
        b_spec = pl.BlockSpec((bk, bn), lambda i, j, k: (k, j))
    r_spec = pl.BlockSpec((bm, bn), lambda i, j, k: (i, j))
    if out_stack:
        o_spec = pl.BlockSpec((None, bm, bn), lambda i, j, k: (j // per_chunk, i, j % per_chunk))
        o_shape = (N_CHIPS, m, chunk)
    else:
        o_spec, o_shape = r_spec, (m, n)
    in_specs = [a_spec, b_spec] + ([r_spec] if res is not None else [])
    args = (a, b) + ((res,) if res is not None else ())
    return pl.pallas_call(
        body, name=name, out_shape=jax.ShapeDtypeStruct(o_shape, out_dtype),
        grid=(m // bm, n // bn, nk), in_specs=in_specs, out_specs=o_spec,
        scratch_shapes=[pltpu.VMEM((bm, bn), F32)] if nk > 1 else [],
        compiler_params=_cparams(("parallel", "parallel", "arbitrary")),
    )(*args)


def _rms_fwd(x, g, name):
    t, d = x.shape
    bt = _block(t, 512, 16)

    def body(x_ref, g_ref, h_ref):
        xv = x_ref[...]
        r = lax.rsqrt(jnp.mean(xv * xv, axis=-1, keepdims=True) + EPS)
        h_ref[...] = (xv * r * g_ref[...]).astype(BF16)

    return pl.pallas_call(
        body, name=name, out_shape=jax.ShapeDtypeStruct((t, d), BF16), grid=(t // bt,),
        in_specs=[pl.BlockSpec((bt, d), lambda i: (i, 0)), pl.BlockSpec((1, d), lambda i: (0, 0))],
        out_specs=pl.BlockSpec((bt, d), lambda i: (i, 0)),
        compiler_params=_cparams(("parallel",)),
    )(x, g)


def _rms_bwd(dh, x, g, dres, name):
    t, d = x.shape
    bt = _block(t, 256, 16)
    want_dx = dres is not None

    def body(*refs):
        if want_dx:
            dh_ref, x_ref, g_ref, dres_ref, dx_ref, dxb_ref, dg_ref = refs
        else:
            dh_ref, x_ref, g_ref, dg_ref = refs
        xv = x_ref[...]
        r = lax.rsqrt(jnp.mean(xv * xv, axis=-1, keepdims=True) + EPS)
        xhat = xv * r
        dhv = dh_ref[...]

        @pl.when(pl.program_id(0) == 0)
        def _():
            dg_ref[...] = jnp.zeros_like(dg_ref)

        dg_ref[...] += jnp.sum(dhv * xhat, axis=0, keepdims=True)
        if want_dx:
            dxhat = dhv * g_ref[...]
            dx = dres_ref[...] + r * (dxhat - xhat * jnp.mean(dxhat * xhat, axis=-1, keepdims=True))
            dx_ref[...] = dx
            dxb_ref[...] = dx.astype(BF16)

    row = pl.BlockSpec((bt, d), lambda i: (i, 0))
    vec = pl.BlockSpec((1, d), lambda i: (0, 0))
    if want_dx:
        return pl.pallas_call(
            body, name=name, grid=(t // bt,),
            out_shape=(jax.ShapeDtypeStruct((t, d), F32), jax.ShapeDtypeStruct((t, d), BF16),
                       jax.ShapeDtypeStruct((1, d), F32)),
            in_specs=[row, row, vec, row], out_specs=(row, row, vec),
            compiler_params=_cparams(("arbitrary",)),
        )(dh, x, g, dres)
    return pl.pallas_call(
        body, name=name, grid=(t // bt,), out_shape=jax.ShapeDtypeStruct((1, d), F32),
        in_specs=[row, row, vec], out_specs=vec,
        compiler_params=_cparams(("arbitrary",)),
    )(dh, x, g)


def _final_loss(x, target, g):
    t, d = x.shape
    bt = _block(t, 256, 16)

    def body(x_ref, t_ref, g_ref, dx_ref, dxb_ref, dg_ref, loss_ref):
        xv = x_ref[...]
        gv = g_ref[...]
        r = lax.rsqrt(jnp.mean(xv * xv, axis=-1, keepdims=True) + EPS)
        xhat = xv * r
        err = xhat * gv - t_ref[...]

        @pl.when(pl.program_id(0) == 0)
        def _():
            dg_ref[...] = jnp.zeros_like(dg_ref)
            loss_ref[...] = jnp.zeros_like(loss_ref)

        loss_ref[...] += 0.5 * jnp.sum(jnp.mean(err * err, axis=-1, keepdims=True), axis=0, keepdims=True)
        dy = err * (1.0 / d)
        dg_ref[...] += jnp.sum(dy * xhat, axis=0, keepdims=True)
        dxhat = dy * gv
        dx = r * (dxhat - xhat * jnp.mean(dxhat * xhat, axis=-1, keepdims=True))
        dx_ref[...] = dx
        dxb_ref[...] = dx.astype(BF16)

    row = pl.BlockSpec((bt, d), lambda i: (i, 0))
    vec = pl.BlockSpec((1, d), lambda i: (0, 0))
    return pl.pallas_call(
        body, name="final_loss", grid=(t // bt,),
        out_shape=(jax.ShapeDtypeStruct((t, d), F32), jax.ShapeDtypeStruct((t, d), BF16),
                   jax.ShapeDtypeStruct((1, d), F32), jax.ShapeDtypeStruct((1, LANES), F32)),
        in_specs=[row, row, vec], out_specs=(row, row, vec, pl.BlockSpec((1, LANES), lambda i: (0, 0))),
        compiler_params=_cparams(("arbitrary",)),
    )(x, target, g)


GU_COLS = GATE_WIDTH + POOL_WIDTH
U_BLK = GATE_WIDTH // POOL_WIDTH


def _shift_down(a, k, row):
    return jnp.where(row >= k, pltpu.roll(a, k, 0), 0.0)


def _shift_up(a, k, row):
    n = a.shape[0]
    return jnp.where(row < n - k, pltpu.roll(a, n - k, 0), 0.0)


def _window_delta(u, w, row):
    s, k = u, 1
    while k < w:
        s = s + _shift_down(s, k, row)
        k *= 2
    cnt = jnp.minimum(row + 1, w).astype(F32)
    return s / cnt - u, cnt


def _pool_fwd(gu, pool_w, pool_scale):
    b, s, _ = gu.shape

    def body(u_ref, pw_ref, sc_ref, y_ref):
        row = lax.broadcasted_iota(jnp.int32, (s, POOL_GC), 0)
        for g, w in enumerate(POOL_WINDOWS):
            cols = slice(g * POOL_GC, (g + 1) * POOL_GC)
            d, _ = _window_delta(u_ref[0, :, cols], w, row)
            z = jnp.dot(d.astype(BF16), pw_ref[g].astype(BF16), preferred_element_type=F32)
            y_ref[0, :, cols] = (z * sc_ref[:, cols]).astype(BF16)

    return pl.pallas_call(
        body, name="pool_fwd", out_shape=jax.ShapeDtypeStruct((b, s, POOL_WIDTH), BF16), grid=(b,),
        in_specs=[pl.BlockSpec((1, s, POOL_WIDTH), lambda i: (i, 0, U_BLK)),
                  pl.BlockSpec((4, POOL_GC, POOL_GC), lambda i: (0, 0, 0)),
                  pl.BlockSpec((1, POOL_WIDTH), lambda i: (0, 0))],
        out_specs=pl.BlockSpec((1, s, POOL_WIDTH), lambda i: (i, 0, 0)),
        compiler_params=_cparams(("parallel",)),
    )(gu, pool_w, pool_scale)


def _pool_bwd(gu, dy, pool_w, pool_scale, dgu):
    b, s, _ = gu.shape

    def body(u_ref, dy_ref, pw_ref, sc_ref, dgu_in, du_ref, dpw_ref, dsc_ref):
        del dgu_in

        @pl.when(pl.program_id(0) == 0)
        def _():
            dpw_ref[...] = jnp.zeros_like(dpw_ref)
            dsc_ref[...] = jnp.zeros_like(dsc_ref)

        row = lax.broadcasted_iota(jnp.int32, (s, POOL_GC), 0)
        for g, w in enumerate(POOL_WINDOWS):
            cols = slice(g * POOL_GC, (g + 1) * POOL_GC)
            d, cnt = _window_delta(u_ref[0, :, cols], w, row)
            db = d.astype(BF16)
            pw = pw_ref[g].astype(BF16)
            z = jnp.dot(db, pw, preferred_element_type=F32)
            dyv = dy_ref[0, :, cols]
            dsc_ref[:, cols] += jnp.sum(dyv * z, axis=0, keepdims=True)
            dz = (dyv * sc_ref[:, cols]).astype(BF16)
            dpw_ref[g] += lax.dot_general(db, dz, (((0,), (0,)), ((), ())), preferred_element_type=F32)
            dd = lax.dot_general(dz, pw, (((1,), (1,)), ((), ())), preferred_element_type=F32)
            acc, k = dd / cnt, 1
            while k < w:
                acc = acc + _shift_up(acc, k, row)
                k *= 2
            du_ref[0, :, cols] = (acc - dd).astype(BF16)

    return pl.pallas_call(
        body, name="pool_bwd", grid=(b,),
        out_shape=(jax.ShapeDtypeStruct((b, s, GU_COLS), BF16), jax.ShapeDtypeStruct((4, POOL_GC, POOL_GC), F32),
                   jax.ShapeDtypeStruct((1, POOL_WIDTH), F32)),
        in_specs=[pl.BlockSpec((1, s, POOL_WIDTH), lambda i: (i, 0, U_BLK)),
                  pl.BlockSpec((1, s, POOL_WIDTH), lambda i: (i, 0, 0)),
                  pl.BlockSpec((4, POOL_GC, POOL_GC), lambda i: (0, 0, 0)),
                  pl.BlockSpec((1, POOL_WIDTH), lambda i: (0, 0)), ANY],
        out_specs=(pl.BlockSpec((1, s, POOL_WIDTH), lambda i: (i, 0, U_BLK)),
                   pl.BlockSpec((4, POOL_GC, POOL_GC), lambda i: (0, 0, 0)),
                   pl.BlockSpec((1, POOL_WIDTH), lambda i: (0, 0))),
        input_output_aliases={4: 0},
        compiler_params=_cparams(("arbitrary",)),
    )(gu, dy, pool_w, pool_scale, dgu)


def _forget_cumsum(f, bias):
    b, s, c = f.shape

    def body(f_ref, b_ref, c_ref):
        row = lax.broadcasted_iota(jnp.int32, (s, LANES), 0)
        z = f_ref[0] + b_ref[...]
        acc = jnp.minimum(z, 0.0) - jnp.log(1.0 + jnp.exp(-jnp.abs(z)))
        k = 1
        while k < s:
            acc = acc + _shift_down(acc, k, row)
            k *= 2
        c_ref[0] = acc

    return pl.pallas_call(
        body, name="forget_cumsum", out_shape=jax.ShapeDtypeStruct((b, s, c), F32), grid=(b, c // LANES),
        in_specs=[pl.BlockSpec((1, s, LANES), lambda i, j: (i, 0, j)), pl.BlockSpec((1, LANES), lambda i, j: (0, j))],
        out_specs=pl.BlockSpec((1, s, LANES), lambda i, j: (i, 0, j)),
        compiler_params=_cparams(("parallel", "parallel")),
    )(f, bias)


def _forget_bwd(dc, f, bias):
    b, s, _ = f.shape

    def body(dc_ref, f_ref, b_ref, df_ref, db_ref):
        @pl.when(pl.program_id(0) == 0)
        def _():
            db_ref[...] = jnp.zeros_like(db_ref)

        row = lax.broadcasted_iota(jnp.int32, (s, LANES), 0)
        acc, k = dc_ref[0], 1
        while k < s:
            acc = acc + _shift_up(acc, k, row)
            k *= 2
        z = f_ref[0] + b_ref[...]
        df = acc / (1.0 + jnp.exp(z))
        db_ref[...] += jnp.sum(df, axis=0, keepdims=True)
        df_ref[0] = df.astype(BF16)

    blk = pl.BlockSpec((1, s, LANES), lambda i: (i, 0, 0))
    vec = pl.BlockSpec((1, LANES), lambda i: (0, 0))
    return pl.pallas_call(
        body, name="forget_bwd", grid=(b,),
        out_shape=(jax.ShapeDtypeStruct((b, s, LANES), BF16), jax.ShapeDtypeStruct((1, LANES), F32)),
        in_specs=[blk, blk, vec], out_specs=(blk, vec),
        compiler_params=_cparams(("arbitrary",)),
    )(dc, f, bias)


KV_BLK0 = 2
PAIRS = FOX_HEADS // 2
FOX_SCALE = FOX_DH ** -0.5
NT_DIMS = (((1,), (1,)), ((), ()))
TN_DIMS = (((0,), (0,)), ((), ()))


def _stack_heads(v):
    head = lax.broadcasted_iota(jnp.int32, v.shape, 1) // FOX_DH
    zero = jnp.zeros_like(v)
    return jnp.concatenate([jnp.where(head == 0, v, zero), jnp.where(head == 1, v, zero)], axis=0)


def _stack_cols(v):
    return jnp.concatenate([v[:, 0:1], v[:, FOX_DH:FOX_DH + 1]], axis=0)


def _unstack(t, blk):
    head = lax.broadcasted_iota(jnp.int32, (blk, LANES), 1) // FOX_DH
    return jnp.where(head == 0, t[:blk], t[blk:])


def _fox_scores(q_all, kblk, row_bias, cr_ref, kb, masked, blk):
    top = lax.broadcasted_iota(jnp.int32, (2 * blk, 1), 0) < blk
    s = lax.dot_general(q_all, kblk, NT_DIMS, preferred_element_type=F32)
    s = s + (row_bias - jnp.where(top, cr_ref[0, 0, kb], cr_ref[0, 1, kb]))
    if masked:
        r = lax.broadcasted_iota(jnp.int32, (2 * blk, blk), 0)
        keep = jnp.where(r >= blk, r - blk, r) >= lax.broadcasted_iota(jnp.int32, (2 * blk, blk), 1)
        s = jnp.where(keep, s, NEG_INF)
    return s


def _fox_fwd(qkv, c_exp, c_row):
    b, s, _ = qkv.shape
    blk = min(ATT_BLOCK, s)
    nq = s // blk

    def body(q_ref, kv_ref, cc_ref, cr_ref, o_ref, ob_ref, lse_ref):
        qi = pl.program_id(2)
        q_all = _stack_heads(q_ref[0] * FOX_SCALE)
        cq = _stack_cols(cc_ref[0])

        def step(kb, carry, masked):
            m, l, acc = carry
            rows = pl.ds(pl.multiple_of(kb * blk, blk), blk)
            sc = _fox_scores(q_all, kv_ref[0, rows, :LANES], cq, cr_ref, kb, masked, blk)
            m_new = jnp.maximum(m, jnp.max(sc, axis=-1, keepdims=True))
            p = jnp.exp(sc - m_new)
            alpha = jnp.exp(m - m_new)
            l = alpha * l + jnp.sum(p, axis=-1, keepdims=True)
            acc = alpha * acc + jnp.dot(p.astype(BF16), kv_ref[0, rows, LANES:], preferred_element_type=F32)
            return m_new, l, acc

        init = (jnp.full((2 * blk, 1), NEG_INF, F32), jnp.zeros((2 * blk, 1), F32), jnp.zeros((2 * blk, LANES), F32))
        m, l, acc = step(qi, lax.fori_loop(0, qi, functools.partial(step, masked=False), init), True)
        o = _unstack(acc / l, blk)
        o_ref[0] = o
        ob_ref[0] = o.astype(BF16)
        lse_ref[0] = _unstack(jnp.broadcast_to(m + jnp.log(l), (2 * blk, LANES)), blk)

    tile = pl.BlockSpec((1, blk, LANES), lambda i, h, q: (i, q, h))
    kvspec = pl.BlockSpec((1, s, 2 * LANES), lambda i, h, q: (i, 0, KV_BLK0 + h))
    shape = jax.ShapeDtypeStruct((b, s, FOX_WIDTH), F32)
    return pl.pallas_call(
        body, name="fox_fwd", out_shape=(shape, jax.ShapeDtypeStruct((b, s, FOX_WIDTH), BF16), shape),
        grid=(b, PAIRS, nq),
        in_specs=[tile, kvspec, tile, pl.BlockSpec((1, 2, nq, 1, blk), lambda i, h, q: (i, h, 0, 0, 0))],
        out_specs=(tile, tile, tile),
        compiler_params=_cparams(("parallel", "parallel", "arbitrary")),
    )(qkv, qkv, c_exp, c_row)


def _fox_bwd_q(qkv, c_exp, c_row, lse, o, do):
    b, s, _ = qkv.shape
    blk = min(ATT_BLOCK, s)
    nq = s // blk

    def body(q_ref, kv_ref, cc_ref, cr_ref, lse_ref, o_ref, do_ref, dq_ref, dl_ref, dcq_ref):
        qi = pl.program_id(2)
        q_all = _stack_heads(q_ref[0] * FOX_SCALE)
        dov = do_ref[0]
        do_all = _stack_heads(dov.astype(BF16))
        delta = jnp.sum(_stack_heads(dov * o_ref[0]), axis=-1, keepdims=True)
        bias = _stack_cols(cc_ref[0]) - _stack_cols(lse_ref[0])

        def step(kb, carry, masked):
            acc, dcq = carry
            rows = pl.ds(pl.multiple_of(kb * blk, blk), blk)
            kblk = kv_ref[0, rows, :LANES]
            p = jnp.exp(_fox_scores(q_all, kblk, bias, cr_ref, kb, masked, blk))
            dp = lax.dot_general(do_all, kv_ref[0, rows, LANES:], NT_DIMS, preferred_element_type=F32)
            ds = p * (dp - delta)
            acc = acc + jnp.dot(ds.astype(BF16), kblk, preferred_element_type=F32)
            return acc, dcq + jnp.sum(ds, axis=-1, keepdims=True)

        init = (jnp.zeros((2 * blk, LANES), F32), jnp.zeros((2 * blk, 1), F32))
        acc, dcq = step(qi, lax.fori_loop(0, qi, functools.partial(step, masked=False), init), True)
        dq_ref[0] = (_unstack(acc, blk) * FOX_SCALE).astype(BF16)
        dl_ref[0] = _unstack(jnp.broadcast_to(delta, (2 * blk, LANES)), blk)
        dcq_ref[0] = _unstack(jnp.broadcast_to(dcq, (2 * blk, LANES)), blk)

    tile = pl.BlockSpec((1, blk, LANES), lambda i, h, q: (i, q, h))
    kvspec = pl.BlockSpec((1, s, 2 * LANES), lambda i, h, q: (i, 0, KV_BLK0 + h))
    shape = jax.ShapeDtypeStruct((b, s, FOX_WIDTH), F32)
    return pl.pallas_call(
        body, name="fox_bwd_q", grid=(b, PAIRS, nq),
        out_shape=(jax.ShapeDtypeStruct(qkv.shape, BF16), shape, shape),
        in_specs=[tile, kvspec, tile, pl.BlockSpec((1, 2, nq, 1, blk), lambda i, h, q: (i, h, 0, 0, 0)),
                  tile, tile, tile],
        out_specs=(tile, tile, tile),
        compiler_params=_cparams(("parallel", "parallel", "arbitrary")),
    )(qkv, qkv, c_exp, c_row, lse, o, do)


def _fox_bwd_kv(qkv, c_exp, c_row, lse, delta, do, dqkv):
    b, s, _ = qkv.shape
    blk = min(ATT_BLOCK, s)
    nq = s // blk

    def body(q_ref, kv_ref, cc_ref, cr_ref, lse_ref, dl_ref, do_ref, dqkv_in, dkv_ref, dc_ref):
        del dqkv_in
        ki = pl.program_id(2)
        kblk = kv_ref[0, :, :LANES]
        vblk = kv_ref[0, :, LANES:]

        def step(qb, carry, masked):
            dk, dv, dc = carry
            rows = pl.ds(pl.multiple_of(qb * blk, blk), blk)
            q_all = _stack_heads(q_ref[0, rows, :] * FOX_SCALE)
            do_all = _stack_heads(do_ref[0, rows, :].astype(BF16))
            bias = _stack_cols(cc_ref[0, rows, :]) - _stack_cols(lse_ref[0, rows, :])
            delta = _stack_cols(dl_ref[0, rows, :])
            p = jnp.exp(_fox_scores(q_all, kblk, bias, cr_ref, 0, masked, blk))
            dv = dv + lax.dot_general(p.astype(BF16), do_all, TN_DIMS, preferred_element_type=F32)
            dp = lax.dot_general(do_all, vblk, NT_DIMS, preferred_element_type=F32)
            ds = p * (dp - delta)
            dk = dk + lax.dot_general(ds.astype(BF16), q_all, TN_DIMS, preferred_element_type=F32)
            col = jnp.concatenate([jnp.sum(ds[:blk], axis=0, keepdims=True), jnp.sum(ds[blk:], axis=0, keepdims=True)],
                                  axis=0)
            return dk, dv, dc - col

        zero = jnp.zeros((blk, LANES), F32)
        carry = step(ki, (zero, zero, jnp.zeros((2, blk), F32)), True)
        dk, dv, dc = lax.fori_loop(ki + 1, nq, functools.partial(step, masked=False), carry)
        dkv_ref[0, :, :LANES] = dk.astype(BF16)
        dkv_ref[0, :, LANES:] = dv.astype(BF16)
        dc_ref[0, 0, 0] = dc[0:1]
        dc_ref[0, 1, 0] = dc[1:2]

    full = pl.BlockSpec((1, s, LANES), lambda i, h, k: (i, 0, h))
    kvtile = pl.BlockSpec((1, blk, 2 * LANES), lambda i, h, k: (i, k, KV_BLK0 + h))
    crow = pl.BlockSpec((1, 2, 1, 1, blk), lambda i, h, k: (i, h, k, 0, 0))
    return pl.pallas_call(
        body, name="fox_bwd_kv", grid=(b, PAIRS, nq),
        out_shape=(jax.ShapeDtypeStruct(qkv.shape, BF16), jax.ShapeDtypeStruct(c_row.shape, F32)),
        in_specs=[full, kvtile, full, crow, full, full, full, ANY],
        out_specs=(kvtile, crow),
        input_output_aliases={7: 0},
        compiler_params=_cparams(("parallel", "parallel", "arbitrary")),
    )(qkv, qkv, c_exp, c_row, lse, delta, do, dqkv)


def _sigmoid(z):
    return 1.0 / (1.0 + jnp.exp(-z))


def _mix_fwd(gu, b_gate, y_pool, y_fox):
    t = gu.shape[0]
    bt = _block(t, 256, 16)

    def body(gp_ref, gf_ref, bp_ref, bf_ref, yp_ref, yf_ref, o_ref):
        gp = _sigmoid(gp_ref[...] + bp_ref[...])
        gf = _sigmoid(gf_ref[...] + bf_ref[...])
        o_ref[...] = (gp * yp_ref[...] + gf * yf_ref[...]).astype(BF16)

    col = lambda j: pl.BlockSpec((bt, D_MODEL), lambda i: (i, j))
    vec = lambda j: pl.BlockSpec((1, D_MODEL), lambda i: (0, j))
    return pl.pallas_call(
        body, name="mix_fwd", out_shape=jax.ShapeDtypeStruct((t, D_MODEL), BF16), grid=(t // bt,),
        in_specs=[col(0), col(1), vec(0), vec(1), col(0), col(0)], out_specs=col(0),
        compiler_params=_cparams(("parallel",)),
    )(gu, gu, b_gate, b_gate, y_pool, y_fox)


def _mix_bwd(gu, b_gate, y_pool, y_fox, dmix):
    t = gu.shape[0]
    bt = _block(t, 256, 16)

    def body(gp_ref, gf_ref, bp_ref, bf_ref, yp_ref, yf_ref, dm_ref, dyp_ref, dyf_ref, dgl_ref, db_ref):
        @pl.when(pl.program_id(0) == 0)
        def _():
            db_ref[...] = jnp.zeros_like(db_ref)

        dm = dm_ref[...]
        gp = _sigmoid(gp_ref[...] + bp_ref[...])
        gf = _sigmoid(gf_ref[...] + bf_ref[...])
        dyp_ref[...] = (dm * gp).astype(BF16)
        dyf_ref[...] = (dm * gf).astype(BF16)
        dlp = dm * yp_ref[...] * gp * (1.0 - gp)
        dlf = dm * yf_ref[...] * gf * (1.0 - gf)
        dgl_ref[:, :D_MODEL] = dlp.astype(BF16)
        dgl_ref[:, D_MODEL:] = dlf.astype(BF16)
        db_ref[:, :D_MODEL] += jnp.sum(dlp, axis=0, keepdims=True)
        db_ref[:, D_MODEL:] += jnp.sum(dlf, axis=0, keepdims=True)

    col = lambda j: pl.BlockSpec((bt, D_MODEL), lambda i: (i, j))
    vec = lambda j: pl.BlockSpec((1, D_MODEL), lambda i: (0, j))
    wide = pl.BlockSpec((bt, GATE_WIDTH), lambda i: (i, 0))
    return pl.pallas_call(
        body, name="mix_bwd", grid=(t // bt,),
        out_shape=(jax.ShapeDtypeStruct((t, D_MODEL), BF16), jax.ShapeDtypeStruct((t, D_MODEL), BF16),
                   jax.ShapeDtypeStruct((t, GU_COLS), BF16), jax.ShapeDtypeStruct((1, GATE_WIDTH), F32)),
        in_specs=[col(0), col(1), vec(0), vec(1), col(0), col(0), col(0)],
        out_specs=(col(0), col(0), wide, pl.BlockSpec((1, GATE_WIDTH), lambda i: (0, 0))),
        compiler_params=_cparams(("arbitrary",)),
    )(gu, gu, b_gate, b_gate, y_pool, y_fox, dmix)


X_SCALE = X_DH ** -0.5


def _xattn_probs(qh, kh):
    s = lax.dot_general(qh, kh, NT_DIMS, preferred_element_type=F32) * X_SCALE
    e = jnp.exp(s - jnp.max(s, axis=-1, keepdims=True))
    return e / jnp.sum(e, axis=-1, keepdims=True)


def _xattn_fwd(q, kv):
    b, s, _ = q.shape
    m = kv.shape[1]
    bq = _block(s, 512, 16)

    def body(q_ref, kv_ref, o_ref):
        for h in range(X_HEADS):
            cols = slice(h * X_DH, (h + 1) * X_DH)
            p = _xattn_probs(q_ref[0, :, cols], kv_ref[0, :, cols])
            vh = kv_ref[0, :, X_WIDTH + h * X_DH:X_WIDTH + (h + 1) * X_DH]
            o_ref[0, :, cols] = jnp.dot(p.astype(BF16), vh, preferred_element_type=F32).astype(BF16)

    return pl.pallas_call(
        body, name="xattn_fwd", out_shape=jax.ShapeDtypeStruct((b, s, X_WIDTH), BF16), grid=(b, s // bq),
        in_specs=[pl.BlockSpec((1, bq, X_WIDTH), lambda i, j: (i, j, 0)),
                  pl.BlockSpec((1, m, 2 * X_WIDTH), lambda i, j: (i, 0, 0))],
        out_specs=pl.BlockSpec((1, bq, X_WIDTH), lambda i, j: (i, j, 0)),
        compiler_params=_cparams(("parallel", "parallel")),
    )(q, kv)


def _xattn_bwd(q, kv, do):
    b, s, _ = q.shape
    m = kv.shape[1]
    bq = _block(s, 512, 16)

    def body(q_ref, kv_ref, do_ref, dq_ref, dkv_ref):
        @pl.when(pl.program_id(1) == 0)
        def _():
            dkv_ref[...] = jnp.zeros_like(dkv_ref)

        for h in range(X_HEADS):
            cols = slice(h * X_DH, (h + 1) * X_DH)
            vcols = slice(X_WIDTH + h * X_DH, X_WIDTH + (h + 1) * X_DH)
            qh, kh, vh, doh = q_ref[0, :, cols], kv_ref[0, :, cols], kv_ref[0, :, vcols], do_ref[0, :, cols]
            p = _xattn_probs(qh, kh)
            dkv_ref[0, :, vcols] += lax.dot_general(p.astype(BF16), doh, TN_DIMS, preferred_element_type=F32)
            dp = lax.dot_general(doh, vh, NT_DIMS, preferred_element_type=F32)
            ds = (p * (dp - jnp.sum(p * dp, axis=-1, keepdims=True)) * X_SCALE).astype(BF16)
            dq_ref[0, :, cols] = jnp.dot(ds, kh, preferred_element_type=F32).astype(BF16)
            dkv_ref[0, :, cols] += lax.dot_general(ds, qh, TN_DIMS, preferred_element_type=F32)

    tile = pl.BlockSpec((1, bq, X_WIDTH), lambda i, j: (i, j, 0))
    mem = pl.BlockSpec((1, m, 2 * X_WIDTH), lambda i, j: (i, 0, 0))
    return pl.pallas_call(
        body, name="xattn_bwd", grid=(b, s // bq),
        out_shape=(jax.ShapeDtypeStruct((b, s, X_WIDTH), BF16), jax.ShapeDtypeStruct((b, m, 2 * X_WIDTH), F32)),
        in_specs=[tile, mem, tile], out_specs=(tile, mem),
        compiler_params=_cparams(("parallel", "arbitrary")),
    )(q, kv, do)


def _swiglu_fwd(gu):
    t = gu.shape[0]
    bt = _block(t, 256, 16)

    def body(gt_ref, up_ref, o_ref):
        gt = gt_ref[...]
        o_ref[...] = (gt * _sigmoid(gt) * up_ref[...]).astype(BF16)

    col = lambda j: pl.BlockSpec((bt, D_FF), lambda i: (i, j))
    return pl.pallas_call(
        body, name="swiglu_fwd", out_shape=jax.ShapeDtypeStruct((t, D_FF), BF16), grid=(t // bt,),
        in_specs=[col(0), col(1)], out_specs=col(0),
        compiler_params=_cparams(("parallel",)),
    )(gu, gu)


def _swiglu_bwd(gu, dact):
    t = gu.shape[0]
    bt = _block(t, 256, 16)

    def body(gt_ref, up_ref, da_ref, o_ref):
        gt = gt_ref[...]
        da = da_ref[...]
        sg = _sigmoid(gt)
        silu = gt * sg
        o_ref[:, :D_FF] = (da * up_ref[...] * (sg + silu * (1.0 - sg))).astype(BF16)
        o_ref[:, D_FF:] = (da * silu).astype(BF16)

    col = lambda j: pl.BlockSpec((bt, D_FF), lambda i: (i, j))
    return pl.pallas_call(
        body, name="swiglu_bwd", out_shape=jax.ShapeDtypeStruct((t, 2 * D_FF), BF16), grid=(t // bt,),
        in_specs=[col(0), col(1), col(0)], out_specs=pl.BlockSpec((bt, 2 * D_FF), lambda i: (i, 0)),
        compiler_params=_cparams(("parallel",)),
    )(gu, gu, dact)


def _stack_of(w, axis):
    r, c = w.shape
    if axis == 0:
        return w.reshape(N_CHIPS, r // N_CHIPS, c)
    return w.reshape(r, N_CHIPS, c // N_CHIPS).transpose(1, 0, 2)


def _unstack_cols(w3):
    n, r, c = w3.shape
    return w3.transpose(1, 0, 2).reshape(r, n * c)


def _stack_t(w3):
    n, r, c = w3.shape
    return w3.transpose(0, 2, 1).reshape(n * c, r)


def _pair_kv(k, v):
    r = k.shape[0]
    return jnp.stack([k.reshape(r, PAIRS, LANES), v.reshape(r, PAIRS, LANES)], axis=2).reshape(r, 2 * FOX_WIDTH)


def _unpair_kv(kv):
    r = kv.shape[0]
    kv = kv.reshape(r, PAIRS, 2, LANES)
    return kv[:, :, 0, :].reshape(r, FOX_WIDTH), kv[:, :, 1, :].reshape(r, FOX_WIDTH)


def _step(x, mem, loss_target, weights, moments_m, moments_v):
    nb, s, d = x.shape
    n_mem = mem.shape[1]
    t = nb * s
    blk = min(ATT_BLOCK, s)
    x2 = x.reshape(t, d)
    mem2 = mem.reshape(nb * n_mem, d)
    tgt2 = loss_target.reshape(t, d)
    local = {n: weights[n].reshape(weights[n].shape[1:]) for n, _, _ in SHARDED}

    names = [n for n, _, _ in SHARDED]
    me = 2 * lax.axis_index("x") + lax.axis_index("y")
    local_b = [local[n].astype(BF16) for n in names]
    stacks = {n: lax.dynamic_update_slice(others, mine[None], (me, 0, 0))
              for n, others, mine in zip(names, _gather_weights(local_b), local_b)}
    w_in = _unstack_cols(stacks["w_in"])
    w_gu = jnp.concatenate([w_in[:, 2056:], w_in[:, :512]], axis=1)
    w_qkv = jnp.concatenate([w_in[:, 512:1024], _pair_kv(w_in[:, 1024:1536], w_in[:, 1536:2048])], axis=1)
    w_f = jnp.pad(w_in[:, 2048:2056], ((0, 0), (0, LANES - FOX_HEADS)))
    w_pool_out3, w_fox_out3, w_xo3, w_ffn_in3 = (stacks[n] for n in ("w_pool_out", "w_fox_out", "w_xo", "w_ffn_in"))
    w_out, w_xq, w_xkv, w_ffn_out = (stacks[n].reshape(-1, stacks[n].shape[2])
                                     for n in ("w_out", "w_xq", "w_xkv", "w_ffn_out"))

    g_mix, g_x, g_mem, g_ffn = (weights[n] for n in ("norm_mix_g", "norm_x_g", "norm_mem_g", "norm_ffn_g"))
    g_final = weights["norm_final_g"].reshape(1, d)
    pool_w = weights["pool_w"].reshape(4, POOL_GC, POOL_GC)
    pool_scale, b_gate = weights["pool_scale"], weights["b_gate"]
    b_f_pad = jnp.pad(weights["b_forget"], ((0, 0), (0, LANES - FOX_HEADS)))
    b_f_exp = jnp.repeat(weights["b_forget"], FOX_DH, axis=1)

    h = _rms_fwd(x2, g_mix, "norm_mix")
    gu = _mm(h, w_gu, bn=512, name="in_proj_gates_pool")
    qkv = _mm(h, w_qkv, out_dtype=BF16, bn=512, name="in_proj_qkv")
    f_pad = _mm(h, w_f, name="in_proj_forget")
    gu3, qkv3 = gu.reshape(nb, s, GU_COLS), qkv.reshape(nb, s, 3 * FOX_WIDTH)
    y = _pool_fwd(gu3, pool_w, pool_scale)
    f_exp = jnp.repeat(f_pad[:, :FOX_HEADS], FOX_DH, axis=1).reshape(nb, s, FOX_WIDTH)
    c_exp = _forget_cumsum(f_exp, b_f_exp)
    c_row = c_exp[:, :, ::FOX_DH].transpose(0, 2, 1).reshape(nb, FOX_HEADS, s // blk, 1, blk)
    o, o_b, lse = _fox_fwd(qkv3, c_exp, c_row)
    y2, o2 = y.reshape(t, POOL_WIDTH), o_b.reshape(t, FOX_WIDTH)
    y_pool = _mm(y2, w_pool_out3, b_stack=True, name="pool_out")
    y_fox = _mm(o2, w_fox_out3, b_stack=True, name="fox_out")
    mix = _mix_fwd(gu, b_gate, y_pool, y_fox)
    x1 = _mm(mix, w_out, res=x2, name="mix_out")
    hx = _rms_fwd(x1, g_x, "norm_x")
    mem_n = _rms_fwd(mem2, g_mem, "norm_mem")
    qx = _mm(hx, w_xq, out_dtype=BF16, name="x_q")
    kv = _mm(mem_n, w_xkv, out_dtype=BF16, name="x_kv")
    qx3, kv3 = qx.reshape(nb, s, X_WIDTH), kv.reshape(nb, n_mem, 2 * X_WIDTH)
    ox = _xattn_fwd(qx3, kv3).reshape(t, X_WIDTH)
    x2_ = _mm(ox, w_xo3, b_stack=True, res=x1, name="x_out")
    hf = _rms_fwd(x2_, g_ffn, "norm_ffn")
    ffn = _mm(hf, w_ffn_in3, b_stack=True, bm=512, bn=1408, name="ffn_in")
    act = _swiglu_fwd(ffn)
    x3 = _mm(act, w_ffn_out, res=x2_, name="ffn_out")

    dx3, dx3_b, dg_final, loss_part = _final_loss(x3, tgt2, g_final)
    dw_ffn_out = _mm(act, dx3_b, ta=True, bm=1408, bn=512, bk=2048, name="d_w_ffn_out")
    dact = _mm(dx3_b, w_ffn_out.T, bn=1408, name="d_act")
    dffn = _swiglu_bwd(ffn, dact)
    dw_ffn_in = _mm(hf, dffn, ta=True, bm=512, bn=1408, bk=2048, out_stack=True, name="d_w_ffn_in")
    dhf = _mm(dffn, _stack_t(w_ffn_in3), bk=2816, name="d_hf")
    dx2, dx2_b, dg_ffn = _rms_bwd(dhf, x2_, g_ffn, dx3, "norm_ffn_bwd")

    dw_xo = _mm(ox, dx2_b, ta=True, bn=256, out_stack=True, name="d_w_xo")
    dox = _mm(dx2_b, _stack_t(w_xo3), out_dtype=BF16, name="d_ox").reshape(nb, s, X_WIDTH)
    dqx, dkv = _xattn_bwd(qx3, kv3, dox)
    dqx2, dkv2 = dqx.reshape(t, X_WIDTH), dkv.reshape(nb * n_mem, 2 * X_WIDTH)
    dw_xkv = _mm(mem_n, dkv2, ta=True, name="d_w_xkv")
    dmem_n = _mm(dkv2, w_xkv.T, name="d_mem_n")
    dg_mem = _rms_bwd(dmem_n, mem2, g_mem, None, "norm_mem_bwd")
    dw_xq = _mm(hx, dqx2, ta=True, name="d_w_xq")
    dhx = _mm(dqx2, w_xq.T, name="d_hx")
    dx1, dx1_b, dg_x = _rms_bwd(dhx, x1, g_x, dx2, "norm_x_bwd")

    dw_out = _mm(mix, dx1_b, ta=True, name="d_w_out")
    dmix = _mm(dx1_b, w_out.T, name="d_mix")
    dyp, dyf, dgu, db_gate = _mix_bwd(gu, b_gate, y_pool, y_fox, dmix)
    dw_pool_out = _mm(y2, dyp, ta=True, bn=256, out_stack=True, name="d_w_pool_out")
    dw_fox_out = _mm(o2, dyf, ta=True, bn=256, out_stack=True, name="d_w_fox_out")
    dy = _mm(dyp, _stack_t(w_pool_out3), name="d_y").reshape(nb, s, POOL_WIDTH)
    do = _mm(dyf, _stack_t(w_fox_out3), name="d_o").reshape(nb, s, FOX_WIDTH)
    dgu3, dpool_w, dpool_scale = _pool_bwd(gu3, dy, pool_w, pool_scale, dgu.reshape(nb, s, GU_COLS))
    dqkv3, delta, dc_q = _fox_bwd_q(qkv3, c_exp, c_row, lse, o, do)
    dqkv3, dc_row = _fox_bwd_kv(qkv3, c_exp, c_row, lse, delta, do, dqkv3)
    dc = dc_row.reshape(nb, FOX_HEADS, s).transpose(0, 2, 1) + dc_q[:, :, ::FOX_DH]
    dc = jnp.pad(dc, ((0, 0), (0, 0), (0, LANES - FOX_HEADS)))
    df, db_f = _forget_bwd(dc, f_pad.reshape(nb, s, LANES), b_f_pad)
    dgu2, dqkv2, df2 = dgu3.reshape(t, GU_COLS), dqkv3.reshape(t, 3 * FOX_WIDTH), df.reshape(t, LANES)
    dw_gu = _mm(h, dgu2, ta=True, bn=512, name="d_w_gates_pool")
    dw_qkv = _mm(h, dqkv2, ta=True, bn=512, name="d_w_qkv")
    dw_f = _mm(h, df2, ta=True, name="d_w_forget")
    dh = _mm(df2, w_f.T, name="d_h_forget")
    dh = _mm(dqkv2, w_qkv.T, res=dh, name="d_h_qkv")
    dh = _mm(dgu2, w_gu.T, res=dh, name="d_h_gates_pool")
    dx, _, dg_mix = _rms_bwd(dh, x2, g_mix, dx1, "norm_mix_bwd")
    dw_k, dw_v = _unpair_kv(dw_qkv[:, FOX_WIDTH:])
    dw_in = jnp.concatenate([dw_gu[:, GATE_WIDTH:], dw_qkv[:, :FOX_WIDTH], dw_k, dw_v, dw_f[:, :FOX_HEADS],
                             dw_gu[:, :GATE_WIDTH]], axis=1)

    grad_stacks = {"w_in": _stack_of(dw_in, 1), "w_pool_out": dw_pool_out, "w_fox_out": dw_fox_out,
                   "w_out": _stack_of(dw_out, 0), "w_xq": _stack_of(dw_xq, 0), "w_xkv": _stack_of(dw_xkv, 0),
                   "w_xo": dw_xo, "w_ffn_in": dw_ffn_in, "w_ffn_out": _stack_of(dw_ffn_out, 0)}
    partial = [grad_stacks[n] for n in names]
    core = lax.axis_index("c").astype(jnp.int32).reshape(1)
    place = jnp.stack([lax.axis_index("c"), me]).astype(jnp.int32)
    theirs = _swap_halves(partial)
    chip_sums = [_sum_halves(g_, t_, core, "sum_halves_" + n) for n, g_, t_ in zip(names, partial, theirs)]
    slots = _exchange_chips(chip_sums)
    reduced = _join_halves([_sum_chips(s_, c_, place, "sum_chips_" + n) for n, s_, c_ in zip(names, slots, chip_sums)])

    small_grads = {"norm_mix_g": dg_mix, "b_forget": db_f[:, :FOX_HEADS], "b_gate": db_gate, "pool_w": dpool_w,
                   "pool_scale": dpool_scale, "norm_x_g": dg_x, "norm_mem_g": dg_mem, "norm_ffn_g": dg_ffn,
                   "norm_final_g": dg_final}
    parts = _gather_small(_pack_small([small_grads[n] for n, _ in SMALL], last=loss_part[0, 0]))
    sg, sd, sm, sv = _adamw_small(parts.reshape(N_DEV, SMALL_ROWS, LANES),
                                  _pack_small([weights[n] for n, _ in SMALL]),
                                  _pack_small([moments_m[n] for n, _ in SMALL]),
                                  _pack_small([moments_v[n] for n, _ in SMALL]))
    loss = sg.reshape(-1)[LOSS_POS]

    grads, deltas, new_m, new_v = {}, {}, {}, {}
    for (n, _), g_, d_, m_, v_ in zip(SMALL, _unpack_small(sg), _unpack_small(sd), _unpack_small(sm), _unpack_small(sv)):
        grads[n], deltas[n], new_m[n], new_v[n] = g_, d_, m_, v_
    for n, g_ in zip(names, reduced):
        shp = weights[n].shape
        d_, m_, v_ = _adamw(local[n], g_, moments_m[n].reshape(g_.shape), moments_v[n].reshape(g_.shape), "adamw_" + n)
        grads[n], deltas[n], new_m[n], new_v[n] = (a.reshape(shp) for a in (g_, d_, m_, v_))
    return loss, dx.reshape(nb, s, d), grads, deltas, new_m, new_v


def kernel(x, mem, norm_mix_g, w_in, b_forget, b_gate, pool_w, pool_scale, w_pool_out, w_fox_out, w_out, norm_x_g, norm_mem_g, w_xq, w_xkv, w_xo, norm_ffn_g, w_ffn_in, w_ffn_out, norm_final_g, loss_target, m_norm_mix_g, m_w_in, m_b_forget, m_b_gate, m_pool_w, m_pool_scale, m_w_pool_out, m_w_fox_out, m_w_out, m_norm_x_g, m_norm_mem_g, m_w_xq, m_w_xkv, m_w_xo, m_norm_ffn_g, m_w_ffn_in, m_w_ffn_out, m_norm_final_g, v_norm_mix_g, v_w_in, v_b_forget, v_b_gate, v_pool_w, v_pool_scale, v_w_pool_out, v_w_fox_out, v_w_out, v_norm_x_g, v_norm_mem_g, v_w_xq, v_w_xkv, v_w_xo, v_norm_ffn_g, v_w_ffn_in, v_w_ffn_out, v_norm_final_g):
    given = dict(locals())
    weights = {n: given[n] for n in WEIGHT_ORDER}
    moments_m = {n: given["m_" + n] for n in WEIGHT_ORDER}
    moments_v = {n: given["v_" + n] for n in WEIGHT_ORDER}
    loss, grad_x, grads, deltas, new_m, new_v = _step(x, mem, loss_target, weights, moments_m, moments_v)
    return (loss, grad_x, *[grads[n] for n in WEIGHT_ORDER], *[deltas[n] for n in WEIGHT_ORDER],
            *[new_m[n] for n in WEIGHT_ORDER], *[new_v[n] for n in WEIGHT_ORDER])
```

```python
import functools
import math

import jax
import jax.numpy as jnp
from jax import lax
from jax.experimental import pallas as pl
from jax.experimental.pallas import tpu as pltpu

F32 = jnp.float32
BF16 = jnp.bfloat16
MESH = pl.DeviceIdType.MESH

D_MODEL = 1024
EPS = 1e-6
POOL_WINDOWS = (2, 4, 8, 16)
POOL_WIDTH = 512
POOL_GC = 128
FOX_HEADS = 8
FOX_DH = 64
FOX_WIDTH = 512
X_HEADS = 4
X_DH = 128
X_WIDTH = 512
D_FF = 2816
IN_COLS = 4104
GATE_WIDTH = 2048
ADAM_LR = 0.001
ADAM_B1 = 0.9
ADAM_B2 = 0.999
ADAM_EPS = 1e-08
ADAM_WD = 0.01
ADAM_STEP = 10

N_CHIPS = 4
N_DEV = 8
LANES = 128
VMEM_LIMIT_BYTES = 56 * 1024 * 1024
NEG_INF = -1e30
ATT_BLOCK = 512

SHARDED = (
    ("w_in", (1024, IN_COLS), 1),
    ("w_pool_out", (POOL_WIDTH, 1024), 1),
    ("w_fox_out", (FOX_WIDTH, 1024), 1),
    ("w_out", (1024, 1024), 0),
    ("w_xq", (1024, X_WIDTH), 0),
    ("w_xkv", (1024, 2 * X_WIDTH), 0),
    ("w_xo", (X_WIDTH, 1024), 1),
    ("w_ffn_in", (1024, 2 * D_FF), 1),
    ("w_ffn_out", (D_FF, 1024), 0),
)
SMALL = (
    ("norm_mix_g", (1, 1024)),
    ("b_forget", (1, 8)),
    ("b_gate", (1, 2048)),
    ("pool_w", (1, 4, 128, 128)),
    ("pool_scale", (1, 512)),
    ("norm_x_g", (1, 1024)),
    ("norm_mem_g", (1, 1024)),
    ("norm_ffn_g", (1, 1024)),
    ("norm_final_g", (1024,)),
)
WEIGHT_ORDER = ("norm_mix_g", "w_in", "b_forget", "b_gate", "pool_w", "pool_scale", "w_pool_out", "w_fox_out", "w_out",
                "norm_x_g", "norm_mem_g", "w_xq", "w_xkv", "w_xo", "norm_ffn_g", "w_ffn_in", "w_ffn_out", "norm_final_g")


def _round_up(n, m):
    return (n + m - 1) // m * m


SMALL_ELEMS = sum(math.prod(s) for _, s in SMALL)
SMALL_ROWS = _round_up(SMALL_ELEMS // LANES + 1, 8)
LOSS_POS = SMALL_ROWS * LANES - 1


def _cparams(sem=None):
    return pltpu.CompilerParams(dimension_semantics=sem, vmem_limit_bytes=VMEM_LIMIT_BYTES)


def _block(dim, pref, unit):
    if dim <= pref:
        return dim
    best = None
    for b in range(unit, pref + 1, unit):
        if dim % b == 0:
            best = b
    assert best is not None, (dim, pref, unit)
    return best


def _rows_block(rows, cols, unit=16, elems=1 << 19):
    return _block(rows, max(unit, elems // cols // unit * unit), unit)


def _pack_small(parts, last=None):
    flat = jnp.concatenate([p.reshape(-1).astype(F32) for p in parts])
    flat = jnp.pad(flat, (0, SMALL_ROWS * LANES - flat.shape[0]))
    if last is not None:
        flat = flat.at[LOSS_POS].set(last)
    return flat.reshape(SMALL_ROWS, LANES)


def _unpack_small(packed):
    flat = packed.reshape(-1)
    out, off = [], 0
    for _, shape in SMALL:
        n = math.prod(shape)
        out.append(flat[off:off + n].reshape(shape))
        off += n
    return out


def _my_place():
    return lax.axis_index("x"), lax.axis_index("y"), lax.axis_index("c")


def _other_chips(x, y):
    return [(1 - x, y), (x, 1 - y), (1 - x, 1 - y)]


def _chip(place):
    return 2 * place[0] + place[1]


ANY = pl.BlockSpec(memory_space=pl.ANY)


class _Exchange:
    def __init__(self, arrays, out_shapes, n_sems, start, finish):
        self.arrays, self.out_shapes, self.n_sems, self.start, self.finish = arrays, out_shapes, n_sems, start, finish

    def scratch(self):
        return [pltpu.SemaphoreType.DMA((self.n_sems,)), pltpu.SemaphoreType.DMA((self.n_sems,))]


def _run_exchange(ex, name):
    n = len(ex.arrays)

    def body(*refs):
        ins, outs, sems = refs[:n], refs[n:2 * n], refs[2 * n:]
        ex.start(ins, outs, *sems)
        ex.finish(ins, outs, *sems)

    return pl.pallas_call(
        body, name=name, out_shape=ex.out_shapes, in_specs=[ANY] * n, out_specs=[ANY] * n, scratch_shapes=ex.scratch(),
    )(*ex.arrays)


def _hosted_call(body, ex, *, name, grid, in_specs, out_specs, out_shape, args, aliases=None):
    n_in, n_out = len(args), len(out_shape)
    if ex is None:
        outs = pl.pallas_call(
            body, name=name, grid=grid, out_shape=out_shape, in_specs=in_specs, out_specs=out_specs,
            input_output_aliases=aliases or {}, compiler_params=_cparams(("arbitrary",) * len(grid)))(*args)
        return outs, None
    nc = len(ex.arrays)

    def full_body(*refs):
        ins, cins = refs[:n_in], refs[n_in:n_in + nc]
        outs, couts = refs[n_in + nc:n_in + nc + n_out], refs[n_in + nc + n_out:n_in + 2 * nc + n_out]
        sems = refs[n_in + 2 * nc + n_out:]
        first = functools.reduce(jnp.logical_and, [pl.program_id(a) == 0 for a in range(len(grid))])
        last = functools.reduce(jnp.logical_and, [pl.program_id(a) == grid[a] - 1 for a in range(len(grid))])

        @pl.when(first)
        def _():
            ex.start(cins, couts, *sems)

        body(*ins, *outs)

        @pl.when(last)
        def _():
            ex.finish(cins, couts, *sems)

    outs = pl.pallas_call(
        full_body, name=name, grid=grid, out_shape=list(out_shape) + list(ex.out_shapes),
        in_specs=list(in_specs) + [ANY] * nc, out_specs=list(out_specs) + [ANY] * nc,
        input_output_aliases=aliases or {}, scratch_shapes=ex.scratch(),
        compiler_params=_cparams(("arbitrary",) * len(grid)))(*args, *ex.arrays)
    return outs[:n_out], outs[n_out:]


def _gather_exchange(shards):
    n = len(shards)

    def copies(ins, outs, send_sems, recv_sems):
        x, y, c = _my_place()

        def half(k, chip, core):
            h = ins[k].shape[0] // 2
            return outs[k].at[_chip(chip), pl.ds(core * h, h), :]

        def copy(k, slot, chip, core, to, src=None):
            return pltpu.make_async_remote_copy(
                src_ref=half(k, chip, core) if src is None else src, dst_ref=half(k, chip, core),
                send_sem=send_sems.at[6 * k + slot], recv_sem=recv_sems.at[6 * k + slot],
                device_id=to, device_id_type=MESH)

        return (x, y, c), copy

    def first_copies(ins, outs, send_sems, recv_sems):
        (x, y, c), copy = copies(ins, outs, send_sems, recv_sems)
        out = []
        for j, chip in enumerate(_other_chips(x, y)):
            for k in range(n):
                h = ins[k].shape[0] // 2
                out.append(copy(k, j, (x, y), c, (*chip, c), src=ins[k].at[pl.ds(c * h, h), :]))
        return out

    def start(ins, outs, send_sems, recv_sems):
        for cp in first_copies(ins, outs, send_sems, recv_sems):
            cp.start()

    def finish(ins, outs, send_sems, recv_sems):
        (x, y, c), copy = copies(ins, outs, send_sems, recv_sems)
        chips = _other_chips(x, y)
        passed = []
        for j, chip in enumerate(chips):
            for k in range(n):
                copy(k, j, chip, c, (x, y, c)).wait_recv()
                passed.append(copy(k, 3 + j, chip, c, (x, y, 1 - c)))
                passed[-1].start()
        for j, chip in enumerate(chips):
            for k in range(n):
                copy(k, 3 + j, chip, 1 - c, (x, y, c)).wait_recv()
        for cp in first_copies(ins, outs, send_sems, recv_sems) + passed:
            cp.wait_send()

    return _Exchange(list(shards), [jax.ShapeDtypeStruct((N_CHIPS,) + s.shape, s.dtype) for s in shards], 6 * n,
                     start, finish)


def _place_own(stacks, shards):
    me = 2 * lax.axis_index("x") + lax.axis_index("y")
    return [lax.dynamic_update_slice(others, mine[None], (me, 0, 0)) for others, mine in zip(stacks, shards)]


def _swap_halves(grads, name):
    n = len(grads)

    def body(*refs):
        ins, outs = refs[:n], refs[n:2 * n]
        send_sems, recv_sems = refs[2 * n:]
        x, y, c = _my_place()
        copies = []
        for k in range(n):
            h = ins[k].shape[1] // 2
            copies.append(pltpu.make_async_remote_copy(
                src_ref=ins[k].at[:, pl.ds((1 - c) * h, h), :], dst_ref=outs[k],
                send_sem=send_sems.at[k], recv_sem=recv_sems.at[k], device_id=(x, y, 1 - c), device_id_type=MESH))
            copies[-1].start()
        for cp in copies:
            cp.wait()

    return pl.pallas_call(
        body, name=name,
        out_shape=[jax.ShapeDtypeStruct((N_CHIPS, g.shape[1] // 2, g.shape[2]), g.dtype) for g in grads],
        in_specs=[ANY] * n, out_specs=[ANY] * n,
        scratch_shapes=[pltpu.SemaphoreType.DMA((n,)), pltpu.SemaphoreType.DMA((n,))],
    )(*grads)


def _chips_exchange(sums):
    n = len(sums)

    def sends(ins, outs, send_sems, recv_sems):
        x, y, c = _my_place()
        return [pltpu.make_async_remote_copy(
            src_ref=ins[k].at[_chip(chip)], dst_ref=outs[k].at[_chip((x, y))],
            send_sem=send_sems.at[3 * k + j], recv_sem=recv_sems.at[3 * k + j],
            device_id=(*chip, c), device_id_type=MESH)
            for j, chip in enumerate(_other_chips(x, y)) for k in range(n)]

    def start(ins, outs, send_sems, recv_sems):
        for cp in sends(ins, outs, send_sems, recv_sems):
            cp.start()

    def finish(ins, outs, send_sems, recv_sems):
        x, y, c = _my_place()
        for j, chip in enumerate(_other_chips(x, y)):
            for k in range(n):
                slot = outs[k].at[_chip(chip)]
                pltpu.make_async_remote_copy(
                    src_ref=slot, dst_ref=slot, send_sem=send_sems.at[3 * k + j], recv_sem=recv_sems.at[3 * k + j],
                    device_id=(x, y, c), device_id_type=MESH).wait_recv()
        for cp in sends(ins, outs, send_sems, recv_sems):
            cp.wait_send()

    return _Exchange(list(sums), [jax.ShapeDtypeStruct(s.shape, s.dtype) for s in sums], 3 * n, start, finish)


def _join_halves(shards):
    n = len(shards)

    def body(*refs):
        ins, outs = refs[:n], refs[n:2 * n]
        send_sems, recv_sems = refs[2 * n:]
        x, y, c = _my_place()
        sends = []
        for k in range(n):
            h = ins[k].shape[0] // 2
            sends.append(pltpu.make_async_remote_copy(
                src_ref=ins[k].at[pl.ds(c * h, h), :], dst_ref=outs[k].at[pl.ds(c * h, h), :],
                send_sem=send_sems.at[k], recv_sem=recv_sems.at[k], device_id=(x, y, 1 - c), device_id_type=MESH))
            sends[-1].start()
        for k in range(n):
            h = ins[k].shape[0] // 2
            theirs = outs[k].at[pl.ds((1 - c) * h, h), :]
            pltpu.make_async_remote_copy(
                src_ref=theirs, dst_ref=theirs, send_sem=send_sems.at[k], recv_sem=recv_sems.at[k],
                device_id=(x, y, c), device_id_type=MESH).wait_recv()
        for cp in sends:
            cp.wait_send()

    return pl.pallas_call(
        body, name="join_halves",
        out_shape=[jax.ShapeDtypeStruct(s.shape, s.dtype) for s in shards],
        in_specs=[ANY] * n, out_specs=[ANY] * n,
        input_output_aliases={k: k for k in range(n)},
        scratch_shapes=[pltpu.SemaphoreType.DMA((n,)), pltpu.SemaphoreType.DMA((n,))],
    )(*shards)


def _gather_small(block):
    m_per = block.shape[0]

    def body(x_ref, out_ref, send_sems, recv_sems, local_sem):
        x, y, c = _my_place()
        me, sibling = (x, y, c), (x, y, 1 - c)
        chips = _other_chips(x, y)

        def rows(px, py, pc):
            return out_ref.at[pl.ds((4 * px + 2 * py + pc) * m_per, m_per), :]

        def copy(k, blk, to, src=None):
            return pltpu.make_async_remote_copy(
                src_ref=rows(*blk) if src is None else src, dst_ref=rows(*blk),
                send_sem=send_sems.at[k], recv_sem=recv_sems.at[k], device_id=to, device_id_type=MESH)

        mine = pltpu.make_async_copy(x_ref, rows(*me), local_sem)
        mine.start()
        first = [copy(0, me, sibling, src=x_ref)]
        first += [copy(1 + j, me, (*chip, c), src=x_ref) for j, chip in enumerate(chips)]
        for cp in first:
            cp.start()
        passed = [copy(4 + j, (*chip, c), sibling) for j, chip in enumerate(chips)]
        for j, chip in enumerate(chips):
            copy(1 + j, (*chip, c), me).wait_recv()
            passed[j].start()
        copy(0, sibling, me).wait_recv()
        for j, chip in enumerate(chips):
            copy(4 + j, (*chip, 1 - c), me).wait_recv()
        for cp in first + passed:
            cp.wait_send()
        mine.wait()

    return pl.pallas_call(
        body, name="gather_small",
        out_shape=jax.ShapeDtypeStruct((N_DEV * m_per, LANES), block.dtype),
        in_specs=[pl.BlockSpec(memory_space=pltpu.VMEM)],
        out_specs=pl.BlockSpec(memory_space=pltpu.VMEM),
        scratch_shapes=[pltpu.SemaphoreType.DMA((7,)), pltpu.SemaphoreType.DMA((7,)), pltpu.SemaphoreType.DMA],
    )(block)


def _sum_halves(grads, theirs, core, name):
    _, h, cols = theirs.shape
    br = _rows_block(h, cols)
    nb = h // br

    def body(core_ref, a_ref, b_ref, o_ref):
        o_ref[...] = (a_ref[...] + b_ref[...]).astype(BF16)

    return pl.pallas_call(
        body, name=name,
        out_shape=jax.ShapeDtypeStruct(theirs.shape, BF16),
        grid_spec=pltpu.PrefetchScalarGridSpec(
            num_scalar_prefetch=1, grid=(N_CHIPS, nb),
            in_specs=[pl.BlockSpec((1, br, cols), lambda j, i, core_ref: (j, core_ref[0] * nb + i, 0)),
                      pl.BlockSpec((1, br, cols), lambda j, i, core_ref: (j, i, 0))],
            out_specs=pl.BlockSpec((1, br, cols), lambda j, i, core_ref: (j, i, 0))),
        compiler_params=_cparams(("parallel", "parallel")),
    )(core, grads, theirs)


def _sum_chips(slots, sums, place, name):
    _, h, cols = slots.shape
    br = _rows_block(h, cols)
    nb = h // br

    def body(place_ref, s_ref, own_ref, o_ref):
        me = place_ref[1]
        acc = None
        for k in range(N_CHIPS):
            term = jnp.where(me == k, own_ref[k], s_ref[k]).astype(F32)
            acc = term if acc is None else acc + term
        o_ref[...] = acc

    stack = pl.BlockSpec((N_CHIPS, br, cols), lambda i, place_ref: (0, i, 0))
    return pl.pallas_call(
        body, name=name,
        out_shape=jax.ShapeDtypeStruct((2 * h, cols), F32),
        grid_spec=pltpu.PrefetchScalarGridSpec(
            num_scalar_prefetch=1, grid=(nb,), in_specs=[stack, stack],
            out_specs=pl.BlockSpec((br, cols), lambda i, place_ref: (place_ref[0] * nb + i, 0))),
        compiler_params=_cparams(("parallel",)),
    )(place, slots, sums)


def _adamw_math(w, g, m, v):
    m = ADAM_B1 * m + (1.0 - ADAM_B1) * g
    v = ADAM_B2 * v + (1.0 - ADAM_B2) * (g * g)
    m_hat = m / (1.0 - ADAM_B1 ** ADAM_STEP)
    v_hat = v / (1.0 - ADAM_B2 ** ADAM_STEP)
    delta = -ADAM_LR * (m_hat / (jnp.sqrt(v_hat) + ADAM_EPS) + ADAM_WD * w)
    return delta, m, v


def _adamw(w, g, m, v, name):
    rows, cols = w.shape
    br = _rows_block(rows, cols, unit=8)

    def body(w_ref, g_ref, m_ref, v_ref, d_ref, nm_ref, nv_ref):
        d, nm, nv = _adamw_math(w_ref[...], g_ref[...], m_ref[...], v_ref[...])
        d_ref[...] = d
        nm_ref[...] = nm
        nv_ref[...] = nv

    spec = pl.BlockSpec((br, cols), lambda i: (i, 0))
    shape = jax.ShapeDtypeStruct(w.shape, F32)
    return pl.pallas_call(
        body, name=name, out_shape=(shape, shape, shape), grid=(rows // br,),
        in_specs=[spec] * 4, out_specs=(spec, spec, spec),
        compiler_params=_cparams(("parallel",)),
    )(w, g, m, v)


def _adamw_small(parts, w, m, v):
    def body(p_ref, w_ref, m_ref, v_ref, g_ref, d_ref, nm_ref, nv_ref):
        g = p_ref[0]
        for k in range(1, N_DEV):
            g = g + p_ref[k]
        d, nm, nv = _adamw_math(w_ref[...], g, m_ref[...], v_ref[...])
        g_ref[...] = g
        d_ref[...] = d
        nm_ref[...] = nm
        nv_ref[...] = nv

    shape = jax.ShapeDtypeStruct((SMALL_ROWS, LANES), F32)
    return pl.pallas_call(body, name="adamw_small", out_shape=(shape,) * 4, compiler_params=_cparams())(parts, w, m, v)


def _mm(a, b, *, name, ta=False, out_dtype=F32, res=None, bm=1024, bn=1024, bk=4096, b_stack=False, out_stack=False):
    if ta:
        kdim, m = a.shape
    else:
        m, kdim = a.shape
    if b_stack:
        _, kb, chunk = b.shape
        n = N_CHIPS * chunk
    else:
        kb, n = b.shape
        chunk = n // N_CHIPS if out_stack else n
    assert kdim == kb, (a.shape, b.shape, ta)
    bm = _block(m, bm, LANES if ta else 16)
    bn = _block(chunk, bn, LANES)
    bk = _block(kdim, bk, LANES)
    nk = kdim // bk
    per_chunk = chunk // bn
    dims = (((0 if ta else 1,), (0,)), ((), ()))

    def body(*refs):
        refs = list(refs)
        a_ref, b_ref = refs[:2]
        r_ref = refs[2] if res is not None else None
        o_ref = refs[3] if res is not None else refs[2]
        part = lax.dot_general(a_ref[...].astype(BF16), b_ref[...].astype(BF16), dims, preferred_element_type=F32)

        def finish(r):
            if r_ref is not None:
                r = r + r_ref[...]
            o_ref[...] = r.astype(out_dtype)

        if nk == 1:
            finish(part)
        else:
            acc_ref = refs[-1]
            k = pl.program_id(2)

            @pl.when(k == 0)
            def _():
                acc_ref[...] = part

            @pl.when(k > 0)
            def _():
                acc_ref[...] += part

            @pl.when(k == nk - 1)
            def _():
                finish(acc_ref[...])

    a_spec = pl.BlockSpec((bk, bm), lambda i, j, k: (k, i)) if ta else pl.BlockSpec((bm, bk), lambda i, j, k: (i, k))
    if b_stack:
        b_spec = pl.BlockSpec((None, bk, bn), lambda i, j, k: (j // per_chunk, k, j % per_chunk))
    else:
        b_spec = pl.BlockSpec((bk, bn), lambda i, j, k: (k, j))
    r_spec = pl.BlockSpec((bm, bn), lambda i, j, k: (i, j))
    if out_stack:
        o_spec = pl.BlockSpec((None, bm, bn), lambda i, j, k: (j // per_chunk, i, j % per_chunk))
        o_shape = (N_CHIPS, m, chunk)
    else:
        o_spec, o_shape = r_spec, (m, n)
    in_specs = [a_spec, b_spec] + ([r_spec] if res is not None else [])
    args = (a, b) + ((res,) if res is not None else ())
    return pl.pallas_call(
        body, name=name, out_shape=jax.ShapeDtypeStruct(o_shape, out_dtype),
        grid=(m // bm, n // bn, nk), in_specs=in_specs, out_specs=o_spec,
        scratch_shapes=[pltpu.VMEM((bm, bn), F32)] if nk > 1 else [],
        compiler_params=_cparams(("parallel", "parallel", "arbitrary")),
    )(*args)


def _rms_fwd(x, g, name):
    t, d = x.shape
    bt = _block(t, 512, 16)

    def body(x_ref, g_ref, h_ref):
        xv = x_ref[...]
        r = lax.rsqrt(jnp.mean(xv * xv, axis=-1, keepdims=True) + EPS)
        h_ref[...] = (xv * r * g_ref[...]).astype(BF16)

    return pl.pallas_call(
        body, name=name, out_shape=jax.ShapeDtypeStruct((t, d), BF16), grid=(t // bt,),
        in_specs=[pl.BlockSpec((bt, d), lambda i: (i, 0)), pl.BlockSpec((1, d), lambda i: (0, 0))],
        out_specs=pl.BlockSpec((bt, d), lambda i: (i, 0)),
        compiler_params=_cparams(("parallel",)),
    )(x, g)


def _rms_bwd(dh, x, g, dres, name):
    t, d = x.shape
    bt = _block(t, 256, 16)
    want_dx = dres is not None

    def body(*refs):
        if want_dx:
            dh_ref, x_ref, g_ref, dres_ref, dx_ref, dxb_ref, dg_ref = refs
        else:
            dh_ref, x_ref, g_ref, dg_ref = refs
        xv = x_ref[...]
        r = lax.rsqrt(jnp.mean(xv * xv, axis=-1, keepdims=True) + EPS)
        xhat = xv * r
        dhv = dh_ref[...]

        @pl.when(pl.program_id(0) == 0)
        def _():
            dg_ref[...] = jnp.zeros_like(dg_ref)

        dg_ref[...] += jnp.sum(dhv * xhat, axis=0, keepdims=True)
        if want_dx:
            dxhat = dhv * g_ref[...]
            dx = dres_ref[...] + r * (dxhat - xhat * jnp.mean(dxhat * xhat, axis=-1, keepdims=True))
            dx_ref[...] = dx
            dxb_ref[...] = dx.astype(BF16)

    row = pl.BlockSpec((bt, d), lambda i: (i, 0))
    vec = pl.BlockSpec((1, d), lambda i: (0, 0))
    if want_dx:
        return pl.pallas_call(
            body, name=name, grid=(t // bt,),
            out_shape=(jax.ShapeDtypeStruct((t, d), F32), jax.ShapeDtypeStruct((t, d), BF16),
                       jax.ShapeDtypeStruct((1, d), F32)),
            in_specs=[row, row, vec, row], out_specs=(row, row, vec),
            compiler_params=_cparams(("arbitrary",)),
        )(dh, x, g, dres)
    return pl.pallas_call(
        body, name=name, grid=(t // bt,), out_shape=jax.ShapeDtypeStruct((1, d), F32),
        in_specs=[row, row, vec], out_specs=vec,
        compiler_params=_cparams(("arbitrary",)),
    )(dh, x, g)


def _final_loss(x, target, g):
    t, d = x.shape
    bt = _block(t, 256, 16)

    def body(x_ref, t_ref, g_ref, dx_ref, dxb_ref, dg_ref, loss_ref):
        xv = x_ref[...]
        gv = g_ref[...]
        r = lax.rsqrt(jnp.mean(xv * xv, axis=-1, keepdims=True) + EPS)
        xhat = xv * r
        err = xhat * gv - t_ref[...]

        @pl.when(pl.program_id(0) == 0)
        def _():
            dg_ref[...] = jnp.zeros_like(dg_ref)
            loss_ref[...] = jnp.zeros_like(loss_ref)

        loss_ref[...] += 0.5 * jnp.sum(jnp.mean(err * err, axis=-1, keepdims=True), axis=0, keepdims=True)
        dy = err * (1.0 / d)
        dg_ref[...] += jnp.sum(dy * xhat, axis=0, keepdims=True)
        dxhat = dy * gv
        dx = r * (dxhat - xhat * jnp.mean(dxhat * xhat, axis=-1, keepdims=True))
        dx_ref[...] = dx
        dxb_ref[...] = dx.astype(BF16)

    row = pl.BlockSpec((bt, d), lambda i: (i, 0))
    vec = pl.BlockSpec((1, d), lambda i: (0, 0))
    return pl.pallas_call(
        body, name="final_loss", grid=(t // bt,),
        out_shape=(jax.ShapeDtypeStruct((t, d), F32), jax.ShapeDtypeStruct((t, d), BF16),
                   jax.ShapeDtypeStruct((1, d), F32), jax.ShapeDtypeStruct((1, LANES), F32)),
        in_specs=[row, row, vec], out_specs=(row, row, vec, pl.BlockSpec((1, LANES), lambda i: (0, 0))),
        compiler_params=_cparams(("arbitrary",)),
    )(x, target, g)


GU_COLS = GATE_WIDTH + POOL_WIDTH
U_BLK = GATE_WIDTH // POOL_WIDTH


def _shift_down(a, k, row):
    return jnp.where(row >= k, pltpu.roll(a, k, 0), 0.0)


def _shift_up(a, k, row):
    n = a.shape[0]
    return jnp.where(row < n - k, pltpu.roll(a, n - k, 0), 0.0)


def _window_delta(u, w, row):
    s, k = u, 1
    while k < w:
        s = s + _shift_down(s, k, row)
        k *= 2
    cnt = jnp.minimum(row + 1, w).astype(F32)
    return s / cnt - u, cnt


def _pool_fwd(gu, pool_w, pool_scale):
    b, s, _ = gu.shape

    def body(u_ref, pw_ref, sc_ref, y_ref):
        row = lax.broadcasted_iota(jnp.int32, (s, POOL_GC), 0)
        for g, w in enumerate(POOL_WINDOWS):
            cols = slice(g * POOL_GC, (g + 1) * POOL_GC)
            d, _ = _window_delta(u_ref[0, :, cols], w, row)
            z = jnp.dot(d.astype(BF16), pw_ref[g].astype(BF16), preferred_element_type=F32)
            y_ref[0, :, cols] = (z * sc_ref[:, cols]).astype(BF16)

    return pl.pallas_call(
        body, name="pool_fwd", out_shape=jax.ShapeDtypeStruct((b, s, POOL_WIDTH), BF16), grid=(b,),
        in_specs=[pl.BlockSpec((1, s, POOL_WIDTH), lambda i: (i, 0, U_BLK)),
                  pl.BlockSpec((4, POOL_GC, POOL_GC), lambda i: (0, 0, 0)),
                  pl.BlockSpec((1, POOL_WIDTH), lambda i: (0, 0))],
        out_specs=pl.BlockSpec((1, s, POOL_WIDTH), lambda i: (i, 0, 0)),
        compiler_params=_cparams(("parallel",)),
    )(gu, pool_w, pool_scale)


def _pool_bwd(gu, dy, pool_w, pool_scale, dgu):
    b, s, _ = gu.shape

    def body(u_ref, dy_ref, pw_ref, sc_ref, dgu_in, du_ref, dpw_ref, dsc_ref):
        del dgu_in

        @pl.when(pl.program_id(0) == 0)
        def _():
            dpw_ref[...] = jnp.zeros_like(dpw_ref)
            dsc_ref[...] = jnp.zeros_like(dsc_ref)

        row = lax.broadcasted_iota(jnp.int32, (s, POOL_GC), 0)
        for g, w in enumerate(POOL_WINDOWS):
            cols = slice(g * POOL_GC, (g + 1) * POOL_GC)
            d, cnt = _window_delta(u_ref[0, :, cols], w, row)
            db = d.astype(BF16)
            pw = pw_ref[g].astype(BF16)
            z = jnp.dot(db, pw, preferred_element_type=F32)
            dyv = dy_ref[0, :, cols]
            dsc_ref[:, cols] += jnp.sum(dyv * z, axis=0, keepdims=True)
            dz = (dyv * sc_ref[:, cols]).astype(BF16)
            dpw_ref[g] += lax.dot_general(db, dz, (((0,), (0,)), ((), ())), preferred_element_type=F32)
            dd = lax.dot_general(dz, pw, (((1,), (1,)), ((), ())), preferred_element_type=F32)
            acc, k = dd / cnt, 1
            while k < w:
                acc = acc + _shift_up(acc, k, row)
                k *= 2
            du_ref[0, :, cols] = (acc - dd).astype(BF16)

    return pl.pallas_call(
        body, name="pool_bwd", grid=(b,),
        out_shape=(jax.ShapeDtypeStruct((b, s, GU_COLS), BF16), jax.ShapeDtypeStruct((4, POOL_GC, POOL_GC), F32),
                   jax.ShapeDtypeStruct((1, POOL_WIDTH), F32)),
        in_specs=[pl.BlockSpec((1, s, POOL_WIDTH), lambda i: (i, 0, U_BLK)),
                  pl.BlockSpec((1, s, POOL_WIDTH), lambda i: (i, 0, 0)),
                  pl.BlockSpec((4, POOL_GC, POOL_GC), lambda i: (0, 0, 0)),
                  pl.BlockSpec((1, POOL_WIDTH), lambda i: (0, 0)), ANY],
        out_specs=(pl.BlockSpec((1, s, POOL_WIDTH), lambda i: (i, 0, U_BLK)),
                   pl.BlockSpec((4, POOL_GC, POOL_GC), lambda i: (0, 0, 0)),
                   pl.BlockSpec((1, POOL_WIDTH), lambda i: (0, 0))),
        input_output_aliases={4: 0},
        compiler_params=_cparams(("arbitrary",)),
    )(gu, dy, pool_w, pool_scale, dgu)


def _forget_cumsum(f, bias):
    b, s, c = f.shape

    def body(f_ref, b_ref, c_ref):
        row = lax.broadcasted_iota(jnp.int32, (s, LANES), 0)
        z = f_ref[0] + b_ref[...]
        acc = jnp.minimum(z, 0.0) - jnp.log(1.0 + jnp.exp(-jnp.abs(z)))
        k = 1
        while k < s:
            acc = acc + _shift_down(acc, k, row)
            k *= 2
        c_ref[0] = acc

    return pl.pallas_call(
        body, name="forget_cumsum", out_shape=jax.ShapeDtypeStruct((b, s, c), F32), grid=(b, c // LANES),
        in_specs=[pl.BlockSpec((1, s, LANES), lambda i, j: (i, 0, j)), pl.BlockSpec((1, LANES), lambda i, j: (0, j))],
        out_specs=pl.BlockSpec((1, s, LANES), lambda i, j: (i, 0, j)),
        compiler_params=_cparams(("parallel", "parallel")),
    )(f, bias)


def _forget_bwd(dc, f, bias):
    b, s, _ = f.shape

    def body(dc_ref, f_ref, b_ref, df_ref, db_ref):
        @pl.when(pl.program_id(0) == 0)
        def _():
            db_ref[...] = jnp.zeros_like(db_ref)

        row = lax.broadcasted_iota(jnp.int32, (s, LANES), 0)
        acc, k = dc_ref[0], 1
        while k < s:
            acc = acc + _shift_up(acc, k, row)
            k *= 2
        z = f_ref[0] + b_ref[...]
        df = acc / (1.0 + jnp.exp(z))
        db_ref[...] += jnp.sum(df, axis=0, keepdims=True)
        df_ref[0] = df.astype(BF16)

    blk = pl.BlockSpec((1, s, LANES), lambda i: (i, 0, 0))
    vec = pl.BlockSpec((1, LANES), lambda i: (0, 0))
    return pl.pallas_call(
        body, name="forget_bwd", grid=(b,),
        out_shape=(jax.ShapeDtypeStruct((b, s, LANES), BF16), jax.ShapeDtypeStruct((1, LANES), F32)),
        in_specs=[blk, blk, vec], out_specs=(blk, vec),
        compiler_params=_cparams(("arbitrary",)),
    )(dc, f, bias)


KV_BLK0 = 2
PAIRS = FOX_HEADS // 2
FOX_SCALE = FOX_DH ** -0.5
NT_DIMS = (((1,), (1,)), ((), ()))
TN_DIMS = (((0,), (0,)), ((), ()))


def _stack_heads(v):
    head = lax.broadcasted_iota(jnp.int32, v.shape, 1) // FOX_DH
    zero = jnp.zeros_like(v)
    return jnp.concatenate([jnp.where(head == 0, v, zero), jnp.where(head == 1, v, zero)], axis=0)


def _stack_cols(v):
    return jnp.concatenate([v[:, 0:1], v[:, FOX_DH:FOX_DH + 1]], axis=0)


def _unstack(t, blk):
    head = lax.broadcasted_iota(jnp.int32, (blk, LANES), 1) // FOX_DH
    return jnp.where(head == 0, t[:blk], t[blk:])


def _fox_scores(q_all, kblk, row_bias, cr_ref, kb, masked, blk):
    top = lax.broadcasted_iota(jnp.int32, (2 * blk, 1), 0) < blk
    s = lax.dot_general(q_all, kblk, NT_DIMS, preferred_element_type=F32)
    s = s + (row_bias - jnp.where(top, cr_ref[0, 0, kb], cr_ref[0, 1, kb]))
    if masked:
        r = lax.broadcasted_iota(jnp.int32, (2 * blk, blk), 0)
        keep = jnp.where(r >= blk, r - blk, r) >= lax.broadcasted_iota(jnp.int32, (2 * blk, blk), 1)
        s = jnp.where(keep, s, NEG_INF)
    return s


def _fox_fwd(qkv, c_exp, c_row, ex=None):
    b, s, _ = qkv.shape
    blk = min(ATT_BLOCK, s)
    nq = s // blk

    def body(q_ref, kv_ref, cc_ref, cr_ref, o_ref, ob_ref, lse_ref):
        qi = pl.program_id(2)
        q_all = _stack_heads(q_ref[0] * FOX_SCALE)
        cq = _stack_cols(cc_ref[0])

        def step(kb, carry, masked):
            m, l, acc = carry
            rows = pl.ds(pl.multiple_of(kb * blk, blk), blk)
            sc = _fox_scores(q_all, kv_ref[0, rows, :LANES], cq, cr_ref, kb, masked, blk)
            m_new = jnp.maximum(m, jnp.max(sc, axis=-1, keepdims=True))
            p = jnp.exp(sc - m_new)
            alpha = jnp.exp(m - m_new)
            l = alpha * l + jnp.sum(p, axis=-1, keepdims=True)
            acc = alpha * acc + jnp.dot(p.astype(BF16), kv_ref[0, rows, LANES:], preferred_element_type=F32)
            return m_new, l, acc

        init = (jnp.full((2 * blk, 1), NEG_INF, F32), jnp.zeros((2 * blk, 1), F32), jnp.zeros((2 * blk, LANES), F32))
        m, l, acc = step(qi, lax.fori_loop(0, qi, functools.partial(step, masked=False), init), True)
        o = _unstack(acc / l, blk)
        o_ref[0] = o
        ob_ref[0] = o.astype(BF16)
        lse_ref[0] = _unstack(jnp.broadcast_to(m + jnp.log(l), (2 * blk, LANES)), blk)

    tile = pl.BlockSpec((1, blk, LANES), lambda i, h, q: (i, q, h))
    kvspec = pl.BlockSpec((1, s, 2 * LANES), lambda i, h, q: (i, 0, KV_BLK0 + h))
    shape = jax.ShapeDtypeStruct((b, s, FOX_WIDTH), F32)
    return _hosted_call(
        body, ex, name="fox_fwd", out_shape=(shape, jax.ShapeDtypeStruct((b, s, FOX_WIDTH), BF16), shape),
        grid=(b, PAIRS, nq),
        in_specs=[tile, kvspec, tile, pl.BlockSpec((1, 2, nq, 1, blk), lambda i, h, q: (i, h, 0, 0, 0))],
        out_specs=(tile, tile, tile), args=(qkv, qkv, c_exp, c_row))


def _fox_bwd_q(qkv, c_exp, c_row, lse, o, do, ex=None):
    b, s, _ = qkv.shape
    blk = min(ATT_BLOCK, s)
    nq = s // blk

    def body(q_ref, kv_ref, cc_ref, cr_ref, lse_ref, o_ref, do_ref, dq_ref, dl_ref, dcq_ref):
        qi = pl.program_id(2)
        q_all = _stack_heads(q_ref[0] * FOX_SCALE)
        dov = do_ref[0]
        do_all = _stack_heads(dov.astype(BF16))
        delta = jnp.sum(_stack_heads(dov * o_ref[0]), axis=-1, keepdims=True)
        bias = _stack_cols(cc_ref[0]) - _stack_cols(lse_ref[0])

        def step(kb, carry, masked):
            acc, dcq = carry
            rows = pl.ds(pl.multiple_of(kb * blk, blk), blk)
            kblk = kv_ref[0, rows, :LANES]
            p = jnp.exp(_fox_scores(q_all, kblk, bias, cr_ref, kb, masked, blk))
            dp = lax.dot_general(do_all, kv_ref[0, rows, LANES:], NT_DIMS, preferred_element_type=F32)
            ds = p * (dp - delta)
            acc = acc + jnp.dot(ds.astype(BF16), kblk, preferred_element_type=F32)
            return acc, dcq + jnp.sum(ds, axis=-1, keepdims=True)

        init = (jnp.zeros((2 * blk, LANES), F32), jnp.zeros((2 * blk, 1), F32))
        acc, dcq = step(qi, lax.fori_loop(0, qi, functools.partial(step, masked=False), init), True)
        dq_ref[0] = (_unstack(acc, blk) * FOX_SCALE).astype(BF16)
        dl_ref[0] = _unstack(jnp.broadcast_to(delta, (2 * blk, LANES)), blk)
        dcq_ref[0] = _unstack(jnp.broadcast_to(dcq, (2 * blk, LANES)), blk)

    tile = pl.BlockSpec((1, blk, LANES), lambda i, h, q: (i, q, h))
    kvspec = pl.BlockSpec((1, s, 2 * LANES), lambda i, h, q: (i, 0, KV_BLK0 + h))
    shape = jax.ShapeDtypeStruct((b, s, FOX_WIDTH), F32)
    return _hosted_call(
        body, ex, name="fox_bwd_q", grid=(b, PAIRS, nq),
        out_shape=(jax.ShapeDtypeStruct(qkv.shape, BF16), shape, shape),
        in_specs=[tile, kvspec, tile, pl.BlockSpec((1, 2, nq, 1, blk), lambda i, h, q: (i, h, 0, 0, 0)),
                  tile, tile, tile],
        out_specs=(tile, tile, tile), args=(qkv, qkv, c_exp, c_row, lse, o, do))


def _fox_bwd_kv(qkv, c_exp, c_row, lse, delta, do, dqkv, ex=None):
    b, s, _ = qkv.shape
    blk = min(ATT_BLOCK, s)
    nq = s // blk

    def body(q_ref, kv_ref, cc_ref, cr_ref, lse_ref, dl_ref, do_ref, dqkv_in, dkv_ref, dc_ref):
        del dqkv_in
        ki = pl.program_id(2)
        kblk = kv_ref[0, :, :LANES]
        vblk = kv_ref[0, :, LANES:]

        def step(qb, carry, masked):
            dk, dv, dc = carry
            rows = pl.ds(pl.multiple_of(qb * blk, blk), blk)
            q_all = _stack_heads(q_ref[0, rows, :] * FOX_SCALE)
            do_all = _stack_heads(do_ref[0, rows, :].astype(BF16))
            bias = _stack_cols(cc_ref[0, rows, :]) - _stack_cols(lse_ref[0, rows, :])
            delta = _stack_cols(dl_ref[0, rows, :])
            p = jnp.exp(_fox_scores(q_all, kblk, bias, cr_ref, 0, masked, blk))
            dv = dv + lax.dot_general(p.astype(BF16), do_all, TN_DIMS, preferred_element_type=F32)
            dp = lax.dot_general(do_all, vblk, NT_DIMS, preferred_element_type=F32)
            ds = p * (dp - delta)
            dk = dk + lax.dot_general(ds.astype(BF16), q_all, TN_DIMS, preferred_element_type=F32)
            col = jnp.concatenate([jnp.sum(ds[:blk], axis=0, keepdims=True), jnp.sum(ds[blk:], axis=0, keepdims=True)],
                                  axis=0)
            return dk, dv, dc - col

        zero = jnp.zeros((blk, LANES), F32)
        carry = step(ki, (zero, zero, jnp.zeros((2, blk), F32)), True)
        dk, dv, dc = lax.fori_loop(ki + 1, nq, functools.partial(step, masked=False), carry)
        dkv_ref[0, :, :LANES] = dk.astype(BF16)
        dkv_ref[0, :, LANES:] = dv.astype(BF16)
        dc_ref[0, 0, 0] = dc[0:1]
        dc_ref[0, 1, 0] = dc[1:2]

    full = pl.BlockSpec((1, s, LANES), lambda i, h, k: (i, 0, h))
    kvtile = pl.BlockSpec((1, blk, 2 * LANES), lambda i, h, k: (i, k, KV_BLK0 + h))
    crow = pl.BlockSpec((1, 2, 1, 1, blk), lambda i, h, k: (i, h, k, 0, 0))
    return _hosted_call(
        body, ex, name="fox_bwd_kv", grid=(b, PAIRS, nq),
        out_shape=(jax.ShapeDtypeStruct(qkv.shape, BF16), jax.ShapeDtypeStruct(c_row.shape, F32)),
        in_specs=[full, kvtile, full, crow, full, full, full, ANY],
        out_specs=(kvtile, crow), aliases={7: 0}, args=(qkv, qkv, c_exp, c_row, lse, delta, do, dqkv))


def _sigmoid(z):
    return 1.0 / (1.0 + jnp.exp(-z))


def _mix_fwd(gu, b_gate, y_pool, y_fox):
    t = gu.shape[0]
    bt = _block(t, 256, 16)

    def body(gp_ref, gf_ref, bp_ref, bf_ref, yp_ref, yf_ref, o_ref):
        gp = _sigmoid(gp_ref[...] + bp_ref[...])
        gf = _sigmoid(gf_ref[...] + bf_ref[...])
        o_ref[...] = (gp * yp_ref[...] + gf * yf_ref[...]).astype(BF16)

    col = lambda j: pl.BlockSpec((bt, D_MODEL), lambda i: (i, j))
    vec = lambda j: pl.BlockSpec((1, D_MODEL), lambda i: (0, j))
    return pl.pallas_call(
        body, name="mix_fwd", out_shape=jax.ShapeDtypeStruct((t, D_MODEL), BF16), grid=(t // bt,),
        in_specs=[col(0), col(1), vec(0), vec(1), col(0), col(0)], out_specs=col(0),
        compiler_params=_cparams(("parallel",)),
    )(gu, gu, b_gate, b_gate, y_pool, y_fox)


def _mix_bwd(gu, b_gate, y_pool, y_fox, dmix):
    t = gu.shape[0]
    bt = _block(t, 256, 16)

    def body(gp_ref, gf_ref, bp_ref, bf_ref, yp_ref, yf_ref, dm_ref, dyp_ref, dyf_ref, dgl_ref, db_ref):
        @pl.when(pl.program_id(0) == 0)
        def _():
            db_ref[...] = jnp.zeros_like(db_ref)

        dm = dm_ref[...]
        gp = _sigmoid(gp_ref[...] + bp_ref[...])
        gf = _sigmoid(gf_ref[...] + bf_ref[...])
        dyp_ref[...] = (dm * gp).astype(BF16)
        dyf_ref[...] = (dm * gf).astype(BF16)
        dlp = dm * yp_ref[...] * gp * (1.0 - gp)
        dlf = dm * yf_ref[...] * gf * (1.0 - gf)
        dgl_ref[:, :D_MODEL] = dlp.astype(BF16)
        dgl_ref[:, D_MODEL:] = dlf.astype(BF16)
        db_ref[:, :D_MODEL] += jnp.sum(dlp, axis=0, keepdims=True)
        db_ref[:, D_MODEL:] += jnp.sum(dlf, axis=0, keepdims=True)

    col = lambda j: pl.BlockSpec((bt, D_MODEL), lambda i: (i, j))
    vec = lambda j: pl.BlockSpec((1, D_MODEL), lambda i: (0, j))
    wide = pl.BlockSpec((bt, GATE_WIDTH), lambda i: (i, 0))
    return pl.pallas_call(
        body, name="mix_bwd", grid=(t // bt,),
        out_shape=(jax.ShapeDtypeStruct((t, D_MODEL), BF16), jax.ShapeDtypeStruct((t, D_MODEL), BF16),
                   jax.ShapeDtypeStruct((t, GU_COLS), BF16), jax.ShapeDtypeStruct((1, GATE_WIDTH), F32)),
        in_specs=[col(0), col(1), vec(0), vec(1), col(0), col(0), col(0)],
        out_specs=(col(0), col(0), wide, pl.BlockSpec((1, GATE_WIDTH), lambda i: (0, 0))),
        compiler_params=_cparams(("arbitrary",)),
    )(gu, gu, b_gate, b_gate, y_pool, y_fox, dmix)


X_SCALE = X_DH ** -0.5


def _xattn_probs(qh, kh):
    s = lax.dot_general(qh, kh, NT_DIMS, preferred_element_type=F32) * X_SCALE
    e = jnp.exp(s - jnp.max(s, axis=-1, keepdims=True))
    return e / jnp.sum(e, axis=-1, keepdims=True)


def _xattn_fwd(q, kv):
    b, s, _ = q.shape
    m = kv.shape[1]
    bq = _block(s, 512, 16)

    def body(q_ref, kv_ref, o_ref):
        for h in range(X_HEADS):
            cols = slice(h * X_DH, (h + 1) * X_DH)
            p = _xattn_probs(q_ref[0, :, cols], kv_ref[0, :, cols])
            vh = kv_ref[0, :, X_WIDTH + h * X_DH:X_WIDTH + (h + 1) * X_DH]
            o_ref[0, :, cols] = jnp.dot(p.astype(BF16), vh, preferred_element_type=F32).astype(BF16)

    return pl.pallas_call(
        body, name="xattn_fwd", out_shape=jax.ShapeDtypeStruct((b, s, X_WIDTH), BF16), grid=(b, s // bq),
        in_specs=[pl.BlockSpec((1, bq, X_WIDTH), lambda i, j: (i, j, 0)),
                  pl.BlockSpec((1, m, 2 * X_WIDTH), lambda i, j: (i, 0, 0))],
        out_specs=pl.BlockSpec((1, bq, X_WIDTH), lambda i, j: (i, j, 0)),
        compiler_params=_cparams(("parallel", "parallel")),
    )(q, kv)


def _xattn_bwd(q, kv, do):
    b, s, _ = q.shape
    m = kv.shape[1]
    bq = _block(s, 512, 16)

    def body(q_ref, kv_ref, do_ref, dq_ref, dkv_ref):
        @pl.when(pl.program_id(1) == 0)
        def _():
            dkv_ref[...] = jnp.zeros_like(dkv_ref)

        for h in range(X_HEADS):
            cols = slice(h * X_DH, (h + 1) * X_DH)
            vcols = slice(X_WIDTH + h * X_DH, X_WIDTH + (h + 1) * X_DH)
            qh, kh, vh, doh = q_ref[0, :, cols], kv_ref[0, :, cols], kv_ref[0, :, vcols], do_ref[0, :, cols]
            p = _xattn_probs(qh, kh)
            dkv_ref[0, :, vcols] += lax.dot_general(p.astype(BF16), doh, TN_DIMS, preferred_element_type=F32)
            dp = lax.dot_general(doh, vh, NT_DIMS, preferred_element_type=F32)
            ds = (p * (dp - jnp.sum(p * dp, axis=-1, keepdims=True)) * X_SCALE).astype(BF16)
            dq_ref[0, :, cols] = jnp.dot(ds, kh, preferred_element_type=F32).astype(BF16)
            dkv_ref[0, :, cols] += lax.dot_general(ds, qh, TN_DIMS, preferred_element_type=F32)

    tile = pl.BlockSpec((1, bq, X_WIDTH), lambda i, j: (i, j, 0))
    mem = pl.BlockSpec((1, m, 2 * X_WIDTH), lambda i, j: (i, 0, 0))
    return pl.pallas_call(
        body, name="xattn_bwd", grid=(b, s // bq),
        out_shape=(jax.ShapeDtypeStruct((b, s, X_WIDTH), BF16), jax.ShapeDtypeStruct((b, m, 2 * X_WIDTH), F32)),
        in_specs=[tile, mem, tile], out_specs=(tile, mem),
        compiler_params=_cparams(("parallel", "arbitrary")),
    )(q, kv, do)


def _swiglu_fwd(gu):
    t = gu.shape[0]
    bt = _block(t, 256, 16)

    def body(gt_ref, up_ref, o_ref):
        gt = gt_ref[...]
        o_ref[...] = (gt * _sigmoid(gt) * up_ref[...]).astype(BF16)

    col = lambda j: pl.BlockSpec((bt, D_FF), lambda i: (i, j))
    return pl.pallas_call(
        body, name="swiglu_fwd", out_shape=jax.ShapeDtypeStruct((t, D_FF), BF16), grid=(t // bt,),
        in_specs=[col(0), col(1)], out_specs=col(0),
        compiler_params=_cparams(("parallel",)),
    )(gu, gu)


def _swiglu_bwd(gu, dact):
    t = gu.shape[0]
    bt = _block(t, 256, 16)

    def body(gt_ref, up_ref, da_ref, o_ref):
        gt = gt_ref[...]
        da = da_ref[...]
        sg = _sigmoid(gt)
        silu = gt * sg
        o_ref[:, :D_FF] = (da * up_ref[...] * (sg + silu * (1.0 - sg))).astype(BF16)
        o_ref[:, D_FF:] = (da * silu).astype(BF16)

    col = lambda j: pl.BlockSpec((bt, D_FF), lambda i: (i, j))
    return pl.pallas_call(
        body, name="swiglu_bwd", out_shape=jax.ShapeDtypeStruct((t, 2 * D_FF), BF16), grid=(t // bt,),
        in_specs=[col(0), col(1), col(0)], out_specs=pl.BlockSpec((bt, 2 * D_FF), lambda i: (i, 0)),
        compiler_params=_cparams(("parallel",)),
    )(gu, gu, dact)


def _stack_of(w, axis):
    r, c = w.shape
    if axis == 0:
        return w.reshape(N_CHIPS, r // N_CHIPS, c)
    return w.reshape(r, N_CHIPS, c // N_CHIPS).transpose(1, 0, 2)


def _unstack_cols(w3):
    n, r, c = w3.shape
    return w3.transpose(1, 0, 2).reshape(r, n * c)


def _stack_t(w3):
    n, r, c = w3.shape
    return w3.transpose(0, 2, 1).reshape(n * c, r)


def _pair_kv(k, v):
    r = k.shape[0]
    return jnp.stack([k.reshape(r, PAIRS, LANES), v.reshape(r, PAIRS, LANES)], axis=2).reshape(r, 2 * FOX_WIDTH)


def _unpair_kv(kv):
    r = kv.shape[0]
    kv = kv.reshape(r, PAIRS, 2, LANES)
    return kv[:, :, 0, :].reshape(r, FOX_WIDTH), kv[:, :, 1, :].reshape(r, FOX_WIDTH)


def _step(x, mem, loss_target, weights, moments_m, moments_v):
    nb, s, d = x.shape
    n_mem = mem.shape[1]
    t = nb * s
    blk = min(ATT_BLOCK, s)
    x2 = x.reshape(t, d)
    mem2 = mem.reshape(nb * n_mem, d)
    tgt2 = loss_target.reshape(t, d)
    local = {n: weights[n].reshape(weights[n].shape[1:]) for n, _, _ in SHARDED}

    names = [n for n, _, _ in SHARDED]
    later = [n for n in names if n != "w_in"]
    local_b = {n: local[n].astype(BF16) for n in names}
    w_in_stack, = _place_own(_run_exchange(_gather_exchange([local_b["w_in"]]), "gather_w_in"), [local_b["w_in"]])
    w_in = _unstack_cols(w_in_stack)
    w_gu = jnp.concatenate([w_in[:, 2056:], w_in[:, :512]], axis=1)
    w_qkv = jnp.concatenate([w_in[:, 512:1024], _pair_kv(w_in[:, 1024:1536], w_in[:, 1536:2048])], axis=1)
    w_f = jnp.pad(w_in[:, 2048:2056], ((0, 0), (0, LANES - FOX_HEADS)))

    g_mix, g_x, g_mem, g_ffn = (weights[n] for n in ("norm_mix_g", "norm_x_g", "norm_mem_g", "norm_ffn_g"))
    g_final = weights["norm_final_g"].reshape(1, d)
    pool_w = weights["pool_w"].reshape(4, POOL_GC, POOL_GC)
    pool_scale, b_gate = weights["pool_scale"], weights["b_gate"]
    b_f_pad = jnp.pad(weights["b_forget"], ((0, 0), (0, LANES - FOX_HEADS)))
    b_f_exp = jnp.repeat(weights["b_forget"], FOX_DH, axis=1)

    h = _rms_fwd(x2, g_mix, "norm_mix")
    gu = _mm(h, w_gu, bn=512, name="in_proj_gates_pool")
    qkv = _mm(h, w_qkv, out_dtype=BF16, bn=512, name="in_proj_qkv")
    f_pad = _mm(h, w_f, name="in_proj_forget")
    gu3, qkv3 = gu.reshape(nb, s, GU_COLS), qkv.reshape(nb, s, 3 * FOX_WIDTH)
    y = _pool_fwd(gu3, pool_w, pool_scale)
    f_exp = jnp.repeat(f_pad[:, :FOX_HEADS], FOX_DH, axis=1).reshape(nb, s, FOX_WIDTH)
    c_exp = _forget_cumsum(f_exp, b_f_exp)
    c_row = c_exp[:, :, ::FOX_DH].transpose(0, 2, 1).reshape(nb, FOX_HEADS, s // blk, 1, blk)
    (o, o_b, lse), gathered = _fox_fwd(qkv3, c_exp, c_row, ex=_gather_exchange([local_b[n] for n in later]))
    stacks = dict(zip(later, _place_own(gathered, [local_b[n] for n in later])))
    w_pool_out3, w_fox_out3, w_xo3, w_ffn_in3 = (stacks[n] for n in ("w_pool_out", "w_fox_out", "w_xo", "w_ffn_in"))
    w_out, w_xq, w_xkv, w_ffn_out = (stacks[n].reshape(-1, stacks[n].shape[2])
                                     for n in ("w_out", "w_xq", "w_xkv", "w_ffn_out"))
    y2, o2 = y.reshape(t, POOL_WIDTH), o_b.reshape(t, FOX_WIDTH)
    y_pool = _mm(y2, w_pool_out3, b_stack=True, name="pool_out")
    y_fox = _mm(o2, w_fox_out3, b_stack=True, name="fox_out")
    mix = _mix_fwd(gu, b_gate, y_pool, y_fox)
    x1 = _mm(mix, w_out, res=x2, name="mix_out")
    hx = _rms_fwd(x1, g_x, "norm_x")
    mem_n = _rms_fwd(mem2, g_mem, "norm_mem")
    qx = _mm(hx, w_xq, out_dtype=BF16, name="x_q")
    kv = _mm(mem_n, w_xkv, out_dtype=BF16, name="x_kv")
    qx3, kv3 = qx.reshape(nb, s, X_WIDTH), kv.reshape(nb, n_mem, 2 * X_WIDTH)
    ox = _xattn_fwd(qx3, kv3).reshape(t, X_WIDTH)
    x2_ = _mm(ox, w_xo3, b_stack=True, res=x1, name="x_out")
    hf = _rms_fwd(x2_, g_ffn, "norm_ffn")
    ffn = _mm(hf, w_ffn_in3, b_stack=True, bm=512, bn=1408, name="ffn_in")
    act = _swiglu_fwd(ffn)
    x3 = _mm(act, w_ffn_out, res=x2_, name="ffn_out")

    dx3, dx3_b, dg_final, loss_part = _final_loss(x3, tgt2, g_final)
    dw_ffn_out = _mm(act, dx3_b, ta=True, bm=1408, bn=512, bk=2048, name="d_w_ffn_out")
    dact = _mm(dx3_b, w_ffn_out.T, bn=1408, name="d_act")
    dffn = _swiglu_bwd(ffn, dact)
    dw_ffn_in = _mm(hf, dffn, ta=True, bm=512, bn=1408, bk=2048, out_stack=True, name="d_w_ffn_in")
    dhf = _mm(dffn, _stack_t(w_ffn_in3), bk=2816, name="d_hf")
    dx2, dx2_b, dg_ffn = _rms_bwd(dhf, x2_, g_ffn, dx3, "norm_ffn_bwd")

    dw_xo = _mm(ox, dx2_b, ta=True, bn=256, out_stack=True, name="d_w_xo")
    dox = _mm(dx2_b, _stack_t(w_xo3), out_dtype=BF16, name="d_ox").reshape(nb, s, X_WIDTH)
    dqx, dkv = _xattn_bwd(qx3, kv3, dox)
    dqx2, dkv2 = dqx.reshape(t, X_WIDTH), dkv.reshape(nb * n_mem, 2 * X_WIDTH)
    dw_xkv = _mm(mem_n, dkv2, ta=True, name="d_w_xkv")
    dmem_n = _mm(dkv2, w_xkv.T, name="d_mem_n")
    dg_mem = _rms_bwd(dmem_n, mem2, g_mem, None, "norm_mem_bwd")
    dw_xq = _mm(hx, dqx2, ta=True, name="d_w_xq")
    dhx = _mm(dqx2, w_xq.T, name="d_hx")
    dx1, dx1_b, dg_x = _rms_bwd(dhx, x1, g_x, dx2, "norm_x_bwd")

    dw_out = _mm(mix, dx1_b, ta=True, name="d_w_out")
    dmix = _mm(dx1_b, w_out.T, name="d_mix")
    dyp, dyf, dgu, db_gate = _mix_bwd(gu, b_gate, y_pool, y_fox, dmix)
    dw_pool_out = _mm(y2, dyp, ta=True, bn=256, out_stack=True, name="d_w_pool_out")
    dw_fox_out = _mm(o2, dyf, ta=True, bn=256, out_stack=True, name="d_w_fox_out")
    dy = _mm(dyp, _stack_t(w_pool_out3), name="d_y").reshape(nb, s, POOL_WIDTH)
    do = _mm(dyf, _stack_t(w_fox_out3), name="d_o").reshape(nb, s, FOX_WIDTH)
    dgu3, dpool_w, dpool_scale = _pool_bwd(gu3, dy, pool_w, pool_scale, dgu.reshape(nb, s, GU_COLS))
    core = lax.axis_index("c").astype(jnp.int32).reshape(1)
    grad_stacks = {"w_pool_out": dw_pool_out, "w_fox_out": dw_fox_out, "w_out": _stack_of(dw_out, 0),
                   "w_xq": _stack_of(dw_xq, 0), "w_xkv": _stack_of(dw_xkv, 0), "w_xo": dw_xo,
                   "w_ffn_in": dw_ffn_in, "w_ffn_out": _stack_of(dw_ffn_out, 0)}

    def chip_sums_of(group):
        theirs = _swap_halves([grad_stacks[n] for n in group], "swap_halves_" + group[0])
        return [_sum_halves(grad_stacks[n], t_, core, "sum_halves_" + n) for n, t_ in zip(group, theirs)]

    ffn_group = ["w_ffn_in", "w_ffn_out"]
    mid_group = ["w_pool_out", "w_fox_out", "w_out", "w_xq", "w_xkv", "w_xo"]
    chip_sums = dict(zip(ffn_group, chip_sums_of(ffn_group)))
    chip_sums.update(zip(mid_group, chip_sums_of(mid_group)))
    (dqkv3, delta, dc_q), ffn_slots = _fox_bwd_q(qkv3, c_exp, c_row, lse, o, do,
                                                 ex=_chips_exchange([chip_sums[n] for n in ffn_group]))
    (dqkv3, dc_row), mid_slots = _fox_bwd_kv(qkv3, c_exp, c_row, lse, delta, do, dqkv3,
                                             ex=_chips_exchange([chip_sums[n] for n in mid_group]))
    slots = dict(zip(ffn_group + mid_group, list(ffn_slots) + list(mid_slots)))
    dc = dc_row.reshape(nb, FOX_HEADS, s).transpose(0, 2, 1) + dc_q[:, :, ::FOX_DH]
    dc = jnp.pad(dc, ((0, 0), (0, 0), (0, LANES - FOX_HEADS)))
    df, db_f = _forget_bwd(dc, f_pad.reshape(nb, s, LANES), b_f_pad)
    dgu2, dqkv2, df2 = dgu3.reshape(t, GU_COLS), dqkv3.reshape(t, 3 * FOX_WIDTH), df.reshape(t, LANES)
    dw_gu = _mm(h, dgu2, ta=True, bn=512, name="d_w_gates_pool")
    dw_qkv = _mm(h, dqkv2, ta=True, bn=512, name="d_w_qkv")
    dw_f = _mm(h, df2, ta=True, name="d_w_forget")
    dh = _mm(df2, w_f.T, name="d_h_forget")
    dh = _mm(dqkv2, w_qkv.T, res=dh, name="d_h_qkv")
    dh = _mm(dgu2, w_gu.T, res=dh, name="d_h_gates_pool")
    dx, _, dg_mix = _rms_bwd(dh, x2, g_mix, dx1, "norm_mix_bwd")
    dw_k, dw_v = _unpair_kv(dw_qkv[:, FOX_WIDTH:])
    dw_in = jnp.concatenate([dw_gu[:, GATE_WIDTH:], dw_qkv[:, :FOX_WIDTH], dw_k, dw_v, dw_f[:, :FOX_HEADS],
                             dw_gu[:, :GATE_WIDTH]], axis=1)

    grad_stacks["w_in"] = _stack_of(dw_in, 1)
    chip_sums["w_in"], = chip_sums_of(["w_in"])
    slots["w_in"], = _run_exchange(_chips_exchange([chip_sums["w_in"]]), "exchange_w_in")
    place = jnp.stack([lax.axis_index("c"), 2 * lax.axis_index("x") + lax.axis_index("y")]).astype(jnp.int32)
    reduced = _join_halves([_sum_chips(slots[n], chip_sums[n], place, "sum_chips_" + n) for n in names])

    small_grads = {"norm_mix_g": dg_mix, "b_forget": db_f[:, :FOX_HEADS], "b_gate": db_gate, "pool_w": dpool_w,
                   "pool_scale": dpool_scale, "norm_x_g": dg_x, "norm_mem_g": dg_mem, "norm_ffn_g": dg_ffn,
                   "norm_final_g": dg_final}
    parts = _gather_small(_pack_small([small_grads[n] for n, _ in SMALL], last=loss_part[0, 0]))
    sg, sd, sm, sv = _adamw_small(parts.reshape(N_DEV, SMALL_ROWS, LANES),
                                  _pack_small([weights[n] for n, _ in SMALL]),
                                  _pack_small([moments_m[n] for n, _ in SMALL]),
                                  _pack_small([moments_v[n] for n, _ in SMALL]))
    loss = sg.reshape(-1)[LOSS_POS]

    grads, deltas, new_m, new_v = {}, {}, {}, {}
    for (n, _), g_, d_, m_, v_ in zip(SMALL, _unpack_small(sg), _unpack_small(sd), _unpack_small(sm), _unpack_small(sv)):
        grads[n], deltas[n], new_m[n], new_v[n] = g_, d_, m_, v_
    for n, g_ in zip(names, reduced):
        shp = weights[n].shape
        d_, m_, v_ = _adamw(local[n], g_, moments_m[n].reshape(g_.shape), moments_v[n].reshape(g_.shape), "adamw_" + n)
        grads[n], deltas[n], new_m[n], new_v[n] = (a.reshape(shp) for a in (g_, d_, m_, v_))
    return loss, dx.reshape(nb, s, d), grads, deltas, new_m, new_v


def kernel(x, mem, norm_mix_g, w_in, b_forget, b_gate, pool_w, pool_scale, w_pool_out, w_fox_out, w_out, norm_x_g, norm_mem_g, w_xq, w_xkv, w_xo, norm_ffn_g, w_ffn_in, w_ffn_out, norm_final_g, loss_target, m_norm_mix_g, m_w_in, m_b_forget, m_b_gate, m_pool_w, m_pool_scale, m_w_pool_out, m_w_fox_out, m_w_out, m_norm_x_g, m_norm_mem_g, m_w_xq, m_w_xkv, m_w_xo, m_norm_ffn_g, m_w_ffn_in, m_w_ffn_out, m_norm_final_g, v_norm_mix_g, v_w_in, v_b_forget, v_b_gate, v_pool_w, v_pool_scale, v_w_pool_out, v_w_fox_out, v_w_out, v_norm_x_g, v_norm_mem_g, v_w_xq, v_w_xkv, v_w_xo, v_norm_ffn_g, v_w_ffn_in, v_w_ffn_out, v_norm_final_g):
    given = dict(locals())
    weights = {n: given[n] for n in WEIGHT_ORDER}
    moments_m = {n: given["m_" + n] for n in WEIGHT_ORDER}
    moments_v = {n: given["v_" + n] for n in WEIGHT_ORDER}
    loss, grad_x, grads, deltas, new_m, new_v = _step(x, mem, loss_target, weights, moments_m, moments_v)
    return (loss, grad_x, *[grads[n] for n in WEIGHT_ORDER], *[deltas[n] for n in WEIGHT_ORDER],
            *[new_m[n] for n in WEIGHT_ORDER], *[new_v[n] for n in WEIGHT_ORDER])
```

```python
import functools
import math

import jax
import jax.numpy as jnp
from jax import lax
from jax.experimental import pallas as pl
from jax.experimental.pallas import tpu as pltpu

F32 = jnp.float32
BF16 = jnp.bfloat16
MESH = pl.DeviceIdType.MESH

D_MODEL = 1024
EPS = 1e-6
POOL_WINDOWS = (2, 4, 8, 16)
POOL_WIDTH = 512
POOL_GC = 128
FOX_HEADS = 8
FOX_DH = 64
FOX_WIDTH = 512
X_HEADS = 4
X_DH = 128
X_WIDTH = 512
D_FF = 2816
IN_COLS = 4104
GATE_WIDTH = 2048
ADAM_LR = 0.001
ADAM_B1 = 0.9
ADAM_B2 = 0.999
ADAM_EPS = 1e-08
ADAM_WD = 0.01
ADAM_STEP = 10

N_CHIPS = 4
N_DEV = 8
LANES = 128
VMEM_LIMIT_BYTES = 56 * 1024 * 1024
NEG_INF = -1e30
ATT_BLOCK = 512

SHARDED = (
    ("w_in", (1024, IN_COLS), 1),
    ("w_pool_out", (POOL_WIDTH, 1024), 1),
    ("w_fox_out", (FOX_WIDTH, 1024), 1),
    ("w_out", (1024, 1024), 0),
    ("w_xq", (1024, X_WIDTH), 0),
    ("w_xkv", (1024, 2 * X_WIDTH), 0),
    ("w_xo", (X_WIDTH, 1024), 1),
    ("w_ffn_in", (1024, 2 * D_FF), 1),
    ("w_ffn_out", (D_FF, 1024), 0),
)
SMALL = (
    ("norm_mix_g", (1, 1024)),
    ("b_forget", (1, 8)),
    ("b_gate", (1, 2048)),
    ("pool_w", (1, 4, 128, 128)),
    ("pool_scale", (1, 512)),
    ("norm_x_g", (1, 1024)),
    ("norm_mem_g", (1, 1024)),
    ("norm_ffn_g", (1, 1024)),
    ("norm_final_g", (1024,)),
)
WEIGHT_ORDER = ("norm_mix_g", "w_in", "b_forget", "b_gate", "pool_w", "pool_scale", "w_pool_out", "w_fox_out", "w_out",
                "norm_x_g", "norm_mem_g", "w_xq", "w_xkv", "w_xo", "norm_ffn_g", "w_ffn_in", "w_ffn_out", "norm_final_g")


def _round_up(n, m):
    return (n + m - 1) // m * m


SMALL_ELEMS = sum(math.prod(s) for _, s in SMALL)
SMALL_ROWS = _round_up(SMALL_ELEMS // LANES + 1, 8)
LOSS_POS = SMALL_ROWS * LANES - 1


def _cparams(sem=None):
    return pltpu.CompilerParams(dimension_semantics=sem, vmem_limit_bytes=VMEM_LIMIT_BYTES)


def _block(dim, pref, unit):
    if dim <= pref:
        return dim
    best = None
    for b in range(unit, pref + 1, unit):
        if dim % b == 0:
            best = b
    assert best is not None, (dim, pref, unit)
    return best


def _rows_block(rows, cols, unit=16, elems=1 << 19):
    return _block(rows, max(unit, elems // cols // unit * unit), unit)


def _pack_small(parts, last=None):
    flat = jnp.concatenate([p.reshape(-1).astype(F32) for p in parts])
    flat = jnp.pad(flat, (0, SMALL_ROWS * LANES - flat.shape[0]))
    if last is not None:
        flat = flat.at[LOSS_POS].set(last)
    return flat.reshape(SMALL_ROWS, LANES)


def _unpack_small(packed):
    flat = packed.reshape(-1)
    out, off = [], 0
    for _, shape in SMALL:
        n = math.prod(shape)
        out.append(flat[off:off + n].reshape(shape))
        off += n
    return out


def _my_place():
    return lax.axis_index("x"), lax.axis_index("y"), lax.axis_index("c")


def _other_chips(x, y):
    return [(1 - x, y), (x, 1 - y), (1 - x, 1 - y)]


def _chip(place):
    return 2 * place[0] + place[1]


ANY = pl.BlockSpec(memory_space=pl.ANY)


def _by_rows(rows):
    return rows % 32 == 0


def _half_shape(rows, cols):
    return (rows // 2, cols) if _by_rows(rows) else (rows, cols // 2)


def _core_half(ref, core, lead=()):
    rows, cols = ref.shape[-2:]
    if _by_rows(rows):
        return ref.at[(*lead, pl.ds(core * (rows // 2), rows // 2), slice(None))]
    return ref.at[(*lead, slice(None), pl.ds(core * (cols // 2), cols // 2))]


class _Exchange:
    def __init__(self, arrays, out_shapes, n_sems, start, finish):
        self.arrays, self.out_shapes, self.n_sems, self.start, self.finish = arrays, out_shapes, n_sems, start, finish

    def scratch(self):
        return [pltpu.SemaphoreType.DMA((self.n_sems,)), pltpu.SemaphoreType.DMA((self.n_sems,))]


def _run_exchange(ex, name):
    n = len(ex.arrays)

    def body(*refs):
        ins, outs, sems = refs[:n], refs[n:2 * n], refs[2 * n:]
        ex.start(ins, outs, *sems)
        ex.finish(ins, outs, *sems)

    return pl.pallas_call(
        body, name=name, out_shape=ex.out_shapes, in_specs=[ANY] * n, out_specs=[ANY] * n, scratch_shapes=ex.scratch(),
    )(*ex.arrays)


def _hosted_call(body, ex, *, name, grid, in_specs, out_specs, out_shape, args, aliases=None):
    n_in, n_out = len(args), len(out_shape)
    if ex is None:
        outs = pl.pallas_call(
            body, name=name, grid=grid, out_shape=out_shape, in_specs=in_specs, out_specs=out_specs,
            input_output_aliases=aliases or {}, compiler_params=_cparams(("arbitrary",) * len(grid)))(*args)
        return outs, None
    nc = len(ex.arrays)

    def full_body(*refs):
        ins, cins = refs[:n_in], refs[n_in:n_in + nc]
        outs, couts = refs[n_in + nc:n_in + nc + n_out], refs[n_in + nc + n_out:n_in + 2 * nc + n_out]
        sems = refs[n_in + 2 * nc + n_out:]
        first = functools.reduce(jnp.logical_and, [pl.program_id(a) == 0 for a in range(len(grid))])
        last = functools.reduce(jnp.logical_and, [pl.program_id(a) == grid[a] - 1 for a in range(len(grid))])

        @pl.when(first)
        def _():
            ex.start(cins, couts, *sems)

        body(*ins, *outs)

        @pl.when(last)
        def _():
            ex.finish(cins, couts, *sems)

    outs = pl.pallas_call(
        full_body, name=name, grid=grid, out_shape=list(out_shape) + list(ex.out_shapes),
        in_specs=list(in_specs) + [ANY] * nc, out_specs=list(out_specs) + [ANY] * nc,
        input_output_aliases=aliases or {}, scratch_shapes=ex.scratch(),
        compiler_params=_cparams(("arbitrary",) * len(grid)))(*args, *ex.arrays)
    return outs[:n_out], outs[n_out:]


def _gather_exchange(shards):
    n = len(shards)

    def copies(ins, outs, send_sems, recv_sems):
        x, y, c = _my_place()

        def half(k, chip, core):
            return _core_half(outs[k], core, lead=(_chip(chip),))

        def copy(k, slot, chip, core, to, src=None):
            return pltpu.make_async_remote_copy(
                src_ref=half(k, chip, core) if src is None else src, dst_ref=half(k, chip, core),
                send_sem=send_sems.at[6 * k + slot], recv_sem=recv_sems.at[6 * k + slot],
                device_id=to, device_id_type=MESH)

        return (x, y, c), copy

    def first_copies(ins, outs, send_sems, recv_sems):
        (x, y, c), copy = copies(ins, outs, send_sems, recv_sems)
        out = []
        for j, chip in enumerate(_other_chips(x, y)):
            for k in range(n):
                out.append(copy(k, j, (x, y), c, (*chip, c), src=_core_half(ins[k], c)))
        return out

    def start(ins, outs, send_sems, recv_sems):
        for cp in first_copies(ins, outs, send_sems, recv_sems):
            cp.start()

    def finish(ins, outs, send_sems, recv_sems):
        (x, y, c), copy = copies(ins, outs, send_sems, recv_sems)
        chips = _other_chips(x, y)
        passed = []
        for j, chip in enumerate(chips):
            for k in range(n):
                copy(k, j, chip, c, (x, y, c)).wait_recv()
                passed.append(copy(k, 3 + j, chip, c, (x, y, 1 - c)))
                passed[-1].start()
        for j, chip in enumerate(chips):
            for k in range(n):
                copy(k, 3 + j, chip, 1 - c, (x, y, c)).wait_recv()
        for cp in first_copies(ins, outs, send_sems, recv_sems) + passed:
            cp.wait_send()

    return _Exchange(list(shards), [jax.ShapeDtypeStruct((N_CHIPS,) + s.shape, s.dtype) for s in shards], 6 * n,
                     start, finish)


def _place_own(stacks, shards):
    me = 2 * lax.axis_index("x") + lax.axis_index("y")
    return [lax.dynamic_update_slice(others, mine[None], (me, 0, 0)) for others, mine in zip(stacks, shards)]


def _swap_halves(grads, name):
    n = len(grads)

    def body(*refs):
        ins, outs = refs[:n], refs[n:2 * n]
        send_sems, recv_sems = refs[2 * n:]
        x, y, c = _my_place()
        copies = []
        for k in range(n):
            copies.append(pltpu.make_async_remote_copy(
                src_ref=_core_half(ins[k], 1 - c, lead=(slice(None),)), dst_ref=outs[k],
                send_sem=send_sems.at[k], recv_sem=recv_sems.at[k], device_id=(x, y, 1 - c), device_id_type=MESH))
            copies[-1].start()
        for cp in copies:
            cp.wait()

    return pl.pallas_call(
        body, name=name,
        out_shape=[jax.ShapeDtypeStruct((N_CHIPS,) + _half_shape(*g.shape[1:]), g.dtype) for g in grads],
        in_specs=[ANY] * n, out_specs=[ANY] * n,
        scratch_shapes=[pltpu.SemaphoreType.DMA((n,)), pltpu.SemaphoreType.DMA((n,))],
    )(*grads)


def _chips_exchange(sums):
    n = len(sums)

    def sends(ins, outs, send_sems, recv_sems):
        x, y, c = _my_place()
        return [pltpu.make_async_remote_copy(
            src_ref=ins[k].at[_chip(chip)], dst_ref=outs[k].at[_chip((x, y))],
            send_sem=send_sems.at[3 * k + j], recv_sem=recv_sems.at[3 * k + j],
            device_id=(*chip, c), device_id_type=MESH)
            for j, chip in enumerate(_other_chips(x, y)) for k in range(n)]

    def start(ins, outs, send_sems, recv_sems):
        for cp in sends(ins, outs, send_sems, recv_sems):
            cp.start()

    def finish(ins, outs, send_sems, recv_sems):
        x, y, c = _my_place()
        for j, chip in enumerate(_other_chips(x, y)):
            for k in range(n):
                slot = outs[k].at[_chip(chip)]
                pltpu.make_async_remote_copy(
                    src_ref=slot, dst_ref=slot, send_sem=send_sems.at[3 * k + j], recv_sem=recv_sems.at[3 * k + j],
                    device_id=(x, y, c), device_id_type=MESH).wait_recv()
        for cp in sends(ins, outs, send_sems, recv_sems):
            cp.wait_send()

    return _Exchange(list(sums), [jax.ShapeDtypeStruct(s.shape, s.dtype) for s in sums], 3 * n, start, finish)


def _join_halves(shards):
    n = len(shards)

    def body(*refs):
        ins, outs = refs[:n], refs[n:2 * n]
        send_sems, recv_sems = refs[2 * n:]
        x, y, c = _my_place()
        sends = []
        for k in range(n):
            sends.append(pltpu.make_async_remote_copy(
                src_ref=_core_half(ins[k], c), dst_ref=_core_half(outs[k], c),
                send_sem=send_sems.at[k], recv_sem=recv_sems.at[k], device_id=(x, y, 1 - c), device_id_type=MESH))
            sends[-1].start()
        for k in range(n):
            theirs = _core_half(outs[k], 1 - c)
            pltpu.make_async_remote_copy(
                src_ref=theirs, dst_ref=theirs, send_sem=send_sems.at[k], recv_sem=recv_sems.at[k],
                device_id=(x, y, c), device_id_type=MESH).wait_recv()
        for cp in sends:
            cp.wait_send()

    return pl.pallas_call(
        body, name="join_halves",
        out_shape=[jax.ShapeDtypeStruct(s.shape, s.dtype) for s in shards],
        in_specs=[ANY] * n, out_specs=[ANY] * n,
        input_output_aliases={k: k for k in range(n)},
        scratch_shapes=[pltpu.SemaphoreType.DMA((n,)), pltpu.SemaphoreType.DMA((n,))],
    )(*shards)


def _gather_small(block):
    m_per = block.shape[0]

    def body(x_ref, out_ref, send_sems, recv_sems, local_sem):
        x, y, c = _my_place()
        me, sibling = (x, y, c), (x, y, 1 - c)
        chips = _other_chips(x, y)

        def rows(px, py, pc):
            return out_ref.at[pl.ds((4 * px + 2 * py + pc) * m_per, m_per), :]

        def copy(k, blk, to, src=None):
            return pltpu.make_async_remote_copy(
                src_ref=rows(*blk) if src is None else src, dst_ref=rows(*blk),
                send_sem=send_sems.at[k], recv_sem=recv_sems.at[k], device_id=to, device_id_type=MESH)

        mine = pltpu.make_async_copy(x_ref, rows(*me), local_sem)
        mine.start()
        first = [copy(0, me, sibling, src=x_ref)]
        first += [copy(1 + j, me, (*chip, c), src=x_ref) for j, chip in enumerate(chips)]
        for cp in first:
            cp.start()
        passed = [copy(4 + j, (*chip, c), sibling) for j, chip in enumerate(chips)]
        for j, chip in enumerate(chips):
            copy(1 + j, (*chip, c), me).wait_recv()
            passed[j].start()
        copy(0, sibling, me).wait_recv()
        for j, chip in enumerate(chips):
            copy(4 + j, (*chip, 1 - c), me).wait_recv()
        for cp in first + passed:
            cp.wait_send()
        mine.wait()

    return pl.pallas_call(
        body, name="gather_small",
        out_shape=jax.ShapeDtypeStruct((N_DEV * m_per, LANES), block.dtype),
        in_specs=[pl.BlockSpec(memory_space=pltpu.VMEM)],
        out_specs=pl.BlockSpec(memory_space=pltpu.VMEM),
        scratch_shapes=[pltpu.SemaphoreType.DMA((7,)), pltpu.SemaphoreType.DMA((7,)), pltpu.SemaphoreType.DMA],
    )(block)


def _sum_halves(grads, theirs, core, name):
    _, h, cols = theirs.shape
    by_rows = _by_rows(grads.shape[1])
    br = _rows_block(h, cols) if by_rows else h
    nb = h // br

    def body(core_ref, a_ref, b_ref, o_ref):
        o_ref[...] = (a_ref[...] + b_ref[...]).astype(BF16)

    if by_rows:
        mine = pl.BlockSpec((1, br, cols), lambda j, i, core_ref: (j, core_ref[0] * nb + i, 0))
    else:
        mine = pl.BlockSpec((1, br, cols), lambda j, i, core_ref: (j, i, core_ref[0]))
    return pl.pallas_call(
        body, name=name,
        out_shape=jax.ShapeDtypeStruct(theirs.shape, BF16),
        grid_spec=pltpu.PrefetchScalarGridSpec(
            num_scalar_prefetch=1, grid=(N_CHIPS, nb),
            in_specs=[mine, pl.BlockSpec((1, br, cols), lambda j, i, core_ref: (j, i, 0))],
            out_specs=pl.BlockSpec((1, br, cols), lambda j, i, core_ref: (j, i, 0))),
        compiler_params=_cparams(("parallel", "parallel")),
    )(core, grads, theirs)


def _sum_chips(slots, sums, place, by_rows, name):
    _, h, cols = slots.shape
    br = _rows_block(h, cols) if by_rows else h
    nb = h // br

    def body(place_ref, s_ref, own_ref, o_ref):
        me = place_ref[1]
        acc = None
        for k in range(N_CHIPS):
            term = jnp.where(me == k, own_ref[k], s_ref[k]).astype(F32)
            acc = term if acc is None else acc + term
        o_ref[...] = acc

    stack = pl.BlockSpec((N_CHIPS, br, cols), lambda i, place_ref: (0, i, 0))
    if by_rows:
        out_shape, out_map = (2 * h, cols), lambda i, place_ref: (place_ref[0] * nb + i, 0)
    else:
        out_shape, out_map = (h, 2 * cols), lambda i, place_ref: (i, place_ref[0])
    return pl.pallas_call(
        body, name=name,
        out_shape=jax.ShapeDtypeStruct(out_shape, F32),
        grid_spec=pltpu.PrefetchScalarGridSpec(
            num_scalar_prefetch=1, grid=(nb,), in_specs=[stack, stack],
            out_specs=pl.BlockSpec((br, cols), out_map)),
        compiler_params=_cparams(("parallel",)),
    )(place, slots, sums)


def _adamw_math(w, g, m, v):
    m = ADAM_B1 * m + (1.0 - ADAM_B1) * g
    v = ADAM_B2 * v + (1.0 - ADAM_B2) * (g * g)
    m_hat = m / (1.0 - ADAM_B1 ** ADAM_STEP)
    v_hat = v / (1.0 - ADAM_B2 ** ADAM_STEP)
    delta = -ADAM_LR * (m_hat / (jnp.sqrt(v_hat) + ADAM_EPS) + ADAM_WD * w)
    return delta, m, v


def _adamw(w, g, m, v, name):
    rows, cols = w.shape

    def body(w_ref, g_ref, m_ref, v_ref, d_ref, nm_ref, nv_ref):
        d, nm, nv = _adamw_math(w_ref[...], g_ref[...], m_ref[...], v_ref[...])
        d_ref[...] = d
        nm_ref[...] = nm
        nv_ref[...] = nv

    if rows % 8 == 0:
        br = _rows_block(rows, cols, unit=8)
        spec, steps = pl.BlockSpec((br, cols), lambda i: (i, 0)), rows // br
    else:
        bc = _block(cols, max(LANES, (1 << 19) // rows // LANES * LANES), LANES)
        spec, steps = pl.BlockSpec((rows, bc), lambda i: (0, i)), cols // bc
    shape = jax.ShapeDtypeStruct(w.shape, F32)
    return pl.pallas_call(
        body, name=name, out_shape=(shape, shape, shape), grid=(steps,),
        in_specs=[spec] * 4, out_specs=(spec, spec, spec),
        compiler_params=_cparams(("parallel",)),
    )(w, g, m, v)


def _adamw_small(parts, w, m, v):
    def body(p_ref, w_ref, m_ref, v_ref, g_ref, d_ref, nm_ref, nv_ref):
        g = p_ref[0]
        for k in range(1, N_DEV):
            g = g + p_ref[k]
        d, nm, nv = _adamw_math(w_ref[...], g, m_ref[...], v_ref[...])
        g_ref[...] = g
        d_ref[...] = d
        nm_ref[...] = nm
        nv_ref[...] = nv

    shape = jax.ShapeDtypeStruct((SMALL_ROWS, LANES), F32)
    return pl.pallas_call(body, name="adamw_small", out_shape=(shape,) * 4, compiler_params=_cparams())(parts, w, m, v)


def _mm(a, b, *, name, ta=False, out_dtype=F32, res=None, bm=1024, bn=1024, bk=4096, b_stack=False, out_stack=False):
    if ta:
        kdim, m = a.shape
    else:
        m, kdim = a.shape
    if b_stack:
        _, kb, chunk = b.shape
        n = N_CHIPS * chunk
    else:
        kb, n = b.shape
        chunk = n // N_CHIPS if out_stack else n
    assert kdim == kb, (a.shape, b.shape, ta)
    bm = _block(m, bm, LANES if ta else 16)
    bn = _block(chunk, bn, LANES)
    bk = _block(kdim, bk, LANES)
    nk = kdim // bk
    per_chunk = chunk // bn
    dims = (((0 if ta else 1,), (0,)), ((), ()))

    def body(*refs):
        refs = list(refs)
        a_ref, b_ref = refs[:2]
        r_ref = refs[2] if res is not None else None
        o_ref = refs[3] if res is not None else refs[2]
        part = lax.dot_general(a_ref[...].astype(BF16), b_ref[...].astype(BF16), dims, preferred_element_type=F32)

        def finish(r):
            if r_ref is not None:
                r = r + r_ref[...]
            o_ref[...] = r.astype(out_dtype)

        if nk == 1:
            finish(part)
        else:
            acc_ref = refs[-1]
            k = pl.program_id(2)

            @pl.when(k == 0)
            def _():
                acc_ref[...] = part

            @pl.when(k > 0)
            def _():
                acc_ref[...] += part

            @pl.when(k == nk - 1)
            def _():
                finish(acc_ref[...])

    a_spec = pl.BlockSpec((bk, bm), lambda i, j, k: (k, i)) if ta else pl.BlockSpec((bm, bk), lambda i, j, k: (i, k))
    if b_stack:
        b_spec = pl.BlockSpec((None, bk, bn), lambda i, j, k: (j // per_chunk, k, j % per_chunk))
    else:
        b_spec = pl.BlockSpec((bk, bn), lambda i, j, k: (k, j))
    r_spec = pl.BlockSpec((bm, bn), lambda i, j, k: (i, j))
    if out_stack:
        o_spec = pl.BlockSpec((None, bm, bn), lambda i, j, k: (j // per_chunk, i, j % per_chunk))
        o_shape = (N_CHIPS, m, chunk)
    else:
        o_spec, o_shape = r_spec, (m, n)
    in_specs = [a_spec, b_spec] + ([r_spec] if res is not None else [])
    args = (a, b) + ((res,) if res is not None else ())
    return pl.pallas_call(
        body, name=name, out_shape=jax.ShapeDtypeStruct(o_shape, out_dtype),
        grid=(m // bm, n // bn, nk), in_specs=in_specs, out_specs=o_spec,
        scratch_shapes=[pltpu.VMEM((bm, bn), F32)] if nk > 1 else [],
        compiler_params=_cparams(("parallel", "parallel", "arbitrary")),
    )(*args)


def _rms_fwd(x, g, name):
    t, d = x.shape
    bt = _block(t, 512, 16)

    def body(x_ref, g_ref, h_ref):
        xv = x_ref[...]
        r = lax.rsqrt(jnp.mean(xv * xv, axis=-1, keepdims=True) + EPS)
        h_ref[...] = (xv * r * g_ref[...]).astype(BF16)

    return pl.pallas_call(
        body, name=name, out_shape=jax.ShapeDtypeStruct((t, d), BF16), grid=(t // bt,),
        in_specs=[pl.BlockSpec((bt, d), lambda i: (i, 0)), pl.BlockSpec((1, d), lambda i: (0, 0))],
        out_specs=pl.BlockSpec((bt, d), lambda i: (i, 0)),
        compiler_params=_cparams(("parallel",)),
    )(x, g)


def _rms_bwd(dh, x, g, dres, name):
    t, d = x.shape
    bt = _block(t, 256, 16)
    want_dx = dres is not None

    def body(*refs):
        if want_dx:
            dh_ref, x_ref, g_ref, dres_ref, dx_ref, dxb_ref, dg_ref = refs
        else:
            dh_ref, x_ref, g_ref, dg_ref = refs
        xv = x_ref[...]
        r = lax.rsqrt(jnp.mean(xv * xv, axis=-1, keepdims=True) + EPS)
        xhat = xv * r
        dhv = dh_ref[...]

        @pl.when(pl.program_id(0) == 0)
        def _():
            dg_ref[...] = jnp.zeros_like(dg_ref)

        dg_ref[...] += jnp.sum(dhv * xhat, axis=0, keepdims=True)
        if want_dx:
            dxhat = dhv * g_ref[...]
            dx = dres_ref[...] + r * (dxhat - xhat * jnp.mean(dxhat * xhat, axis=-1, keepdims=True))
            dx_ref[...] = dx
            dxb_ref[...] = dx.astype(BF16)

    row = pl.BlockSpec((bt, d), lambda i: (i, 0))
    vec = pl.BlockSpec((1, d), lambda i: (0, 0))
    if want_dx:
        return pl.pallas_call(
            body, name=name, grid=(t // bt,),
            out_shape=(jax.ShapeDtypeStruct((t, d), F32), jax.ShapeDtypeStruct((t, d), BF16),
                       jax.ShapeDtypeStruct((1, d), F32)),
            in_specs=[row, row, vec, row], out_specs=(row, row, vec),
            compiler_params=_cparams(("arbitrary",)),
        )(dh, x, g, dres)
    return pl.pallas_call(
        body, name=name, grid=(t // bt,), out_shape=jax.ShapeDtypeStruct((1, d), F32),
        in_specs=[row, row, vec], out_specs=vec,
        compiler_params=_cparams(("arbitrary",)),
    )(dh, x, g)


def _final_loss(x, target, g):
    t, d = x.shape
    bt = _block(t, 256, 16)

    def body(x_ref, t_ref, g_ref, dx_ref, dxb_ref, dg_ref, loss_ref):
        xv = x_ref[...]
        gv = g_ref[...]
        r = lax.rsqrt(jnp.mean(xv * xv, axis=-1, keepdims=True) + EPS)
        xhat = xv * r
        err = xhat * gv - t_ref[...]

        @pl.when(pl.program_id(0) == 0)
        def _():
            dg_ref[...] = jnp.zeros_like(dg_ref)
            loss_ref[...] = jnp.zeros_like(loss_ref)

        loss_ref[...] += 0.5 * jnp.sum(jnp.mean(err * err, axis=-1, keepdims=True), axis=0, keepdims=True)
        dy = err * (1.0 / d)
        dg_ref[...] += jnp.sum(dy * xhat, axis=0, keepdims=True)
        dxhat = dy * gv
        dx = r * (dxhat - xhat * jnp.mean(dxhat * xhat, axis=-1, keepdims=True))
        dx_ref[...] = dx
        dxb_ref[...] = dx.astype(BF16)

    row = pl.BlockSpec((bt, d), lambda i: (i, 0))
    vec = pl.BlockSpec((1, d), lambda i: (0, 0))
    return pl.pallas_call(
        body, name="final_loss", grid=(t // bt,),
        out_shape=(jax.ShapeDtypeStruct((t, d), F32), jax.ShapeDtypeStruct((t, d), BF16),
                   jax.ShapeDtypeStruct((1, d), F32), jax.ShapeDtypeStruct((1, LANES), F32)),
        in_specs=[row, row, vec], out_specs=(row, row, vec, pl.BlockSpec((1, LANES), lambda i: (0, 0))),
        compiler_params=_cparams(("arbitrary",)),
    )(x, target, g)


GU_COLS = GATE_WIDTH + POOL_WIDTH
U_BLK = GATE_WIDTH // POOL_WIDTH


def _shift_down(a, k, row):
    return jnp.where(row >= k, pltpu.roll(a, k, 0), 0.0)


def _shift_up(a, k, row):
    n = a.shape[0]
    return jnp.where(row < n - k, pltpu.roll(a, n - k, 0), 0.0)


def _window_delta(u, w, row):
    s, k = u, 1
    while k < w:
        s = s + _shift_down(s, k, row)
        k *= 2
    cnt = jnp.minimum(row + 1, w).astype(F32)
    return s / cnt - u, cnt


def _pool_fwd(gu, pool_w, pool_scale):
    b, s, _ = gu.shape

    def body(u_ref, pw_ref, sc_ref, y_ref):
        row = lax.broadcasted_iota(jnp.int32, (s, POOL_GC), 0)
        for g, w in enumerate(POOL_WINDOWS):
            cols = slice(g * POOL_GC, (g + 1) * POOL_GC)
            d, _ = _window_delta(u_ref[0, :, cols], w, row)
            z = jnp.dot(d.astype(BF16), pw_ref[g].astype(BF16), preferred_element_type=F32)
            y_ref[0, :, cols] = (z * sc_ref[:, cols]).astype(BF16)

    return pl.pallas_call(
        body, name="pool_fwd", out_shape=jax.ShapeDtypeStruct((b, s, POOL_WIDTH), BF16), grid=(b,),
        in_specs=[pl.BlockSpec((1, s, POOL_WIDTH), lambda i: (i, 0, U_BLK)),
                  pl.BlockSpec((4, POOL_GC, POOL_GC), lambda i: (0, 0, 0)),
                  pl.BlockSpec((1, POOL_WIDTH), lambda i: (0, 0))],
        out_specs=pl.BlockSpec((1, s, POOL_WIDTH), lambda i: (i, 0, 0)),
        compiler_params=_cparams(("parallel",)),
    )(gu, pool_w, pool_scale)


def _pool_bwd(gu, dy, pool_w, pool_scale, dgu):
    b, s, _ = gu.shape

    def body(u_ref, dy_ref, pw_ref, sc_ref, dgu_in, du_ref, dpw_ref, dsc_ref):
        del dgu_in

        @pl.when(pl.program_id(0) == 0)
        def _():
            dpw_ref[...] = jnp.zeros_like(dpw_ref)
            dsc_ref[...] = jnp.zeros_like(dsc_ref)

        row = lax.broadcasted_iota(jnp.int32, (s, POOL_GC), 0)
        for g, w in enumerate(POOL_WINDOWS):
            cols = slice(g * POOL_GC, (g + 1) * POOL_GC)
            d, cnt = _window_delta(u_ref[0, :, cols], w, row)
            db = d.astype(BF16)
            pw = pw_ref[g].astype(BF16)
            z = jnp.dot(db, pw, preferred_element_type=F32)
            dyv = dy_ref[0, :, cols]
            dsc_ref[:, cols] += jnp.sum(dyv * z, axis=0, keepdims=True)
            dz = (dyv * sc_ref[:, cols]).astype(BF16)
            dpw_ref[g] += lax.dot_general(db, dz, (((0,), (0,)), ((), ())), preferred_element_type=F32)
            dd = lax.dot_general(dz, pw, (((1,), (1,)), ((), ())), preferred_element_type=F32)
            acc, k = dd / cnt, 1
            while k < w:
                acc = acc + _shift_up(acc, k, row)
                k *= 2
            du_ref[0, :, cols] = (acc - dd).astype(BF16)

    return pl.pallas_call(
        body, name="pool_bwd", grid=(b,),
        out_shape=(jax.ShapeDtypeStruct((b, s, GU_COLS), BF16), jax.ShapeDtypeStruct((4, POOL_GC, POOL_GC), F32),
                   jax.ShapeDtypeStruct((1, POOL_WIDTH), F32)),
        in_specs=[pl.BlockSpec((1, s, POOL_WIDTH), lambda i: (i, 0, U_BLK)),
                  pl.BlockSpec((1, s, POOL_WIDTH), lambda i: (i, 0, 0)),
                  pl.BlockSpec((4, POOL_GC, POOL_GC), lambda i: (0, 0, 0)),
                  pl.BlockSpec((1, POOL_WIDTH), lambda i: (0, 0)), ANY],
        out_specs=(pl.BlockSpec((1, s, POOL_WIDTH), lambda i: (i, 0, U_BLK)),
                   pl.BlockSpec((4, POOL_GC, POOL_GC), lambda i: (0, 0, 0)),
                   pl.BlockSpec((1, POOL_WIDTH), lambda i: (0, 0))),
        input_output_aliases={4: 0},
        compiler_params=_cparams(("arbitrary",)),
    )(gu, dy, pool_w, pool_scale, dgu)


def _forget_cumsum(f, bias):
    b, s, c = f.shape

    def body(f_ref, b_ref, c_ref):
        row = lax.broadcasted_iota(jnp.int32, (s, LANES), 0)
        z = f_ref[0] + b_ref[...]
        acc = jnp.minimum(z, 0.0) - jnp.log(1.0 + jnp.exp(-jnp.abs(z)))
        k = 1
        while k < s:
            acc = acc + _shift_down(acc, k, row)
            k *= 2
        c_ref[0] = acc

    return pl.pallas_call(
        body, name="forget_cumsum", out_shape=jax.ShapeDtypeStruct((b, s, c), F32), grid=(b, c // LANES),
        in_specs=[pl.BlockSpec((1, s, LANES), lambda i, j: (i, 0, j)), pl.BlockSpec((1, LANES), lambda i, j: (0, j))],
        out_specs=pl.BlockSpec((1, s, LANES), lambda i, j: (i, 0, j)),
        compiler_params=_cparams(("parallel", "parallel")),
    )(f, bias)


def _forget_bwd(dc, f, bias):
    b, s, _ = f.shape

    def body(dc_ref, f_ref, b_ref, df_ref, db_ref):
        @pl.when(pl.program_id(0) == 0)
        def _():
            db_ref[...] = jnp.zeros_like(db_ref)

        row = lax.broadcasted_iota(jnp.int32, (s, LANES), 0)
        acc, k = dc_ref[0], 1
        while k < s:
            acc = acc + _shift_up(acc, k, row)
            k *= 2
        z = f_ref[0] + b_ref[...]
        df = acc / (1.0 + jnp.exp(z))
        db_ref[...] += jnp.sum(df, axis=0, keepdims=True)
        df_ref[0] = df.astype(BF16)

    blk = pl.BlockSpec((1, s, LANES), lambda i: (i, 0, 0))
    vec = pl.BlockSpec((1, LANES), lambda i: (0, 0))
    return pl.pallas_call(
        body, name="forget_bwd", grid=(b,),
        out_shape=(jax.ShapeDtypeStruct((b, s, LANES), BF16), jax.ShapeDtypeStruct((1, LANES), F32)),
        in_specs=[blk, blk, vec], out_specs=(blk, vec),
        compiler_params=_cparams(("arbitrary",)),
    )(dc, f, bias)


KV_BLK0 = 2
PAIRS = FOX_HEADS // 2
FOX_SCALE = FOX_DH ** -0.5
NT_DIMS = (((1,), (1,)), ((), ()))
TN_DIMS = (((0,), (0,)), ((), ()))


def _stack_heads(v):
    head = lax.broadcasted_iota(jnp.int32, v.shape, 1) // FOX_DH
    zero = jnp.zeros_like(v)
    return jnp.concatenate([jnp.where(head == 0, v, zero), jnp.where(head == 1, v, zero)], axis=0)


def _stack_cols(v):
    return jnp.concatenate([v[:, 0:1], v[:, FOX_DH:FOX_DH + 1]], axis=0)


def _unstack(t, blk):
    head = lax.broadcasted_iota(jnp.int32, (blk, LANES), 1) // FOX_DH
    return jnp.where(head == 0, t[:blk], t[blk:])


def _fox_scores(q_all, kblk, row_bias, cr_ref, kb, masked, blk):
    top = lax.broadcasted_iota(jnp.int32, (2 * blk, 1), 0) < blk
    s = lax.dot_general(q_all, kblk, NT_DIMS, preferred_element_type=F32)
    s = s + (row_bias - jnp.where(top, cr_ref[0, 0, kb], cr_ref[0, 1, kb]))
    if masked:
        r = lax.broadcasted_iota(jnp.int32, (2 * blk, blk), 0)
        keep = jnp.where(r >= blk, r - blk, r) >= lax.broadcasted_iota(jnp.int32, (2 * blk, blk), 1)
        s = jnp.where(keep, s, NEG_INF)
    return s


def _fox_fwd(qkv, c_exp, c_row, ex=None):
    b, s, _ = qkv.shape
    blk = min(ATT_BLOCK, s)
    nq = s // blk

    def body(q_ref, kv_ref, cc_ref, cr_ref, o_ref, ob_ref, lse_ref):
        qi = pl.program_id(2)
        q_all = _stack_heads(q_ref[0] * FOX_SCALE)
        cq = _stack_cols(cc_ref[0])

        def step(kb, carry, masked):
            m, l, acc = carry
            rows = pl.ds(pl.multiple_of(kb * blk, blk), blk)
            sc = _fox_scores(q_all, kv_ref[0, rows, :LANES], cq, cr_ref, kb, masked, blk)
            m_new = jnp.maximum(m, jnp.max(sc, axis=-1, keepdims=True))
            p = jnp.exp(sc - m_new)
            alpha = jnp.exp(m - m_new)
            l = alpha * l + jnp.sum(p, axis=-1, keepdims=True)
            acc = alpha * acc + jnp.dot(p.astype(BF16), kv_ref[0, rows, LANES:], preferred_element_type=F32)
            return m_new, l, acc

        init = (jnp.full((2 * blk, 1), NEG_INF, F32), jnp.zeros((2 * blk, 1), F32), jnp.zeros((2 * blk, LANES), F32))
        m, l, acc = step(qi, lax.fori_loop(0, qi, functools.partial(step, masked=False), init), True)
        o = _unstack(acc / l, blk)
        o_ref[0] = o
        ob_ref[0] = o.astype(BF16)
        lse_ref[0] = _unstack(jnp.broadcast_to(m + jnp.log(l), (2 * blk, LANES)), blk)

    tile = pl.BlockSpec((1, blk, LANES), lambda i, h, q: (i, q, h))
    kvspec = pl.BlockSpec((1, s, 2 * LANES), lambda i, h, q: (i, 0, KV_BLK0 + h))
    shape = jax.ShapeDtypeStruct((b, s, FOX_WIDTH), F32)
    return _hosted_call(
        body, ex, name="fox_fwd", out_shape=(shape, jax.ShapeDtypeStruct((b, s, FOX_WIDTH), BF16), shape),
        grid=(b, PAIRS, nq),
        in_specs=[tile, kvspec, tile, pl.BlockSpec((1, 2, nq, 1, blk), lambda i, h, q: (i, h, 0, 0, 0))],
        out_specs=(tile, tile, tile), args=(qkv, qkv, c_exp, c_row))


def _fox_bwd_q(qkv, c_exp, c_row, lse, o, do, ex=None):
    b, s, _ = qkv.shape
    blk = min(ATT_BLOCK, s)
    nq = s // blk

    def body(q_ref, kv_ref, cc_ref, cr_ref, lse_ref, o_ref, do_ref, dq_ref, dl_ref, dcq_ref):
        qi = pl.program_id(2)
        q_all = _stack_heads(q_ref[0] * FOX_SCALE)
        dov = do_ref[0]
        do_all = _stack_heads(dov.astype(BF16))
        delta = jnp.sum(_stack_heads(dov * o_ref[0]), axis=-1, keepdims=True)
        bias = _stack_cols(cc_ref[0]) - _stack_cols(lse_ref[0])

        def step(kb, carry, masked):
            acc, dcq = carry
            rows = pl.ds(pl.multiple_of(kb * blk, blk), blk)
            kblk = kv_ref[0, rows, :LANES]
            p = jnp.exp(_fox_scores(q_all, kblk, bias, cr_ref, kb, masked, blk))
            dp = lax.dot_general(do_all, kv_ref[0, rows, LANES:], NT_DIMS, preferred_element_type=F32)
            ds = p * (dp - delta)
            acc = acc + jnp.dot(ds.astype(BF16), kblk, preferred_element_type=F32)
            return acc, dcq + jnp.sum(ds, axis=-1, keepdims=True)

        init = (jnp.zeros((2 * blk, LANES), F32), jnp.zeros((2 * blk, 1), F32))
        acc, dcq = step(qi, lax.fori_loop(0, qi, functools.partial(step, masked=False), init), True)
        dq_ref[0] = (_unstack(acc, blk) * FOX_SCALE).astype(BF16)
        dl_ref[0] = _unstack(jnp.broadcast_to(delta, (2 * blk, LANES)), blk)
        dcq_ref[0] = _unstack(jnp.broadcast_to(dcq, (2 * blk, LANES)), blk)

    tile = pl.BlockSpec((1, blk, LANES), lambda i, h, q: (i, q, h))
    kvspec = pl.BlockSpec((1, s, 2 * LANES), lambda i, h, q: (i, 0, KV_BLK0 + h))
    shape = jax.ShapeDtypeStruct((b, s, FOX_WIDTH), F32)
    return _hosted_call(
        body, ex, name="fox_bwd_q", grid=(b, PAIRS, nq),
        out_shape=(jax.ShapeDtypeStruct(qkv.shape, BF16), shape, shape),
        in_specs=[tile, kvspec, tile, pl.BlockSpec((1, 2, nq, 1, blk), lambda i, h, q: (i, h, 0, 0, 0)),
                  tile, tile, tile],
        out_specs=(tile, tile, tile), args=(qkv, qkv, c_exp, c_row, lse, o, do))


def _fox_bwd_kv(qkv, c_exp, c_row, lse, delta, do, dqkv, ex=None):
    b, s, _ = qkv.shape
    blk = min(ATT_BLOCK, s)
    nq = s // blk

    def body(q_ref, kv_ref, cc_ref, cr_ref, lse_ref, dl_ref, do_ref, dqkv_in, dkv_ref, dc_ref):
        del dqkv_in
        ki = pl.program_id(2)
        kblk = kv_ref[0, :, :LANES]
        vblk = kv_ref[0, :, LANES:]

        def step(qb, carry, masked):
            dk, dv, dc = carry
            rows = pl.ds(pl.multiple_of(qb * blk, blk), blk)
            q_all = _stack_heads(q_ref[0, rows, :] * FOX_SCALE)
            do_all = _stack_heads(do_ref[0, rows, :].astype(BF16))
            bias = _stack_cols(cc_ref[0, rows, :]) - _stack_cols(lse_ref[0, rows, :])
            delta = _stack_cols(dl_ref[0, rows, :])
            p = jnp.exp(_fox_scores(q_all, kblk, bias, cr_ref, 0, masked, blk))
            dv = dv + lax.dot_general(p.astype(BF16), do_all, TN_DIMS, preferred_element_type=F32)
            dp = lax.dot_general(do_all, vblk, NT_DIMS, preferred_element_type=F32)
            ds = p * (dp - delta)
            dk = dk + lax.dot_general(ds.astype(BF16), q_all, TN_DIMS, preferred_element_type=F32)
            col = jnp.concatenate([jnp.sum(ds[:blk], axis=0, keepdims=True), jnp.sum(ds[blk:], axis=0, keepdims=True)],
                                  axis=0)
            return dk, dv, dc - col

        zero = jnp.zeros((blk, LANES), F32)
        carry = step(ki, (zero, zero, jnp.zeros((2, blk), F32)), True)
        dk, dv, dc = lax.fori_loop(ki + 1, nq, functools.partial(step, masked=False), carry)
        dkv_ref[0, :, :LANES] = dk.astype(BF16)
        dkv_ref[0, :, LANES:] = dv.astype(BF16)
        dc_ref[0, 0, 0] = dc[0:1]
        dc_ref[0, 1, 0] = dc[1:2]

    full = pl.BlockSpec((1, s, LANES), lambda i, h, k: (i, 0, h))
    kvtile = pl.BlockSpec((1, blk, 2 * LANES), lambda i, h, k: (i, k, KV_BLK0 + h))
    crow = pl.BlockSpec((1, 2, 1, 1, blk), lambda i, h, k: (i, h, k, 0, 0))
    return _hosted_call(
        body, ex, name="fox_bwd_kv", grid=(b, PAIRS, nq),
        out_shape=(jax.ShapeDtypeStruct(qkv.shape, BF16), jax.ShapeDtypeStruct(c_row.shape, F32)),
        in_specs=[full, kvtile, full, crow, full, full, full, ANY],
        out_specs=(kvtile, crow), aliases={7: 0}, args=(qkv, qkv, c_exp, c_row, lse, delta, do, dqkv))


def _sigmoid(z):
    return 1.0 / (1.0 + jnp.exp(-z))


def _mix_fwd(gu, b_gate, y_pool, y_fox):
    t = gu.shape[0]
    bt = _block(t, 256, 16)

    def body(gp_ref, gf_ref, bp_ref, bf_ref, yp_ref, yf_ref, o_ref):
        gp = _sigmoid(gp_ref[...] + bp_ref[...])
        gf = _sigmoid(gf_ref[...] + bf_ref[...])
        o_ref[...] = (gp * yp_ref[...] + gf * yf_ref[...]).astype(BF16)

    col = lambda j: pl.BlockSpec((bt, D_MODEL), lambda i: (i, j))
    vec = lambda j: pl.BlockSpec((1, D_MODEL), lambda i: (0, j))
    return pl.pallas_call(
        body, name="mix_fwd", out_shape=jax.ShapeDtypeStruct((t, D_MODEL), BF16), grid=(t // bt,),
        in_specs=[col(0), col(1), vec(0), vec(1), col(0), col(0)], out_specs=col(0),
        compiler_params=_cparams(("parallel",)),
    )(gu, gu, b_gate, b_gate, y_pool, y_fox)


def _mix_bwd(gu, b_gate, y_pool, y_fox, dmix):
    t = gu.shape[0]
    bt = _block(t, 256, 16)

    def body(gp_ref, gf_ref, bp_ref, bf_ref, yp_ref, yf_ref, dm_ref, dyp_ref, dyf_ref, dgl_ref, db_ref):
        @pl.when(pl.program_id(0) == 0)
        def _():
            db_ref[...] = jnp.zeros_like(db_ref)

        dm = dm_ref[...]
        gp = _sigmoid(gp_ref[...] + bp_ref[...])
        gf = _sigmoid(gf_ref[...] + bf_ref[...])
        dyp_ref[...] = (dm * gp).astype(BF16)
        dyf_ref[...] = (dm * gf).astype(BF16)
        dlp = dm * yp_ref[...] * gp * (1.0 - gp)
        dlf = dm * yf_ref[...] * gf * (1.0 - gf)
        dgl_ref[:, :D_MODEL] = dlp.astype(BF16)
        dgl_ref[:, D_MODEL:] = dlf.astype(BF16)
        db_ref[:, :D_MODEL] += jnp.sum(dlp, axis=0, keepdims=True)
        db_ref[:, D_MODEL:] += jnp.sum(dlf, axis=0, keepdims=True)

    col = lambda j: pl.BlockSpec((bt, D_MODEL), lambda i: (i, j))
    vec = lambda j: pl.BlockSpec((1, D_MODEL), lambda i: (0, j))
    wide = pl.BlockSpec((bt, GATE_WIDTH), lambda i: (i, 0))
    return pl.pallas_call(
        body, name="mix_bwd", grid=(t // bt,),
        out_shape=(jax.ShapeDtypeStruct((t, D_MODEL), BF16), jax.ShapeDtypeStruct((t, D_MODEL), BF16),
                   jax.ShapeDtypeStruct((t, GU_COLS), BF16), jax.ShapeDtypeStruct((1, GATE_WIDTH), F32)),
        in_specs=[col(0), col(1), vec(0), vec(1), col(0), col(0), col(0)],
        out_specs=(col(0), col(0), wide, pl.BlockSpec((1, GATE_WIDTH), lambda i: (0, 0))),
        compiler_params=_cparams(("arbitrary",)),
    )(gu, gu, b_gate, b_gate, y_pool, y_fox, dmix)


X_SCALE = X_DH ** -0.5


def _xattn_probs(qh, kh):
    s = lax.dot_general(qh, kh, NT_DIMS, preferred_element_type=F32) * X_SCALE
    e = jnp.exp(s - jnp.max(s, axis=-1, keepdims=True))
    return e / jnp.sum(e, axis=-1, keepdims=True)


def _xattn_fwd(q, kv):
    b, s, _ = q.shape
    m = kv.shape[1]
    bq = _block(s, 512, 16)

    def body(q_ref, kv_ref, o_ref):
        for h in range(X_HEADS):
            cols = slice(h * X_DH, (h + 1) * X_DH)
            p = _xattn_probs(q_ref[0, :, cols], kv_ref[0, :, cols])
            vh = kv_ref[0, :, X_WIDTH + h * X_DH:X_WIDTH + (h + 1) * X_DH]
            o_ref[0, :, cols] = jnp.dot(p.astype(BF16), vh, preferred_element_type=F32).astype(BF16)

    return pl.pallas_call(
        body, name="xattn_fwd", out_shape=jax.ShapeDtypeStruct((b, s, X_WIDTH), BF16), grid=(b, s // bq),
        in_specs=[pl.BlockSpec((1, bq, X_WIDTH), lambda i, j: (i, j, 0)),
                  pl.BlockSpec((1, m, 2 * X_WIDTH), lambda i, j: (i, 0, 0))],
        out_specs=pl.BlockSpec((1, bq, X_WIDTH), lambda i, j: (i, j, 0)),
        compiler_params=_cparams(("parallel", "parallel")),
    )(q, kv)


def _xattn_bwd(q, kv, do):
    b, s, _ = q.shape
    m = kv.shape[1]
    bq = _block(s, 512, 16)

    def body(q_ref, kv_ref, do_ref, dq_ref, dkv_ref):
        @pl.when(pl.program_id(1) == 0)
        def _():
            dkv_ref[...] = jnp.zeros_like(dkv_ref)

        for h in range(X_HEADS):
            cols = slice(h * X_DH, (h + 1) * X_DH)
            vcols = slice(X_WIDTH + h * X_DH, X_WIDTH + (h + 1) * X_DH)
            qh, kh, vh, doh = q_ref[0, :, cols], kv_ref[0, :, cols], kv_ref[0, :, vcols], do_ref[0, :, cols]
            p = _xattn_probs(qh, kh)
            dkv_ref[0, :, vcols] += lax.dot_general(p.astype(BF16), doh, TN_DIMS, preferred_element_type=F32)
            dp = lax.dot_general(doh, vh, NT_DIMS, preferred_element_type=F32)
            ds = (p * (dp - jnp.sum(p * dp, axis=-1, keepdims=True)) * X_SCALE).astype(BF16)
            dq_ref[0, :, cols] = jnp.dot(ds, kh, preferred_element_type=F32).astype(BF16)
            dkv_ref[0, :, cols] += lax.dot_general(ds, qh, TN_DIMS, preferred_element_type=F32)

    tile = pl.BlockSpec((1, bq, X_WIDTH), lambda i, j: (i, j, 0))
    mem = pl.BlockSpec((1, m, 2 * X_WIDTH), lambda i, j: (i, 0, 0))
    return pl.pallas_call(
        body, name="xattn_bwd", grid=(b, s // bq),
        out_shape=(jax.ShapeDtypeStruct((b, s, X_WIDTH), BF16), jax.ShapeDtypeStruct((b, m, 2 * X_WIDTH), F32)),
        in_specs=[tile, mem, tile], out_specs=(tile, mem),
        compiler_params=_cparams(("parallel", "arbitrary")),
    )(q, kv, do)


def _swiglu_fwd(gu):
    t = gu.shape[0]
    bt = _block(t, 256, 16)

    def body(gt_ref, up_ref, o_ref):
        gt = gt_ref[...]
        o_ref[...] = (gt * _sigmoid(gt) * up_ref[...]).astype(BF16)

    col = lambda j: pl.BlockSpec((bt, D_FF), lambda i: (i, j))
    return pl.pallas_call(
        body, name="swiglu_fwd", out_shape=jax.ShapeDtypeStruct((t, D_FF), BF16), grid=(t // bt,),
        in_specs=[col(0), col(1)], out_specs=col(0),
        compiler_params=_cparams(("parallel",)),
    )(gu, gu)


def _swiglu_bwd(gu, dact):
    t = gu.shape[0]
    bt = _block(t, 256, 16)

    def body(gt_ref, up_ref, da_ref, o_ref):
        gt = gt_ref[...]
        da = da_ref[...]
        sg = _sigmoid(gt)
        silu = gt * sg
        o_ref[:, :D_FF] = (da * up_ref[...] * (sg + silu * (1.0 - sg))).astype(BF16)
        o_ref[:, D_FF:] = (da * silu).astype(BF16)

    col = lambda j: pl.BlockSpec((bt, D_FF), lambda i: (i, j))
    return pl.pallas_call(
        body, name="swiglu_bwd", out_shape=jax.ShapeDtypeStruct((t, 2 * D_FF), BF16), grid=(t // bt,),
        in_specs=[col(0), col(1), col(0)], out_specs=pl.BlockSpec((bt, 2 * D_FF), lambda i: (i, 0)),
        compiler_params=_cparams(("parallel",)),
    )(gu, gu, dact)


def _stack_of(w, axis):
    r, c = w.shape
    if axis == 0:
        return w.reshape(N_CHIPS, r // N_CHIPS, c)
    return w.reshape(r, N_CHIPS, c // N_CHIPS).transpose(1, 0, 2)


def _stack_t(w3):
    n, r, c = w3.shape
    return w3.transpose(0, 2, 1).reshape(n * c, r)


def _pair_rows(k, v):
    c = k.shape[1]
    return jnp.stack([k.reshape(PAIRS, LANES, c), v.reshape(PAIRS, LANES, c)], axis=1).reshape(2 * FOX_WIDTH, c)


def _unpair_rows(kv):
    c = kv.shape[1]
    kv = kv.reshape(PAIRS, 2, LANES, c)
    return kv[:, 0].reshape(FOX_WIDTH, c), kv[:, 1].reshape(FOX_WIDTH, c)


def _input_grad(parts, weights_t, ex):
    t = parts[0].shape[0]
    d = weights_t[0].shape[1]
    bm = _block(t, 512, 16)
    n = len(parts)

    def body(*refs):
        acc = None
        for a_ref, b_ref in zip(refs[:n], refs[n:2 * n]):
            term = jnp.dot(a_ref[...], b_ref[...], preferred_element_type=F32)
            acc = term if acc is None else acc + term
        refs[2 * n][...] = acc

    (out,), moved = _hosted_call(
        body, ex, name="d_h", grid=(t // bm,), out_shape=(jax.ShapeDtypeStruct((t, d), F32),),
        in_specs=[pl.BlockSpec((bm, p.shape[1]), lambda i: (i, 0)) for p in parts]
        + [pl.BlockSpec(w.shape, lambda i: (0, 0)) for w in weights_t],
        out_specs=(pl.BlockSpec((bm, d), lambda i: (i, 0)),), args=tuple(parts) + tuple(weights_t))
    return out, moved


def _step(x, mem, loss_target, weights, moments_m, moments_v):
    nb, s, d = x.shape
    n_mem = mem.shape[1]
    t = nb * s
    blk = min(ATT_BLOCK, s)
    x2 = x.reshape(t, d)
    mem2 = mem.reshape(nb * n_mem, d)
    tgt2 = loss_target.reshape(t, d)

    def shard2d(a, n):
        a = a.reshape(a.shape[1:])
        return a.T if n == "w_in" else a

    def unshard(a, n):
        return (a.T if n == "w_in" else a)[None]

    local = {n: shard2d(weights[n], n) for n, _, _ in SHARDED}

    names = [n for n, _, _ in SHARDED]
    later = [n for n in names if n != "w_in"]
    local_b = {n: local[n].astype(BF16) for n in names}
    w_in_stack, = _place_own(_run_exchange(_gather_exchange([local_b["w_in"]]), "gather_w_in"), [local_b["w_in"]])
    w_in_t = w_in_stack.reshape(IN_COLS, D_MODEL)
    w_gu_t = jnp.concatenate([w_in_t[2056:], w_in_t[:512]])
    w_qkv_t = jnp.concatenate([w_in_t[512:1024], _pair_rows(w_in_t[1024:1536], w_in_t[1536:2048])])
    w_f_t = jnp.pad(w_in_t[2048:2056], ((0, LANES - FOX_HEADS), (0, 0)))
    w_gu, w_qkv, w_f = w_gu_t.T, w_qkv_t.T, w_f_t.T

    g_mix, g_x, g_mem, g_ffn = (weights[n] for n in ("norm_mix_g", "norm_x_g", "norm_mem_g", "norm_ffn_g"))
    g_final = weights["norm_final_g"].reshape(1, d)
    pool_w = weights["pool_w"].reshape(4, POOL_GC, POOL_GC)
    pool_scale, b_gate = weights["pool_scale"], weights["b_gate"]
    b_f_pad = jnp.pad(weights["b_forget"], ((0, 0), (0, LANES - FOX_HEADS)))
    b_f_exp = jnp.repeat(weights["b_forget"], FOX_DH, axis=1)

    h = _rms_fwd(x2, g_mix, "norm_mix")
    gu = _mm(h, w_gu, bn=512, name="in_proj_gates_pool")
    qkv = _mm(h, w_qkv, out_dtype=BF16, bn=512, name="in_proj_qkv")
    f_pad = _mm(h, w_f, name="in_proj_forget")
    gu3, qkv3 = gu.reshape(nb, s, GU_COLS), qkv.reshape(nb, s, 3 * FOX_WIDTH)
    y = _pool_fwd(gu3, pool_w, pool_scale)
    f_exp = jnp.repeat(f_pad[:, :FOX_HEADS], FOX_DH, axis=1).reshape(nb, s, FOX_WIDTH)
    c_exp = _forget_cumsum(f_exp, b_f_exp)
    c_row = c_exp[:, :, ::FOX_DH].transpose(0, 2, 1).reshape(nb, FOX_HEADS, s // blk, 1, blk)
    (o, o_b, lse), gathered = _fox_fwd(qkv3, c_exp, c_row, ex=_gather_exchange([local_b[n] for n in later]))
    stacks = dict(zip(later, _place_own(gathered, [local_b[n] for n in later])))
    w_pool_out3, w_fox_out3, w_xo3, w_ffn_in3 = (stacks[n] for n in ("w_pool_out", "w_fox_out", "w_xo", "w_ffn_in"))
    w_out, w_xq, w_xkv, w_ffn_out = (stacks[n].reshape(-1, stacks[n].shape[2])
                                     for n in ("w_out", "w_xq", "w_xkv", "w_ffn_out"))
    y2, o2 = y.reshape(t, POOL_WIDTH), o_b.reshape(t, FOX_WIDTH)
    y_pool = _mm(y2, w_pool_out3, b_stack=True, name="pool_out")
    y_fox = _mm(o2, w_fox_out3, b_stack=True, name="fox_out")
    mix = _mix_fwd(gu, b_gate, y_pool, y_fox)
    x1 = _mm(mix, w_out, res=x2, name="mix_out")
    hx = _rms_fwd(x1, g_x, "norm_x")
    mem_n = _rms_fwd(mem2, g_mem, "norm_mem")
    qx = _mm(hx, w_xq, out_dtype=BF16, name="x_q")
    kv = _mm(mem_n, w_xkv, out_dtype=BF16, name="x_kv")
    qx3, kv3 = qx.reshape(nb, s, X_WIDTH), kv.reshape(nb, n_mem, 2 * X_WIDTH)
    ox = _xattn_fwd(qx3, kv3).reshape(t, X_WIDTH)
    x2_ = _mm(ox, w_xo3, b_stack=True, res=x1, name="x_out")
    hf = _rms_fwd(x2_, g_ffn, "norm_ffn")
    ffn = _mm(hf, w_ffn_in3, b_stack=True, bm=512, bn=1408, name="ffn_in")
    act = _swiglu_fwd(ffn)
    x3 = _mm(act, w_ffn_out, res=x2_, name="ffn_out")

    dx3, dx3_b, dg_final, loss_part = _final_loss(x3, tgt2, g_final)
    dw_ffn_out = _mm(act, dx3_b, ta=True, bm=1408, bn=512, bk=2048, name="d_w_ffn_out")
    dact = _mm(dx3_b, w_ffn_out.T, bn=1408, name="d_act")
    dffn = _swiglu_bwd(ffn, dact)
    dw_ffn_in = _mm(hf, dffn, ta=True, bm=512, bn=1408, bk=2048, out_stack=True, name="d_w_ffn_in")
    dhf = _mm(dffn, _stack_t(w_ffn_in3), bk=2816, name="d_hf")
    dx2, dx2_b, dg_ffn = _rms_bwd(dhf, x2_, g_ffn, dx3, "norm_ffn_bwd")

    dw_xo = _mm(ox, dx2_b, ta=True, bn=256, out_stack=True, name="d_w_xo")
    dox = _mm(dx2_b, _stack_t(w_xo3), out_dtype=BF16, name="d_ox").reshape(nb, s, X_WIDTH)
    dqx, dkv = _xattn_bwd(qx3, kv3, dox)
    dqx2, dkv2 = dqx.reshape(t, X_WIDTH), dkv.reshape(nb * n_mem, 2 * X_WIDTH)
    dw_xkv = _mm(mem_n, dkv2, ta=True, name="d_w_xkv")
    dmem_n = _mm(dkv2, w_xkv.T, name="d_mem_n")
    dg_mem = _rms_bwd(dmem_n, mem2, g_mem, None, "norm_mem_bwd")
    dw_xq = _mm(hx, dqx2, ta=True, name="d_w_xq")
    dhx = _mm(dqx2, w_xq.T, name="d_hx")
    dx1, dx1_b, dg_x = _rms_bwd(dhx, x1, g_x, dx2, "norm_x_bwd")

    dw_out = _mm(mix, dx1_b, ta=True, name="d_w_out")
    dmix = _mm(dx1_b, w_out.T, name="d_mix")
    dyp, dyf, dgu, db_gate = _mix_bwd(gu, b_gate, y_pool, y_fox, dmix)
    dw_pool_out = _mm(y2, dyp, ta=True, bn=256, out_stack=True, name="d_w_pool_out")
    dw_fox_out = _mm(o2, dyf, ta=True, bn=256, out_stack=True, name="d_w_fox_out")
    dy = _mm(dyp, _stack_t(w_pool_out3), name="d_y").reshape(nb, s, POOL_WIDTH)
    do = _mm(dyf, _stack_t(w_fox_out3), name="d_o").reshape(nb, s, FOX_WIDTH)
    dgu3, dpool_w, dpool_scale = _pool_bwd(gu3, dy, pool_w, pool_scale, dgu.reshape(nb, s, GU_COLS))
    core = lax.axis_index("c").astype(jnp.int32).reshape(1)
    grad_stacks = {"w_pool_out": dw_pool_out, "w_fox_out": dw_fox_out, "w_out": _stack_of(dw_out, 0),
                   "w_xq": _stack_of(dw_xq, 0), "w_xkv": _stack_of(dw_xkv, 0), "w_xo": dw_xo,
                   "w_ffn_in": dw_ffn_in, "w_ffn_out": _stack_of(dw_ffn_out, 0)}

    def chip_sums_of(group):
        theirs = _swap_halves([grad_stacks[n] for n in group], "swap_halves_" + group[0])
        return [_sum_halves(grad_stacks[n], t_, core, "sum_halves_" + n) for n, t_ in zip(group, theirs)]

    ffn_group = ["w_ffn_in", "w_ffn_out"]
    mid_group = ["w_pool_out", "w_fox_out", "w_out", "w_xq", "w_xkv", "w_xo"]
    chip_sums = dict(zip(ffn_group, chip_sums_of(ffn_group)))
    chip_sums.update(zip(mid_group, chip_sums_of(mid_group)))
    (dqkv3, delta, dc_q), ffn_slots = _fox_bwd_q(qkv3, c_exp, c_row, lse, o, do,
                                                 ex=_chips_exchange([chip_sums[n] for n in ffn_group]))
    (dqkv3, dc_row), mid_slots = _fox_bwd_kv(qkv3, c_exp, c_row, lse, delta, do, dqkv3,
                                             ex=_chips_exchange([chip_sums[n] for n in mid_group]))
    slots = dict(zip(ffn_group + mid_group, list(ffn_slots) + list(mid_slots)))
    dc = dc_row.reshape(nb, FOX_HEADS, s).transpose(0, 2, 1) + dc_q[:, :, ::FOX_DH]
    dc = jnp.pad(dc, ((0, 0), (0, 0), (0, LANES - FOX_HEADS)))
    df, db_f = _forget_bwd(dc, f_pad.reshape(nb, s, LANES), b_f_pad)
    dgu2, dqkv2, df2 = dgu3.reshape(t, GU_COLS), dqkv3.reshape(t, 3 * FOX_WIDTH), df.reshape(t, LANES)
    dw_gu_t = _mm(dgu2, h, ta=True, name="d_w_gates_pool")
    dw_qkv_t = _mm(dqkv2, h, ta=True, name="d_w_qkv")
    dw_f_t = _mm(df2, h, ta=True, name="d_w_forget")
    dw_k_t, dw_v_t = _unpair_rows(dw_qkv_t[FOX_WIDTH:])
    dw_in_t = jnp.concatenate([dw_gu_t[GATE_WIDTH:], dw_qkv_t[:FOX_WIDTH], dw_k_t, dw_v_t, dw_f_t[:FOX_HEADS],
                               dw_gu_t[:GATE_WIDTH]])
    grad_stacks["w_in"] = dw_in_t.reshape(N_CHIPS, IN_COLS // N_CHIPS, D_MODEL)
    chip_sums["w_in"], = chip_sums_of(["w_in"])
    dh, (slots["w_in"],) = _input_grad([dgu2, dqkv2, df2], [w_gu_t, w_qkv_t, w_f_t],
                                       _chips_exchange([chip_sums["w_in"]]))
    dx, _, dg_mix = _rms_bwd(dh, x2, g_mix, dx1, "norm_mix_bwd")

    place = jnp.stack([lax.axis_index("c"), 2 * lax.axis_index("x") + lax.axis_index("y")]).astype(jnp.int32)
    reduced = _join_halves([_sum_chips(slots[n], chip_sums[n], place, _by_rows(local[n].shape[0]), "sum_chips_" + n)
                            for n in names])

    small_grads = {"norm_mix_g": dg_mix, "b_forget": db_f[:, :FOX_HEADS], "b_gate": db_gate, "pool_w": dpool_w,
                   "pool_scale": dpool_scale, "norm_x_g": dg_x, "norm_mem_g": dg_mem, "norm_ffn_g": dg_ffn,
                   "norm_final_g": dg_final}
    parts = _gather_small(_pack_small([small_grads[n] for n, _ in SMALL], last=loss_part[0, 0]))
    sg, sd, sm, sv = _adamw_small(parts.reshape(N_DEV, SMALL_ROWS, LANES),
                                  _pack_small([weights[n] for n, _ in SMALL]),
                                  _pack_small([moments_m[n] for n, _ in SMALL]),
                                  _pack_small([moments_v[n] for n, _ in SMALL]))
    loss = sg.reshape(-1)[LOSS_POS]

    grads, deltas, new_m, new_v = {}, {}, {}, {}
    for (n, _), g_, d_, m_, v_ in zip(SMALL, _unpack_small(sg), _unpack_small(sd), _unpack_small(sm), _unpack_small(sv)):
        grads[n], deltas[n], new_m[n], new_v[n] = g_, d_, m_, v_
    for n, g_ in zip(names, reduced):
        d_, m_, v_ = _adamw(local[n], g_, shard2d(moments_m[n], n), shard2d(moments_v[n], n), "adamw_" + n)
        grads[n], deltas[n], new_m[n], new_v[n] = (unshard(a, n) for a in (g_, d_, m_, v_))
    return loss, dx.reshape(nb, s, d), grads, deltas, new_m, new_v


def kernel(x, mem, norm_mix_g, w_in, b_forget, b_gate, pool_w, pool_scale, w_pool_out, w_fox_out, w_out, norm_x_g, norm_mem_g, w_xq, w_xkv, w_xo, norm_ffn_g, w_ffn_in, w_ffn_out, norm_final_g, loss_target, m_norm_mix_g, m_w_in, m_b_forget, m_b_gate, m_pool_w, m_pool_scale, m_w_pool_out, m_w_fox_out, m_w_out, m_norm_x_g, m_norm_mem_g, m_w_xq, m_w_xkv, m_w_xo, m_norm_ffn_g, m_w_ffn_in, m_w_ffn_out, m_norm_final_g, v_norm_mix_g, v_w_in, v_b_forget, v_b_gate, v_pool_w, v_pool_scale, v_w_pool_out, v_w_fox_out, v_w_out, v_norm_x_g, v_norm_mem_g, v_w_xq, v_w_xkv, v_w_xo, v_norm_ffn_g, v_w_ffn_in, v_w_ffn_out, v_norm_final_g):
    given = dict(locals())
    weights = {n: given[n] for n in WEIGHT_ORDER}
    moments_m = {n: given["m_" + n] for n in WEIGHT_ORDER}
    moments_v = {n: given["v_" + n] for n in WEIGHT_ORDER}
    loss, grad_x, grads, deltas, new_m, new_v = _step(x, mem, loss_target, weights, moments_m, moments_v)
    return (loss, grad_x, *[grads[n] for n in WEIGHT_ORDER], *[deltas[n] for n in WEIGHT_ORDER],
            *[new_m[n] for n in WEIGHT_ORDER], *[new_v[n] for n in WEIGHT_ORDER])
```

```python
import functools
import math

import jax
import jax.numpy as jnp
from jax import lax
from jax.experimental import pallas as pl
from jax.experimental.pallas import tpu as pltpu

F32 = jnp.float32
BF16 = jnp.bfloat16
MESH = pl.DeviceIdType.MESH

D_MODEL = 1024
EPS = 1e-6
POOL_WINDOWS = (2, 4, 8, 16)
POOL_WIDTH = 512
POOL_GC = 128
FOX_HEADS = 8
FOX_DH = 64
FOX_WIDTH = 512
X_HEADS = 4
X_DH = 128
X_WIDTH = 512
D_FF = 2816
IN_COLS = 4104
GATE_WIDTH = 2048
ADAM_LR = 0.001
ADAM_B1 = 0.9
ADAM_B2 = 0.999
ADAM_EPS = 1e-08
ADAM_WD = 0.01
ADAM_STEP = 10

N_CHIPS = 4
N_DEV = 8
LANES = 128
VMEM_LIMIT_BYTES = 56 * 1024 * 1024
NEG_INF = -1e30
ATT_BLOCK = 512

SHARDED = (
    ("w_in", (1024, IN_COLS), 1),
    ("w_pool_out", (POOL_WIDTH, 1024), 1),
    ("w_fox_out", (FOX_WIDTH, 1024), 1),
    ("w_out", (1024, 1024), 0),
    ("w_xq", (1024, X_WIDTH), 0),
    ("w_xkv", (1024, 2 * X_WIDTH), 0),
    ("w_xo", (X_WIDTH, 1024), 1),
    ("w_ffn_in", (1024, 2 * D_FF), 1),
    ("w_ffn_out", (D_FF, 1024), 0),
)
SMALL = (
    ("norm_mix_g", (1, 1024)),
    ("b_forget", (1, 8)),
    ("b_gate", (1, 2048)),
    ("pool_w", (1, 4, 128, 128)),
    ("pool_scale", (1, 512)),
    ("norm_x_g", (1, 1024)),
    ("norm_mem_g", (1, 1024)),
    ("norm_ffn_g", (1, 1024)),
    ("norm_final_g", (1024,)),
)
WEIGHT_ORDER = ("norm_mix_g", "w_in", "b_forget", "b_gate", "pool_w", "pool_scale", "w_pool_out", "w_fox_out", "w_out",
                "norm_x_g", "norm_mem_g", "w_xq", "w_xkv", "w_xo", "norm_ffn_g", "w_ffn_in", "w_ffn_out", "norm_final_g")


def _round_up(n, m):
    return (n + m - 1) // m * m


SMALL_ELEMS = sum(math.prod(s) for _, s in SMALL)
SMALL_ROWS = _round_up(SMALL_ELEMS // LANES + 1, 8)
LOSS_POS = SMALL_ROWS * LANES - 1


def _cparams(sem=None):
    return pltpu.CompilerParams(dimension_semantics=sem, vmem_limit_bytes=VMEM_LIMIT_BYTES)


def _block(dim, pref, unit):
    if dim <= pref:
        return dim
    best = None
    for b in range(unit, pref + 1, unit):
        if dim % b == 0:
            best = b
    assert best is not None, (dim, pref, unit)
    return best


def _rows_block(rows, cols, unit=16, elems=1 << 19):
    return _block(rows, max(unit, elems // cols // unit * unit), unit)


def _pack_small(parts, last=None):
    flat = jnp.concatenate([p.reshape(-1).astype(F32) for p in parts])
    flat = jnp.pad(flat, (0, SMALL_ROWS * LANES - flat.shape[0]))
    if last is not None:
        flat = flat.at[LOSS_POS].set(last)
    return flat.reshape(SMALL_ROWS, LANES)


def _unpack_small(packed):
    flat = packed.reshape(-1)
    out, off = [], 0
    for _, shape in SMALL:
        n = math.prod(shape)
        out.append(flat[off:off + n].reshape(shape))
        off += n
    return out


def _my_place():
    return lax.axis_index("x"), lax.axis_index("y"), lax.axis_index("c")


def _other_chips(x, y):
    return [(1 - x, y), (x, 1 - y), (1 - x, 1 - y)]


def _chip(place):
    return 2 * place[0] + place[1]


ANY = pl.BlockSpec(memory_space=pl.ANY)


def _by_rows(rows):
    return rows % 32 == 0


def _half_shape(rows, cols):
    return (rows // 2, cols) if _by_rows(rows) else (rows, cols // 2)


def _core_half(ref, core, lead=()):
    rows, cols = ref.shape[-2:]
    if _by_rows(rows):
        return ref.at[(*lead, pl.ds(core * (rows // 2), rows // 2), slice(None))]
    return ref.at[(*lead, slice(None), pl.ds(core * (cols // 2), cols // 2))]


class _Exchange:
    def __init__(self, arrays, out_shapes, n_sems, start, finish):
        self.arrays, self.out_shapes, self.n_sems, self.start, self.finish = arrays, out_shapes, n_sems, start, finish

    def scratch(self):
        return [pltpu.SemaphoreType.DMA((self.n_sems,)), pltpu.SemaphoreType.DMA((self.n_sems,))]


def _run_exchange(ex, name):
    n = len(ex.arrays)

    def body(*refs):
        ins, outs, sems = refs[:n], refs[n:2 * n], refs[2 * n:]
        ex.start(ins, outs, *sems)
        ex.finish(ins, outs, *sems)

    return pl.pallas_call(
        body, name=name, out_shape=ex.out_shapes, in_specs=[ANY] * n, out_specs=[ANY] * n, scratch_shapes=ex.scratch(),
    )(*ex.arrays)


def _hosted_call(body, ex, *, name, grid, in_specs, out_specs, out_shape, args, aliases=None):
    n_in, n_out = len(args), len(out_shape)
    if ex is None:
        outs = pl.pallas_call(
            body, name=name, grid=grid, out_shape=out_shape, in_specs=in_specs, out_specs=out_specs,
            input_output_aliases=aliases or {}, compiler_params=_cparams(("arbitrary",) * len(grid)))(*args)
        return outs, None
    nc = len(ex.arrays)

    def full_body(*refs):
        ins, cins = refs[:n_in], refs[n_in:n_in + nc]
        outs, couts = refs[n_in + nc:n_in + nc + n_out], refs[n_in + nc + n_out:n_in + 2 * nc + n_out]
        sems = refs[n_in + 2 * nc + n_out:]
        first = functools.reduce(jnp.logical_and, [pl.program_id(a) == 0 for a in range(len(grid))])
        last = functools.reduce(jnp.logical_and, [pl.program_id(a) == grid[a] - 1 for a in range(len(grid))])

        @pl.when(first)
        def _():
            ex.start(cins, couts, *sems)

        body(*ins, *outs)

        @pl.when(last)
        def _():
            ex.finish(cins, couts, *sems)

    outs = pl.pallas_call(
        full_body, name=name, grid=grid, out_shape=list(out_shape) + list(ex.out_shapes),
        in_specs=list(in_specs) + [ANY] * nc, out_specs=list(out_specs) + [ANY] * nc,
        input_output_aliases=aliases or {}, scratch_shapes=ex.scratch(),
        compiler_params=_cparams(("arbitrary",) * len(grid)))(*args, *ex.arrays)
    return outs[:n_out], outs[n_out:]


def _gather_exchange(shards):
    n = len(shards)

    def copies(ins, outs, send_sems, recv_sems):
        x, y, c = _my_place()

        def half(k, chip, core):
            return _core_half(outs[k], core, lead=(_chip(chip),))

        def copy(k, slot, chip, core, to, src=None):
            return pltpu.make_async_remote_copy(
                src_ref=half(k, chip, core) if src is None else src, dst_ref=half(k, chip, core),
                send_sem=send_sems.at[6 * k + slot], recv_sem=recv_sems.at[6 * k + slot],
                device_id=to, device_id_type=MESH)

        return (x, y, c), copy

    def first_copies(ins, outs, send_sems, recv_sems):
        (x, y, c), copy = copies(ins, outs, send_sems, recv_sems)
        out = []
        for j, chip in enumerate(_other_chips(x, y)):
            for k in range(n):
                out.append(copy(k, j, (x, y), c, (*chip, c), src=_core_half(ins[k], c)))
        return out

    def start(ins, outs, send_sems, recv_sems):
        for cp in first_copies(ins, outs, send_sems, recv_sems):
            cp.start()

    def finish(ins, outs, send_sems, recv_sems):
        (x, y, c), copy = copies(ins, outs, send_sems, recv_sems)
        chips = _other_chips(x, y)
        passed = []
        for j, chip in enumerate(chips):
            for k in range(n):
                copy(k, j, chip, c, (x, y, c)).wait_recv()
                passed.append(copy(k, 3 + j, chip, c, (x, y, 1 - c)))
                passed[-1].start()
        for j, chip in enumerate(chips):
            for k in range(n):
                copy(k, 3 + j, chip, 1 - c, (x, y, c)).wait_recv()
        for cp in first_copies(ins, outs, send_sems, recv_sems) + passed:
            cp.wait_send()

    return _Exchange(list(shards), [jax.ShapeDtypeStruct((N_CHIPS,) + s.shape, s.dtype) for s in shards], 6 * n,
                     start, finish)


def _place_own(stacks, shards):
    me = 2 * lax.axis_index("x") + lax.axis_index("y")
    return [lax.dynamic_update_slice(others, mine[None], (me, 0, 0)) for others, mine in zip(stacks, shards)]


def _swap_halves(grads, name):
    n = len(grads)

    def body(*refs):
        ins, outs = refs[:n], refs[n:2 * n]
        send_sems, recv_sems = refs[2 * n:]
        x, y, c = _my_place()
        copies = []
        for k in range(n):
            copies.append(pltpu.make_async_remote_copy(
                src_ref=_core_half(ins[k], 1 - c, lead=(slice(None),)), dst_ref=outs[k],
                send_sem=send_sems.at[k], recv_sem=recv_sems.at[k], device_id=(x, y, 1 - c), device_id_type=MESH))
            copies[-1].start()
        for cp in copies:
            cp.wait()

    return pl.pallas_call(
        body, name=name,
        out_shape=[jax.ShapeDtypeStruct((N_CHIPS,) + _half_shape(*g.shape[1:]), g.dtype) for g in grads],
        in_specs=[ANY] * n, out_specs=[ANY] * n,
        scratch_shapes=[pltpu.SemaphoreType.DMA((n,)), pltpu.SemaphoreType.DMA((n,))],
    )(*grads)


def _chips_exchange(sums):
    n = len(sums)

    def sends(ins, outs, send_sems, recv_sems):
        x, y, c = _my_place()
        return [pltpu.make_async_remote_copy(
            src_ref=ins[k].at[_chip(chip)], dst_ref=outs[k].at[_chip((x, y))],
            send_sem=send_sems.at[3 * k + j], recv_sem=recv_sems.at[3 * k + j],
            device_id=(*chip, c), device_id_type=MESH)
            for j, chip in enumerate(_other_chips(x, y)) for k in range(n)]

    def start(ins, outs, send_sems, recv_sems):
        for cp in sends(ins, outs, send_sems, recv_sems):
            cp.start()

    def finish(ins, outs, send_sems, recv_sems):
        x, y, c = _my_place()
        for j, chip in enumerate(_other_chips(x, y)):
            for k in range(n):
                slot = outs[k].at[_chip(chip)]
                pltpu.make_async_remote_copy(
                    src_ref=slot, dst_ref=slot, send_sem=send_sems.at[3 * k + j], recv_sem=recv_sems.at[3 * k + j],
                    device_id=(x, y, c), device_id_type=MESH).wait_recv()
        for cp in sends(ins, outs, send_sems, recv_sems):
            cp.wait_send()

    return _Exchange(list(sums), [jax.ShapeDtypeStruct(s.shape, s.dtype) for s in sums], 3 * n, start, finish)


def _join_halves(shards):
    n = len(shards)

    def body(*refs):
        ins, outs = refs[:n], refs[n:2 * n]
        send_sems, recv_sems = refs[2 * n:]
        x, y, c = _my_place()
        sends = []
        for k in range(n):
            sends.append(pltpu.make_async_remote_copy(
                src_ref=_core_half(ins[k], c), dst_ref=_core_half(outs[k], c),
                send_sem=send_sems.at[k], recv_sem=recv_sems.at[k], device_id=(x, y, 1 - c), device_id_type=MESH))
            sends[-1].start()
        for k in range(n):
            theirs = _core_half(outs[k], 1 - c)
            pltpu.make_async_remote_copy(
                src_ref=theirs, dst_ref=theirs, send_sem=send_sems.at[k], recv_sem=recv_sems.at[k],
                device_id=(x, y, c), device_id_type=MESH).wait_recv()
        for cp in sends:
            cp.wait_send()

    return pl.pallas_call(
        body, name="join_halves",
        out_shape=[jax.ShapeDtypeStruct(s.shape, s.dtype) for s in shards],
        in_specs=[ANY] * n, out_specs=[ANY] * n,
        input_output_aliases={k: k for k in range(n)},
        scratch_shapes=[pltpu.SemaphoreType.DMA((n,)), pltpu.SemaphoreType.DMA((n,))],
    )(*shards)


def _gather_small(block):
    m_per = block.shape[0]

    def body(x_ref, out_ref, send_sems, recv_sems, local_sem):
        x, y, c = _my_place()
        me, sibling = (x, y, c), (x, y, 1 - c)
        chips = _other_chips(x, y)

        def rows(px, py, pc):
            return out_ref.at[pl.ds((4 * px + 2 * py + pc) * m_per, m_per), :]

        def copy(k, blk, to, src=None):
            return pltpu.make_async_remote_copy(
                src_ref=rows(*blk) if src is None else src, dst_ref=rows(*blk),
                send_sem=send_sems.at[k], recv_sem=recv_sems.at[k], device_id=to, device_id_type=MESH)

        mine = pltpu.make_async_copy(x_ref, rows(*me), local_sem)
        mine.start()
        first = [copy(0, me, sibling, src=x_ref)]
        first += [copy(1 + j, me, (*chip, c), src=x_ref) for j, chip in enumerate(chips)]
        for cp in first:
            cp.start()
        passed = [copy(4 + j, (*chip, c), sibling) for j, chip in enumerate(chips)]
        for j, chip in enumerate(chips):
            copy(1 + j, (*chip, c), me).wait_recv()
            passed[j].start()
        copy(0, sibling, me).wait_recv()
        for j, chip in enumerate(chips):
            copy(4 + j, (*chip, 1 - c), me).wait_recv()
        for cp in first + passed:
            cp.wait_send()
        mine.wait()

    return pl.pallas_call(
        body, name="gather_small",
        out_shape=jax.ShapeDtypeStruct((N_DEV * m_per, LANES), block.dtype),
        in_specs=[pl.BlockSpec(memory_space=pltpu.VMEM)],
        out_specs=pl.BlockSpec(memory_space=pltpu.VMEM),
        scratch_shapes=[pltpu.SemaphoreType.DMA((7,)), pltpu.SemaphoreType.DMA((7,)), pltpu.SemaphoreType.DMA],
    )(block)


def _sum_halves(grads, theirs, core, name):
    _, h, cols = theirs.shape
    by_rows = _by_rows(grads.shape[1])
    br = _rows_block(h, cols) if by_rows else h
    nb = h // br

    def body(core_ref, a_ref, b_ref, o_ref):
        o_ref[...] = (a_ref[...] + b_ref[...]).astype(BF16)

    if by_rows:
        mine = pl.BlockSpec((1, br, cols), lambda j, i, core_ref: (j, core_ref[0] * nb + i, 0))
    else:
        mine = pl.BlockSpec((1, br, cols), lambda j, i, core_ref: (j, i, core_ref[0]))
    return pl.pallas_call(
        body, name=name,
        out_shape=jax.ShapeDtypeStruct(theirs.shape, BF16),
        grid_spec=pltpu.PrefetchScalarGridSpec(
            num_scalar_prefetch=1, grid=(N_CHIPS, nb),
            in_specs=[mine, pl.BlockSpec((1, br, cols), lambda j, i, core_ref: (j, i, 0))],
            out_specs=pl.BlockSpec((1, br, cols), lambda j, i, core_ref: (j, i, 0))),
        compiler_params=_cparams(("parallel", "parallel")),
    )(core, grads, theirs)


def _sum_chips(slots, sums, place, by_rows, name):
    _, h, cols = slots.shape
    br = _rows_block(h, cols) if by_rows else h
    nb = h // br

    def body(place_ref, s_ref, own_ref, o_ref):
        me = place_ref[1]
        acc = None
        for k in range(N_CHIPS):
            term = jnp.where(me == k, own_ref[k], s_ref[k]).astype(F32)
            acc = term if acc is None else acc + term
        o_ref[...] = acc

    stack = pl.BlockSpec((N_CHIPS, br, cols), lambda i, place_ref: (0, i, 0))
    if by_rows:
        out_shape, out_map = (2 * h, cols), lambda i, place_ref: (place_ref[0] * nb + i, 0)
    else:
        out_shape, out_map = (h, 2 * cols), lambda i, place_ref: (i, place_ref[0])
    return pl.pallas_call(
        body, name=name,
        out_shape=jax.ShapeDtypeStruct(out_shape, F32),
        grid_spec=pltpu.PrefetchScalarGridSpec(
            num_scalar_prefetch=1, grid=(nb,), in_specs=[stack, stack],
            out_specs=pl.BlockSpec((br, cols), out_map)),
        compiler_params=_cparams(("parallel",)),
    )(place, slots, sums)


def _adamw_math(w, g, m, v):
    m = ADAM_B1 * m + (1.0 - ADAM_B1) * g
    v = ADAM_B2 * v + (1.0 - ADAM_B2) * (g * g)
    m_hat = m / (1.0 - ADAM_B1 ** ADAM_STEP)
    v_hat = v / (1.0 - ADAM_B2 ** ADAM_STEP)
    delta = -ADAM_LR * (m_hat / (jnp.sqrt(v_hat) + ADAM_EPS) + ADAM_WD * w)
    return delta, m, v


def _adamw(w, g, m, v, name):
    def body(w_ref, g_ref, m_ref, v_ref, d_ref, nm_ref, nv_ref):
        d, nm, nv = _adamw_math(w_ref[...], g_ref[...], m_ref[...], v_ref[...])
        d_ref[...] = d
        nm_ref[...] = nm
        nv_ref[...] = nv

    if w.ndim == 3:
        rows = w.shape[0]
        br = max(b for b in range(1, 65) if rows % b == 0)
        spec, steps = pl.BlockSpec((br,) + w.shape[1:], lambda i: (i, 0, 0)), rows // br
    else:
        rows, cols = w.shape
        br = _rows_block(rows, cols, unit=8)
        spec, steps = pl.BlockSpec((br, cols), lambda i: (i, 0)), rows // br
    shape = jax.ShapeDtypeStruct(w.shape, F32)
    return pl.pallas_call(
        body, name=name, out_shape=(shape, shape, shape), grid=(steps,),
        in_specs=[spec] * 4, out_specs=(spec, spec, spec),
        compiler_params=_cparams(("parallel",)),
    )(w, g, m, v)


def _adamw_small(parts, w, m, v):
    def body(p_ref, w_ref, m_ref, v_ref, g_ref, d_ref, nm_ref, nv_ref):
        g = p_ref[0]
        for k in range(1, N_DEV):
            g = g + p_ref[k]
        d, nm, nv = _adamw_math(w_ref[...], g, m_ref[...], v_ref[...])
        g_ref[...] = g
        d_ref[...] = d
        nm_ref[...] = nm
        nv_ref[...] = nv

    shape = jax.ShapeDtypeStruct((SMALL_ROWS, LANES), F32)
    return pl.pallas_call(body, name="adamw_small", out_shape=(shape,) * 4, compiler_params=_cparams())(parts, w, m, v)


def _mm(a, b, *, name, ta=False, out_dtype=F32, res=None, bm=1024, bn=1024, bk=4096, b_stack=False, out_stack=False):
    if ta:
        kdim, m = a.shape
    else:
        m, kdim = a.shape
    if b_stack:
        _, kb, chunk = b.shape
        n = N_CHIPS * chunk
    else:
        kb, n = b.shape
        chunk = n // N_CHIPS if out_stack else n
    assert kdim == kb, (a.shape, b.shape, ta)
    bm = _block(m, bm, LANES if ta else 16)
    bn = _block(chunk, bn, LANES)
    bk = _block(kdim, bk, LANES)
    nk = kdim // bk
    per_chunk = chunk // bn
    dims = (((0 if ta else 1,), (0,)), ((), ()))

    def body(*refs):
        refs = list(refs)
        a_ref, b_ref = refs[:2]
        r_ref = refs[2] if res is not None else None
        o_ref = refs[3] if res is not None else refs[2]
        part = lax.dot_general(a_ref[...].astype(BF16), b_ref[...].astype(BF16), dims, preferred_element_type=F32)

        def finish(r):
            if r_ref is not None:
                r = r + r_ref[...]
            o_ref[...] = r.astype(out_dtype)

        if nk == 1:
            finish(part)
        else:
            acc_ref = refs[-1]
            k = pl.program_id(2)

            @pl.when(k == 0)
            def _():
                acc_ref[...] = part

            @pl.when(k > 0)
            def _():
                acc_ref[...] += part

            @pl.when(k == nk - 1)
            def _():
                finish(acc_ref[...])

    a_spec = pl.BlockSpec((bk, bm), lambda i, j, k: (k, i)) if ta else pl.BlockSpec((bm, bk), lambda i, j, k: (i, k))
    if b_stack:
        b_spec = pl.BlockSpec((None, bk, bn), lambda i, j, k: (j // per_chunk, k, j % per_chunk))
    else:
        b_spec = pl.BlockSpec((bk, bn), lambda i, j, k: (k, j))
    r_spec = pl.BlockSpec((bm, bn), lambda i, j, k: (i, j))
    if out_stack:
        o_spec = pl.BlockSpec((None, bm, bn), lambda i, j, k: (j // per_chunk, i, j % per_chunk))
        o_shape = (N_CHIPS, m, chunk)
    else:
        o_spec, o_shape = r_spec, (m, n)
    in_specs = [a_spec, b_spec] + ([r_spec] if res is not None else [])
    args = (a, b) + ((res,) if res is not None else ())
    return pl.pallas_call(
        body, name=name, out_shape=jax.ShapeDtypeStruct(o_shape, out_dtype),
        grid=(m // bm, n // bn, nk), in_specs=in_specs, out_specs=o_spec,
        scratch_shapes=[pltpu.VMEM((bm, bn), F32)] if nk > 1 else [],
        compiler_params=_cparams(("parallel", "parallel", "arbitrary")),
    )(*args)


def _rms_fwd(x, g, name):
    t, d = x.shape
    bt = _block(t, 512, 16)

    def body(x_ref, g_ref, h_ref):
        xv = x_ref[...]
        r = lax.rsqrt(jnp.mean(xv * xv, axis=-1, keepdims=True) + EPS)
        h_ref[...] = (xv * r * g_ref[...]).astype(BF16)

    return pl.pallas_call(
        body, name=name, out_shape=jax.ShapeDtypeStruct((t, d), BF16), grid=(t // bt,),
        in_specs=[pl.BlockSpec((bt, d), lambda i: (i, 0)), pl.BlockSpec((1, d), lambda i: (0, 0))],
        out_specs=pl.BlockSpec((bt, d), lambda i: (i, 0)),
        compiler_params=_cparams(("parallel",)),
    )(x, g)


def _rms_bwd(dh, x, g, dres, name):
    t, d = x.shape
    bt = _block(t, 256, 16)
    want_dx = dres is not None

    def body(*refs):
        if want_dx:
            dh_ref, x_ref, g_ref, dres_ref, dx_ref, dxb_ref, dg_ref = refs
        else:
            dh_ref, x_ref, g_ref, dg_ref = refs
        xv = x_ref[...]
        r = lax.rsqrt(jnp.mean(xv * xv, axis=-1, keepdims=True) + EPS)
        xhat = xv * r
        dhv = dh_ref[...]

        @pl.when(pl.program_id(0) == 0)
        def _():
            dg_ref[...] = jnp.zeros_like(dg_ref)

        dg_ref[...] += jnp.sum(dhv * xhat, axis=0, keepdims=True)
        if want_dx:
            dxhat = dhv * g_ref[...]
            dx = dres_ref[...] + r * (dxhat - xhat * jnp.mean(dxhat * xhat, axis=-1, keepdims=True))
            dx_ref[...] = dx
            dxb_ref[...] = dx.astype(BF16)

    row = pl.BlockSpec((bt, d), lambda i: (i, 0))
    vec = pl.BlockSpec((1, d), lambda i: (0, 0))
    if want_dx:
        return pl.pallas_call(
            body, name=name, grid=(t // bt,),
            out_shape=(jax.ShapeDtypeStruct((t, d), F32), jax.ShapeDtypeStruct((t, d), BF16),
                       jax.ShapeDtypeStruct((1, d), F32)),
            in_specs=[row, row, vec, row], out_specs=(row, row, vec),
            compiler_params=_cparams(("arbitrary",)),
        )(dh, x, g, dres)
    return pl.pallas_call(
        body, name=name, grid=(t // bt,), out_shape=jax.ShapeDtypeStruct((1, d), F32),
        in_specs=[row, row, vec], out_specs=vec,
        compiler_params=_cparams(("arbitrary",)),
    )(dh, x, g)


def _final_loss(x, target, g):
    t, d = x.shape
    bt = _block(t, 256, 16)

    def body(x_ref, t_ref, g_ref, dx_ref, dxb_ref, dg_ref, loss_ref):
        xv = x_ref[...]
        gv = g_ref[...]
        r = lax.rsqrt(jnp.mean(xv * xv, axis=-1, keepdims=True) + EPS)
        xhat = xv * r
        err = xhat * gv - t_ref[...]

        @pl.when(pl.program_id(0) == 0)
        def _():
            dg_ref[...] = jnp.zeros_like(dg_ref)
            loss_ref[...] = jnp.zeros_like(loss_ref)

        loss_ref[...] += 0.5 * jnp.sum(jnp.mean(err * err, axis=-1, keepdims=True), axis=0, keepdims=True)
        dy = err * (1.0 / d)
        dg_ref[...] += jnp.sum(dy * xhat, axis=0, keepdims=True)
        dxhat = dy * gv
        dx = r * (dxhat - xhat * jnp.mean(dxhat * xhat, axis=-1, keepdims=True))
        dx_ref[...] = dx
        dxb_ref[...] = dx.astype(BF16)

    row = pl.BlockSpec((bt, d), lambda i: (i, 0))
    vec = pl.BlockSpec((1, d), lambda i: (0, 0))
    return pl.pallas_call(
        body, name="final_loss", grid=(t // bt,),
        out_shape=(jax.ShapeDtypeStruct((t, d), F32), jax.ShapeDtypeStruct((t, d), BF16),
                   jax.ShapeDtypeStruct((1, d), F32), jax.ShapeDtypeStruct((1, LANES), F32)),
        in_specs=[row, row, vec], out_specs=(row, row, vec, pl.BlockSpec((1, LANES), lambda i: (0, 0))),
        compiler_params=_cparams(("arbitrary",)),
    )(x, target, g)


GU_COLS = GATE_WIDTH + POOL_WIDTH
U_BLK = GATE_WIDTH // POOL_WIDTH


def _shift_down(a, k, row):
    return jnp.where(row >= k, pltpu.roll(a, k, 0), 0.0)


def _shift_up(a, k, row):
    n = a.shape[0]
    return jnp.where(row < n - k, pltpu.roll(a, n - k, 0), 0.0)


def _window_delta(u, w, row):
    s, k = u, 1
    while k < w:
        s = s + _shift_down(s, k, row)
        k *= 2
    cnt = jnp.minimum(row + 1, w).astype(F32)
    return s / cnt - u, cnt


def _pool_fwd(gu, pool_w, pool_scale):
    b, s, _ = gu.shape

    def body(u_ref, pw_ref, sc_ref, y_ref):
        row = lax.broadcasted_iota(jnp.int32, (s, POOL_GC), 0)
        for g, w in enumerate(POOL_WINDOWS):
            cols = slice(g * POOL_GC, (g + 1) * POOL_GC)
            d, _ = _window_delta(u_ref[0, :, cols], w, row)
            z = jnp.dot(d.astype(BF16), pw_ref[g].astype(BF16), preferred_element_type=F32)
            y_ref[0, :, cols] = (z * sc_ref[:, cols]).astype(BF16)

    return pl.pallas_call(
        body, name="pool_fwd", out_shape=jax.ShapeDtypeStruct((b, s, POOL_WIDTH), BF16), grid=(b,),
        in_specs=[pl.BlockSpec((1, s, POOL_WIDTH), lambda i: (i, 0, U_BLK)),
                  pl.BlockSpec((4, POOL_GC, POOL_GC), lambda i: (0, 0, 0)),
                  pl.BlockSpec((1, POOL_WIDTH), lambda i: (0, 0))],
        out_specs=pl.BlockSpec((1, s, POOL_WIDTH), lambda i: (i, 0, 0)),
        compiler_params=_cparams(("parallel",)),
    )(gu, pool_w, pool_scale)


def _pool_bwd(gu, dy, pool_w, pool_scale, dgu):
    b, s, _ = gu.shape

    def body(u_ref, dy_ref, pw_ref, sc_ref, dgu_in, du_ref, dpw_ref, dsc_ref):
        del dgu_in

        @pl.when(pl.program_id(0) == 0)
        def _():
            dpw_ref[...] = jnp.zeros_like(dpw_ref)
            dsc_ref[...] = jnp.zeros_like(dsc_ref)

        row = lax.broadcasted_iota(jnp.int32, (s, POOL_GC), 0)
        for g, w in enumerate(POOL_WINDOWS):
            cols = slice(g * POOL_GC, (g + 1) * POOL_GC)
            d, cnt = _window_delta(u_ref[0, :, cols], w, row)
            db = d.astype(BF16)
            pw = pw_ref[g].astype(BF16)
            z = jnp.dot(db, pw, preferred_element_type=F32)
            dyv = dy_ref[0, :, cols]
            dsc_ref[:, cols] += jnp.sum(dyv * z, axis=0, keepdims=True)
            dz = (dyv * sc_ref[:, cols]).astype(BF16)
            dpw_ref[g] += lax.dot_general(db, dz, (((0,), (0,)), ((), ())), preferred_element_type=F32)
            dd = lax.dot_general(dz, pw, (((1,), (1,)), ((), ())), preferred_element_type=F32)
            acc, k = dd / cnt, 1
            while k < w:
                acc = acc + _shift_up(acc, k, row)
                k *= 2
            du_ref[0, :, cols] = (acc - dd).astype(BF16)

    return pl.pallas_call(
        body, name="pool_bwd", grid=(b,),
        out_shape=(jax.ShapeDtypeStruct((b, s, GU_COLS), BF16), jax.ShapeDtypeStruct((4, POOL_GC, POOL_GC), F32),
                   jax.ShapeDtypeStruct((1, POOL_WIDTH), F32)),
        in_specs=[pl.BlockSpec((1, s, POOL_WIDTH), lambda i: (i, 0, U_BLK)),
                  pl.BlockSpec((1, s, POOL_WIDTH), lambda i: (i, 0, 0)),
                  pl.BlockSpec((4, POOL_GC, POOL_GC), lambda i: (0, 0, 0)),
                  pl.BlockSpec((1, POOL_WIDTH), lambda i: (0, 0)), ANY],
        out_specs=(pl.BlockSpec((1, s, POOL_WIDTH), lambda i: (i, 0, U_BLK)),
                   pl.BlockSpec((4, POOL_GC, POOL_GC), lambda i: (0, 0, 0)),
                   pl.BlockSpec((1, POOL_WIDTH), lambda i: (0, 0))),
        input_output_aliases={4: 0},
        compiler_params=_cparams(("arbitrary",)),
    )(gu, dy, pool_w, pool_scale, dgu)


def _forget_cumsum(f, bias):
    b, s, c = f.shape

    def body(f_ref, b_ref, c_ref):
        row = lax.broadcasted_iota(jnp.int32, (s, LANES), 0)
        z = f_ref[0] + b_ref[...]
        acc = jnp.minimum(z, 0.0) - jnp.log(1.0 + jnp.exp(-jnp.abs(z)))
        k = 1
        while k < s:
            acc = acc + _shift_down(acc, k, row)
            k *= 2
        c_ref[0] = acc

    return pl.pallas_call(
        body, name="forget_cumsum", out_shape=jax.ShapeDtypeStruct((b, s, c), F32), grid=(b, c // LANES),
        in_specs=[pl.BlockSpec((1, s, LANES), lambda i, j: (i, 0, j)), pl.BlockSpec((1, LANES), lambda i, j: (0, j))],
        out_specs=pl.BlockSpec((1, s, LANES), lambda i, j: (i, 0, j)),
        compiler_params=_cparams(("parallel", "parallel")),
    )(f, bias)


def _forget_bwd(dc, f, bias):
    b, s, _ = f.shape

    def body(dc_ref, f_ref, b_ref, df_ref, db_ref):
        @pl.when(pl.program_id(0) == 0)
        def _():
            db_ref[...] = jnp.zeros_like(db_ref)

        row = lax.broadcasted_iota(jnp.int32, (s, LANES), 0)
        acc, k = dc_ref[0], 1
        while k < s:
            acc = acc + _shift_up(acc, k, row)
            k *= 2
        z = f_ref[0] + b_ref[...]
        df = acc / (1.0 + jnp.exp(z))
        db_ref[...] += jnp.sum(df, axis=0, keepdims=True)
        df_ref[0] = df.astype(BF16)

    blk = pl.BlockSpec((1, s, LANES), lambda i: (i, 0, 0))
    vec = pl.BlockSpec((1, LANES), lambda i: (0, 0))
    return pl.pallas_call(
        body, name="forget_bwd", grid=(b,),
        out_shape=(jax.ShapeDtypeStruct((b, s, LANES), BF16), jax.ShapeDtypeStruct((1, LANES), F32)),
        in_specs=[blk, blk, vec], out_specs=(blk, vec),
        compiler_params=_cparams(("arbitrary",)),
    )(dc, f, bias)


KV_BLK0 = 2
PAIRS = FOX_HEADS // 2
FOX_SCALE = FOX_DH ** -0.5
NT_DIMS = (((1,), (1,)), ((), ()))
TN_DIMS = (((0,), (0,)), ((), ()))


def _stack_heads(v):
    head = lax.broadcasted_iota(jnp.int32, v.shape, 1) // FOX_DH
    zero = jnp.zeros_like(v)
    return jnp.concatenate([jnp.where(head == 0, v, zero), jnp.where(head == 1, v, zero)], axis=0)


def _stack_cols(v):
    return jnp.concatenate([v[:, 0:1], v[:, FOX_DH:FOX_DH + 1]], axis=0)


def _unstack(t, blk):
    head = lax.broadcasted_iota(jnp.int32, (blk, LANES), 1) // FOX_DH
    return jnp.where(head == 0, t[:blk], t[blk:])


def _fox_scores(q_all, kblk, row_bias, cr_ref, kb, masked, blk):
    top = lax.broadcasted_iota(jnp.int32, (2 * blk, 1), 0) < blk
    s = lax.dot_general(q_all, kblk, NT_DIMS, preferred_element_type=F32)
    s = s + (row_bias - jnp.where(top, cr_ref[0, 0, kb], cr_ref[0, 1, kb]))
    if masked:
        r = lax.broadcasted_iota(jnp.int32, (2 * blk, blk), 0)
        keep = jnp.where(r >= blk, r - blk, r) >= lax.broadcasted_iota(jnp.int32, (2 * blk, blk), 1)
        s = jnp.where(keep, s, NEG_INF)
    return s


def _fox_fwd(qkv, c_exp, c_row, ex=None):
    b, s, _ = qkv.shape
    blk = min(ATT_BLOCK, s)
    nq = s // blk

    def body(q_ref, kv_ref, cc_ref, cr_ref, o_ref, ob_ref, lse_ref):
        qi = pl.program_id(2)
        q_all = _stack_heads(q_ref[0] * FOX_SCALE)
        cq = _stack_cols(cc_ref[0])

        def step(kb, carry, masked):
            m, l, acc = carry
            rows = pl.ds(pl.multiple_of(kb * blk, blk), blk)
            sc = _fox_scores(q_all, kv_ref[0, rows, :LANES], cq, cr_ref, kb, masked, blk)
            m_new = jnp.maximum(m, jnp.max(sc, axis=-1, keepdims=True))
            p = jnp.exp(sc - m_new)
            alpha = jnp.exp(m - m_new)
            l = alpha * l + jnp.sum(p, axis=-1, keepdims=True)
            acc = alpha * acc + jnp.dot(p.astype(BF16), kv_ref[0, rows, LANES:], preferred_element_type=F32)
            return m_new, l, acc

        init = (jnp.full((2 * blk, 1), NEG_INF, F32), jnp.zeros((2 * blk, 1), F32), jnp.zeros((2 * blk, LANES), F32))
        m, l, acc = step(qi, lax.fori_loop(0, qi, functools.partial(step, masked=False), init), True)
        o = _unstack(acc / l, blk)
        o_ref[0] = o
        ob_ref[0] = o.astype(BF16)
        lse_ref[0] = _unstack(jnp.broadcast_to(m + jnp.log(l), (2 * blk, LANES)), blk)

    tile = pl.BlockSpec((1, blk, LANES), lambda i, h, q: (i, q, h))
    kvspec = pl.BlockSpec((1, s, 2 * LANES), lambda i, h, q: (i, 0, KV_BLK0 + h))
    shape = jax.ShapeDtypeStruct((b, s, FOX_WIDTH), F32)
    return _hosted_call(
        body, ex, name="fox_fwd", out_shape=(shape, jax.ShapeDtypeStruct((b, s, FOX_WIDTH), BF16), shape),
        grid=(b, PAIRS, nq),
        in_specs=[tile, kvspec, tile, pl.BlockSpec((1, 2, nq, 1, blk), lambda i, h, q: (i, h, 0, 0, 0))],
        out_specs=(tile, tile, tile), args=(qkv, qkv, c_exp, c_row))


def _fox_bwd_q(qkv, c_exp, c_row, lse, o, do, ex=None):
    b, s, _ = qkv.shape
    blk = min(ATT_BLOCK, s)
    nq = s // blk

    def body(q_ref, kv_ref, cc_ref, cr_ref, lse_ref, o_ref, do_ref, dq_ref, dl_ref, dcq_ref):
        qi = pl.program_id(2)
        q_all = _stack_heads(q_ref[0] * FOX_SCALE)
        dov = do_ref[0]
        do_all = _stack_heads(dov.astype(BF16))
        delta = jnp.sum(_stack_heads(dov * o_ref[0]), axis=-1, keepdims=True)
        bias = _stack_cols(cc_ref[0]) - _stack_cols(lse_ref[0])

        def step(kb, carry, masked):
            acc, dcq = carry
            rows = pl.ds(pl.multiple_of(kb * blk, blk), blk)
            kblk = kv_ref[0, rows, :LANES]
            p = jnp.exp(_fox_scores(q_all, kblk, bias, cr_ref, kb, masked, blk))
            dp = lax.dot_general(do_all, kv_ref[0, rows, LANES:], NT_DIMS, preferred_element_type=F32)
            ds = p * (dp - delta)
            acc = acc + jnp.dot(ds.astype(BF16), kblk, preferred_element_type=F32)
            return acc, dcq + jnp.sum(ds, axis=-1, keepdims=True)

        init = (jnp.zeros((2 * blk, LANES), F32), jnp.zeros((2 * blk, 1), F32))
        acc, dcq = step(qi, lax.fori_loop(0, qi, functools.partial(step, masked=False), init), True)
        dq_ref[0] = (_unstack(acc, blk) * FOX_SCALE).astype(BF16)
        dl_ref[0] = _unstack(jnp.broadcast_to(delta, (2 * blk, LANES)), blk)
        dcq_ref[0] = _unstack(jnp.broadcast_to(dcq, (2 * blk, LANES)), blk)

    tile = pl.BlockSpec((1, blk, LANES), lambda i, h, q: (i, q, h))
    kvspec = pl.BlockSpec((1, s, 2 * LANES), lambda i, h, q: (i, 0, KV_BLK0 + h))
    shape = jax.ShapeDtypeStruct((b, s, FOX_WIDTH), F32)
    return _hosted_call(
        body, ex, name="fox_bwd_q", grid=(b, PAIRS, nq),
        out_shape=(jax.ShapeDtypeStruct(qkv.shape, BF16), shape, shape),
        in_specs=[tile, kvspec, tile, pl.BlockSpec((1, 2, nq, 1, blk), lambda i, h, q: (i, h, 0, 0, 0)),
                  tile, tile, tile],
        out_specs=(tile, tile, tile), args=(qkv, qkv, c_exp, c_row, lse, o, do))


def _fox_bwd_kv(qkv, c_exp, c_row, lse, delta, do, dqkv, ex=None):
    b, s, _ = qkv.shape
    blk = min(ATT_BLOCK, s)
    nq = s // blk

    def body(q_ref, kv_ref, cc_ref, cr_ref, lse_ref, dl_ref, do_ref, dqkv_in, dkv_ref, dc_ref):
        del dqkv_in
        ki = pl.program_id(2)
        kblk = kv_ref[0, :, :LANES]
        vblk = kv_ref[0, :, LANES:]

        def step(qb, carry, masked):
            dk, dv, dc = carry
            rows = pl.ds(pl.multiple_of(qb * blk, blk), blk)
            q_all = _stack_heads(q_ref[0, rows, :] * FOX_SCALE)
            do_all = _stack_heads(do_ref[0, rows, :].astype(BF16))
            bias = _stack_cols(cc_ref[0, rows, :]) - _stack_cols(lse_ref[0, rows, :])
            delta = _stack_cols(dl_ref[0, rows, :])
            p = jnp.exp(_fox_scores(q_all, kblk, bias, cr_ref, 0, masked, blk))
            dv = dv + lax.dot_general(p.astype(BF16), do_all, TN_DIMS, preferred_element_type=F32)
            dp = lax.dot_general(do_all, vblk, NT_DIMS, preferred_element_type=F32)
            ds = p * (dp - delta)
            dk = dk + lax.dot_general(ds.astype(BF16), q_all, TN_DIMS, preferred_element_type=F32)
            col = jnp.concatenate([jnp.sum(ds[:blk], axis=0, keepdims=True), jnp.sum(ds[blk:], axis=0, keepdims=True)],
                                  axis=0)
            return dk, dv, dc - col

        zero = jnp.zeros((blk, LANES), F32)
        carry = step(ki, (zero, zero, jnp.zeros((2, blk), F32)), True)
        dk, dv, dc = lax.fori_loop(ki + 1, nq, functools.partial(step, masked=False), carry)
        dkv_ref[0, :, :LANES] = dk.astype(BF16)
        dkv_ref[0, :, LANES:] = dv.astype(BF16)
        dc_ref[0, 0, 0] = dc[0:1]
        dc_ref[0, 1, 0] = dc[1:2]

    full = pl.BlockSpec((1, s, LANES), lambda i, h, k: (i, 0, h))
    kvtile = pl.BlockSpec((1, blk, 2 * LANES), lambda i, h, k: (i, k, KV_BLK0 + h))
    crow = pl.BlockSpec((1, 2, 1, 1, blk), lambda i, h, k: (i, h, k, 0, 0))
    return _hosted_call(
        body, ex, name="fox_bwd_kv", grid=(b, PAIRS, nq),
        out_shape=(jax.ShapeDtypeStruct(qkv.shape, BF16), jax.ShapeDtypeStruct(c_row.shape, F32)),
        in_specs=[full, kvtile, full, crow, full, full, full, ANY],
        out_specs=(kvtile, crow), aliases={7: 0}, args=(qkv, qkv, c_exp, c_row, lse, delta, do, dqkv))


def _sigmoid(z):
    return 1.0 / (1.0 + jnp.exp(-z))


def _mix_fwd(gu, b_gate, y_pool, y_fox):
    t = gu.shape[0]
    bt = _block(t, 256, 16)

    def body(gp_ref, gf_ref, bp_ref, bf_ref, yp_ref, yf_ref, o_ref):
        gp = _sigmoid(gp_ref[...] + bp_ref[...])
        gf = _sigmoid(gf_ref[...] + bf_ref[...])
        o_ref[...] = (gp * yp_ref[...] + gf * yf_ref[...]).astype(BF16)

    col = lambda j: pl.BlockSpec((bt, D_MODEL), lambda i: (i, j))
    vec = lambda j: pl.BlockSpec((1, D_MODEL), lambda i: (0, j))
    return pl.pallas_call(
        body, name="mix_fwd", out_shape=jax.ShapeDtypeStruct((t, D_MODEL), BF16), grid=(t // bt,),
        in_specs=[col(0), col(1), vec(0), vec(1), col(0), col(0)], out_specs=col(0),
        compiler_params=_cparams(("parallel",)),
    )(gu, gu, b_gate, b_gate, y_pool, y_fox)


def _mix_bwd(gu, b_gate, y_pool, y_fox, dmix):
    t = gu.shape[0]
    bt = _block(t, 256, 16)

    def body(gp_ref, gf_ref, bp_ref, bf_ref, yp_ref, yf_ref, dm_ref, dyp_ref, dyf_ref, dgl_ref, db_ref):
        @pl.when(pl.program_id(0) == 0)
        def _():
            db_ref[...] = jnp.zeros_like(db_ref)

        dm = dm_ref[...]
        gp = _sigmoid(gp_ref[...] + bp_ref[...])
        gf = _sigmoid(gf_ref[...] + bf_ref[...])
        dyp_ref[...] = (dm * gp).astype(BF16)
        dyf_ref[...] = (dm * gf).astype(BF16)
        dlp = dm * yp_ref[...] * gp * (1.0 - gp)
        dlf = dm * yf_ref[...] * gf * (1.0 - gf)
        dgl_ref[:, :D_MODEL] = dlp.astype(BF16)
        dgl_ref[:, D_MODEL:] = dlf.astype(BF16)
        db_ref[:, :D_MODEL] += jnp.sum(dlp, axis=0, keepdims=True)
        db_ref[:, D_MODEL:] += jnp.sum(dlf, axis=0, keepdims=True)

    col = lambda j: pl.BlockSpec((bt, D_MODEL), lambda i: (i, j))
    vec = lambda j: pl.BlockSpec((1, D_MODEL), lambda i: (0, j))
    wide = pl.BlockSpec((bt, GATE_WIDTH), lambda i: (i, 0))
    return pl.pallas_call(
        body, name="mix_bwd", grid=(t // bt,),
        out_shape=(jax.ShapeDtypeStruct((t, D_MODEL), BF16), jax.ShapeDtypeStruct((t, D_MODEL), BF16),
                   jax.ShapeDtypeStruct((t, GU_COLS), BF16), jax.ShapeDtypeStruct((1, GATE_WIDTH), F32)),
        in_specs=[col(0), col(1), vec(0), vec(1), col(0), col(0), col(0)],
        out_specs=(col(0), col(0), wide, pl.BlockSpec((1, GATE_WIDTH), lambda i: (0, 0))),
        compiler_params=_cparams(("arbitrary",)),
    )(gu, gu, b_gate, b_gate, y_pool, y_fox, dmix)


X_SCALE = X_DH ** -0.5


def _xattn_probs(qh, kh):
    s = lax.dot_general(qh, kh, NT_DIMS, preferred_element_type=F32) * X_SCALE
    e = jnp.exp(s - jnp.max(s, axis=-1, keepdims=True))
    return e / jnp.sum(e, axis=-1, keepdims=True)


def _xattn_fwd(q, kv):
    b, s, _ = q.shape
    m = kv.shape[1]
    bq = _block(s, 512, 16)

    def body(q_ref, kv_ref, o_ref):
        for h in range(X_HEADS):
            cols = slice(h * X_DH, (h + 1) * X_DH)
            p = _xattn_probs(q_ref[0, :, cols], kv_ref[0, :, cols])
            vh = kv_ref[0, :, X_WIDTH + h * X_DH:X_WIDTH + (h + 1) * X_DH]
            o_ref[0, :, cols] = jnp.dot(p.astype(BF16), vh, preferred_element_type=F32).astype(BF16)

    return pl.pallas_call(
        body, name="xattn_fwd", out_shape=jax.ShapeDtypeStruct((b, s, X_WIDTH), BF16), grid=(b, s // bq),
        in_specs=[pl.BlockSpec((1, bq, X_WIDTH), lambda i, j: (i, j, 0)),
                  pl.BlockSpec((1, m, 2 * X_WIDTH), lambda i, j: (i, 0, 0))],
        out_specs=pl.BlockSpec((1, bq, X_WIDTH), lambda i, j: (i, j, 0)),
        compiler_params=_cparams(("parallel", "parallel")),
    )(q, kv)


def _xattn_bwd(q, kv, do):
    b, s, _ = q.shape
    m = kv.shape[1]
    bq = _block(s, 512, 16)

    def body(q_ref, kv_ref, do_ref, dq_ref, dkv_ref):
        @pl.when(pl.program_id(1) == 0)
        def _():
            dkv_ref[...] = jnp.zeros_like(dkv_ref)

        for h in range(X_HEADS):
            cols = slice(h * X_DH, (h + 1) * X_DH)
            vcols = slice(X_WIDTH + h * X_DH, X_WIDTH + (h + 1) * X_DH)
            qh, kh, vh, doh = q_ref[0, :, cols], kv_ref[0, :, cols], kv_ref[0, :, vcols], do_ref[0, :, cols]
            p = _xattn_probs(qh, kh)
            dkv_ref[0, :, vcols] += lax.dot_general(p.astype(BF16), doh, TN_DIMS, preferred_element_type=F32)
            dp = lax.dot_general(doh, vh, NT_DIMS, preferred_element_type=F32)
            ds = (p * (dp - jnp.sum(p * dp, axis=-1, keepdims=True)) * X_SCALE).astype(BF16)
            dq_ref[0, :, cols] = jnp.dot(ds, kh, preferred_element_type=F32).astype(BF16)
            dkv_ref[0, :, cols] += lax.dot_general(ds, qh, TN_DIMS, preferred_element_type=F32)

    tile = pl.BlockSpec((1, bq, X_WIDTH), lambda i, j: (i, j, 0))
    mem = pl.BlockSpec((1, m, 2 * X_WIDTH), lambda i, j: (i, 0, 0))
    return pl.pallas_call(
        body, name="xattn_bwd", grid=(b, s // bq),
        out_shape=(jax.ShapeDtypeStruct((b, s, X_WIDTH), BF16), jax.ShapeDtypeStruct((b, m, 2 * X_WIDTH), F32)),
        in_specs=[tile, mem, tile], out_specs=(tile, mem),
        compiler_params=_cparams(("parallel", "arbitrary")),
    )(q, kv, do)


def _swiglu_fwd(gu):
    t = gu.shape[0]
    bt = _block(t, 256, 16)

    def body(gt_ref, up_ref, o_ref):
        gt = gt_ref[...]
        o_ref[...] = (gt * _sigmoid(gt) * up_ref[...]).astype(BF16)

    col = lambda j: pl.BlockSpec((bt, D_FF), lambda i: (i, j))
    return pl.pallas_call(
        body, name="swiglu_fwd", out_shape=jax.ShapeDtypeStruct((t, D_FF), BF16), grid=(t // bt,),
        in_specs=[col(0), col(1)], out_specs=col(0),
        compiler_params=_cparams(("parallel",)),
    )(gu, gu)


def _swiglu_bwd(gu, dact):
    t = gu.shape[0]
    bt = _block(t, 256, 16)

    def body(gt_ref, up_ref, da_ref, o_ref):
        gt = gt_ref[...]
        da = da_ref[...]
        sg = _sigmoid(gt)
        silu = gt * sg
        o_ref[:, :D_FF] = (da * up_ref[...] * (sg + silu * (1.0 - sg))).astype(BF16)
        o_ref[:, D_FF:] = (da * silu).astype(BF16)

    col = lambda j: pl.BlockSpec((bt, D_FF), lambda i: (i, j))
    return pl.pallas_call(
        body, name="swiglu_bwd", out_shape=jax.ShapeDtypeStruct((t, 2 * D_FF), BF16), grid=(t // bt,),
        in_specs=[col(0), col(1), col(0)], out_specs=pl.BlockSpec((bt, 2 * D_FF), lambda i: (i, 0)),
        compiler_params=_cparams(("parallel",)),
    )(gu, gu, dact)


def _stack_of(w, axis):
    r, c = w.shape
    if axis == 0:
        return w.reshape(N_CHIPS, r // N_CHIPS, c)
    return w.reshape(r, N_CHIPS, c // N_CHIPS).transpose(1, 0, 2)


def _stack_t(w3):
    n, r, c = w3.shape
    return w3.transpose(0, 2, 1).reshape(n * c, r)


def _pair_rows(k, v):
    c = k.shape[1]
    return jnp.stack([k.reshape(PAIRS, LANES, c), v.reshape(PAIRS, LANES, c)], axis=1).reshape(2 * FOX_WIDTH, c)


def _unpair_rows(kv):
    c = kv.shape[1]
    kv = kv.reshape(PAIRS, 2, LANES, c)
    return kv[:, 0].reshape(FOX_WIDTH, c), kv[:, 1].reshape(FOX_WIDTH, c)


def _input_grad(parts, weights_t, ex):
    t = parts[0].shape[0]
    d = weights_t[0].shape[1]
    bm = _block(t, 512, 16)
    n = len(parts)

    def body(*refs):
        acc = None
        for a_ref, b_ref in zip(refs[:n], refs[n:2 * n]):
            term = jnp.dot(a_ref[...], b_ref[...], preferred_element_type=F32)
            acc = term if acc is None else acc + term
        refs[2 * n][...] = acc

    (out,), moved = _hosted_call(
        body, ex, name="d_h", grid=(t // bm,), out_shape=(jax.ShapeDtypeStruct((t, d), F32),),
        in_specs=[pl.BlockSpec((bm, p.shape[1]), lambda i: (i, 0)) for p in parts]
        + [pl.BlockSpec(w.shape, lambda i: (0, 0)) for w in weights_t],
        out_specs=(pl.BlockSpec((bm, d), lambda i: (i, 0)),), args=tuple(parts) + tuple(weights_t))
    return out, moved


def _step(x, mem, loss_target, weights, moments_m, moments_v):
    nb, s, d = x.shape
    n_mem = mem.shape[1]
    t = nb * s
    blk = min(ATT_BLOCK, s)
    x2 = x.reshape(t, d)
    mem2 = mem.reshape(nb * n_mem, d)
    tgt2 = loss_target.reshape(t, d)

    def shard2d(a, n):
        a = a.reshape(a.shape[1:])
        return a.T if n == "w_in" else a

    def unshard(a, n):
        return (a.T if n == "w_in" else a)[None]

    local = {n: shard2d(weights[n], n) for n, _, _ in SHARDED}

    names = [n for n, _, _ in SHARDED]
    later = [n for n in names if n != "w_in"]
    local_b = {n: local[n].astype(BF16) for n in names}
    w_in_stack, = _place_own(_run_exchange(_gather_exchange([local_b["w_in"]]), "gather_w_in"), [local_b["w_in"]])
    w_in_t = w_in_stack.reshape(IN_COLS, D_MODEL)
    w_gu_t = jnp.concatenate([w_in_t[2056:], w_in_t[:512]])
    w_qkv_t = jnp.concatenate([w_in_t[512:1024], _pair_rows(w_in_t[1024:1536], w_in_t[1536:2048])])
    w_f_t = jnp.pad(w_in_t[2048:2056], ((0, LANES - FOX_HEADS), (0, 0)))
    w_gu, w_qkv, w_f = w_gu_t.T, w_qkv_t.T, w_f_t.T

    g_mix, g_x, g_mem, g_ffn = (weights[n] for n in ("norm_mix_g", "norm_x_g", "norm_mem_g", "norm_ffn_g"))
    g_final = weights["norm_final_g"].reshape(1, d)
    pool_w = weights["pool_w"].reshape(4, POOL_GC, POOL_GC)
    pool_scale, b_gate = weights["pool_scale"], weights["b_gate"]
    b_f_pad = jnp.pad(weights["b_forget"], ((0, 0), (0, LANES - FOX_HEADS)))
    b_f_exp = jnp.repeat(weights["b_forget"], FOX_DH, axis=1)

    h = _rms_fwd(x2, g_mix, "norm_mix")
    gu = _mm(h, w_gu, bn=512, name="in_proj_gates_pool")
    qkv = _mm(h, w_qkv, out_dtype=BF16, bn=512, name="in_proj_qkv")
    f_pad = _mm(h, w_f, name="in_proj_forget")
    gu3, qkv3 = gu.reshape(nb, s, GU_COLS), qkv.reshape(nb, s, 3 * FOX_WIDTH)
    y = _pool_fwd(gu3, pool_w, pool_scale)
    f_exp = jnp.repeat(f_pad[:, :FOX_HEADS], FOX_DH, axis=1).reshape(nb, s, FOX_WIDTH)
    c_exp = _forget_cumsum(f_exp, b_f_exp)
    c_row = c_exp[:, :, ::FOX_DH].transpose(0, 2, 1).reshape(nb, FOX_HEADS, s // blk, 1, blk)
    (o, o_b, lse), gathered = _fox_fwd(qkv3, c_exp, c_row, ex=_gather_exchange([local_b[n] for n in later]))
    stacks = dict(zip(later, _place_own(gathered, [local_b[n] for n in later])))
    w_pool_out3, w_fox_out3, w_xo3, w_ffn_in3 = (stacks[n] for n in ("w_pool_out", "w_fox_out", "w_xo", "w_ffn_in"))
    w_out, w_xq, w_xkv, w_ffn_out = (stacks[n].reshape(-1, stacks[n].shape[2])
                                     for n in ("w_out", "w_xq", "w_xkv", "w_ffn_out"))
    y2, o2 = y.reshape(t, POOL_WIDTH), o_b.reshape(t, FOX_WIDTH)
    y_pool = _mm(y2, w_pool_out3, b_stack=True, name="pool_out")
    y_fox = _mm(o2, w_fox_out3, b_stack=True, name="fox_out")
    mix = _mix_fwd(gu, b_gate, y_pool, y_fox)
    x1 = _mm(mix, w_out, res=x2, name="mix_out")
    hx = _rms_fwd(x1, g_x, "norm_x")
    mem_n = _rms_fwd(mem2, g_mem, "norm_mem")
    qx = _mm(hx, w_xq, out_dtype=BF16, name="x_q")
    kv = _mm(mem_n, w_xkv, out_dtype=BF16, name="x_kv")
    qx3, kv3 = qx.reshape(nb, s, X_WIDTH), kv.reshape(nb, n_mem, 2 * X_WIDTH)
    ox = _xattn_fwd(qx3, kv3).reshape(t, X_WIDTH)
    x2_ = _mm(ox, w_xo3, b_stack=True, res=x1, name="x_out")
    hf = _rms_fwd(x2_, g_ffn, "norm_ffn")
    ffn = _mm(hf, w_ffn_in3, b_stack=True, bm=512, bn=1408, name="ffn_in")
    act = _swiglu_fwd(ffn)
    x3 = _mm(act, w_ffn_out, res=x2_, name="ffn_out")

    dx3, dx3_b, dg_final, loss_part = _final_loss(x3, tgt2, g_final)
    dw_ffn_out = _mm(act, dx3_b, ta=True, bm=1408, bn=512, bk=2048, name="d_w_ffn_out")
    dact = _mm(dx3_b, w_ffn_out.T, bn=1408, name="d_act")
    dffn = _swiglu_bwd(ffn, dact)
    dw_ffn_in = _mm(hf, dffn, ta=True, bm=512, bn=1408, bk=2048, out_stack=True, name="d_w_ffn_in")
    dhf = _mm(dffn, _stack_t(w_ffn_in3), bk=2816, name="d_hf")
    dx2, dx2_b, dg_ffn = _rms_bwd(dhf, x2_, g_ffn, dx3, "norm_ffn_bwd")

    dw_xo = _mm(ox, dx2_b, ta=True, bn=256, out_stack=True, name="d_w_xo")
    dox = _mm(dx2_b, _stack_t(w_xo3), out_dtype=BF16, name="d_ox").reshape(nb, s, X_WIDTH)
    dqx, dkv = _xattn_bwd(qx3, kv3, dox)
    dqx2, dkv2 = dqx.reshape(t, X_WIDTH), dkv.reshape(nb * n_mem, 2 * X_WIDTH)
    dw_xkv = _mm(mem_n, dkv2, ta=True, name="d_w_xkv")
    dmem_n = _mm(dkv2, w_xkv.T, name="d_mem_n")
    dg_mem = _rms_bwd(dmem_n, mem2, g_mem, None, "norm_mem_bwd")
    dw_xq = _mm(hx, dqx2, ta=True, name="d_w_xq")
    dhx = _mm(dqx2, w_xq.T, name="d_hx")
    dx1, dx1_b, dg_x = _rms_bwd(dhx, x1, g_x, dx2, "norm_x_bwd")

    dw_out = _mm(mix, dx1_b, ta=True, name="d_w_out")
    dmix = _mm(dx1_b, w_out.T, name="d_mix")
    dyp, dyf, dgu, db_gate = _mix_bwd(gu, b_gate, y_pool, y_fox, dmix)
    dw_pool_out = _mm(y2, dyp, ta=True, bn=256, out_stack=True, name="d_w_pool_out")
    dw_fox_out = _mm(o2, dyf, ta=True, bn=256, out_stack=True, name="d_w_fox_out")
    dy = _mm(dyp, _stack_t(w_pool_out3), name="d_y").reshape(nb, s, POOL_WIDTH)
    do = _mm(dyf, _stack_t(w_fox_out3), name="d_o").reshape(nb, s, FOX_WIDTH)
    dgu3, dpool_w, dpool_scale = _pool_bwd(gu3, dy, pool_w, pool_scale, dgu.reshape(nb, s, GU_COLS))
    core = lax.axis_index("c").astype(jnp.int32).reshape(1)
    grad_stacks = {"w_pool_out": dw_pool_out, "w_fox_out": dw_fox_out, "w_out": _stack_of(dw_out, 0),
                   "w_xq": _stack_of(dw_xq, 0), "w_xkv": _stack_of(dw_xkv, 0), "w_xo": dw_xo,
                   "w_ffn_in": dw_ffn_in, "w_ffn_out": _stack_of(dw_ffn_out, 0)}

    def chip_sums_of(group):
        theirs = _swap_halves([grad_stacks[n] for n in group], "swap_halves_" + group[0])
        return [_sum_halves(grad_stacks[n], t_, core, "sum_halves_" + n) for n, t_ in zip(group, theirs)]

    ffn_group = ["w_ffn_in", "w_ffn_out"]
    mid_group = ["w_pool_out", "w_fox_out", "w_out", "w_xq", "w_xkv", "w_xo"]
    chip_sums = dict(zip(ffn_group, chip_sums_of(ffn_group)))
    chip_sums.update(zip(mid_group, chip_sums_of(mid_group)))
    (dqkv3, delta, dc_q), ffn_slots = _fox_bwd_q(qkv3, c_exp, c_row, lse, o, do,
                                                 ex=_chips_exchange([chip_sums[n] for n in ffn_group]))
    (dqkv3, dc_row), mid_slots = _fox_bwd_kv(qkv3, c_exp, c_row, lse, delta, do, dqkv3,
                                             ex=_chips_exchange([chip_sums[n] for n in mid_group]))
    slots = dict(zip(ffn_group + mid_group, list(ffn_slots) + list(mid_slots)))
    dc = dc_row.reshape(nb, FOX_HEADS, s).transpose(0, 2, 1) + dc_q[:, :, ::FOX_DH]
    dc = jnp.pad(dc, ((0, 0), (0, 0), (0, LANES - FOX_HEADS)))
    df, db_f = _forget_bwd(dc, f_pad.reshape(nb, s, LANES), b_f_pad)
    dgu2, dqkv2, df2 = dgu3.reshape(t, GU_COLS), dqkv3.reshape(t, 3 * FOX_WIDTH), df.reshape(t, LANES)
    dw_gu_t = _mm(dgu2, h, ta=True, name="d_w_gates_pool")
    dw_qkv_t = _mm(dqkv2, h, ta=True, name="d_w_qkv")
    dw_f_t = _mm(df2, h, ta=True, name="d_w_forget")
    dw_k_t, dw_v_t = _unpair_rows(dw_qkv_t[FOX_WIDTH:])
    dw_in_t = jnp.concatenate([dw_gu_t[GATE_WIDTH:], dw_qkv_t[:FOX_WIDTH], dw_k_t, dw_v_t, dw_f_t[:FOX_HEADS],
                               dw_gu_t[:GATE_WIDTH]])
    grad_stacks["w_in"] = dw_in_t.reshape(N_CHIPS, IN_COLS // N_CHIPS, D_MODEL)
    chip_sums["w_in"], = chip_sums_of(["w_in"])
    dh, (slots["w_in"],) = _input_grad([dgu2, dqkv2, df2], [w_gu_t, w_qkv_t, w_f_t],
                                       _chips_exchange([chip_sums["w_in"]]))
    dx, _, dg_mix = _rms_bwd(dh, x2, g_mix, dx1, "norm_mix_bwd")

    place = jnp.stack([lax.axis_index("c"), 2 * lax.axis_index("x") + lax.axis_index("y")]).astype(jnp.int32)
    reduced = _join_halves([_sum_chips(slots[n], chip_sums[n], place, _by_rows(local[n].shape[0]), "sum_chips_" + n)
                            for n in names])

    small_grads = {"norm_mix_g": dg_mix, "b_forget": db_f[:, :FOX_HEADS], "b_gate": db_gate, "pool_w": dpool_w,
                   "pool_scale": dpool_scale, "norm_x_g": dg_x, "norm_mem_g": dg_mem, "norm_ffn_g": dg_ffn,
                   "norm_final_g": dg_final}
    parts = _gather_small(_pack_small([small_grads[n] for n, _ in SMALL], last=loss_part[0, 0]))
    sg, sd, sm, sv = _adamw_small(parts.reshape(N_DEV, SMALL_ROWS, LANES),
                                  _pack_small([weights[n] for n, _ in SMALL]),
                                  _pack_small([moments_m[n] for n, _ in SMALL]),
                                  _pack_small([moments_v[n] for n, _ in SMALL]))
    loss = sg.reshape(-1)[LOSS_POS]

    grads, deltas, new_m, new_v = {}, {}, {}, {}
    for (n, _), g_, d_, m_, v_ in zip(SMALL, _unpack_small(sg), _unpack_small(sd), _unpack_small(sm), _unpack_small(sv)):
        grads[n], deltas[n], new_m[n], new_v[n] = g_, d_, m_, v_
    def tiles_of(a):
        return a.transpose(2, 0, 1)

    def block_of(a3):
        return a3.transpose(1, 2, 0)

    for n, g_ in zip(names, reduced):
        if n == "w_in":
            g_ = g_.reshape(IN_COLS // N_CHIPS, 1, D_MODEL)
            w_, m_, v_ = tiles_of(weights[n]), tiles_of(moments_m[n]), tiles_of(moments_v[n])
            back = block_of
        else:
            w_, m_, v_ = local[n], shard2d(moments_m[n], n), shard2d(moments_v[n], n)
            back = functools.partial(unshard, n=n)
        d_, m_, v_ = _adamw(w_, g_, m_, v_, "adamw_" + n)
        grads[n], deltas[n], new_m[n], new_v[n] = (back(a) for a in (g_, d_, m_, v_))
    return loss, dx.reshape(nb, s, d), grads, deltas, new_m, new_v


def kernel(x, mem, norm_mix_g, w_in, b_forget, b_gate, pool_w, pool_scale, w_pool_out, w_fox_out, w_out, norm_x_g, norm_mem_g, w_xq, w_xkv, w_xo, norm_ffn_g, w_ffn_in, w_ffn_out, norm_final_g, loss_target, m_norm_mix_g, m_w_in, m_b_forget, m_b_gate, m_pool_w, m_pool_scale, m_w_pool_out, m_w_fox_out, m_w_out, m_norm_x_g, m_norm_mem_g, m_w_xq, m_w_xkv, m_w_xo, m_norm_ffn_g, m_w_ffn_in, m_w_ffn_out, m_norm_final_g, v_norm_mix_g, v_w_in, v_b_forget, v_b_gate, v_pool_w, v_pool_scale, v_w_pool_out, v_w_fox_out, v_w_out, v_norm_x_g, v_norm_mem_g, v_w_xq, v_w_xkv, v_w_xo, v_norm_ffn_g, v_w_ffn_in, v_w_ffn_out, v_norm_final_g):
    given = dict(locals())
    weights = {n: given[n] for n in WEIGHT_ORDER}
    moments_m = {n: given["m_" + n] for n in WEIGHT_ORDER}
    moments_v = {n: given["v_" + n] for n in WEIGHT_ORDER}
    loss, grad_x, grads, deltas, new_m, new_v = _step(x, mem, loss_target, weights, moments_m, moments_v)
    return (loss, grad_x, *[grads[n] for n in WEIGHT_ORDER], *[deltas[n] for n in WEIGHT_ORDER],
            *[new_m[n] for n in WEIGHT_ORDER], *[new_v[n] for n in WEIGHT_ORDER])
```

```python
import functools
import math

import jax
import jax.numpy as jnp
from jax import lax
from jax.experimental import pallas as pl
from jax.experimental.pallas import tpu as pltpu

F32 = jnp.float32
BF16 = jnp.bfloat16
MESH = pl.DeviceIdType.MESH

D_MODEL = 1024
EPS = 1e-6
POOL_WINDOWS = (2, 4, 8, 16)
POOL_WIDTH = 512
POOL_GC = 128
FOX_HEADS = 8
FOX_DH = 64
FOX_WIDTH = 512
X_HEADS = 4
X_DH = 128
X_WIDTH = 512
D_FF = 2816
IN_COLS = 4104
GATE_WIDTH = 2048
ADAM_LR = 0.001
ADAM_B1 = 0.9
ADAM_B2 = 0.999
ADAM_EPS = 1e-08
ADAM_WD = 0.01
ADAM_STEP = 10

N_CHIPS = 4
N_DEV = 8
LANES = 128
VMEM_LIMIT_BYTES = 56 * 1024 * 1024
NEG_INF = -1e30
ATT_BLOCK = 512

SHARDED = (
    ("w_in", (1024, IN_COLS), 1),
    ("w_pool_out", (POOL_WIDTH, 1024), 1),
    ("w_fox_out", (FOX_WIDTH, 1024), 1),
    ("w_out", (1024, 1024), 0),
    ("w_xq", (1024, X_WIDTH), 0),
    ("w_xkv", (1024, 2 * X_WIDTH), 0),
    ("w_xo", (X_WIDTH, 1024), 1),
    ("w_ffn_in", (1024, 2 * D_FF), 1),
    ("w_ffn_out", (D_FF, 1024), 0),
)
SMALL = (
    ("norm_mix_g", (1, 1024)),
    ("b_forget", (1, 8)),
    ("b_gate", (1, 2048)),
    ("pool_w", (1, 4, 128, 128)),
    ("pool_scale", (1, 512)),
    ("norm_x_g", (1, 1024)),
    ("norm_mem_g", (1, 1024)),
    ("norm_ffn_g", (1, 1024)),
    ("norm_final_g", (1024,)),
)
WEIGHT_ORDER = ("norm_mix_g", "w_in", "b_forget", "b_gate", "pool_w", "pool_scale", "w_pool_out", "w_fox_out", "w_out",
                "norm_x_g", "norm_mem_g", "w_xq", "w_xkv", "w_xo", "norm_ffn_g", "w_ffn_in", "w_ffn_out", "norm_final_g")


def _round_up(n, m):
    return (n + m - 1) // m * m


SMALL_ELEMS = sum(math.prod(s) for _, s in SMALL)
SMALL_ROWS = _round_up(SMALL_ELEMS // LANES + 1, 8)
LOSS_POS = SMALL_ROWS * LANES - 1


def _cparams(sem=None):
    return pltpu.CompilerParams(dimension_semantics=sem, vmem_limit_bytes=VMEM_LIMIT_BYTES)


def _block(dim, pref, unit):
    if dim <= pref:
        return dim
    best = None
    for b in range(unit, pref + 1, unit):
        if dim % b == 0:
            best = b
    assert best is not None, (dim, pref, unit)
    return best


def _rows_block(rows, cols, unit=16, elems=1 << 19):
    return _block(rows, max(unit, elems // cols // unit * unit), unit)


def _pack_small(parts, last=None):
    flat = jnp.concatenate([p.reshape(-1).astype(F32) for p in parts])
    flat = jnp.pad(flat, (0, SMALL_ROWS * LANES - flat.shape[0]))
    if last is not None:
        flat = flat.at[LOSS_POS].set(last)
    return flat.reshape(SMALL_ROWS, LANES)


def _unpack_small(packed):
    flat = packed.reshape(-1)
    out, off = [], 0
    for _, shape in SMALL:
        n = math.prod(shape)
        out.append(flat[off:off + n].reshape(shape))
        off += n
    return out


def _my_place():
    return lax.axis_index("x"), lax.axis_index("y"), lax.axis_index("c")


def _other_chips(x, y):
    return [(1 - x, y), (x, 1 - y), (1 - x, 1 - y)]


def _chip(place):
    return 2 * place[0] + place[1]


ANY = pl.BlockSpec(memory_space=pl.ANY)


def _by_rows(rows):
    return rows % 32 == 0


def _half_shape(rows, cols):
    return (rows // 2, cols) if _by_rows(rows) else (rows, cols // 2)


def _core_half(ref, core, lead=()):
    rows, cols = ref.shape[-2:]
    if _by_rows(rows):
        return ref.at[(*lead, pl.ds(core * (rows // 2), rows // 2), slice(None))]
    return ref.at[(*lead, slice(None), pl.ds(core * (cols // 2), cols // 2))]


class _Exchange:
    def __init__(self, arrays, out_shapes, n_sems, start, finish):
        self.arrays, self.out_shapes, self.n_sems, self.start, self.finish = arrays, out_shapes, n_sems, start, finish

    def scratch(self):
        return [pltpu.SemaphoreType.DMA((self.n_sems,)), pltpu.SemaphoreType.DMA((self.n_sems,))]


def _run_exchange(ex, name):
    n = len(ex.arrays)

    def body(*refs):
        ins, outs, sems = refs[:n], refs[n:2 * n], refs[2 * n:]
        ex.start(ins, outs, *sems)
        ex.finish(ins, outs, *sems)

    return pl.pallas_call(
        body, name=name, out_shape=ex.out_shapes, in_specs=[ANY] * n, out_specs=[ANY] * n, scratch_shapes=ex.scratch(),
    )(*ex.arrays)


def _hosted_call(body, ex, *, name, grid, in_specs, out_specs, out_shape, args, scratch=()):
    n_in, n_out, n_scr = len(args), len(out_shape), len(scratch)
    if ex is None:
        outs = pl.pallas_call(
            body, name=name, grid=grid, out_shape=out_shape, in_specs=in_specs, out_specs=out_specs,
            scratch_shapes=list(scratch), compiler_params=_cparams(("arbitrary",) * len(grid)))(*args)
        return outs, None
    nc = len(ex.arrays)

    def full_body(*refs):
        ins, cins = refs[:n_in], refs[n_in:n_in + nc]
        outs, couts = refs[n_in + nc:n_in + nc + n_out], refs[n_in + nc + n_out:n_in + 2 * nc + n_out]
        rest = refs[n_in + 2 * nc + n_out:]
        scr, sems = rest[:n_scr], rest[n_scr:]
        first = functools.reduce(jnp.logical_and, [pl.program_id(a) == 0 for a in range(len(grid))])
        last = functools.reduce(jnp.logical_and, [pl.program_id(a) == grid[a] - 1 for a in range(len(grid))])

        @pl.when(first)
        def _():
            ex.start(cins, couts, *sems)

        body(*ins, *outs, *scr)

        @pl.when(last)
        def _():
            ex.finish(cins, couts, *sems)

    outs = pl.pallas_call(
        full_body, name=name, grid=grid, out_shape=list(out_shape) + list(ex.out_shapes),
        in_specs=list(in_specs) + [ANY] * nc, out_specs=list(out_specs) + [ANY] * nc,
        scratch_shapes=list(scratch) + ex.scratch(),
        compiler_params=_cparams(("arbitrary",) * len(grid)))(*args, *ex.arrays)
    return outs[:n_out], outs[n_out:]


def _gather_exchange(shards):
    n = len(shards)

    def copies(ins, outs, send_sems, recv_sems):
        x, y, c = _my_place()

        def half(k, chip, core):
            return _core_half(outs[k], core, lead=(_chip(chip),))

        def copy(k, slot, chip, core, to, src=None):
            return pltpu.make_async_remote_copy(
                src_ref=half(k, chip, core) if src is None else src, dst_ref=half(k, chip, core),
                send_sem=send_sems.at[6 * k + slot], recv_sem=recv_sems.at[6 * k + slot],
                device_id=to, device_id_type=MESH)

        return (x, y, c), copy

    def first_copies(ins, outs, send_sems, recv_sems):
        (x, y, c), copy = copies(ins, outs, send_sems, recv_sems)
        out = []
        for j, chip in enumerate(_other_chips(x, y)):
            for k in range(n):
                out.append(copy(k, j, (x, y), c, (*chip, c), src=_core_half(ins[k], c)))
        return out

    def start(ins, outs, send_sems, recv_sems):
        for cp in first_copies(ins, outs, send_sems, recv_sems):
            cp.start()

    def finish(ins, outs, send_sems, recv_sems):
        (x, y, c), copy = copies(ins, outs, send_sems, recv_sems)
        chips = _other_chips(x, y)
        passed = []
        for j, chip in enumerate(chips):
            for k in range(n):
                copy(k, j, chip, c, (x, y, c)).wait_recv()
                passed.append(copy(k, 3 + j, chip, c, (x, y, 1 - c)))
                passed[-1].start()
        for j, chip in enumerate(chips):
            for k in range(n):
                copy(k, 3 + j, chip, 1 - c, (x, y, c)).wait_recv()
        for cp in first_copies(ins, outs, send_sems, recv_sems) + passed:
            cp.wait_send()

    return _Exchange(list(shards), [jax.ShapeDtypeStruct((N_CHIPS,) + s.shape, s.dtype) for s in shards], 6 * n,
                     start, finish)


def _place_own(stacks, shards):
    me = 2 * lax.axis_index("x") + lax.axis_index("y")
    return [lax.dynamic_update_slice(others, mine[None], (me, 0, 0)) for others, mine in zip(stacks, shards)]


def _swap_halves(grads, name):
    n = len(grads)

    def body(*refs):
        ins, outs = refs[:n], refs[n:2 * n]
        send_sems, recv_sems = refs[2 * n:]
        x, y, c = _my_place()
        copies = []
        for k in range(n):
            copies.append(pltpu.make_async_remote_copy(
                src_ref=_core_half(ins[k], 1 - c, lead=(slice(None),)), dst_ref=outs[k],
                send_sem=send_sems.at[k], recv_sem=recv_sems.at[k], device_id=(x, y, 1 - c), device_id_type=MESH))
            copies[-1].start()
        for cp in copies:
            cp.wait()

    return pl.pallas_call(
        body, name=name,
        out_shape=[jax.ShapeDtypeStruct((N_CHIPS,) + _half_shape(*g.shape[1:]), g.dtype) for g in grads],
        in_specs=[ANY] * n, out_specs=[ANY] * n,
        scratch_shapes=[pltpu.SemaphoreType.DMA((n,)), pltpu.SemaphoreType.DMA((n,))],
    )(*grads)


def _chips_exchange(sums):
    n = len(sums)

    def sends(ins, outs, send_sems, recv_sems):
        x, y, c = _my_place()
        return [pltpu.make_async_remote_copy(
            src_ref=ins[k].at[_chip(chip)], dst_ref=outs[k].at[_chip((x, y))],
            send_sem=send_sems.at[3 * k + j], recv_sem=recv_sems.at[3 * k + j],
            device_id=(*chip, c), device_id_type=MESH)
            for j, chip in enumerate(_other_chips(x, y)) for k in range(n)]

    def start(ins, outs, send_sems, recv_sems):
        for cp in sends(ins, outs, send_sems, recv_sems):
            cp.start()

    def finish(ins, outs, send_sems, recv_sems):
        x, y, c = _my_place()
        for j, chip in enumerate(_other_chips(x, y)):
            for k in range(n):
                slot = outs[k].at[_chip(chip)]
                pltpu.make_async_remote_copy(
                    src_ref=slot, dst_ref=slot, send_sem=send_sems.at[3 * k + j], recv_sem=recv_sems.at[3 * k + j],
                    device_id=(x, y, c), device_id_type=MESH).wait_recv()
        for cp in sends(ins, outs, send_sems, recv_sems):
            cp.wait_send()

    return _Exchange(list(sums), [jax.ShapeDtypeStruct(s.shape, s.dtype) for s in sums], 3 * n, start, finish)


def _join_halves(shards):
    n = len(shards)

    def body(*refs):
        ins, outs = refs[:n], refs[n:2 * n]
        send_sems, recv_sems = refs[2 * n:]
        x, y, c = _my_place()
        sends = []
        for k in range(n):
            sends.append(pltpu.make_async_remote_copy(
                src_ref=_core_half(ins[k], c), dst_ref=_core_half(outs[k], c),
                send_sem=send_sems.at[k], recv_sem=recv_sems.at[k], device_id=(x, y, 1 - c), device_id_type=MESH))
            sends[-1].start()
        for k in range(n):
            theirs = _core_half(outs[k], 1 - c)
            pltpu.make_async_remote_copy(
                src_ref=theirs, dst_ref=theirs, send_sem=send_sems.at[k], recv_sem=recv_sems.at[k],
                device_id=(x, y, c), device_id_type=MESH).wait_recv()
        for cp in sends:
            cp.wait_send()

    return pl.pallas_call(
        body, name="join_halves",
        out_shape=[jax.ShapeDtypeStruct(s.shape, s.dtype) for s in shards],
        in_specs=[ANY] * n, out_specs=[ANY] * n,
        input_output_aliases={k: k for k in range(n)},
        scratch_shapes=[pltpu.SemaphoreType.DMA((n,)), pltpu.SemaphoreType.DMA((n,))],
    )(*shards)


def _gather_small(block):
    m_per = block.shape[0]

    def body(x_ref, out_ref, send_sems, recv_sems, local_sem):
        x, y, c = _my_place()
        me, sibling = (x, y, c), (x, y, 1 - c)
        chips = _other_chips(x, y)

        def rows(px, py, pc):
            return out_ref.at[pl.ds((4 * px + 2 * py + pc) * m_per, m_per), :]

        def copy(k, blk, to, src=None):
            return pltpu.make_async_remote_copy(
                src_ref=rows(*blk) if src is None else src, dst_ref=rows(*blk),
                send_sem=send_sems.at[k], recv_sem=recv_sems.at[k], device_id=to, device_id_type=MESH)

        mine = pltpu.make_async_copy(x_ref, rows(*me), local_sem)
        mine.start()
        first = [copy(0, me, sibling, src=x_ref)]
        first += [copy(1 + j, me, (*chip, c), src=x_ref) for j, chip in enumerate(chips)]
        for cp in first:
            cp.start()
        passed = [copy(4 + j, (*chip, c), sibling) for j, chip in enumerate(chips)]
        for j, chip in enumerate(chips):
            copy(1 + j, (*chip, c), me).wait_recv()
            passed[j].start()
        copy(0, sibling, me).wait_recv()
        for j, chip in enumerate(chips):
            copy(4 + j, (*chip, 1 - c), me).wait_recv()
        for cp in first + passed:
            cp.wait_send()
        mine.wait()

    return pl.pallas_call(
        body, name="gather_small",
        out_shape=jax.ShapeDtypeStruct((N_DEV * m_per, LANES), block.dtype),
        in_specs=[pl.BlockSpec(memory_space=pltpu.VMEM)],
        out_specs=pl.BlockSpec(memory_space=pltpu.VMEM),
        scratch_shapes=[pltpu.SemaphoreType.DMA((7,)), pltpu.SemaphoreType.DMA((7,)), pltpu.SemaphoreType.DMA],
    )(block)


def _sum_halves(grads, theirs, core, name):
    _, h, cols = theirs.shape
    by_rows = _by_rows(grads.shape[1])
    br = _rows_block(h, cols) if by_rows else h
    nb = h // br

    def body(core_ref, a_ref, b_ref, o_ref):
        o_ref[...] = (a_ref[...] + b_ref[...]).astype(BF16)

    if by_rows:
        mine = pl.BlockSpec((1, br, cols), lambda j, i, core_ref: (j, core_ref[0] * nb + i, 0))
    else:
        mine = pl.BlockSpec((1, br, cols), lambda j, i, core_ref: (j, i, core_ref[0]))
    return pl.pallas_call(
        body, name=name,
        out_shape=jax.ShapeDtypeStruct(theirs.shape, BF16),
        grid_spec=pltpu.PrefetchScalarGridSpec(
            num_scalar_prefetch=1, grid=(N_CHIPS, nb),
            in_specs=[mine, pl.BlockSpec((1, br, cols), lambda j, i, core_ref: (j, i, 0))],
            out_specs=pl.BlockSpec((1, br, cols), lambda j, i, core_ref: (j, i, 0))),
        compiler_params=_cparams(("parallel", "parallel")),
    )(core, grads, theirs)


def _sum_chips(slots, sums, place, by_rows, name):
    _, h, cols = slots.shape
    br = _rows_block(h, cols) if by_rows else h
    nb = h // br

    def body(place_ref, s_ref, own_ref, o_ref):
        me = place_ref[1]
        acc = None
        for k in range(N_CHIPS):
            term = jnp.where(me == k, own_ref[k], s_ref[k]).astype(F32)
            acc = term if acc is None else acc + term
        o_ref[...] = acc

    stack = pl.BlockSpec((N_CHIPS, br, cols), lambda i, place_ref: (0, i, 0))
    if by_rows:
        out_shape, out_map = (2 * h, cols), lambda i, place_ref: (place_ref[0] * nb + i, 0)
    else:
        out_shape, out_map = (h, 2 * cols), lambda i, place_ref: (i, place_ref[0])
    return pl.pallas_call(
        body, name=name,
        out_shape=jax.ShapeDtypeStruct(out_shape, F32),
        grid_spec=pltpu.PrefetchScalarGridSpec(
            num_scalar_prefetch=1, grid=(nb,), in_specs=[stack, stack],
            out_specs=pl.BlockSpec((br, cols), out_map)),
        compiler_params=_cparams(("parallel",)),
    )(place, slots, sums)


def _adamw_math(w, g, m, v):
    m = ADAM_B1 * m + (1.0 - ADAM_B1) * g
    v = ADAM_B2 * v + (1.0 - ADAM_B2) * (g * g)
    m_hat = m / (1.0 - ADAM_B1 ** ADAM_STEP)
    v_hat = v / (1.0 - ADAM_B2 ** ADAM_STEP)
    delta = -ADAM_LR * (m_hat / (jnp.sqrt(v_hat) + ADAM_EPS) + ADAM_WD * w)
    return delta, m, v


def _adamw(w, g, m, v, name):
    def body(w_ref, g_ref, m_ref, v_ref, d_ref, nm_ref, nv_ref):
        d, nm, nv = _adamw_math(w_ref[...], g_ref[...], m_ref[...], v_ref[...])
        d_ref[...] = d
        nm_ref[...] = nm
        nv_ref[...] = nv

    if w.ndim == 3:
        rows = w.shape[0]
        br = max(b for b in range(1, 65) if rows % b == 0)
        spec, steps = pl.BlockSpec((br,) + w.shape[1:], lambda i: (i, 0, 0)), rows // br
    else:
        rows, cols = w.shape
        br = _rows_block(rows, cols, unit=8)
        spec, steps = pl.BlockSpec((br, cols), lambda i: (i, 0)), rows // br
    shape = jax.ShapeDtypeStruct(w.shape, F32)
    return pl.pallas_call(
        body, name=name, out_shape=(shape, shape, shape), grid=(steps,),
        in_specs=[spec] * 4, out_specs=(spec, spec, spec),
        compiler_params=_cparams(("parallel",)),
    )(w, g, m, v)


def _adamw_small(parts, w, m, v):
    def body(p_ref, w_ref, m_ref, v_ref, g_ref, d_ref, nm_ref, nv_ref):
        g = p_ref[0]
        for k in range(1, N_DEV):
            g = g + p_ref[k]
        d, nm, nv = _adamw_math(w_ref[...], g, m_ref[...], v_ref[...])
        g_ref[...] = g
        d_ref[...] = d
        nm_ref[...] = nm
        nv_ref[...] = nv

    shape = jax.ShapeDtypeStruct((SMALL_ROWS, LANES), F32)
    return pl.pallas_call(body, name="adamw_small", out_shape=(shape,) * 4, compiler_params=_cparams())(parts, w, m, v)


def _mm(a, b, *, name, ta=False, out_dtype=F32, res=None, bm=1024, bn=1024, bk=4096, b_stack=False, out_stack=False):
    if ta:
        kdim, m = a.shape
    else:
        m, kdim = a.shape
    if b_stack:
        _, kb, chunk = b.shape
        n = N_CHIPS * chunk
    else:
        kb, n = b.shape
        chunk = n // N_CHIPS if out_stack else n
    assert kdim == kb, (a.shape, b.shape, ta)
    bm = _block(m, bm, LANES if ta else 16)
    bn = _block(chunk, bn, LANES)
    bk = _block(kdim, bk, LANES)
    nk = kdim // bk
    per_chunk = chunk // bn
    dims = (((0 if ta else 1,), (0,)), ((), ()))

    def body(*refs):
        refs = list(refs)
        a_ref, b_ref = refs[:2]
        r_ref = refs[2] if res is not None else None
        o_ref = refs[3] if res is not None else refs[2]
        part = lax.dot_general(a_ref[...].astype(BF16), b_ref[...].astype(BF16), dims, preferred_element_type=F32)

        def finish(r):
            if r_ref is not None:
                r = r + r_ref[...]
            o_ref[...] = r.astype(out_dtype)

        if nk == 1:
            finish(part)
        else:
            acc_ref = refs[-1]
            k = pl.program_id(2)

            @pl.when(k == 0)
            def _():
                acc_ref[...] = part

            @pl.when(k > 0)
            def _():
                acc_ref[...] += part

            @pl.when(k == nk - 1)
            def _():
                finish(acc_ref[...])

    a_spec = pl.BlockSpec((bk, bm), lambda i, j, k: (k, i)) if ta else pl.BlockSpec((bm, bk), lambda i, j, k: (i, k))
    if b_stack:
        b_spec = pl.BlockSpec((None, bk, bn), lambda i, j, k: (j // per_chunk, k, j % per_chunk))
    else:
        b_spec = pl.BlockSpec((bk, bn), lambda i, j, k: (k, j))
    r_spec = pl.BlockSpec((bm, bn), lambda i, j, k: (i, j))
    if out_stack:
        o_spec = pl.BlockSpec((None, bm, bn), lambda i, j, k: (j // per_chunk, i, j % per_chunk))
        o_shape = (N_CHIPS, m, chunk)
    else:
        o_spec, o_shape = r_spec, (m, n)
    in_specs = [a_spec, b_spec] + ([r_spec] if res is not None else [])
    args = (a, b) + ((res,) if res is not None else ())
    return pl.pallas_call(
        body, name=name, out_shape=jax.ShapeDtypeStruct(o_shape, out_dtype),
        grid=(m // bm, n // bn, nk), in_specs=in_specs, out_specs=o_spec,
        scratch_shapes=[pltpu.VMEM((bm, bn), F32)] if nk > 1 else [],
        compiler_params=_cparams(("parallel", "parallel", "arbitrary")),
    )(*args)


def _rms_fwd(x, g, name):
    t, d = x.shape
    bt = _block(t, 512, 16)

    def body(x_ref, g_ref, h_ref):
        xv = x_ref[...]
        r = lax.rsqrt(jnp.mean(xv * xv, axis=-1, keepdims=True) + EPS)
        h_ref[...] = (xv * r * g_ref[...]).astype(BF16)

    return pl.pallas_call(
        body, name=name, out_shape=jax.ShapeDtypeStruct((t, d), BF16), grid=(t // bt,),
        in_specs=[pl.BlockSpec((bt, d), lambda i: (i, 0)), pl.BlockSpec((1, d), lambda i: (0, 0))],
        out_specs=pl.BlockSpec((bt, d), lambda i: (i, 0)),
        compiler_params=_cparams(("parallel",)),
    )(x, g)


def _rms_bwd(dh, x, g, dres, name):
    t, d = x.shape
    bt = _block(t, 256, 16)
    want_dx = dres is not None

    def body(*refs):
        if want_dx:
            dh_ref, x_ref, g_ref, dres_ref, dx_ref, dxb_ref, dg_ref = refs
        else:
            dh_ref, x_ref, g_ref, dg_ref = refs
        xv = x_ref[...]
        r = lax.rsqrt(jnp.mean(xv * xv, axis=-1, keepdims=True) + EPS)
        xhat = xv * r
        dhv = dh_ref[...]

        @pl.when(pl.program_id(0) == 0)
        def _():
            dg_ref[...] = jnp.zeros_like(dg_ref)

        dg_ref[...] += jnp.sum(dhv * xhat, axis=0, keepdims=True)
        if want_dx:
            dxhat = dhv * g_ref[...]
            dx = dres_ref[...] + r * (dxhat - xhat * jnp.mean(dxhat * xhat, axis=-1, keepdims=True))
            dx_ref[...] = dx
            dxb_ref[...] = dx.astype(BF16)

    row = pl.BlockSpec((bt, d), lambda i: (i, 0))
    vec = pl.BlockSpec((1, d), lambda i: (0, 0))
    if want_dx:
        return pl.pallas_call(
            body, name=name, grid=(t // bt,),
            out_shape=(jax.ShapeDtypeStruct((t, d), F32), jax.ShapeDtypeStruct((t, d), BF16),
                       jax.ShapeDtypeStruct((1, d), F32)),
            in_specs=[row, row, vec, row], out_specs=(row, row, vec),
            compiler_params=_cparams(("arbitrary",)),
        )(dh, x, g, dres)
    return pl.pallas_call(
        body, name=name, grid=(t // bt,), out_shape=jax.ShapeDtypeStruct((1, d), F32),
        in_specs=[row, row, vec], out_specs=vec,
        compiler_params=_cparams(("arbitrary",)),
    )(dh, x, g)


def _final_loss(x, target, g):
    t, d = x.shape
    bt = _block(t, 256, 16)

    def body(x_ref, t_ref, g_ref, dx_ref, dxb_ref, dg_ref, loss_ref):
        xv = x_ref[...]
        gv = g_ref[...]
        r = lax.rsqrt(jnp.mean(xv * xv, axis=-1, keepdims=True) + EPS)
        xhat = xv * r
        err = xhat * gv - t_ref[...]

        @pl.when(pl.program_id(0) == 0)
        def _():
            dg_ref[...] = jnp.zeros_like(dg_ref)
            loss_ref[...] = jnp.zeros_like(loss_ref)

        loss_ref[...] += 0.5 * jnp.sum(jnp.mean(err * err, axis=-1, keepdims=True), axis=0, keepdims=True)
        dy = err * (1.0 / d)
        dg_ref[...] += jnp.sum(dy * xhat, axis=0, keepdims=True)
        dxhat = dy * gv
        dx = r * (dxhat - xhat * jnp.mean(dxhat * xhat, axis=-1, keepdims=True))
        dx_ref[...] = dx
        dxb_ref[...] = dx.astype(BF16)

    row = pl.BlockSpec((bt, d), lambda i: (i, 0))
    vec = pl.BlockSpec((1, d), lambda i: (0, 0))
    return pl.pallas_call(
        body, name="final_loss", grid=(t // bt,),
        out_shape=(jax.ShapeDtypeStruct((t, d), F32), jax.ShapeDtypeStruct((t, d), BF16),
                   jax.ShapeDtypeStruct((1, d), F32), jax.ShapeDtypeStruct((1, LANES), F32)),
        in_specs=[row, row, vec], out_specs=(row, row, vec, pl.BlockSpec((1, LANES), lambda i: (0, 0))),
        compiler_params=_cparams(("arbitrary",)),
    )(x, target, g)


GU_COLS = GATE_WIDTH + POOL_WIDTH
U_BLK = GATE_WIDTH // POOL_WIDTH


def _shift_down(a, k, row):
    return jnp.where(row >= k, pltpu.roll(a, k, 0), 0.0)


def _shift_up(a, k, row):
    n = a.shape[0]
    return jnp.where(row < n - k, pltpu.roll(a, n - k, 0), 0.0)


def _window_delta(u, w, row):
    s, k = u, 1
    while k < w:
        s = s + _shift_down(s, k, row)
        k *= 2
    cnt = jnp.minimum(row + 1, w).astype(F32)
    return s / cnt - u, cnt


def _pool_fwd(gu, pool_w, pool_scale):
    b, s, _ = gu.shape

    def body(u_ref, pw_ref, sc_ref, y_ref):
        row = lax.broadcasted_iota(jnp.int32, (s, POOL_GC), 0)
        for g, w in enumerate(POOL_WINDOWS):
            cols = slice(g * POOL_GC, (g + 1) * POOL_GC)
            d, _ = _window_delta(u_ref[0, :, cols], w, row)
            z = jnp.dot(d.astype(BF16), pw_ref[g].astype(BF16), preferred_element_type=F32)
            y_ref[0, :, cols] = (z * sc_ref[:, cols]).astype(BF16)

    return pl.pallas_call(
        body, name="pool_fwd", out_shape=jax.ShapeDtypeStruct((b, s, POOL_WIDTH), BF16), grid=(b,),
        in_specs=[pl.BlockSpec((1, s, POOL_WIDTH), lambda i: (i, 0, U_BLK)),
                  pl.BlockSpec((4, POOL_GC, POOL_GC), lambda i: (0, 0, 0)),
                  pl.BlockSpec((1, POOL_WIDTH), lambda i: (0, 0))],
        out_specs=pl.BlockSpec((1, s, POOL_WIDTH), lambda i: (i, 0, 0)),
        compiler_params=_cparams(("parallel",)),
    )(gu, pool_w, pool_scale)


def _pool_bwd(gu, dy, pool_w, pool_scale, dgu):
    b, s, _ = gu.shape

    def body(u_ref, dy_ref, pw_ref, sc_ref, dgu_in, du_ref, dpw_ref, dsc_ref):
        del dgu_in

        @pl.when(pl.program_id(0) == 0)
        def _():
            dpw_ref[...] = jnp.zeros_like(dpw_ref)
            dsc_ref[...] = jnp.zeros_like(dsc_ref)

        row = lax.broadcasted_iota(jnp.int32, (s, POOL_GC), 0)
        for g, w in enumerate(POOL_WINDOWS):
            cols = slice(g * POOL_GC, (g + 1) * POOL_GC)
            d, cnt = _window_delta(u_ref[0, :, cols], w, row)
            db = d.astype(BF16)
            pw = pw_ref[g].astype(BF16)
            z = jnp.dot(db, pw, preferred_element_type=F32)
            dyv = dy_ref[0, :, cols]
            dsc_ref[:, cols] += jnp.sum(dyv * z, axis=0, keepdims=True)
            dz = (dyv * sc_ref[:, cols]).astype(BF16)
            dpw_ref[g] += lax.dot_general(db, dz, (((0,), (0,)), ((), ())), preferred_element_type=F32)
            dd = lax.dot_general(dz, pw, (((1,), (1,)), ((), ())), preferred_element_type=F32)
            acc, k = dd / cnt, 1
            while k < w:
                acc = acc + _shift_up(acc, k, row)
                k *= 2
            du_ref[0, :, cols] = (acc - dd).astype(BF16)

    return pl.pallas_call(
        body, name="pool_bwd", grid=(b,),
        out_shape=(jax.ShapeDtypeStruct((b, s, GU_COLS), BF16), jax.ShapeDtypeStruct((4, POOL_GC, POOL_GC), F32),
                   jax.ShapeDtypeStruct((1, POOL_WIDTH), F32)),
        in_specs=[pl.BlockSpec((1, s, POOL_WIDTH), lambda i: (i, 0, U_BLK)),
                  pl.BlockSpec((1, s, POOL_WIDTH), lambda i: (i, 0, 0)),
                  pl.BlockSpec((4, POOL_GC, POOL_GC), lambda i: (0, 0, 0)),
                  pl.BlockSpec((1, POOL_WIDTH), lambda i: (0, 0)), ANY],
        out_specs=(pl.BlockSpec((1, s, POOL_WIDTH), lambda i: (i, 0, U_BLK)),
                   pl.BlockSpec((4, POOL_GC, POOL_GC), lambda i: (0, 0, 0)),
                   pl.BlockSpec((1, POOL_WIDTH), lambda i: (0, 0))),
        input_output_aliases={4: 0},
        compiler_params=_cparams(("arbitrary",)),
    )(gu, dy, pool_w, pool_scale, dgu)


def _forget_cumsum(f, bias):
    b, s, c = f.shape

    def body(f_ref, b_ref, c_ref):
        row = lax.broadcasted_iota(jnp.int32, (s, LANES), 0)
        z = f_ref[0] + b_ref[...]
        acc = jnp.minimum(z, 0.0) - jnp.log(1.0 + jnp.exp(-jnp.abs(z)))
        k = 1
        while k < s:
            acc = acc + _shift_down(acc, k, row)
            k *= 2
        c_ref[0] = acc

    return pl.pallas_call(
        body, name="forget_cumsum", out_shape=jax.ShapeDtypeStruct((b, s, c), F32), grid=(b, c // LANES),
        in_specs=[pl.BlockSpec((1, s, LANES), lambda i, j: (i, 0, j)), pl.BlockSpec((1, LANES), lambda i, j: (0, j))],
        out_specs=pl.BlockSpec((1, s, LANES), lambda i, j: (i, 0, j)),
        compiler_params=_cparams(("parallel", "parallel")),
    )(f, bias)


def _forget_bwd(dc, f, bias):
    b, s, _ = f.shape

    def body(dc_ref, f_ref, b_ref, df_ref, db_ref):
        @pl.when(pl.program_id(0) == 0)
        def _():
            db_ref[...] = jnp.zeros_like(db_ref)

        row = lax.broadcasted_iota(jnp.int32, (s, LANES), 0)
        acc, k = dc_ref[0], 1
        while k < s:
            acc = acc + _shift_up(acc, k, row)
            k *= 2
        z = f_ref[0] + b_ref[...]
        df = acc / (1.0 + jnp.exp(z))
        db_ref[...] += jnp.sum(df, axis=0, keepdims=True)
        df_ref[0] = df.astype(BF16)

    blk = pl.BlockSpec((1, s, LANES), lambda i: (i, 0, 0))
    vec = pl.BlockSpec((1, LANES), lambda i: (0, 0))
    return pl.pallas_call(
        body, name="forget_bwd", grid=(b,),
        out_shape=(jax.ShapeDtypeStruct((b, s, LANES), BF16), jax.ShapeDtypeStruct((1, LANES), F32)),
        in_specs=[blk, blk, vec], out_specs=(blk, vec),
        compiler_params=_cparams(("arbitrary",)),
    )(dc, f, bias)


KV_BLK0 = 2
PAIRS = FOX_HEADS // 2
FOX_SCALE = FOX_DH ** -0.5
NT_DIMS = (((1,), (1,)), ((), ()))
TN_DIMS = (((0,), (0,)), ((), ()))


def _stack_heads(v):
    head = lax.broadcasted_iota(jnp.int32, v.shape, 1) // FOX_DH
    zero = jnp.zeros_like(v)
    return jnp.concatenate([jnp.where(head == 0, v, zero), jnp.where(head == 1, v, zero)], axis=0)


def _stack_cols(v):
    return jnp.concatenate([v[:, 0:1], v[:, FOX_DH:FOX_DH + 1]], axis=0)


def _unstack(t, blk):
    head = lax.broadcasted_iota(jnp.int32, (blk, LANES), 1) // FOX_DH
    return jnp.where(head == 0, t[:blk], t[blk:])


def _fox_scores(q_all, kblk, row_bias, cr_ref, kb, masked, blk):
    top = lax.broadcasted_iota(jnp.int32, (2 * blk, 1), 0) < blk
    s = lax.dot_general(q_all, kblk, NT_DIMS, preferred_element_type=F32)
    s = s + (row_bias - jnp.where(top, cr_ref[0, 0, kb], cr_ref[0, 1, kb]))
    if masked:
        r = lax.broadcasted_iota(jnp.int32, (2 * blk, blk), 0)
        keep = jnp.where(r >= blk, r - blk, r) >= lax.broadcasted_iota(jnp.int32, (2 * blk, blk), 1)
        s = jnp.where(keep, s, NEG_INF)
    return s


def _fox_fwd(qkv, c_exp, c_row, ex=None):
    b, s, _ = qkv.shape
    blk = min(ATT_BLOCK, s)
    nq = s // blk

    def body(q_ref, kv_ref, cc_ref, cr_ref, o_ref, ob_ref, lse_ref):
        qi = pl.program_id(2)
        q_all = _stack_heads(q_ref[0] * FOX_SCALE)
        cq = _stack_cols(cc_ref[0])

        def step(kb, carry, masked):
            m, l, acc = carry
            rows = pl.ds(pl.multiple_of(kb * blk, blk), blk)
            sc = _fox_scores(q_all, kv_ref[0, rows, :LANES], cq, cr_ref, kb, masked, blk)
            m_new = jnp.maximum(m, jnp.max(sc, axis=-1, keepdims=True))
            p = jnp.exp(sc - m_new)
            alpha = jnp.exp(m - m_new)
            l = alpha * l + jnp.sum(p, axis=-1, keepdims=True)
            acc = alpha * acc + jnp.dot(p.astype(BF16), kv_ref[0, rows, LANES:], preferred_element_type=F32)
            return m_new, l, acc

        init = (jnp.full((2 * blk, 1), NEG_INF, F32), jnp.zeros((2 * blk, 1), F32), jnp.zeros((2 * blk, LANES), F32))
        m, l, acc = step(qi, lax.fori_loop(0, qi, functools.partial(step, masked=False), init), True)
        o = _unstack(acc / l, blk)
        o_ref[0] = o
        ob_ref[0] = o.astype(BF16)
        lse_ref[0] = _unstack(jnp.broadcast_to(m + jnp.log(l), (2 * blk, LANES)), blk)

    tile = pl.BlockSpec((1, blk, LANES), lambda i, h, q: (i, q, h))
    kvspec = pl.BlockSpec((1, s, 2 * LANES), lambda i, h, q: (i, 0, KV_BLK0 + h))
    shape = jax.ShapeDtypeStruct((b, s, FOX_WIDTH), F32)
    return _hosted_call(
        body, ex, name="fox_fwd", out_shape=(shape, jax.ShapeDtypeStruct((b, s, FOX_WIDTH), BF16), shape),
        grid=(b, PAIRS, nq),
        in_specs=[tile, kvspec, tile, pl.BlockSpec((1, 2, nq, 1, blk), lambda i, h, q: (i, h, 0, 0, 0))],
        out_specs=(tile, tile, tile), args=(qkv, qkv, c_exp, c_row))


def _fox_bwd(qkv, c_exp, c_row, lse, o, do, ex=None):
    b, s, _ = qkv.shape
    blk = min(ATT_BLOCK, s)
    nq = s // blk

    def body(q_ref, kv_ref, cc_ref, cr_ref, lse_ref, o_ref, do_ref, dq_ref, dkv_ref, dcq_ref, dc_ref, dk_acc, dv_acc):
        qi = pl.program_id(2)

        @pl.when(qi == 0)
        def _():
            dk_acc[...] = jnp.zeros_like(dk_acc)
            dv_acc[...] = jnp.zeros_like(dv_acc)
            dc_ref[...] = jnp.zeros_like(dc_ref)

        q_all = _stack_heads(q_ref[0] * FOX_SCALE)
        dov = do_ref[0]
        do_all = _stack_heads(dov.astype(BF16))
        delta = jnp.sum(_stack_heads(dov * o_ref[0]), axis=-1, keepdims=True)
        bias = _stack_cols(cc_ref[0]) - _stack_cols(lse_ref[0])

        def step(kb, carry, masked):
            acc, dcq = carry
            rows = pl.ds(pl.multiple_of(kb * blk, blk), blk)
            kblk = kv_ref[0, rows, :LANES]
            p = jnp.exp(_fox_scores(q_all, kblk, bias, cr_ref, kb, masked, blk))
            dp = lax.dot_general(do_all, kv_ref[0, rows, LANES:], NT_DIMS, preferred_element_type=F32)
            ds = p * (dp - delta)
            dsb = ds.astype(BF16)
            dv_acc[rows, :] += lax.dot_general(p.astype(BF16), do_all, TN_DIMS, preferred_element_type=F32)
            dk_acc[rows, :] += lax.dot_general(dsb, q_all, TN_DIMS, preferred_element_type=F32)
            dc_ref[0, 0, kb] -= jnp.sum(ds[:blk], axis=0, keepdims=True)
            dc_ref[0, 1, kb] -= jnp.sum(ds[blk:], axis=0, keepdims=True)
            acc = acc + jnp.dot(dsb, kblk, preferred_element_type=F32)
            return acc, dcq + jnp.sum(ds, axis=-1, keepdims=True)

        init = (jnp.zeros((2 * blk, LANES), F32), jnp.zeros((2 * blk, 1), F32))
        acc, dcq = step(qi, lax.fori_loop(0, qi, functools.partial(step, masked=False), init), True)
        dq_ref[0] = (_unstack(acc, blk) * FOX_SCALE).astype(BF16)
        dcq_ref[0] = _unstack(jnp.broadcast_to(dcq, (2 * blk, LANES)), blk)

        @pl.when(qi == nq - 1)
        def _():
            dkv_ref[0, :, :LANES] = dk_acc[...].astype(BF16)
            dkv_ref[0, :, LANES:] = dv_acc[...].astype(BF16)

    tile = pl.BlockSpec((1, blk, LANES), lambda i, h, q: (i, q, h))
    kvspec = pl.BlockSpec((1, s, 2 * LANES), lambda i, h, q: (i, 0, KV_BLK0 + h))
    crow = pl.BlockSpec((1, 2, nq, 1, blk), lambda i, h, q: (i, h, 0, 0, 0))
    return _hosted_call(
        body, ex, name="fox_bwd", grid=(b, PAIRS, nq),
        out_shape=(jax.ShapeDtypeStruct((b, s, FOX_WIDTH), BF16), jax.ShapeDtypeStruct((b, s, 2 * FOX_WIDTH), BF16),
                   jax.ShapeDtypeStruct((b, s, FOX_WIDTH), F32), jax.ShapeDtypeStruct(c_row.shape, F32)),
        in_specs=[tile, kvspec, tile, crow, tile, tile, tile],
        out_specs=(tile, pl.BlockSpec((1, s, 2 * LANES), lambda i, h, q: (i, 0, h)), tile, crow),
        scratch=[pltpu.VMEM((s, LANES), F32), pltpu.VMEM((s, LANES), F32)],
        args=(qkv, qkv, c_exp, c_row, lse, o, do))


def _sigmoid(z):
    return 1.0 / (1.0 + jnp.exp(-z))


def _mix_fwd(gu, b_gate, y_pool, y_fox):
    t = gu.shape[0]
    bt = _block(t, 256, 16)

    def body(gp_ref, gf_ref, bp_ref, bf_ref, yp_ref, yf_ref, o_ref):
        gp = _sigmoid(gp_ref[...] + bp_ref[...])
        gf = _sigmoid(gf_ref[...] + bf_ref[...])
        o_ref[...] = (gp * yp_ref[...] + gf * yf_ref[...]).astype(BF16)

    col = lambda j: pl.BlockSpec((bt, D_MODEL), lambda i: (i, j))
    vec = lambda j: pl.BlockSpec((1, D_MODEL), lambda i: (0, j))
    return pl.pallas_call(
        body, name="mix_fwd", out_shape=jax.ShapeDtypeStruct((t, D_MODEL), BF16), grid=(t // bt,),
        in_specs=[col(0), col(1), vec(0), vec(1), col(0), col(0)], out_specs=col(0),
        compiler_params=_cparams(("parallel",)),
    )(gu, gu, b_gate, b_gate, y_pool, y_fox)


def _mix_bwd(gu, b_gate, y_pool, y_fox, dmix):
    t = gu.shape[0]
    bt = _block(t, 256, 16)

    def body(gp_ref, gf_ref, bp_ref, bf_ref, yp_ref, yf_ref, dm_ref, dyp_ref, dyf_ref, dgl_ref, db_ref):
        @pl.when(pl.program_id(0) == 0)
        def _():
            db_ref[...] = jnp.zeros_like(db_ref)

        dm = dm_ref[...]
        gp = _sigmoid(gp_ref[...] + bp_ref[...])
        gf = _sigmoid(gf_ref[...] + bf_ref[...])
        dyp_ref[...] = (dm * gp).astype(BF16)
        dyf_ref[...] = (dm * gf).astype(BF16)
        dlp = dm * yp_ref[...] * gp * (1.0 - gp)
        dlf = dm * yf_ref[...] * gf * (1.0 - gf)
        dgl_ref[:, :D_MODEL] = dlp.astype(BF16)
        dgl_ref[:, D_MODEL:] = dlf.astype(BF16)
        db_ref[:, :D_MODEL] += jnp.sum(dlp, axis=0, keepdims=True)
        db_ref[:, D_MODEL:] += jnp.sum(dlf, axis=0, keepdims=True)

    col = lambda j: pl.BlockSpec((bt, D_MODEL), lambda i: (i, j))
    vec = lambda j: pl.BlockSpec((1, D_MODEL), lambda i: (0, j))
    wide = pl.BlockSpec((bt, GATE_WIDTH), lambda i: (i, 0))
    return pl.pallas_call(
        body, name="mix_bwd", grid=(t // bt,),
        out_shape=(jax.ShapeDtypeStruct((t, D_MODEL), BF16), jax.ShapeDtypeStruct((t, D_MODEL), BF16),
                   jax.ShapeDtypeStruct((t, GU_COLS), BF16), jax.ShapeDtypeStruct((1, GATE_WIDTH), F32)),
        in_specs=[col(0), col(1), vec(0), vec(1), col(0), col(0), col(0)],
        out_specs=(col(0), col(0), wide, pl.BlockSpec((1, GATE_WIDTH), lambda i: (0, 0))),
        compiler_params=_cparams(("arbitrary",)),
    )(gu, gu, b_gate, b_gate, y_pool, y_fox, dmix)


X_SCALE = X_DH ** -0.5


def _xattn_probs(qh, kh):
    s = lax.dot_general(qh, kh, NT_DIMS, preferred_element_type=F32) * X_SCALE
    e = jnp.exp(s - jnp.max(s, axis=-1, keepdims=True))
    return e / jnp.sum(e, axis=-1, keepdims=True)


def _xattn_fwd(q, kv):
    b, s, _ = q.shape
    m = kv.shape[1]
    bq = _block(s, 512, 16)

    def body(q_ref, kv_ref, o_ref):
        for h in range(X_HEADS):
            cols = slice(h * X_DH, (h + 1) * X_DH)
            p = _xattn_probs(q_ref[0, :, cols], kv_ref[0, :, cols])
            vh = kv_ref[0, :, X_WIDTH + h * X_DH:X_WIDTH + (h + 1) * X_DH]
            o_ref[0, :, cols] = jnp.dot(p.astype(BF16), vh, preferred_element_type=F32).astype(BF16)

    return pl.pallas_call(
        body, name="xattn_fwd", out_shape=jax.ShapeDtypeStruct((b, s, X_WIDTH), BF16), grid=(b, s // bq),
        in_specs=[pl.BlockSpec((1, bq, X_WIDTH), lambda i, j: (i, j, 0)),
                  pl.BlockSpec((1, m, 2 * X_WIDTH), lambda i, j: (i, 0, 0))],
        out_specs=pl.BlockSpec((1, bq, X_WIDTH), lambda i, j: (i, j, 0)),
        compiler_params=_cparams(("parallel", "parallel")),
    )(q, kv)


def _xattn_bwd(q, kv, do):
    b, s, _ = q.shape
    m = kv.shape[1]
    bq = _block(s, 512, 16)

    def body(q_ref, kv_ref, do_ref, dq_ref, dkv_ref):
        @pl.when(pl.program_id(1) == 0)
        def _():
            dkv_ref[...] = jnp.zeros_like(dkv_ref)

        for h in range(X_HEADS):
            cols = slice(h * X_DH, (h + 1) * X_DH)
            vcols = slice(X_WIDTH + h * X_DH, X_WIDTH + (h + 1) * X_DH)
            qh, kh, vh, doh = q_ref[0, :, cols], kv_ref[0, :, cols], kv_ref[0, :, vcols], do_ref[0, :, cols]
            p = _xattn_probs(qh, kh)
            dkv_ref[0, :, vcols] += lax.dot_general(p.astype(BF16), doh, TN_DIMS, preferred_element_type=F32)
            dp = lax.dot_general(doh, vh, NT_DIMS, preferred_element_type=F32)
            ds = (p * (dp - jnp.sum(p * dp, axis=-1, keepdims=True)) * X_SCALE).astype(BF16)
            dq_ref[0, :, cols] = jnp.dot(ds, kh, preferred_element_type=F32).astype(BF16)
            dkv_ref[0, :, cols] += lax.dot_general(ds, qh, TN_DIMS, preferred_element_type=F32)

    tile = pl.BlockSpec((1, bq, X_WIDTH), lambda i, j: (i, j, 0))
    mem = pl.BlockSpec((1, m, 2 * X_WIDTH), lambda i, j: (i, 0, 0))
    return pl.pallas_call(
        body, name="xattn_bwd", grid=(b, s // bq),
        out_shape=(jax.ShapeDtypeStruct((b, s, X_WIDTH), BF16), jax.ShapeDtypeStruct((b, m, 2 * X_WIDTH), F32)),
        in_specs=[tile, mem, tile], out_specs=(tile, mem),
        compiler_params=_cparams(("parallel", "arbitrary")),
    )(q, kv, do)


def _swiglu_fwd(gu):
    t = gu.shape[0]
    bt = _block(t, 256, 16)

    def body(gt_ref, up_ref, o_ref):
        gt = gt_ref[...].astype(F32)
        o_ref[...] = (gt * _sigmoid(gt) * up_ref[...].astype(F32)).astype(BF16)

    col = lambda j: pl.BlockSpec((bt, D_FF), lambda i: (i, j))
    return pl.pallas_call(
        body, name="swiglu_fwd", out_shape=jax.ShapeDtypeStruct((t, D_FF), BF16), grid=(t // bt,),
        in_specs=[col(0), col(1)], out_specs=col(0),
        compiler_params=_cparams(("parallel",)),
    )(gu, gu)


def _swiglu_bwd(gu, dact):
    t = gu.shape[0]
    bt = _block(t, 256, 16)

    def body(gt_ref, up_ref, da_ref, o_ref):
        gt = gt_ref[...].astype(F32)
        da = da_ref[...].astype(F32)
        sg = _sigmoid(gt)
        silu = gt * sg
        o_ref[:, :D_FF] = (da * up_ref[...].astype(F32) * (sg + silu * (1.0 - sg))).astype(BF16)
        o_ref[:, D_FF:] = (da * silu).astype(BF16)

    col = lambda j: pl.BlockSpec((bt, D_FF), lambda i: (i, j))
    return pl.pallas_call(
        body, name="swiglu_bwd", out_shape=jax.ShapeDtypeStruct((t, 2 * D_FF), BF16), grid=(t // bt,),
        in_specs=[col(0), col(1), col(0)], out_specs=pl.BlockSpec((bt, 2 * D_FF), lambda i: (i, 0)),
        compiler_params=_cparams(("parallel",)),
    )(gu, gu, dact)


def _stack_of(w, axis):
    r, c = w.shape
    if axis == 0:
        return w.reshape(N_CHIPS, r // N_CHIPS, c)
    return w.reshape(r, N_CHIPS, c // N_CHIPS).transpose(1, 0, 2)


def _stack_t(w3):
    n, r, c = w3.shape
    return w3.transpose(0, 2, 1).reshape(n * c, r)


def _pair_rows(k, v):
    c = k.shape[1]
    return jnp.stack([k.reshape(PAIRS, LANES, c), v.reshape(PAIRS, LANES, c)], axis=1).reshape(2 * FOX_WIDTH, c)


def _unpair_rows(kv):
    c = kv.shape[1]
    kv = kv.reshape(PAIRS, 2, LANES, c)
    return kv[:, 0].reshape(FOX_WIDTH, c), kv[:, 1].reshape(FOX_WIDTH, c)


def _input_grad(parts, weights_t, ex):
    t = parts[0].shape[0]
    d = weights_t[0].shape[1]
    bm = _block(t, 512, 16)
    n = len(parts)

    def body(*refs):
        acc = None
        for a_ref, b_ref in zip(refs[:n], refs[n:2 * n]):
            term = jnp.dot(a_ref[...], b_ref[...], preferred_element_type=F32)
            acc = term if acc is None else acc + term
        refs[2 * n][...] = acc

    (out,), moved = _hosted_call(
        body, ex, name="d_h", grid=(t // bm,), out_shape=(jax.ShapeDtypeStruct((t, d), F32),),
        in_specs=[pl.BlockSpec((bm, p.shape[1]), lambda i: (i, 0)) for p in parts]
        + [pl.BlockSpec(w.shape, lambda i: (0, 0)) for w in weights_t],
        out_specs=(pl.BlockSpec((bm, d), lambda i: (i, 0)),), args=tuple(parts) + tuple(weights_t))
    return out, moved


def _step(x, mem, loss_target, weights, moments_m, moments_v):
    nb, s, d = x.shape
    n_mem = mem.shape[1]
    t = nb * s
    blk = min(ATT_BLOCK, s)
    x2 = x.reshape(t, d)
    mem2 = mem.reshape(nb * n_mem, d)
    tgt2 = loss_target.reshape(t, d)

    def shard2d(a, n):
        a = a.reshape(a.shape[1:])
        return a.T if n == "w_in" else a

    def unshard(a, n):
        return (a.T if n == "w_in" else a)[None]

    local = {n: shard2d(weights[n], n) for n, _, _ in SHARDED}

    names = [n for n, _, _ in SHARDED]
    later = [n for n in names if n != "w_in"]
    local_b = {n: local[n].astype(BF16) for n in names}
    w_in_stack, = _place_own(_run_exchange(_gather_exchange([local_b["w_in"]]), "gather_w_in"), [local_b["w_in"]])
    w_in_t = w_in_stack.reshape(IN_COLS, D_MODEL)
    w_gu_t = jnp.concatenate([w_in_t[2056:], w_in_t[:512]])
    w_qkv_t = jnp.concatenate([w_in_t[512:1024], _pair_rows(w_in_t[1024:1536], w_in_t[1536:2048])])
    w_f_t = jnp.pad(w_in_t[2048:2056], ((0, LANES - FOX_HEADS), (0, 0)))
    w_gu, w_qkv, w_f = w_gu_t.T, w_qkv_t.T, w_f_t.T

    g_mix, g_x, g_mem, g_ffn = (weights[n] for n in ("norm_mix_g", "norm_x_g", "norm_mem_g", "norm_ffn_g"))
    g_final = weights["norm_final_g"].reshape(1, d)
    pool_w = weights["pool_w"].reshape(4, POOL_GC, POOL_GC)
    pool_scale, b_gate = weights["pool_scale"], weights["b_gate"]
    b_f_pad = jnp.pad(weights["b_forget"], ((0, 0), (0, LANES - FOX_HEADS)))
    b_f_exp = jnp.repeat(weights["b_forget"], FOX_DH, axis=1)

    h = _rms_fwd(x2, g_mix, "norm_mix")
    gu = _mm(h, w_gu, bn=512, name="in_proj_gates_pool")
    qkv = _mm(h, w_qkv, out_dtype=BF16, bn=512, name="in_proj_qkv")
    f_pad = _mm(h, w_f, name="in_proj_forget")
    gu3, qkv3 = gu.reshape(nb, s, GU_COLS), qkv.reshape(nb, s, 3 * FOX_WIDTH)
    y = _pool_fwd(gu3, pool_w, pool_scale)
    f_exp = jnp.repeat(f_pad[:, :FOX_HEADS], FOX_DH, axis=1).reshape(nb, s, FOX_WIDTH)
    c_exp = _forget_cumsum(f_exp, b_f_exp)
    c_row = c_exp[:, :, ::FOX_DH].transpose(0, 2, 1).reshape(nb, FOX_HEADS, s // blk, 1, blk)
    (o, o_b, lse), gathered = _fox_fwd(qkv3, c_exp, c_row, ex=_gather_exchange([local_b[n] for n in later]))
    stacks = dict(zip(later, _place_own(gathered, [local_b[n] for n in later])))
    w_pool_out3, w_fox_out3, w_xo3, w_ffn_in3 = (stacks[n] for n in ("w_pool_out", "w_fox_out", "w_xo", "w_ffn_in"))
    w_out, w_xq, w_xkv, w_ffn_out = (stacks[n].reshape(-1, stacks[n].shape[2])
                                     for n in ("w_out", "w_xq", "w_xkv", "w_ffn_out"))
    y2, o2 = y.reshape(t, POOL_WIDTH), o_b.reshape(t, FOX_WIDTH)
    y_pool = _mm(y2, w_pool_out3, b_stack=True, name="pool_out")
    y_fox = _mm(o2, w_fox_out3, b_stack=True, name="fox_out")
    mix = _mix_fwd(gu, b_gate, y_pool, y_fox)
    x1 = _mm(mix, w_out, res=x2, name="mix_out")
    hx = _rms_fwd(x1, g_x, "norm_x")
    mem_n = _rms_fwd(mem2, g_mem, "norm_mem")
    qx = _mm(hx, w_xq, out_dtype=BF16, name="x_q")
    kv = _mm(mem_n, w_xkv, out_dtype=BF16, name="x_kv")
    qx3, kv3 = qx.reshape(nb, s, X_WIDTH), kv.reshape(nb, n_mem, 2 * X_WIDTH)
    ox = _xattn_fwd(qx3, kv3).reshape(t, X_WIDTH)
    x2_ = _mm(ox, w_xo3, b_stack=True, res=x1, name="x_out")
    hf = _rms_fwd(x2_, g_ffn, "norm_ffn")
    ffn = _mm(hf, w_ffn_in3, b_stack=True, out_dtype=BF16, bn=1408, name="ffn_in")
    act = _swiglu_fwd(ffn)
    x3 = _mm(act, w_ffn_out, res=x2_, name="ffn_out")

    dx3, dx3_b, dg_final, loss_part = _final_loss(x3, tgt2, g_final)
    dw_ffn_out = _mm(act, dx3_b, ta=True, bm=1408, bn=512, bk=2048, name="d_w_ffn_out")
    dact = _mm(dx3_b, w_ffn_out.T, out_dtype=BF16, bn=1408, name="d_act")
    dffn = _swiglu_bwd(ffn, dact)
    dw_ffn_in = _mm(hf, dffn, ta=True, bm=512, bn=1408, bk=2048, out_stack=True, name="d_w_ffn_in")
    dhf = _mm(dffn, _stack_t(w_ffn_in3), bk=2816, name="d_hf")
    dx2, dx2_b, dg_ffn = _rms_bwd(dhf, x2_, g_ffn, dx3, "norm_ffn_bwd")

    dw_xo = _mm(ox, dx2_b, ta=True, bn=256, out_stack=True, name="d_w_xo")
    dox = _mm(dx2_b, _stack_t(w_xo3), out_dtype=BF16, name="d_ox").reshape(nb, s, X_WIDTH)
    dqx, dkv = _xattn_bwd(qx3, kv3, dox)
    dqx2, dkv2 = dqx.reshape(t, X_WIDTH), dkv.reshape(nb * n_mem, 2 * X_WIDTH)
    dw_xkv = _mm(mem_n, dkv2, ta=True, name="d_w_xkv")
    dmem_n = _mm(dkv2, w_xkv.T, name="d_mem_n")
    dg_mem = _rms_bwd(dmem_n, mem2, g_mem, None, "norm_mem_bwd")
    dw_xq = _mm(hx, dqx2, ta=True, name="d_w_xq")
    dhx = _mm(dqx2, w_xq.T, name="d_hx")
    dx1, dx1_b, dg_x = _rms_bwd(dhx, x1, g_x, dx2, "norm_x_bwd")

    dw_out = _mm(mix, dx1_b, ta=True, name="d_w_out")
    dmix = _mm(dx1_b, w_out.T, name="d_mix")
    dyp, dyf, dgu, db_gate = _mix_bwd(gu, b_gate, y_pool, y_fox, dmix)
    dw_pool_out = _mm(y2, dyp, ta=True, bn=256, out_stack=True, name="d_w_pool_out")
    dw_fox_out = _mm(o2, dyf, ta=True, bn=256, out_stack=True, name="d_w_fox_out")
    dy = _mm(dyp, _stack_t(w_pool_out3), name="d_y").reshape(nb, s, POOL_WIDTH)
    do = _mm(dyf, _stack_t(w_fox_out3), name="d_o").reshape(nb, s, FOX_WIDTH)
    dgu3, dpool_w, dpool_scale = _pool_bwd(gu3, dy, pool_w, pool_scale, dgu.reshape(nb, s, GU_COLS))
    core = lax.axis_index("c").astype(jnp.int32).reshape(1)
    grad_stacks = {"w_pool_out": dw_pool_out, "w_fox_out": dw_fox_out, "w_out": _stack_of(dw_out, 0),
                   "w_xq": _stack_of(dw_xq, 0), "w_xkv": _stack_of(dw_xkv, 0), "w_xo": dw_xo,
                   "w_ffn_in": dw_ffn_in, "w_ffn_out": _stack_of(dw_ffn_out, 0)}

    def chip_sums_of(group):
        theirs = _swap_halves([grad_stacks[n] for n in group], "swap_halves_" + group[0])
        return [_sum_halves(grad_stacks[n], t_, core, "sum_halves_" + n) for n, t_ in zip(group, theirs)]

    ffn_group = ["w_ffn_in", "w_ffn_out"]
    mid_group = ["w_pool_out", "w_fox_out", "w_out", "w_xq", "w_xkv", "w_xo"]
    chip_sums = dict(zip(ffn_group, chip_sums_of(ffn_group)))
    chip_sums.update(zip(mid_group, chip_sums_of(mid_group)))
    early = ffn_group + mid_group
    (dq3, dkv3, dc_q, dc_row), early_slots = _fox_bwd(qkv3, c_exp, c_row, lse, o, do,
                                                      ex=_chips_exchange([chip_sums[n] for n in early]))
    slots = dict(zip(early, early_slots))
    dc = dc_row.reshape(nb, FOX_HEADS, s).transpose(0, 2, 1) + dc_q[:, :, ::FOX_DH]
    dc = jnp.pad(dc, ((0, 0), (0, 0), (0, LANES - FOX_HEADS)))
    df, db_f = _forget_bwd(dc, f_pad.reshape(nb, s, LANES), b_f_pad)
    dgu2, dq2, dkv2, df2 = (dgu3.reshape(t, GU_COLS), dq3.reshape(t, FOX_WIDTH), dkv3.reshape(t, 2 * FOX_WIDTH),
                            df.reshape(t, LANES))
    dw_gu_t = _mm(dgu2, h, ta=True, name="d_w_gates_pool")
    dw_q_t = _mm(dq2, h, ta=True, name="d_w_q")
    dw_kv_t = _mm(dkv2, h, ta=True, name="d_w_kv")
    dw_f_t = _mm(df2, h, ta=True, name="d_w_forget")
    dw_k_t, dw_v_t = _unpair_rows(dw_kv_t)
    dw_in_t = jnp.concatenate([dw_gu_t[GATE_WIDTH:], dw_q_t, dw_k_t, dw_v_t, dw_f_t[:FOX_HEADS],
                               dw_gu_t[:GATE_WIDTH]])
    grad_stacks["w_in"] = dw_in_t.reshape(N_CHIPS, IN_COLS // N_CHIPS, D_MODEL)
    chip_sums["w_in"], = chip_sums_of(["w_in"])
    dh, (slots["w_in"],) = _input_grad([dgu2, dq2, dkv2, df2],
                                       [w_gu_t, w_qkv_t[:FOX_WIDTH], w_qkv_t[FOX_WIDTH:], w_f_t],
                                       _chips_exchange([chip_sums["w_in"]]))
    dx, _, dg_mix = _rms_bwd(dh, x2, g_mix, dx1, "norm_mix_bwd")

    place = jnp.stack([lax.axis_index("c"), 2 * lax.axis_index("x") + lax.axis_index("y")]).astype(jnp.int32)
    reduced = _join_halves([_sum_chips(slots[n], chip_sums[n], place, _by_rows(local[n].shape[0]), "sum_chips_" + n)
                            for n in names])

    small_grads = {"norm_mix_g": dg_mix, "b_forget": db_f[:, :FOX_HEADS], "b_gate": db_gate, "pool_w": dpool_w,
                   "pool_scale": dpool_scale, "norm_x_g": dg_x, "norm_mem_g": dg_mem, "norm_ffn_g": dg_ffn,
                   "norm_final_g": dg_final}
    parts = _gather_small(_pack_small([small_grads[n] for n, _ in SMALL], last=loss_part[0, 0]))
    sg, sd, sm, sv = _adamw_small(parts.reshape(N_DEV, SMALL_ROWS, LANES),
                                  _pack_small([weights[n] for n, _ in SMALL]),
                                  _pack_small([moments_m[n] for n, _ in SMALL]),
                                  _pack_small([moments_v[n] for n, _ in SMALL]))
    loss = sg.reshape(-1)[LOSS_POS]

    grads, deltas, new_m, new_v = {}, {}, {}, {}
    for (n, _), g_, d_, m_, v_ in zip(SMALL, _unpack_small(sg), _unpack_small(sd), _unpack_small(sm), _unpack_small(sv)):
        grads[n], deltas[n], new_m[n], new_v[n] = g_, d_, m_, v_
    def tiles_of(a):
        return a.transpose(2, 0, 1)

    def block_of(a3):
        return a3.transpose(1, 2, 0)

    for n, g_ in zip(names, reduced):
        if n == "w_in":
            g_ = g_.reshape(IN_COLS // N_CHIPS, 1, D_MODEL)
            w_, m_, v_ = tiles_of(weights[n]), tiles_of(moments_m[n]), tiles_of(moments_v[n])
            back = block_of
        else:
            w_, m_, v_ = local[n], shard2d(moments_m[n], n), shard2d(moments_v[n], n)
            back = functools.partial(unshard, n=n)
        d_, m_, v_ = _adamw(w_, g_, m_, v_, "adamw_" + n)
        grads[n], deltas[n], new_m[n], new_v[n] = (back(a) for a in (g_, d_, m_, v_))
    return loss, dx.reshape(nb, s, d), grads, deltas, new_m, new_v


def kernel(x, mem, norm_mix_g, w_in, b_forget, b_gate, pool_w, pool_scale, w_pool_out, w_fox_out, w_out, norm_x_g, norm_mem_g, w_xq, w_xkv, w_xo, norm_ffn_g, w_ffn_in, w_ffn_out, norm_final_g, loss_target, m_norm_mix_g, m_w_in, m_b_forget, m_b_gate, m_pool_w, m_pool_scale, m_w_pool_out, m_w_fox_out, m_w_out, m_norm_x_g, m_norm_mem_g, m_w_xq, m_w_xkv, m_w_xo, m_norm_ffn_g, m_w_ffn_in, m_w_ffn_out, m_norm_final_g, v_norm_mix_g, v_w_in, v_b_forget, v_b_gate, v_pool_w, v_pool_scale, v_w_pool_out, v_w_fox_out, v_w_out, v_norm_x_g, v_norm_mem_g, v_w_xq, v_w_xkv, v_w_xo, v_norm_ffn_g, v_w_ffn_in, v_w_ffn_out, v_norm_final_g):
    given = dict(locals())
    weights = {n: given[n] for n in WEIGHT_ORDER}
    moments_m = {n: given["m_" + n] for n in WEIGHT_ORDER}
    moments_v = {n: given["v_" + n] for n in WEIGHT_ORDER}
    loss, grad_x, grads, deltas, new_m, new_v = _step(x, mem, loss_target, weights, moments_m, moments_v)
    return (loss, grad_x, *[grads[n] for n in WEIGHT_ORDER], *[deltas[n] for n in WEIGHT_ORDER],
            *[new_m[n] for n in WEIGHT_ORDER], *[new_v[n] for n in WEIGHT_ORDER])
```

```python
import functools
import math

import jax
import jax.numpy as jnp
from jax import lax
from jax.experimental import pallas as pl
from jax.experimental.pallas import tpu as pltpu

F32 = jnp.float32
BF16 = jnp.bfloat16
MESH = pl.DeviceIdType.MESH

D_MODEL = 1024
EPS = 1e-6
POOL_WINDOWS = (2, 4, 8, 16)
POOL_WIDTH = 512
POOL_GC = 128
FOX_HEADS = 8
FOX_DH = 64
FOX_WIDTH = 512
X_HEADS = 4
X_DH = 128
X_WIDTH = 512
D_FF = 2816
IN_COLS = 4104
GATE_WIDTH = 2048
ADAM_LR = 0.001
ADAM_B1 = 0.9
ADAM_B2 = 0.999
ADAM_EPS = 1e-08
ADAM_WD = 0.01
ADAM_STEP = 10

N_CHIPS = 4
N_DEV = 8
LANES = 128
VMEM_LIMIT_BYTES = 56 * 1024 * 1024
NEG_INF = -1e30
ATT_BLOCK = 512

SHARDED = (
    ("w_in", (1024, IN_COLS), 1),
    ("w_pool_out", (POOL_WIDTH, 1024), 1),
    ("w_fox_out", (FOX_WIDTH, 1024), 1),
    ("w_out", (1024, 1024), 0),
    ("w_xq", (1024, X_WIDTH), 0),
    ("w_xkv", (1024, 2 * X_WIDTH), 0),
    ("w_xo", (X_WIDTH, 1024), 1),
    ("w_ffn_in", (1024, 2 * D_FF), 1),
    ("w_ffn_out", (D_FF, 1024), 0),
)
SMALL = (
    ("norm_mix_g", (1, 1024)),
    ("b_forget", (1, 8)),
    ("b_gate", (1, 2048)),
    ("pool_w", (1, 4, 128, 128)),
    ("pool_scale", (1, 512)),
    ("norm_x_g", (1, 1024)),
    ("norm_mem_g", (1, 1024)),
    ("norm_ffn_g", (1, 1024)),
    ("norm_final_g", (1024,)),
)
WEIGHT_ORDER = ("norm_mix_g", "w_in", "b_forget", "b_gate", "pool_w", "pool_scale", "w_pool_out", "w_fox_out", "w_out",
                "norm_x_g", "norm_mem_g", "w_xq", "w_xkv", "w_xo", "norm_ffn_g", "w_ffn_in", "w_ffn_out", "norm_final_g")


def _round_up(n, m):
    return (n + m - 1) // m * m


SMALL_ELEMS = sum(math.prod(s) for _, s in SMALL)
SMALL_ROWS = _round_up(SMALL_ELEMS // LANES + 1, 8)
LOSS_POS = SMALL_ROWS * LANES - 1


def _cparams(sem=None):
    return pltpu.CompilerParams(dimension_semantics=sem, vmem_limit_bytes=VMEM_LIMIT_BYTES)


def _block(dim, pref, unit):
    if dim <= pref:
        return dim
    best = None
    for b in range(unit, pref + 1, unit):
        if dim % b == 0:
            best = b
    assert best is not None, (dim, pref, unit)
    return best


def _rows_block(rows, cols, unit=16, elems=1 << 19):
    return _block(rows, max(unit, elems // cols // unit * unit), unit)


def _pack_small(parts, last=None):
    flat = jnp.concatenate([p.reshape(-1).astype(F32) for p in parts])
    flat = jnp.pad(flat, (0, SMALL_ROWS * LANES - flat.shape[0]))
    if last is not None:
        flat = flat.at[LOSS_POS].set(last)
    return flat.reshape(SMALL_ROWS, LANES)


def _unpack_small(packed):
    flat = packed.reshape(-1)
    out, off = [], 0
    for _, shape in SMALL:
        n = math.prod(shape)
        out.append(flat[off:off + n].reshape(shape))
        off += n
    return out


def _my_place():
    return lax.axis_index("x"), lax.axis_index("y"), lax.axis_index("c")


def _other_chips(x, y):
    return [(1 - x, y), (x, 1 - y), (1 - x, 1 - y)]


def _chip(place):
    return 2 * place[0] + place[1]


ANY = pl.BlockSpec(memory_space=pl.ANY)


def _by_rows(rows):
    return rows % 32 == 0


def _half_shape(rows, cols):
    return (rows // 2, cols) if _by_rows(rows) else (rows, cols // 2)


def _core_half(ref, core, lead=()):
    rows, cols = ref.shape[-2:]
    if _by_rows(rows):
        return ref.at[(*lead, pl.ds(core * (rows // 2), rows // 2), slice(None))]
    return ref.at[(*lead, slice(None), pl.ds(core * (cols // 2), cols // 2))]


class _Exchange:
    def __init__(self, arrays, out_shapes, n_sems, start, finish):
        self.arrays, self.out_shapes, self.n_sems, self.start, self.finish = arrays, out_shapes, n_sems, start, finish

    def scratch(self):
        return [pltpu.SemaphoreType.DMA((self.n_sems,)), pltpu.SemaphoreType.DMA((self.n_sems,))]


def _run_exchange(ex, name):
    n = len(ex.arrays)

    def body(*refs):
        ins, outs, sems = refs[:n], refs[n:2 * n], refs[2 * n:]
        ex.start(ins, outs, *sems)
        ex.finish(ins, outs, *sems)

    return pl.pallas_call(
        body, name=name, out_shape=ex.out_shapes, in_specs=[ANY] * n, out_specs=[ANY] * n, scratch_shapes=ex.scratch(),
    )(*ex.arrays)


def _hosted_call(body, ex, *, name, grid, in_specs, out_specs, out_shape, args, scratch=()):
    n_in, n_out, n_scr = len(args), len(out_shape), len(scratch)
    if ex is None:
        outs = pl.pallas_call(
            body, name=name, grid=grid, out_shape=out_shape, in_specs=in_specs, out_specs=out_specs,
            scratch_shapes=list(scratch), compiler_params=_cparams(("arbitrary",) * len(grid)))(*args)
        return outs, None
    nc = len(ex.arrays)

    def full_body(*refs):
        ins, cins = refs[:n_in], refs[n_in:n_in + nc]
        outs, couts = refs[n_in + nc:n_in + nc + n_out], refs[n_in + nc + n_out:n_in + 2 * nc + n_out]
        rest = refs[n_in + 2 * nc + n_out:]
        scr, sems = rest[:n_scr], rest[n_scr:]
        first = functools.reduce(jnp.logical_and, [pl.program_id(a) == 0 for a in range(len(grid))])
        last = functools.reduce(jnp.logical_and, [pl.program_id(a) == grid[a] - 1 for a in range(len(grid))])

        @pl.when(first)
        def _():
            ex.start(cins, couts, *sems)

        body(*ins, *outs, *scr)

        @pl.when(last)
        def _():
            ex.finish(cins, couts, *sems)

    outs = pl.pallas_call(
        full_body, name=name, grid=grid, out_shape=list(out_shape) + list(ex.out_shapes),
        in_specs=list(in_specs) + [ANY] * nc, out_specs=list(out_specs) + [ANY] * nc,
        scratch_shapes=list(scratch) + ex.scratch(),
        compiler_params=_cparams(("arbitrary",) * len(grid)))(*args, *ex.arrays)
    return outs[:n_out], outs[n_out:]


def _gather_exchange(shards):
    n = len(shards)

    def copies(ins, outs, send_sems, recv_sems):
        x, y, c = _my_place()

        def half(k, chip, core):
            return _core_half(outs[k], core, lead=(_chip(chip),))

        def copy(k, slot, chip, core, to, src=None):
            return pltpu.make_async_remote_copy(
                src_ref=half(k, chip, core) if src is None else src, dst_ref=half(k, chip, core),
                send_sem=send_sems.at[6 * k + slot], recv_sem=recv_sems.at[6 * k + slot],
                device_id=to, device_id_type=MESH)

        return (x, y, c), copy

    def first_copies(ins, outs, send_sems, recv_sems):
        (x, y, c), copy = copies(ins, outs, send_sems, recv_sems)
        out = []
        for j, chip in enumerate(_other_chips(x, y)):
            for k in range(n):
                out.append(copy(k, j, (x, y), c, (*chip, c), src=_core_half(ins[k], c)))
        return out

    def start(ins, outs, send_sems, recv_sems):
        for cp in first_copies(ins, outs, send_sems, recv_sems):
            cp.start()

    def finish(ins, outs, send_sems, recv_sems):
        (x, y, c), copy = copies(ins, outs, send_sems, recv_sems)
        chips = _other_chips(x, y)
        passed = []
        for j, chip in enumerate(chips):
            for k in range(n):
                copy(k, j, chip, c, (x, y, c)).wait_recv()
                passed.append(copy(k, 3 + j, chip, c, (x, y, 1 - c)))
                passed[-1].start()
        for j, chip in enumerate(chips):
            for k in range(n):
                copy(k, 3 + j, chip, 1 - c, (x, y, c)).wait_recv()
        for cp in first_copies(ins, outs, send_sems, recv_sems) + passed:
            cp.wait_send()

    return _Exchange(list(shards), [jax.ShapeDtypeStruct((N_CHIPS,) + s.shape, s.dtype) for s in shards], 6 * n,
                     start, finish)


def _place_own(stacks, shards):
    me = 2 * lax.axis_index("x") + lax.axis_index("y")
    return [lax.dynamic_update_slice(others, mine[None], (me, 0, 0)) for others, mine in zip(stacks, shards)]


def _swap_exchange(grads):
    n = len(grads)

    def copies(ins, outs, send_sems, recv_sems):
        x, y, c = _my_place()
        return [pltpu.make_async_remote_copy(
            src_ref=_core_half(ins[k], 1 - c, lead=(slice(None),)), dst_ref=outs[k],
            send_sem=send_sems.at[k], recv_sem=recv_sems.at[k], device_id=(x, y, 1 - c), device_id_type=MESH)
            for k in range(n)]

    def start(ins, outs, send_sems, recv_sems):
        for cp in copies(ins, outs, send_sems, recv_sems):
            cp.start()

    def finish(ins, outs, send_sems, recv_sems):
        for cp in copies(ins, outs, send_sems, recv_sems):
            cp.wait()

    return _Exchange(list(grads), [jax.ShapeDtypeStruct((N_CHIPS,) + _half_shape(*g.shape[1:]), g.dtype) for g in grads],
                     n, start, finish)


def _chips_exchange(sums):
    n = len(sums)

    def sends(ins, outs, send_sems, recv_sems):
        x, y, c = _my_place()
        return [pltpu.make_async_remote_copy(
            src_ref=ins[k].at[_chip(chip)], dst_ref=outs[k].at[_chip((x, y))],
            send_sem=send_sems.at[3 * k + j], recv_sem=recv_sems.at[3 * k + j],
            device_id=(*chip, c), device_id_type=MESH)
            for j, chip in enumerate(_other_chips(x, y)) for k in range(n)]

    def start(ins, outs, send_sems, recv_sems):
        for cp in sends(ins, outs, send_sems, recv_sems):
            cp.start()

    def finish(ins, outs, send_sems, recv_sems):
        x, y, c = _my_place()
        for j, chip in enumerate(_other_chips(x, y)):
            for k in range(n):
                slot = outs[k].at[_chip(chip)]
                pltpu.make_async_remote_copy(
                    src_ref=slot, dst_ref=slot, send_sem=send_sems.at[3 * k + j], recv_sem=recv_sems.at[3 * k + j],
                    device_id=(x, y, c), device_id_type=MESH).wait_recv()
        for cp in sends(ins, outs, send_sems, recv_sems):
            cp.wait_send()

    return _Exchange(list(sums), [jax.ShapeDtypeStruct(s.shape, s.dtype) for s in sums], 3 * n, start, finish)


def _join_halves(shards):
    n = len(shards)

    def body(*refs):
        ins, outs = refs[:n], refs[n:2 * n]
        send_sems, recv_sems = refs[2 * n:]
        x, y, c = _my_place()
        sends = []
        for k in range(n):
            sends.append(pltpu.make_async_remote_copy(
                src_ref=_core_half(ins[k], c), dst_ref=_core_half(outs[k], c),
                send_sem=send_sems.at[k], recv_sem=recv_sems.at[k], device_id=(x, y, 1 - c), device_id_type=MESH))
            sends[-1].start()
        for k in range(n):
            theirs = _core_half(outs[k], 1 - c)
            pltpu.make_async_remote_copy(
                src_ref=theirs, dst_ref=theirs, send_sem=send_sems.at[k], recv_sem=recv_sems.at[k],
                device_id=(x, y, c), device_id_type=MESH).wait_recv()
        for cp in sends:
            cp.wait_send()

    return pl.pallas_call(
        body, name="join_halves",
        out_shape=[jax.ShapeDtypeStruct(s.shape, s.dtype) for s in shards],
        in_specs=[ANY] * n, out_specs=[ANY] * n,
        input_output_aliases={k: k for k in range(n)},
        scratch_shapes=[pltpu.SemaphoreType.DMA((n,)), pltpu.SemaphoreType.DMA((n,))],
    )(*shards)


def _gather_small(block):
    m_per = block.shape[0]

    def body(x_ref, out_ref, send_sems, recv_sems, local_sem):
        x, y, c = _my_place()
        me, sibling = (x, y, c), (x, y, 1 - c)
        chips = _other_chips(x, y)

        def rows(px, py, pc):
            return out_ref.at[pl.ds((4 * px + 2 * py + pc) * m_per, m_per), :]

        def copy(k, blk, to, src=None):
            return pltpu.make_async_remote_copy(
                src_ref=rows(*blk) if src is None else src, dst_ref=rows(*blk),
                send_sem=send_sems.at[k], recv_sem=recv_sems.at[k], device_id=to, device_id_type=MESH)

        mine = pltpu.make_async_copy(x_ref, rows(*me), local_sem)
        mine.start()
        first = [copy(0, me, sibling, src=x_ref)]
        first += [copy(1 + j, me, (*chip, c), src=x_ref) for j, chip in enumerate(chips)]
        for cp in first:
            cp.start()
        passed = [copy(4 + j, (*chip, c), sibling) for j, chip in enumerate(chips)]
        for j, chip in enumerate(chips):
            copy(1 + j, (*chip, c), me).wait_recv()
            passed[j].start()
        copy(0, sibling, me).wait_recv()
        for j, chip in enumerate(chips):
            copy(4 + j, (*chip, 1 - c), me).wait_recv()
        for cp in first + passed:
            cp.wait_send()
        mine.wait()

    return pl.pallas_call(
        body, name="gather_small",
        out_shape=jax.ShapeDtypeStruct((N_DEV * m_per, LANES), block.dtype),
        in_specs=[pl.BlockSpec(memory_space=pltpu.VMEM)],
        out_specs=pl.BlockSpec(memory_space=pltpu.VMEM),
        scratch_shapes=[pltpu.SemaphoreType.DMA((7,)), pltpu.SemaphoreType.DMA((7,)), pltpu.SemaphoreType.DMA],
    )(block)


def _sum_halves(grads, theirs, core, name):
    _, h, cols = theirs.shape
    by_rows = _by_rows(grads.shape[1])
    br = _rows_block(h, cols) if by_rows else h
    nb = h // br

    def body(core_ref, a_ref, b_ref, o_ref):
        o_ref[...] = (a_ref[...] + b_ref[...]).astype(BF16)

    if by_rows:
        mine = pl.BlockSpec((1, br, cols), lambda j, i, core_ref: (j, core_ref[0] * nb + i, 0))
    else:
        mine = pl.BlockSpec((1, br, cols), lambda j, i, core_ref: (j, i, core_ref[0]))
    return pl.pallas_call(
        body, name=name,
        out_shape=jax.ShapeDtypeStruct(theirs.shape, BF16),
        grid_spec=pltpu.PrefetchScalarGridSpec(
            num_scalar_prefetch=1, grid=(N_CHIPS, nb),
            in_specs=[mine, pl.BlockSpec((1, br, cols), lambda j, i, core_ref: (j, i, 0))],
            out_specs=pl.BlockSpec((1, br, cols), lambda j, i, core_ref: (j, i, 0))),
        compiler_params=_cparams(("parallel", "parallel")),
    )(core, grads, theirs)


def _sum_chips(slots, sums, place, by_rows, name):
    _, h, cols = slots.shape
    br = _rows_block(h, cols) if by_rows else h
    nb = h // br

    def body(place_ref, s_ref, own_ref, o_ref):
        me = place_ref[1]
        acc = None
        for k in range(N_CHIPS):
            term = jnp.where(me == k, own_ref[k], s_ref[k]).astype(F32)
            acc = term if acc is None else acc + term
        o_ref[...] = acc

    stack = pl.BlockSpec((N_CHIPS, br, cols), lambda i, place_ref: (0, i, 0))
    if by_rows:
        out_shape, out_map = (2 * h, cols), lambda i, place_ref: (place_ref[0] * nb + i, 0)
    else:
        out_shape, out_map = (h, 2 * cols), lambda i, place_ref: (i, place_ref[0])
    return pl.pallas_call(
        body, name=name,
        out_shape=jax.ShapeDtypeStruct(out_shape, F32),
        grid_spec=pltpu.PrefetchScalarGridSpec(
            num_scalar_prefetch=1, grid=(nb,), in_specs=[stack, stack],
            out_specs=pl.BlockSpec((br, cols), out_map)),
        compiler_params=_cparams(("parallel",)),
    )(place, slots, sums)


def _adamw_math(w, g, m, v):
    m = ADAM_B1 * m + (1.0 - ADAM_B1) * g
    v = ADAM_B2 * v + (1.0 - ADAM_B2) * (g * g)
    m_hat = m / (1.0 - ADAM_B1 ** ADAM_STEP)
    v_hat = v / (1.0 - ADAM_B2 ** ADAM_STEP)
    delta = -ADAM_LR * (m_hat / (jnp.sqrt(v_hat) + ADAM_EPS) + ADAM_WD * w)
    return delta, m, v


def _adamw(w, g, m, v, name):
    def body(w_ref, g_ref, m_ref, v_ref, d_ref, nm_ref, nv_ref):
        d, nm, nv = _adamw_math(w_ref[...], g_ref[...], m_ref[...], v_ref[...])
        d_ref[...] = d
        nm_ref[...] = nm
        nv_ref[...] = nv

    if w.ndim == 3:
        rows = w.shape[0]
        br = max(b for b in range(1, 65) if rows % b == 0)
        spec, steps = pl.BlockSpec((br,) + w.shape[1:], lambda i: (i, 0, 0)), rows // br
    else:
        rows, cols = w.shape
        br = _rows_block(rows, cols, unit=8)
        spec, steps = pl.BlockSpec((br, cols), lambda i: (i, 0)), rows // br
    shape = jax.ShapeDtypeStruct(w.shape, F32)
    return pl.pallas_call(
        body, name=name, out_shape=(shape, shape, shape), grid=(steps,),
        in_specs=[spec] * 4, out_specs=(spec, spec, spec),
        compiler_params=_cparams(("parallel",)),
    )(w, g, m, v)


def _adamw_small(parts, w, m, v):
    def body(p_ref, w_ref, m_ref, v_ref, g_ref, d_ref, nm_ref, nv_ref):
        g = p_ref[0]
        for k in range(1, N_DEV):
            g = g + p_ref[k]
        d, nm, nv = _adamw_math(w_ref[...], g, m_ref[...], v_ref[...])
        g_ref[...] = g
        d_ref[...] = d
        nm_ref[...] = nm
        nv_ref[...] = nv

    shape = jax.ShapeDtypeStruct((SMALL_ROWS, LANES), F32)
    return pl.pallas_call(body, name="adamw_small", out_shape=(shape,) * 4, compiler_params=_cparams())(parts, w, m, v)


def _mm(a, b, *, name, ta=False, out_dtype=F32, res=None, bm=1024, bn=1024, bk=4096, b_stack=False, out_stack=False,
        ex=None):
    if ta:
        kdim, m = a.shape
    else:
        m, kdim = a.shape
    if b_stack:
        _, kb, chunk = b.shape
        n = N_CHIPS * chunk
    else:
        kb, n = b.shape
        chunk = n // N_CHIPS if out_stack else n
    assert kdim == kb, (a.shape, b.shape, ta)
    bm = _block(m, bm, LANES if ta else 16)
    bn = _block(chunk, bn, LANES)
    bk = _block(kdim, bk, LANES)
    nk = kdim // bk
    per_chunk = chunk // bn
    dims = (((0 if ta else 1,), (0,)), ((), ()))

    def body(*refs):
        refs = list(refs)
        a_ref, b_ref = refs[:2]
        r_ref = refs[2] if res is not None else None
        o_ref = refs[3] if res is not None else refs[2]
        part = lax.dot_general(a_ref[...].astype(BF16), b_ref[...].astype(BF16), dims, preferred_element_type=F32)

        def finish(r):
            if r_ref is not None:
                r = r + r_ref[...]
            o_ref[...] = r.astype(out_dtype)

        if nk == 1:
            finish(part)
        else:
            acc_ref = refs[-1]
            k = pl.program_id(2)

            @pl.when(k == 0)
            def _():
                acc_ref[...] = part

            @pl.when(k > 0)
            def _():
                acc_ref[...] += part

            @pl.when(k == nk - 1)
            def _():
                finish(acc_ref[...])

    a_spec = pl.BlockSpec((bk, bm), lambda i, j, k: (k, i)) if ta else pl.BlockSpec((bm, bk), lambda i, j, k: (i, k))
    if b_stack:
        b_spec = pl.BlockSpec((None, bk, bn), lambda i, j, k: (j // per_chunk, k, j % per_chunk))
    else:
        b_spec = pl.BlockSpec((bk, bn), lambda i, j, k: (k, j))
    r_spec = pl.BlockSpec((bm, bn), lambda i, j, k: (i, j))
    if out_stack:
        o_spec = pl.BlockSpec((None, bm, bn), lambda i, j, k: (j // per_chunk, i, j % per_chunk))
        o_shape = (N_CHIPS, m, chunk)
    else:
        o_spec, o_shape = r_spec, (m, n)
    in_specs = [a_spec, b_spec] + ([r_spec] if res is not None else [])
    args = (a, b) + ((res,) if res is not None else ())
    (out,), moved = _hosted_call(
        body, ex, name=name, out_shape=(jax.ShapeDtypeStruct(o_shape, out_dtype),),
        grid=(m // bm, n // bn, nk), in_specs=in_specs, out_specs=(o_spec,),
        scratch=[pltpu.VMEM((bm, bn), F32)] if nk > 1 else [], args=args)
    return out if ex is None else (out, moved)


def _rms_fwd(x, g, name):
    t, d = x.shape
    bt = _block(t, 512, 16)

    def body(x_ref, g_ref, h_ref):
        xv = x_ref[...]
        r = lax.rsqrt(jnp.mean(xv * xv, axis=-1, keepdims=True) + EPS)
        h_ref[...] = (xv * r * g_ref[...]).astype(BF16)

    return pl.pallas_call(
        body, name=name, out_shape=jax.ShapeDtypeStruct((t, d), BF16), grid=(t // bt,),
        in_specs=[pl.BlockSpec((bt, d), lambda i: (i, 0)), pl.BlockSpec((1, d), lambda i: (0, 0))],
        out_specs=pl.BlockSpec((bt, d), lambda i: (i, 0)),
        compiler_params=_cparams(("parallel",)),
    )(x, g)


def _rms_bwd(dh, x, g, dres, name):
    t, d = x.shape
    bt = _block(t, 256, 16)
    want_dx = dres is not None

    def body(*refs):
        if want_dx:
            dh_ref, x_ref, g_ref, dres_ref, dx_ref, dxb_ref, dg_ref = refs
        else:
            dh_ref, x_ref, g_ref, dg_ref = refs
        xv = x_ref[...]
        r = lax.rsqrt(jnp.mean(xv * xv, axis=-1, keepdims=True) + EPS)
        xhat = xv * r
        dhv = dh_ref[...]

        @pl.when(pl.program_id(0) == 0)
        def _():
            dg_ref[...] = jnp.zeros_like(dg_ref)

        dg_ref[...] += jnp.sum(dhv * xhat, axis=0, keepdims=True)
        if want_dx:
            dxhat = dhv * g_ref[...]
            dx = dres_ref[...] + r * (dxhat - xhat * jnp.mean(dxhat * xhat, axis=-1, keepdims=True))
            dx_ref[...] = dx
            dxb_ref[...] = dx.astype(BF16)

    row = pl.BlockSpec((bt, d), lambda i: (i, 0))
    vec = pl.BlockSpec((1, d), lambda i: (0, 0))
    if want_dx:
        return pl.pallas_call(
            body, name=name, grid=(t // bt,),
            out_shape=(jax.ShapeDtypeStruct((t, d), F32), jax.ShapeDtypeStruct((t, d), BF16),
                       jax.ShapeDtypeStruct((1, d), F32)),
            in_specs=[row, row, vec, row], out_specs=(row, row, vec),
            compiler_params=_cparams(("arbitrary",)),
        )(dh, x, g, dres)
    return pl.pallas_call(
        body, name=name, grid=(t // bt,), out_shape=jax.ShapeDtypeStruct((1, d), F32),
        in_specs=[row, row, vec], out_specs=vec,
        compiler_params=_cparams(("arbitrary",)),
    )(dh, x, g)


def _final_loss(x, target, g):
    t, d = x.shape
    bt = _block(t, 256, 16)

    def body(x_ref, t_ref, g_ref, dx_ref, dxb_ref, dg_ref, loss_ref):
        xv = x_ref[...]
        gv = g_ref[...]
        r = lax.rsqrt(jnp.mean(xv * xv, axis=-1, keepdims=True) + EPS)
        xhat = xv * r
        err = xhat * gv - t_ref[...]

        @pl.when(pl.program_id(0) == 0)
        def _():
            dg_ref[...] = jnp.zeros_like(dg_ref)
            loss_ref[...] = jnp.zeros_like(loss_ref)

        loss_ref[...] += 0.5 * jnp.sum(jnp.mean(err * err, axis=-1, keepdims=True), axis=0, keepdims=True)
        dy = err * (1.0 / d)
        dg_ref[...] += jnp.sum(dy * xhat, axis=0, keepdims=True)
        dxhat = dy * gv
        dx = r * (dxhat - xhat * jnp.mean(dxhat * xhat, axis=-1, keepdims=True))
        dx_ref[...] = dx
        dxb_ref[...] = dx.astype(BF16)

    row = pl.BlockSpec((bt, d), lambda i: (i, 0))
    vec = pl.BlockSpec((1, d), lambda i: (0, 0))
    return pl.pallas_call(
        body, name="final_loss", grid=(t // bt,),
        out_shape=(jax.ShapeDtypeStruct((t, d), F32), jax.ShapeDtypeStruct((t, d), BF16),
                   jax.ShapeDtypeStruct((1, d), F32), jax.ShapeDtypeStruct((1, LANES), F32)),
        in_specs=[row, row, vec], out_specs=(row, row, vec, pl.BlockSpec((1, LANES), lambda i: (0, 0))),
        compiler_params=_cparams(("arbitrary",)),
    )(x, target, g)


GU_COLS = GATE_WIDTH + POOL_WIDTH
U_BLK = GATE_WIDTH // POOL_WIDTH


def _shift_down(a, k, row):
    return jnp.where(row >= k, pltpu.roll(a, k, 0), 0.0)


def _shift_up(a, k, row):
    n = a.shape[0]
    return jnp.where(row < n - k, pltpu.roll(a, n - k, 0), 0.0)


def _window_delta(u, w, row):
    s, k = u, 1
    while k < w:
        s = s + _shift_down(s, k, row)
        k *= 2
    cnt = jnp.minimum(row + 1, w).astype(F32)
    return s / cnt - u, cnt


def _pool_fwd(gu, pool_w, pool_scale):
    b, s, _ = gu.shape

    def body(u_ref, pw_ref, sc_ref, y_ref):
        row = lax.broadcasted_iota(jnp.int32, (s, POOL_GC), 0)
        for g, w in enumerate(POOL_WINDOWS):
            cols = slice(g * POOL_GC, (g + 1) * POOL_GC)
            d, _ = _window_delta(u_ref[0, :, cols], w, row)
            z = jnp.dot(d.astype(BF16), pw_ref[g].astype(BF16), preferred_element_type=F32)
            y_ref[0, :, cols] = (z * sc_ref[:, cols]).astype(BF16)

    return pl.pallas_call(
        body, name="pool_fwd", out_shape=jax.ShapeDtypeStruct((b, s, POOL_WIDTH), BF16), grid=(b,),
        in_specs=[pl.BlockSpec((1, s, POOL_WIDTH), lambda i: (i, 0, U_BLK)),
                  pl.BlockSpec((4, POOL_GC, POOL_GC), lambda i: (0, 0, 0)),
                  pl.BlockSpec((1, POOL_WIDTH), lambda i: (0, 0))],
        out_specs=pl.BlockSpec((1, s, POOL_WIDTH), lambda i: (i, 0, 0)),
        compiler_params=_cparams(("parallel",)),
    )(gu, pool_w, pool_scale)


def _pool_bwd(gu, dy, pool_w, pool_scale, dgu):
    b, s, _ = gu.shape

    def body(u_ref, dy_ref, pw_ref, sc_ref, dgu_in, du_ref, dpw_ref, dsc_ref):
        del dgu_in

        @pl.when(pl.program_id(0) == 0)
        def _():
            dpw_ref[...] = jnp.zeros_like(dpw_ref)
            dsc_ref[...] = jnp.zeros_like(dsc_ref)

        row = lax.broadcasted_iota(jnp.int32, (s, POOL_GC), 0)
        for g, w in enumerate(POOL_WINDOWS):
            cols = slice(g * POOL_GC, (g + 1) * POOL_GC)
            d, cnt = _window_delta(u_ref[0, :, cols], w, row)
            db = d.astype(BF16)
            pw = pw_ref[g].astype(BF16)
            z = jnp.dot(db, pw, preferred_element_type=F32)
            dyv = dy_ref[0, :, cols]
            dsc_ref[:, cols] += jnp.sum(dyv * z, axis=0, keepdims=True)
            dz = (dyv * sc_ref[:, cols]).astype(BF16)
            dpw_ref[g] += lax.dot_general(db, dz, (((0,), (0,)), ((), ())), preferred_element_type=F32)
            dd = lax.dot_general(dz, pw, (((1,), (1,)), ((), ())), preferred_element_type=F32)
            acc, k = dd / cnt, 1
            while k < w:
                acc = acc + _shift_up(acc, k, row)
                k *= 2
            du_ref[0, :, cols] = (acc - dd).astype(BF16)

    return pl.pallas_call(
        body, name="pool_bwd", grid=(b,),
        out_shape=(jax.ShapeDtypeStruct((b, s, GU_COLS), BF16), jax.ShapeDtypeStruct((4, POOL_GC, POOL_GC), F32),
                   jax.ShapeDtypeStruct((1, POOL_WIDTH), F32)),
        in_specs=[pl.BlockSpec((1, s, POOL_WIDTH), lambda i: (i, 0, U_BLK)),
                  pl.BlockSpec((1, s, POOL_WIDTH), lambda i: (i, 0, 0)),
                  pl.BlockSpec((4, POOL_GC, POOL_GC), lambda i: (0, 0, 0)),
                  pl.BlockSpec((1, POOL_WIDTH), lambda i: (0, 0)), ANY],
        out_specs=(pl.BlockSpec((1, s, POOL_WIDTH), lambda i: (i, 0, U_BLK)),
                   pl.BlockSpec((4, POOL_GC, POOL_GC), lambda i: (0, 0, 0)),
                   pl.BlockSpec((1, POOL_WIDTH), lambda i: (0, 0))),
        input_output_aliases={4: 0},
        compiler_params=_cparams(("arbitrary",)),
    )(gu, dy, pool_w, pool_scale, dgu)


def _forget_cumsum(f, bias, name):
    b, s, c = f.shape

    def body(f_ref, b_ref, c_ref):
        row = lax.broadcasted_iota(jnp.int32, (s, LANES), 0)
        z = f_ref[0] + b_ref[...]
        acc = jnp.minimum(z, 0.0) - jnp.log(1.0 + jnp.exp(-jnp.abs(z)))
        k = 1
        while k < s:
            acc = acc + _shift_down(acc, k, row)
            k *= 2
        c_ref[0] = acc

    return pl.pallas_call(
        body, name=name, out_shape=jax.ShapeDtypeStruct((b, s, c), F32), grid=(b, c // LANES),
        in_specs=[pl.BlockSpec((1, s, LANES), lambda i, j: (i, 0, j)), pl.BlockSpec((1, LANES), lambda i, j: (0, j))],
        out_specs=pl.BlockSpec((1, s, LANES), lambda i, j: (i, 0, j)),
        compiler_params=_cparams(("parallel", "parallel")),
    )(f, bias)


def _forget_bwd(dc, f, bias):
    b, s, _ = f.shape

    def body(dc_ref, f_ref, b_ref, df_ref, db_ref):
        @pl.when(pl.program_id(0) == 0)
        def _():
            db_ref[...] = jnp.zeros_like(db_ref)

        row = lax.broadcasted_iota(jnp.int32, (s, LANES), 0)
        acc, k = dc_ref[0], 1
        while k < s:
            acc = acc + _shift_up(acc, k, row)
            k *= 2
        z = f_ref[0] + b_ref[...]
        df = acc / (1.0 + jnp.exp(z))
        db_ref[...] += jnp.sum(df, axis=0, keepdims=True)
        df_ref[0] = df.astype(BF16)

    blk = pl.BlockSpec((1, s, LANES), lambda i: (i, 0, 0))
    vec = pl.BlockSpec((1, LANES), lambda i: (0, 0))
    return pl.pallas_call(
        body, name="forget_bwd", grid=(b,),
        out_shape=(jax.ShapeDtypeStruct((b, s, LANES), BF16), jax.ShapeDtypeStruct((1, LANES), F32)),
        in_specs=[blk, blk, vec], out_specs=(blk, vec),
        compiler_params=_cparams(("arbitrary",)),
    )(dc, f, bias)


KV_BLK0 = 2
PAIRS = FOX_HEADS // 2
FOX_SCALE = FOX_DH ** -0.5
NT_DIMS = (((1,), (1,)), ((), ()))
TN_DIMS = (((0,), (0,)), ((), ()))


def _stack_heads(v):
    head = lax.broadcasted_iota(jnp.int32, v.shape, 1) // FOX_DH
    zero = jnp.zeros_like(v)
    return jnp.concatenate([jnp.where(head == 0, v, zero), jnp.where(head == 1, v, zero)], axis=0)


def _stack_cols(v):
    return jnp.concatenate([v[:, 0:1], v[:, FOX_DH:FOX_DH + 1]], axis=0)


def _unstack(t, blk):
    head = lax.broadcasted_iota(jnp.int32, (blk, LANES), 1) // FOX_DH
    return jnp.where(head == 0, t[:blk], t[blk:])


def _fox_scores(q_all, kblk, row_bias, cr_ref, kb, masked, blk):
    top = lax.broadcasted_iota(jnp.int32, (2 * blk, 1), 0) < blk
    s = lax.dot_general(q_all, kblk, NT_DIMS, preferred_element_type=F32)
    s = s + (row_bias - jnp.where(top, cr_ref[0, 0, kb], cr_ref[0, 1, kb]))
    if masked:
        r = lax.broadcasted_iota(jnp.int32, (2 * blk, blk), 0)
        keep = jnp.where(r >= blk, r - blk, r) >= lax.broadcasted_iota(jnp.int32, (2 * blk, blk), 1)
        s = jnp.where(keep, s, NEG_INF)
    return s


def _fox_fwd(qkv, c_exp, c_row, ex=None):
    b, s, _ = qkv.shape
    blk = min(ATT_BLOCK, s)
    nq = s // blk

    def body(q_ref, kv_ref, cc_ref, cr_ref, o_ref, ob_ref, lse_ref):
        qi = pl.program_id(2)
        q_all = _stack_heads(q_ref[0] * FOX_SCALE)
        cq = _stack_cols(cc_ref[0])

        def step(kb, carry, masked):
            m, l, acc = carry
            rows = pl.ds(pl.multiple_of(kb * blk, blk), blk)
            sc = _fox_scores(q_all, kv_ref[0, rows, :LANES], cq, cr_ref, kb, masked, blk)
            m_new = jnp.maximum(m, jnp.max(sc, axis=-1, keepdims=True))
            p = jnp.exp(sc - m_new)
            alpha = jnp.exp(m - m_new)
            l = alpha * l + jnp.sum(p, axis=-1, keepdims=True)
            acc = alpha * acc + jnp.dot(p.astype(BF16), kv_ref[0, rows, LANES:], preferred_element_type=F32)
            return m_new, l, acc

        init = (jnp.full((2 * blk, 1), NEG_INF, F32), jnp.zeros((2 * blk, 1), F32), jnp.zeros((2 * blk, LANES), F32))
        m, l, acc = step(qi, lax.fori_loop(0, qi, functools.partial(step, masked=False), init), True)
        o = _unstack(acc / l, blk)
        o_ref[0] = o
        ob_ref[0] = o.astype(BF16)
        lse_ref[0] = _unstack(jnp.broadcast_to(m + jnp.log(l), (2 * blk, LANES)), blk)

    tile = pl.BlockSpec((1, blk, LANES), lambda i, h, q: (i, q, h))
    kvspec = pl.BlockSpec((1, s, 2 * LANES), lambda i, h, q: (i, 0, KV_BLK0 + h))
    shape = jax.ShapeDtypeStruct((b, s, FOX_WIDTH), F32)
    return _hosted_call(
        body, ex, name="fox_fwd", out_shape=(shape, jax.ShapeDtypeStruct((b, s, FOX_WIDTH), BF16), shape),
        grid=(b, PAIRS, nq),
        in_specs=[tile, kvspec, tile, pl.BlockSpec((1, 2, nq, 1, blk), lambda i, h, q: (i, h, 0, 0, 0))],
        out_specs=(tile, tile, tile), args=(qkv, qkv, c_exp, c_row))


def _fox_bwd(qkv, c_exp, c_row, lse, o, do, ex=None):
    b, s, _ = qkv.shape
    blk = min(ATT_BLOCK, s)
    nq = s // blk

    def body(q_ref, kv_ref, cc_ref, cr_ref, lse_ref, o_ref, do_ref, dq_ref, dkv_ref, dcq_ref, dc_ref, dk_acc, dv_acc):
        qi = pl.program_id(2)

        @pl.when(qi == 0)
        def _():
            dk_acc[...] = jnp.zeros_like(dk_acc)
            dv_acc[...] = jnp.zeros_like(dv_acc)
            dc_ref[...] = jnp.zeros_like(dc_ref)

        q_all = _stack_heads(q_ref[0] * FOX_SCALE)
        dov = do_ref[0]
        do_all = _stack_heads(dov.astype(BF16))
        delta = jnp.sum(_stack_heads(dov * o_ref[0]), axis=-1, keepdims=True)
        bias = _stack_cols(cc_ref[0]) - _stack_cols(lse_ref[0])

        def step(kb, carry, masked):
            acc, dcq = carry
            rows = pl.ds(pl.multiple_of(kb * blk, blk), blk)
            kblk = kv_ref[0, rows, :LANES]
            p = jnp.exp(_fox_scores(q_all, kblk, bias, cr_ref, kb, masked, blk))
            dp = lax.dot_general(do_all, kv_ref[0, rows, LANES:], NT_DIMS, preferred_element_type=F32)
            ds = p * (dp - delta)
            dsb = ds.astype(BF16)
            dv_acc[rows, :] += lax.dot_general(p.astype(BF16), do_all, TN_DIMS, preferred_element_type=F32)
            dk_acc[rows, :] += lax.dot_general(dsb, q_all, TN_DIMS, preferred_element_type=F32)
            dc_ref[0, 0, kb] -= jnp.sum(ds[:blk], axis=0, keepdims=True)
            dc_ref[0, 1, kb] -= jnp.sum(ds[blk:], axis=0, keepdims=True)
            acc = acc + jnp.dot(dsb, kblk, preferred_element_type=F32)
            return acc, dcq + jnp.sum(ds, axis=-1, keepdims=True)

        init = (jnp.zeros((2 * blk, LANES), F32), jnp.zeros((2 * blk, 1), F32))
        acc, dcq = step(qi, lax.fori_loop(0, qi, functools.partial(step, masked=False), init), True)
        dq_ref[0] = (_unstack(acc, blk) * FOX_SCALE).astype(BF16)
        dcq_ref[0, 0] = jnp.where(lax.broadcasted_iota(jnp.int32, (blk, 2), 1) == 0, dcq[:blk], dcq[blk:])

        @pl.when(qi == nq - 1)
        def _():
            dkv_ref[0, :, :LANES] = dk_acc[...].astype(BF16)
            dkv_ref[0, :, LANES:] = dv_acc[...].astype(BF16)

    tile = pl.BlockSpec((1, blk, LANES), lambda i, h, q: (i, q, h))
    kvspec = pl.BlockSpec((1, s, 2 * LANES), lambda i, h, q: (i, 0, KV_BLK0 + h))
    crow = pl.BlockSpec((1, 2, nq, 1, blk), lambda i, h, q: (i, h, 0, 0, 0))
    return _hosted_call(
        body, ex, name="fox_bwd", grid=(b, PAIRS, nq),
        out_shape=(jax.ShapeDtypeStruct((b, s, FOX_WIDTH), BF16), jax.ShapeDtypeStruct((b, s, 2 * FOX_WIDTH), BF16),
                   jax.ShapeDtypeStruct((b, PAIRS, s, 2), F32), jax.ShapeDtypeStruct(c_row.shape, F32)),
        in_specs=[tile, kvspec, tile, crow, tile, tile, tile],
        out_specs=(tile, pl.BlockSpec((1, s, 2 * LANES), lambda i, h, q: (i, 0, h)),
                   pl.BlockSpec((1, 1, blk, 2), lambda i, h, q: (i, h, q, 0)), crow),
        scratch=[pltpu.VMEM((s, LANES), F32), pltpu.VMEM((s, LANES), F32)],
        args=(qkv, qkv, c_exp, c_row, lse, o, do))


def _sigmoid(z):
    return 1.0 / (1.0 + jnp.exp(-z))


def _mix_fwd(gu, b_gate, y_pool, y_fox):
    t = gu.shape[0]
    bt = _block(t, 256, 16)

    def body(gp_ref, gf_ref, bp_ref, bf_ref, yp_ref, yf_ref, o_ref):
        gp = _sigmoid(gp_ref[...] + bp_ref[...])
        gf = _sigmoid(gf_ref[...] + bf_ref[...])
        o_ref[...] = (gp * yp_ref[...] + gf * yf_ref[...]).astype(BF16)

    col = lambda j: pl.BlockSpec((bt, D_MODEL), lambda i: (i, j))
    vec = lambda j: pl.BlockSpec((1, D_MODEL), lambda i: (0, j))
    return pl.pallas_call(
        body, name="mix_fwd", out_shape=jax.ShapeDtypeStruct((t, D_MODEL), BF16), grid=(t // bt,),
        in_specs=[col(0), col(1), vec(0), vec(1), col(0), col(0)], out_specs=col(0),
        compiler_params=_cparams(("parallel",)),
    )(gu, gu, b_gate, b_gate, y_pool, y_fox)


def _mix_bwd(gu, b_gate, y_pool, y_fox, dmix):
    t = gu.shape[0]
    bt = _block(t, 256, 16)

    def body(gp_ref, gf_ref, bp_ref, bf_ref, yp_ref, yf_ref, dm_ref, dyp_ref, dyf_ref, dgl_ref, db_ref):
        @pl.when(pl.program_id(0) == 0)
        def _():
            db_ref[...] = jnp.zeros_like(db_ref)

        dm = dm_ref[...]
        gp = _sigmoid(gp_ref[...] + bp_ref[...])
        gf = _sigmoid(gf_ref[...] + bf_ref[...])
        dyp_ref[...] = (dm * gp).astype(BF16)
        dyf_ref[...] = (dm * gf).astype(BF16)
        dlp = dm * yp_ref[...] * gp * (1.0 - gp)
        dlf = dm * yf_ref[...] * gf * (1.0 - gf)
        dgl_ref[:, :D_MODEL] = dlp.astype(BF16)
        dgl_ref[:, D_MODEL:] = dlf.astype(BF16)
        db_ref[:, :D_MODEL] += jnp.sum(dlp, axis=0, keepdims=True)
        db_ref[:, D_MODEL:] += jnp.sum(dlf, axis=0, keepdims=True)

    col = lambda j: pl.BlockSpec((bt, D_MODEL), lambda i: (i, j))
    vec = lambda j: pl.BlockSpec((1, D_MODEL), lambda i: (0, j))
    wide = pl.BlockSpec((bt, GATE_WIDTH), lambda i: (i, 0))
    return pl.pallas_call(
        body, name="mix_bwd", grid=(t // bt,),
        out_shape=(jax.ShapeDtypeStruct((t, D_MODEL), BF16), jax.ShapeDtypeStruct((t, D_MODEL), BF16),
                   jax.ShapeDtypeStruct((t, GU_COLS), BF16), jax.ShapeDtypeStruct((1, GATE_WIDTH), F32)),
        in_specs=[col(0), col(1), vec(0), vec(1), col(0), col(0), col(0)],
        out_specs=(col(0), col(0), wide, pl.BlockSpec((1, GATE_WIDTH), lambda i: (0, 0))),
        compiler_params=_cparams(("arbitrary",)),
    )(gu, gu, b_gate, b_gate, y_pool, y_fox, dmix)


X_SCALE = X_DH ** -0.5


def _xattn_probs(qh, kh):
    s = lax.dot_general(qh, kh, NT_DIMS, preferred_element_type=F32) * X_SCALE
    e = jnp.exp(s - jnp.max(s, axis=-1, keepdims=True))
    return e / jnp.sum(e, axis=-1, keepdims=True)


def _xattn_fwd(q, kv):
    b, s, _ = q.shape
    m = kv.shape[1]
    bq = _block(s, 512, 16)

    def body(q_ref, kv_ref, o_ref):
        for h in range(X_HEADS):
            cols = slice(h * X_DH, (h + 1) * X_DH)
            p = _xattn_probs(q_ref[0, :, cols], kv_ref[0, :, cols])
            vh = kv_ref[0, :, X_WIDTH + h * X_DH:X_WIDTH + (h + 1) * X_DH]
            o_ref[0, :, cols] = jnp.dot(p.astype(BF16), vh, preferred_element_type=F32).astype(BF16)

    return pl.pallas_call(
        body, name="xattn_fwd", out_shape=jax.ShapeDtypeStruct((b, s, X_WIDTH), BF16), grid=(b, s // bq),
        in_specs=[pl.BlockSpec((1, bq, X_WIDTH), lambda i, j: (i, j, 0)),
                  pl.BlockSpec((1, m, 2 * X_WIDTH), lambda i, j: (i, 0, 0))],
        out_specs=pl.BlockSpec((1, bq, X_WIDTH), lambda i, j: (i, j, 0)),
        compiler_params=_cparams(("parallel", "parallel")),
    )(q, kv)


def _xattn_bwd(q, kv, do):
    b, s, _ = q.shape
    m = kv.shape[1]
    bq = _block(s, 512, 16)

    def body(q_ref, kv_ref, do_ref, dq_ref, dkv_ref):
        @pl.when(pl.program_id(1) == 0)
        def _():
            dkv_ref[...] = jnp.zeros_like(dkv_ref)

        for h in range(X_HEADS):
            cols = slice(h * X_DH, (h + 1) * X_DH)
            vcols = slice(X_WIDTH + h * X_DH, X_WIDTH + (h + 1) * X_DH)
            qh, kh, vh, doh = q_ref[0, :, cols], kv_ref[0, :, cols], kv_ref[0, :, vcols], do_ref[0, :, cols]
            p = _xattn_probs(qh, kh)
            dkv_ref[0, :, vcols] += lax.dot_general(p.astype(BF16), doh, TN_DIMS, preferred_element_type=F32)
            dp = lax.dot_general(doh, vh, NT_DIMS, preferred_element_type=F32)
            ds = (p * (dp - jnp.sum(p * dp, axis=-1, keepdims=True)) * X_SCALE).astype(BF16)
            dq_ref[0, :, cols] = jnp.dot(ds, kh, preferred_element_type=F32).astype(BF16)
            dkv_ref[0, :, cols] += lax.dot_general(ds, qh, TN_DIMS, preferred_element_type=F32)

    tile = pl.BlockSpec((1, bq, X_WIDTH), lambda i, j: (i, j, 0))
    mem = pl.BlockSpec((1, m, 2 * X_WIDTH), lambda i, j: (i, 0, 0))
    return pl.pallas_call(
        body, name="xattn_bwd", grid=(b, s // bq),
        out_shape=(jax.ShapeDtypeStruct((b, s, X_WIDTH), BF16), jax.ShapeDtypeStruct((b, m, 2 * X_WIDTH), F32)),
        in_specs=[tile, mem, tile], out_specs=(tile, mem),
        compiler_params=_cparams(("parallel", "arbitrary")),
    )(q, kv, do)


def _swiglu_fwd(gu):
    t = gu.shape[0]
    bt = _block(t, 256, 16)

    def body(gt_ref, up_ref, o_ref):
        gt = gt_ref[...].astype(F32)
        o_ref[...] = (gt * _sigmoid(gt) * up_ref[...].astype(F32)).astype(BF16)

    col = lambda j: pl.BlockSpec((bt, D_FF), lambda i: (i, j))
    return pl.pallas_call(
        body, name="swiglu_fwd", out_shape=jax.ShapeDtypeStruct((t, D_FF), BF16), grid=(t // bt,),
        in_specs=[col(0), col(1)], out_specs=col(0),
        compiler_params=_cparams(("parallel",)),
    )(gu, gu)


def _swiglu_bwd(gu, dact):
    t = gu.shape[0]
    bt = _block(t, 256, 16)

    def body(gt_ref, up_ref, da_ref, o_ref):
        gt = gt_ref[...].astype(F32)
        da = da_ref[...].astype(F32)
        sg = _sigmoid(gt)
        silu = gt * sg
        o_ref[:, :D_FF] = (da * up_ref[...].astype(F32) * (sg + silu * (1.0 - sg))).astype(BF16)
        o_ref[:, D_FF:] = (da * silu).astype(BF16)

    col = lambda j: pl.BlockSpec((bt, D_FF), lambda i: (i, j))
    return pl.pallas_call(
        body, name="swiglu_bwd", out_shape=jax.ShapeDtypeStruct((t, 2 * D_FF), BF16), grid=(t // bt,),
        in_specs=[col(0), col(1), col(0)], out_specs=pl.BlockSpec((bt, 2 * D_FF), lambda i: (i, 0)),
        compiler_params=_cparams(("parallel",)),
    )(gu, gu, dact)


def _stack_of(w, axis):
    r, c = w.shape
    if axis == 0:
        return w.reshape(N_CHIPS, r // N_CHIPS, c)
    return w.reshape(r, N_CHIPS, c // N_CHIPS).transpose(1, 0, 2)


def _stack_t(w3):
    n, r, c = w3.shape
    return w3.transpose(0, 2, 1).reshape(n * c, r)


def _pair_rows(k, v):
    c = k.shape[1]
    return jnp.stack([k.reshape(PAIRS, LANES, c), v.reshape(PAIRS, LANES, c)], axis=1).reshape(2 * FOX_WIDTH, c)


def _unpair_rows(kv):
    c = kv.shape[1]
    kv = kv.reshape(PAIRS, 2, LANES, c)
    return kv[:, 0].reshape(FOX_WIDTH, c), kv[:, 1].reshape(FOX_WIDTH, c)


def _input_grad(parts, weights_t, ex):
    t = parts[0].shape[0]
    d = weights_t[0].shape[1]
    bm = _block(t, 512, 16)
    n = len(parts)

    def body(*refs):
        acc = None
        for a_ref, b_ref in zip(refs[:n], refs[n:2 * n]):
            term = jnp.dot(a_ref[...], b_ref[...], preferred_element_type=F32)
            acc = term if acc is None else acc + term
        refs[2 * n][...] = acc

    (out,), moved = _hosted_call(
        body, ex, name="d_h", grid=(t // bm,), out_shape=(jax.ShapeDtypeStruct((t, d), F32),),
        in_specs=[pl.BlockSpec((bm, p.shape[1]), lambda i: (i, 0)) for p in parts]
        + [pl.BlockSpec(w.shape, lambda i: (0, 0)) for w in weights_t],
        out_specs=(pl.BlockSpec((bm, d), lambda i: (i, 0)),), args=tuple(parts) + tuple(weights_t))
    return out, moved


def _step(x, mem, loss_target, weights, moments_m, moments_v):
    nb, s, d = x.shape
    n_mem = mem.shape[1]
    t = nb * s
    blk = min(ATT_BLOCK, s)
    x2 = x.reshape(t, d)
    mem2 = mem.reshape(nb * n_mem, d)
    tgt2 = loss_target.reshape(t, d)

    def shard2d(a, n):
        a = a.reshape(a.shape[1:])
        return a.T if n == "w_in" else a

    def unshard(a, n):
        return (a.T if n == "w_in" else a)[None]

    local = {n: shard2d(weights[n], n) for n, _, _ in SHARDED}

    names = [n for n, _, _ in SHARDED]
    later = [n for n in names if n != "w_in"]
    local_b = {n: local[n].astype(BF16) for n in names}
    w_in_stack, = _place_own(_run_exchange(_gather_exchange([local_b["w_in"]]), "gather_w_in"), [local_b["w_in"]])

    def w_in_rows(lo, hi):
        per = IN_COLS // N_CHIPS
        parts = [w_in_stack[j, max(lo, j * per) - j * per:min(hi, (j + 1) * per) - j * per]
                 for j in range(N_CHIPS) if max(lo, j * per) < min(hi, (j + 1) * per)]
        return parts[0] if len(parts) == 1 else jnp.concatenate(parts)

    w_gu_t = jnp.concatenate([w_in_rows(2056, IN_COLS), w_in_rows(0, 512)])
    w_qkv_t = jnp.concatenate([w_in_rows(512, 1024), _pair_rows(w_in_rows(1024, 1536), w_in_rows(1536, 2048))])
    w_f_t = jnp.pad(w_in_rows(2048, 2056), ((0, LANES - FOX_HEADS), (0, 0)))
    w_gu, w_qkv, w_f = w_gu_t.T, w_qkv_t.T, w_f_t.T
    w_f_exp = jnp.repeat(w_f[:, :FOX_HEADS], FOX_DH, axis=1)

    g_mix, g_x, g_mem, g_ffn = (weights[n] for n in ("norm_mix_g", "norm_x_g", "norm_mem_g", "norm_ffn_g"))
    g_final = weights["norm_final_g"].reshape(1, d)
    pool_w = weights["pool_w"].reshape(4, POOL_GC, POOL_GC)
    pool_scale, b_gate = weights["pool_scale"], weights["b_gate"]
    b_f_pad = jnp.pad(weights["b_forget"], ((0, 0), (0, LANES - FOX_HEADS)))
    b_f_exp = jnp.repeat(weights["b_forget"], FOX_DH, axis=1)

    h = _rms_fwd(x2, g_mix, "norm_mix")
    gu = _mm(h, w_gu, bn=512, name="in_proj_gates_pool")
    qkv = _mm(h, w_qkv, out_dtype=BF16, bn=512, name="in_proj_qkv")
    f_pad = _mm(h, w_f, name="in_proj_forget")
    gu3, qkv3 = gu.reshape(nb, s, GU_COLS), qkv.reshape(nb, s, 3 * FOX_WIDTH)
    y = _pool_fwd(gu3, pool_w, pool_scale)
    f_exp = _mm(h, w_f_exp, name="in_proj_forget_lanes").reshape(nb, s, FOX_WIDTH)
    c_exp = _forget_cumsum(f_exp, b_f_exp, "forget_cumsum_lanes")
    c_pad = _forget_cumsum(f_pad.reshape(nb, s, LANES), b_f_pad, "forget_cumsum")
    c_row = c_pad[:, :, :FOX_HEADS].transpose(0, 2, 1).reshape(nb, FOX_HEADS, s // blk, 1, blk)
    (o, o_b, lse), gathered = _fox_fwd(qkv3, c_exp, c_row, ex=_gather_exchange([local_b[n] for n in later]))
    stacks = dict(zip(later, _place_own(gathered, [local_b[n] for n in later])))
    w_pool_out3, w_fox_out3, w_xo3, w_ffn_in3 = (stacks[n] for n in ("w_pool_out", "w_fox_out", "w_xo", "w_ffn_in"))
    w_out, w_xq, w_xkv, w_ffn_out = (stacks[n].reshape(-1, stacks[n].shape[2])
                                     for n in ("w_out", "w_xq", "w_xkv", "w_ffn_out"))
    y2, o2 = y.reshape(t, POOL_WIDTH), o_b.reshape(t, FOX_WIDTH)
    y_pool = _mm(y2, w_pool_out3, b_stack=True, name="pool_out")
    y_fox = _mm(o2, w_fox_out3, b_stack=True, name="fox_out")
    mix = _mix_fwd(gu, b_gate, y_pool, y_fox)
    x1 = _mm(mix, w_out, res=x2, name="mix_out")
    hx = _rms_fwd(x1, g_x, "norm_x")
    mem_n = _rms_fwd(mem2, g_mem, "norm_mem")
    qx = _mm(hx, w_xq, out_dtype=BF16, name="x_q")
    kv = _mm(mem_n, w_xkv, out_dtype=BF16, name="x_kv")
    qx3, kv3 = qx.reshape(nb, s, X_WIDTH), kv.reshape(nb, n_mem, 2 * X_WIDTH)
    ox = _xattn_fwd(qx3, kv3).reshape(t, X_WIDTH)
    x2_ = _mm(ox, w_xo3, b_stack=True, res=x1, name="x_out")
    hf = _rms_fwd(x2_, g_ffn, "norm_ffn")
    ffn = _mm(hf, w_ffn_in3, b_stack=True, out_dtype=BF16, bn=1408, name="ffn_in")
    act = _swiglu_fwd(ffn)
    x3 = _mm(act, w_ffn_out, res=x2_, name="ffn_out")

    dx3, dx3_b, dg_final, loss_part = _final_loss(x3, tgt2, g_final)
    dw_ffn_out = _mm(act, dx3_b, ta=True, bm=1408, bn=512, bk=2048, name="d_w_ffn_out")
    dact = _mm(dx3_b, w_ffn_out.T, out_dtype=BF16, bn=1408, name="d_act")
    dffn = _swiglu_bwd(ffn, dact)
    dw_ffn_in = _mm(hf, dffn, ta=True, bm=512, bn=1408, bk=2048, out_stack=True, name="d_w_ffn_in")
    core = lax.axis_index("c").astype(jnp.int32).reshape(1)
    ffn_group = ["w_ffn_in", "w_ffn_out"]
    mid_group = ["w_pool_out", "w_fox_out", "w_out", "w_xq", "w_xkv", "w_xo"]
    grad_stacks = {"w_ffn_in": dw_ffn_in, "w_ffn_out": _stack_of(dw_ffn_out, 0)}

    def presum(group, theirs):
        return [_sum_halves(grad_stacks[n], t_, core, "sum_halves_" + n) for n, t_ in zip(group, theirs)]

    dhf, theirs = _mm(dffn, _stack_t(w_ffn_in3), bk=2816, name="d_hf",
                      ex=_swap_exchange([grad_stacks[n] for n in ffn_group]))
    chip_sums = dict(zip(ffn_group, presum(ffn_group, theirs)))
    dx2, dx2_b, dg_ffn = _rms_bwd(dhf, x2_, g_ffn, dx3, "norm_ffn_bwd")

    dw_xo = _mm(ox, dx2_b, ta=True, bn=256, out_stack=True, name="d_w_xo")
    dox = _mm(dx2_b, _stack_t(w_xo3), out_dtype=BF16, name="d_ox").reshape(nb, s, X_WIDTH)
    dqx, dkv = _xattn_bwd(qx3, kv3, dox)
    dqx2, dkv2 = dqx.reshape(t, X_WIDTH), dkv.reshape(nb * n_mem, 2 * X_WIDTH)
    dw_xkv = _mm(mem_n, dkv2, ta=True, name="d_w_xkv")
    dmem_n = _mm(dkv2, w_xkv.T, name="d_mem_n")
    dg_mem = _rms_bwd(dmem_n, mem2, g_mem, None, "norm_mem_bwd")
    dw_xq = _mm(hx, dqx2, ta=True, name="d_w_xq")
    dhx = _mm(dqx2, w_xq.T, name="d_hx")
    dx1, dx1_b, dg_x = _rms_bwd(dhx, x1, g_x, dx2, "norm_x_bwd")

    dw_out = _mm(mix, dx1_b, ta=True, name="d_w_out")
    dmix = _mm(dx1_b, w_out.T, name="d_mix")
    dyp, dyf, dgu, db_gate = _mix_bwd(gu, b_gate, y_pool, y_fox, dmix)
    dw_pool_out = _mm(y2, dyp, ta=True, bn=256, out_stack=True, name="d_w_pool_out")
    dw_fox_out = _mm(o2, dyf, ta=True, bn=256, out_stack=True, name="d_w_fox_out")
    dy = _mm(dyp, _stack_t(w_pool_out3), name="d_y").reshape(nb, s, POOL_WIDTH)
    do = _mm(dyf, _stack_t(w_fox_out3), name="d_o").reshape(nb, s, FOX_WIDTH)
    dgu3, dpool_w, dpool_scale = _pool_bwd(gu3, dy, pool_w, pool_scale, dgu.reshape(nb, s, GU_COLS))
    grad_stacks.update({"w_pool_out": dw_pool_out, "w_fox_out": dw_fox_out, "w_out": _stack_of(dw_out, 0),
                        "w_xq": _stack_of(dw_xq, 0), "w_xkv": _stack_of(dw_xkv, 0), "w_xo": dw_xo})
    dgu2 = dgu3.reshape(t, GU_COLS)
    dw_gu_t, theirs = _mm(dgu2, h, ta=True, name="d_w_gates_pool",
                          ex=_swap_exchange([grad_stacks[n] for n in mid_group]))
    chip_sums.update(zip(mid_group, presum(mid_group, theirs)))
    early = ffn_group + mid_group
    (dq3, dkv3, dc_q, dc_row), early_slots = _fox_bwd(qkv3, c_exp, c_row, lse, o, do,
                                                      ex=_chips_exchange([chip_sums[n] for n in early]))
    slots = dict(zip(early, early_slots))
    dc = dc_row.reshape(nb, FOX_HEADS, s).transpose(0, 2, 1) + dc_q.transpose(0, 2, 1, 3).reshape(nb, s, FOX_HEADS)
    dc = jnp.pad(dc, ((0, 0), (0, 0), (0, LANES - FOX_HEADS)))
    df, db_f = _forget_bwd(dc, f_pad.reshape(nb, s, LANES), b_f_pad)
    dq2, dkv2, df2 = dq3.reshape(t, FOX_WIDTH), dkv3.reshape(t, 2 * FOX_WIDTH), df.reshape(t, LANES)
    dw_q_t = _mm(dq2, h, ta=True, name="d_w_q")
    dw_kv_t = _mm(dkv2, h, ta=True, name="d_w_kv")
    dw_f_t = _mm(df2, h, ta=True, name="d_w_forget")
    dw_k_t, dw_v_t = _unpair_rows(dw_kv_t)
    dw_in_t = jnp.concatenate([dw_gu_t[GATE_WIDTH:], dw_q_t, dw_k_t, dw_v_t, dw_f_t[:FOX_HEADS],
                               dw_gu_t[:GATE_WIDTH]])
    grad_stacks["w_in"] = dw_in_t.reshape(N_CHIPS, IN_COLS // N_CHIPS, D_MODEL)
    chip_sums["w_in"], = presum(["w_in"], _run_exchange(_swap_exchange([grad_stacks["w_in"]]), "swap_halves_w_in"))
    dh, (slots["w_in"],) = _input_grad([dgu2, dq2, dkv2, df2],
                                       [w_gu_t, w_qkv_t[:FOX_WIDTH], w_qkv_t[FOX_WIDTH:], w_f_t],
                                       _chips_exchange([chip_sums["w_in"]]))
    dx, _, dg_mix = _rms_bwd(dh, x2, g_mix, dx1, "norm_mix_bwd")

    place = jnp.stack([lax.axis_index("c"), 2 * lax.axis_index("x") + lax.axis_index("y")]).astype(jnp.int32)
    reduced = _join_halves([_sum_chips(slots[n], chip_sums[n], place, _by_rows(local[n].shape[0]), "sum_chips_" + n)
                            for n in names])

    small_grads = {"norm_mix_g": dg_mix, "b_forget": db_f[:, :FOX_HEADS], "b_gate": db_gate, "pool_w": dpool_w,
                   "pool_scale": dpool_scale, "norm_x_g": dg_x, "norm_mem_g": dg_mem, "norm_ffn_g": dg_ffn,
                   "norm_final_g": dg_final}
    parts = _gather_small(_pack_small([small_grads[n] for n, _ in SMALL], last=loss_part[0, 0]))
    sg, sd, sm, sv = _adamw_small(parts.reshape(N_DEV, SMALL_ROWS, LANES),
                                  _pack_small([weights[n] for n, _ in SMALL]),
                                  _pack_small([moments_m[n] for n, _ in SMALL]),
                                  _pack_small([moments_v[n] for n, _ in SMALL]))
    loss = sg.reshape(-1)[LOSS_POS]

    grads, deltas, new_m, new_v = {}, {}, {}, {}
    for (n, _), g_, d_, m_, v_ in zip(SMALL, _unpack_small(sg), _unpack_small(sd), _unpack_small(sm), _unpack_small(sv)):
        grads[n], deltas[n], new_m[n], new_v[n] = g_, d_, m_, v_
    def tiles_of(a):
        return a.transpose(2, 0, 1)

    def block_of(a3):
        return a3.transpose(1, 2, 0)

    for n, g_ in zip(names, reduced):
        if n == "w_in":
            g_ = g_.reshape(IN_COLS // N_CHIPS, 1, D_MODEL)
            w_, m_, v_ = tiles_of(weights[n]), tiles_of(moments_m[n]), tiles_of(moments_v[n])
            back = block_of
        else:
            w_, m_, v_ = local[n], shard2d(moments_m[n], n), shard2d(moments_v[n], n)
            back = functools.partial(unshard, n=n)
        d_, m_, v_ = _adamw(w_, g_, m_, v_, "adamw_" + n)
        grads[n], deltas[n], new_m[n], new_v[n] = (back(a) for a in (g_, d_, m_, v_))
    return loss, dx.reshape(nb, s, d), grads, deltas, new_m, new_v


def kernel(x, mem, norm_mix_g, w_in, b_forget, b_gate, pool_w, pool_scale, w_pool_out, w_fox_out, w_out, norm_x_g, norm_mem_g, w_xq, w_xkv, w_xo, norm_ffn_g, w_ffn_in, w_ffn_out, norm_final_g, loss_target, m_norm_mix_g, m_w_in, m_b_forget, m_b_gate, m_pool_w, m_pool_scale, m_w_pool_out, m_w_fox_out, m_w_out, m_norm_x_g, m_norm_mem_g, m_w_xq, m_w_xkv, m_w_xo, m_norm_ffn_g, m_w_ffn_in, m_w_ffn_out, m_norm_final_g, v_norm_mix_g, v_w_in, v_b_forget, v_b_gate, v_pool_w, v_pool_scale, v_w_pool_out, v_w_fox_out, v_w_out, v_norm_x_g, v_norm_mem_g, v_w_xq, v_w_xkv, v_w_xo, v_norm_ffn_g, v_w_ffn_in, v_w_ffn_out, v_norm_final_g):
    given = dict(locals())
    weights = {n: given[n] for n in WEIGHT_ORDER}
    moments_m = {n: given["m_" + n] for n in WEIGHT_ORDER}
    moments_v = {n: given["v_" + n] for n in WEIGHT_ORDER}
    loss, grad_x, grads, deltas, new_m, new_v = _step(x, mem, loss_target, weights, moments_m, moments_v)
    return (loss, grad_x, *[grads[n] for n in WEIGHT_ORDER], *[deltas[n] for n in WEIGHT_ORDER],
            *[new_m[n] for n in WEIGHT_ORDER], *[new_v[n] for n in WEIGHT_ORDER])
```

```python
import functools
import math

import jax
import jax.numpy as jnp
from jax import lax
from jax.experimental import pallas as pl
from jax.experimental.pallas import tpu as pltpu

F32 = jnp.float32
BF16 = jnp.bfloat16
MESH = pl.DeviceIdType.MESH

D_MODEL = 1024
EPS = 1e-6
POOL_WINDOWS = (2, 4, 8, 16)
POOL_WIDTH = 512
POOL_GC = 128
FOX_HEADS = 8
FOX_DH = 64
FOX_WIDTH = 512
X_HEADS = 4
X_DH = 128
X_WIDTH = 512
D_FF = 2816
IN_COLS = 4104
GATE_WIDTH = 2048
ADAM_LR = 0.001
ADAM_B1 = 0.9
ADAM_B2 = 0.999
ADAM_EPS = 1e-08
ADAM_WD = 0.01
ADAM_STEP = 10

N_CHIPS = 4
N_DEV = 8
LANES = 128
VMEM_LIMIT_BYTES = 56 * 1024 * 1024
NEG_INF = -1e30
ATT_BLOCK = 512

SHARDED = (
    ("w_in", (1024, IN_COLS), 1),
    ("w_pool_out", (POOL_WIDTH, 1024), 1),
    ("w_fox_out", (FOX_WIDTH, 1024), 1),
    ("w_out", (1024, 1024), 0),
    ("w_xq", (1024, X_WIDTH), 0),
    ("w_xkv", (1024, 2 * X_WIDTH), 0),
    ("w_xo", (X_WIDTH, 1024), 1),
    ("w_ffn_in", (1024, 2 * D_FF), 1),
    ("w_ffn_out", (D_FF, 1024), 0),
)
SMALL = (
    ("norm_mix_g", (1, 1024)),
    ("b_forget", (1, 8)),
    ("b_gate", (1, 2048)),
    ("pool_w", (1, 4, 128, 128)),
    ("pool_scale", (1, 512)),
    ("norm_x_g", (1, 1024)),
    ("norm_mem_g", (1, 1024)),
    ("norm_ffn_g", (1, 1024)),
    ("norm_final_g", (1024,)),
)
WEIGHT_ORDER = ("norm_mix_g", "w_in", "b_forget", "b_gate", "pool_w", "pool_scale", "w_pool_out", "w_fox_out", "w_out",
                "norm_x_g", "norm_mem_g", "w_xq", "w_xkv", "w_xo", "norm_ffn_g", "w_ffn_in", "w_ffn_out", "norm_final_g")


def _cparams(sem=None):
    return pltpu.CompilerParams(dimension_semantics=sem, vmem_limit_bytes=VMEM_LIMIT_BYTES)


def _block(dim, pref, unit):
    if dim <= pref:
        return dim
    best = None
    for b in range(unit, pref + 1, unit):
        if dim % b == 0:
            best = b
    assert best is not None, (dim, pref, unit)
    return best


def _rows_block(rows, cols, unit=16, elems=1 << 19):
    return _block(rows, max(unit, elems // cols // unit * unit), unit)


def _my_place():
    return lax.axis_index("x"), lax.axis_index("y"), lax.axis_index("c")


def _other_chips(x, y):
    return [(1 - x, y), (x, 1 - y), (1 - x, 1 - y)]


def _chip(place):
    return 2 * place[0] + place[1]


ANY = pl.BlockSpec(memory_space=pl.ANY)


def _by_rows(rows):
    return rows % 32 == 0


def _half_shape(rows, cols):
    return (rows // 2, cols) if _by_rows(rows) else (rows, cols // 2)


def _core_half(ref, core, lead=()):
    rows, cols = ref.shape[-2:]
    if _by_rows(rows):
        return ref.at[(*lead, pl.ds(core * (rows // 2), rows // 2), slice(None))]
    return ref.at[(*lead, slice(None), pl.ds(core * (cols // 2), cols // 2))]


class _Exchange:
    def __init__(self, arrays, out_shapes, n_sems, start, finish):
        self.arrays, self.out_shapes, self.n_sems, self.start, self.finish = arrays, out_shapes, n_sems, start, finish

    def scratch(self):
        return [pltpu.SemaphoreType.DMA((self.n_sems,)), pltpu.SemaphoreType.DMA((self.n_sems,))]


def _run_exchange(ex, name):
    n = len(ex.arrays)

    def body(*refs):
        ins, outs, sems = refs[:n], refs[n:2 * n], refs[2 * n:]
        ex.start(ins, outs, *sems)
        ex.finish(ins, outs, *sems)

    return pl.pallas_call(
        body, name=name, out_shape=ex.out_shapes, in_specs=[ANY] * n, out_specs=[ANY] * n, scratch_shapes=ex.scratch(),
    )(*ex.arrays)


def _hosted_call(body, ex, *, name, grid, in_specs, out_specs, out_shape, args, scratch=()):
    n_in, n_out, n_scr = len(args), len(out_shape), len(scratch)
    if ex is None:
        outs = pl.pallas_call(
            body, name=name, grid=grid, out_shape=out_shape, in_specs=in_specs, out_specs=out_specs,
            scratch_shapes=list(scratch), compiler_params=_cparams(("arbitrary",) * len(grid)))(*args)
        return outs, None
    nc = len(ex.arrays)

    def full_body(*refs):
        ins, cins = refs[:n_in], refs[n_in:n_in + nc]
        outs, couts = refs[n_in + nc:n_in + nc + n_out], refs[n_in + nc + n_out:n_in + 2 * nc + n_out]
        rest = refs[n_in + 2 * nc + n_out:]
        scr, sems = rest[:n_scr], rest[n_scr:]
        first = functools.reduce(jnp.logical_and, [pl.program_id(a) == 0 for a in range(len(grid))])
        last = functools.reduce(jnp.logical_and, [pl.program_id(a) == grid[a] - 1 for a in range(len(grid))])

        @pl.when(first)
        def _():
            ex.start(cins, couts, *sems)

        body(*ins, *outs, *scr)

        @pl.when(last)
        def _():
            ex.finish(cins, couts, *sems)

    outs = pl.pallas_call(
        full_body, name=name, grid=grid, out_shape=list(out_shape) + list(ex.out_shapes),
        in_specs=list(in_specs) + [ANY] * nc, out_specs=list(out_specs) + [ANY] * nc,
        scratch_shapes=list(scratch) + ex.scratch(),
        compiler_params=_cparams(("arbitrary",) * len(grid)))(*args, *ex.arrays)
    return outs[:n_out], outs[n_out:]


def _gather_exchange(shards):
    n = len(shards)

    def copies(ins, outs, send_sems, recv_sems):
        x, y, c = _my_place()

        def half(k, chip, core):
            return _core_half(outs[k], core, lead=(_chip(chip),))

        def copy(k, slot, chip, core, to, src=None):
            return pltpu.make_async_remote_copy(
                src_ref=half(k, chip, core) if src is None else src, dst_ref=half(k, chip, core),
                send_sem=send_sems.at[6 * k + slot], recv_sem=recv_sems.at[6 * k + slot],
                device_id=to, device_id_type=MESH)

        return (x, y, c), copy

    def first_copies(ins, outs, send_sems, recv_sems):
        (x, y, c), copy = copies(ins, outs, send_sems, recv_sems)
        out = []
        for j, chip in enumerate(_other_chips(x, y)):
            for k in range(n):
                out.append(copy(k, j, (x, y), c, (*chip, c), src=_core_half(ins[k], c)))
        return out

    def start(ins, outs, send_sems, recv_sems):
        for cp in first_copies(ins, outs, send_sems, recv_sems):
            cp.start()

    def finish(ins, outs, send_sems, recv_sems):
        (x, y, c), copy = copies(ins, outs, send_sems, recv_sems)
        chips = _other_chips(x, y)
        passed = []
        for j, chip in enumerate(chips):
            for k in range(n):
                copy(k, j, chip, c, (x, y, c)).wait_recv()
                passed.append(copy(k, 3 + j, chip, c, (x, y, 1 - c)))
                passed[-1].start()
        for j, chip in enumerate(chips):
            for k in range(n):
                copy(k, 3 + j, chip, 1 - c, (x, y, c)).wait_recv()
        for cp in first_copies(ins, outs, send_sems, recv_sems) + passed:
            cp.wait_send()

    return _Exchange(list(shards), [jax.ShapeDtypeStruct((N_CHIPS,) + s.shape, s.dtype) for s in shards], 6 * n,
                     start, finish)


def _place_own(stacks, shards):
    me = 2 * lax.axis_index("x") + lax.axis_index("y")
    return [lax.dynamic_update_slice(others, mine[None], (me, 0, 0)) for others, mine in zip(stacks, shards)]


def _swap_exchange(grads):
    n = len(grads)

    def copies(ins, outs, send_sems, recv_sems):
        x, y, c = _my_place()
        return [pltpu.make_async_remote_copy(
            src_ref=_core_half(ins[k], 1 - c, lead=(slice(None),)), dst_ref=outs[k],
            send_sem=send_sems.at[k], recv_sem=recv_sems.at[k], device_id=(x, y, 1 - c), device_id_type=MESH)
            for k in range(n)]

    def start(ins, outs, send_sems, recv_sems):
        for cp in copies(ins, outs, send_sems, recv_sems):
            cp.start()

    def finish(ins, outs, send_sems, recv_sems):
        for cp in copies(ins, outs, send_sems, recv_sems):
            cp.wait()

    return _Exchange(list(grads), [jax.ShapeDtypeStruct((N_CHIPS,) + _half_shape(*g.shape[1:]), g.dtype) for g in grads],
                     n, start, finish)


def _chips_exchange(sums):
    n = len(sums)

    def sends(ins, outs, send_sems, recv_sems):
        x, y, c = _my_place()
        return [pltpu.make_async_remote_copy(
            src_ref=ins[k].at[_chip(chip)], dst_ref=outs[k].at[_chip((x, y))],
            send_sem=send_sems.at[3 * k + j], recv_sem=recv_sems.at[3 * k + j],
            device_id=(*chip, c), device_id_type=MESH)
            for j, chip in enumerate(_other_chips(x, y)) for k in range(n)]

    def start(ins, outs, send_sems, recv_sems):
        for cp in sends(ins, outs, send_sems, recv_sems):
            cp.start()

    def finish(ins, outs, send_sems, recv_sems):
        x, y, c = _my_place()
        for j, chip in enumerate(_other_chips(x, y)):
            for k in range(n):
                slot = outs[k].at[_chip(chip)]
                pltpu.make_async_remote_copy(
                    src_ref=slot, dst_ref=slot, send_sem=send_sems.at[3 * k + j], recv_sem=recv_sems.at[3 * k + j],
                    device_id=(x, y, c), device_id_type=MESH).wait_recv()
        for cp in sends(ins, outs, send_sems, recv_sems):
            cp.wait_send()

    return _Exchange(list(sums), [jax.ShapeDtypeStruct(s.shape, s.dtype) for s in sums], 3 * n, start, finish)


def _join_halves(shards):
    n = len(shards)

    def body(*refs):
        ins, outs = refs[:n], refs[n:2 * n]
        send_sems, recv_sems = refs[2 * n:]
        x, y, c = _my_place()
        sends = []
        for k in range(n):
            sends.append(pltpu.make_async_remote_copy(
                src_ref=_core_half(ins[k], c), dst_ref=_core_half(outs[k], c),
                send_sem=send_sems.at[k], recv_sem=recv_sems.at[k], device_id=(x, y, 1 - c), device_id_type=MESH))
            sends[-1].start()
        for k in range(n):
            theirs = _core_half(outs[k], 1 - c)
            pltpu.make_async_remote_copy(
                src_ref=theirs, dst_ref=theirs, send_sem=send_sems.at[k], recv_sem=recv_sems.at[k],
                device_id=(x, y, c), device_id_type=MESH).wait_recv()
        for cp in sends:
            cp.wait_send()

    return pl.pallas_call(
        body, name="join_halves",
        out_shape=[jax.ShapeDtypeStruct(s.shape, s.dtype) for s in shards],
        in_specs=[ANY] * n, out_specs=[ANY] * n,
        input_output_aliases={k: k for k in range(n)},
        scratch_shapes=[pltpu.SemaphoreType.DMA((n,)), pltpu.SemaphoreType.DMA((n,))],
    )(*shards)


def _small_exchange(blocks):
    n = len(blocks)

    def copies(ins, outs, send_sems, recv_sems):
        x, y, c = _my_place()

        def copy(k, j, whose, to, src=None):
            slot = outs[k].at[4 * whose[0] + 2 * whose[1] + whose[2]]
            return pltpu.make_async_remote_copy(
                src_ref=slot if src is None else src, dst_ref=slot,
                send_sem=send_sems.at[7 * k + j], recv_sem=recv_sems.at[7 * k + j], device_id=to, device_id_type=MESH)

        return (x, y, c), copy

    def first_copies(ins, outs, send_sems, recv_sems):
        (x, y, c), copy = copies(ins, outs, send_sems, recv_sems)
        out = []
        for k in range(n):
            out.append(copy(k, 0, (x, y, c), (x, y, 1 - c), src=ins[k]))
            out += [copy(k, 1 + j, (x, y, c), (*chip, c), src=ins[k]) for j, chip in enumerate(_other_chips(x, y))]
        return out

    def start(ins, outs, send_sems, recv_sems):
        for cp in first_copies(ins, outs, send_sems, recv_sems):
            cp.start()

    def finish(ins, outs, send_sems, recv_sems):
        (x, y, c), copy = copies(ins, outs, send_sems, recv_sems)
        chips = _other_chips(x, y)
        passed = []
        for j, chip in enumerate(chips):
            for k in range(n):
                copy(k, 1 + j, (*chip, c), (x, y, c)).wait_recv()
                passed.append(copy(k, 4 + j, (*chip, c), (x, y, 1 - c)))
                passed[-1].start()
        for k in range(n):
            copy(k, 0, (x, y, 1 - c), (x, y, c)).wait_recv()
        for j, chip in enumerate(chips):
            for k in range(n):
                copy(k, 4 + j, (*chip, 1 - c), (x, y, c)).wait_recv()
        for cp in first_copies(ins, outs, send_sems, recv_sems) + passed:
            cp.wait_send()

    return _Exchange(list(blocks), [jax.ShapeDtypeStruct((N_DEV,) + blk.shape, blk.dtype) for blk in blocks], 7 * n,
                     start, finish)


def _sum_halves(grads, theirs, core, name):
    _, h, cols = theirs.shape
    by_rows = _by_rows(grads.shape[1])
    br = _rows_block(h, cols) if by_rows else h
    nb = h // br

    def body(core_ref, a_ref, b_ref, o_ref):
        o_ref[...] = (a_ref[...] + b_ref[...]).astype(BF16)

    if by_rows:
        mine = pl.BlockSpec((1, br, cols), lambda j, i, core_ref: (j, core_ref[0] * nb + i, 0))
    else:
        mine = pl.BlockSpec((1, br, cols), lambda j, i, core_ref: (j, i, core_ref[0]))
    return pl.pallas_call(
        body, name=name,
        out_shape=jax.ShapeDtypeStruct(theirs.shape, BF16),
        grid_spec=pltpu.PrefetchScalarGridSpec(
            num_scalar_prefetch=1, grid=(N_CHIPS, nb),
            in_specs=[mine, pl.BlockSpec((1, br, cols), lambda j, i, core_ref: (j, i, 0))],
            out_specs=pl.BlockSpec((1, br, cols), lambda j, i, core_ref: (j, i, 0))),
        compiler_params=_cparams(("parallel", "parallel")),
    )(core, grads, theirs)


def _sum_chips(slots, sums, place, by_rows, name):
    _, h, cols = slots.shape
    br = _rows_block(h, cols) if by_rows else h
    nb = h // br

    def body(place_ref, s_ref, own_ref, o_ref):
        me = place_ref[1]
        acc = None
        for k in range(N_CHIPS):
            term = jnp.where(me == k, own_ref[k], s_ref[k]).astype(F32)
            acc = term if acc is None else acc + term
        o_ref[...] = acc

    stack = pl.BlockSpec((N_CHIPS, br, cols), lambda i, place_ref: (0, i, 0))
    if by_rows:
        out_shape, out_map = (2 * h, cols), lambda i, place_ref: (place_ref[0] * nb + i, 0)
    else:
        out_shape, out_map = (h, 2 * cols), lambda i, place_ref: (i, place_ref[0])
    return pl.pallas_call(
        body, name=name,
        out_shape=jax.ShapeDtypeStruct(out_shape, F32),
        grid_spec=pltpu.PrefetchScalarGridSpec(
            num_scalar_prefetch=1, grid=(nb,), in_specs=[stack, stack],
            out_specs=pl.BlockSpec((br, cols), out_map)),
        compiler_params=_cparams(("parallel",)),
    )(place, slots, sums)


def _adamw_math(w, g, m, v):
    m = ADAM_B1 * m + (1.0 - ADAM_B1) * g
    v = ADAM_B2 * v + (1.0 - ADAM_B2) * (g * g)
    m_hat = m / (1.0 - ADAM_B1 ** ADAM_STEP)
    v_hat = v / (1.0 - ADAM_B2 ** ADAM_STEP)
    delta = -ADAM_LR * (m_hat / (jnp.sqrt(v_hat) + ADAM_EPS) + ADAM_WD * w)
    return delta, m, v


def _adamw(w, g, m, v, name):
    def body(w_ref, g_ref, m_ref, v_ref, d_ref, nm_ref, nv_ref):
        d, nm, nv = _adamw_math(w_ref[...], g_ref[...], m_ref[...], v_ref[...])
        d_ref[...] = d
        nm_ref[...] = nm
        nv_ref[...] = nv

    if w.ndim == 3:
        rows = w.shape[0]
        br = max(b for b in range(1, 65) if rows % b == 0)
        spec, steps = pl.BlockSpec((br,) + w.shape[1:], lambda i: (i, 0, 0)), rows // br
    else:
        rows, cols = w.shape
        br = _rows_block(rows, cols, unit=8)
        spec, steps = pl.BlockSpec((br, cols), lambda i: (i, 0)), rows // br
    shape = jax.ShapeDtypeStruct(w.shape, F32)
    return pl.pallas_call(
        body, name=name, out_shape=(shape, shape, shape), grid=(steps,),
        in_specs=[spec] * 4, out_specs=(spec, spec, spec),
        compiler_params=_cparams(("parallel",)),
    )(w, g, m, v)


def _adamw_small(parts, own, ws, ms, vs, loss_parts, loss_own, device):
    n = len(ws)

    def total(device_ref, parts_ref, own_ref):
        acc = None
        for dev in range(N_DEV):
            term = jnp.where(device_ref[0] == dev, own_ref[...], parts_ref[dev])
            acc = term if acc is None else acc + term
        return acc

    def body(device_ref, *refs):
        ins, outs = refs[:5 * n + 2], refs[5 * n + 2:]
        for k in range(n):
            g = total(device_ref, ins[k], ins[n + k])
            d, nm, nv = _adamw_math(ins[2 * n + k][...], g, ins[3 * n + k][...], ins[4 * n + k][...])
            for o_ref, val in zip(outs[k::n][:4], (g, d, nm, nv)):
                o_ref[...] = val
        outs[4 * n][...] = total(device_ref, ins[5 * n], ins[5 * n + 1])

    args = list(parts) + list(own) + list(ws) + list(ms) + list(vs) + [loss_parts, loss_own]
    whole = lambda a: pl.BlockSpec(a.shape, lambda i, device_ref, nd=a.ndim: (0,) * nd)
    shapes = [jax.ShapeDtypeStruct(w.shape, F32) for w in ws] * 4 + [jax.ShapeDtypeStruct(loss_own.shape, F32)]
    outs = pl.pallas_call(
        body, name="adamw_small", out_shape=shapes,
        grid_spec=pltpu.PrefetchScalarGridSpec(
            num_scalar_prefetch=1, grid=(1,), in_specs=[whole(a) for a in args], out_specs=[whole(a) for a in shapes]),
        compiler_params=_cparams(("arbitrary",)),
    )(device, *args)
    return outs[:n], outs[n:2 * n], outs[2 * n:3 * n], outs[3 * n:4 * n], outs[4 * n]


def _mm(a, b, *, name, ta=False, out_dtype=F32, res=None, bm=1024, bn=1024, bk=4096, b_stack=False, out_stack=False,
        ex=None):
    if ta:
        kdim, m = a.shape
    else:
        m, kdim = a.shape
    if b_stack:
        _, kb, chunk = b.shape
        n = N_CHIPS * chunk
    else:
        kb, n = b.shape
        chunk = n // N_CHIPS if out_stack else n
    assert kdim == kb, (a.shape, b.shape, ta)
    bm = _block(m, bm, LANES if ta else 16)
    bn = _block(chunk, bn, LANES)
    bk = _block(kdim, bk, LANES)
    nk = kdim // bk
    per_chunk = chunk // bn
    dims = (((0 if ta else 1,), (0,)), ((), ()))

    def body(*refs):
        refs = list(refs)
        a_ref, b_ref = refs[:2]
        r_ref = refs[2] if res is not None else None
        o_ref = refs[3] if res is not None else refs[2]
        part = lax.dot_general(a_ref[...].astype(BF16), b_ref[...].astype(BF16), dims, preferred_element_type=F32)

        def finish(r):
            if r_ref is not None:
                r = r + r_ref[...]
            o_ref[...] = r.astype(out_dtype)

        if nk == 1:
            finish(part)
        else:
            acc_ref = refs[-1]
            k = pl.program_id(2)

            @pl.when(k == 0)
            def _():
                acc_ref[...] = part

            @pl.when(k > 0)
            def _():
                acc_ref[...] += part

            @pl.when(k == nk - 1)
            def _():
                finish(acc_ref[...])

    a_spec = pl.BlockSpec((bk, bm), lambda i, j, k: (k, i)) if ta else pl.BlockSpec((bm, bk), lambda i, j, k: (i, k))
    if b_stack:
        b_spec = pl.BlockSpec((None, bk, bn), lambda i, j, k: (j // per_chunk, k, j % per_chunk))
    else:
        b_spec = pl.BlockSpec((bk, bn), lambda i, j, k: (k, j))
    r_spec = pl.BlockSpec((bm, bn), lambda i, j, k: (i, j))
    if out_stack:
        o_spec = pl.BlockSpec((None, bm, bn), lambda i, j, k: (j // per_chunk, i, j % per_chunk))
        o_shape = (N_CHIPS, m, chunk)
    else:
        o_spec, o_shape = r_spec, (m, n)
    in_specs = [a_spec, b_spec] + ([r_spec] if res is not None else [])
    args = (a, b) + ((res,) if res is not None else ())
    (out,), moved = _hosted_call(
        body, ex, name=name, out_shape=(jax.ShapeDtypeStruct(o_shape, out_dtype),),
        grid=(m // bm, n // bn, nk), in_specs=in_specs, out_specs=(o_spec,),
        scratch=[pltpu.VMEM((bm, bn), F32)] if nk > 1 else [], args=args)
    return out if ex is None else (out, moved)


def _rms_fwd(x, g, name):
    t, d = x.shape
    bt = _block(t, 512, 16)

    def body(x_ref, g_ref, h_ref):
        xv = x_ref[...]
        r = lax.rsqrt(jnp.mean(xv * xv, axis=-1, keepdims=True) + EPS)
        h_ref[...] = (xv * r * g_ref[...]).astype(BF16)

    return pl.pallas_call(
        body, name=name, out_shape=jax.ShapeDtypeStruct((t, d), BF16), grid=(t // bt,),
        in_specs=[pl.BlockSpec((bt, d), lambda i: (i, 0)), pl.BlockSpec((1, d), lambda i: (0, 0))],
        out_specs=pl.BlockSpec((bt, d), lambda i: (i, 0)),
        compiler_params=_cparams(("parallel",)),
    )(x, g)


def _rms_bwd(dh, x, g, dres, name):
    t, d = x.shape
    bt = _block(t, 256, 16)
    want_dx = dres is not None

    def body(*refs):
        if want_dx:
            dh_ref, x_ref, g_ref, dres_ref, dx_ref, dxb_ref, dg_ref = refs
        else:
            dh_ref, x_ref, g_ref, dg_ref = refs
        xv = x_ref[...]
        r = lax.rsqrt(jnp.mean(xv * xv, axis=-1, keepdims=True) + EPS)
        xhat = xv * r
        dhv = dh_ref[...]

        @pl.when(pl.program_id(0) == 0)
        def _():
            dg_ref[...] = jnp.zeros_like(dg_ref)

        dg_ref[...] += jnp.sum(dhv * xhat, axis=0, keepdims=True)
        if want_dx:
            dxhat = dhv * g_ref[...]
            dx = dres_ref[...] + r * (dxhat - xhat * jnp.mean(dxhat * xhat, axis=-1, keepdims=True))
            dx_ref[...] = dx
            dxb_ref[...] = dx.astype(BF16)

    row = pl.BlockSpec((bt, d), lambda i: (i, 0))
    vec = pl.BlockSpec((1, d), lambda i: (0, 0))
    if want_dx:
        return pl.pallas_call(
            body, name=name, grid=(t // bt,),
            out_shape=(jax.ShapeDtypeStruct((t, d), F32), jax.ShapeDtypeStruct((t, d), BF16),
                       jax.ShapeDtypeStruct((1, d), F32)),
            in_specs=[row, row, vec, row], out_specs=(row, row, vec),
            compiler_params=_cparams(("arbitrary",)),
        )(dh, x, g, dres)
    return pl.pallas_call(
        body, name=name, grid=(t // bt,), out_shape=jax.ShapeDtypeStruct((1, d), F32),
        in_specs=[row, row, vec], out_specs=vec,
        compiler_params=_cparams(("arbitrary",)),
    )(dh, x, g)


def _final_loss(x, target, g):
    t, d = x.shape
    bt = _block(t, 256, 16)

    def body(x_ref, t_ref, g_ref, dx_ref, dxb_ref, dg_ref, loss_ref):
        xv = x_ref[...]
        gv = g_ref[...]
        r = lax.rsqrt(jnp.mean(xv * xv, axis=-1, keepdims=True) + EPS)
        xhat = xv * r
        err = xhat * gv - t_ref[...]

        @pl.when(pl.program_id(0) == 0)
        def _():
            dg_ref[...] = jnp.zeros_like(dg_ref)
            loss_ref[...] = jnp.zeros_like(loss_ref)

        loss_ref[...] += 0.5 * jnp.sum(jnp.mean(err * err, axis=-1, keepdims=True), axis=0, keepdims=True)
        dy = err * (1.0 / d)
        dg_ref[...] += jnp.sum(dy * xhat, axis=0, keepdims=True)
        dxhat = dy * gv
        dx = r * (dxhat - xhat * jnp.mean(dxhat * xhat, axis=-1, keepdims=True))
        dx_ref[...] = dx
        dxb_ref[...] = dx.astype(BF16)

    row = pl.BlockSpec((bt, d), lambda i: (i, 0))
    vec = pl.BlockSpec((1, d), lambda i: (0, 0))
    return pl.pallas_call(
        body, name="final_loss", grid=(t // bt,),
        out_shape=(jax.ShapeDtypeStruct((t, d), F32), jax.ShapeDtypeStruct((t, d), BF16),
                   jax.ShapeDtypeStruct((1, d), F32), jax.ShapeDtypeStruct((1, LANES), F32)),
        in_specs=[row, row, vec], out_specs=(row, row, vec, pl.BlockSpec((1, LANES), lambda i: (0, 0))),
        compiler_params=_cparams(("arbitrary",)),
    )(x, target, g)


GU_COLS = GATE_WIDTH + POOL_WIDTH
U_BLK = GATE_WIDTH // POOL_WIDTH


def _shift_down(a, k, row):
    return jnp.where(row >= k, pltpu.roll(a, k, 0), 0.0)


def _shift_up(a, k, row):
    n = a.shape[0]
    return jnp.where(row < n - k, pltpu.roll(a, n - k, 0), 0.0)


def _window_delta(u, w, row):
    s, k = u, 1
    while k < w:
        s = s + _shift_down(s, k, row)
        k *= 2
    cnt = jnp.minimum(row + 1, w).astype(F32)
    return s / cnt - u, cnt


def _pool_fwd(gu, pool_w, pool_scale):
    b, s, _ = gu.shape

    def body(u_ref, pw_ref, sc_ref, y_ref):
        row = lax.broadcasted_iota(jnp.int32, (s, POOL_GC), 0)
        for g, w in enumerate(POOL_WINDOWS):
            cols = slice(g * POOL_GC, (g + 1) * POOL_GC)
            d, _ = _window_delta(u_ref[0, :, cols].astype(F32), w, row)
            z = jnp.dot(d.astype(BF16), pw_ref[g].astype(BF16), preferred_element_type=F32)
            y_ref[0, :, cols] = (z * sc_ref[:, cols]).astype(BF16)

    return pl.pallas_call(
        body, name="pool_fwd", out_shape=jax.ShapeDtypeStruct((b, s, POOL_WIDTH), BF16), grid=(b,),
        in_specs=[pl.BlockSpec((1, s, POOL_WIDTH), lambda i: (i, 0, U_BLK)),
                  pl.BlockSpec((4, POOL_GC, POOL_GC), lambda i: (0, 0, 0)),
                  pl.BlockSpec((1, POOL_WIDTH), lambda i: (0, 0))],
        out_specs=pl.BlockSpec((1, s, POOL_WIDTH), lambda i: (i, 0, 0)),
        compiler_params=_cparams(("parallel",)),
    )(gu, pool_w, pool_scale)


def _pool_bwd(gu, dy, pool_w, pool_scale, dgu):
    b, s, _ = gu.shape

    def body(u_ref, dy_ref, pw_ref, sc_ref, dgu_in, du_ref, dpw_ref, dsc_ref):
        del dgu_in

        @pl.when(pl.program_id(0) == 0)
        def _():
            dpw_ref[...] = jnp.zeros_like(dpw_ref)
            dsc_ref[...] = jnp.zeros_like(dsc_ref)

        row = lax.broadcasted_iota(jnp.int32, (s, POOL_GC), 0)
        for g, w in enumerate(POOL_WINDOWS):
            cols = slice(g * POOL_GC, (g + 1) * POOL_GC)
            d, cnt = _window_delta(u_ref[0, :, cols].astype(F32), w, row)
            db = d.astype(BF16)
            pw = pw_ref[g].astype(BF16)
            z = jnp.dot(db, pw, preferred_element_type=F32)
            dyv = dy_ref[0, :, cols]
            dsc_ref[:, cols] += jnp.sum(dyv * z, axis=0, keepdims=True)
            dz = (dyv * sc_ref[:, cols]).astype(BF16)
            dpw_ref[g] += lax.dot_general(db, dz, (((0,), (0,)), ((), ())), preferred_element_type=F32)
            dd = lax.dot_general(dz, pw, (((1,), (1,)), ((), ())), preferred_element_type=F32)
            acc, k = dd / cnt, 1
            while k < w:
                acc = acc + _shift_up(acc, k, row)
                k *= 2
            du_ref[0, :, cols] = (acc - dd).astype(BF16)

    return pl.pallas_call(
        body, name="pool_bwd", grid=(b,),
        out_shape=(jax.ShapeDtypeStruct((b, s, GU_COLS), BF16), jax.ShapeDtypeStruct((4, POOL_GC, POOL_GC), F32),
                   jax.ShapeDtypeStruct((1, POOL_WIDTH), F32)),
        in_specs=[pl.BlockSpec((1, s, POOL_WIDTH), lambda i: (i, 0, U_BLK)),
                  pl.BlockSpec((1, s, POOL_WIDTH), lambda i: (i, 0, 0)),
                  pl.BlockSpec((4, POOL_GC, POOL_GC), lambda i: (0, 0, 0)),
                  pl.BlockSpec((1, POOL_WIDTH), lambda i: (0, 0)), ANY],
        out_specs=(pl.BlockSpec((1, s, POOL_WIDTH), lambda i: (i, 0, U_BLK)),
                   pl.BlockSpec((4, POOL_GC, POOL_GC), lambda i: (0, 0, 0)),
                   pl.BlockSpec((1, POOL_WIDTH), lambda i: (0, 0))),
        input_output_aliases={4: 0},
        compiler_params=_cparams(("arbitrary",)),
    )(gu, dy, pool_w, pool_scale, dgu)


def _forget_cumsum(f, bias, name):
    b, s, c = f.shape

    def body(f_ref, b_ref, c_ref):
        row = lax.broadcasted_iota(jnp.int32, (s, LANES), 0)
        z = f_ref[0] + b_ref[...]
        acc = jnp.minimum(z, 0.0) - jnp.log(1.0 + jnp.exp(-jnp.abs(z)))
        k = 1
        while k < s:
            acc = acc + _shift_down(acc, k, row)
            k *= 2
        c_ref[0] = acc

    return pl.pallas_call(
        body, name=name, out_shape=jax.ShapeDtypeStruct((b, s, c), F32), grid=(b, c // LANES),
        in_specs=[pl.BlockSpec((1, s, LANES), lambda i, j: (i, 0, j)), pl.BlockSpec((1, LANES), lambda i, j: (0, j))],
        out_specs=pl.BlockSpec((1, s, LANES), lambda i, j: (i, 0, j)),
        compiler_params=_cparams(("parallel", "parallel")),
    )(f, bias)


def _forget_bwd(dc, f, bias):
    b, s, _ = f.shape

    def body(dc_ref, f_ref, b_ref, df_ref, db_ref):
        @pl.when(pl.program_id(0) == 0)
        def _():
            db_ref[...] = jnp.zeros_like(db_ref)

        row = lax.broadcasted_iota(jnp.int32, (s, LANES), 0)
        acc, k = dc_ref[0], 1
        while k < s:
            acc = acc + _shift_up(acc, k, row)
            k *= 2
        z = f_ref[0] + b_ref[...]
        df = acc / (1.0 + jnp.exp(z))
        db_ref[...] += jnp.sum(df, axis=0, keepdims=True)
        df_ref[0] = df.astype(BF16)

    blk = pl.BlockSpec((1, s, LANES), lambda i: (i, 0, 0))
    vec = pl.BlockSpec((1, LANES), lambda i: (0, 0))
    return pl.pallas_call(
        body, name="forget_bwd", grid=(b,),
        out_shape=(jax.ShapeDtypeStruct((b, s, LANES), BF16), jax.ShapeDtypeStruct((1, LANES), F32)),
        in_specs=[blk, blk, vec], out_specs=(blk, vec),
        compiler_params=_cparams(("arbitrary",)),
    )(dc, f, bias)


KV_BLK0 = 2
PAIRS = FOX_HEADS // 2
FOX_SCALE = FOX_DH ** -0.5
NT_DIMS = (((1,), (1,)), ((), ()))
TN_DIMS = (((0,), (0,)), ((), ()))


def _stack_heads(v):
    head = lax.broadcasted_iota(jnp.int32, v.shape, 1) // FOX_DH
    zero = jnp.zeros_like(v)
    return jnp.concatenate([jnp.where(head == 0, v, zero), jnp.where(head == 1, v, zero)], axis=0)


def _stack_cols(v):
    return jnp.concatenate([v[:, 0:1], v[:, FOX_DH:FOX_DH + 1]], axis=0)


def _unstack(t, blk):
    head = lax.broadcasted_iota(jnp.int32, (blk, LANES), 1) // FOX_DH
    return jnp.where(head == 0, t[:blk], t[blk:])


def _fox_scores(q_all, kblk, row_bias, cr_ref, kb, masked, blk):
    top = lax.broadcasted_iota(jnp.int32, (2 * blk, 1), 0) < blk
    s = lax.dot_general(q_all, kblk, NT_DIMS, preferred_element_type=F32)
    s = s + (row_bias - jnp.where(top, cr_ref[0, 0, kb], cr_ref[0, 1, kb]))
    if masked:
        r = lax.broadcasted_iota(jnp.int32, (2 * blk, blk), 0)
        keep = jnp.where(r >= blk, r - blk, r) >= lax.broadcasted_iota(jnp.int32, (2 * blk, blk), 1)
        s = jnp.where(keep, s, NEG_INF)
    return s


def _fox_fwd(qkv, c_exp, c_row, ex=None):
    b, s, _ = qkv.shape
    blk = min(ATT_BLOCK, s)
    nq = s // blk

    def body(q_ref, kv_ref, cc_ref, cr_ref, o_ref, ob_ref, lse_ref):
        qi = pl.program_id(2)
        q_all = _stack_heads(q_ref[0] * FOX_SCALE)
        cq = _stack_cols(cc_ref[0])

        def step(kb, carry, masked):
            m, l, acc = carry
            rows = pl.ds(pl.multiple_of(kb * blk, blk), blk)
            sc = _fox_scores(q_all, kv_ref[0, rows, :LANES], cq, cr_ref, kb, masked, blk)
            m_new = jnp.maximum(m, jnp.max(sc, axis=-1, keepdims=True))
            p = jnp.exp(sc - m_new)
            alpha = jnp.exp(m - m_new)
            l = alpha * l + jnp.sum(p, axis=-1, keepdims=True)
            acc = alpha * acc + jnp.dot(p.astype(BF16), kv_ref[0, rows, LANES:], preferred_element_type=F32)
            return m_new, l, acc

        init = (jnp.full((2 * blk, 1), NEG_INF, F32), jnp.zeros((2 * blk, 1), F32), jnp.zeros((2 * blk, LANES), F32))
        m, l, acc = step(qi, lax.fori_loop(0, qi, functools.partial(step, masked=False), init), True)
        o = _unstack(acc / l, blk)
        o_ref[0] = o
        ob_ref[0] = o.astype(BF16)
        lse_ref[0] = _unstack(jnp.broadcast_to(m + jnp.log(l), (2 * blk, LANES)), blk)

    tile = pl.BlockSpec((1, blk, LANES), lambda i, h, q: (i, q, h))
    kvspec = pl.BlockSpec((1, s, 2 * LANES), lambda i, h, q: (i, 0, KV_BLK0 + h))
    shape = jax.ShapeDtypeStruct((b, s, FOX_WIDTH), F32)
    return _hosted_call(
        body, ex, name="fox_fwd", out_shape=(shape, jax.ShapeDtypeStruct((b, s, FOX_WIDTH), BF16), shape),
        grid=(b, PAIRS, nq),
        in_specs=[tile, kvspec, tile, pl.BlockSpec((1, 2, nq, 1, blk), lambda i, h, q: (i, h, 0, 0, 0))],
        out_specs=(tile, tile, tile), args=(qkv, qkv, c_exp, c_row))


def _fox_bwd(qkv, c_exp, c_row, lse, o, do, ex=None):
    b, s, _ = qkv.shape
    blk = min(ATT_BLOCK, s)
    nq = s // blk

    def body(q_ref, kv_ref, cc_ref, cr_ref, lse_ref, o_ref, do_ref, dq_ref, dkv_ref, dcq_ref, dc_ref, dk_acc, dv_acc):
        qi = pl.program_id(2)

        @pl.when(qi == 0)
        def _():
            dk_acc[...] = jnp.zeros_like(dk_acc)
            dv_acc[...] = jnp.zeros_like(dv_acc)
            dc_ref[...] = jnp.zeros_like(dc_ref)

        q_all = _stack_heads(q_ref[0] * FOX_SCALE)
        dov = do_ref[0]
        do_all = _stack_heads(dov.astype(BF16))
        delta = jnp.sum(_stack_heads(dov * o_ref[0]), axis=-1, keepdims=True)
        bias = _stack_cols(cc_ref[0]) - _stack_cols(lse_ref[0])

        def step(kb, carry, masked):
            acc, dcq = carry
            rows = pl.ds(pl.multiple_of(kb * blk, blk), blk)
            kblk = kv_ref[0, rows, :LANES]
            p = jnp.exp(_fox_scores(q_all, kblk, bias, cr_ref, kb, masked, blk))
            dp = lax.dot_general(do_all, kv_ref[0, rows, LANES:], NT_DIMS, preferred_element_type=F32)
            ds = p * (dp - delta)
            dsb = ds.astype(BF16)
            dv_acc[rows, :] += lax.dot_general(p.astype(BF16), do_all, TN_DIMS, preferred_element_type=F32)
            dk_acc[rows, :] += lax.dot_general(dsb, q_all, TN_DIMS, preferred_element_type=F32)
            dc_ref[0, 0, kb] -= jnp.sum(ds[:blk], axis=0, keepdims=True)
            dc_ref[0, 1, kb] -= jnp.sum(ds[blk:], axis=0, keepdims=True)
            acc = acc + jnp.dot(dsb, kblk, preferred_element_type=F32)
            return acc, dcq + jnp.sum(ds, axis=-1, keepdims=True)

        init = (jnp.zeros((2 * blk, LANES), F32), jnp.zeros((2 * blk, 1), F32))
        acc, dcq = step(qi, lax.fori_loop(0, qi, functools.partial(step, masked=False), init), True)
        dq_ref[0] = (_unstack(acc, blk) * FOX_SCALE).astype(BF16)
        dcq_ref[0, 0] = jnp.where(lax.broadcasted_iota(jnp.int32, (blk, 2), 1) == 0, dcq[:blk], dcq[blk:])

        @pl.when(qi == nq - 1)
        def _():
            dkv_ref[0, :, :LANES] = dk_acc[...].astype(BF16)
            dkv_ref[0, :, LANES:] = dv_acc[...].astype(BF16)

    tile = pl.BlockSpec((1, blk, LANES), lambda i, h, q: (i, q, h))
    kvspec = pl.BlockSpec((1, s, 2 * LANES), lambda i, h, q: (i, 0, KV_BLK0 + h))
    crow = pl.BlockSpec((1, 2, nq, 1, blk), lambda i, h, q: (i, h, 0, 0, 0))
    return _hosted_call(
        body, ex, name="fox_bwd", grid=(b, PAIRS, nq),
        out_shape=(jax.ShapeDtypeStruct((b, s, FOX_WIDTH), BF16), jax.ShapeDtypeStruct((b, s, 2 * FOX_WIDTH), BF16),
                   jax.ShapeDtypeStruct((b, PAIRS, s, 2), F32), jax.ShapeDtypeStruct(c_row.shape, F32)),
        in_specs=[tile, kvspec, tile, crow, tile, tile, tile],
        out_specs=(tile, pl.BlockSpec((1, s, 2 * LANES), lambda i, h, q: (i, 0, h)),
                   pl.BlockSpec((1, 1, blk, 2), lambda i, h, q: (i, h, q, 0)), crow),
        scratch=[pltpu.VMEM((s, LANES), F32), pltpu.VMEM((s, LANES), F32)],
        args=(qkv, qkv, c_exp, c_row, lse, o, do))


def _sigmoid(z):
    return 1.0 / (1.0 + jnp.exp(-z))


def _mix_fwd(gu, b_gate, y_pool, y_fox):
    t = gu.shape[0]
    bt = _block(t, 256, 16)

    def body(gp_ref, gf_ref, bp_ref, bf_ref, yp_ref, yf_ref, o_ref):
        gp = _sigmoid(gp_ref[...].astype(F32) + bp_ref[...])
        gf = _sigmoid(gf_ref[...].astype(F32) + bf_ref[...])
        o_ref[...] = (gp * yp_ref[...].astype(F32) + gf * yf_ref[...].astype(F32)).astype(BF16)

    col = lambda j: pl.BlockSpec((bt, D_MODEL), lambda i: (i, j))
    vec = lambda j: pl.BlockSpec((1, D_MODEL), lambda i: (0, j))
    return pl.pallas_call(
        body, name="mix_fwd", out_shape=jax.ShapeDtypeStruct((t, D_MODEL), BF16), grid=(t // bt,),
        in_specs=[col(0), col(1), vec(0), vec(1), col(0), col(0)], out_specs=col(0),
        compiler_params=_cparams(("parallel",)),
    )(gu, gu, b_gate, b_gate, y_pool, y_fox)


def _mix_bwd(gu, b_gate, y_pool, y_fox, dmix):
    t = gu.shape[0]
    bt = _block(t, 256, 16)

    def body(gp_ref, gf_ref, bp_ref, bf_ref, yp_ref, yf_ref, dm_ref, dyp_ref, dyf_ref, dgl_ref, db_ref):
        @pl.when(pl.program_id(0) == 0)
        def _():
            db_ref[...] = jnp.zeros_like(db_ref)

        dm = dm_ref[...]
        gp = _sigmoid(gp_ref[...].astype(F32) + bp_ref[...])
        gf = _sigmoid(gf_ref[...].astype(F32) + bf_ref[...])
        dyp_ref[...] = (dm * gp).astype(BF16)
        dyf_ref[...] = (dm * gf).astype(BF16)
        dlp = dm * yp_ref[...].astype(F32) * gp * (1.0 - gp)
        dlf = dm * yf_ref[...].astype(F32) * gf * (1.0 - gf)
        dgl_ref[:, :D_MODEL] = dlp.astype(BF16)
        dgl_ref[:, D_MODEL:] = dlf.astype(BF16)
        db_ref[:, :D_MODEL] += jnp.sum(dlp, axis=0, keepdims=True)
        db_ref[:, D_MODEL:] += jnp.sum(dlf, axis=0, keepdims=True)

    col = lambda j: pl.BlockSpec((bt, D_MODEL), lambda i: (i, j))
    vec = lambda j: pl.BlockSpec((1, D_MODEL), lambda i: (0, j))
    wide = pl.BlockSpec((bt, GATE_WIDTH), lambda i: (i, 0))
    return pl.pallas_call(
        body, name="mix_bwd", grid=(t // bt,),
        out_shape=(jax.ShapeDtypeStruct((t, D_MODEL), BF16), jax.ShapeDtypeStruct((t, D_MODEL), BF16),
                   jax.ShapeDtypeStruct((t, GU_COLS), BF16), jax.ShapeDtypeStruct((1, GATE_WIDTH), F32)),
        in_specs=[col(0), col(1), vec(0), vec(1), col(0), col(0), col(0)],
        out_specs=(col(0), col(0), wide, pl.BlockSpec((1, GATE_WIDTH), lambda i: (0, 0))),
        compiler_params=_cparams(("arbitrary",)),
    )(gu, gu, b_gate, b_gate, y_pool, y_fox, dmix)


X_SCALE = X_DH ** -0.5


def _xattn_probs(qh, kh):
    s = lax.dot_general(qh, kh, NT_DIMS, preferred_element_type=F32) * X_SCALE
    e = jnp.exp(s - jnp.max(s, axis=-1, keepdims=True))
    return e / jnp.sum(e, axis=-1, keepdims=True)


def _xattn_fwd(q, kv):
    b, s, _ = q.shape
    m = kv.shape[1]
    bq = _block(s, 512, 16)

    def body(q_ref, kv_ref, o_ref):
        for h in range(X_HEADS):
            cols = slice(h * X_DH, (h + 1) * X_DH)
            p = _xattn_probs(q_ref[0, :, cols], kv_ref[0, :, cols])
            vh = kv_ref[0, :, X_WIDTH + h * X_DH:X_WIDTH + (h + 1) * X_DH]
            o_ref[0, :, cols] = jnp.dot(p.astype(BF16), vh, preferred_element_type=F32).astype(BF16)

    return pl.pallas_call(
        body, name="xattn_fwd", out_shape=jax.ShapeDtypeStruct((b, s, X_WIDTH), BF16), grid=(b, s // bq),
        in_specs=[pl.BlockSpec((1, bq, X_WIDTH), lambda i, j: (i, j, 0)),
                  pl.BlockSpec((1, m, 2 * X_WIDTH), lambda i, j: (i, 0, 0))],
        out_specs=pl.BlockSpec((1, bq, X_WIDTH), lambda i, j: (i, j, 0)),
        compiler_params=_cparams(("parallel", "parallel")),
    )(q, kv)


def _xattn_bwd(q, kv, do):
    b, s, _ = q.shape
    m = kv.shape[1]
    bq = _block(s, 512, 16)

    def body(q_ref, kv_ref, do_ref, dq_ref, dkv_ref):
        @pl.when(pl.program_id(1) == 0)
        def _():
            dkv_ref[...] = jnp.zeros_like(dkv_ref)

        for h in range(X_HEADS):
            cols = slice(h * X_DH, (h + 1) * X_DH)
            vcols = slice(X_WIDTH + h * X_DH, X_WIDTH + (h + 1) * X_DH)
            qh, kh, vh, doh = q_ref[0, :, cols], kv_ref[0, :, cols], kv_ref[0, :, vcols], do_ref[0, :, cols]
            p = _xattn_probs(qh, kh)
            dkv_ref[0, :, vcols] += lax.dot_general(p.astype(BF16), doh, TN_DIMS, preferred_element_type=F32)
            dp = lax.dot_general(doh, vh, NT_DIMS, preferred_element_type=F32)
            ds = (p * (dp - jnp.sum(p * dp, axis=-1, keepdims=True)) * X_SCALE).astype(BF16)
            dq_ref[0, :, cols] = jnp.dot(ds, kh, preferred_element_type=F32).astype(BF16)
            dkv_ref[0, :, cols] += lax.dot_general(ds, qh, TN_DIMS, preferred_element_type=F32)

    tile = pl.BlockSpec((1, bq, X_WIDTH), lambda i, j: (i, j, 0))
    mem = pl.BlockSpec((1, m, 2 * X_WIDTH), lambda i, j: (i, 0, 0))
    return pl.pallas_call(
        body, name="xattn_bwd", grid=(b, s // bq),
        out_shape=(jax.ShapeDtypeStruct((b, s, X_WIDTH), BF16), jax.ShapeDtypeStruct((b, m, 2 * X_WIDTH), F32)),
        in_specs=[tile, mem, tile], out_specs=(tile, mem),
        compiler_params=_cparams(("parallel", "arbitrary")),
    )(q, kv, do)


def _swiglu_fwd(gu):
    t = gu.shape[0]
    bt = _block(t, 256, 16)

    def body(gt_ref, up_ref, o_ref):
        gt = gt_ref[...].astype(F32)
        o_ref[...] = (gt * _sigmoid(gt) * up_ref[...].astype(F32)).astype(BF16)

    col = lambda j: pl.BlockSpec((bt, D_FF), lambda i: (i, j))
    return pl.pallas_call(
        body, name="swiglu_fwd", out_shape=jax.ShapeDtypeStruct((t, D_FF), BF16), grid=(t // bt,),
        in_specs=[col(0), col(1)], out_specs=col(0),
        compiler_params=_cparams(("parallel",)),
    )(gu, gu)


def _swiglu_bwd(gu, dact):
    t = gu.shape[0]
    bt = _block(t, 256, 16)

    def body(gt_ref, up_ref, da_ref, o_ref):
        gt = gt_ref[...].astype(F32)
        da = da_ref[...].astype(F32)
        sg = _sigmoid(gt)
        silu = gt * sg
        o_ref[:, :D_FF] = (da * up_ref[...].astype(F32) * (sg + silu * (1.0 - sg))).astype(BF16)
        o_ref[:, D_FF:] = (da * silu).astype(BF16)

    col = lambda j: pl.BlockSpec((bt, D_FF), lambda i: (i, j))
    return pl.pallas_call(
        body, name="swiglu_bwd", out_shape=jax.ShapeDtypeStruct((t, 2 * D_FF), BF16), grid=(t // bt,),
        in_specs=[col(0), col(1), col(0)], out_specs=pl.BlockSpec((bt, 2 * D_FF), lambda i: (i, 0)),
        compiler_params=_cparams(("parallel",)),
    )(gu, gu, dact)


def _stack_of(w, axis):
    r, c = w.shape
    if axis == 0:
        return w.reshape(N_CHIPS, r // N_CHIPS, c)
    return w.reshape(r, N_CHIPS, c // N_CHIPS).transpose(1, 0, 2)


def _stack_t(w3):
    n, r, c = w3.shape
    return w3.transpose(0, 2, 1).reshape(n * c, r)


def _pair_rows(k, v):
    c = k.shape[1]
    return jnp.stack([k.reshape(PAIRS, LANES, c), v.reshape(PAIRS, LANES, c)], axis=1).reshape(2 * FOX_WIDTH, c)


def _unpair_rows(kv):
    c = kv.shape[1]
    kv = kv.reshape(PAIRS, 2, LANES, c)
    return kv[:, 0].reshape(FOX_WIDTH, c), kv[:, 1].reshape(FOX_WIDTH, c)


def _input_grad(parts, weights_t, ex):
    t = parts[0].shape[0]
    d = weights_t[0].shape[1]
    bm = _block(t, 512, 16)
    n = len(parts)

    def body(*refs):
        acc = None
        for a_ref, b_ref in zip(refs[:n], refs[n:2 * n]):
            term = jnp.dot(a_ref[...], b_ref[...], preferred_element_type=F32)
            acc = term if acc is None else acc + term
        refs[2 * n][...] = acc

    (out,), moved = _hosted_call(
        body, ex, name="d_h", grid=(t // bm,), out_shape=(jax.ShapeDtypeStruct((t, d), F32),),
        in_specs=[pl.BlockSpec((bm, p.shape[1]), lambda i: (i, 0)) for p in parts]
        + [pl.BlockSpec(w.shape, lambda i: (0, 0)) for w in weights_t],
        out_specs=(pl.BlockSpec((bm, d), lambda i: (i, 0)),), args=tuple(parts) + tuple(weights_t))
    return out, moved


def _step(x, mem, loss_target, weights, moments_m, moments_v):
    nb, s, d = x.shape
    n_mem = mem.shape[1]
    t = nb * s
    blk = min(ATT_BLOCK, s)
    x2 = x.reshape(t, d)
    mem2 = mem.reshape(nb * n_mem, d)
    tgt2 = loss_target.reshape(t, d)

    def shard2d(a, n):
        a = a.reshape(a.shape[1:])
        return a.T if n == "w_in" else a

    def unshard(a, n):
        return (a.T if n == "w_in" else a)[None]

    local = {n: shard2d(weights[n], n) for n, _, _ in SHARDED}

    names = [n for n, _, _ in SHARDED]
    later = [n for n in names if n != "w_in"]
    local_b = {n: local[n].astype(BF16) for n in names}
    w_in_stack, = _place_own(_run_exchange(_gather_exchange([local_b["w_in"]]), "gather_w_in"), [local_b["w_in"]])

    def w_in_rows(lo, hi):
        per = IN_COLS // N_CHIPS
        parts = [w_in_stack[j, max(lo, j * per) - j * per:min(hi, (j + 1) * per) - j * per]
                 for j in range(N_CHIPS) if max(lo, j * per) < min(hi, (j + 1) * per)]
        return parts[0] if len(parts) == 1 else jnp.concatenate(parts)

    w_gu_t = jnp.concatenate([w_in_rows(2056, IN_COLS), w_in_rows(0, 512)])
    w_qkv_t = jnp.concatenate([w_in_rows(512, 1024), _pair_rows(w_in_rows(1024, 1536), w_in_rows(1536, 2048))])
    w_f_t = jnp.pad(w_in_rows(2048, 2056), ((0, LANES - FOX_HEADS), (0, 0)))
    w_gu, w_qkv, w_f = w_gu_t.T, w_qkv_t.T, w_f_t.T
    w_f_exp = jnp.repeat(w_f[:, :FOX_HEADS], FOX_DH, axis=1)

    g_mix, g_x, g_mem, g_ffn = (weights[n] for n in ("norm_mix_g", "norm_x_g", "norm_mem_g", "norm_ffn_g"))
    g_final = weights["norm_final_g"].reshape(1, d)
    pool_w = weights["pool_w"].reshape(4, POOL_GC, POOL_GC)
    pool_scale, b_gate = weights["pool_scale"], weights["b_gate"]
    b_f_pad = jnp.pad(weights["b_forget"], ((0, 0), (0, LANES - FOX_HEADS)))
    b_f_exp = jnp.repeat(weights["b_forget"], FOX_DH, axis=1)

    h = _rms_fwd(x2, g_mix, "norm_mix")
    gu = _mm(h, w_gu, out_dtype=BF16, bn=512, name="in_proj_gates_pool")
    qkv = _mm(h, w_qkv, out_dtype=BF16, bn=512, name="in_proj_qkv")
    f_pad = _mm(h, w_f, name="in_proj_forget")
    gu3, qkv3 = gu.reshape(nb, s, GU_COLS), qkv.reshape(nb, s, 3 * FOX_WIDTH)
    y = _pool_fwd(gu3, pool_w, pool_scale)
    f_exp = _mm(h, w_f_exp, name="in_proj_forget_lanes").reshape(nb, s, FOX_WIDTH)
    c_exp = _forget_cumsum(f_exp, b_f_exp, "forget_cumsum_lanes")
    c_pad = _forget_cumsum(f_pad.reshape(nb, s, LANES), b_f_pad, "forget_cumsum")
    c_row = c_pad[:, :, :FOX_HEADS].transpose(0, 2, 1).reshape(nb, FOX_HEADS, s // blk, 1, blk)
    (o, o_b, lse), gathered = _fox_fwd(qkv3, c_exp, c_row, ex=_gather_exchange([local_b[n] for n in later]))
    stacks = dict(zip(later, _place_own(gathered, [local_b[n] for n in later])))
    w_pool_out3, w_fox_out3, w_xo3, w_ffn_in3 = (stacks[n] for n in ("w_pool_out", "w_fox_out", "w_xo", "w_ffn_in"))
    w_out, w_xq, w_xkv, w_ffn_out = (stacks[n].reshape(-1, stacks[n].shape[2])
                                     for n in ("w_out", "w_xq", "w_xkv", "w_ffn_out"))
    y2, o2 = y.reshape(t, POOL_WIDTH), o_b.reshape(t, FOX_WIDTH)
    y_pool = _mm(y2, w_pool_out3, b_stack=True, out_dtype=BF16, name="pool_out")
    y_fox = _mm(o2, w_fox_out3, b_stack=True, out_dtype=BF16, name="fox_out")
    mix = _mix_fwd(gu, b_gate, y_pool, y_fox)
    x1 = _mm(mix, w_out, res=x2, name="mix_out")
    hx = _rms_fwd(x1, g_x, "norm_x")
    mem_n = _rms_fwd(mem2, g_mem, "norm_mem")
    qx = _mm(hx, w_xq, out_dtype=BF16, name="x_q")
    kv = _mm(mem_n, w_xkv, out_dtype=BF16, name="x_kv")
    qx3, kv3 = qx.reshape(nb, s, X_WIDTH), kv.reshape(nb, n_mem, 2 * X_WIDTH)
    ox = _xattn_fwd(qx3, kv3).reshape(t, X_WIDTH)
    x2_ = _mm(ox, w_xo3, b_stack=True, res=x1, name="x_out")
    hf = _rms_fwd(x2_, g_ffn, "norm_ffn")
    ffn = _mm(hf, w_ffn_in3, b_stack=True, out_dtype=BF16, bn=1408, name="ffn_in")
    act = _swiglu_fwd(ffn)
    x3 = _mm(act, w_ffn_out, res=x2_, name="ffn_out")

    dx3, dx3_b, dg_final, loss_part = _final_loss(x3, tgt2, g_final)
    dw_ffn_out = _mm(act, dx3_b, ta=True, bm=1408, bn=512, bk=2048, name="d_w_ffn_out")
    dact = _mm(dx3_b, w_ffn_out.T, out_dtype=BF16, bn=1408, name="d_act")
    dffn = _swiglu_bwd(ffn, dact)
    dw_ffn_in = _mm(hf, dffn, ta=True, bm=512, bn=1408, bk=2048, out_stack=True, name="d_w_ffn_in")
    core = lax.axis_index("c").astype(jnp.int32).reshape(1)
    ffn_group = ["w_ffn_in", "w_ffn_out"]
    mid_group = ["w_pool_out", "w_fox_out", "w_out", "w_xq", "w_xkv", "w_xo"]
    grad_stacks = {"w_ffn_in": dw_ffn_in, "w_ffn_out": _stack_of(dw_ffn_out, 0)}

    def presum(group, theirs):
        return [_sum_halves(grad_stacks[n], t_, core, "sum_halves_" + n) for n, t_ in zip(group, theirs)]

    dhf, theirs = _mm(dffn, _stack_t(w_ffn_in3), bk=2816, name="d_hf",
                      ex=_swap_exchange([grad_stacks[n] for n in ffn_group]))
    chip_sums = dict(zip(ffn_group, presum(ffn_group, theirs)))
    dx2, dx2_b, dg_ffn = _rms_bwd(dhf, x2_, g_ffn, dx3, "norm_ffn_bwd")

    dw_xo = _mm(ox, dx2_b, ta=True, bn=256, out_stack=True, name="d_w_xo")
    dox = _mm(dx2_b, _stack_t(w_xo3), out_dtype=BF16, name="d_ox").reshape(nb, s, X_WIDTH)
    dqx, dkv = _xattn_bwd(qx3, kv3, dox)
    dqx2, dkv2 = dqx.reshape(t, X_WIDTH), dkv.reshape(nb * n_mem, 2 * X_WIDTH)
    dw_xkv = _mm(mem_n, dkv2, ta=True, name="d_w_xkv")
    dmem_n = _mm(dkv2, w_xkv.T, name="d_mem_n")
    dg_mem = _rms_bwd(dmem_n, mem2, g_mem, None, "norm_mem_bwd")
    dw_xq = _mm(hx, dqx2, ta=True, name="d_w_xq")
    dhx = _mm(dqx2, w_xq.T, name="d_hx")
    dx1, dx1_b, dg_x = _rms_bwd(dhx, x1, g_x, dx2, "norm_x_bwd")

    dw_out = _mm(mix, dx1_b, ta=True, name="d_w_out")
    dmix = _mm(dx1_b, w_out.T, name="d_mix")
    dyp, dyf, dgu, db_gate = _mix_bwd(gu, b_gate, y_pool, y_fox, dmix)
    dw_pool_out = _mm(y2, dyp, ta=True, bn=256, out_stack=True, name="d_w_pool_out")
    dw_fox_out = _mm(o2, dyf, ta=True, bn=256, out_stack=True, name="d_w_fox_out")
    dy = _mm(dyp, _stack_t(w_pool_out3), name="d_y").reshape(nb, s, POOL_WIDTH)
    do = _mm(dyf, _stack_t(w_fox_out3), name="d_o").reshape(nb, s, FOX_WIDTH)
    dgu3, dpool_w, dpool_scale = _pool_bwd(gu3, dy, pool_w, pool_scale, dgu.reshape(nb, s, GU_COLS))
    grad_stacks.update({"w_pool_out": dw_pool_out, "w_fox_out": dw_fox_out, "w_out": _stack_of(dw_out, 0),
                        "w_xq": _stack_of(dw_xq, 0), "w_xkv": _stack_of(dw_xkv, 0), "w_xo": dw_xo})
    dgu2 = dgu3.reshape(t, GU_COLS)
    dw_gu_t, theirs = _mm(dgu2, h, ta=True, name="d_w_gates_pool",
                          ex=_swap_exchange([grad_stacks[n] for n in mid_group]))
    chip_sums.update(zip(mid_group, presum(mid_group, theirs)))
    early = ffn_group + mid_group
    (dq3, dkv3, dc_q, dc_row), early_slots = _fox_bwd(qkv3, c_exp, c_row, lse, o, do,
                                                      ex=_chips_exchange([chip_sums[n] for n in early]))
    slots = dict(zip(early, early_slots))
    dc = dc_row.reshape(nb, FOX_HEADS, s).transpose(0, 2, 1) + dc_q.transpose(0, 2, 1, 3).reshape(nb, s, FOX_HEADS)
    dc = jnp.pad(dc, ((0, 0), (0, 0), (0, LANES - FOX_HEADS)))
    df, db_f = _forget_bwd(dc, f_pad.reshape(nb, s, LANES), b_f_pad)
    dq2, dkv2, df2 = dq3.reshape(t, FOX_WIDTH), dkv3.reshape(t, 2 * FOX_WIDTH), df.reshape(t, LANES)
    dw_q_t = _mm(dq2, h, ta=True, name="d_w_q")
    dw_kv_t = _mm(dkv2, h, ta=True, name="d_w_kv")
    dw_f_t = _mm(df2, h, ta=True, name="d_w_forget")
    dw_k_t, dw_v_t = _unpair_rows(dw_kv_t)
    dw_in_t = jnp.concatenate([dw_gu_t[GATE_WIDTH:], dw_q_t, dw_k_t, dw_v_t, dw_f_t[:FOX_HEADS],
                               dw_gu_t[:GATE_WIDTH]])
    grad_stacks["w_in"] = dw_in_t.reshape(N_CHIPS, IN_COLS // N_CHIPS, D_MODEL)
    chip_sums["w_in"], = presum(["w_in"], _run_exchange(_swap_exchange([grad_stacks["w_in"]]), "swap_halves_w_in"))
    dh, (slots["w_in"],) = _input_grad([dgu2, dq2, dkv2, df2],
                                       [w_gu_t, w_qkv_t[:FOX_WIDTH], w_qkv_t[FOX_WIDTH:], w_f_t],
                                       _chips_exchange([chip_sums["w_in"]]))
    dx, _, dg_mix = _rms_bwd(dh, x2, g_mix, dx1, "norm_mix_bwd")

    place = jnp.stack([lax.axis_index("c"), 2 * lax.axis_index("x") + lax.axis_index("y")]).astype(jnp.int32)
    reduced = _join_halves([_sum_chips(slots[n], chip_sums[n], place, _by_rows(local[n].shape[0]), "sum_chips_" + n)
                            for n in names])

    small_grads = {"norm_mix_g": dg_mix, "b_forget": db_f[:, :FOX_HEADS], "b_gate": db_gate, "pool_w": dpool_w,
                   "pool_scale": dpool_scale, "norm_x_g": dg_x, "norm_mem_g": dg_mem, "norm_ffn_g": dg_ffn,
                   "norm_final_g": dg_final}
    def flat2d(a):
        return a.reshape(-1, a.shape[-1])

    small_names = [n for n, _ in SMALL]
    own = [flat2d(small_grads[n]) for n in small_names]
    gathered = _run_exchange(_small_exchange(own + [loss_part]), "gather_small")
    device = (4 * lax.axis_index("x") + 2 * lax.axis_index("y") + lax.axis_index("c")).astype(jnp.int32).reshape(1)
    sg, sd, sm, sv, loss_sum = _adamw_small(
        gathered[:-1], own, [flat2d(weights[n]) for n in small_names], [flat2d(moments_m[n]) for n in small_names],
        [flat2d(moments_v[n]) for n in small_names], gathered[-1], loss_part, device)
    loss = loss_sum[0, 0]

    grads, deltas, new_m, new_v = {}, {}, {}, {}
    for n, g_, d_, m_, v_ in zip(small_names, sg, sd, sm, sv):
        grads[n], deltas[n], new_m[n], new_v[n] = (a.reshape(weights[n].shape) for a in (g_, d_, m_, v_))
    def tiles_of(a):
        return a.transpose(2, 0, 1)

    def block_of(a3):
        return a3.transpose(1, 2, 0)

    for n, g_ in zip(names, reduced):
        if n == "w_in":
            g_ = g_.reshape(IN_COLS // N_CHIPS, 1, D_MODEL)
            w_, m_, v_ = tiles_of(weights[n]), tiles_of(moments_m[n]), tiles_of(moments_v[n])
            back = block_of
        else:
            w_, m_, v_ = local[n], shard2d(moments_m[n], n), shard2d(moments_v[n], n)
            back = functools.partial(unshard, n=n)
        d_, m_, v_ = _adamw(w_, g_, m_, v_, "adamw_" + n)
        grads[n], deltas[n], new_m[n], new_v[n] = (back(a) for a in (g_, d_, m_, v_))
    return loss, dx.reshape(nb, s, d), grads, deltas, new_m, new_v


def kernel(x, mem, norm_mix_g, w_in, b_forget, b_gate, pool_w, pool_scale, w_pool_out, w_fox_out, w_out, norm_x_g, norm_mem_g, w_xq, w_xkv, w_xo, norm_ffn_g, w_ffn_in, w_ffn_out, norm_final_g, loss_target, m_norm_mix_g, m_w_in, m_b_forget, m_b_gate, m_pool_w, m_pool_scale, m_w_pool_out, m_w_fox_out, m_w_out, m_norm_x_g, m_norm_mem_g, m_w_xq, m_w_xkv, m_w_xo, m_norm_ffn_g, m_w_ffn_in, m_w_ffn_out, m_norm_final_g, v_norm_mix_g, v_w_in, v_b_forget, v_b_gate, v_pool_w, v_pool_scale, v_w_pool_out, v_w_fox_out, v_w_out, v_norm_x_g, v_norm_mem_g, v_w_xq, v_w_xkv, v_w_xo, v_norm_ffn_g, v_w_ffn_in, v_w_ffn_out, v_norm_final_g):
    given = dict(locals())
    weights = {n: given[n] for n in WEIGHT_ORDER}
    moments_m = {n: given["m_" + n] for n in WEIGHT_ORDER}
    moments_v = {n: given["v_" + n] for n in WEIGHT_ORDER}
    loss, grad_x, grads, deltas, new_m, new_v = _step(x, mem, loss_target, weights, moments_m, moments_v)
    return (loss, grad_x, *[grads[n] for n in WEIGHT_ORDER], *[deltas[n] for n in WEIGHT_ORDER],
            *[new_m[n] for n in WEIGHT_ORDER], *[new_v[n] for n in WEIGHT_ORDER])
```

```python
import functools
import math

import jax
import jax.numpy as jnp
from jax import lax
from jax.experimental import pallas as pl
from jax.experimental.pallas import tpu as pltpu

F32 = jnp.float32
BF16 = jnp.bfloat16
MESH = pl.DeviceIdType.MESH

D_MODEL = 1024
EPS = 1e-6
POOL_WINDOWS = (2, 4, 8, 16)
POOL_WIDTH = 512
POOL_GC = 128
FOX_HEADS = 8
FOX_DH = 64
FOX_WIDTH = 512
X_HEADS = 4
X_DH = 128
X_WIDTH = 512
D_FF = 2816
IN_COLS = 4104
GATE_WIDTH = 2048
ADAM_LR = 0.001
ADAM_B1 = 0.9
ADAM_B2 = 0.999
ADAM_EPS = 1e-08
ADAM_WD = 0.01
ADAM_STEP = 10

N_CHIPS = 4
N_DEV = 8
LANES = 128
VMEM_LIMIT_BYTES = 56 * 1024 * 1024
NEG_INF = -1e30
ATT_BLOCK = 512

SHARDED = (
    ("w_in", (1024, IN_COLS), 1),
    ("w_pool_out", (POOL_WIDTH, 1024), 1),
    ("w_fox_out", (FOX_WIDTH, 1024), 1),
    ("w_out", (1024, 1024), 0),
    ("w_xq", (1024, X_WIDTH), 0),
    ("w_xkv", (1024, 2 * X_WIDTH), 0),
    ("w_xo", (X_WIDTH, 1024), 1),
    ("w_ffn_in", (1024, 2 * D_FF), 1),
    ("w_ffn_out", (D_FF, 1024), 0),
)
SMALL = (
    ("norm_mix_g", (1, 1024)),
    ("b_forget", (1, 8)),
    ("b_gate", (1, 2048)),
    ("pool_w", (1, 4, 128, 128)),
    ("pool_scale", (1, 512)),
    ("norm_x_g", (1, 1024)),
    ("norm_mem_g", (1, 1024)),
    ("norm_ffn_g", (1, 1024)),
    ("norm_final_g", (1024,)),
)
WEIGHT_ORDER = ("norm_mix_g", "w_in", "b_forget", "b_gate", "pool_w", "pool_scale", "w_pool_out", "w_fox_out", "w_out",
                "norm_x_g", "norm_mem_g", "w_xq", "w_xkv", "w_xo", "norm_ffn_g", "w_ffn_in", "w_ffn_out", "norm_final_g")


def _cparams(sem=None):
    return pltpu.CompilerParams(dimension_semantics=sem, vmem_limit_bytes=VMEM_LIMIT_BYTES)


def _block(dim, pref, unit):
    if dim <= pref:
        return dim
    best = None
    for b in range(unit, pref + 1, unit):
        if dim % b == 0:
            best = b
    assert best is not None, (dim, pref, unit)
    return best


def _rows_block(rows, cols, unit=16, elems=1 << 19):
    return _block(rows, max(unit, elems // cols // unit * unit), unit)


def _my_place():
    return lax.axis_index("x"), lax.axis_index("y"), lax.axis_index("c")


def _other_chips(x, y):
    return [(1 - x, y), (x, 1 - y), (1 - x, 1 - y)]


def _chip(place):
    return 2 * place[0] + place[1]


ANY = pl.BlockSpec(memory_space=pl.ANY)


def _by_rows(rows):
    return rows % 32 == 0


def _half_shape(rows, cols):
    return (rows // 2, cols) if _by_rows(rows) else (rows, cols // 2)


def _core_half(ref, core, lead=()):
    rows, cols = ref.shape[-2:]
    if _by_rows(rows):
        return ref.at[(*lead, pl.ds(core * (rows // 2), rows // 2), slice(None))]
    return ref.at[(*lead, slice(None), pl.ds(core * (cols // 2), cols // 2))]


class _Exchange:
    def __init__(self, arrays, out_shapes, n_sems, start, finish, in_place=False):
        self.arrays, self.out_shapes, self.n_sems, self.start, self.finish = arrays, out_shapes, n_sems, start, finish
        self.in_place = in_place

    def scratch(self):
        return [pltpu.SemaphoreType.DMA((self.n_sems,)), pltpu.SemaphoreType.DMA((self.n_sems,))]

    def aliases(self, first_in, first_out):
        return {first_in + k: first_out + k for k in range(len(self.arrays))} if self.in_place else {}


def _run_exchange(ex, name):
    n = len(ex.arrays)

    def body(*refs):
        ins, outs, sems = refs[:n], refs[n:2 * n], refs[2 * n:]
        ex.start(ins, outs, *sems)
        ex.finish(ins, outs, *sems)

    return pl.pallas_call(
        body, name=name, out_shape=ex.out_shapes, in_specs=[ANY] * n, out_specs=[ANY] * n, scratch_shapes=ex.scratch(),
        input_output_aliases=ex.aliases(0, 0),
    )(*ex.arrays)


def _hosted_call(body, ex, *, name, grid, in_specs, out_specs, out_shape, args, scratch=()):
    n_in, n_out, n_scr = len(args), len(out_shape), len(scratch)
    if ex is None:
        outs = pl.pallas_call(
            body, name=name, grid=grid, out_shape=out_shape, in_specs=in_specs, out_specs=out_specs,
            scratch_shapes=list(scratch), compiler_params=_cparams(("arbitrary",) * len(grid)))(*args)
        return outs, None
    nc = len(ex.arrays)

    def full_body(*refs):
        ins, cins = refs[:n_in], refs[n_in:n_in + nc]
        outs, couts = refs[n_in + nc:n_in + nc + n_out], refs[n_in + nc + n_out:n_in + 2 * nc + n_out]
        rest = refs[n_in + 2 * nc + n_out:]
        scr, sems = rest[:n_scr], rest[n_scr:]
        first = functools.reduce(jnp.logical_and, [pl.program_id(a) == 0 for a in range(len(grid))])
        last = functools.reduce(jnp.logical_and, [pl.program_id(a) == grid[a] - 1 for a in range(len(grid))])

        @pl.when(first)
        def _():
            ex.start(cins, couts, *sems)

        body(*ins, *outs, *scr)

        @pl.when(last)
        def _():
            ex.finish(cins, couts, *sems)

    outs = pl.pallas_call(
        full_body, name=name, grid=grid, out_shape=list(out_shape) + list(ex.out_shapes),
        in_specs=list(in_specs) + [ANY] * nc, out_specs=list(out_specs) + [ANY] * nc,
        scratch_shapes=list(scratch) + ex.scratch(), input_output_aliases=ex.aliases(n_in, n_out),
        compiler_params=_cparams(("arbitrary",) * len(grid)))(*args, *ex.arrays)
    return outs[:n_out], outs[n_out:]


def _gather_exchange(shards):
    n = len(shards)

    def copies(ins, outs, send_sems, recv_sems):
        x, y, c = _my_place()

        def half(k, chip, core):
            return _core_half(outs[k], core, lead=(_chip(chip),))

        def copy(k, slot, chip, core, to, src=None):
            return pltpu.make_async_remote_copy(
                src_ref=half(k, chip, core) if src is None else src, dst_ref=half(k, chip, core),
                send_sem=send_sems.at[6 * k + slot], recv_sem=recv_sems.at[6 * k + slot],
                device_id=to, device_id_type=MESH)

        return (x, y, c), copy

    def first_copies(ins, outs, send_sems, recv_sems):
        (x, y, c), copy = copies(ins, outs, send_sems, recv_sems)
        out = []
        for j, chip in enumerate(_other_chips(x, y)):
            for k in range(n):
                out.append(copy(k, j, (x, y), c, (*chip, c), src=_core_half(ins[k], c)))
        return out

    def start(ins, outs, send_sems, recv_sems):
        for cp in first_copies(ins, outs, send_sems, recv_sems):
            cp.start()

    def finish(ins, outs, send_sems, recv_sems):
        (x, y, c), copy = copies(ins, outs, send_sems, recv_sems)
        chips = _other_chips(x, y)
        passed = []
        for j, chip in enumerate(chips):
            for k in range(n):
                copy(k, j, chip, c, (x, y, c)).wait_recv()
                passed.append(copy(k, 3 + j, chip, c, (x, y, 1 - c)))
                passed[-1].start()
        for j, chip in enumerate(chips):
            for k in range(n):
                copy(k, 3 + j, chip, 1 - c, (x, y, c)).wait_recv()
        for cp in first_copies(ins, outs, send_sems, recv_sems) + passed:
            cp.wait_send()

    return _Exchange(list(shards), [jax.ShapeDtypeStruct((N_CHIPS,) + s.shape, s.dtype) for s in shards], 6 * n,
                     start, finish)


def _place_own(stacks, shards):
    me = 2 * lax.axis_index("x") + lax.axis_index("y")
    return [lax.dynamic_update_slice(others, mine[None], (me, 0, 0)) for others, mine in zip(stacks, shards)]


def _swap_exchange(grads):
    n = len(grads)

    def copies(ins, outs, send_sems, recv_sems):
        x, y, c = _my_place()
        return [pltpu.make_async_remote_copy(
            src_ref=_core_half(ins[k], 1 - c, lead=(slice(None),)), dst_ref=outs[k],
            send_sem=send_sems.at[k], recv_sem=recv_sems.at[k], device_id=(x, y, 1 - c), device_id_type=MESH)
            for k in range(n)]

    def start(ins, outs, send_sems, recv_sems):
        for cp in copies(ins, outs, send_sems, recv_sems):
            cp.start()

    def finish(ins, outs, send_sems, recv_sems):
        for cp in copies(ins, outs, send_sems, recv_sems):
            cp.wait()

    return _Exchange(list(grads), [jax.ShapeDtypeStruct((N_CHIPS,) + _half_shape(*g.shape[1:]), g.dtype) for g in grads],
                     n, start, finish)


def _chips_exchange(sums):
    n = len(sums)

    def sends(ins, outs, send_sems, recv_sems):
        x, y, c = _my_place()
        return [pltpu.make_async_remote_copy(
            src_ref=ins[k].at[_chip(chip)], dst_ref=outs[k].at[_chip((x, y))],
            send_sem=send_sems.at[3 * k + j], recv_sem=recv_sems.at[3 * k + j],
            device_id=(*chip, c), device_id_type=MESH)
            for j, chip in enumerate(_other_chips(x, y)) for k in range(n)]

    def start(ins, outs, send_sems, recv_sems):
        for cp in sends(ins, outs, send_sems, recv_sems):
            cp.start()

    def finish(ins, outs, send_sems, recv_sems):
        x, y, c = _my_place()
        for j, chip in enumerate(_other_chips(x, y)):
            for k in range(n):
                slot = outs[k].at[_chip(chip)]
                pltpu.make_async_remote_copy(
                    src_ref=slot, dst_ref=slot, send_sem=send_sems.at[3 * k + j], recv_sem=recv_sems.at[3 * k + j],
                    device_id=(x, y, c), device_id_type=MESH).wait_recv()
        for cp in sends(ins, outs, send_sems, recv_sems):
            cp.wait_send()

    return _Exchange(list(sums), [jax.ShapeDtypeStruct(s.shape, s.dtype) for s in sums], 3 * n, start, finish)


def _join_exchange(shards):
    n = len(shards)

    def sends(ins, outs, send_sems, recv_sems):
        x, y, c = _my_place()
        return [pltpu.make_async_remote_copy(
            src_ref=_core_half(ins[k], c), dst_ref=_core_half(outs[k], c),
            send_sem=send_sems.at[k], recv_sem=recv_sems.at[k], device_id=(x, y, 1 - c), device_id_type=MESH)
            for k in range(n)]

    def start(ins, outs, send_sems, recv_sems):
        for cp in sends(ins, outs, send_sems, recv_sems):
            cp.start()

    def finish(ins, outs, send_sems, recv_sems):
        x, y, c = _my_place()
        for k in range(n):
            theirs = _core_half(outs[k], 1 - c)
            pltpu.make_async_remote_copy(
                src_ref=theirs, dst_ref=theirs, send_sem=send_sems.at[k], recv_sem=recv_sems.at[k],
                device_id=(x, y, c), device_id_type=MESH).wait_recv()
        for cp in sends(ins, outs, send_sems, recv_sems):
            cp.wait_send()

    return _Exchange(list(shards), [jax.ShapeDtypeStruct(s.shape, s.dtype) for s in shards], n, start, finish,
                     in_place=True)


def _small_exchange(blocks):
    n = len(blocks)

    def copies(ins, outs, send_sems, recv_sems):
        x, y, c = _my_place()

        def copy(k, j, whose, to, src=None):
            slot = outs[k].at[4 * whose[0] + 2 * whose[1] + whose[2]]
            return pltpu.make_async_remote_copy(
                src_ref=slot if src is None else src, dst_ref=slot,
                send_sem=send_sems.at[7 * k + j], recv_sem=recv_sems.at[7 * k + j], device_id=to, device_id_type=MESH)

        return (x, y, c), copy

    def first_copies(ins, outs, send_sems, recv_sems):
        (x, y, c), copy = copies(ins, outs, send_sems, recv_sems)
        out = []
        for k in range(n):
            out.append(copy(k, 0, (x, y, c), (x, y, 1 - c), src=ins[k]))
            out += [copy(k, 1 + j, (x, y, c), (*chip, c), src=ins[k]) for j, chip in enumerate(_other_chips(x, y))]
        return out

    def start(ins, outs, send_sems, recv_sems):
        for cp in first_copies(ins, outs, send_sems, recv_sems):
            cp.start()

    def finish(ins, outs, send_sems, recv_sems):
        (x, y, c), copy = copies(ins, outs, send_sems, recv_sems)
        chips = _other_chips(x, y)
        passed = []
        for j, chip in enumerate(chips):
            for k in range(n):
                copy(k, 1 + j, (*chip, c), (x, y, c)).wait_recv()
                passed.append(copy(k, 4 + j, (*chip, c), (x, y, 1 - c)))
                passed[-1].start()
        for k in range(n):
            copy(k, 0, (x, y, 1 - c), (x, y, c)).wait_recv()
        for j, chip in enumerate(chips):
            for k in range(n):
                copy(k, 4 + j, (*chip, 1 - c), (x, y, c)).wait_recv()
        for cp in first_copies(ins, outs, send_sems, recv_sems) + passed:
            cp.wait_send()

    return _Exchange(list(blocks), [jax.ShapeDtypeStruct((N_DEV,) + blk.shape, blk.dtype) for blk in blocks], 7 * n,
                     start, finish)


def _sum_halves(grads, theirs, core, name):
    _, h, cols = theirs.shape
    by_rows = _by_rows(grads.shape[1])
    br = _rows_block(h, cols) if by_rows else h
    nb = h // br

    def body(core_ref, a_ref, b_ref, o_ref):
        o_ref[...] = (a_ref[...] + b_ref[...]).astype(BF16)

    if by_rows:
        mine = pl.BlockSpec((1, br, cols), lambda j, i, core_ref: (j, core_ref[0] * nb + i, 0))
    else:
        mine = pl.BlockSpec((1, br, cols), lambda j, i, core_ref: (j, i, core_ref[0]))
    return pl.pallas_call(
        body, name=name,
        out_shape=jax.ShapeDtypeStruct(theirs.shape, BF16),
        grid_spec=pltpu.PrefetchScalarGridSpec(
            num_scalar_prefetch=1, grid=(N_CHIPS, nb),
            in_specs=[mine, pl.BlockSpec((1, br, cols), lambda j, i, core_ref: (j, i, 0))],
            out_specs=pl.BlockSpec((1, br, cols), lambda j, i, core_ref: (j, i, 0))),
        compiler_params=_cparams(("parallel", "parallel")),
    )(core, grads, theirs)


def _sum_chips(slots, sums, place, by_rows, name):
    _, h, cols = slots.shape
    br = _rows_block(h, cols) if by_rows else h
    nb = h // br

    def body(place_ref, s_ref, own_ref, o_ref):
        me = place_ref[1]
        acc = None
        for k in range(N_CHIPS):
            term = jnp.where(me == k, own_ref[k], s_ref[k]).astype(F32)
            acc = term if acc is None else acc + term
        o_ref[...] = acc

    stack = pl.BlockSpec((N_CHIPS, br, cols), lambda i, place_ref: (0, i, 0))
    if by_rows:
        out_shape, out_map = (2 * h, cols), lambda i, place_ref: (place_ref[0] * nb + i, 0)
    else:
        out_shape, out_map = (h, 2 * cols), lambda i, place_ref: (i, place_ref[0])
    return pl.pallas_call(
        body, name=name,
        out_shape=jax.ShapeDtypeStruct(out_shape, F32),
        grid_spec=pltpu.PrefetchScalarGridSpec(
            num_scalar_prefetch=1, grid=(nb,), in_specs=[stack, stack],
            out_specs=pl.BlockSpec((br, cols), out_map)),
        compiler_params=_cparams(("parallel",)),
    )(place, slots, sums)


def _adamw_math(w, g, m, v):
    m = ADAM_B1 * m + (1.0 - ADAM_B1) * g
    v = ADAM_B2 * v + (1.0 - ADAM_B2) * (g * g)
    m_hat = m / (1.0 - ADAM_B1 ** ADAM_STEP)
    v_hat = v / (1.0 - ADAM_B2 ** ADAM_STEP)
    delta = -ADAM_LR * (m_hat / (jnp.sqrt(v_hat) + ADAM_EPS) + ADAM_WD * w)
    return delta, m, v


def _adamw(w, g, m, v, name, ex=None):
    def body(w_ref, g_ref, m_ref, v_ref, d_ref, nm_ref, nv_ref):
        d, nm, nv = _adamw_math(w_ref[...], g_ref[...], m_ref[...], v_ref[...])
        d_ref[...] = d
        nm_ref[...] = nm
        nv_ref[...] = nv

    if w.ndim == 3:
        rows = w.shape[0]
        br = max(b for b in range(1, 65) if rows % b == 0)
        spec, steps = pl.BlockSpec((br,) + w.shape[1:], lambda i: (i, 0, 0)), rows // br
    else:
        rows, cols = w.shape
        br = _rows_block(rows, cols, unit=8)
        spec, steps = pl.BlockSpec((br, cols), lambda i: (i, 0)), rows // br
    shape = jax.ShapeDtypeStruct(w.shape, F32)
    outs, moved = _hosted_call(
        body, ex, name=name, out_shape=(shape, shape, shape), grid=(steps,),
        in_specs=[spec] * 4, out_specs=(spec, spec, spec), args=(w, g, m, v))
    return tuple(outs) if ex is None else (tuple(outs), moved)


def _adamw_small(parts, own, ws, ms, vs, loss_parts, loss_own, device):
    n = len(ws)

    def total(device_ref, parts_ref, own_ref):
        acc = None
        for dev in range(N_DEV):
            term = jnp.where(device_ref[0] == dev, own_ref[...], parts_ref[dev])
            acc = term if acc is None else acc + term
        return acc

    def body(device_ref, *refs):
        ins, outs = refs[:5 * n + 2], refs[5 * n + 2:]
        for k in range(n):
            g = total(device_ref, ins[k], ins[n + k])
            d, nm, nv = _adamw_math(ins[2 * n + k][...], g, ins[3 * n + k][...], ins[4 * n + k][...])
            for o_ref, val in zip(outs[k::n][:4], (g, d, nm, nv)):
                o_ref[...] = val
        outs[4 * n][...] = total(device_ref, ins[5 * n], ins[5 * n + 1])

    args = list(parts) + list(own) + list(ws) + list(ms) + list(vs) + [loss_parts, loss_own]
    whole = lambda a: pl.BlockSpec(a.shape, lambda i, device_ref, nd=a.ndim: (0,) * nd)
    shapes = [jax.ShapeDtypeStruct(w.shape, F32) for w in ws] * 4 + [jax.ShapeDtypeStruct(loss_own.shape, F32)]
    outs = pl.pallas_call(
        body, name="adamw_small", out_shape=shapes,
        grid_spec=pltpu.PrefetchScalarGridSpec(
            num_scalar_prefetch=1, grid=(1,), in_specs=[whole(a) for a in args], out_specs=[whole(a) for a in shapes]),
        compiler_params=_cparams(("arbitrary",)),
    )(device, *args)
    return outs[:n], outs[n:2 * n], outs[2 * n:3 * n], outs[3 * n:4 * n], outs[4 * n]


def _mm(a, b, *, name, ta=False, out_dtype=F32, res=None, bm=1024, bn=1024, bk=4096, b_stack=False, out_stack=False,
        ex=None):
    if ta:
        kdim, m = a.shape
    else:
        m, kdim = a.shape
    if b_stack:
        _, kb, chunk = b.shape
        n = N_CHIPS * chunk
    else:
        kb, n = b.shape
        chunk = n // N_CHIPS if out_stack else n
    assert kdim == kb, (a.shape, b.shape, ta)
    bm = _block(m, bm, LANES if ta else 16)
    bn = _block(chunk, bn, LANES)
    bk = _block(kdim, bk, LANES)
    nk = kdim // bk
    per_chunk = chunk // bn
    dims = (((0 if ta else 1,), (0,)), ((), ()))

    def body(*refs):
        refs = list(refs)
        a_ref, b_ref = refs[:2]
        r_ref = refs[2] if res is not None else None
        o_ref = refs[3] if res is not None else refs[2]
        part = lax.dot_general(a_ref[...].astype(BF16), b_ref[...].astype(BF16), dims, preferred_element_type=F32)

        def finish(r):
            if r_ref is not None:
                r = r + r_ref[...]
            o_ref[...] = r.astype(out_dtype)

        if nk == 1:
            finish(part)
        else:
            acc_ref = refs[-1]
            k = pl.program_id(2)

            @pl.when(k == 0)
            def _():
                acc_ref[...] = part

            @pl.when(k > 0)
            def _():
                acc_ref[...] += part

            @pl.when(k == nk - 1)
            def _():
                finish(acc_ref[...])

    a_spec = pl.BlockSpec((bk, bm), lambda i, j, k: (k, i)) if ta else pl.BlockSpec((bm, bk), lambda i, j, k: (i, k))
    if b_stack:
        b_spec = pl.BlockSpec((None, bk, bn), lambda i, j, k: (j // per_chunk, k, j % per_chunk))
    else:
        b_spec = pl.BlockSpec((bk, bn), lambda i, j, k: (k, j))
    r_spec = pl.BlockSpec((bm, bn), lambda i, j, k: (i, j))
    if out_stack:
        o_spec = pl.BlockSpec((None, bm, bn), lambda i, j, k: (j // per_chunk, i, j % per_chunk))
        o_shape = (N_CHIPS, m, chunk)
    else:
        o_spec, o_shape = r_spec, (m, n)
    in_specs = [a_spec, b_spec] + ([r_spec] if res is not None else [])
    args = (a, b) + ((res,) if res is not None else ())
    (out,), moved = _hosted_call(
        body, ex, name=name, out_shape=(jax.ShapeDtypeStruct(o_shape, out_dtype),),
        grid=(m // bm, n // bn, nk), in_specs=in_specs, out_specs=(o_spec,),
        scratch=[pltpu.VMEM((bm, bn), F32)] if nk > 1 else [], args=args)
    return out if ex is None else (out, moved)


def _rms_fwd(x, g, name, ex=None):
    t, d = x.shape
    bt = _block(t, 512, 16)

    def body(x_ref, g_ref, h_ref):
        xv = x_ref[...]
        r = lax.rsqrt(jnp.mean(xv * xv, axis=-1, keepdims=True) + EPS)
        h_ref[...] = (xv * r * g_ref[...]).astype(BF16)

    (out,), moved = _hosted_call(
        body, ex, name=name, out_shape=(jax.ShapeDtypeStruct((t, d), BF16),), grid=(t // bt,),
        in_specs=[pl.BlockSpec((bt, d), lambda i: (i, 0)), pl.BlockSpec((1, d), lambda i: (0, 0))],
        out_specs=(pl.BlockSpec((bt, d), lambda i: (i, 0)),), args=(x, g))
    return out if ex is None else (out, moved)


def _rms_bwd(dh, x, g, dres, name, ex=None):
    t, d = x.shape
    bt = _block(t, 256, 16)
    want_dx = dres is not None

    def body(*refs):
        if want_dx:
            dh_ref, x_ref, g_ref, dres_ref, dx_ref, dxb_ref, dg_ref = refs
        else:
            dh_ref, x_ref, g_ref, dg_ref = refs
        xv = x_ref[...]
        r = lax.rsqrt(jnp.mean(xv * xv, axis=-1, keepdims=True) + EPS)
        xhat = xv * r
        dhv = dh_ref[...]

        @pl.when(pl.program_id(0) == 0)
        def _():
            dg_ref[...] = jnp.zeros_like(dg_ref)

        dg_ref[...] += jnp.sum(dhv * xhat, axis=0, keepdims=True)
        if want_dx:
            dxhat = dhv * g_ref[...]
            dx = dres_ref[...] + r * (dxhat - xhat * jnp.mean(dxhat * xhat, axis=-1, keepdims=True))
            dx_ref[...] = dx
            dxb_ref[...] = dx.astype(BF16)

    row = pl.BlockSpec((bt, d), lambda i: (i, 0))
    vec = pl.BlockSpec((1, d), lambda i: (0, 0))
    if want_dx:
        outs, moved = _hosted_call(
            body, ex, name=name, grid=(t // bt,),
            out_shape=(jax.ShapeDtypeStruct((t, d), F32), jax.ShapeDtypeStruct((t, d), BF16),
                       jax.ShapeDtypeStruct((1, d), F32)),
            in_specs=[row, row, vec, row], out_specs=(row, row, vec), args=(dh, x, g, dres))
        return tuple(outs) if ex is None else (tuple(outs), moved)
    return pl.pallas_call(
        body, name=name, grid=(t // bt,), out_shape=jax.ShapeDtypeStruct((1, d), F32),
        in_specs=[row, row, vec], out_specs=vec,
        compiler_params=_cparams(("arbitrary",)),
    )(dh, x, g)


def _final_loss(x, target, g):
    t, d = x.shape
    bt = _block(t, 256, 16)

    def body(x_ref, t_ref, g_ref, dx_ref, dxb_ref, dg_ref, loss_ref):
        xv = x_ref[...]
        gv = g_ref[...]
        r = lax.rsqrt(jnp.mean(xv * xv, axis=-1, keepdims=True) + EPS)
        xhat = xv * r
        err = xhat * gv - t_ref[...]

        @pl.when(pl.program_id(0) == 0)
        def _():
            dg_ref[...] = jnp.zeros_like(dg_ref)
            loss_ref[...] = jnp.zeros_like(loss_ref)

        loss_ref[...] += 0.5 * jnp.sum(jnp.mean(err * err, axis=-1, keepdims=True), axis=0, keepdims=True)
        dy = err * (1.0 / d)
        dg_ref[...] += jnp.sum(dy * xhat, axis=0, keepdims=True)
        dxhat = dy * gv
        dx = r * (dxhat - xhat * jnp.mean(dxhat * xhat, axis=-1, keepdims=True))
        dx_ref[...] = dx
        dxb_ref[...] = dx.astype(BF16)

    row = pl.BlockSpec((bt, d), lambda i: (i, 0))
    vec = pl.BlockSpec((1, d), lambda i: (0, 0))
    return pl.pallas_call(
        body, name="final_loss", grid=(t // bt,),
        out_shape=(jax.ShapeDtypeStruct((t, d), F32), jax.ShapeDtypeStruct((t, d), BF16),
                   jax.ShapeDtypeStruct((1, d), F32), jax.ShapeDtypeStruct((1, LANES), F32)),
        in_specs=[row, row, vec], out_specs=(row, row, vec, pl.BlockSpec((1, LANES), lambda i: (0, 0))),
        compiler_params=_cparams(("arbitrary",)),
    )(x, target, g)


GU_COLS = GATE_WIDTH + POOL_WIDTH
U_BLK = GATE_WIDTH // POOL_WIDTH


def _shift_down(a, k, row):
    return jnp.where(row >= k, pltpu.roll(a, k, 0), 0.0)


def _shift_up(a, k, row):
    n = a.shape[0]
    return jnp.where(row < n - k, pltpu.roll(a, n - k, 0), 0.0)


def _window_delta(u, w, row):
    s, k = u, 1
    while k < w:
        s = s + _shift_down(s, k, row)
        k *= 2
    cnt = jnp.minimum(row + 1, w).astype(F32)
    return s / cnt - u, cnt


def _pool_fwd(gu, pool_w, pool_scale):
    b, s, _ = gu.shape

    def body(u_ref, pw_ref, sc_ref, y_ref):
        row = lax.broadcasted_iota(jnp.int32, (s, POOL_GC), 0)
        for g, w in enumerate(POOL_WINDOWS):
            cols = slice(g * POOL_GC, (g + 1) * POOL_GC)
            d, _ = _window_delta(u_ref[0, :, cols].astype(F32), w, row)
            z = jnp.dot(d.astype(BF16), pw_ref[g].astype(BF16), preferred_element_type=F32)
            y_ref[0, :, cols] = (z * sc_ref[:, cols]).astype(BF16)

    return pl.pallas_call(
        body, name="pool_fwd", out_shape=jax.ShapeDtypeStruct((b, s, POOL_WIDTH), BF16), grid=(b,),
        in_specs=[pl.BlockSpec((1, s, POOL_WIDTH), lambda i: (i, 0, U_BLK)),
                  pl.BlockSpec((4, POOL_GC, POOL_GC), lambda i: (0, 0, 0)),
                  pl.BlockSpec((1, POOL_WIDTH), lambda i: (0, 0))],
        out_specs=pl.BlockSpec((1, s, POOL_WIDTH), lambda i: (i, 0, 0)),
        compiler_params=_cparams(("parallel",)),
    )(gu, pool_w, pool_scale)


def _pool_bwd(gu, dy, pool_w, pool_scale, dgu):
    b, s, _ = gu.shape

    def body(u_ref, dy_ref, pw_ref, sc_ref, dgu_in, du_ref, dpw_ref, dsc_ref):
        del dgu_in

        @pl.when(pl.program_id(0) == 0)
        def _():
            dpw_ref[...] = jnp.zeros_like(dpw_ref)
            dsc_ref[...] = jnp.zeros_like(dsc_ref)

        row = lax.broadcasted_iota(jnp.int32, (s, POOL_GC), 0)
        for g, w in enumerate(POOL_WINDOWS):
            cols = slice(g * POOL_GC, (g + 1) * POOL_GC)
            d, cnt = _window_delta(u_ref[0, :, cols].astype(F32), w, row)
            db = d.astype(BF16)
            pw = pw_ref[g].astype(BF16)
            z = jnp.dot(db, pw, preferred_element_type=F32)
            dyv = dy_ref[0, :, cols]
            dsc_ref[:, cols] += jnp.sum(dyv * z, axis=0, keepdims=True)
            dz = (dyv * sc_ref[:, cols]).astype(BF16)
            dpw_ref[g] += lax.dot_general(db, dz, (((0,), (0,)), ((), ())), preferred_element_type=F32)
            dd = lax.dot_general(dz, pw, (((1,), (1,)), ((), ())), preferred_element_type=F32)
            acc, k = dd / cnt, 1
            while k < w:
                acc = acc + _shift_up(acc, k, row)
                k *= 2
            du_ref[0, :, cols] = (acc - dd).astype(BF16)

    return pl.pallas_call(
        body, name="pool_bwd", grid=(b,),
        out_shape=(jax.ShapeDtypeStruct((b, s, GU_COLS), BF16), jax.ShapeDtypeStruct((4, POOL_GC, POOL_GC), F32),
                   jax.ShapeDtypeStruct((1, POOL_WIDTH), F32)),
        in_specs=[pl.BlockSpec((1, s, POOL_WIDTH), lambda i: (i, 0, U_BLK)),
                  pl.BlockSpec((1, s, POOL_WIDTH), lambda i: (i, 0, 0)),
                  pl.BlockSpec((4, POOL_GC, POOL_GC), lambda i: (0, 0, 0)),
                  pl.BlockSpec((1, POOL_WIDTH), lambda i: (0, 0)), ANY],
        out_specs=(pl.BlockSpec((1, s, POOL_WIDTH), lambda i: (i, 0, U_BLK)),
                   pl.BlockSpec((4, POOL_GC, POOL_GC), lambda i: (0, 0, 0)),
                   pl.BlockSpec((1, POOL_WIDTH), lambda i: (0, 0))),
        input_output_aliases={4: 0},
        compiler_params=_cparams(("arbitrary",)),
    )(gu, dy, pool_w, pool_scale, dgu)


def _forget_cumsum(f, bias, name):
    b, s, c = f.shape

    def body(f_ref, b_ref, c_ref):
        row = lax.broadcasted_iota(jnp.int32, (s, LANES), 0)
        z = f_ref[0] + b_ref[...]
        acc = jnp.minimum(z, 0.0) - jnp.log(1.0 + jnp.exp(-jnp.abs(z)))
        k = 1
        while k < s:
            acc = acc + _shift_down(acc, k, row)
            k *= 2
        c_ref[0] = acc

    return pl.pallas_call(
        body, name=name, out_shape=jax.ShapeDtypeStruct((b, s, c), F32), grid=(b, c // LANES),
        in_specs=[pl.BlockSpec((1, s, LANES), lambda i, j: (i, 0, j)), pl.BlockSpec((1, LANES), lambda i, j: (0, j))],
        out_specs=pl.BlockSpec((1, s, LANES), lambda i, j: (i, 0, j)),
        compiler_params=_cparams(("parallel", "parallel")),
    )(f, bias)


def _forget_bwd(dc, f, bias):
    b, s, _ = f.shape

    def body(dc_ref, f_ref, b_ref, df_ref, db_ref):
        @pl.when(pl.program_id(0) == 0)
        def _():
            db_ref[...] = jnp.zeros_like(db_ref)

        row = lax.broadcasted_iota(jnp.int32, (s, LANES), 0)
        acc, k = dc_ref[0], 1
        while k < s:
            acc = acc + _shift_up(acc, k, row)
            k *= 2
        z = f_ref[0] + b_ref[...]
        df = acc / (1.0 + jnp.exp(z))
        db_ref[...] += jnp.sum(df, axis=0, keepdims=True)
        df_ref[0] = df.astype(BF16)

    blk = pl.BlockSpec((1, s, LANES), lambda i: (i, 0, 0))
    vec = pl.BlockSpec((1, LANES), lambda i: (0, 0))
    return pl.pallas_call(
        body, name="forget_bwd", grid=(b,),
        out_shape=(jax.ShapeDtypeStruct((b, s, LANES), BF16), jax.ShapeDtypeStruct((1, LANES), F32)),
        in_specs=[blk, blk, vec], out_specs=(blk, vec),
        compiler_params=_cparams(("arbitrary",)),
    )(dc, f, bias)


KV_BLK0 = 2
PAIRS = FOX_HEADS // 2
FOX_SCALE = FOX_DH ** -0.5
NT_DIMS = (((1,), (1,)), ((), ()))
TN_DIMS = (((0,), (0,)), ((), ()))


def _stack_heads(v):
    head = lax.broadcasted_iota(jnp.int32, v.shape, 1) // FOX_DH
    zero = jnp.zeros_like(v)
    return jnp.concatenate([jnp.where(head == 0, v, zero), jnp.where(head == 1, v, zero)], axis=0)


def _stack_cols(v):
    return jnp.concatenate([v[:, 0:1], v[:, FOX_DH:FOX_DH + 1]], axis=0)


def _unstack(t, blk):
    head = lax.broadcasted_iota(jnp.int32, (blk, LANES), 1) // FOX_DH
    return jnp.where(head == 0, t[:blk], t[blk:])


def _fox_scores(q_all, kblk, row_bias, cr_ref, kb, masked, blk):
    top = lax.broadcasted_iota(jnp.int32, (2 * blk, 1), 0) < blk
    s = lax.dot_general(q_all, kblk, NT_DIMS, preferred_element_type=F32)
    s = s + (row_bias - jnp.where(top, cr_ref[0, 0, kb], cr_ref[0, 1, kb]))
    if masked:
        r = lax.broadcasted_iota(jnp.int32, (2 * blk, blk), 0)
        keep = jnp.where(r >= blk, r - blk, r) >= lax.broadcasted_iota(jnp.int32, (2 * blk, blk), 1)
        s = jnp.where(keep, s, NEG_INF)
    return s


def _fox_fwd(qkv, c_exp, c_row, ex=None):
    b, s, _ = qkv.shape
    blk = min(ATT_BLOCK, s)
    nq = s // blk

    def body(q_ref, kv_ref, cc_ref, cr_ref, o_ref, ob_ref, lse_ref):
        qi = pl.program_id(2)
        q_all = _stack_heads(q_ref[0] * FOX_SCALE)
        cq = _stack_cols(cc_ref[0])

        def step(kb, carry, masked):
            m, l, acc = carry
            rows = pl.ds(pl.multiple_of(kb * blk, blk), blk)
            sc = _fox_scores(q_all, kv_ref[0, rows, :LANES], cq, cr_ref, kb, masked, blk)
            m_new = jnp.maximum(m, jnp.max(sc, axis=-1, keepdims=True))
            p = jnp.exp(sc - m_new)
            alpha = jnp.exp(m - m_new)
            l = alpha * l + jnp.sum(p, axis=-1, keepdims=True)
            acc = alpha * acc + jnp.dot(p.astype(BF16), kv_ref[0, rows, LANES:], preferred_element_type=F32)
            return m_new, l, acc

        init = (jnp.full((2 * blk, 1), NEG_INF, F32), jnp.zeros((2 * blk, 1), F32), jnp.zeros((2 * blk, LANES), F32))
        m, l, acc = step(qi, lax.fori_loop(0, qi, functools.partial(step, masked=False), init), True)
        o = _unstack(acc / l, blk)
        o_ref[0] = o
        ob_ref[0] = o.astype(BF16)
        lse_ref[0] = _unstack(jnp.broadcast_to(m + jnp.log(l), (2 * blk, LANES)), blk)

    tile = pl.BlockSpec((1, blk, LANES), lambda i, h, q: (i, q, h))
    kvspec = pl.BlockSpec((1, s, 2 * LANES), lambda i, h, q: (i, 0, KV_BLK0 + h))
    shape = jax.ShapeDtypeStruct((b, s, FOX_WIDTH), F32)
    return _hosted_call(
        body, ex, name="fox_fwd", out_shape=(shape, jax.ShapeDtypeStruct((b, s, FOX_WIDTH), BF16), shape),
        grid=(b, PAIRS, nq),
        in_specs=[tile, kvspec, tile, pl.BlockSpec((1, 2, nq, 1, blk), lambda i, h, q: (i, h, 0, 0, 0))],
        out_specs=(tile, tile, tile), args=(qkv, qkv, c_exp, c_row))


def _fox_bwd(qkv, c_exp, c_row, lse, o, do, ex=None):
    b, s, _ = qkv.shape
    blk = min(ATT_BLOCK, s)
    nq = s // blk

    def body(q_ref, kv_ref, cc_ref, cr_ref, lse_ref, o_ref, do_ref, dq_ref, dkv_ref, dcq_ref, dc_ref, dk_acc, dv_acc):
        qi = pl.program_id(2)

        @pl.when(qi == 0)
        def _():
            dk_acc[...] = jnp.zeros_like(dk_acc)
            dv_acc[...] = jnp.zeros_like(dv_acc)
            dc_ref[...] = jnp.zeros_like(dc_ref)

        q_all = _stack_heads(q_ref[0] * FOX_SCALE)
        dov = do_ref[0]
        do_all = _stack_heads(dov.astype(BF16))
        delta = jnp.sum(_stack_heads(dov * o_ref[0]), axis=-1, keepdims=True)
        bias = _stack_cols(cc_ref[0]) - _stack_cols(lse_ref[0])

        def step(kb, carry, masked):
            acc, dcq = carry
            rows = pl.ds(pl.multiple_of(kb * blk, blk), blk)
            kblk = kv_ref[0, rows, :LANES]
            p = jnp.exp(_fox_scores(q_all, kblk, bias, cr_ref, kb, masked, blk))
            dp = lax.dot_general(do_all, kv_ref[0, rows, LANES:], NT_DIMS, preferred_element_type=F32)
            ds = p * (dp - delta)
            dsb = ds.astype(BF16)
            dv_acc[rows, :] += lax.dot_general(p.astype(BF16), do_all, TN_DIMS, preferred_element_type=F32)
            dk_acc[rows, :] += lax.dot_general(dsb, q_all, TN_DIMS, preferred_element_type=F32)
            dc_ref[0, 0, kb] -= jnp.sum(ds[:blk], axis=0, keepdims=True)
            dc_ref[0, 1, kb] -= jnp.sum(ds[blk:], axis=0, keepdims=True)
            acc = acc + jnp.dot(dsb, kblk, preferred_element_type=F32)
            return acc, dcq + jnp.sum(ds, axis=-1, keepdims=True)

        init = (jnp.zeros((2 * blk, LANES), F32), jnp.zeros((2 * blk, 1), F32))
        acc, dcq = step(qi, lax.fori_loop(0, qi, functools.partial(step, masked=False), init), True)
        dq_ref[0] = (_unstack(acc, blk) * FOX_SCALE).astype(BF16)
        dcq_ref[0, 0] = jnp.where(lax.broadcasted_iota(jnp.int32, (blk, 2), 1) == 0, dcq[:blk], dcq[blk:])

        @pl.when(qi == nq - 1)
        def _():
            dkv_ref[0, :, :LANES] = dk_acc[...].astype(BF16)
            dkv_ref[0, :, LANES:] = dv_acc[...].astype(BF16)

    tile = pl.BlockSpec((1, blk, LANES), lambda i, h, q: (i, q, h))
    kvspec = pl.BlockSpec((1, s, 2 * LANES), lambda i, h, q: (i, 0, KV_BLK0 + h))
    crow = pl.BlockSpec((1, 2, nq, 1, blk), lambda i, h, q: (i, h, 0, 0, 0))
    return _hosted_call(
        body, ex, name="fox_bwd", grid=(b, PAIRS, nq),
        out_shape=(jax.ShapeDtypeStruct((b, s, FOX_WIDTH), BF16), jax.ShapeDtypeStruct((b, s, 2 * FOX_WIDTH), BF16),
                   jax.ShapeDtypeStruct((b, PAIRS, s, 2), F32), jax.ShapeDtypeStruct(c_row.shape, F32)),
        in_specs=[tile, kvspec, tile, crow, tile, tile, tile],
        out_specs=(tile, pl.BlockSpec((1, s, 2 * LANES), lambda i, h, q: (i, 0, h)),
                   pl.BlockSpec((1, 1, blk, 2), lambda i, h, q: (i, h, q, 0)), crow),
        scratch=[pltpu.VMEM((s, LANES), F32), pltpu.VMEM((s, LANES), F32)],
        args=(qkv, qkv, c_exp, c_row, lse, o, do))


def _sigmoid(z):
    return 1.0 / (1.0 + jnp.exp(-z))


def _mix_fwd(gu, b_gate, y_pool, y_fox):
    t = gu.shape[0]
    bt = _block(t, 256, 16)

    def body(gp_ref, gf_ref, bp_ref, bf_ref, yp_ref, yf_ref, o_ref):
        gp = _sigmoid(gp_ref[...].astype(F32) + bp_ref[...])
        gf = _sigmoid(gf_ref[...].astype(F32) + bf_ref[...])
        o_ref[...] = (gp * yp_ref[...].astype(F32) + gf * yf_ref[...].astype(F32)).astype(BF16)

    col = lambda j: pl.BlockSpec((bt, D_MODEL), lambda i: (i, j))
    vec = lambda j: pl.BlockSpec((1, D_MODEL), lambda i: (0, j))
    return pl.pallas_call(
        body, name="mix_fwd", out_shape=jax.ShapeDtypeStruct((t, D_MODEL), BF16), grid=(t // bt,),
        in_specs=[col(0), col(1), vec(0), vec(1), col(0), col(0)], out_specs=col(0),
        compiler_params=_cparams(("parallel",)),
    )(gu, gu, b_gate, b_gate, y_pool, y_fox)


def _mix_bwd(gu, b_gate, y_pool, y_fox, dmix):
    t = gu.shape[0]
    bt = _block(t, 256, 16)

    def body(gp_ref, gf_ref, bp_ref, bf_ref, yp_ref, yf_ref, dm_ref, dyp_ref, dyf_ref, dgl_ref, db_ref):
        @pl.when(pl.program_id(0) == 0)
        def _():
            db_ref[...] = jnp.zeros_like(db_ref)

        dm = dm_ref[...]
        gp = _sigmoid(gp_ref[...].astype(F32) + bp_ref[...])
        gf = _sigmoid(gf_ref[...].astype(F32) + bf_ref[...])
        dyp_ref[...] = (dm * gp).astype(BF16)
        dyf_ref[...] = (dm * gf).astype(BF16)
        dlp = dm * yp_ref[...].astype(F32) * gp * (1.0 - gp)
        dlf = dm * yf_ref[...].astype(F32) * gf * (1.0 - gf)
        dgl_ref[:, :D_MODEL] = dlp.astype(BF16)
        dgl_ref[:, D_MODEL:] = dlf.astype(BF16)
        db_ref[:, :D_MODEL] += jnp.sum(dlp, axis=0, keepdims=True)
        db_ref[:, D_MODEL:] += jnp.sum(dlf, axis=0, keepdims=True)

    col = lambda j: pl.BlockSpec((bt, D_MODEL), lambda i: (i, j))
    vec = lambda j: pl.BlockSpec((1, D_MODEL), lambda i: (0, j))
    wide = pl.BlockSpec((bt, GATE_WIDTH), lambda i: (i, 0))
    return pl.pallas_call(
        body, name="mix_bwd", grid=(t // bt,),
        out_shape=(jax.ShapeDtypeStruct((t, D_MODEL), BF16), jax.ShapeDtypeStruct((t, D_MODEL), BF16),
                   jax.ShapeDtypeStruct((t, GU_COLS), BF16), jax.ShapeDtypeStruct((1, GATE_WIDTH), F32)),
        in_specs=[col(0), col(1), vec(0), vec(1), col(0), col(0), col(0)],
        out_specs=(col(0), col(0), wide, pl.BlockSpec((1, GATE_WIDTH), lambda i: (0, 0))),
        compiler_params=_cparams(("arbitrary",)),
    )(gu, gu, b_gate, b_gate, y_pool, y_fox, dmix)


X_SCALE = X_DH ** -0.5


def _xattn_probs(qh, kh):
    s = lax.dot_general(qh, kh, NT_DIMS, preferred_element_type=F32) * X_SCALE
    e = jnp.exp(s - jnp.max(s, axis=-1, keepdims=True))
    return e / jnp.sum(e, axis=-1, keepdims=True)


def _xattn_fwd(q, kv):
    b, s, _ = q.shape
    m = kv.shape[1]
    bq = _block(s, 512, 16)

    def body(q_ref, kv_ref, o_ref):
        for h in range(X_HEADS):
            cols = slice(h * X_DH, (h + 1) * X_DH)
            p = _xattn_probs(q_ref[0, :, cols], kv_ref[0, :, cols])
            vh = kv_ref[0, :, X_WIDTH + h * X_DH:X_WIDTH + (h + 1) * X_DH]
            o_ref[0, :, cols] = jnp.dot(p.astype(BF16), vh, preferred_element_type=F32).astype(BF16)

    return pl.pallas_call(
        body, name="xattn_fwd", out_shape=jax.ShapeDtypeStruct((b, s, X_WIDTH), BF16), grid=(b, s // bq),
        in_specs=[pl.BlockSpec((1, bq, X_WIDTH), lambda i, j: (i, j, 0)),
                  pl.BlockSpec((1, m, 2 * X_WIDTH), lambda i, j: (i, 0, 0))],
        out_specs=pl.BlockSpec((1, bq, X_WIDTH), lambda i, j: (i, j, 0)),
        compiler_params=_cparams(("parallel", "parallel")),
    )(q, kv)


def _xattn_bwd(q, kv, do):
    b, s, _ = q.shape
    m = kv.shape[1]
    bq = _block(s, 512, 16)

    def body(q_ref, kv_ref, do_ref, dq_ref, dkv_ref):
        @pl.when(pl.program_id(1) == 0)
        def _():
            dkv_ref[...] = jnp.zeros_like(dkv_ref)

        for h in range(X_HEADS):
            cols = slice(h * X_DH, (h + 1) * X_DH)
            vcols = slice(X_WIDTH + h * X_DH, X_WIDTH + (h + 1) * X_DH)
            qh, kh, vh, doh = q_ref[0, :, cols], kv_ref[0, :, cols], kv_ref[0, :, vcols], do_ref[0, :, cols]
            p = _xattn_probs(qh, kh)
            dkv_ref[0, :, vcols] += lax.dot_general(p.astype(BF16), doh, TN_DIMS, preferred_element_type=F32)
            dp = lax.dot_general(doh, vh, NT_DIMS, preferred_element_type=F32)
            ds = (p * (dp - jnp.sum(p * dp, axis=-1, keepdims=True)) * X_SCALE).astype(BF16)
            dq_ref[0, :, cols] = jnp.dot(ds, kh, preferred_element_type=F32).astype(BF16)
            dkv_ref[0, :, cols] += lax.dot_general(ds, qh, TN_DIMS, preferred_element_type=F32)

    tile = pl.BlockSpec((1, bq, X_WIDTH), lambda i, j: (i, j, 0))
    mem = pl.BlockSpec((1, m, 2 * X_WIDTH), lambda i, j: (i, 0, 0))
    return pl.pallas_call(
        body, name="xattn_bwd", grid=(b, s // bq),
        out_shape=(jax.ShapeDtypeStruct((b, s, X_WIDTH), BF16), jax.ShapeDtypeStruct((b, m, 2 * X_WIDTH), F32)),
        in_specs=[tile, mem, tile], out_specs=(tile, mem),
        compiler_params=_cparams(("parallel", "arbitrary")),
    )(q, kv, do)


def _swiglu_fwd(gu):
    t = gu.shape[0]
    bt = _block(t, 256, 16)

    def body(gt_ref, up_ref, o_ref):
        gt = gt_ref[...].astype(F32)
        o_ref[...] = (gt * _sigmoid(gt) * up_ref[...].astype(F32)).astype(BF16)

    col = lambda j: pl.BlockSpec((bt, D_FF), lambda i: (i, j))
    return pl.pallas_call(
        body, name="swiglu_fwd", out_shape=jax.ShapeDtypeStruct((t, D_FF), BF16), grid=(t // bt,),
        in_specs=[col(0), col(1)], out_specs=col(0),
        compiler_params=_cparams(("parallel",)),
    )(gu, gu)


def _swiglu_bwd(gu, dact):
    t = gu.shape[0]
    bt = _block(t, 256, 16)

    def body(gt_ref, up_ref, da_ref, o_ref):
        gt = gt_ref[...].astype(F32)
        da = da_ref[...].astype(F32)
        sg = _sigmoid(gt)
        silu = gt * sg
        o_ref[:, :D_FF] = (da * up_ref[...].astype(F32) * (sg + silu * (1.0 - sg))).astype(BF16)
        o_ref[:, D_FF:] = (da * silu).astype(BF16)

    col = lambda j: pl.BlockSpec((bt, D_FF), lambda i: (i, j))
    return pl.pallas_call(
        body, name="swiglu_bwd", out_shape=jax.ShapeDtypeStruct((t, 2 * D_FF), BF16), grid=(t // bt,),
        in_specs=[col(0), col(1), col(0)], out_specs=pl.BlockSpec((bt, 2 * D_FF), lambda i: (i, 0)),
        compiler_params=_cparams(("parallel",)),
    )(gu, gu, dact)


def _stack_of(w, axis):
    r, c = w.shape
    if axis == 0:
        return w.reshape(N_CHIPS, r // N_CHIPS, c)
    return w.reshape(r, N_CHIPS, c // N_CHIPS).transpose(1, 0, 2)


def _stack_t(w3):
    n, r, c = w3.shape
    return w3.transpose(0, 2, 1).reshape(n * c, r)


def _pair_rows(k, v):
    c = k.shape[1]
    return jnp.stack([k.reshape(PAIRS, LANES, c), v.reshape(PAIRS, LANES, c)], axis=1).reshape(2 * FOX_WIDTH, c)


def _unpair_rows(kv):
    c = kv.shape[1]
    kv = kv.reshape(PAIRS, 2, LANES, c)
    return kv[:, 0].reshape(FOX_WIDTH, c), kv[:, 1].reshape(FOX_WIDTH, c)


def _input_grad(parts, weights_t, ex):
    t = parts[0].shape[0]
    d = weights_t[0].shape[1]
    bm = _block(t, 512, 16)
    n = len(parts)

    def body(*refs):
        acc = None
        for a_ref, b_ref in zip(refs[:n], refs[n:2 * n]):
            term = jnp.dot(a_ref[...], b_ref[...], preferred_element_type=F32)
            acc = term if acc is None else acc + term
        refs[2 * n][...] = acc

    (out,), moved = _hosted_call(
        body, ex, name="d_h", grid=(t // bm,), out_shape=(jax.ShapeDtypeStruct((t, d), F32),),
        in_specs=[pl.BlockSpec((bm, p.shape[1]), lambda i: (i, 0)) for p in parts]
        + [pl.BlockSpec(w.shape, lambda i: (0, 0)) for w in weights_t],
        out_specs=(pl.BlockSpec((bm, d), lambda i: (i, 0)),), args=tuple(parts) + tuple(weights_t))
    return out, moved


def _step(x, mem, loss_target, weights, moments_m, moments_v):
    nb, s, d = x.shape
    n_mem = mem.shape[1]
    t = nb * s
    blk = min(ATT_BLOCK, s)
    x2 = x.reshape(t, d)
    mem2 = mem.reshape(nb * n_mem, d)
    tgt2 = loss_target.reshape(t, d)

    def shard2d(a, n):
        a = a.reshape(a.shape[1:])
        return a.T if n == "w_in" else a

    def unshard(a, n):
        return (a.T if n == "w_in" else a)[None]

    local = {n: shard2d(weights[n], n) for n, _, _ in SHARDED}

    names = [n for n, _, _ in SHARDED]
    later = [n for n in names if n != "w_in"]
    local_b = {n: local[n].astype(BF16) for n in names}
    g_mix = weights["norm_mix_g"]
    h, w_in_others = _rms_fwd(x2, g_mix, "norm_mix", ex=_gather_exchange([local_b["w_in"]]))
    w_in_stack, = _place_own(w_in_others, [local_b["w_in"]])

    def w_in_rows(lo, hi):
        per = IN_COLS // N_CHIPS
        parts = [w_in_stack[j, max(lo, j * per) - j * per:min(hi, (j + 1) * per) - j * per]
                 for j in range(N_CHIPS) if max(lo, j * per) < min(hi, (j + 1) * per)]
        return parts[0] if len(parts) == 1 else jnp.concatenate(parts)

    w_gu_t = jnp.concatenate([w_in_rows(2056, IN_COLS), w_in_rows(0, 512)])
    w_qkv_t = jnp.concatenate([w_in_rows(512, 1024), _pair_rows(w_in_rows(1024, 1536), w_in_rows(1536, 2048))])
    w_f_t = jnp.pad(w_in_rows(2048, 2056), ((0, LANES - FOX_HEADS), (0, 0)))
    w_gu, w_qkv, w_f = w_gu_t.T, w_qkv_t.T, w_f_t.T
    w_f_exp = jnp.repeat(w_f[:, :FOX_HEADS], FOX_DH, axis=1)

    g_mix, g_x, g_mem, g_ffn = (weights[n] for n in ("norm_mix_g", "norm_x_g", "norm_mem_g", "norm_ffn_g"))
    g_final = weights["norm_final_g"].reshape(1, d)
    pool_w = weights["pool_w"].reshape(4, POOL_GC, POOL_GC)
    pool_scale, b_gate = weights["pool_scale"], weights["b_gate"]
    b_f_pad = jnp.pad(weights["b_forget"], ((0, 0), (0, LANES - FOX_HEADS)))
    b_f_exp = jnp.repeat(weights["b_forget"], FOX_DH, axis=1)

    gu = _mm(h, w_gu, out_dtype=BF16, bn=512, name="in_proj_gates_pool")
    qkv = _mm(h, w_qkv, out_dtype=BF16, bn=512, name="in_proj_qkv")
    f_pad = _mm(h, w_f, name="in_proj_forget")
    gu3, qkv3 = gu.reshape(nb, s, GU_COLS), qkv.reshape(nb, s, 3 * FOX_WIDTH)
    y = _pool_fwd(gu3, pool_w, pool_scale)
    f_exp = _mm(h, w_f_exp, name="in_proj_forget_lanes").reshape(nb, s, FOX_WIDTH)
    c_exp = _forget_cumsum(f_exp, b_f_exp, "forget_cumsum_lanes")
    c_pad = _forget_cumsum(f_pad.reshape(nb, s, LANES), b_f_pad, "forget_cumsum")
    c_row = c_pad[:, :, :FOX_HEADS].transpose(0, 2, 1).reshape(nb, FOX_HEADS, s // blk, 1, blk)
    (o, o_b, lse), gathered = _fox_fwd(qkv3, c_exp, c_row, ex=_gather_exchange([local_b[n] for n in later]))
    stacks = dict(zip(later, _place_own(gathered, [local_b[n] for n in later])))
    w_pool_out3, w_fox_out3, w_xo3, w_ffn_in3 = (stacks[n] for n in ("w_pool_out", "w_fox_out", "w_xo", "w_ffn_in"))
    w_out, w_xq, w_xkv, w_ffn_out = (stacks[n].reshape(-1, stacks[n].shape[2])
                                     for n in ("w_out", "w_xq", "w_xkv", "w_ffn_out"))
    y2, o2 = y.reshape(t, POOL_WIDTH), o_b.reshape(t, FOX_WIDTH)
    y_pool = _mm(y2, w_pool_out3, b_stack=True, out_dtype=BF16, name="pool_out")
    y_fox = _mm(o2, w_fox_out3, b_stack=True, out_dtype=BF16, name="fox_out")
    mix = _mix_fwd(gu, b_gate, y_pool, y_fox)
    x1 = _mm(mix, w_out, res=x2, name="mix_out")
    hx = _rms_fwd(x1, g_x, "norm_x")
    mem_n = _rms_fwd(mem2, g_mem, "norm_mem")
    qx = _mm(hx, w_xq, out_dtype=BF16, name="x_q")
    kv = _mm(mem_n, w_xkv, out_dtype=BF16, name="x_kv")
    qx3, kv3 = qx.reshape(nb, s, X_WIDTH), kv.reshape(nb, n_mem, 2 * X_WIDTH)
    ox = _xattn_fwd(qx3, kv3).reshape(t, X_WIDTH)
    x2_ = _mm(ox, w_xo3, b_stack=True, res=x1, name="x_out")
    hf = _rms_fwd(x2_, g_ffn, "norm_ffn")
    ffn = _mm(hf, w_ffn_in3, b_stack=True, out_dtype=BF16, bm=2048, bn=1408, name="ffn_in")
    act = _swiglu_fwd(ffn)
    x3 = _mm(act, w_ffn_out, res=x2_, name="ffn_out")

    dx3, dx3_b, dg_final, loss_part = _final_loss(x3, tgt2, g_final)
    dw_ffn_out = _mm(act, dx3_b, ta=True, bm=1408, bn=512, bk=2048, name="d_w_ffn_out")
    dact = _mm(dx3_b, w_ffn_out.T, out_dtype=BF16, bm=2048, bn=1408, name="d_act")
    dffn = _swiglu_bwd(ffn, dact)
    dw_ffn_in = _mm(hf, dffn, ta=True, bm=512, bn=1408, bk=2048, out_stack=True, name="d_w_ffn_in")
    core = lax.axis_index("c").astype(jnp.int32).reshape(1)
    ffn_group = ["w_ffn_in", "w_ffn_out"]
    mid_group = ["w_pool_out", "w_fox_out", "w_out", "w_xq", "w_xkv", "w_xo"]
    grad_stacks = {"w_ffn_in": dw_ffn_in, "w_ffn_out": _stack_of(dw_ffn_out, 0)}

    def presum(group, theirs):
        return [_sum_halves(grad_stacks[n], t_, core, "sum_halves_" + n) for n, t_ in zip(group, theirs)]

    dhf, theirs = _mm(dffn, _stack_t(w_ffn_in3), bk=2816, name="d_hf",
                      ex=_swap_exchange([grad_stacks[n] for n in ffn_group]))
    chip_sums = dict(zip(ffn_group, presum(ffn_group, theirs)))
    dx2, dx2_b, dg_ffn = _rms_bwd(dhf, x2_, g_ffn, dx3, "norm_ffn_bwd")

    dw_xo = _mm(ox, dx2_b, ta=True, bn=256, out_stack=True, name="d_w_xo")
    dox = _mm(dx2_b, _stack_t(w_xo3), out_dtype=BF16, name="d_ox").reshape(nb, s, X_WIDTH)
    dqx, dkv = _xattn_bwd(qx3, kv3, dox)
    dqx2, dkv2 = dqx.reshape(t, X_WIDTH), dkv.reshape(nb * n_mem, 2 * X_WIDTH)
    dw_xkv = _mm(mem_n, dkv2, ta=True, name="d_w_xkv")
    dmem_n = _mm(dkv2, w_xkv.T, name="d_mem_n")
    dg_mem = _rms_bwd(dmem_n, mem2, g_mem, None, "norm_mem_bwd")
    dw_xq = _mm(hx, dqx2, ta=True, name="d_w_xq")
    dhx = _mm(dqx2, w_xq.T, name="d_hx")
    dx1, dx1_b, dg_x = _rms_bwd(dhx, x1, g_x, dx2, "norm_x_bwd")

    dw_out = _mm(mix, dx1_b, ta=True, name="d_w_out")
    dmix = _mm(dx1_b, w_out.T, name="d_mix")
    dyp, dyf, dgu, db_gate = _mix_bwd(gu, b_gate, y_pool, y_fox, dmix)
    dw_pool_out = _mm(y2, dyp, ta=True, bn=256, out_stack=True, name="d_w_pool_out")
    dw_fox_out = _mm(o2, dyf, ta=True, bn=256, out_stack=True, name="d_w_fox_out")
    dy = _mm(dyp, _stack_t(w_pool_out3), name="d_y").reshape(nb, s, POOL_WIDTH)
    do = _mm(dyf, _stack_t(w_fox_out3), name="d_o").reshape(nb, s, FOX_WIDTH)
    dgu3, dpool_w, dpool_scale = _pool_bwd(gu3, dy, pool_w, pool_scale, dgu.reshape(nb, s, GU_COLS))
    grad_stacks.update({"w_pool_out": dw_pool_out, "w_fox_out": dw_fox_out, "w_out": _stack_of(dw_out, 0),
                        "w_xq": _stack_of(dw_xq, 0), "w_xkv": _stack_of(dw_xkv, 0), "w_xo": dw_xo})
    dgu2 = dgu3.reshape(t, GU_COLS)
    dw_gu_t, theirs = _mm(dgu2, h, ta=True, name="d_w_gates_pool",
                          ex=_swap_exchange([grad_stacks[n] for n in mid_group]))
    chip_sums.update(zip(mid_group, presum(mid_group, theirs)))
    early = ffn_group + mid_group
    (dq3, dkv3, dc_q, dc_row), early_slots = _fox_bwd(qkv3, c_exp, c_row, lse, o, do,
                                                      ex=_chips_exchange([chip_sums[n] for n in early]))
    slots = dict(zip(early, early_slots))
    dc = dc_row.reshape(nb, FOX_HEADS, s).transpose(0, 2, 1) + dc_q.transpose(0, 2, 1, 3).reshape(nb, s, FOX_HEADS)
    dc = jnp.pad(dc, ((0, 0), (0, 0), (0, LANES - FOX_HEADS)))
    df, db_f = _forget_bwd(dc, f_pad.reshape(nb, s, LANES), b_f_pad)
    dq2, dkv2, df2 = dq3.reshape(t, FOX_WIDTH), dkv3.reshape(t, 2 * FOX_WIDTH), df.reshape(t, LANES)
    dw_q_t = _mm(dq2, h, ta=True, name="d_w_q")
    dw_kv_t = _mm(dkv2, h, ta=True, name="d_w_kv")
    dw_f_t = _mm(df2, h, ta=True, name="d_w_forget")
    dw_k_t, dw_v_t = _unpair_rows(dw_kv_t)
    pieces = [dw_gu_t[GATE_WIDTH:], dw_q_t, dw_k_t, dw_v_t, dw_f_t[:FOX_HEADS], dw_gu_t[:GATE_WIDTH]]
    per, slabs, first = IN_COLS // N_CHIPS, [], 0
    bounds = []
    for p in pieces:
        bounds.append((first, first + p.shape[0], p))
        first += p.shape[0]
    for j in range(N_CHIPS):
        lo, hi = j * per, (j + 1) * per
        slabs.append(jnp.concatenate([p[max(lo, a) - a:min(hi, b) - a] for a, b, p in bounds if max(lo, a) < min(hi, b)]))
    grad_stacks["w_in"] = jnp.stack(slabs)
    chip_sums["w_in"], = presum(["w_in"], _run_exchange(_swap_exchange([grad_stacks["w_in"]]), "swap_halves_w_in"))
    dh, (slots["w_in"],) = _input_grad([dgu2, dq2, dkv2, df2],
                                       [w_gu_t, w_qkv_t[:FOX_WIDTH], w_qkv_t[FOX_WIDTH:], w_f_t],
                                       _chips_exchange([chip_sums["w_in"]]))

    place = jnp.stack([lax.axis_index("c"), 2 * lax.axis_index("x") + lax.axis_index("y")]).astype(jnp.int32)
    halves = [_sum_chips(slots[n], chip_sums[n], place, _by_rows(local[n].shape[0]), "sum_chips_" + n) for n in names]
    (dx, _, dg_mix), reduced = _rms_bwd(dh, x2, g_mix, dx1, "norm_mix_bwd", ex=_join_exchange(halves))

    small_grads = {"norm_mix_g": dg_mix, "b_forget": db_f[:, :FOX_HEADS], "b_gate": db_gate, "pool_w": dpool_w,
                   "pool_scale": dpool_scale, "norm_x_g": dg_x, "norm_mem_g": dg_mem, "norm_ffn_g": dg_ffn,
                   "norm_final_g": dg_final}
    def flat2d(a):
        return a.reshape(-1, a.shape[-1])

    small_names = [n for n, _ in SMALL]
    own = [flat2d(small_grads[n]) for n in small_names]
    small_gather = _small_exchange(own + [loss_part])

    def tiles_of(a):
        return a.transpose(2, 0, 1)

    def block_of(a3):
        return a3.transpose(1, 2, 0)

    grads, deltas, new_m, new_v = {}, {}, {}, {}
    gathered = None
    for n, g_ in zip(names, reduced):
        if n == "w_in":
            g_ = lax.optimization_barrier(g_.reshape(IN_COLS // N_CHIPS, 1, D_MODEL))
            (d_, m_, v_), gathered = _adamw(tiles_of(weights[n]), g_, tiles_of(moments_m[n]), tiles_of(moments_v[n]),
                                            "adamw_" + n, ex=small_gather)
            back = block_of
        else:
            d_, m_, v_ = _adamw(local[n], g_, shard2d(moments_m[n], n), shard2d(moments_v[n], n), "adamw_" + n)
            back = functools.partial(unshard, n=n)
        grads[n], deltas[n], new_m[n], new_v[n] = (back(a) for a in (g_, d_, m_, v_))

    device = (4 * lax.axis_index("x") + 2 * lax.axis_index("y") + lax.axis_index("c")).astype(jnp.int32).reshape(1)
    sg, sd, sm, sv, loss_sum = _adamw_small(
        gathered[:-1], own, [flat2d(weights[n]) for n in small_names], [flat2d(moments_m[n]) for n in small_names],
        [flat2d(moments_v[n]) for n in small_names], gathered[-1], loss_part, device)
    for n, g_, d_, m_, v_ in zip(small_names, sg, sd, sm, sv):
        grads[n], deltas[n], new_m[n], new_v[n] = (a.reshape(weights[n].shape) for a in (g_, d_, m_, v_))
    return loss_sum[0, 0], dx.reshape(nb, s, d), grads, deltas, new_m, new_v


def kernel(x, mem, norm_mix_g, w_in, b_forget, b_gate, pool_w, pool_scale, w_pool_out, w_fox_out, w_out, norm_x_g, norm_mem_g, w_xq, w_xkv, w_xo, norm_ffn_g, w_ffn_in, w_ffn_out, norm_final_g, loss_target, m_norm_mix_g, m_w_in, m_b_forget, m_b_gate, m_pool_w, m_pool_scale, m_w_pool_out, m_w_fox_out, m_w_out, m_norm_x_g, m_norm_mem_g, m_w_xq, m_w_xkv, m_w_xo, m_norm_ffn_g, m_w_ffn_in, m_w_ffn_out, m_norm_final_g, v_norm_mix_g, v_w_in, v_b_forget, v_b_gate, v_pool_w, v_pool_scale, v_w_pool_out, v_w_fox_out, v_w_out, v_norm_x_g, v_norm_mem_g, v_w_xq, v_w_xkv, v_w_xo, v_norm_ffn_g, v_w_ffn_in, v_w_ffn_out, v_norm_final_g):
    given = dict(locals())
    weights = {n: given[n] for n in WEIGHT_ORDER}
    moments_m = {n: given["m_" + n] for n in WEIGHT_ORDER}
    moments_v = {n: given["v_" + n] for n in WEIGHT_ORDER}
    loss, grad_x, grads, deltas, new_m, new_v = _step(x, mem, loss_target, weights, moments_m, moments_v)
    return (loss, grad_x, *[grads[n] for n in WEIGHT_ORDER], *[deltas[n] for n in WEIGHT_ORDER],
            *[new_m[n] for n in WEIGHT_ORDER], *[new_v[n] for n in WEIGHT_ORDER])
```

```python
import functools
import math

import jax
import jax.numpy as jnp
from jax import lax
from jax.experimental import pallas as pl
from jax.experimental.pallas import tpu as pltpu

F32 = jnp.float32
BF16 = jnp.bfloat16
MESH = pl.DeviceIdType.MESH

D_MODEL = 1024
EPS = 1e-6
POOL_WINDOWS = (2, 4, 8, 16)
POOL_WIDTH = 512
POOL_GC = 128
FOX_HEADS = 8
FOX_DH = 64
FOX_WIDTH = 512
X_HEADS = 4
X_DH = 128
X_WIDTH = 512
D_FF = 2816
IN_COLS = 4104
GATE_WIDTH = 2048
ADAM_LR = 0.001
ADAM_B1 = 0.9
ADAM_B2 = 0.999
ADAM_EPS = 1e-08
ADAM_WD = 0.01
ADAM_STEP = 10

N_CHIPS = 4
N_DEV = 8
LANES = 128
VMEM_LIMIT_BYTES = 56 * 1024 * 1024
NEG_INF = -1e30
ATT_BLOCK = 512

SHARDED = (
    ("w_in", (1024, IN_COLS), 1),
    ("w_pool_out", (POOL_WIDTH, 1024), 1),
    ("w_fox_out", (FOX_WIDTH, 1024), 1),
    ("w_out", (1024, 1024), 0),
    ("w_xq", (1024, X_WIDTH), 0),
    ("w_xkv", (1024, 2 * X_WIDTH), 0),
    ("w_xo", (X_WIDTH, 1024), 1),
    ("w_ffn_in", (1024, 2 * D_FF), 1),
    ("w_ffn_out", (D_FF, 1024), 0),
)
SMALL = (
    ("norm_mix_g", (1, 1024)),
    ("b_forget", (1, 8)),
    ("b_gate", (1, 2048)),
    ("pool_w", (1, 4, 128, 128)),
    ("pool_scale", (1, 512)),
    ("norm_x_g", (1, 1024)),
    ("norm_mem_g", (1, 1024)),
    ("norm_ffn_g", (1, 1024)),
    ("norm_final_g", (1024,)),
)
WEIGHT_ORDER = ("norm_mix_g", "w_in", "b_forget", "b_gate", "pool_w", "pool_scale", "w_pool_out", "w_fox_out", "w_out",
                "norm_x_g", "norm_mem_g", "w_xq", "w_xkv", "w_xo", "norm_ffn_g", "w_ffn_in", "w_ffn_out", "norm_final_g")


def _cparams(sem=None):
    return pltpu.CompilerParams(dimension_semantics=sem, vmem_limit_bytes=VMEM_LIMIT_BYTES)


def _block(dim, pref, unit):
    if dim <= pref:
        return dim
    best = None
    for b in range(unit, pref + 1, unit):
        if dim % b == 0:
            best = b
    assert best is not None, (dim, pref, unit)
    return best


def _rows_block(rows, cols, unit=16, elems=1 << 19):
    return _block(rows, max(unit, elems // cols // unit * unit), unit)


def _my_place():
    return lax.axis_index("x"), lax.axis_index("y"), lax.axis_index("c")


def _other_chips(x, y):
    return [(1 - x, y), (x, 1 - y), (1 - x, 1 - y)]


def _chip(place):
    return 2 * place[0] + place[1]


ANY = pl.BlockSpec(memory_space=pl.ANY)


def _by_rows(rows):
    return rows % 32 == 0


def _half_shape(rows, cols):
    return (rows // 2, cols) if _by_rows(rows) else (rows, cols // 2)


def _core_half(ref, core, lead=()):
    rows, cols = ref.shape[-2:]
    if _by_rows(rows):
        return ref.at[(*lead, pl.ds(core * (rows // 2), rows // 2), slice(None))]
    return ref.at[(*lead, slice(None), pl.ds(core * (cols // 2), cols // 2))]


class _Exchange:
    def __init__(self, arrays, out_shapes, n_sems, start, finish, in_place=False):
        self.arrays, self.out_shapes, self.n_sems, self.start, self.finish = arrays, out_shapes, n_sems, start, finish
        self.in_place = in_place

    def scratch(self):
        return [pltpu.SemaphoreType.DMA((self.n_sems,)), pltpu.SemaphoreType.DMA((self.n_sems,))]

    def aliases(self, first_in, first_out):
        return {first_in + k: first_out + k for k in range(len(self.arrays))} if self.in_place else {}


def _run_exchange(ex, name):
    n = len(ex.arrays)

    def body(*refs):
        ins, outs, sems = refs[:n], refs[n:2 * n], refs[2 * n:]
        ex.start(ins, outs, *sems)
        ex.finish(ins, outs, *sems)

    return pl.pallas_call(
        body, name=name, out_shape=ex.out_shapes, in_specs=[ANY] * n, out_specs=[ANY] * n, scratch_shapes=ex.scratch(),
        input_output_aliases=ex.aliases(0, 0),
    )(*ex.arrays)


def _hosted_call(body, ex, *, name, grid, in_specs, out_specs, out_shape, args, scratch=()):
    n_in, n_out, n_scr = len(args), len(out_shape), len(scratch)
    if ex is None:
        outs = pl.pallas_call(
            body, name=name, grid=grid, out_shape=out_shape, in_specs=in_specs, out_specs=out_specs,
            scratch_shapes=list(scratch), compiler_params=_cparams(("arbitrary",) * len(grid)))(*args)
        return outs, None
    nc = len(ex.arrays)

    def full_body(*refs):
        ins, cins = refs[:n_in], refs[n_in:n_in + nc]
        outs, couts = refs[n_in + nc:n_in + nc + n_out], refs[n_in + nc + n_out:n_in + 2 * nc + n_out]
        rest = refs[n_in + 2 * nc + n_out:]
        scr, sems = rest[:n_scr], rest[n_scr:]
        first = functools.reduce(jnp.logical_and, [pl.program_id(a) == 0 for a in range(len(grid))])
        last = functools.reduce(jnp.logical_and, [pl.program_id(a) == grid[a] - 1 for a in range(len(grid))])

        @pl.when(first)
        def _():
            ex.start(cins, couts, *sems)

        body(*ins, *outs, *scr)

        @pl.when(last)
        def _():
            ex.finish(cins, couts, *sems)

    outs = pl.pallas_call(
        full_body, name=name, grid=grid, out_shape=list(out_shape) + list(ex.out_shapes),
        in_specs=list(in_specs) + [ANY] * nc, out_specs=list(out_specs) + [ANY] * nc,
        scratch_shapes=list(scratch) + ex.scratch(), input_output_aliases=ex.aliases(n_in, n_out),
        compiler_params=_cparams(("arbitrary",) * len(grid)))(*args, *ex.arrays)
    return outs[:n_out], outs[n_out:]


def _gather_exchange(shards):
    n = len(shards)

    def copies(ins, outs, send_sems, recv_sems):
        x, y, c = _my_place()

        def half(k, chip, core):
            return _core_half(outs[k], core, lead=(_chip(chip),))

        def copy(k, slot, chip, core, to, src=None):
            return pltpu.make_async_remote_copy(
                src_ref=half(k, chip, core) if src is None else src, dst_ref=half(k, chip, core),
                send_sem=send_sems.at[6 * k + slot], recv_sem=recv_sems.at[6 * k + slot],
                device_id=to, device_id_type=MESH)

        return (x, y, c), copy

    def first_copies(ins, outs, send_sems, recv_sems):
        (x, y, c), copy = copies(ins, outs, send_sems, recv_sems)
        out = []
        for j, chip in enumerate(_other_chips(x, y)):
            for k in range(n):
                out.append(copy(k, j, (x, y), c, (*chip, c), src=_core_half(ins[k], c)))
        return out

    def start(ins, outs, send_sems, recv_sems):
        for cp in first_copies(ins, outs, send_sems, recv_sems):
            cp.start()

    def finish(ins, outs, send_sems, recv_sems):
        (x, y, c), copy = copies(ins, outs, send_sems, recv_sems)
        chips = _other_chips(x, y)
        passed = []
        for j, chip in enumerate(chips):
            for k in range(n):
                copy(k, j, chip, c, (x, y, c)).wait_recv()
                passed.append(copy(k, 3 + j, chip, c, (x, y, 1 - c)))
                passed[-1].start()
        for j, chip in enumerate(chips):
            for k in range(n):
                copy(k, 3 + j, chip, 1 - c, (x, y, c)).wait_recv()
        for cp in first_copies(ins, outs, send_sems, recv_sems) + passed:
            cp.wait_send()

    return _Exchange(list(shards), [jax.ShapeDtypeStruct((N_CHIPS,) + s.shape, s.dtype) for s in shards], 6 * n,
                     start, finish)


def _place_own(stacks, shards):
    me = 2 * lax.axis_index("x") + lax.axis_index("y")
    return [lax.dynamic_update_slice(others, mine[None], (me, 0, 0)) for others, mine in zip(stacks, shards)]


def _swap_exchange(grads):
    n = len(grads)

    def copies(ins, outs, send_sems, recv_sems):
        x, y, c = _my_place()
        return [pltpu.make_async_remote_copy(
            src_ref=_core_half(ins[k], 1 - c, lead=(slice(None),)), dst_ref=outs[k],
            send_sem=send_sems.at[k], recv_sem=recv_sems.at[k], device_id=(x, y, 1 - c), device_id_type=MESH)
            for k in range(n)]

    def start(ins, outs, send_sems, recv_sems):
        for cp in copies(ins, outs, send_sems, recv_sems):
            cp.start()

    def finish(ins, outs, send_sems, recv_sems):
        for cp in copies(ins, outs, send_sems, recv_sems):
            cp.wait()

    return _Exchange(list(grads), [jax.ShapeDtypeStruct((N_CHIPS,) + _half_shape(*g.shape[1:]), g.dtype) for g in grads],
                     n, start, finish)


def _chips_exchange(sums):
    n = len(sums)

    def sends(ins, outs, send_sems, recv_sems):
        x, y, c = _my_place()
        return [pltpu.make_async_remote_copy(
            src_ref=ins[k].at[_chip(chip)], dst_ref=outs[k].at[_chip((x, y))],
            send_sem=send_sems.at[3 * k + j], recv_sem=recv_sems.at[3 * k + j],
            device_id=(*chip, c), device_id_type=MESH)
            for j, chip in enumerate(_other_chips(x, y)) for k in range(n)]

    def start(ins, outs, send_sems, recv_sems):
        for cp in sends(ins, outs, send_sems, recv_sems):
            cp.start()

    def finish(ins, outs, send_sems, recv_sems):
        x, y, c = _my_place()
        for j, chip in enumerate(_other_chips(x, y)):
            for k in range(n):
                slot = outs[k].at[_chip(chip)]
                pltpu.make_async_remote_copy(
                    src_ref=slot, dst_ref=slot, send_sem=send_sems.at[3 * k + j], recv_sem=recv_sems.at[3 * k + j],
                    device_id=(x, y, c), device_id_type=MESH).wait_recv()
        for cp in sends(ins, outs, send_sems, recv_sems):
            cp.wait_send()

    return _Exchange(list(sums), [jax.ShapeDtypeStruct(s.shape, s.dtype) for s in sums], 3 * n, start, finish)


def _join_exchange(shards):
    n = len(shards)

    def sends(ins, outs, send_sems, recv_sems):
        x, y, c = _my_place()
        return [pltpu.make_async_remote_copy(
            src_ref=_core_half(ins[k], c), dst_ref=_core_half(outs[k], c),
            send_sem=send_sems.at[k], recv_sem=recv_sems.at[k], device_id=(x, y, 1 - c), device_id_type=MESH)
            for k in range(n)]

    def start(ins, outs, send_sems, recv_sems):
        for cp in sends(ins, outs, send_sems, recv_sems):
            cp.start()

    def finish(ins, outs, send_sems, recv_sems):
        x, y, c = _my_place()
        for k in range(n):
            theirs = _core_half(outs[k], 1 - c)
            pltpu.make_async_remote_copy(
                src_ref=theirs, dst_ref=theirs, send_sem=send_sems.at[k], recv_sem=recv_sems.at[k],
                device_id=(x, y, c), device_id_type=MESH).wait_recv()
        for cp in sends(ins, outs, send_sems, recv_sems):
            cp.wait_send()

    return _Exchange(list(shards), [jax.ShapeDtypeStruct(s.shape, s.dtype) for s in shards], n, start, finish,
                     in_place=True)


def _small_exchange(blocks):
    n = len(blocks)

    def copies(ins, outs, send_sems, recv_sems):
        x, y, c = _my_place()

        def copy(k, j, whose, to, src=None):
            slot = outs[k].at[4 * whose[0] + 2 * whose[1] + whose[2]]
            return pltpu.make_async_remote_copy(
                src_ref=slot if src is None else src, dst_ref=slot,
                send_sem=send_sems.at[7 * k + j], recv_sem=recv_sems.at[7 * k + j], device_id=to, device_id_type=MESH)

        return (x, y, c), copy

    def first_copies(ins, outs, send_sems, recv_sems):
        (x, y, c), copy = copies(ins, outs, send_sems, recv_sems)
        out = []
        for k in range(n):
            out.append(copy(k, 0, (x, y, c), (x, y, 1 - c), src=ins[k]))
            out += [copy(k, 1 + j, (x, y, c), (*chip, c), src=ins[k]) for j, chip in enumerate(_other_chips(x, y))]
        return out

    def start(ins, outs, send_sems, recv_sems):
        for cp in first_copies(ins, outs, send_sems, recv_sems):
            cp.start()

    def finish(ins, outs, send_sems, recv_sems):
        (x, y, c), copy = copies(ins, outs, send_sems, recv_sems)
        chips = _other_chips(x, y)
        passed = []
        for j, chip in enumerate(chips):
            for k in range(n):
                copy(k, 1 + j, (*chip, c), (x, y, c)).wait_recv()
                passed.append(copy(k, 4 + j, (*chip, c), (x, y, 1 - c)))
                passed[-1].start()
        for k in range(n):
            copy(k, 0, (x, y, 1 - c), (x, y, c)).wait_recv()
        for j, chip in enumerate(chips):
            for k in range(n):
                copy(k, 4 + j, (*chip, 1 - c), (x, y, c)).wait_recv()
        for cp in first_copies(ins, outs, send_sems, recv_sems) + passed:
            cp.wait_send()

    return _Exchange(list(blocks), [jax.ShapeDtypeStruct((N_DEV,) + blk.shape, blk.dtype) for blk in blocks], 7 * n,
                     start, finish)


def _sum_halves(grads, theirs, core, name):
    _, h, cols = theirs.shape
    by_rows = _by_rows(grads.shape[1])
    br = _rows_block(h, cols) if by_rows else h
    nb = h // br

    def body(core_ref, a_ref, b_ref, o_ref):
        o_ref[...] = (a_ref[...] + b_ref[...]).astype(BF16)

    if by_rows:
        mine = pl.BlockSpec((1, br, cols), lambda j, i, core_ref: (j, core_ref[0] * nb + i, 0))
    else:
        mine = pl.BlockSpec((1, br, cols), lambda j, i, core_ref: (j, i, core_ref[0]))
    return pl.pallas_call(
        body, name=name,
        out_shape=jax.ShapeDtypeStruct(theirs.shape, BF16),
        grid_spec=pltpu.PrefetchScalarGridSpec(
            num_scalar_prefetch=1, grid=(N_CHIPS, nb),
            in_specs=[mine, pl.BlockSpec((1, br, cols), lambda j, i, core_ref: (j, i, 0))],
            out_specs=pl.BlockSpec((1, br, cols), lambda j, i, core_ref: (j, i, 0))),
        compiler_params=_cparams(("parallel", "parallel")),
    )(core, grads, theirs)


def _sum_chips(slots, sums, place, by_rows, name):
    _, h, cols = slots.shape
    br = _rows_block(h, cols) if by_rows else h
    nb = h // br

    def body(place_ref, s_ref, own_ref, o_ref):
        me = place_ref[1]
        acc = None
        for k in range(N_CHIPS):
            term = jnp.where(me == k, own_ref[k], s_ref[k]).astype(F32)
            acc = term if acc is None else acc + term
        o_ref[...] = acc

    stack = pl.BlockSpec((N_CHIPS, br, cols), lambda i, place_ref: (0, i, 0))
    if by_rows:
        out_shape, out_map = (2 * h, cols), lambda i, place_ref: (place_ref[0] * nb + i, 0)
    else:
        out_shape, out_map = (h, 2 * cols), lambda i, place_ref: (i, place_ref[0])
    return pl.pallas_call(
        body, name=name,
        out_shape=jax.ShapeDtypeStruct(out_shape, F32),
        grid_spec=pltpu.PrefetchScalarGridSpec(
            num_scalar_prefetch=1, grid=(nb,), in_specs=[stack, stack],
            out_specs=pl.BlockSpec((br, cols), out_map)),
        compiler_params=_cparams(("parallel",)),
    )(place, slots, sums)


def _adamw_math(w, g, m, v):
    m = ADAM_B1 * m + (1.0 - ADAM_B1) * g
    v = ADAM_B2 * v + (1.0 - ADAM_B2) * (g * g)
    m_hat = m / (1.0 - ADAM_B1 ** ADAM_STEP)
    v_hat = v / (1.0 - ADAM_B2 ** ADAM_STEP)
    delta = -ADAM_LR * (m_hat / (jnp.sqrt(v_hat) + ADAM_EPS) + ADAM_WD * w)
    return delta, m, v


def _adamw(w, g, m, v, name, ex=None):
    def body(w_ref, g_ref, m_ref, v_ref, d_ref, nm_ref, nv_ref):
        d, nm, nv = _adamw_math(w_ref[...], g_ref[...], m_ref[...], v_ref[...])
        d_ref[...] = d
        nm_ref[...] = nm
        nv_ref[...] = nv

    if w.ndim == 3:
        rows = w.shape[0]
        br = max(b for b in range(1, 65) if rows % b == 0)
        spec, steps = pl.BlockSpec((br,) + w.shape[1:], lambda i: (i, 0, 0)), rows // br
    else:
        rows, cols = w.shape
        br = _rows_block(rows, cols, unit=8)
        spec, steps = pl.BlockSpec((br, cols), lambda i: (i, 0)), rows // br
    shape = jax.ShapeDtypeStruct(w.shape, F32)
    outs, moved = _hosted_call(
        body, ex, name=name, out_shape=(shape, shape, shape), grid=(steps,),
        in_specs=[spec] * 4, out_specs=(spec, spec, spec), args=(w, g, m, v))
    return tuple(outs) if ex is None else (tuple(outs), moved)


def _adamw_small(parts, own, ws, ms, vs, loss_parts, loss_own, device):
    n = len(ws)

    def total(device_ref, parts_ref, own_ref):
        acc = None
        for dev in range(N_DEV):
            term = jnp.where(device_ref[0] == dev, own_ref[...], parts_ref[dev])
            acc = term if acc is None else acc + term
        return acc

    def body(device_ref, *refs):
        ins, outs = refs[:5 * n + 2], refs[5 * n + 2:]
        for k in range(n):
            g = total(device_ref, ins[k], ins[n + k])
            d, nm, nv = _adamw_math(ins[2 * n + k][...], g, ins[3 * n + k][...], ins[4 * n + k][...])
            for o_ref, val in zip(outs[k::n][:4], (g, d, nm, nv)):
                o_ref[...] = val
        outs[4 * n][...] = total(device_ref, ins[5 * n], ins[5 * n + 1])

    args = list(parts) + list(own) + list(ws) + list(ms) + list(vs) + [loss_parts, loss_own]
    whole = lambda a: pl.BlockSpec(a.shape, lambda i, device_ref, nd=a.ndim: (0,) * nd)
    shapes = [jax.ShapeDtypeStruct(w.shape, F32) for w in ws] * 4 + [jax.ShapeDtypeStruct(loss_own.shape, F32)]
    outs = pl.pallas_call(
        body, name="adamw_small", out_shape=shapes,
        grid_spec=pltpu.PrefetchScalarGridSpec(
            num_scalar_prefetch=1, grid=(1,), in_specs=[whole(a) for a in args], out_specs=[whole(a) for a in shapes]),
        compiler_params=_cparams(("arbitrary",)),
    )(device, *args)
    return outs[:n], outs[n:2 * n], outs[2 * n:3 * n], outs[3 * n:4 * n], outs[4 * n]


def _mm(a, b, *, name, ta=False, out_dtype=F32, res=None, bm=1024, bn=1024, bk=4096, b_stack=False, out_stack=False,
        ex=None):
    if ta:
        kdim, m = a.shape
    else:
        m, kdim = a.shape
    if b_stack:
        _, kb, chunk = b.shape
        n = N_CHIPS * chunk
    else:
        kb, n = b.shape
        chunk = n // N_CHIPS if out_stack else n
    assert kdim == kb, (a.shape, b.shape, ta)
    bm = _block(m, bm, LANES if ta else 16)
    bn = _block(chunk, bn, LANES)
    bk = _block(kdim, bk, LANES)
    nk = kdim // bk
    per_chunk = chunk // bn
    dims = (((0 if ta else 1,), (0,)), ((), ()))

    def body(*refs):
        refs = list(refs)
        a_ref, b_ref = refs[:2]
        r_ref = refs[2] if res is not None else None
        o_ref = refs[3] if res is not None else refs[2]
        part = lax.dot_general(a_ref[...].astype(BF16), b_ref[...].astype(BF16), dims, preferred_element_type=F32)

        def finish(r):
            if r_ref is not None:
                r = r + r_ref[...]
            o_ref[...] = r.astype(out_dtype)

        if nk == 1:
            finish(part)
        else:
            acc_ref = refs[-1]
            k = pl.program_id(2)

            @pl.when(k == 0)
            def _():
                acc_ref[...] = part

            @pl.when(k > 0)
            def _():
                acc_ref[...] += part

            @pl.when(k == nk - 1)
            def _():
                finish(acc_ref[...])

    a_spec = pl.BlockSpec((bk, bm), lambda i, j, k: (k, i)) if ta else pl.BlockSpec((bm, bk), lambda i, j, k: (i, k))
    if b_stack:
        b_spec = pl.BlockSpec((None, bk, bn), lambda i, j, k: (j // per_chunk, k, j % per_chunk))
    else:
        b_spec = pl.BlockSpec((bk, bn), lambda i, j, k: (k, j))
    r_spec = pl.BlockSpec((bm, bn), lambda i, j, k: (i, j))
    if out_stack:
        o_spec = pl.BlockSpec((None, bm, bn), lambda i, j, k: (j // per_chunk, i, j % per_chunk))
        o_shape = (N_CHIPS, m, chunk)
    else:
        o_spec, o_shape = r_spec, (m, n)
    in_specs = [a_spec, b_spec] + ([r_spec] if res is not None else [])
    args = (a, b) + ((res,) if res is not None else ())
    (out,), moved = _hosted_call(
        body, ex, name=name, out_shape=(jax.ShapeDtypeStruct(o_shape, out_dtype),),
        grid=(m // bm, n // bn, nk), in_specs=in_specs, out_specs=(o_spec,),
        scratch=[pltpu.VMEM((bm, bn), F32)] if nk > 1 else [], args=args)
    return out if ex is None else (out, moved)


def _rms_fwd(x, g, name, ex=None):
    t, d = x.shape
    bt = _block(t, 512, 16)

    def body(x_ref, g_ref, h_ref):
        xv = x_ref[...]
        r = lax.rsqrt(jnp.mean(xv * xv, axis=-1, keepdims=True) + EPS)
        h_ref[...] = (xv * r * g_ref[...]).astype(BF16)

    (out,), moved = _hosted_call(
        body, ex, name=name, out_shape=(jax.ShapeDtypeStruct((t, d), BF16),), grid=(t // bt,),
        in_specs=[pl.BlockSpec((bt, d), lambda i: (i, 0)), pl.BlockSpec((1, d), lambda i: (0, 0))],
        out_specs=(pl.BlockSpec((bt, d), lambda i: (i, 0)),), args=(x, g))
    return out if ex is None else (out, moved)


def _rms_bwd(dh, x, g, dres, name, ex=None):
    t, d = x.shape
    bt = _block(t, 256, 16)
    want_dx = dres is not None

    def body(*refs):
        if want_dx:
            dh_ref, x_ref, g_ref, dres_ref, dx_ref, dxb_ref, dg_ref = refs
        else:
            dh_ref, x_ref, g_ref, dg_ref = refs
        xv = x_ref[...]
        r = lax.rsqrt(jnp.mean(xv * xv, axis=-1, keepdims=True) + EPS)
        xhat = xv * r
        dhv = dh_ref[...]

        @pl.when(pl.program_id(0) == 0)
        def _():
            dg_ref[...] = jnp.zeros_like(dg_ref)

        dg_ref[...] += jnp.sum(dhv * xhat, axis=0, keepdims=True)
        if want_dx:
            dxhat = dhv * g_ref[...]
            dx = dres_ref[...] + r * (dxhat - xhat * jnp.mean(dxhat * xhat, axis=-1, keepdims=True))
            dx_ref[...] = dx
            dxb_ref[...] = dx.astype(BF16)

    row = pl.BlockSpec((bt, d), lambda i: (i, 0))
    vec = pl.BlockSpec((1, d), lambda i: (0, 0))
    if want_dx:
        outs, moved = _hosted_call(
            body, ex, name=name, grid=(t // bt,),
            out_shape=(jax.ShapeDtypeStruct((t, d), F32), jax.ShapeDtypeStruct((t, d), BF16),
                       jax.ShapeDtypeStruct((1, d), F32)),
            in_specs=[row, row, vec, row], out_specs=(row, row, vec), args=(dh, x, g, dres))
        return tuple(outs) if ex is None else (tuple(outs), moved)
    return pl.pallas_call(
        body, name=name, grid=(t // bt,), out_shape=jax.ShapeDtypeStruct((1, d), F32),
        in_specs=[row, row, vec], out_specs=vec,
        compiler_params=_cparams(("arbitrary",)),
    )(dh, x, g)


def _final_loss(x, target, g):
    t, d = x.shape
    bt = _block(t, 256, 16)

    def body(x_ref, t_ref, g_ref, dx_ref, dxb_ref, dg_ref, loss_ref):
        xv = x_ref[...]
        gv = g_ref[...]
        r = lax.rsqrt(jnp.mean(xv * xv, axis=-1, keepdims=True) + EPS)
        xhat = xv * r
        err = xhat * gv - t_ref[...]

        @pl.when(pl.program_id(0) == 0)
        def _():
            dg_ref[...] = jnp.zeros_like(dg_ref)
            loss_ref[...] = jnp.zeros_like(loss_ref)

        loss_ref[...] += 0.5 * jnp.sum(jnp.mean(err * err, axis=-1, keepdims=True), axis=0, keepdims=True)
        dy = err * (1.0 / d)
        dg_ref[...] += jnp.sum(dy * xhat, axis=0, keepdims=True)
        dxhat = dy * gv
        dx = r * (dxhat - xhat * jnp.mean(dxhat * xhat, axis=-1, keepdims=True))
        dx_ref[...] = dx
        dxb_ref[...] = dx.astype(BF16)

    row = pl.BlockSpec((bt, d), lambda i: (i, 0))
    vec = pl.BlockSpec((1, d), lambda i: (0, 0))
    return pl.pallas_call(
        body, name="final_loss", grid=(t // bt,),
        out_shape=(jax.ShapeDtypeStruct((t, d), F32), jax.ShapeDtypeStruct((t, d), BF16),
                   jax.ShapeDtypeStruct((1, d), F32), jax.ShapeDtypeStruct((1, LANES), F32)),
        in_specs=[row, row, vec], out_specs=(row, row, vec, pl.BlockSpec((1, LANES), lambda i: (0, 0))),
        compiler_params=_cparams(("arbitrary",)),
    )(x, target, g)


GU_COLS = GATE_WIDTH + POOL_WIDTH
U_BLK = GATE_WIDTH // POOL_WIDTH


def _shift_down(a, k, row):
    return jnp.where(row >= k, pltpu.roll(a, k, 0), 0.0)


def _shift_up(a, k, row):
    n = a.shape[0]
    return jnp.where(row < n - k, pltpu.roll(a, n - k, 0), 0.0)


def _window_delta(u, w, row):
    s, k = u, 1
    while k < w:
        s = s + _shift_down(s, k, row)
        k *= 2
    cnt = jnp.minimum(row + 1, w).astype(F32)
    return s / cnt - u, cnt


def _pool_fwd(gu, pool_w, pool_scale):
    b, s, _ = gu.shape

    def body(u_ref, pw_ref, sc_ref, y_ref):
        row = lax.broadcasted_iota(jnp.int32, (s, POOL_GC), 0)
        for g, w in enumerate(POOL_WINDOWS):
            cols = slice(g * POOL_GC, (g + 1) * POOL_GC)
            d, _ = _window_delta(u_ref[0, :, cols].astype(F32), w, row)
            z = jnp.dot(d.astype(BF16), pw_ref[g].astype(BF16), preferred_element_type=F32)
            y_ref[0, :, cols] = (z * sc_ref[:, cols]).astype(BF16)

    return pl.pallas_call(
        body, name="pool_fwd", out_shape=jax.ShapeDtypeStruct((b, s, POOL_WIDTH), BF16), grid=(b,),
        in_specs=[pl.BlockSpec((1, s, POOL_WIDTH), lambda i: (i, 0, U_BLK)),
                  pl.BlockSpec((4, POOL_GC, POOL_GC), lambda i: (0, 0, 0)),
                  pl.BlockSpec((1, POOL_WIDTH), lambda i: (0, 0))],
        out_specs=pl.BlockSpec((1, s, POOL_WIDTH), lambda i: (i, 0, 0)),
        compiler_params=_cparams(("parallel",)),
    )(gu, pool_w, pool_scale)


def _pool_bwd(gu, dy, pool_w, pool_scale, dgu):
    b, s, _ = gu.shape

    def body(u_ref, dy_ref, pw_ref, sc_ref, dgu_in, du_ref, dpw_ref, dsc_ref):
        del dgu_in

        @pl.when(pl.program_id(0) == 0)
        def _():
            dpw_ref[...] = jnp.zeros_like(dpw_ref)
            dsc_ref[...] = jnp.zeros_like(dsc_ref)

        row = lax.broadcasted_iota(jnp.int32, (s, POOL_GC), 0)
        for g, w in enumerate(POOL_WINDOWS):
            cols = slice(g * POOL_GC, (g + 1) * POOL_GC)
            d, cnt = _window_delta(u_ref[0, :, cols].astype(F32), w, row)
            db = d.astype(BF16)
            pw = pw_ref[g].astype(BF16)
            z = jnp.dot(db, pw, preferred_element_type=F32)
            dyv = dy_ref[0, :, cols]
            dsc_ref[:, cols] += jnp.sum(dyv * z, axis=0, keepdims=True)
            dz = (dyv * sc_ref[:, cols]).astype(BF16)
            dpw_ref[g] += lax.dot_general(db, dz, (((0,), (0,)), ((), ())), preferred_element_type=F32)
            dd = lax.dot_general(dz, pw, (((1,), (1,)), ((), ())), preferred_element_type=F32)
            acc, k = dd / cnt, 1
            while k < w:
                acc = acc + _shift_up(acc, k, row)
                k *= 2
            du_ref[0, :, cols] = (acc - dd).astype(BF16)

    return pl.pallas_call(
        body, name="pool_bwd", grid=(b,),
        out_shape=(jax.ShapeDtypeStruct((b, s, GU_COLS), BF16), jax.ShapeDtypeStruct((4, POOL_GC, POOL_GC), F32),
                   jax.ShapeDtypeStruct((1, POOL_WIDTH), F32)),
        in_specs=[pl.BlockSpec((1, s, POOL_WIDTH), lambda i: (i, 0, U_BLK)),
                  pl.BlockSpec((1, s, POOL_WIDTH), lambda i: (i, 0, 0)),
                  pl.BlockSpec((4, POOL_GC, POOL_GC), lambda i: (0, 0, 0)),
                  pl.BlockSpec((1, POOL_WIDTH), lambda i: (0, 0)), ANY],
        out_specs=(pl.BlockSpec((1, s, POOL_WIDTH), lambda i: (i, 0, U_BLK)),
                   pl.BlockSpec((4, POOL_GC, POOL_GC), lambda i: (0, 0, 0)),
                   pl.BlockSpec((1, POOL_WIDTH), lambda i: (0, 0))),
        input_output_aliases={4: 0},
        compiler_params=_cparams(("arbitrary",)),
    )(gu, dy, pool_w, pool_scale, dgu)


def _forget_cumsum(f, bias, name):
    b, s, c = f.shape

    def body(f_ref, b_ref, c_ref):
        row = lax.broadcasted_iota(jnp.int32, (s, LANES), 0)
        z = f_ref[0] + b_ref[...]
        acc = jnp.minimum(z, 0.0) - jnp.log(1.0 + jnp.exp(-jnp.abs(z)))
        k = 1
        while k < s:
            acc = acc + _shift_down(acc, k, row)
            k *= 2
        c_ref[0] = acc

    return pl.pallas_call(
        body, name=name, out_shape=jax.ShapeDtypeStruct((b, s, c), F32), grid=(b, c // LANES),
        in_specs=[pl.BlockSpec((1, s, LANES), lambda i, j: (i, 0, j)), pl.BlockSpec((1, LANES), lambda i, j: (0, j))],
        out_specs=pl.BlockSpec((1, s, LANES), lambda i, j: (i, 0, j)),
        compiler_params=_cparams(("parallel", "parallel")),
    )(f, bias)


def _forget_bwd(dc, f, bias):
    b, s, _ = f.shape

    def body(dc_ref, f_ref, b_ref, df_ref, db_ref):
        @pl.when(pl.program_id(0) == 0)
        def _():
            db_ref[...] = jnp.zeros_like(db_ref)

        row = lax.broadcasted_iota(jnp.int32, (s, LANES), 0)
        acc, k = dc_ref[0], 1
        while k < s:
            acc = acc + _shift_up(acc, k, row)
            k *= 2
        z = f_ref[0] + b_ref[...]
        df = acc / (1.0 + jnp.exp(z))
        db_ref[...] += jnp.sum(df, axis=0, keepdims=True)
        df_ref[0] = df.astype(BF16)

    blk = pl.BlockSpec((1, s, LANES), lambda i: (i, 0, 0))
    vec = pl.BlockSpec((1, LANES), lambda i: (0, 0))
    return pl.pallas_call(
        body, name="forget_bwd", grid=(b,),
        out_shape=(jax.ShapeDtypeStruct((b, s, LANES), BF16), jax.ShapeDtypeStruct((1, LANES), F32)),
        in_specs=[blk, blk, vec], out_specs=(blk, vec),
        compiler_params=_cparams(("arbitrary",)),
    )(dc, f, bias)


KV_BLK0 = 2
PAIRS = FOX_HEADS // 2
FOX_SCALE = FOX_DH ** -0.5
NT_DIMS = (((1,), (1,)), ((), ()))
TN_DIMS = (((0,), (0,)), ((), ()))


def _stack_heads(v):
    head = lax.broadcasted_iota(jnp.int32, v.shape, 1) // FOX_DH
    zero = jnp.zeros_like(v)
    return jnp.concatenate([jnp.where(head == 0, v, zero), jnp.where(head == 1, v, zero)], axis=0)


def _stack_cols(v):
    return jnp.concatenate([v[:, 0:1], v[:, FOX_DH:FOX_DH + 1]], axis=0)


def _unstack(t, blk):
    head = lax.broadcasted_iota(jnp.int32, (blk, LANES), 1) // FOX_DH
    return jnp.where(head == 0, t[:blk], t[blk:])


def _fox_scores(q_all, kblk, row_bias, cr_ref, kb, masked, blk):
    top = lax.broadcasted_iota(jnp.int32, (2 * blk, 1), 0) < blk
    s = lax.dot_general(q_all, kblk, NT_DIMS, preferred_element_type=F32)
    s = s + (row_bias - jnp.where(top, cr_ref[0, 0, kb], cr_ref[0, 1, kb]))
    if masked:
        r = lax.broadcasted_iota(jnp.int32, (2 * blk, blk), 0)
        keep = jnp.where(r >= blk, r - blk, r) >= lax.broadcasted_iota(jnp.int32, (2 * blk, blk), 1)
        s = jnp.where(keep, s, NEG_INF)
    return s


def _fox_fwd(qkv, c_exp, c_row, ex=None):
    b, s, _ = qkv.shape
    blk = min(ATT_BLOCK, s)
    nq = s // blk

    def body(q_ref, kv_ref, cc_ref, cr_ref, o_ref, ob_ref, lse_ref):
        qi = pl.program_id(2)
        q_all = _stack_heads(q_ref[0] * FOX_SCALE)
        cq = _stack_cols(cc_ref[0])

        def step(kb, carry, masked):
            m, l, acc = carry
            rows = pl.ds(pl.multiple_of(kb * blk, blk), blk)
            sc = _fox_scores(q_all, kv_ref[0, rows, :LANES], cq, cr_ref, kb, masked, blk)
            m_new = jnp.maximum(m, jnp.max(sc, axis=-1, keepdims=True))
            p = jnp.exp(sc - m_new)
            alpha = jnp.exp(m - m_new)
            l = alpha * l + jnp.sum(p, axis=-1, keepdims=True)
            acc = alpha * acc + jnp.dot(p.astype(BF16), kv_ref[0, rows, LANES:], preferred_element_type=F32)
            return m_new, l, acc

        init = (jnp.full((2 * blk, 1), NEG_INF, F32), jnp.zeros((2 * blk, 1), F32), jnp.zeros((2 * blk, LANES), F32))
        m, l, acc = step(qi, lax.fori_loop(0, qi, functools.partial(step, masked=False), init), True)
        o = _unstack(acc / l, blk)
        o_ref[0] = o
        ob_ref[0] = o.astype(BF16)
        lse_ref[0] = _unstack(jnp.broadcast_to(m + jnp.log(l), (2 * blk, LANES)), blk)

    tile = pl.BlockSpec((1, blk, LANES), lambda i, h, q: (i, q, h))
    kvspec = pl.BlockSpec((1, s, 2 * LANES), lambda i, h, q: (i, 0, KV_BLK0 + h))
    shape = jax.ShapeDtypeStruct((b, s, FOX_WIDTH), F32)
    return _hosted_call(
        body, ex, name="fox_fwd", out_shape=(shape, jax.ShapeDtypeStruct((b, s, FOX_WIDTH), BF16), shape),
        grid=(b, PAIRS, nq),
        in_specs=[tile, kvspec, tile, pl.BlockSpec((1, 2, nq, 1, blk), lambda i, h, q: (i, h, 0, 0, 0))],
        out_specs=(tile, tile, tile), args=(qkv, qkv, c_exp, c_row))


def _fox_bwd(qkv, c_exp, c_row, lse, o, do, ex=None):
    b, s, _ = qkv.shape
    blk = min(ATT_BLOCK, s)
    nq = s // blk

    def body(q_ref, kv_ref, cc_ref, cr_ref, lse_ref, o_ref, do_ref, dq_ref, dkv_ref, dcq_ref, dc_ref, dk_acc, dv_acc):
        qi = pl.program_id(2)

        @pl.when(qi == 0)
        def _():
            dk_acc[...] = jnp.zeros_like(dk_acc)
            dv_acc[...] = jnp.zeros_like(dv_acc)
            dc_ref[...] = jnp.zeros_like(dc_ref)

        q_all = _stack_heads(q_ref[0] * FOX_SCALE)
        dov = do_ref[0]
        do_all = _stack_heads(dov.astype(BF16))
        delta = jnp.sum(_stack_heads(dov * o_ref[0]), axis=-1, keepdims=True)
        bias = _stack_cols(cc_ref[0]) - _stack_cols(lse_ref[0])

        def step(kb, carry, masked):
            acc, dcq = carry
            rows = pl.ds(pl.multiple_of(kb * blk, blk), blk)
            kblk = kv_ref[0, rows, :LANES]
            p = jnp.exp(_fox_scores(q_all, kblk, bias, cr_ref, kb, masked, blk))
            dp = lax.dot_general(do_all, kv_ref[0, rows, LANES:], NT_DIMS, preferred_element_type=F32)
            ds = p * (dp - delta)
            dsb = ds.astype(BF16)
            dv_acc[rows, :] += lax.dot_general(p.astype(BF16), do_all, TN_DIMS, preferred_element_type=F32)
            dk_acc[rows, :] += lax.dot_general(dsb, q_all, TN_DIMS, preferred_element_type=F32)
            dc_ref[0, 0, kb] -= jnp.sum(ds[:blk], axis=0, keepdims=True)
            dc_ref[0, 1, kb] -= jnp.sum(ds[blk:], axis=0, keepdims=True)
            acc = acc + jnp.dot(dsb, kblk, preferred_element_type=F32)
            return acc, dcq + jnp.sum(ds, axis=-1, keepdims=True)

        init = (jnp.zeros((2 * blk, LANES), F32), jnp.zeros((2 * blk, 1), F32))
        acc, dcq = step(qi, lax.fori_loop(0, qi, functools.partial(step, masked=False), init), True)
        dq_ref[0] = (_unstack(acc, blk) * FOX_SCALE).astype(BF16)
        dcq_ref[0, 0] = jnp.where(lax.broadcasted_iota(jnp.int32, (blk, 2), 1) == 0, dcq[:blk], dcq[blk:])

        @pl.when(qi == nq - 1)
        def _():
            dkv_ref[0, :, :LANES] = dk_acc[...].astype(BF16)
            dkv_ref[0, :, LANES:] = dv_acc[...].astype(BF16)

    tile = pl.BlockSpec((1, blk, LANES), lambda i, h, q: (i, q, h))
    kvspec = pl.BlockSpec((1, s, 2 * LANES), lambda i, h, q: (i, 0, KV_BLK0 + h))
    crow = pl.BlockSpec((1, 2, nq, 1, blk), lambda i, h, q: (i, h, 0, 0, 0))
    return _hosted_call(
        body, ex, name="fox_bwd", grid=(b, PAIRS, nq),
        out_shape=(jax.ShapeDtypeStruct((b, s, FOX_WIDTH), BF16), jax.ShapeDtypeStruct((b, s, 2 * FOX_WIDTH), BF16),
                   jax.ShapeDtypeStruct((b, PAIRS, s, 2), F32), jax.ShapeDtypeStruct(c_row.shape, F32)),
        in_specs=[tile, kvspec, tile, crow, tile, tile, tile],
        out_specs=(tile, pl.BlockSpec((1, s, 2 * LANES), lambda i, h, q: (i, 0, h)),
                   pl.BlockSpec((1, 1, blk, 2), lambda i, h, q: (i, h, q, 0)), crow),
        scratch=[pltpu.VMEM((s, LANES), F32), pltpu.VMEM((s, LANES), F32)],
        args=(qkv, qkv, c_exp, c_row, lse, o, do))


def _sigmoid(z):
    return 1.0 / (1.0 + jnp.exp(-z))


def _mix_fwd(gu, b_gate, y_pool, y_fox):
    t = gu.shape[0]
    bt = _block(t, 256, 16)

    def body(gp_ref, gf_ref, bp_ref, bf_ref, yp_ref, yf_ref, o_ref):
        gp = _sigmoid(gp_ref[...].astype(F32) + bp_ref[...])
        gf = _sigmoid(gf_ref[...].astype(F32) + bf_ref[...])
        o_ref[...] = (gp * yp_ref[...].astype(F32) + gf * yf_ref[...].astype(F32)).astype(BF16)

    col = lambda j: pl.BlockSpec((bt, D_MODEL), lambda i: (i, j))
    vec = lambda j: pl.BlockSpec((1, D_MODEL), lambda i: (0, j))
    return pl.pallas_call(
        body, name="mix_fwd", out_shape=jax.ShapeDtypeStruct((t, D_MODEL), BF16), grid=(t // bt,),
        in_specs=[col(0), col(1), vec(0), vec(1), col(0), col(0)], out_specs=col(0),
        compiler_params=_cparams(("parallel",)),
    )(gu, gu, b_gate, b_gate, y_pool, y_fox)


def _mix_bwd(gu, b_gate, y_pool, y_fox, dmix):
    t = gu.shape[0]
    bt = _block(t, 256, 16)

    def body(gp_ref, gf_ref, bp_ref, bf_ref, yp_ref, yf_ref, dm_ref, dyp_ref, dyf_ref, dgl_ref, db_ref):
        @pl.when(pl.program_id(0) == 0)
        def _():
            db_ref[...] = jnp.zeros_like(db_ref)

        dm = dm_ref[...]
        gp = _sigmoid(gp_ref[...].astype(F32) + bp_ref[...])
        gf = _sigmoid(gf_ref[...].astype(F32) + bf_ref[...])
        dyp_ref[...] = (dm * gp).astype(BF16)
        dyf_ref[...] = (dm * gf).astype(BF16)
        dlp = dm * yp_ref[...].astype(F32) * gp * (1.0 - gp)
        dlf = dm * yf_ref[...].astype(F32) * gf * (1.0 - gf)
        dgl_ref[:, :D_MODEL] = dlp.astype(BF16)
        dgl_ref[:, D_MODEL:] = dlf.astype(BF16)
        db_ref[:, :D_MODEL] += jnp.sum(dlp, axis=0, keepdims=True)
        db_ref[:, D_MODEL:] += jnp.sum(dlf, axis=0, keepdims=True)

    col = lambda j: pl.BlockSpec((bt, D_MODEL), lambda i: (i, j))
    vec = lambda j: pl.BlockSpec((1, D_MODEL), lambda i: (0, j))
    wide = pl.BlockSpec((bt, GATE_WIDTH), lambda i: (i, 0))
    return pl.pallas_call(
        body, name="mix_bwd", grid=(t // bt,),
        out_shape=(jax.ShapeDtypeStruct((t, D_MODEL), BF16), jax.ShapeDtypeStruct((t, D_MODEL), BF16),
                   jax.ShapeDtypeStruct((t, GU_COLS), BF16), jax.ShapeDtypeStruct((1, GATE_WIDTH), F32)),
        in_specs=[col(0), col(1), vec(0), vec(1), col(0), col(0), col(0)],
        out_specs=(col(0), col(0), wide, pl.BlockSpec((1, GATE_WIDTH), lambda i: (0, 0))),
        compiler_params=_cparams(("arbitrary",)),
    )(gu, gu, b_gate, b_gate, y_pool, y_fox, dmix)


X_SCALE = X_DH ** -0.5


def _xattn_probs(qh, kh):
    s = lax.dot_general(qh, kh, NT_DIMS, preferred_element_type=F32) * X_SCALE
    e = jnp.exp(s - jnp.max(s, axis=-1, keepdims=True))
    return e / jnp.sum(e, axis=-1, keepdims=True)


def _xattn_fwd(q, kv):
    b, s, _ = q.shape
    m = kv.shape[1]
    bq = _block(s, 512, 16)

    def body(q_ref, kv_ref, o_ref):
        for h in range(X_HEADS):
            cols = slice(h * X_DH, (h + 1) * X_DH)
            p = _xattn_probs(q_ref[0, :, cols], kv_ref[0, :, cols])
            vh = kv_ref[0, :, X_WIDTH + h * X_DH:X_WIDTH + (h + 1) * X_DH]
            o_ref[0, :, cols] = jnp.dot(p.astype(BF16), vh, preferred_element_type=F32).astype(BF16)

    return pl.pallas_call(
        body, name="xattn_fwd", out_shape=jax.ShapeDtypeStruct((b, s, X_WIDTH), BF16), grid=(b, s // bq),
        in_specs=[pl.BlockSpec((1, bq, X_WIDTH), lambda i, j: (i, j, 0)),
                  pl.BlockSpec((1, m, 2 * X_WIDTH), lambda i, j: (i, 0, 0))],
        out_specs=pl.BlockSpec((1, bq, X_WIDTH), lambda i, j: (i, j, 0)),
        compiler_params=_cparams(("parallel", "parallel")),
    )(q, kv)


def _xattn_bwd(q, kv, do):
    b, s, _ = q.shape
    m = kv.shape[1]
    bq = _block(s, 512, 16)

    def body(q_ref, kv_ref, do_ref, dq_ref, dkv_ref):
        @pl.when(pl.program_id(1) == 0)
        def _():
            dkv_ref[...] = jnp.zeros_like(dkv_ref)

        for h in range(X_HEADS):
            cols = slice(h * X_DH, (h + 1) * X_DH)
            vcols = slice(X_WIDTH + h * X_DH, X_WIDTH + (h + 1) * X_DH)
            qh, kh, vh, doh = q_ref[0, :, cols], kv_ref[0, :, cols], kv_ref[0, :, vcols], do_ref[0, :, cols]
            p = _xattn_probs(qh, kh)
            dkv_ref[0, :, vcols] += lax.dot_general(p.astype(BF16), doh, TN_DIMS, preferred_element_type=F32)
            dp = lax.dot_general(doh, vh, NT_DIMS, preferred_element_type=F32)
            ds = (p * (dp - jnp.sum(p * dp, axis=-1, keepdims=True)) * X_SCALE).astype(BF16)
            dq_ref[0, :, cols] = jnp.dot(ds, kh, preferred_element_type=F32).astype(BF16)
            dkv_ref[0, :, cols] += lax.dot_general(ds, qh, TN_DIMS, preferred_element_type=F32)

    tile = pl.BlockSpec((1, bq, X_WIDTH), lambda i, j: (i, j, 0))
    mem = pl.BlockSpec((1, m, 2 * X_WIDTH), lambda i, j: (i, 0, 0))
    return pl.pallas_call(
        body, name="xattn_bwd", grid=(b, s // bq),
        out_shape=(jax.ShapeDtypeStruct((b, s, X_WIDTH), BF16), jax.ShapeDtypeStruct((b, m, 2 * X_WIDTH), F32)),
        in_specs=[tile, mem, tile], out_specs=(tile, mem),
        compiler_params=_cparams(("parallel", "arbitrary")),
    )(q, kv, do)


def _swiglu_fwd(gu):
    t = gu.shape[0]
    bt = _block(t, 256, 16)

    def body(gt_ref, up_ref, o_ref):
        gt = gt_ref[...].astype(F32)
        o_ref[...] = (gt * _sigmoid(gt) * up_ref[...].astype(F32)).astype(BF16)

    col = lambda j: pl.BlockSpec((bt, D_FF), lambda i: (i, j))
    return pl.pallas_call(
        body, name="swiglu_fwd", out_shape=jax.ShapeDtypeStruct((t, D_FF), BF16), grid=(t // bt,),
        in_specs=[col(0), col(1)], out_specs=col(0),
        compiler_params=_cparams(("parallel",)),
    )(gu, gu)


def _swiglu_bwd(gu, dact):
    t = gu.shape[0]
    bt = _block(t, 256, 16)

    def body(gt_ref, up_ref, da_ref, o_ref):
        gt = gt_ref[...].astype(F32)
        da = da_ref[...].astype(F32)
        sg = _sigmoid(gt)
        silu = gt * sg
        o_ref[:, :D_FF] = (da * up_ref[...].astype(F32) * (sg + silu * (1.0 - sg))).astype(BF16)
        o_ref[:, D_FF:] = (da * silu).astype(BF16)

    col = lambda j: pl.BlockSpec((bt, D_FF), lambda i: (i, j))
    return pl.pallas_call(
        body, name="swiglu_bwd", out_shape=jax.ShapeDtypeStruct((t, 2 * D_FF), BF16), grid=(t // bt,),
        in_specs=[col(0), col(1), col(0)], out_specs=pl.BlockSpec((bt, 2 * D_FF), lambda i: (i, 0)),
        compiler_params=_cparams(("parallel",)),
    )(gu, gu, dact)


def _stack_of(w, axis):
    r, c = w.shape
    if axis == 0:
        return w.reshape(N_CHIPS, r // N_CHIPS, c)
    return w.reshape(r, N_CHIPS, c // N_CHIPS).transpose(1, 0, 2)


def _stack_t(w3):
    n, r, c = w3.shape
    return w3.transpose(0, 2, 1).reshape(n * c, r)


def _pair_rows(k, v):
    c = k.shape[1]
    return jnp.stack([k.reshape(PAIRS, LANES, c), v.reshape(PAIRS, LANES, c)], axis=1).reshape(2 * FOX_WIDTH, c)


def _unpair_rows(kv):
    c = kv.shape[1]
    kv = kv.reshape(PAIRS, 2, LANES, c)
    return kv[:, 0].reshape(FOX_WIDTH, c), kv[:, 1].reshape(FOX_WIDTH, c)


def _input_grad(parts, weights_t, ex):
    t = parts[0].shape[0]
    d = weights_t[0].shape[1]
    bm = _block(t, 512, 16)
    n = len(parts)

    def body(*refs):
        acc = None
        for a_ref, b_ref in zip(refs[:n], refs[n:2 * n]):
            term = jnp.dot(a_ref[...], b_ref[...], preferred_element_type=F32)
            acc = term if acc is None else acc + term
        refs[2 * n][...] = acc

    (out,), moved = _hosted_call(
        body, ex, name="d_h", grid=(t // bm,), out_shape=(jax.ShapeDtypeStruct((t, d), F32),),
        in_specs=[pl.BlockSpec((bm, p.shape[1]), lambda i: (i, 0)) for p in parts]
        + [pl.BlockSpec(w.shape, lambda i: (0, 0)) for w in weights_t],
        out_specs=(pl.BlockSpec((bm, d), lambda i: (i, 0)),), args=tuple(parts) + tuple(weights_t))
    return out, moved


def _step(x, mem, loss_target, weights, moments_m, moments_v):
    nb, s, d = x.shape
    n_mem = mem.shape[1]
    t = nb * s
    blk = min(ATT_BLOCK, s)
    x2 = x.reshape(t, d)
    mem2 = mem.reshape(nb * n_mem, d)
    tgt2 = loss_target.reshape(t, d)

    def shard2d(a, n):
        a = a.reshape(a.shape[1:])
        return a.T if n == "w_in" else a

    def unshard(a, n):
        return (a.T if n == "w_in" else a)[None]

    local = {n: shard2d(weights[n], n) for n, _, _ in SHARDED}

    names = [n for n, _, _ in SHARDED]
    last = ["w_ffn_out"]
    later = [n for n in names if n != "w_in" and n not in last]
    local_b = {n: local[n].astype(BF16) for n in names}
    g_mix = weights["norm_mix_g"]
    h, w_in_others = _rms_fwd(x2, g_mix, "norm_mix", ex=_gather_exchange([local_b["w_in"]]))
    w_in_stack, = _place_own(w_in_others, [local_b["w_in"]])

    def w_in_rows(lo, hi):
        per = IN_COLS // N_CHIPS
        parts = [w_in_stack[j, max(lo, j * per) - j * per:min(hi, (j + 1) * per) - j * per]
                 for j in range(N_CHIPS) if max(lo, j * per) < min(hi, (j + 1) * per)]
        return parts[0] if len(parts) == 1 else jnp.concatenate(parts)

    w_gu_t = jnp.concatenate([w_in_rows(2056, IN_COLS), w_in_rows(0, 512)])
    w_qkv_t = jnp.concatenate([w_in_rows(512, 1024), _pair_rows(w_in_rows(1024, 1536), w_in_rows(1536, 2048))])
    w_f_t = jnp.pad(w_in_rows(2048, 2056), ((0, LANES - FOX_HEADS), (0, 0)))
    w_gu, w_qkv, w_f = w_gu_t.T, w_qkv_t.T, w_f_t.T
    w_f_exp = jnp.repeat(w_f[:, :FOX_HEADS], FOX_DH, axis=1)

    g_mix, g_x, g_mem, g_ffn = (weights[n] for n in ("norm_mix_g", "norm_x_g", "norm_mem_g", "norm_ffn_g"))
    g_final = weights["norm_final_g"].reshape(1, d)
    pool_w = weights["pool_w"].reshape(4, POOL_GC, POOL_GC)
    pool_scale, b_gate = weights["pool_scale"], weights["b_gate"]
    b_f_pad = jnp.pad(weights["b_forget"], ((0, 0), (0, LANES - FOX_HEADS)))
    b_f_exp = jnp.repeat(weights["b_forget"], FOX_DH, axis=1)

    gu, last_others = _mm(h, w_gu, out_dtype=BF16, bn=512, name="in_proj_gates_pool",
                          ex=_gather_exchange([local_b[n] for n in last]))
    qkv = _mm(h, w_qkv, out_dtype=BF16, bn=512, name="in_proj_qkv")
    f_pad = _mm(h, w_f, name="in_proj_forget")
    gu3, qkv3 = gu.reshape(nb, s, GU_COLS), qkv.reshape(nb, s, 3 * FOX_WIDTH)
    y = _pool_fwd(gu3, pool_w, pool_scale)
    f_exp = _mm(h, w_f_exp, name="in_proj_forget_lanes").reshape(nb, s, FOX_WIDTH)
    c_exp = _forget_cumsum(f_exp, b_f_exp, "forget_cumsum_lanes")
    c_pad = _forget_cumsum(f_pad.reshape(nb, s, LANES), b_f_pad, "forget_cumsum")
    c_row = c_pad[:, :, :FOX_HEADS].transpose(0, 2, 1).reshape(nb, FOX_HEADS, s // blk, 1, blk)
    (o, o_b, lse), gathered = _fox_fwd(qkv3, c_exp, c_row, ex=_gather_exchange([local_b[n] for n in later]))
    stacks = dict(zip(later, _place_own(gathered, [local_b[n] for n in later])))
    stacks.update(zip(last, _place_own(last_others, [local_b[n] for n in last])))
    w_pool_out3, w_fox_out3, w_xo3, w_ffn_in3 = (stacks[n] for n in ("w_pool_out", "w_fox_out", "w_xo", "w_ffn_in"))
    w_out, w_xq, w_xkv, w_ffn_out = (stacks[n].reshape(-1, stacks[n].shape[2])
                                     for n in ("w_out", "w_xq", "w_xkv", "w_ffn_out"))
    y2, o2 = y.reshape(t, POOL_WIDTH), o_b.reshape(t, FOX_WIDTH)
    y_pool = _mm(y2, w_pool_out3, b_stack=True, out_dtype=BF16, name="pool_out")
    y_fox = _mm(o2, w_fox_out3, b_stack=True, out_dtype=BF16, name="fox_out")
    mix = _mix_fwd(gu, b_gate, y_pool, y_fox)
    x1 = _mm(mix, w_out, res=x2, name="mix_out")
    hx = _rms_fwd(x1, g_x, "norm_x")
    mem_n = _rms_fwd(mem2, g_mem, "norm_mem")
    qx = _mm(hx, w_xq, out_dtype=BF16, name="x_q")
    kv = _mm(mem_n, w_xkv, out_dtype=BF16, name="x_kv")
    qx3, kv3 = qx.reshape(nb, s, X_WIDTH), kv.reshape(nb, n_mem, 2 * X_WIDTH)
    ox = _xattn_fwd(qx3, kv3).reshape(t, X_WIDTH)
    x2_ = _mm(ox, w_xo3, b_stack=True, res=x1, name="x_out")
    hf = _rms_fwd(x2_, g_ffn, "norm_ffn")
    ffn = _mm(hf, w_ffn_in3, b_stack=True, out_dtype=BF16, bm=2048, bn=1408, name="ffn_in")
    act = _swiglu_fwd(ffn)
    x3 = _mm(act, w_ffn_out, res=x2_, name="ffn_out")

    dx3, dx3_b, dg_final, loss_part = _final_loss(x3, tgt2, g_final)
    dw_ffn_out = _mm(act, dx3_b, ta=True, bm=1408, bn=512, bk=2048, name="d_w_ffn_out")
    dact = _mm(dx3_b, w_ffn_out.T, out_dtype=BF16, bm=2048, bn=1408, name="d_act")
    dffn = _swiglu_bwd(ffn, dact)
    dw_ffn_in = _mm(hf, dffn, ta=True, bm=512, bn=1408, bk=2048, out_stack=True, name="d_w_ffn_in")
    core = lax.axis_index("c").astype(jnp.int32).reshape(1)
    ffn_group = ["w_ffn_in", "w_ffn_out"]
    mid_group = ["w_pool_out", "w_fox_out", "w_out", "w_xq", "w_xkv", "w_xo"]
    grad_stacks = {"w_ffn_in": dw_ffn_in, "w_ffn_out": _stack_of(dw_ffn_out, 0)}

    def presum(group, theirs):
        return [_sum_halves(grad_stacks[n], t_, core, "sum_halves_" + n) for n, t_ in zip(group, theirs)]

    dhf, theirs = _mm(dffn, _stack_t(w_ffn_in3), bk=2816, name="d_hf",
                      ex=_swap_exchange([grad_stacks[n] for n in ffn_group]))
    chip_sums = dict(zip(ffn_group, presum(ffn_group, theirs)))
    dx2, dx2_b, dg_ffn = _rms_bwd(dhf, x2_, g_ffn, dx3, "norm_ffn_bwd")

    dw_xo = _mm(ox, dx2_b, ta=True, bn=256, out_stack=True, name="d_w_xo")
    dox = _mm(dx2_b, _stack_t(w_xo3), out_dtype=BF16, name="d_ox").reshape(nb, s, X_WIDTH)
    dqx, dkv = _xattn_bwd(qx3, kv3, dox)
    dqx2, dkv2 = dqx.reshape(t, X_WIDTH), dkv.reshape(nb * n_mem, 2 * X_WIDTH)
    dw_xkv = _mm(mem_n, dkv2, ta=True, name="d_w_xkv")
    dmem_n = _mm(dkv2, w_xkv.T, name="d_mem_n")
    dg_mem = _rms_bwd(dmem_n, mem2, g_mem, None, "norm_mem_bwd")
    dw_xq = _mm(hx, dqx2, ta=True, name="d_w_xq")
    dhx = _mm(dqx2, w_xq.T, name="d_hx")
    dx1, dx1_b, dg_x = _rms_bwd(dhx, x1, g_x, dx2, "norm_x_bwd")

    dw_out = _mm(mix, dx1_b, ta=True, name="d_w_out")
    dmix = _mm(dx1_b, w_out.T, name="d_mix")
    dyp, dyf, dgu, db_gate = _mix_bwd(gu, b_gate, y_pool, y_fox, dmix)
    dw_pool_out = _mm(y2, dyp, ta=True, bn=256, out_stack=True, name="d_w_pool_out")
    dw_fox_out = _mm(o2, dyf, ta=True, bn=256, out_stack=True, name="d_w_fox_out")
    dy = _mm(dyp, _stack_t(w_pool_out3), name="d_y").reshape(nb, s, POOL_WIDTH)
    do = _mm(dyf, _stack_t(w_fox_out3), name="d_o").reshape(nb, s, FOX_WIDTH)
    dgu3, dpool_w, dpool_scale = _pool_bwd(gu3, dy, pool_w, pool_scale, dgu.reshape(nb, s, GU_COLS))
    grad_stacks.update({"w_pool_out": dw_pool_out, "w_fox_out": dw_fox_out, "w_out": _stack_of(dw_out, 0),
                        "w_xq": _stack_of(dw_xq, 0), "w_xkv": _stack_of(dw_xkv, 0), "w_xo": dw_xo})
    dgu2 = dgu3.reshape(t, GU_COLS)
    dw_gu_t, theirs = _mm(dgu2, h, ta=True, name="d_w_gates_pool",
                          ex=_swap_exchange([grad_stacks[n] for n in mid_group]))
    chip_sums.update(zip(mid_group, presum(mid_group, theirs)))
    early = ffn_group + mid_group
    (dq3, dkv3, dc_q, dc_row), early_slots = _fox_bwd(qkv3, c_exp, c_row, lse, o, do,
                                                      ex=_chips_exchange([chip_sums[n] for n in early]))
    slots = dict(zip(early, early_slots))
    dc = dc_row.reshape(nb, FOX_HEADS, s).transpose(0, 2, 1) + dc_q.transpose(0, 2, 1, 3).reshape(nb, s, FOX_HEADS)
    dc = jnp.pad(dc, ((0, 0), (0, 0), (0, LANES - FOX_HEADS)))
    df, db_f = _forget_bwd(dc, f_pad.reshape(nb, s, LANES), b_f_pad)
    dq2, dkv2, df2 = dq3.reshape(t, FOX_WIDTH), dkv3.reshape(t, 2 * FOX_WIDTH), df.reshape(t, LANES)
    dw_q_t = _mm(dq2, h, ta=True, name="d_w_q")
    dw_kv_t = _mm(dkv2, h, ta=True, name="d_w_kv")
    dw_f_t = _mm(df2, h, ta=True, name="d_w_forget")
    dw_k_t, dw_v_t = _unpair_rows(dw_kv_t)
    dw_in_t = jnp.concatenate([dw_gu_t[GATE_WIDTH:], dw_q_t, dw_k_t, dw_v_t, dw_f_t[:FOX_HEADS],
                               dw_gu_t[:GATE_WIDTH]])
    grad_stacks["w_in"] = dw_in_t.reshape(N_CHIPS, IN_COLS // N_CHIPS, D_MODEL)
    chip_sums["w_in"], = presum(["w_in"], _run_exchange(_swap_exchange([grad_stacks["w_in"]]), "swap_halves_w_in"))
    dh, (slots["w_in"],) = _input_grad([dgu2, dq2, dkv2, df2],
                                       [w_gu_t, w_qkv_t[:FOX_WIDTH], w_qkv_t[FOX_WIDTH:], w_f_t],
                                       _chips_exchange([chip_sums["w_in"]]))

    place = jnp.stack([lax.axis_index("c"), 2 * lax.axis_index("x") + lax.axis_index("y")]).astype(jnp.int32)
    halves = [_sum_chips(slots[n], chip_sums[n], place, _by_rows(local[n].shape[0]), "sum_chips_" + n) for n in names]
    (dx, _, dg_mix), reduced = _rms_bwd(dh, x2, g_mix, dx1, "norm_mix_bwd", ex=_join_exchange(halves))

    small_grads = {"norm_mix_g": dg_mix, "b_forget": db_f[:, :FOX_HEADS], "b_gate": db_gate, "pool_w": dpool_w,
                   "pool_scale": dpool_scale, "norm_x_g": dg_x, "norm_mem_g": dg_mem, "norm_ffn_g": dg_ffn,
                   "norm_final_g": dg_final}
    def flat2d(a):
        return a.reshape(-1, a.shape[-1])

    small_names = [n for n, _ in SMALL]
    own = [flat2d(small_grads[n]) for n in small_names]
    small_gather = _small_exchange(own + [loss_part])

    def tiles_of(a):
        return a.transpose(2, 0, 1)

    def block_of(a3):
        return a3.transpose(1, 2, 0)

    grads, deltas, new_m, new_v = {}, {}, {}, {}
    gathered = None
    for n, g_ in zip(names, reduced):
        if n == "w_in":
            g_ = lax.optimization_barrier(g_.reshape(IN_COLS // N_CHIPS, 1, D_MODEL))
            (d_, m_, v_), gathered = _adamw(tiles_of(weights[n]), g_, tiles_of(moments_m[n]), tiles_of(moments_v[n]),
                                            "adamw_" + n, ex=small_gather)
            back = block_of
        else:
            d_, m_, v_ = _adamw(local[n], g_, shard2d(moments_m[n], n), shard2d(moments_v[n], n), "adamw_" + n)
            back = functools.partial(unshard, n=n)
        grads[n], deltas[n], new_m[n], new_v[n] = (back(a) for a in (g_, d_, m_, v_))

    device = (4 * lax.axis_index("x") + 2 * lax.axis_index("y") + lax.axis_index("c")).astype(jnp.int32).reshape(1)
    sg, sd, sm, sv, loss_sum = _adamw_small(
        gathered[:-1], own, [flat2d(weights[n]) for n in small_names], [flat2d(moments_m[n]) for n in small_names],
        [flat2d(moments_v[n]) for n in small_names], gathered[-1], loss_part, device)
    for n, g_, d_, m_, v_ in zip(small_names, sg, sd, sm, sv):
        grads[n], deltas[n], new_m[n], new_v[n] = (a.reshape(weights[n].shape) for a in (g_, d_, m_, v_))
    return loss_sum[0, 0], dx.reshape(nb, s, d), grads, deltas, new_m, new_v


def kernel(x, mem, norm_mix_g, w_in, b_forget, b_gate, pool_w, pool_scale, w_pool_out, w_fox_out, w_out, norm_x_g, norm_mem_g, w_xq, w_xkv, w_xo, norm_ffn_g, w_ffn_in, w_ffn_out, norm_final_g, loss_target, m_norm_mix_g, m_w_in, m_b_forget, m_b_gate, m_pool_w, m_pool_scale, m_w_pool_out, m_w_fox_out, m_w_out, m_norm_x_g, m_norm_mem_g, m_w_xq, m_w_xkv, m_w_xo, m_norm_ffn_g, m_w_ffn_in, m_w_ffn_out, m_norm_final_g, v_norm_mix_g, v_w_in, v_b_forget, v_b_gate, v_pool_w, v_pool_scale, v_w_pool_out, v_w_fox_out, v_w_out, v_norm_x_g, v_norm_mem_g, v_w_xq, v_w_xkv, v_w_xo, v_norm_ffn_g, v_w_ffn_in, v_w_ffn_out, v_norm_final_g):
    given = dict(locals())
    weights = {n: given[n] for n in WEIGHT_ORDER}
    moments_m = {n: given["m_" + n] for n in WEIGHT_ORDER}
    moments_v = {n: given["v_" + n] for n in WEIGHT_ORDER}
    loss, grad_x, grads, deltas, new_m, new_v = _step(x, mem, loss_target, weights, moments_m, moments_v)
    return (loss, grad_x, *[grads[n] for n in WEIGHT_ORDER], *[deltas[n] for n in WEIGHT_ORDER],
            *[new_m[n] for n in WEIGHT_ORDER], *[new_v[n] for n in WEIGHT_ORDER])
```

```python
import functools
import math

import jax
import jax.numpy as jnp
from jax import lax
from jax.experimental import pallas as pl
from jax.experimental.pallas import tpu as pltpu

F32 = jnp.float32
BF16 = jnp.bfloat16
MESH = pl.DeviceIdType.MESH

D_MODEL = 1024
EPS = 1e-6
POOL_WINDOWS = (2, 4, 8, 16)
POOL_WIDTH = 512
POOL_GC = 128
FOX_HEADS = 8
FOX_DH = 64
FOX_WIDTH = 512
X_HEADS = 4
X_DH = 128
X_WIDTH = 512
D_FF = 2816
IN_COLS = 4104
GATE_WIDTH = 2048
ADAM_LR = 0.001
ADAM_B1 = 0.9
ADAM_B2 = 0.999
ADAM_EPS = 1e-08
ADAM_WD = 0.01
ADAM_STEP = 10

N_CHIPS = 4
N_DEV = 8
LANES = 128
VMEM_LIMIT_BYTES = 56 * 1024 * 1024
NEG_INF = -1e30
ATT_BLOCK = 512

SHARDED = (
    ("w_in", (1024, IN_COLS), 1),
    ("w_pool_out", (POOL_WIDTH, 1024), 1),
    ("w_fox_out", (FOX_WIDTH, 1024), 1),
    ("w_out", (1024, 1024), 0),
    ("w_xq", (1024, X_WIDTH), 0),
    ("w_xkv", (1024, 2 * X_WIDTH), 0),
    ("w_xo", (X_WIDTH, 1024), 1),
    ("w_ffn_in", (1024, 2 * D_FF), 1),
    ("w_ffn_out", (D_FF, 1024), 0),
)
SMALL = (
    ("norm_mix_g", (1, 1024)),
    ("b_forget", (1, 8)),
    ("b_gate", (1, 2048)),
    ("pool_w", (1, 4, 128, 128)),
    ("pool_scale", (1, 512)),
    ("norm_x_g", (1, 1024)),
    ("norm_mem_g", (1, 1024)),
    ("norm_ffn_g", (1, 1024)),
    ("norm_final_g", (1024,)),
)
WEIGHT_ORDER = ("norm_mix_g", "w_in", "b_forget", "b_gate", "pool_w", "pool_scale", "w_pool_out", "w_fox_out", "w_out",
                "norm_x_g", "norm_mem_g", "w_xq", "w_xkv", "w_xo", "norm_ffn_g", "w_ffn_in", "w_ffn_out", "norm_final_g")


def _cparams(sem=None):
    return pltpu.CompilerParams(dimension_semantics=sem, vmem_limit_bytes=VMEM_LIMIT_BYTES)


def _block(dim, pref, unit):
    if dim <= pref:
        return dim
    best = None
    for b in range(unit, pref + 1, unit):
        if dim % b == 0:
            best = b
    assert best is not None, (dim, pref, unit)
    return best


def _rows_block(rows, cols, unit=16, elems=1 << 19):
    return _block(rows, max(unit, elems // cols // unit * unit), unit)


def _my_place():
    return lax.axis_index("x"), lax.axis_index("y"), lax.axis_index("c")


def _other_chips(x, y):
    return [(1 - x, y), (x, 1 - y), (1 - x, 1 - y)]


def _chip(place):
    return 2 * place[0] + place[1]


ANY = pl.BlockSpec(memory_space=pl.ANY)


def _by_rows(rows):
    return rows % 32 == 0


def _half_shape(rows, cols):
    return (rows // 2, cols) if _by_rows(rows) else (rows, cols // 2)


def _core_half(ref, core, lead=()):
    rows, cols = ref.shape[-2:]
    if _by_rows(rows):
        return ref.at[(*lead, pl.ds(core * (rows // 2), rows // 2), slice(None))]
    return ref.at[(*lead, slice(None), pl.ds(core * (cols // 2), cols // 2))]


class _Exchange:
    def __init__(self, arrays, out_shapes, n_sems, start, finish, in_place=False):
        self.arrays, self.out_shapes, self.n_sems, self.start, self.finish = arrays, out_shapes, n_sems, start, finish
        self.in_place = in_place

    def scratch(self):
        return [pltpu.SemaphoreType.DMA((self.n_sems,)), pltpu.SemaphoreType.DMA((self.n_sems,))]

    def aliases(self, first_in, first_out):
        return {first_in + k: first_out + k for k in range(len(self.arrays))} if self.in_place else {}


def _run_exchange(ex, name):
    n = len(ex.arrays)

    def body(*refs):
        ins, outs, sems = refs[:n], refs[n:2 * n], refs[2 * n:]
        ex.start(ins, outs, *sems)
        ex.finish(ins, outs, *sems)

    return pl.pallas_call(
        body, name=name, out_shape=ex.out_shapes, in_specs=[ANY] * n, out_specs=[ANY] * n, scratch_shapes=ex.scratch(),
        input_output_aliases=ex.aliases(0, 0),
    )(*ex.arrays)


def _hosted_call(body, ex, *, name, grid, in_specs, out_specs, out_shape, args, scratch=()):
    n_in, n_out, n_scr = len(args), len(out_shape), len(scratch)
    if ex is None:
        outs = pl.pallas_call(
            body, name=name, grid=grid, out_shape=out_shape, in_specs=in_specs, out_specs=out_specs,
            scratch_shapes=list(scratch), compiler_params=_cparams(("arbitrary",) * len(grid)))(*args)
        return outs, None
    nc = len(ex.arrays)

    def full_body(*refs):
        ins, cins = refs[:n_in], refs[n_in:n_in + nc]
        outs, couts = refs[n_in + nc:n_in + nc + n_out], refs[n_in + nc + n_out:n_in + 2 * nc + n_out]
        rest = refs[n_in + 2 * nc + n_out:]
        scr, sems = rest[:n_scr], rest[n_scr:]
        first = functools.reduce(jnp.logical_and, [pl.program_id(a) == 0 for a in range(len(grid))])
        last = functools.reduce(jnp.logical_and, [pl.program_id(a) == grid[a] - 1 for a in range(len(grid))])

        @pl.when(first)
        def _():
            ex.start(cins, couts, *sems)

        body(*ins, *outs, *scr)

        @pl.when(last)
        def _():
            ex.finish(cins, couts, *sems)

    outs = pl.pallas_call(
        full_body, name=name, grid=grid, out_shape=list(out_shape) + list(ex.out_shapes),
        in_specs=list(in_specs) + [ANY] * nc, out_specs=list(out_specs) + [ANY] * nc,
        scratch_shapes=list(scratch) + ex.scratch(), input_output_aliases=ex.aliases(n_in, n_out),
        compiler_params=_cparams(("arbitrary",) * len(grid)))(*args, *ex.arrays)
    return outs[:n_out], outs[n_out:]


def _gather_exchange(shards):
    n = len(shards)

    def copies(ins, outs, send_sems, recv_sems):
        x, y, c = _my_place()

        def half(k, chip, core):
            return _core_half(outs[k], core, lead=(_chip(chip),))

        def copy(k, slot, chip, core, to, src=None):
            return pltpu.make_async_remote_copy(
                src_ref=half(k, chip, core) if src is None else src, dst_ref=half(k, chip, core),
                send_sem=send_sems.at[6 * k + slot], recv_sem=recv_sems.at[6 * k + slot],
                device_id=to, device_id_type=MESH)

        return (x, y, c), copy

    def first_copies(ins, outs, send_sems, recv_sems):
        (x, y, c), copy = copies(ins, outs, send_sems, recv_sems)
        out = []
        for j, chip in enumerate(_other_chips(x, y)):
            for k in range(n):
                out.append(copy(k, j, (x, y), c, (*chip, c), src=_core_half(ins[k], c)))
        return out

    def start(ins, outs, send_sems, recv_sems):
        for cp in first_copies(ins, outs, send_sems, recv_sems):
            cp.start()

    def finish(ins, outs, send_sems, recv_sems):
        (x, y, c), copy = copies(ins, outs, send_sems, recv_sems)
        chips = _other_chips(x, y)
        passed = []
        for j, chip in enumerate(chips):
            for k in range(n):
                copy(k, j, chip, c, (x, y, c)).wait_recv()
                passed.append(copy(k, 3 + j, chip, c, (x, y, 1 - c)))
                passed[-1].start()
        for j, chip in enumerate(chips):
            for k in range(n):
                copy(k, 3 + j, chip, 1 - c, (x, y, c)).wait_recv()
        for cp in first_copies(ins, outs, send_sems, recv_sems) + passed:
            cp.wait_send()

    return _Exchange(list(shards), [jax.ShapeDtypeStruct((N_CHIPS,) + s.shape, s.dtype) for s in shards], 6 * n,
                     start, finish)


def _place_own(stacks, shards):
    me = 2 * lax.axis_index("x") + lax.axis_index("y")
    return [lax.dynamic_update_slice(others, mine[None], (me, 0, 0)) for others, mine in zip(stacks, shards)]


def _swap_exchange(grads):
    n = len(grads)

    def copies(ins, outs, send_sems, recv_sems):
        x, y, c = _my_place()
        return [pltpu.make_async_remote_copy(
            src_ref=_core_half(ins[k], 1 - c, lead=(slice(None),)), dst_ref=outs[k],
            send_sem=send_sems.at[k], recv_sem=recv_sems.at[k], device_id=(x, y, 1 - c), device_id_type=MESH)
            for k in range(n)]

    def start(ins, outs, send_sems, recv_sems):
        for cp in copies(ins, outs, send_sems, recv_sems):
            cp.start()

    def finish(ins, outs, send_sems, recv_sems):
        for cp in copies(ins, outs, send_sems, recv_sems):
            cp.wait()

    return _Exchange(list(grads), [jax.ShapeDtypeStruct((N_CHIPS,) + _half_shape(*g.shape[1:]), g.dtype) for g in grads],
                     n, start, finish)


def _chips_exchange(sums):
    n = len(sums)

    def sends(ins, outs, send_sems, recv_sems):
        x, y, c = _my_place()
        return [pltpu.make_async_remote_copy(
            src_ref=ins[k].at[_chip(chip)], dst_ref=outs[k].at[_chip((x, y))],
            send_sem=send_sems.at[3 * k + j], recv_sem=recv_sems.at[3 * k + j],
            device_id=(*chip, c), device_id_type=MESH)
            for j, chip in enumerate(_other_chips(x, y)) for k in range(n)]

    def start(ins, outs, send_sems, recv_sems):
        for cp in sends(ins, outs, send_sems, recv_sems):
            cp.start()

    def finish(ins, outs, send_sems, recv_sems):
        x, y, c = _my_place()
        for j, chip in enumerate(_other_chips(x, y)):
            for k in range(n):
                slot = outs[k].at[_chip(chip)]
                pltpu.make_async_remote_copy(
                    src_ref=slot, dst_ref=slot, send_sem=send_sems.at[3 * k + j], recv_sem=recv_sems.at[3 * k + j],
                    device_id=(x, y, c), device_id_type=MESH).wait_recv()
        for cp in sends(ins, outs, send_sems, recv_sems):
            cp.wait_send()

    return _Exchange(list(sums), [jax.ShapeDtypeStruct(s.shape, s.dtype) for s in sums], 3 * n, start, finish)


def _join_exchange(shards):
    n = len(shards)

    def sends(ins, outs, send_sems, recv_sems):
        x, y, c = _my_place()
        return [pltpu.make_async_remote_copy(
            src_ref=_core_half(ins[k], c), dst_ref=_core_half(outs[k], c),
            send_sem=send_sems.at[k], recv_sem=recv_sems.at[k], device_id=(x, y, 1 - c), device_id_type=MESH)
            for k in range(n)]

    def start(ins, outs, send_sems, recv_sems):
        for cp in sends(ins, outs, send_sems, recv_sems):
            cp.start()

    def finish(ins, outs, send_sems, recv_sems):
        x, y, c = _my_place()
        for k in range(n):
            theirs = _core_half(outs[k], 1 - c)
            pltpu.make_async_remote_copy(
                src_ref=theirs, dst_ref=theirs, send_sem=send_sems.at[k], recv_sem=recv_sems.at[k],
                device_id=(x, y, c), device_id_type=MESH).wait_recv()
        for cp in sends(ins, outs, send_sems, recv_sems):
            cp.wait_send()

    return _Exchange(list(shards), [jax.ShapeDtypeStruct(s.shape, s.dtype) for s in shards], n, start, finish,
                     in_place=True)


def _small_exchange(blocks):
    n = len(blocks)

    def copies(ins, outs, send_sems, recv_sems):
        x, y, c = _my_place()

        def copy(k, j, whose, to, src=None):
            slot = outs[k].at[4 * whose[0] + 2 * whose[1] + whose[2]]
            return pltpu.make_async_remote_copy(
                src_ref=slot if src is None else src, dst_ref=slot,
                send_sem=send_sems.at[7 * k + j], recv_sem=recv_sems.at[7 * k + j], device_id=to, device_id_type=MESH)

        return (x, y, c), copy

    def first_copies(ins, outs, send_sems, recv_sems):
        (x, y, c), copy = copies(ins, outs, send_sems, recv_sems)
        out = []
        for k in range(n):
            out.append(copy(k, 0, (x, y, c), (x, y, 1 - c), src=ins[k]))
            out += [copy(k, 1 + j, (x, y, c), (*chip, c), src=ins[k]) for j, chip in enumerate(_other_chips(x, y))]
        return out

    def start(ins, outs, send_sems, recv_sems):
        for cp in first_copies(ins, outs, send_sems, recv_sems):
            cp.start()

    def finish(ins, outs, send_sems, recv_sems):
        (x, y, c), copy = copies(ins, outs, send_sems, recv_sems)
        chips = _other_chips(x, y)
        passed = []
        for j, chip in enumerate(chips):
            for k in range(n):
                copy(k, 1 + j, (*chip, c), (x, y, c)).wait_recv()
                passed.append(copy(k, 4 + j, (*chip, c), (x, y, 1 - c)))
                passed[-1].start()
        for k in range(n):
            copy(k, 0, (x, y, 1 - c), (x, y, c)).wait_recv()
        for j, chip in enumerate(chips):
            for k in range(n):
                copy(k, 4 + j, (*chip, 1 - c), (x, y, c)).wait_recv()
        for cp in first_copies(ins, outs, send_sems, recv_sems) + passed:
            cp.wait_send()

    return _Exchange(list(blocks), [jax.ShapeDtypeStruct((N_DEV,) + blk.shape, blk.dtype) for blk in blocks], 7 * n,
                     start, finish)


def _sum_halves(grads, theirs, core, name):
    _, h, cols = theirs.shape
    by_rows = _by_rows(grads.shape[1])
    br = _rows_block(h, cols) if by_rows else h
    nb = h // br

    def body(core_ref, a_ref, b_ref, o_ref):
        o_ref[...] = (a_ref[...] + b_ref[...]).astype(BF16)

    if by_rows:
        mine = pl.BlockSpec((1, br, cols), lambda j, i, core_ref: (j, core_ref[0] * nb + i, 0))
    else:
        mine = pl.BlockSpec((1, br, cols), lambda j, i, core_ref: (j, i, core_ref[0]))
    return pl.pallas_call(
        body, name=name,
        out_shape=jax.ShapeDtypeStruct(theirs.shape, BF16),
        grid_spec=pltpu.PrefetchScalarGridSpec(
            num_scalar_prefetch=1, grid=(N_CHIPS, nb),
            in_specs=[mine, pl.BlockSpec((1, br, cols), lambda j, i, core_ref: (j, i, 0))],
            out_specs=pl.BlockSpec((1, br, cols), lambda j, i, core_ref: (j, i, 0))),
        compiler_params=_cparams(("parallel", "parallel")),
    )(core, grads, theirs)


def _sum_chips(slots, sums, place, by_rows, name):
    _, h, cols = slots.shape
    br = _rows_block(h, cols) if by_rows else h
    nb = h // br

    def body(place_ref, s_ref, own_ref, o_ref):
        me = place_ref[1]
        acc = None
        for k in range(N_CHIPS):
            term = jnp.where(me == k, own_ref[k], s_ref[k]).astype(F32)
            acc = term if acc is None else acc + term
        o_ref[...] = acc

    stack = pl.BlockSpec((N_CHIPS, br, cols), lambda i, place_ref: (0, i, 0))
    if by_rows:
        out_shape, out_map = (2 * h, cols), lambda i, place_ref: (place_ref[0] * nb + i, 0)
    else:
        out_shape, out_map = (h, 2 * cols), lambda i, place_ref: (i, place_ref[0])
    return pl.pallas_call(
        body, name=name,
        out_shape=jax.ShapeDtypeStruct(out_shape, F32),
        grid_spec=pltpu.PrefetchScalarGridSpec(
            num_scalar_prefetch=1, grid=(nb,), in_specs=[stack, stack],
            out_specs=pl.BlockSpec((br, cols), out_map)),
        compiler_params=_cparams(("parallel",)),
    )(place, slots, sums)


def _adamw_math(w, g, m, v):
    m = ADAM_B1 * m + (1.0 - ADAM_B1) * g
    v = ADAM_B2 * v + (1.0 - ADAM_B2) * (g * g)
    m_hat = m / (1.0 - ADAM_B1 ** ADAM_STEP)
    v_hat = v / (1.0 - ADAM_B2 ** ADAM_STEP)
    delta = -ADAM_LR * (m_hat / (jnp.sqrt(v_hat) + ADAM_EPS) + ADAM_WD * w)
    return delta, m, v


def _adamw(w, g, m, v, name, ex=None):
    def body(w_ref, g_ref, m_ref, v_ref, d_ref, nm_ref, nv_ref):
        d, nm, nv = _adamw_math(w_ref[...], g_ref[...], m_ref[...], v_ref[...])
        d_ref[...] = d
        nm_ref[...] = nm
        nv_ref[...] = nv

    if w.ndim == 3:
        rows = w.shape[0]
        br = max(b for b in range(1, 65) if rows % b == 0)
        spec, steps = pl.BlockSpec((br,) + w.shape[1:], lambda i: (i, 0, 0)), rows // br
    else:
        rows, cols = w.shape
        br = _rows_block(rows, cols, unit=8)
        spec, steps = pl.BlockSpec((br, cols), lambda i: (i, 0)), rows // br
    shape = jax.ShapeDtypeStruct(w.shape, F32)
    outs, moved = _hosted_call(
        body, ex, name=name, out_shape=(shape, shape, shape), grid=(steps,),
        in_specs=[spec] * 4, out_specs=(spec, spec, spec), args=(w, g, m, v))
    return tuple(outs) if ex is None else (tuple(outs), moved)


def _adamw_small(parts, own, ws, ms, vs, loss_parts, loss_own, device):
    n = len(ws)

    def total(device_ref, parts_ref, own_ref):
        acc = None
        for dev in range(N_DEV):
            term = jnp.where(device_ref[0] == dev, own_ref[...], parts_ref[dev])
            acc = term if acc is None else acc + term
        return acc

    def body(device_ref, *refs):
        ins, outs = refs[:5 * n + 2], refs[5 * n + 2:]
        for k in range(n):
            g = total(device_ref, ins[k], ins[n + k])
            d, nm, nv = _adamw_math(ins[2 * n + k][...], g, ins[3 * n + k][...], ins[4 * n + k][...])
            for o_ref, val in zip(outs[k::n][:4], (g, d, nm, nv)):
                o_ref[...] = val
        outs[4 * n][...] = total(device_ref, ins[5 * n], ins[5 * n + 1])

    args = list(parts) + list(own) + list(ws) + list(ms) + list(vs) + [loss_parts, loss_own]
    whole = lambda a: pl.BlockSpec(a.shape, lambda i, device_ref, nd=a.ndim: (0,) * nd)
    shapes = [jax.ShapeDtypeStruct(w.shape, F32) for w in ws] * 4 + [jax.ShapeDtypeStruct(loss_own.shape, F32)]
    outs = pl.pallas_call(
        body, name="adamw_small", out_shape=shapes,
        grid_spec=pltpu.PrefetchScalarGridSpec(
            num_scalar_prefetch=1, grid=(1,), in_specs=[whole(a) for a in args], out_specs=[whole(a) for a in shapes]),
        compiler_params=_cparams(("arbitrary",)),
    )(device, *args)
    return outs[:n], outs[n:2 * n], outs[2 * n:3 * n], outs[3 * n:4 * n], outs[4 * n]


def _mm(a, b, *, name, ta=False, out_dtype=F32, res=None, bm=1024, bn=1024, bk=4096, b_stack=False, out_stack=False,
        ex=None):
    if ta:
        kdim, m = a.shape
    else:
        m, kdim = a.shape
    if b_stack:
        _, kb, chunk = b.shape
        n = N_CHIPS * chunk
    else:
        kb, n = b.shape
        chunk = n // N_CHIPS if out_stack else n
    assert kdim == kb, (a.shape, b.shape, ta)
    bm = _block(m, bm, LANES if ta else 16)
    bn = _block(chunk, bn, LANES)
    bk = _block(kdim, bk, LANES)
    nk = kdim // bk
    per_chunk = chunk // bn
    dims = (((0 if ta else 1,), (0,)), ((), ()))

    def body(*refs):
        refs = list(refs)
        a_ref, b_ref = refs[:2]
        r_ref = refs[2] if res is not None else None
        o_ref = refs[3] if res is not None else refs[2]
        part = lax.dot_general(a_ref[...].astype(BF16), b_ref[...].astype(BF16), dims, preferred_element_type=F32)

        def finish(r):
            if r_ref is not None:
                r = r + r_ref[...]
            o_ref[...] = r.astype(out_dtype)

        if nk == 1:
            finish(part)
        else:
            acc_ref = refs[-1]
            k = pl.program_id(2)

            @pl.when(k == 0)
            def _():
                acc_ref[...] = part

            @pl.when(k > 0)
            def _():
                acc_ref[...] += part

            @pl.when(k == nk - 1)
            def _():
                finish(acc_ref[...])

    a_spec = pl.BlockSpec((bk, bm), lambda i, j, k: (k, i)) if ta else pl.BlockSpec((bm, bk), lambda i, j, k: (i, k))
    if b_stack:
        b_spec = pl.BlockSpec((None, bk, bn), lambda i, j, k: (j // per_chunk, k, j % per_chunk))
    else:
        b_spec = pl.BlockSpec((bk, bn), lambda i, j, k: (k, j))
    r_spec = pl.BlockSpec((bm, bn), lambda i, j, k: (i, j))
    if out_stack:
        o_spec = pl.BlockSpec((None, bm, bn), lambda i, j, k: (j // per_chunk, i, j % per_chunk))
        o_shape = (N_CHIPS, m, chunk)
    else:
        o_spec, o_shape = r_spec, (m, n)
    in_specs = [a_spec, b_spec] + ([r_spec] if res is not None else [])
    args = (a, b) + ((res,) if res is not None else ())
    (out,), moved = _hosted_call(
        body, ex, name=name, out_shape=(jax.ShapeDtypeStruct(o_shape, out_dtype),),
        grid=(m // bm, n // bn, nk), in_specs=in_specs, out_specs=(o_spec,),
        scratch=[pltpu.VMEM((bm, bn), F32)] if nk > 1 else [], args=args)
    return out if ex is None else (out, moved)


def _rms_fwd(x, g, name, ex=None):
    t, d = x.shape
    bt = _block(t, 512, 16)

    def body(x_ref, g_ref, h_ref):
        xv = x_ref[...]
        r = lax.rsqrt(jnp.mean(xv * xv, axis=-1, keepdims=True) + EPS)
        h_ref[...] = (xv * r * g_ref[...]).astype(BF16)

    (out,), moved = _hosted_call(
        body, ex, name=name, out_shape=(jax.ShapeDtypeStruct((t, d), BF16),), grid=(t // bt,),
        in_specs=[pl.BlockSpec((bt, d), lambda i: (i, 0)), pl.BlockSpec((1, d), lambda i: (0, 0))],
        out_specs=(pl.BlockSpec((bt, d), lambda i: (i, 0)),), args=(x, g))
    return out if ex is None else (out, moved)


def _rms_bwd(dh, x, g, dres, name, ex=None):
    t, d = x.shape
    bt = _block(t, 256, 16)
    want_dx = dres is not None

    def body(*refs):
        if want_dx:
            dh_ref, x_ref, g_ref, dres_ref, dx_ref, dxb_ref, dg_ref = refs
        else:
            dh_ref, x_ref, g_ref, dg_ref = refs
        xv = x_ref[...]
        r = lax.rsqrt(jnp.mean(xv * xv, axis=-1, keepdims=True) + EPS)
        xhat = xv * r
        dhv = dh_ref[...]

        @pl.when(pl.program_id(0) == 0)
        def _():
            dg_ref[...] = jnp.zeros_like(dg_ref)

        dg_ref[...] += jnp.sum(dhv * xhat, axis=0, keepdims=True)
        if want_dx:
            dxhat = dhv * g_ref[...]
            dx = dres_ref[...] + r * (dxhat - xhat * jnp.mean(dxhat * xhat, axis=-1, keepdims=True))
            dx_ref[...] = dx
            dxb_ref[...] = dx.astype(BF16)

    row = pl.BlockSpec((bt, d), lambda i: (i, 0))
    vec = pl.BlockSpec((1, d), lambda i: (0, 0))
    if want_dx:
        outs, moved = _hosted_call(
            body, ex, name=name, grid=(t // bt,),
            out_shape=(jax.ShapeDtypeStruct((t, d), F32), jax.ShapeDtypeStruct((t, d), BF16),
                       jax.ShapeDtypeStruct((1, d), F32)),
            in_specs=[row, row, vec, row], out_specs=(row, row, vec), args=(dh, x, g, dres))
        return tuple(outs) if ex is None else (tuple(outs), moved)
    return pl.pallas_call(
        body, name=name, grid=(t // bt,), out_shape=jax.ShapeDtypeStruct((1, d), F32),
        in_specs=[row, row, vec], out_specs=vec,
        compiler_params=_cparams(("arbitrary",)),
    )(dh, x, g)


def _final_loss(x, target, g):
    t, d = x.shape
    bt = _block(t, 256, 16)

    def body(x_ref, t_ref, g_ref, dx_ref, dxb_ref, dg_ref, loss_ref):
        xv = x_ref[...]
        gv = g_ref[...]
        r = lax.rsqrt(jnp.mean(xv * xv, axis=-1, keepdims=True) + EPS)
        xhat = xv * r
        err = xhat * gv - t_ref[...]

        @pl.when(pl.program_id(0) == 0)
        def _():
            dg_ref[...] = jnp.zeros_like(dg_ref)
            loss_ref[...] = jnp.zeros_like(loss_ref)

        loss_ref[...] += 0.5 * jnp.sum(jnp.mean(err * err, axis=-1, keepdims=True), axis=0, keepdims=True)
        dy = err * (1.0 / d)
        dg_ref[...] += jnp.sum(dy * xhat, axis=0, keepdims=True)
        dxhat = dy * gv
        dx = r * (dxhat - xhat * jnp.mean(dxhat * xhat, axis=-1, keepdims=True))
        dx_ref[...] = dx
        dxb_ref[...] = dx.astype(BF16)

    row = pl.BlockSpec((bt, d), lambda i: (i, 0))
    vec = pl.BlockSpec((1, d), lambda i: (0, 0))
    return pl.pallas_call(
        body, name="final_loss", grid=(t // bt,),
        out_shape=(jax.ShapeDtypeStruct((t, d), F32), jax.ShapeDtypeStruct((t, d), BF16),
                   jax.ShapeDtypeStruct((1, d), F32), jax.ShapeDtypeStruct((1, LANES), F32)),
        in_specs=[row, row, vec], out_specs=(row, row, vec, pl.BlockSpec((1, LANES), lambda i: (0, 0))),
        compiler_params=_cparams(("arbitrary",)),
    )(x, target, g)


GU_COLS = GATE_WIDTH + POOL_WIDTH
U_BLK = GATE_WIDTH // POOL_WIDTH


def _shift_down(a, k, row):
    return jnp.where(row >= k, pltpu.roll(a, k, 0), 0.0)


def _shift_up(a, k, row):
    n = a.shape[0]
    return jnp.where(row < n - k, pltpu.roll(a, n - k, 0), 0.0)


def _window_delta(u, w, row):
    s, k = u, 1
    while k < w:
        s = s + _shift_down(s, k, row)
        k *= 2
    cnt = jnp.minimum(row + 1, w).astype(F32)
    return s / cnt - u, cnt


def _pool_fwd(gu, pool_w, pool_scale):
    b, s, _ = gu.shape

    def body(u_ref, pw_ref, sc_ref, y_ref):
        row = lax.broadcasted_iota(jnp.int32, (s, POOL_GC), 0)
        for g, w in enumerate(POOL_WINDOWS):
            cols = slice(g * POOL_GC, (g + 1) * POOL_GC)
            d, _ = _window_delta(u_ref[0, :, cols].astype(F32), w, row)
            z = jnp.dot(d.astype(BF16), pw_ref[g].astype(BF16), preferred_element_type=F32)
            y_ref[0, :, cols] = (z * sc_ref[:, cols]).astype(BF16)

    return pl.pallas_call(
        body, name="pool_fwd", out_shape=jax.ShapeDtypeStruct((b, s, POOL_WIDTH), BF16), grid=(b,),
        in_specs=[pl.BlockSpec((1, s, POOL_WIDTH), lambda i: (i, 0, U_BLK)),
                  pl.BlockSpec((4, POOL_GC, POOL_GC), lambda i: (0, 0, 0)),
                  pl.BlockSpec((1, POOL_WIDTH), lambda i: (0, 0))],
        out_specs=pl.BlockSpec((1, s, POOL_WIDTH), lambda i: (i, 0, 0)),
        compiler_params=_cparams(("parallel",)),
    )(gu, pool_w, pool_scale)


def _pool_bwd(gu, dy, pool_w, pool_scale, dgu):
    b, s, _ = gu.shape

    def body(u_ref, dy_ref, pw_ref, sc_ref, dgu_in, du_ref, dpw_ref, dsc_ref):
        del dgu_in

        @pl.when(pl.program_id(0) == 0)
        def _():
            dpw_ref[...] = jnp.zeros_like(dpw_ref)
            dsc_ref[...] = jnp.zeros_like(dsc_ref)

        row = lax.broadcasted_iota(jnp.int32, (s, POOL_GC), 0)
        for g, w in enumerate(POOL_WINDOWS):
            cols = slice(g * POOL_GC, (g + 1) * POOL_GC)
            d, cnt = _window_delta(u_ref[0, :, cols].astype(F32), w, row)
            db = d.astype(BF16)
            pw = pw_ref[g].astype(BF16)
            z = jnp.dot(db, pw, preferred_element_type=F32)
            dyv = dy_ref[0, :, cols]
            dsc_ref[:, cols] += jnp.sum(dyv * z, axis=0, keepdims=True)
            dz = (dyv * sc_ref[:, cols]).astype(BF16)
            dpw_ref[g] += lax.dot_general(db, dz, (((0,), (0,)), ((), ())), preferred_element_type=F32)
            dd = lax.dot_general(dz, pw, (((1,), (1,)), ((), ())), preferred_element_type=F32)
            acc, k = dd / cnt, 1
            while k < w:
                acc = acc + _shift_up(acc, k, row)
                k *= 2
            du_ref[0, :, cols] = (acc - dd).astype(BF16)

    return pl.pallas_call(
        body, name="pool_bwd", grid=(b,),
        out_shape=(jax.ShapeDtypeStruct((b, s, GU_COLS), BF16), jax.ShapeDtypeStruct((4, POOL_GC, POOL_GC), F32),
                   jax.ShapeDtypeStruct((1, POOL_WIDTH), F32)),
        in_specs=[pl.BlockSpec((1, s, POOL_WIDTH), lambda i: (i, 0, U_BLK)),
                  pl.BlockSpec((1, s, POOL_WIDTH), lambda i: (i, 0, 0)),
                  pl.BlockSpec((4, POOL_GC, POOL_GC), lambda i: (0, 0, 0)),
                  pl.BlockSpec((1, POOL_WIDTH), lambda i: (0, 0)), ANY],
        out_specs=(pl.BlockSpec((1, s, POOL_WIDTH), lambda i: (i, 0, U_BLK)),
                   pl.BlockSpec((4, POOL_GC, POOL_GC), lambda i: (0, 0, 0)),
                   pl.BlockSpec((1, POOL_WIDTH), lambda i: (0, 0))),
        input_output_aliases={4: 0},
        compiler_params=_cparams(("arbitrary",)),
    )(gu, dy, pool_w, pool_scale, dgu)


def _forget_cumsum(f, bias, name):
    b, s, c = f.shape

    def body(f_ref, b_ref, c_ref):
        row = lax.broadcasted_iota(jnp.int32, (s, LANES), 0)
        z = f_ref[0] + b_ref[...]
        acc = jnp.minimum(z, 0.0) - jnp.log(1.0 + jnp.exp(-jnp.abs(z)))
        k = 1
        while k < s:
            acc = acc + _shift_down(acc, k, row)
            k *= 2
        c_ref[0] = acc

    return pl.pallas_call(
        body, name=name, out_shape=jax.ShapeDtypeStruct((b, s, c), F32), grid=(b, c // LANES),
        in_specs=[pl.BlockSpec((1, s, LANES), lambda i, j: (i, 0, j)), pl.BlockSpec((1, LANES), lambda i, j: (0, j))],
        out_specs=pl.BlockSpec((1, s, LANES), lambda i, j: (i, 0, j)),
        compiler_params=_cparams(("parallel", "parallel")),
    )(f, bias)


def _forget_bwd(dc, f, bias):
    b, s, _ = f.shape

    def body(dc_ref, f_ref, b_ref, df_ref, db_ref):
        @pl.when(pl.program_id(0) == 0)
        def _():
            db_ref[...] = jnp.zeros_like(db_ref)

        row = lax.broadcasted_iota(jnp.int32, (s, LANES), 0)
        acc, k = dc_ref[0], 1
        while k < s:
            acc = acc + _shift_up(acc, k, row)
            k *= 2
        z = f_ref[0] + b_ref[...]
        df = acc / (1.0 + jnp.exp(z))
        db_ref[...] += jnp.sum(df, axis=0, keepdims=True)
        df_ref[0] = df.astype(BF16)

    blk = pl.BlockSpec((1, s, LANES), lambda i: (i, 0, 0))
    vec = pl.BlockSpec((1, LANES), lambda i: (0, 0))
    return pl.pallas_call(
        body, name="forget_bwd", grid=(b,),
        out_shape=(jax.ShapeDtypeStruct((b, s, LANES), BF16), jax.ShapeDtypeStruct((1, LANES), F32)),
        in_specs=[blk, blk, vec], out_specs=(blk, vec),
        compiler_params=_cparams(("arbitrary",)),
    )(dc, f, bias)


KV_BLK0 = 2
PAIRS = FOX_HEADS // 2
FOX_SCALE = FOX_DH ** -0.5
NT_DIMS = (((1,), (1,)), ((), ()))
TN_DIMS = (((0,), (0,)), ((), ()))


def _stack_heads(v):
    head = lax.broadcasted_iota(jnp.int32, v.shape, 1) // FOX_DH
    zero = jnp.zeros_like(v)
    return jnp.concatenate([jnp.where(head == 0, v, zero), jnp.where(head == 1, v, zero)], axis=0)


def _stack_cols(v):
    return jnp.concatenate([v[:, 0:1], v[:, FOX_DH:FOX_DH + 1]], axis=0)


def _unstack(t, blk):
    head = lax.broadcasted_iota(jnp.int32, (blk, LANES), 1) // FOX_DH
    return jnp.where(head == 0, t[:blk], t[blk:])


def _fox_scores(q_all, kblk, row_bias, cr_ref, kb, masked, blk):
    top = lax.broadcasted_iota(jnp.int32, (2 * blk, 1), 0) < blk
    s = lax.dot_general(q_all, kblk, NT_DIMS, preferred_element_type=F32)
    s = s + (row_bias - jnp.where(top, cr_ref[0, 0, kb], cr_ref[0, 1, kb]))
    if masked:
        r = lax.broadcasted_iota(jnp.int32, (2 * blk, blk), 0)
        keep = jnp.where(r >= blk, r - blk, r) >= lax.broadcasted_iota(jnp.int32, (2 * blk, blk), 1)
        s = jnp.where(keep, s, NEG_INF)
    return s


def _fox_fwd(qkv, c_exp, c_row, ex=None):
    b, s, _ = qkv.shape
    blk = min(ATT_BLOCK, s)
    nq = s // blk

    def body(q_ref, kv_ref, cc_ref, cr_ref, o_ref, ob_ref, lse_ref):
        qi = pl.program_id(2)
        q_all = _stack_heads(q_ref[0] * FOX_SCALE)
        cq = _stack_cols(cc_ref[0])

        def step(kb, carry, masked):
            m, l, acc = carry
            rows = pl.ds(pl.multiple_of(kb * blk, blk), blk)
            sc = _fox_scores(q_all, kv_ref[0, rows, :LANES], cq, cr_ref, kb, masked, blk)
            m_new = jnp.maximum(m, jnp.max(sc, axis=-1, keepdims=True))
            p = jnp.exp(sc - m_new)
            alpha = jnp.exp(m - m_new)
            l = alpha * l + jnp.sum(p, axis=-1, keepdims=True)
            acc = alpha * acc + jnp.dot(p.astype(BF16), kv_ref[0, rows, LANES:], preferred_element_type=F32)
            return m_new, l, acc

        init = (jnp.full((2 * blk, 1), NEG_INF, F32), jnp.zeros((2 * blk, 1), F32), jnp.zeros((2 * blk, LANES), F32))
        m, l, acc = step(qi, lax.fori_loop(0, qi, functools.partial(step, masked=False), init), True)
        o = _unstack(acc / l, blk)
        o_ref[0] = o
        ob_ref[0] = o.astype(BF16)
        lse_ref[0] = _unstack(jnp.broadcast_to(m + jnp.log(l), (2 * blk, LANES)), blk)

    tile = pl.BlockSpec((1, blk, LANES), lambda i, h, q: (i, q, h))
    kvspec = pl.BlockSpec((1, s, 2 * LANES), lambda i, h, q: (i, 0, KV_BLK0 + h))
    shape = jax.ShapeDtypeStruct((b, s, FOX_WIDTH), F32)
    return _hosted_call(
        body, ex, name="fox_fwd", out_shape=(shape, jax.ShapeDtypeStruct((b, s, FOX_WIDTH), BF16), shape),
        grid=(b, PAIRS, nq),
        in_specs=[tile, kvspec, tile, pl.BlockSpec((1, 2, nq, 1, blk), lambda i, h, q: (i, h, 0, 0, 0))],
        out_specs=(tile, tile, tile), args=(qkv, qkv, c_exp, c_row))


def _fox_bwd(qkv, c_exp, c_row, lse, o, do, ex=None):
    b, s, _ = qkv.shape
    blk = min(ATT_BLOCK, s)
    nq = s // blk

    def body(q_ref, kv_ref, cc_ref, cr_ref, lse_ref, o_ref, do_ref, dq_ref, dkv_ref, dcq_ref, dc_ref, dk_acc, dv_acc):
        qi = pl.program_id(2)

        @pl.when(qi == 0)
        def _():
            dk_acc[...] = jnp.zeros_like(dk_acc)
            dv_acc[...] = jnp.zeros_like(dv_acc)
            dc_ref[...] = jnp.zeros_like(dc_ref)

        q_all = _stack_heads(q_ref[0] * FOX_SCALE)
        dov = do_ref[0]
        do_all = _stack_heads(dov.astype(BF16))
        delta = jnp.sum(_stack_heads(dov * o_ref[0]), axis=-1, keepdims=True)
        bias = _stack_cols(cc_ref[0]) - _stack_cols(lse_ref[0])

        def step(kb, carry, masked):
            acc, dcq = carry
            rows = pl.ds(pl.multiple_of(kb * blk, blk), blk)
            kblk = kv_ref[0, rows, :LANES]
            p = jnp.exp(_fox_scores(q_all, kblk, bias, cr_ref, kb, masked, blk))
            dp = lax.dot_general(do_all, kv_ref[0, rows, LANES:], NT_DIMS, preferred_element_type=F32)
            ds = p * (dp - delta)
            dsb = ds.astype(BF16)
            dv_acc[rows, :] += lax.dot_general(p.astype(BF16), do_all, TN_DIMS, preferred_element_type=F32)
            dk_acc[rows, :] += lax.dot_general(dsb, q_all, TN_DIMS, preferred_element_type=F32)
            dc_ref[0, 0, kb] -= jnp.sum(ds[:blk], axis=0, keepdims=True)
            dc_ref[0, 1, kb] -= jnp.sum(ds[blk:], axis=0, keepdims=True)
            acc = acc + jnp.dot(dsb, kblk, preferred_element_type=F32)
            return acc, dcq + jnp.sum(ds, axis=-1, keepdims=True)

        init = (jnp.zeros((2 * blk, LANES), F32), jnp.zeros((2 * blk, 1), F32))
        acc, dcq = step(qi, lax.fori_loop(0, qi, functools.partial(step, masked=False), init), True)
        dq_ref[0] = (_unstack(acc, blk) * FOX_SCALE).astype(BF16)
        dcq_ref[0, 0] = jnp.where(lax.broadcasted_iota(jnp.int32, (blk, 2), 1) == 0, dcq[:blk], dcq[blk:])

        @pl.when(qi == nq - 1)
        def _():
            dkv_ref[0, :, :LANES] = dk_acc[...].astype(BF16)
            dkv_ref[0, :, LANES:] = dv_acc[...].astype(BF16)

    tile = pl.BlockSpec((1, blk, LANES), lambda i, h, q: (i, q, h))
    kvspec = pl.BlockSpec((1, s, 2 * LANES), lambda i, h, q: (i, 0, KV_BLK0 + h))
    crow = pl.BlockSpec((1, 2, nq, 1, blk), lambda i, h, q: (i, h, 0, 0, 0))
    return _hosted_call(
        body, ex, name="fox_bwd", grid=(b, PAIRS, nq),
        out_shape=(jax.ShapeDtypeStruct((b, s, FOX_WIDTH), BF16), jax.ShapeDtypeStruct((b, s, 2 * FOX_WIDTH), BF16),
                   jax.ShapeDtypeStruct((b, PAIRS, s, 2), F32), jax.ShapeDtypeStruct(c_row.shape, F32)),
        in_specs=[tile, kvspec, tile, crow, tile, tile, tile],
        out_specs=(tile, pl.BlockSpec((1, s, 2 * LANES), lambda i, h, q: (i, 0, h)),
                   pl.BlockSpec((1, 1, blk, 2), lambda i, h, q: (i, h, q, 0)), crow),
        scratch=[pltpu.VMEM((s, LANES), F32), pltpu.VMEM((s, LANES), F32)],
        args=(qkv, qkv, c_exp, c_row, lse, o, do))


def _sigmoid(z):
    return 1.0 / (1.0 + jnp.exp(-z))


def _mix_fwd(gu, b_gate, y_pool, y_fox):
    t = gu.shape[0]
    bt = _block(t, 256, 16)

    def body(gp_ref, gf_ref, bp_ref, bf_ref, yp_ref, yf_ref, o_ref):
        gp = _sigmoid(gp_ref[...].astype(F32) + bp_ref[...])
        gf = _sigmoid(gf_ref[...].astype(F32) + bf_ref[...])
        o_ref[...] = (gp * yp_ref[...].astype(F32) + gf * yf_ref[...].astype(F32)).astype(BF16)

    col = lambda j: pl.BlockSpec((bt, D_MODEL), lambda i: (i, j))
    vec = lambda j: pl.BlockSpec((1, D_MODEL), lambda i: (0, j))
    return pl.pallas_call(
        body, name="mix_fwd", out_shape=jax.ShapeDtypeStruct((t, D_MODEL), BF16), grid=(t // bt,),
        in_specs=[col(0), col(1), vec(0), vec(1), col(0), col(0)], out_specs=col(0),
        compiler_params=_cparams(("parallel",)),
    )(gu, gu, b_gate, b_gate, y_pool, y_fox)


def _mix_bwd(gu, b_gate, y_pool, y_fox, dmix):
    t = gu.shape[0]
    bt = _block(t, 256, 16)

    def body(gp_ref, gf_ref, bp_ref, bf_ref, yp_ref, yf_ref, dm_ref, dyp_ref, dyf_ref, dgl_ref, db_ref):
        @pl.when(pl.program_id(0) == 0)
        def _():
            db_ref[...] = jnp.zeros_like(db_ref)

        dm = dm_ref[...]
        gp = _sigmoid(gp_ref[...].astype(F32) + bp_ref[...])
        gf = _sigmoid(gf_ref[...].astype(F32) + bf_ref[...])
        dyp_ref[...] = (dm * gp).astype(BF16)
        dyf_ref[...] = (dm * gf).astype(BF16)
        dlp = dm * yp_ref[...].astype(F32) * gp * (1.0 - gp)
        dlf = dm * yf_ref[...].astype(F32) * gf * (1.0 - gf)
        dgl_ref[:, :D_MODEL] = dlp.astype(BF16)
        dgl_ref[:, D_MODEL:] = dlf.astype(BF16)
        db_ref[:, :D_MODEL] += jnp.sum(dlp, axis=0, keepdims=True)
        db_ref[:, D_MODEL:] += jnp.sum(dlf, axis=0, keepdims=True)

    col = lambda j: pl.BlockSpec((bt, D_MODEL), lambda i: (i, j))
    vec = lambda j: pl.BlockSpec((1, D_MODEL), lambda i: (0, j))
    wide = pl.BlockSpec((bt, GATE_WIDTH), lambda i: (i, 0))
    return pl.pallas_call(
        body, name="mix_bwd", grid=(t // bt,),
        out_shape=(jax.ShapeDtypeStruct((t, D_MODEL), BF16), jax.ShapeDtypeStruct((t, D_MODEL), BF16),
                   jax.ShapeDtypeStruct((t, GU_COLS), BF16), jax.ShapeDtypeStruct((1, GATE_WIDTH), F32)),
        in_specs=[col(0), col(1), vec(0), vec(1), col(0), col(0), col(0)],
        out_specs=(col(0), col(0), wide, pl.BlockSpec((1, GATE_WIDTH), lambda i: (0, 0))),
        compiler_params=_cparams(("arbitrary",)),
    )(gu, gu, b_gate, b_gate, y_pool, y_fox, dmix)


X_SCALE = X_DH ** -0.5


def _xattn_probs(qh, kh):
    s = lax.dot_general(qh, kh, NT_DIMS, preferred_element_type=F32) * X_SCALE
    e = jnp.exp(s - jnp.max(s, axis=-1, keepdims=True))
    return e / jnp.sum(e, axis=-1, keepdims=True)


def _xattn_fwd(q, kv):
    b, s, _ = q.shape
    m = kv.shape[1]
    bq = _block(s, 512, 16)

    def body(q_ref, kv_ref, o_ref):
        for h in range(X_HEADS):
            cols = slice(h * X_DH, (h + 1) * X_DH)
            p = _xattn_probs(q_ref[0, :, cols], kv_ref[0, :, cols])
            vh = kv_ref[0, :, X_WIDTH + h * X_DH:X_WIDTH + (h + 1) * X_DH]
            o_ref[0, :, cols] = jnp.dot(p.astype(BF16), vh, preferred_element_type=F32).astype(BF16)

    return pl.pallas_call(
        body, name="xattn_fwd", out_shape=jax.ShapeDtypeStruct((b, s, X_WIDTH), BF16), grid=(b, s // bq),
        in_specs=[pl.BlockSpec((1, bq, X_WIDTH), lambda i, j: (i, j, 0)),
                  pl.BlockSpec((1, m, 2 * X_WIDTH), lambda i, j: (i, 0, 0))],
        out_specs=pl.BlockSpec((1, bq, X_WIDTH), lambda i, j: (i, j, 0)),
        compiler_params=_cparams(("parallel", "parallel")),
    )(q, kv)


def _xattn_bwd(q, kv, do):
    b, s, _ = q.shape
    m = kv.shape[1]
    bq = _block(s, 512, 16)

    def body(q_ref, kv_ref, do_ref, dq_ref, dkv_ref):
        @pl.when(pl.program_id(1) == 0)
        def _():
            dkv_ref[...] = jnp.zeros_like(dkv_ref)

        for h in range(X_HEADS):
            cols = slice(h * X_DH, (h + 1) * X_DH)
            vcols = slice(X_WIDTH + h * X_DH, X_WIDTH + (h + 1) * X_DH)
            qh, kh, vh, doh = q_ref[0, :, cols], kv_ref[0, :, cols], kv_ref[0, :, vcols], do_ref[0, :, cols]
            p = _xattn_probs(qh, kh)
            dkv_ref[0, :, vcols] += lax.dot_general(p.astype(BF16), doh, TN_DIMS, preferred_element_type=F32)
            dp = lax.dot_general(doh, vh, NT_DIMS, preferred_element_type=F32)
            ds = (p * (dp - jnp.sum(p * dp, axis=-1, keepdims=True)) * X_SCALE).astype(BF16)
            dq_ref[0, :, cols] = jnp.dot(ds, kh, preferred_element_type=F32).astype(BF16)
            dkv_ref[0, :, cols] += lax.dot_general(ds, qh, TN_DIMS, preferred_element_type=F32)

    tile = pl.BlockSpec((1, bq, X_WIDTH), lambda i, j: (i, j, 0))
    mem = pl.BlockSpec((1, m, 2 * X_WIDTH), lambda i, j: (i, 0, 0))
    return pl.pallas_call(
        body, name="xattn_bwd", grid=(b, s // bq),
        out_shape=(jax.ShapeDtypeStruct((b, s, X_WIDTH), BF16), jax.ShapeDtypeStruct((b, m, 2 * X_WIDTH), F32)),
        in_specs=[tile, mem, tile], out_specs=(tile, mem),
        compiler_params=_cparams(("parallel", "arbitrary")),
    )(q, kv, do)


def _ffn_in(hf, w3):
    t, d = hf.shape
    chunk = w3.shape[2]
    half = N_CHIPS // 2
    bm = _block(t, 1024, 16)

    def body(a_ref, wg_ref, wu_ref, gt_ref, up_ref, act_ref):
        a = a_ref[...]
        gt = jnp.dot(a, wg_ref[...], preferred_element_type=F32).astype(BF16)
        up = jnp.dot(a, wu_ref[...], preferred_element_type=F32).astype(BF16)
        gt_ref[...] = gt
        up_ref[...] = up
        g32 = gt.astype(F32)
        act_ref[...] = (g32 * _sigmoid(g32) * up.astype(F32)).astype(BF16)

    tile = pl.BlockSpec((bm, chunk), lambda i, j: (i, j))
    shape = jax.ShapeDtypeStruct((t, half * chunk), BF16)
    return pl.pallas_call(
        body, name="ffn_in", out_shape=(shape, shape, shape), grid=(t // bm, half),
        in_specs=[pl.BlockSpec((bm, d), lambda i, j: (i, 0)),
                  pl.BlockSpec((None, d, chunk), lambda i, j: (j, 0, 0)),
                  pl.BlockSpec((None, d, chunk), lambda i, j: (j + half, 0, 0))],
        out_specs=(tile, tile, tile),
        compiler_params=_cparams(("parallel", "arbitrary")),
    )(hf, w3, w3)


def _swiglu_bwd(gate, up, dact):
    t = gate.shape[0]
    bt = _block(t, 256, 16)

    def body(gt_ref, up_ref, da_ref, o_ref):
        gt = gt_ref[...].astype(F32)
        da = da_ref[...].astype(F32)
        sg = _sigmoid(gt)
        silu = gt * sg
        o_ref[:, :D_FF] = (da * up_ref[...].astype(F32) * (sg + silu * (1.0 - sg))).astype(BF16)
        o_ref[:, D_FF:] = (da * silu).astype(BF16)

    col = pl.BlockSpec((bt, D_FF), lambda i: (i, 0))
    return pl.pallas_call(
        body, name="swiglu_bwd", out_shape=jax.ShapeDtypeStruct((t, 2 * D_FF), BF16), grid=(t // bt,),
        in_specs=[col, col, col], out_specs=pl.BlockSpec((bt, 2 * D_FF), lambda i: (i, 0)),
        compiler_params=_cparams(("parallel",)),
    )(gate, up, dact)


def _stack_of(w, axis):
    r, c = w.shape
    if axis == 0:
        return w.reshape(N_CHIPS, r // N_CHIPS, c)
    return w.reshape(r, N_CHIPS, c // N_CHIPS).transpose(1, 0, 2)


def _stack_t(w3):
    n, r, c = w3.shape
    return w3.transpose(0, 2, 1).reshape(n * c, r)


def _pair_rows(k, v):
    c = k.shape[1]
    return jnp.stack([k.reshape(PAIRS, LANES, c), v.reshape(PAIRS, LANES, c)], axis=1).reshape(2 * FOX_WIDTH, c)


def _unpair_rows(kv):
    c = kv.shape[1]
    kv = kv.reshape(PAIRS, 2, LANES, c)
    return kv[:, 0].reshape(FOX_WIDTH, c), kv[:, 1].reshape(FOX_WIDTH, c)


def _input_grad(parts, weights_t, ex):
    t = parts[0].shape[0]
    d = weights_t[0].shape[1]
    bm = _block(t, 512, 16)
    n = len(parts)

    def body(*refs):
        acc = None
        for a_ref, b_ref in zip(refs[:n], refs[n:2 * n]):
            term = jnp.dot(a_ref[...], b_ref[...], preferred_element_type=F32)
            acc = term if acc is None else acc + term
        refs[2 * n][...] = acc

    (out,), moved = _hosted_call(
        body, ex, name="d_h", grid=(t // bm,), out_shape=(jax.ShapeDtypeStruct((t, d), F32),),
        in_specs=[pl.BlockSpec((bm, p.shape[1]), lambda i: (i, 0)) for p in parts]
        + [pl.BlockSpec(w.shape, lambda i: (0, 0)) for w in weights_t],
        out_specs=(pl.BlockSpec((bm, d), lambda i: (i, 0)),), args=tuple(parts) + tuple(weights_t))
    return out, moved


def _step(x, mem, loss_target, weights, moments_m, moments_v):
    nb, s, d = x.shape
    n_mem = mem.shape[1]
    t = nb * s
    blk = min(ATT_BLOCK, s)
    x2 = x.reshape(t, d)
    mem2 = mem.reshape(nb * n_mem, d)
    tgt2 = loss_target.reshape(t, d)

    def shard2d(a, n):
        a = a.reshape(a.shape[1:])
        return a.T if n == "w_in" else a

    def unshard(a, n):
        return (a.T if n == "w_in" else a)[None]

    local = {n: shard2d(weights[n], n) for n, _, _ in SHARDED}

    names = [n for n, _, _ in SHARDED]
    last = ["w_ffn_out"]
    later = [n for n in names if n != "w_in" and n not in last]
    local_b = {n: local[n].astype(BF16) for n in names}
    g_mix = weights["norm_mix_g"]
    h, w_in_others = _rms_fwd(x2, g_mix, "norm_mix", ex=_gather_exchange([local_b["w_in"]]))
    w_in_stack, = _place_own(w_in_others, [local_b["w_in"]])

    def w_in_rows(lo, hi):
        per = IN_COLS // N_CHIPS
        parts = [w_in_stack[j, max(lo, j * per) - j * per:min(hi, (j + 1) * per) - j * per]
                 for j in range(N_CHIPS) if max(lo, j * per) < min(hi, (j + 1) * per)]
        return parts[0] if len(parts) == 1 else jnp.concatenate(parts)

    w_gu_t = jnp.concatenate([w_in_rows(2056, IN_COLS), w_in_rows(0, 512)])
    w_qkv_t = jnp.concatenate([w_in_rows(512, 1024), _pair_rows(w_in_rows(1024, 1536), w_in_rows(1536, 2048))])
    w_f_t = jnp.pad(w_in_rows(2048, 2056), ((0, LANES - FOX_HEADS), (0, 0)))
    w_gu, w_qkv, w_f = w_gu_t.T, w_qkv_t.T, w_f_t.T
    w_f_exp = jnp.repeat(w_f[:, :FOX_HEADS], FOX_DH, axis=1)

    g_mix, g_x, g_mem, g_ffn = (weights[n] for n in ("norm_mix_g", "norm_x_g", "norm_mem_g", "norm_ffn_g"))
    g_final = weights["norm_final_g"].reshape(1, d)
    pool_w = weights["pool_w"].reshape(4, POOL_GC, POOL_GC)
    pool_scale, b_gate = weights["pool_scale"], weights["b_gate"]
    b_f_pad = jnp.pad(weights["b_forget"], ((0, 0), (0, LANES - FOX_HEADS)))
    b_f_exp = jnp.repeat(weights["b_forget"], FOX_DH, axis=1)

    gu, last_others = _mm(h, w_gu, out_dtype=BF16, bn=512, name="in_proj_gates_pool",
                          ex=_gather_exchange([local_b[n] for n in last]))
    qkv = _mm(h, w_qkv, out_dtype=BF16, bn=512, name="in_proj_qkv")
    f_pad = _mm(h, w_f, name="in_proj_forget")
    gu3, qkv3 = gu.reshape(nb, s, GU_COLS), qkv.reshape(nb, s, 3 * FOX_WIDTH)
    y = _pool_fwd(gu3, pool_w, pool_scale)
    f_exp = _mm(h, w_f_exp, name="in_proj_forget_lanes").reshape(nb, s, FOX_WIDTH)
    c_exp = _forget_cumsum(f_exp, b_f_exp, "forget_cumsum_lanes")
    c_pad = _forget_cumsum(f_pad.reshape(nb, s, LANES), b_f_pad, "forget_cumsum")
    c_row = c_pad[:, :, :FOX_HEADS].transpose(0, 2, 1).reshape(nb, FOX_HEADS, s // blk, 1, blk)
    (o, o_b, lse), gathered = _fox_fwd(qkv3, c_exp, c_row, ex=_gather_exchange([local_b[n] for n in later]))
    stacks = dict(zip(later, _place_own(gathered, [local_b[n] for n in later])))
    stacks.update(zip(last, _place_own(last_others, [local_b[n] for n in last])))
    w_pool_out3, w_fox_out3, w_xo3, w_ffn_in3 = (stacks[n] for n in ("w_pool_out", "w_fox_out", "w_xo", "w_ffn_in"))
    w_out, w_xq, w_xkv, w_ffn_out = (stacks[n].reshape(-1, stacks[n].shape[2])
                                     for n in ("w_out", "w_xq", "w_xkv", "w_ffn_out"))
    y2, o2 = y.reshape(t, POOL_WIDTH), o_b.reshape(t, FOX_WIDTH)
    y_pool = _mm(y2, w_pool_out3, b_stack=True, out_dtype=BF16, name="pool_out")
    y_fox = _mm(o2, w_fox_out3, b_stack=True, out_dtype=BF16, name="fox_out")
    mix = _mix_fwd(gu, b_gate, y_pool, y_fox)
    x1 = _mm(mix, w_out, res=x2, name="mix_out")
    hx = _rms_fwd(x1, g_x, "norm_x")
    mem_n = _rms_fwd(mem2, g_mem, "norm_mem")
    qx = _mm(hx, w_xq, out_dtype=BF16, name="x_q")
    kv = _mm(mem_n, w_xkv, out_dtype=BF16, name="x_kv")
    qx3, kv3 = qx.reshape(nb, s, X_WIDTH), kv.reshape(nb, n_mem, 2 * X_WIDTH)
    ox = _xattn_fwd(qx3, kv3).reshape(t, X_WIDTH)
    x2_ = _mm(ox, w_xo3, b_stack=True, res=x1, name="x_out")
    hf = _rms_fwd(x2_, g_ffn, "norm_ffn")
    ffn_gate, ffn_up, act = _ffn_in(hf, w_ffn_in3)
    x3 = _mm(act, w_ffn_out, res=x2_, name="ffn_out")

    dx3, dx3_b, dg_final, loss_part = _final_loss(x3, tgt2, g_final)
    dw_ffn_out = _mm(act, dx3_b, ta=True, bm=1408, bn=512, bk=2048, name="d_w_ffn_out")
    dact = _mm(dx3_b, w_ffn_out.T, out_dtype=BF16, bm=2048, bn=1408, name="d_act")
    dffn = _swiglu_bwd(ffn_gate, ffn_up, dact)
    dw_ffn_in = _mm(hf, dffn, ta=True, bm=512, bn=1408, bk=2048, out_stack=True, name="d_w_ffn_in")
    core = lax.axis_index("c").astype(jnp.int32).reshape(1)
    ffn_group = ["w_ffn_in", "w_ffn_out"]
    mid_group = ["w_pool_out", "w_fox_out", "w_out", "w_xq", "w_xkv", "w_xo"]
    grad_stacks = {"w_ffn_in": dw_ffn_in, "w_ffn_out": _stack_of(dw_ffn_out, 0)}

    def presum(group, theirs):
        return [_sum_halves(grad_stacks[n], t_, core, "sum_halves_" + n) for n, t_ in zip(group, theirs)]

    dhf, theirs = _mm(dffn, _stack_t(w_ffn_in3), bk=2816, name="d_hf",
                      ex=_swap_exchange([grad_stacks[n] for n in ffn_group]))
    chip_sums = dict(zip(ffn_group, presum(ffn_group, theirs)))
    dx2, dx2_b, dg_ffn = _rms_bwd(dhf, x2_, g_ffn, dx3, "norm_ffn_bwd")

    dw_xo = _mm(ox, dx2_b, ta=True, bn=256, out_stack=True, name="d_w_xo")
    dox = _mm(dx2_b, _stack_t(w_xo3), out_dtype=BF16, name="d_ox").reshape(nb, s, X_WIDTH)
    dqx, dkv = _xattn_bwd(qx3, kv3, dox)
    dqx2, dkv2 = dqx.reshape(t, X_WIDTH), dkv.reshape(nb * n_mem, 2 * X_WIDTH)
    dw_xkv = _mm(mem_n, dkv2, ta=True, name="d_w_xkv")
    dmem_n = _mm(dkv2, w_xkv.T, name="d_mem_n")
    dg_mem = _rms_bwd(dmem_n, mem2, g_mem, None, "norm_mem_bwd")
    dw_xq = _mm(hx, dqx2, ta=True, name="d_w_xq")
    dhx = _mm(dqx2, w_xq.T, name="d_hx")
    dx1, dx1_b, dg_x = _rms_bwd(dhx, x1, g_x, dx2, "norm_x_bwd")

    dw_out = _mm(mix, dx1_b, ta=True, name="d_w_out")
    dmix = _mm(dx1_b, w_out.T, name="d_mix")
    dyp, dyf, dgu, db_gate = _mix_bwd(gu, b_gate, y_pool, y_fox, dmix)
    dw_pool_out = _mm(y2, dyp, ta=True, bn=256, out_stack=True, name="d_w_pool_out")
    dw_fox_out = _mm(o2, dyf, ta=True, bn=256, out_stack=True, name="d_w_fox_out")
    dy = _mm(dyp, _stack_t(w_pool_out3), name="d_y").reshape(nb, s, POOL_WIDTH)
    do = _mm(dyf, _stack_t(w_fox_out3), name="d_o").reshape(nb, s, FOX_WIDTH)
    dgu3, dpool_w, dpool_scale = _pool_bwd(gu3, dy, pool_w, pool_scale, dgu.reshape(nb, s, GU_COLS))
    grad_stacks.update({"w_pool_out": dw_pool_out, "w_fox_out": dw_fox_out, "w_out": _stack_of(dw_out, 0),
                        "w_xq": _stack_of(dw_xq, 0), "w_xkv": _stack_of(dw_xkv, 0), "w_xo": dw_xo})
    dgu2 = dgu3.reshape(t, GU_COLS)
    dw_gu_t, theirs = _mm(dgu2, h, ta=True, name="d_w_gates_pool",
                          ex=_swap_exchange([grad_stacks[n] for n in mid_group]))
    chip_sums.update(zip(mid_group, presum(mid_group, theirs)))
    early = ffn_group + mid_group
    (dq3, dkv3, dc_q, dc_row), early_slots = _fox_bwd(qkv3, c_exp, c_row, lse, o, do,
                                                      ex=_chips_exchange([chip_sums[n] for n in early]))
    slots = dict(zip(early, early_slots))
    dc = dc_row.reshape(nb, FOX_HEADS, s).transpose(0, 2, 1) + dc_q.transpose(0, 2, 1, 3).reshape(nb, s, FOX_HEADS)
    dc = jnp.pad(dc, ((0, 0), (0, 0), (0, LANES - FOX_HEADS)))
    df, db_f = _forget_bwd(dc, f_pad.reshape(nb, s, LANES), b_f_pad)
    dq2, dkv2, df2 = dq3.reshape(t, FOX_WIDTH), dkv3.reshape(t, 2 * FOX_WIDTH), df.reshape(t, LANES)
    dw_q_t = _mm(dq2, h, ta=True, name="d_w_q")
    dw_kv_t = _mm(dkv2, h, ta=True, name="d_w_kv")
    dw_f_t = _mm(df2, h, ta=True, name="d_w_forget")
    dw_k_t, dw_v_t = _unpair_rows(dw_kv_t)
    dw_in_t = jnp.concatenate([dw_gu_t[GATE_WIDTH:], dw_q_t, dw_k_t, dw_v_t, dw_f_t[:FOX_HEADS],
                               dw_gu_t[:GATE_WIDTH]])
    grad_stacks["w_in"] = dw_in_t.reshape(N_CHIPS, IN_COLS // N_CHIPS, D_MODEL)
    chip_sums["w_in"], = presum(["w_in"], _run_exchange(_swap_exchange([grad_stacks["w_in"]]), "swap_halves_w_in"))
    dh, (slots["w_in"],) = _input_grad([dgu2, dq2, dkv2, df2],
                                       [w_gu_t, w_qkv_t[:FOX_WIDTH], w_qkv_t[FOX_WIDTH:], w_f_t],
                                       _chips_exchange([chip_sums["w_in"]]))

    place = jnp.stack([lax.axis_index("c"), 2 * lax.axis_index("x") + lax.axis_index("y")]).astype(jnp.int32)
    halves = [_sum_chips(slots[n], chip_sums[n], place, _by_rows(local[n].shape[0]), "sum_chips_" + n) for n in names]
    (dx, _, dg_mix), reduced = _rms_bwd(dh, x2, g_mix, dx1, "norm_mix_bwd", ex=_join_exchange(halves))

    small_grads = {"norm_mix_g": dg_mix, "b_forget": db_f[:, :FOX_HEADS], "b_gate": db_gate, "pool_w": dpool_w,
                   "pool_scale": dpool_scale, "norm_x_g": dg_x, "norm_mem_g": dg_mem, "norm_ffn_g": dg_ffn,
                   "norm_final_g": dg_final}
    def flat2d(a):
        return a.reshape(-1, a.shape[-1])

    small_names = [n for n, _ in SMALL]
    own = [flat2d(small_grads[n]) for n in small_names]
    small_gather = _small_exchange(own + [loss_part])

    def tiles_of(a):
        return a.transpose(2, 0, 1)

    def block_of(a3):
        return a3.transpose(1, 2, 0)

    grads, deltas, new_m, new_v = {}, {}, {}, {}
    gathered = None
    for n, g_ in zip(names, reduced):
        if n == "w_in":
            g_ = lax.optimization_barrier(g_.reshape(IN_COLS // N_CHIPS, 1, D_MODEL))
            (d_, m_, v_), gathered = _adamw(tiles_of(weights[n]), g_, tiles_of(moments_m[n]), tiles_of(moments_v[n]),
                                            "adamw_" + n, ex=small_gather)
            back = block_of
        else:
            d_, m_, v_ = _adamw(local[n], g_, shard2d(moments_m[n], n), shard2d(moments_v[n], n), "adamw_" + n)
            back = functools.partial(unshard, n=n)
        grads[n], deltas[n], new_m[n], new_v[n] = (back(a) for a in (g_, d_, m_, v_))

    device = (4 * lax.axis_index("x") + 2 * lax.axis_index("y") + lax.axis_index("c")).astype(jnp.int32).reshape(1)
    sg, sd, sm, sv, loss_sum = _adamw_small(
        gathered[:-1], own, [flat2d(weights[n]) for n in small_names], [flat2d(moments_m[n]) for n in small_names],
        [flat2d(moments_v[n]) for n in small_names], gathered[-1], loss_part, device)
    for n, g_, d_, m_, v_ in zip(small_names, sg, sd, sm, sv):
        grads[n], deltas[n], new_m[n], new_v[n] = (a.reshape(weights[n].shape) for a in (g_, d_, m_, v_))
    return loss_sum[0, 0], dx.reshape(nb, s, d), grads, deltas, new_m, new_v


def kernel(x, mem, norm_mix_g, w_in, b_forget, b_gate, pool_w, pool_scale, w_pool_out, w_fox_out, w_out, norm_x_g, norm_mem_g, w_xq, w_xkv, w_xo, norm_ffn_g, w_ffn_in, w_ffn_out, norm_final_g, loss_target, m_norm_mix_g, m_w_in, m_b_forget, m_b_gate, m_pool_w, m_pool_scale, m_w_pool_out, m_w_fox_out, m_w_out, m_norm_x_g, m_norm_mem_g, m_w_xq, m_w_xkv, m_w_xo, m_norm_ffn_g, m_w_ffn_in, m_w_ffn_out, m_norm_final_g, v_norm_mix_g, v_w_in, v_b_forget, v_b_gate, v_pool_w, v_pool_scale, v_w_pool_out, v_w_fox_out, v_w_out, v_norm_x_g, v_norm_mem_g, v_w_xq, v_w_xkv, v_w_xo, v_norm_ffn_g, v_w_ffn_in, v_w_ffn_out, v_norm_final_g):
    given = dict(locals())
    weights = {n: given[n] for n in WEIGHT_ORDER}
    moments_m = {n: given["m_" + n] for n in WEIGHT_ORDER}
    moments_v = {n: given["v_" + n] for n in WEIGHT_ORDER}
    loss, grad_x, grads, deltas, new_m, new_v = _step(x, mem, loss_target, weights, moments_m, moments_v)
    return (loss, grad_x, *[grads[n] for n in WEIGHT_ORDER], *[deltas[n] for n in WEIGHT_ORDER],
            *[new_m[n] for n in WEIGHT_ORDER], *[new_v[n] for n in WEIGHT_ORDER])
```

```python
import functools
import math

import jax
import jax.numpy as jnp
from jax import lax
from jax.experimental import pallas as pl
from jax.experimental.pallas import tpu as pltpu

F32 = jnp.float32
BF16 = jnp.bfloat16
MESH = pl.DeviceIdType.MESH

D_MODEL = 1024
EPS = 1e-6
POOL_WINDOWS = (2, 4, 8, 16)
POOL_WIDTH = 512
POOL_GC = 128
FOX_HEADS = 8
FOX_DH = 64
FOX_WIDTH = 512
X_HEADS = 4
X_DH = 128
X_WIDTH = 512
D_FF = 2816
IN_COLS = 4104
GATE_WIDTH = 2048
ADAM_LR = 0.001
ADAM_B1 = 0.9
ADAM_B2 = 0.999
ADAM_EPS = 1e-08
ADAM_WD = 0.01
ADAM_STEP = 10

N_CHIPS = 4
N_DEV = 8
LANES = 128
VMEM_LIMIT_BYTES = 56 * 1024 * 1024
NEG_INF = -1e30
ATT_BLOCK = 512

SHARDED = (
    ("w_in", (1024, IN_COLS), 1),
    ("w_pool_out", (POOL_WIDTH, 1024), 1),
    ("w_fox_out", (FOX_WIDTH, 1024), 1),
    ("w_out", (1024, 1024), 0),
    ("w_xq", (1024, X_WIDTH), 0),
    ("w_xkv", (1024, 2 * X_WIDTH), 0),
    ("w_xo", (X_WIDTH, 1024), 1),
    ("w_ffn_in", (1024, 2 * D_FF), 1),
    ("w_ffn_out", (D_FF, 1024), 0),
)
SMALL = (
    ("norm_mix_g", (1, 1024)),
    ("b_forget", (1, 8)),
    ("b_gate", (1, 2048)),
    ("pool_w", (1, 4, 128, 128)),
    ("pool_scale", (1, 512)),
    ("norm_x_g", (1, 1024)),
    ("norm_mem_g", (1, 1024)),
    ("norm_ffn_g", (1, 1024)),
    ("norm_final_g", (1024,)),
)
WEIGHT_ORDER = ("norm_mix_g", "w_in", "b_forget", "b_gate", "pool_w", "pool_scale", "w_pool_out", "w_fox_out", "w_out",
                "norm_x_g", "norm_mem_g", "w_xq", "w_xkv", "w_xo", "norm_ffn_g", "w_ffn_in", "w_ffn_out", "norm_final_g")


def _cparams(sem=None):
    return pltpu.CompilerParams(dimension_semantics=sem, vmem_limit_bytes=VMEM_LIMIT_BYTES)


def _block(dim, pref, unit):
    if dim <= pref:
        return dim
    best = None
    for b in range(unit, pref + 1, unit):
        if dim % b == 0:
            best = b
    assert best is not None, (dim, pref, unit)
    return best


def _rows_block(rows, cols, unit=16, elems=1 << 19):
    return _block(rows, max(unit, elems // cols // unit * unit), unit)


def _my_place():
    return lax.axis_index("x"), lax.axis_index("y"), lax.axis_index("c")


def _other_chips(x, y):
    return [(1 - x, y), (x, 1 - y), (1 - x, 1 - y)]


def _chip(place):
    return 2 * place[0] + place[1]


ANY = pl.BlockSpec(memory_space=pl.ANY)


def _by_rows(rows):
    return rows % 32 == 0


def _half_shape(rows, cols):
    return (rows // 2, cols) if _by_rows(rows) else (rows, cols // 2)


def _core_half(ref, core, lead=()):
    rows, cols = ref.shape[-2:]
    if _by_rows(rows):
        return ref.at[(*lead, pl.ds(core * (rows // 2), rows // 2), slice(None))]
    return ref.at[(*lead, slice(None), pl.ds(core * (cols // 2), cols // 2))]


class _Exchange:
    def __init__(self, arrays, out_shapes, n_sems, start, finish, in_place=False):
        self.arrays, self.out_shapes, self.n_sems, self.start, self.finish = arrays, out_shapes, n_sems, start, finish
        self.in_place = in_place

    def scratch(self):
        return [pltpu.SemaphoreType.DMA((self.n_sems,)), pltpu.SemaphoreType.DMA((self.n_sems,))]

    def aliases(self, first_in, first_out):
        return {first_in + k: first_out + k for k in range(len(self.arrays))} if self.in_place else {}


def _run_exchange(ex, name):
    n = len(ex.arrays)

    def body(*refs):
        ins, outs, sems = refs[:n], refs[n:2 * n], refs[2 * n:]
        ex.start(ins, outs, *sems)
        ex.finish(ins, outs, *sems)

    return pl.pallas_call(
        body, name=name, out_shape=ex.out_shapes, in_specs=[ANY] * n, out_specs=[ANY] * n, scratch_shapes=ex.scratch(),
        input_output_aliases=ex.aliases(0, 0),
    )(*ex.arrays)


def _hosted_call(body, ex, *, name, grid, in_specs, out_specs, out_shape, args, scratch=()):
    n_in, n_out, n_scr = len(args), len(out_shape), len(scratch)
    if ex is None:
        outs = pl.pallas_call(
            body, name=name, grid=grid, out_shape=out_shape, in_specs=in_specs, out_specs=out_specs,
            scratch_shapes=list(scratch), compiler_params=_cparams(("arbitrary",) * len(grid)))(*args)
        return outs, None
    nc = len(ex.arrays)

    def full_body(*refs):
        ins, cins = refs[:n_in], refs[n_in:n_in + nc]
        outs, couts = refs[n_in + nc:n_in + nc + n_out], refs[n_in + nc + n_out:n_in + 2 * nc + n_out]
        rest = refs[n_in + 2 * nc + n_out:]
        scr, sems = rest[:n_scr], rest[n_scr:]
        first = functools.reduce(jnp.logical_and, [pl.program_id(a) == 0 for a in range(len(grid))])
        last = functools.reduce(jnp.logical_and, [pl.program_id(a) == grid[a] - 1 for a in range(len(grid))])

        @pl.when(first)
        def _():
            ex.start(cins, couts, *sems)

        body(*ins, *outs, *scr)

        @pl.when(last)
        def _():
            ex.finish(cins, couts, *sems)

    outs = pl.pallas_call(
        full_body, name=name, grid=grid, out_shape=list(out_shape) + list(ex.out_shapes),
        in_specs=list(in_specs) + [ANY] * nc, out_specs=list(out_specs) + [ANY] * nc,
        scratch_shapes=list(scratch) + ex.scratch(), input_output_aliases=ex.aliases(n_in, n_out),
        compiler_params=_cparams(("arbitrary",) * len(grid)))(*args, *ex.arrays)
    return outs[:n_out], outs[n_out:]


def _gather_exchange(shards):
    n = len(shards)

    def copies(ins, outs, send_sems, recv_sems):
        x, y, c = _my_place()

        def half(k, chip, core):
            return _core_half(outs[k], core, lead=(_chip(chip),))

        def copy(k, slot, chip, core, to, src=None):
            return pltpu.make_async_remote_copy(
                src_ref=half(k, chip, core) if src is None else src, dst_ref=half(k, chip, core),
                send_sem=send_sems.at[6 * k + slot], recv_sem=recv_sems.at[6 * k + slot],
                device_id=to, device_id_type=MESH)

        return (x, y, c), copy

    def first_copies(ins, outs, send_sems, recv_sems):
        (x, y, c), copy = copies(ins, outs, send_sems, recv_sems)
        out = []
        for j, chip in enumerate(_other_chips(x, y)):
            for k in range(n):
                out.append(copy(k, j, (x, y), c, (*chip, c), src=_core_half(ins[k], c)))
        return out

    def start(ins, outs, send_sems, recv_sems):
        for cp in first_copies(ins, outs, send_sems, recv_sems):
            cp.start()

    def finish(ins, outs, send_sems, recv_sems):
        (x, y, c), copy = copies(ins, outs, send_sems, recv_sems)
        chips = _other_chips(x, y)
        passed = []
        for j, chip in enumerate(chips):
            for k in range(n):
                copy(k, j, chip, c, (x, y, c)).wait_recv()
                passed.append(copy(k, 3 + j, chip, c, (x, y, 1 - c)))
                passed[-1].start()
        for j, chip in enumerate(chips):
            for k in range(n):
                copy(k, 3 + j, chip, 1 - c, (x, y, c)).wait_recv()
        for cp in first_copies(ins, outs, send_sems, recv_sems) + passed:
            cp.wait_send()

    return _Exchange(list(shards), [jax.ShapeDtypeStruct((N_CHIPS,) + s.shape, s.dtype) for s in shards], 6 * n,
                     start, finish)


def _place_own(stacks, shards):
    me = 2 * lax.axis_index("x") + lax.axis_index("y")
    return [lax.dynamic_update_slice(others, mine[None], (me, 0, 0)) for others, mine in zip(stacks, shards)]


def _swap_exchange(grads):
    n = len(grads)

    def copies(ins, outs, send_sems, recv_sems):
        x, y, c = _my_place()
        return [pltpu.make_async_remote_copy(
            src_ref=_core_half(ins[k], 1 - c, lead=(slice(None),)), dst_ref=outs[k],
            send_sem=send_sems.at[k], recv_sem=recv_sems.at[k], device_id=(x, y, 1 - c), device_id_type=MESH)
            for k in range(n)]

    def start(ins, outs, send_sems, recv_sems):
        for cp in copies(ins, outs, send_sems, recv_sems):
            cp.start()

    def finish(ins, outs, send_sems, recv_sems):
        for cp in copies(ins, outs, send_sems, recv_sems):
            cp.wait()

    return _Exchange(list(grads), [jax.ShapeDtypeStruct((N_CHIPS,) + _half_shape(*g.shape[1:]), g.dtype) for g in grads],
                     n, start, finish)


def _chips_exchange(sums):
    n = len(sums)

    def sends(ins, outs, send_sems, recv_sems):
        x, y, c = _my_place()
        return [pltpu.make_async_remote_copy(
            src_ref=ins[k].at[_chip(chip)], dst_ref=outs[k].at[_chip((x, y))],
            send_sem=send_sems.at[3 * k + j], recv_sem=recv_sems.at[3 * k + j],
            device_id=(*chip, c), device_id_type=MESH)
            for j, chip in enumerate(_other_chips(x, y)) for k in range(n)]

    def start(ins, outs, send_sems, recv_sems):
        for cp in sends(ins, outs, send_sems, recv_sems):
            cp.start()

    def finish(ins, outs, send_sems, recv_sems):
        x, y, c = _my_place()
        for j, chip in enumerate(_other_chips(x, y)):
            for k in range(n):
                slot = outs[k].at[_chip(chip)]
                pltpu.make_async_remote_copy(
                    src_ref=slot, dst_ref=slot, send_sem=send_sems.at[3 * k + j], recv_sem=recv_sems.at[3 * k + j],
                    device_id=(x, y, c), device_id_type=MESH).wait_recv()
        for cp in sends(ins, outs, send_sems, recv_sems):
            cp.wait_send()

    return _Exchange(list(sums), [jax.ShapeDtypeStruct(s.shape, s.dtype) for s in sums], 3 * n, start, finish)


def _join_exchange(shards):
    n = len(shards)

    def sends(ins, outs, send_sems, recv_sems):
        x, y, c = _my_place()
        return [pltpu.make_async_remote_copy(
            src_ref=_core_half(ins[k], c), dst_ref=_core_half(outs[k], c),
            send_sem=send_sems.at[k], recv_sem=recv_sems.at[k], device_id=(x, y, 1 - c), device_id_type=MESH)
            for k in range(n)]

    def start(ins, outs, send_sems, recv_sems):
        for cp in sends(ins, outs, send_sems, recv_sems):
            cp.start()

    def finish(ins, outs, send_sems, recv_sems):
        x, y, c = _my_place()
        for k in range(n):
            theirs = _core_half(outs[k], 1 - c)
            pltpu.make_async_remote_copy(
                src_ref=theirs, dst_ref=theirs, send_sem=send_sems.at[k], recv_sem=recv_sems.at[k],
                device_id=(x, y, c), device_id_type=MESH).wait_recv()
        for cp in sends(ins, outs, send_sems, recv_sems):
            cp.wait_send()

    return _Exchange(list(shards), [jax.ShapeDtypeStruct(s.shape, s.dtype) for s in shards], n, start, finish,
                     in_place=True)


def _small_exchange(blocks):
    n = len(blocks)

    def copies(ins, outs, send_sems, recv_sems):
        x, y, c = _my_place()

        def copy(k, j, whose, to, src=None):
            slot = outs[k].at[4 * whose[0] + 2 * whose[1] + whose[2]]
            return pltpu.make_async_remote_copy(
                src_ref=slot if src is None else src, dst_ref=slot,
                send_sem=send_sems.at[7 * k + j], recv_sem=recv_sems.at[7 * k + j], device_id=to, device_id_type=MESH)

        return (x, y, c), copy

    def first_copies(ins, outs, send_sems, recv_sems):
        (x, y, c), copy = copies(ins, outs, send_sems, recv_sems)
        out = []
        for k in range(n):
            out.append(copy(k, 0, (x, y, c), (x, y, 1 - c), src=ins[k]))
            out += [copy(k, 1 + j, (x, y, c), (*chip, c), src=ins[k]) for j, chip in enumerate(_other_chips(x, y))]
        return out

    def start(ins, outs, send_sems, recv_sems):
        for cp in first_copies(ins, outs, send_sems, recv_sems):
            cp.start()

    def finish(ins, outs, send_sems, recv_sems):
        (x, y, c), copy = copies(ins, outs, send_sems, recv_sems)
        chips = _other_chips(x, y)
        passed = []
        for j, chip in enumerate(chips):
            for k in range(n):
                copy(k, 1 + j, (*chip, c), (x, y, c)).wait_recv()
                passed.append(copy(k, 4 + j, (*chip, c), (x, y, 1 - c)))
                passed[-1].start()
        for k in range(n):
            copy(k, 0, (x, y, 1 - c), (x, y, c)).wait_recv()
        for j, chip in enumerate(chips):
            for k in range(n):
                copy(k, 4 + j, (*chip, 1 - c), (x, y, c)).wait_recv()
        for cp in first_copies(ins, outs, send_sems, recv_sems) + passed:
            cp.wait_send()

    return _Exchange(list(blocks), [jax.ShapeDtypeStruct((N_DEV,) + blk.shape, blk.dtype) for blk in blocks], 7 * n,
                     start, finish)


def _sum_halves(grads, theirs, core, name):
    _, h, cols = theirs.shape
    by_rows = _by_rows(grads.shape[1])
    br = _rows_block(h, cols) if by_rows else h
    nb = h // br

    def body(core_ref, a_ref, b_ref, o_ref):
        o_ref[...] = (a_ref[...] + b_ref[...]).astype(BF16)

    if by_rows:
        mine = pl.BlockSpec((1, br, cols), lambda j, i, core_ref: (j, core_ref[0] * nb + i, 0))
    else:
        mine = pl.BlockSpec((1, br, cols), lambda j, i, core_ref: (j, i, core_ref[0]))
    return pl.pallas_call(
        body, name=name,
        out_shape=jax.ShapeDtypeStruct(theirs.shape, BF16),
        grid_spec=pltpu.PrefetchScalarGridSpec(
            num_scalar_prefetch=1, grid=(N_CHIPS, nb),
            in_specs=[mine, pl.BlockSpec((1, br, cols), lambda j, i, core_ref: (j, i, 0))],
            out_specs=pl.BlockSpec((1, br, cols), lambda j, i, core_ref: (j, i, 0))),
        compiler_params=_cparams(("parallel", "parallel")),
    )(core, grads, theirs)


def _sum_chips(slots, sums, place, by_rows, name):
    _, h, cols = slots.shape
    br = _rows_block(h, cols) if by_rows else h
    nb = h // br

    def body(place_ref, s_ref, own_ref, o_ref):
        me = place_ref[1]
        acc = None
        for k in range(N_CHIPS):
            term = jnp.where(me == k, own_ref[k], s_ref[k]).astype(F32)
            acc = term if acc is None else acc + term
        o_ref[...] = acc

    stack = pl.BlockSpec((N_CHIPS, br, cols), lambda i, place_ref: (0, i, 0))
    if by_rows:
        out_shape, out_map = (2 * h, cols), lambda i, place_ref: (place_ref[0] * nb + i, 0)
    else:
        out_shape, out_map = (h, 2 * cols), lambda i, place_ref: (i, place_ref[0])
    return pl.pallas_call(
        body, name=name,
        out_shape=jax.ShapeDtypeStruct(out_shape, F32),
        grid_spec=pltpu.PrefetchScalarGridSpec(
            num_scalar_prefetch=1, grid=(nb,), in_specs=[stack, stack],
            out_specs=pl.BlockSpec((br, cols), out_map)),
        compiler_params=_cparams(("parallel",)),
    )(place, slots, sums)


def _adamw_math(w, g, m, v):
    m = ADAM_B1 * m + (1.0 - ADAM_B1) * g
    v = ADAM_B2 * v + (1.0 - ADAM_B2) * (g * g)
    m_hat = m / (1.0 - ADAM_B1 ** ADAM_STEP)
    v_hat = v / (1.0 - ADAM_B2 ** ADAM_STEP)
    delta = -ADAM_LR * (m_hat / (jnp.sqrt(v_hat) + ADAM_EPS) + ADAM_WD * w)
    return delta, m, v


def _adamw(w, g, m, v, name, ex=None):
    def body(w_ref, g_ref, m_ref, v_ref, d_ref, nm_ref, nv_ref):
        d, nm, nv = _adamw_math(w_ref[...], g_ref[...], m_ref[...], v_ref[...])
        d_ref[...] = d
        nm_ref[...] = nm
        nv_ref[...] = nv

    if w.ndim == 3:
        rows = w.shape[0]
        br = max(b for b in range(1, 65) if rows % b == 0)
        spec, steps = pl.BlockSpec((br,) + w.shape[1:], lambda i: (i, 0, 0)), rows // br
    else:
        rows, cols = w.shape
        br = _rows_block(rows, cols, unit=8)
        spec, steps = pl.BlockSpec((br, cols), lambda i: (i, 0)), rows // br
    shape = jax.ShapeDtypeStruct(w.shape, F32)
    outs, moved = _hosted_call(
        body, ex, name=name, out_shape=(shape, shape, shape), grid=(steps,),
        in_specs=[spec] * 4, out_specs=(spec, spec, spec), args=(w, g, m, v))
    return tuple(outs) if ex is None else (tuple(outs), moved)


def _adamw_small(parts, own, ws, ms, vs, loss_parts, loss_own, device):
    n = len(ws)

    def total(device_ref, parts_ref, own_ref):
        acc = None
        for dev in range(N_DEV):
            term = jnp.where(device_ref[0] == dev, own_ref[...], parts_ref[dev])
            acc = term if acc is None else acc + term
        return acc

    def body(device_ref, *refs):
        ins, outs = refs[:5 * n + 2], refs[5 * n + 2:]
        for k in range(n):
            g = total(device_ref, ins[k], ins[n + k])
            d, nm, nv = _adamw_math(ins[2 * n + k][...], g, ins[3 * n + k][...], ins[4 * n + k][...])
            for o_ref, val in zip(outs[k::n][:4], (g, d, nm, nv)):
                o_ref[...] = val
        outs[4 * n][...] = total(device_ref, ins[5 * n], ins[5 * n + 1])

    args = list(parts) + list(own) + list(ws) + list(ms) + list(vs) + [loss_parts, loss_own]
    whole = lambda a: pl.BlockSpec(a.shape, lambda i, device_ref, nd=a.ndim: (0,) * nd)
    shapes = [jax.ShapeDtypeStruct(w.shape, F32) for w in ws] * 4 + [jax.ShapeDtypeStruct(loss_own.shape, F32)]
    outs = pl.pallas_call(
        body, name="adamw_small", out_shape=shapes,
        grid_spec=pltpu.PrefetchScalarGridSpec(
            num_scalar_prefetch=1, grid=(1,), in_specs=[whole(a) for a in args], out_specs=[whole(a) for a in shapes]),
        compiler_params=_cparams(("arbitrary",)),
    )(device, *args)
    return outs[:n], outs[n:2 * n], outs[2 * n:3 * n], outs[3 * n:4 * n], outs[4 * n]


def _mm(a, b, *, name, ta=False, out_dtype=F32, res=None, bm=1024, bn=1024, bk=4096, b_stack=False, out_stack=False,
        ex=None):
    if ta:
        kdim, m = a.shape
    else:
        m, kdim = a.shape
    if b_stack:
        _, kb, chunk = b.shape
        n = N_CHIPS * chunk
    else:
        kb, n = b.shape
        chunk = n // N_CHIPS if out_stack else n
    assert kdim == kb, (a.shape, b.shape, ta)
    bm = _block(m, bm, LANES if ta else 16)
    bn = _block(chunk, bn, LANES)
    bk = _block(kdim, bk, LANES)
    nk = kdim // bk
    per_chunk = chunk // bn
    dims = (((0 if ta else 1,), (0,)), ((), ()))

    def body(*refs):
        refs = list(refs)
        a_ref, b_ref = refs[:2]
        r_ref = refs[2] if res is not None else None
        o_ref = refs[3] if res is not None else refs[2]
        part = lax.dot_general(a_ref[...].astype(BF16), b_ref[...].astype(BF16), dims, preferred_element_type=F32)

        def finish(r):
            if r_ref is not None:
                r = r + r_ref[...]
            o_ref[...] = r.astype(out_dtype)

        if nk == 1:
            finish(part)
        else:
            acc_ref = refs[-1]
            k = pl.program_id(2)

            @pl.when(k == 0)
            def _():
                acc_ref[...] = part

            @pl.when(k > 0)
            def _():
                acc_ref[...] += part

            @pl.when(k == nk - 1)
            def _():
                finish(acc_ref[...])

    a_spec = pl.BlockSpec((bk, bm), lambda i, j, k: (k, i)) if ta else pl.BlockSpec((bm, bk), lambda i, j, k: (i, k))
    if b_stack:
        b_spec = pl.BlockSpec((None, bk, bn), lambda i, j, k: (j // per_chunk, k, j % per_chunk))
    else:
        b_spec = pl.BlockSpec((bk, bn), lambda i, j, k: (k, j))
    r_spec = pl.BlockSpec((bm, bn), lambda i, j, k: (i, j))
    if out_stack:
        o_spec = pl.BlockSpec((None, bm, bn), lambda i, j, k: (j // per_chunk, i, j % per_chunk))
        o_shape = (N_CHIPS, m, chunk)
    else:
        o_spec, o_shape = r_spec, (m, n)
    in_specs = [a_spec, b_spec] + ([r_spec] if res is not None else [])
    args = (a, b) + ((res,) if res is not None else ())
    (out,), moved = _hosted_call(
        body, ex, name=name, out_shape=(jax.ShapeDtypeStruct(o_shape, out_dtype),),
        grid=(m // bm, n // bn, nk), in_specs=in_specs, out_specs=(o_spec,),
        scratch=[pltpu.VMEM((bm, bn), F32)] if nk > 1 else [], args=args)
    return out if ex is None else (out, moved)


def _rms_fwd(x, g, name, ex=None):
    t, d = x.shape
    bt = _block(t, 512, 16)

    def body(x_ref, g_ref, h_ref):
        xv = x_ref[...]
        r = lax.rsqrt(jnp.mean(xv * xv, axis=-1, keepdims=True) + EPS)
        h_ref[...] = (xv * r * g_ref[...]).astype(BF16)

    (out,), moved = _hosted_call(
        body, ex, name=name, out_shape=(jax.ShapeDtypeStruct((t, d), BF16),), grid=(t // bt,),
        in_specs=[pl.BlockSpec((bt, d), lambda i: (i, 0)), pl.BlockSpec((1, d), lambda i: (0, 0))],
        out_specs=(pl.BlockSpec((bt, d), lambda i: (i, 0)),), args=(x, g))
    return out if ex is None else (out, moved)


def _rms_bwd(dh, x, g, dres, name, ex=None):
    t, d = x.shape
    bt = _block(t, 256, 16)
    want_dx = dres is not None

    def body(*refs):
        if want_dx:
            dh_ref, x_ref, g_ref, dres_ref, dx_ref, dxb_ref, dg_ref = refs
        else:
            dh_ref, x_ref, g_ref, dg_ref = refs
        xv = x_ref[...]
        r = lax.rsqrt(jnp.mean(xv * xv, axis=-1, keepdims=True) + EPS)
        xhat = xv * r
        dhv = dh_ref[...]

        @pl.when(pl.program_id(0) == 0)
        def _():
            dg_ref[...] = jnp.zeros_like(dg_ref)

        dg_ref[...] += jnp.sum(dhv * xhat, axis=0, keepdims=True)
        if want_dx:
            dxhat = dhv * g_ref[...]
            dx = dres_ref[...] + r * (dxhat - xhat * jnp.mean(dxhat * xhat, axis=-1, keepdims=True))
            dx_ref[...] = dx
            dxb_ref[...] = dx.astype(BF16)

    row = pl.BlockSpec((bt, d), lambda i: (i, 0))
    vec = pl.BlockSpec((1, d), lambda i: (0, 0))
    if want_dx:
        outs, moved = _hosted_call(
            body, ex, name=name, grid=(t // bt,),
            out_shape=(jax.ShapeDtypeStruct((t, d), F32), jax.ShapeDtypeStruct((t, d), BF16),
                       jax.ShapeDtypeStruct((1, d), F32)),
            in_specs=[row, row, vec, row], out_specs=(row, row, vec), args=(dh, x, g, dres))
        return tuple(outs) if ex is None else (tuple(outs), moved)
    return pl.pallas_call(
        body, name=name, grid=(t // bt,), out_shape=jax.ShapeDtypeStruct((1, d), F32),
        in_specs=[row, row, vec], out_specs=vec,
        compiler_params=_cparams(("arbitrary",)),
    )(dh, x, g)


def _final_loss(x, target, g):
    t, d = x.shape
    bt = _block(t, 256, 16)

    def body(x_ref, t_ref, g_ref, dx_ref, dxb_ref, dg_ref, loss_ref):
        xv = x_ref[...]
        gv = g_ref[...]
        r = lax.rsqrt(jnp.mean(xv * xv, axis=-1, keepdims=True) + EPS)
        xhat = xv * r
        err = xhat * gv - t_ref[...]

        @pl.when(pl.program_id(0) == 0)
        def _():
            dg_ref[...] = jnp.zeros_like(dg_ref)
            loss_ref[...] = jnp.zeros_like(loss_ref)

        loss_ref[...] += 0.5 * jnp.sum(jnp.mean(err * err, axis=-1, keepdims=True), axis=0, keepdims=True)
        dy = err * (1.0 / d)
        dg_ref[...] += jnp.sum(dy * xhat, axis=0, keepdims=True)
        dxhat = dy * gv
        dx = r * (dxhat - xhat * jnp.mean(dxhat * xhat, axis=-1, keepdims=True))
        dx_ref[...] = dx
        dxb_ref[...] = dx.astype(BF16)

    row = pl.BlockSpec((bt, d), lambda i: (i, 0))
    vec = pl.BlockSpec((1, d), lambda i: (0, 0))
    return pl.pallas_call(
        body, name="final_loss", grid=(t // bt,),
        out_shape=(jax.ShapeDtypeStruct((t, d), F32), jax.ShapeDtypeStruct((t, d), BF16),
                   jax.ShapeDtypeStruct((1, d), F32), jax.ShapeDtypeStruct((1, LANES), F32)),
        in_specs=[row, row, vec], out_specs=(row, row, vec, pl.BlockSpec((1, LANES), lambda i: (0, 0))),
        compiler_params=_cparams(("arbitrary",)),
    )(x, target, g)


GU_COLS = GATE_WIDTH + POOL_WIDTH
U_BLK = GATE_WIDTH // POOL_WIDTH


def _shift_down(a, k, row):
    return jnp.where(row >= k, pltpu.roll(a, k, 0), 0.0)


def _shift_up(a, k, row):
    n = a.shape[0]
    return jnp.where(row < n - k, pltpu.roll(a, n - k, 0), 0.0)


def _window_delta(u, w, row):
    s, k = u, 1
    while k < w:
        s = s + _shift_down(s, k, row)
        k *= 2
    cnt = jnp.minimum(row + 1, w).astype(F32)
    return s / cnt - u, cnt


def _pool_fwd(gu, pool_w, pool_scale):
    b, s, _ = gu.shape

    def body(u_ref, pw_ref, sc_ref, y_ref):
        row = lax.broadcasted_iota(jnp.int32, (s, POOL_GC), 0)
        for g, w in enumerate(POOL_WINDOWS):
            cols = slice(g * POOL_GC, (g + 1) * POOL_GC)
            d, _ = _window_delta(u_ref[0, :, cols].astype(F32), w, row)
            z = jnp.dot(d.astype(BF16), pw_ref[g].astype(BF16), preferred_element_type=F32)
            y_ref[0, :, cols] = (z * sc_ref[:, cols]).astype(BF16)

    return pl.pallas_call(
        body, name="pool_fwd", out_shape=jax.ShapeDtypeStruct((b, s, POOL_WIDTH), BF16), grid=(b,),
        in_specs=[pl.BlockSpec((1, s, POOL_WIDTH), lambda i: (i, 0, U_BLK)),
                  pl.BlockSpec((4, POOL_GC, POOL_GC), lambda i: (0, 0, 0)),
                  pl.BlockSpec((1, POOL_WIDTH), lambda i: (0, 0))],
        out_specs=pl.BlockSpec((1, s, POOL_WIDTH), lambda i: (i, 0, 0)),
        compiler_params=_cparams(("parallel",)),
    )(gu, pool_w, pool_scale)


def _pool_bwd(gu, dy, pool_w, pool_scale, dgu):
    b, s, _ = gu.shape

    def body(u_ref, dy_ref, pw_ref, sc_ref, dgu_in, du_ref, dpw_ref, dsc_ref):
        del dgu_in

        @pl.when(pl.program_id(0) == 0)
        def _():
            dpw_ref[...] = jnp.zeros_like(dpw_ref)
            dsc_ref[...] = jnp.zeros_like(dsc_ref)

        row = lax.broadcasted_iota(jnp.int32, (s, POOL_GC), 0)
        for g, w in enumerate(POOL_WINDOWS):
            cols = slice(g * POOL_GC, (g + 1) * POOL_GC)
            d, cnt = _window_delta(u_ref[0, :, cols].astype(F32), w, row)
            db = d.astype(BF16)
            pw = pw_ref[g].astype(BF16)
            z = jnp.dot(db, pw, preferred_element_type=F32)
            dyv = dy_ref[0, :, cols]
            dsc_ref[:, cols] += jnp.sum(dyv * z, axis=0, keepdims=True)
            dz = (dyv * sc_ref[:, cols]).astype(BF16)
            dpw_ref[g] += lax.dot_general(db, dz, (((0,), (0,)), ((), ())), preferred_element_type=F32)
            dd = lax.dot_general(dz, pw, (((1,), (1,)), ((), ())), preferred_element_type=F32)
            acc, k = dd / cnt, 1
            while k < w:
                acc = acc + _shift_up(acc, k, row)
                k *= 2
            du_ref[0, :, cols] = (acc - dd).astype(BF16)

    return pl.pallas_call(
        body, name="pool_bwd", grid=(b,),
        out_shape=(jax.ShapeDtypeStruct((b, s, GU_COLS), BF16), jax.ShapeDtypeStruct((4, POOL_GC, POOL_GC), F32),
                   jax.ShapeDtypeStruct((1, POOL_WIDTH), F32)),
        in_specs=[pl.BlockSpec((1, s, POOL_WIDTH), lambda i: (i, 0, U_BLK)),
                  pl.BlockSpec((1, s, POOL_WIDTH), lambda i: (i, 0, 0)),
                  pl.BlockSpec((4, POOL_GC, POOL_GC), lambda i: (0, 0, 0)),
                  pl.BlockSpec((1, POOL_WIDTH), lambda i: (0, 0)), ANY],
        out_specs=(pl.BlockSpec((1, s, POOL_WIDTH), lambda i: (i, 0, U_BLK)),
                   pl.BlockSpec((4, POOL_GC, POOL_GC), lambda i: (0, 0, 0)),
                   pl.BlockSpec((1, POOL_WIDTH), lambda i: (0, 0))),
        input_output_aliases={4: 0},
        compiler_params=_cparams(("arbitrary",)),
    )(gu, dy, pool_w, pool_scale, dgu)


def _forget_cumsum(f, bias, name):
    b, s, c = f.shape

    def body(f_ref, b_ref, c_ref):
        row = lax.broadcasted_iota(jnp.int32, (s, LANES), 0)
        z = f_ref[0] + b_ref[...]
        acc = jnp.minimum(z, 0.0) - jnp.log(1.0 + jnp.exp(-jnp.abs(z)))
        k = 1
        while k < s:
            acc = acc + _shift_down(acc, k, row)
            k *= 2
        c_ref[0] = acc

    return pl.pallas_call(
        body, name=name, out_shape=jax.ShapeDtypeStruct((b, s, c), F32), grid=(b, c // LANES),
        in_specs=[pl.BlockSpec((1, s, LANES), lambda i, j: (i, 0, j)), pl.BlockSpec((1, LANES), lambda i, j: (0, j))],
        out_specs=pl.BlockSpec((1, s, LANES), lambda i, j: (i, 0, j)),
        compiler_params=_cparams(("parallel", "parallel")),
    )(f, bias)


def _forget_bwd(dc, f, bias):
    b, s, _ = f.shape

    def body(dc_ref, f_ref, b_ref, df_ref, db_ref):
        @pl.when(pl.program_id(0) == 0)
        def _():
            db_ref[...] = jnp.zeros_like(db_ref)

        row = lax.broadcasted_iota(jnp.int32, (s, LANES), 0)
        acc, k = dc_ref[0], 1
        while k < s:
            acc = acc + _shift_up(acc, k, row)
            k *= 2
        z = f_ref[0] + b_ref[...]
        df = acc / (1.0 + jnp.exp(z))
        db_ref[...] += jnp.sum(df, axis=0, keepdims=True)
        df_ref[0] = df.astype(BF16)

    blk = pl.BlockSpec((1, s, LANES), lambda i: (i, 0, 0))
    vec = pl.BlockSpec((1, LANES), lambda i: (0, 0))
    return pl.pallas_call(
        body, name="forget_bwd", grid=(b,),
        out_shape=(jax.ShapeDtypeStruct((b, s, LANES), BF16), jax.ShapeDtypeStruct((1, LANES), F32)),
        in_specs=[blk, blk, vec], out_specs=(blk, vec),
        compiler_params=_cparams(("arbitrary",)),
    )(dc, f, bias)


KV_BLK0 = 2
PAIRS = FOX_HEADS // 2
FOX_SCALE = FOX_DH ** -0.5
NT_DIMS = (((1,), (1,)), ((), ()))
TN_DIMS = (((0,), (0,)), ((), ()))


def _stack_heads(v):
    head = lax.broadcasted_iota(jnp.int32, v.shape, 1) // FOX_DH
    zero = jnp.zeros_like(v)
    return jnp.concatenate([jnp.where(head == 0, v, zero), jnp.where(head == 1, v, zero)], axis=0)


def _stack_cols(v):
    return jnp.concatenate([v[:, 0:1], v[:, FOX_DH:FOX_DH + 1]], axis=0)


def _unstack(t, blk):
    head = lax.broadcasted_iota(jnp.int32, (blk, LANES), 1) // FOX_DH
    return jnp.where(head == 0, t[:blk], t[blk:])


def _fox_scores(q_all, kblk, row_bias, cr_ref, kb, masked, blk):
    top = lax.broadcasted_iota(jnp.int32, (2 * blk, 1), 0) < blk
    s = lax.dot_general(q_all, kblk, NT_DIMS, preferred_element_type=F32)
    s = s + (row_bias - jnp.where(top, cr_ref[0, 0, kb], cr_ref[0, 1, kb]))
    if masked:
        r = lax.broadcasted_iota(jnp.int32, (2 * blk, blk), 0)
        keep = jnp.where(r >= blk, r - blk, r) >= lax.broadcasted_iota(jnp.int32, (2 * blk, blk), 1)
        s = jnp.where(keep, s, NEG_INF)
    return s


def _fox_fwd(qkv, c_exp, c_row, ex=None):
    b, s, _ = qkv.shape
    blk = min(ATT_BLOCK, s)
    nq = s // blk

    def body(q_ref, kv_ref, cc_ref, cr_ref, o_ref, ob_ref, lse_ref):
        qi = pl.program_id(2)
        q_all = _stack_heads(q_ref[0] * FOX_SCALE)
        cq = _stack_cols(cc_ref[0])

        def step(kb, carry, masked):
            m, l, acc = carry
            rows = pl.ds(pl.multiple_of(kb * blk, blk), blk)
            sc = _fox_scores(q_all, kv_ref[0, rows, :LANES], cq, cr_ref, kb, masked, blk)
            m_new = jnp.maximum(m, jnp.max(sc, axis=-1, keepdims=True))
            p = jnp.exp(sc - m_new)
            alpha = jnp.exp(m - m_new)
            l = alpha * l + jnp.sum(p, axis=-1, keepdims=True)
            acc = alpha * acc + jnp.dot(p.astype(BF16), kv_ref[0, rows, LANES:], preferred_element_type=F32)
            return m_new, l, acc

        init = (jnp.full((2 * blk, 1), NEG_INF, F32), jnp.zeros((2 * blk, 1), F32), jnp.zeros((2 * blk, LANES), F32))
        m, l, acc = step(qi, lax.fori_loop(0, qi, functools.partial(step, masked=False), init), True)
        o = _unstack(acc / l, blk)
        o_ref[0] = o
        ob_ref[0] = o.astype(BF16)
        lse_ref[0] = _unstack(jnp.broadcast_to(m + jnp.log(l), (2 * blk, LANES)), blk)

    tile = pl.BlockSpec((1, blk, LANES), lambda i, h, q: (i, q, h))
    kvspec = pl.BlockSpec((1, s, 2 * LANES), lambda i, h, q: (i, 0, KV_BLK0 + h))
    shape = jax.ShapeDtypeStruct((b, s, FOX_WIDTH), F32)
    return _hosted_call(
        body, ex, name="fox_fwd", out_shape=(shape, jax.ShapeDtypeStruct((b, s, FOX_WIDTH), BF16), shape),
        grid=(b, PAIRS, nq),
        in_specs=[tile, kvspec, tile, pl.BlockSpec((1, 2, nq, 1, blk), lambda i, h, q: (i, h, 0, 0, 0))],
        out_specs=(tile, tile, tile), args=(qkv, qkv, c_exp, c_row))


def _fox_bwd(qkv, c_exp, c_row, lse, o, do, ex=None):
    b, s, _ = qkv.shape
    blk = min(ATT_BLOCK, s)
    nq = s // blk

    def body(q_ref, kv_ref, cc_ref, cr_ref, lse_ref, o_ref, do_ref, dq_ref, dkv_ref, dcq_ref, dc_ref, dk_acc, dv_acc):
        qi = pl.program_id(2)

        @pl.when(qi == 0)
        def _():
            dk_acc[...] = jnp.zeros_like(dk_acc)
            dv_acc[...] = jnp.zeros_like(dv_acc)
            dc_ref[...] = jnp.zeros_like(dc_ref)

        q_all = _stack_heads(q_ref[0] * FOX_SCALE)
        dov = do_ref[0]
        do_all = _stack_heads(dov.astype(BF16))
        delta = jnp.sum(_stack_heads(dov * o_ref[0]), axis=-1, keepdims=True)
        bias = _stack_cols(cc_ref[0]) - _stack_cols(lse_ref[0])

        def step(kb, carry, masked):
            acc, dcq = carry
            rows = pl.ds(pl.multiple_of(kb * blk, blk), blk)
            kblk = kv_ref[0, rows, :LANES]
            p = jnp.exp(_fox_scores(q_all, kblk, bias, cr_ref, kb, masked, blk))
            dp = lax.dot_general(do_all, kv_ref[0, rows, LANES:], NT_DIMS, preferred_element_type=F32)
            ds = p * (dp - delta)
            dsb = ds.astype(BF16)
            dv_acc[rows, :] += lax.dot_general(p.astype(BF16), do_all, TN_DIMS, preferred_element_type=F32)
            dk_acc[rows, :] += lax.dot_general(dsb, q_all, TN_DIMS, preferred_element_type=F32)
            dc_ref[0, 0, kb] -= jnp.sum(ds[:blk], axis=0, keepdims=True)
            dc_ref[0, 1, kb] -= jnp.sum(ds[blk:], axis=0, keepdims=True)
            acc = acc + jnp.dot(dsb, kblk, preferred_element_type=F32)
            return acc, dcq + jnp.sum(ds, axis=-1, keepdims=True)

        init = (jnp.zeros((2 * blk, LANES), F32), jnp.zeros((2 * blk, 1), F32))
        acc, dcq = step(qi, lax.fori_loop(0, qi, functools.partial(step, masked=False), init), True)
        dq_ref[0] = (_unstack(acc, blk) * FOX_SCALE).astype(BF16)
        dcq_ref[0, 0] = jnp.where(lax.broadcasted_iota(jnp.int32, (blk, 2), 1) == 0, dcq[:blk], dcq[blk:])

        @pl.when(qi == nq - 1)
        def _():
            dkv_ref[0, :, :LANES] = dk_acc[...].astype(BF16)
            dkv_ref[0, :, LANES:] = dv_acc[...].astype(BF16)

    tile = pl.BlockSpec((1, blk, LANES), lambda i, h, q: (i, q, h))
    kvspec = pl.BlockSpec((1, s, 2 * LANES), lambda i, h, q: (i, 0, KV_BLK0 + h))
    crow = pl.BlockSpec((1, 2, nq, 1, blk), lambda i, h, q: (i, h, 0, 0, 0))
    return _hosted_call(
        body, ex, name="fox_bwd", grid=(b, PAIRS, nq),
        out_shape=(jax.ShapeDtypeStruct((b, s, FOX_WIDTH), BF16), jax.ShapeDtypeStruct((b, s, 2 * FOX_WIDTH), BF16),
                   jax.ShapeDtypeStruct((b, PAIRS, s, 2), F32), jax.ShapeDtypeStruct(c_row.shape, F32)),
        in_specs=[tile, kvspec, tile, crow, tile, tile, tile],
        out_specs=(tile, pl.BlockSpec((1, s, 2 * LANES), lambda i, h, q: (i, 0, h)),
                   pl.BlockSpec((1, 1, blk, 2), lambda i, h, q: (i, h, q, 0)), crow),
        scratch=[pltpu.VMEM((s, LANES), F32), pltpu.VMEM((s, LANES), F32)],
        args=(qkv, qkv, c_exp, c_row, lse, o, do))


def _sigmoid(z):
    return 1.0 / (1.0 + jnp.exp(-z))


def _branches_mix(y, o, w_pool3, w_fox3, gu, b_gate):
    t = y.shape[0]
    chunk = w_pool3.shape[2]
    per_branch = D_MODEL // chunk
    bm = _block(t, 1024, 16)

    def body(y_ref, o_ref, wp_ref, wf_ref, gp_ref, gf_ref, bp_ref, bf_ref, yp_ref, yf_ref, mix_ref):
        yp = jnp.dot(y_ref[...], wp_ref[...], preferred_element_type=F32).astype(BF16)
        yf = jnp.dot(o_ref[...], wf_ref[...], preferred_element_type=F32).astype(BF16)
        yp_ref[...] = yp
        yf_ref[...] = yf
        gp = _sigmoid(gp_ref[...].astype(F32) + bp_ref[...])
        gf = _sigmoid(gf_ref[...].astype(F32) + bf_ref[...])
        mix_ref[...] = (gp * yp.astype(F32) + gf * yf.astype(F32)).astype(BF16)

    rows = pl.BlockSpec((bm, y.shape[1]), lambda i, j: (i, 0))
    weight = pl.BlockSpec((None, y.shape[1], chunk), lambda i, j: (j, 0, 0))
    tile = lambda base: pl.BlockSpec((bm, chunk), lambda i, j: (i, base + j))
    vec = lambda base: pl.BlockSpec((1, chunk), lambda i, j: (0, base + j))
    shape = jax.ShapeDtypeStruct((t, D_MODEL), BF16)
    return pl.pallas_call(
        body, name="branches_mix", out_shape=(shape, shape, shape), grid=(t // bm, per_branch),
        in_specs=[rows, rows, weight, weight, tile(0), tile(per_branch), vec(0), vec(per_branch)],
        out_specs=(tile(0), tile(0), tile(0)),
        compiler_params=_cparams(("parallel", "arbitrary")),
    )(y, o, w_pool3, w_fox3, gu, gu, b_gate, b_gate)


def _mix_bwd(gu, b_gate, y_pool, y_fox, dmix):
    t = gu.shape[0]
    bt = _block(t, 256, 16)

    def body(gp_ref, gf_ref, bp_ref, bf_ref, yp_ref, yf_ref, dm_ref, dyp_ref, dyf_ref, dgl_ref, db_ref):
        @pl.when(pl.program_id(0) == 0)
        def _():
            db_ref[...] = jnp.zeros_like(db_ref)

        dm = dm_ref[...]
        gp = _sigmoid(gp_ref[...].astype(F32) + bp_ref[...])
        gf = _sigmoid(gf_ref[...].astype(F32) + bf_ref[...])
        dyp_ref[...] = (dm * gp).astype(BF16)
        dyf_ref[...] = (dm * gf).astype(BF16)
        dlp = dm * yp_ref[...].astype(F32) * gp * (1.0 - gp)
        dlf = dm * yf_ref[...].astype(F32) * gf * (1.0 - gf)
        dgl_ref[:, :D_MODEL] = dlp.astype(BF16)
        dgl_ref[:, D_MODEL:] = dlf.astype(BF16)
        db_ref[:, :D_MODEL] += jnp.sum(dlp, axis=0, keepdims=True)
        db_ref[:, D_MODEL:] += jnp.sum(dlf, axis=0, keepdims=True)

    col = lambda j: pl.BlockSpec((bt, D_MODEL), lambda i: (i, j))
    vec = lambda j: pl.BlockSpec((1, D_MODEL), lambda i: (0, j))
    wide = pl.BlockSpec((bt, GATE_WIDTH), lambda i: (i, 0))
    return pl.pallas_call(
        body, name="mix_bwd", grid=(t // bt,),
        out_shape=(jax.ShapeDtypeStruct((t, D_MODEL), BF16), jax.ShapeDtypeStruct((t, D_MODEL), BF16),
                   jax.ShapeDtypeStruct((t, GU_COLS), BF16), jax.ShapeDtypeStruct((1, GATE_WIDTH), F32)),
        in_specs=[col(0), col(1), vec(0), vec(1), col(0), col(0), col(0)],
        out_specs=(col(0), col(0), wide, pl.BlockSpec((1, GATE_WIDTH), lambda i: (0, 0))),
        compiler_params=_cparams(("arbitrary",)),
    )(gu, gu, b_gate, b_gate, y_pool, y_fox, dmix)


X_SCALE = X_DH ** -0.5


def _xattn_probs(qh, kh):
    s = lax.dot_general(qh, kh, NT_DIMS, preferred_element_type=F32) * X_SCALE
    e = jnp.exp(s - jnp.max(s, axis=-1, keepdims=True))
    return e / jnp.sum(e, axis=-1, keepdims=True)


def _xattn_fwd(q, kv):
    b, s, _ = q.shape
    m = kv.shape[1]
    bq = _block(s, 512, 16)

    def body(q_ref, kv_ref, o_ref):
        for h in range(X_HEADS):
            cols = slice(h * X_DH, (h + 1) * X_DH)
            p = _xattn_probs(q_ref[0, :, cols], kv_ref[0, :, cols])
            vh = kv_ref[0, :, X_WIDTH + h * X_DH:X_WIDTH + (h + 1) * X_DH]
            o_ref[0, :, cols] = jnp.dot(p.astype(BF16), vh, preferred_element_type=F32).astype(BF16)

    return pl.pallas_call(
        body, name="xattn_fwd", out_shape=jax.ShapeDtypeStruct((b, s, X_WIDTH), BF16), grid=(b, s // bq),
        in_specs=[pl.BlockSpec((1, bq, X_WIDTH), lambda i, j: (i, j, 0)),
                  pl.BlockSpec((1, m, 2 * X_WIDTH), lambda i, j: (i, 0, 0))],
        out_specs=pl.BlockSpec((1, bq, X_WIDTH), lambda i, j: (i, j, 0)),
        compiler_params=_cparams(("parallel", "parallel")),
    )(q, kv)


def _xattn_bwd(q, kv, do):
    b, s, _ = q.shape
    m = kv.shape[1]
    bq = _block(s, 512, 16)

    def body(q_ref, kv_ref, do_ref, dq_ref, dkv_ref):
        @pl.when(pl.program_id(1) == 0)
        def _():
            dkv_ref[...] = jnp.zeros_like(dkv_ref)

        for h in range(X_HEADS):
            cols = slice(h * X_DH, (h + 1) * X_DH)
            vcols = slice(X_WIDTH + h * X_DH, X_WIDTH + (h + 1) * X_DH)
            qh, kh, vh, doh = q_ref[0, :, cols], kv_ref[0, :, cols], kv_ref[0, :, vcols], do_ref[0, :, cols]
            p = _xattn_probs(qh, kh)
            dkv_ref[0, :, vcols] += lax.dot_general(p.astype(BF16), doh, TN_DIMS, preferred_element_type=F32)
            dp = lax.dot_general(doh, vh, NT_DIMS, preferred_element_type=F32)
            ds = (p * (dp - jnp.sum(p * dp, axis=-1, keepdims=True)) * X_SCALE).astype(BF16)
            dq_ref[0, :, cols] = jnp.dot(ds, kh, preferred_element_type=F32).astype(BF16)
            dkv_ref[0, :, cols] += lax.dot_general(ds, qh, TN_DIMS, preferred_element_type=F32)

    tile = pl.BlockSpec((1, bq, X_WIDTH), lambda i, j: (i, j, 0))
    mem = pl.BlockSpec((1, m, 2 * X_WIDTH), lambda i, j: (i, 0, 0))
    return pl.pallas_call(
        body, name="xattn_bwd", grid=(b, s // bq),
        out_shape=(jax.ShapeDtypeStruct((b, s, X_WIDTH), BF16), jax.ShapeDtypeStruct((b, m, 2 * X_WIDTH), F32)),
        in_specs=[tile, mem, tile], out_specs=(tile, mem),
        compiler_params=_cparams(("parallel", "arbitrary")),
    )(q, kv, do)


def _ffn_in(hf, w3):
    t, d = hf.shape
    chunk = w3.shape[2]
    half = N_CHIPS // 2
    bm = _block(t, 1024, 16)

    def body(a_ref, wg_ref, wu_ref, gt_ref, up_ref, act_ref):
        a = a_ref[...]
        gt = jnp.dot(a, wg_ref[...], preferred_element_type=F32).astype(BF16)
        up = jnp.dot(a, wu_ref[...], preferred_element_type=F32).astype(BF16)
        gt_ref[...] = gt
        up_ref[...] = up
        g32 = gt.astype(F32)
        act_ref[...] = (g32 * _sigmoid(g32) * up.astype(F32)).astype(BF16)

    tile = pl.BlockSpec((bm, chunk), lambda i, j: (i, j))
    shape = jax.ShapeDtypeStruct((t, half * chunk), BF16)
    return pl.pallas_call(
        body, name="ffn_in", out_shape=(shape, shape, shape), grid=(t // bm, half),
        in_specs=[pl.BlockSpec((bm, d), lambda i, j: (i, 0)),
                  pl.BlockSpec((None, d, chunk), lambda i, j: (j, 0, 0)),
                  pl.BlockSpec((None, d, chunk), lambda i, j: (j + half, 0, 0))],
        out_specs=(tile, tile, tile),
        compiler_params=_cparams(("parallel", "arbitrary")),
    )(hf, w3, w3)


def _ffn_act_bwd(dx, w_out_t, gate, up):
    t, d = dx.shape
    bt = _block(t, 256, 16)

    def body(dx_ref, w_ref, gt_ref, up_ref, o_ref):
        da = jnp.dot(dx_ref[...], w_ref[...], preferred_element_type=F32).astype(BF16).astype(F32)
        gt = gt_ref[...].astype(F32)
        sg = _sigmoid(gt)
        silu = gt * sg
        o_ref[:, :D_FF] = (da * up_ref[...].astype(F32) * (sg + silu * (1.0 - sg))).astype(BF16)
        o_ref[:, D_FF:] = (da * silu).astype(BF16)

    col = pl.BlockSpec((bt, D_FF), lambda i: (i, 0))
    return pl.pallas_call(
        body, name="ffn_act_bwd", out_shape=jax.ShapeDtypeStruct((t, 2 * D_FF), BF16), grid=(t // bt,),
        in_specs=[pl.BlockSpec((bt, d), lambda i: (i, 0)), pl.BlockSpec((d, D_FF), lambda i: (0, 0)), col, col],
        out_specs=pl.BlockSpec((bt, 2 * D_FF), lambda i: (i, 0)),
        compiler_params=_cparams(("parallel",)),
    )(dx, w_out_t, gate, up)


def _stack_of(w, axis):
    r, c = w.shape
    if axis == 0:
        return w.reshape(N_CHIPS, r // N_CHIPS, c)
    return w.reshape(r, N_CHIPS, c // N_CHIPS).transpose(1, 0, 2)


def _stack_t(w3):
    n, r, c = w3.shape
    return w3.transpose(0, 2, 1).reshape(n * c, r)


def _pair_rows(k, v):
    c = k.shape[1]
    return jnp.stack([k.reshape(PAIRS, LANES, c), v.reshape(PAIRS, LANES, c)], axis=1).reshape(2 * FOX_WIDTH, c)


def _unpair_rows(kv):
    c = kv.shape[1]
    kv = kv.reshape(PAIRS, 2, LANES, c)
    return kv[:, 0].reshape(FOX_WIDTH, c), kv[:, 1].reshape(FOX_WIDTH, c)


def _input_grad(parts, weights_t, ex):
    t = parts[0].shape[0]
    d = weights_t[0].shape[1]
    bm = _block(t, 512, 16)
    n = len(parts)

    def body(*refs):
        acc = None
        for a_ref, b_ref in zip(refs[:n], refs[n:2 * n]):
            term = jnp.dot(a_ref[...], b_ref[...], preferred_element_type=F32)
            acc = term if acc is None else acc + term
        refs[2 * n][...] = acc

    (out,), moved = _hosted_call(
        body, ex, name="d_h", grid=(t // bm,), out_shape=(jax.ShapeDtypeStruct((t, d), F32),),
        in_specs=[pl.BlockSpec((bm, p.shape[1]), lambda i: (i, 0)) for p in parts]
        + [pl.BlockSpec(w.shape, lambda i: (0, 0)) for w in weights_t],
        out_specs=(pl.BlockSpec((bm, d), lambda i: (i, 0)),), args=tuple(parts) + tuple(weights_t))
    return out, moved


def _step(x, mem, loss_target, weights, moments_m, moments_v):
    nb, s, d = x.shape
    n_mem = mem.shape[1]
    t = nb * s
    blk = min(ATT_BLOCK, s)
    x2 = x.reshape(t, d)
    mem2 = mem.reshape(nb * n_mem, d)
    tgt2 = loss_target.reshape(t, d)

    def shard2d(a, n):
        a = a.reshape(a.shape[1:])
        return a.T if n == "w_in" else a

    def unshard(a, n):
        return (a.T if n == "w_in" else a)[None]

    local = {n: shard2d(weights[n], n) for n, _, _ in SHARDED}

    names = [n for n, _, _ in SHARDED]
    last = ["w_ffn_out"]
    later = [n for n in names if n != "w_in" and n not in last]
    local_b = {n: local[n].astype(BF16) for n in names}
    g_mix = weights["norm_mix_g"]
    h, w_in_others = _rms_fwd(x2, g_mix, "norm_mix", ex=_gather_exchange([local_b["w_in"]]))
    w_in_stack, = _place_own(w_in_others, [local_b["w_in"]])

    def w_in_rows(lo, hi):
        per = IN_COLS // N_CHIPS
        parts = [w_in_stack[j, max(lo, j * per) - j * per:min(hi, (j + 1) * per) - j * per]
                 for j in range(N_CHIPS) if max(lo, j * per) < min(hi, (j + 1) * per)]
        return parts[0] if len(parts) == 1 else jnp.concatenate(parts)

    w_gu_t = jnp.concatenate([w_in_rows(2056, IN_COLS), w_in_rows(0, 512)])
    w_qkv_t = jnp.concatenate([w_in_rows(512, 1024), _pair_rows(w_in_rows(1024, 1536), w_in_rows(1536, 2048))])
    w_f_t = jnp.pad(w_in_rows(2048, 2056), ((0, LANES - FOX_HEADS), (0, 0)))
    w_gu, w_qkv, w_f = w_gu_t.T, w_qkv_t.T, w_f_t.T
    w_f_exp = jnp.repeat(w_f[:, :FOX_HEADS], FOX_DH, axis=1)

    g_mix, g_x, g_mem, g_ffn = (weights[n] for n in ("norm_mix_g", "norm_x_g", "norm_mem_g", "norm_ffn_g"))
    g_final = weights["norm_final_g"].reshape(1, d)
    pool_w = weights["pool_w"].reshape(4, POOL_GC, POOL_GC)
    pool_scale, b_gate = weights["pool_scale"], weights["b_gate"]
    b_f_pad = jnp.pad(weights["b_forget"], ((0, 0), (0, LANES - FOX_HEADS)))
    b_f_exp = jnp.repeat(weights["b_forget"], FOX_DH, axis=1)

    gu, last_others = _mm(h, w_gu, out_dtype=BF16, bn=512, name="in_proj_gates_pool",
                          ex=_gather_exchange([local_b[n] for n in last]))
    qkv = _mm(h, w_qkv, out_dtype=BF16, bn=512, name="in_proj_qkv")
    f_pad = _mm(h, w_f, name="in_proj_forget")
    gu3, qkv3 = gu.reshape(nb, s, GU_COLS), qkv.reshape(nb, s, 3 * FOX_WIDTH)
    y = _pool_fwd(gu3, pool_w, pool_scale)
    f_exp = _mm(h, w_f_exp, name="in_proj_forget_lanes").reshape(nb, s, FOX_WIDTH)
    c_exp = _forget_cumsum(f_exp, b_f_exp, "forget_cumsum_lanes")
    c_pad = _forget_cumsum(f_pad.reshape(nb, s, LANES), b_f_pad, "forget_cumsum")
    c_row = c_pad[:, :, :FOX_HEADS].transpose(0, 2, 1).reshape(nb, FOX_HEADS, s // blk, 1, blk)
    (o, o_b, lse), gathered = _fox_fwd(qkv3, c_exp, c_row, ex=_gather_exchange([local_b[n] for n in later]))
    stacks = dict(zip(later, _place_own(gathered, [local_b[n] for n in later])))
    stacks.update(zip(last, _place_own(last_others, [local_b[n] for n in last])))
    w_pool_out3, w_fox_out3, w_xo3, w_ffn_in3 = (stacks[n] for n in ("w_pool_out", "w_fox_out", "w_xo", "w_ffn_in"))
    w_out, w_xq, w_xkv, w_ffn_out = (stacks[n].reshape(-1, stacks[n].shape[2])
                                     for n in ("w_out", "w_xq", "w_xkv", "w_ffn_out"))
    y2, o2 = y.reshape(t, POOL_WIDTH), o_b.reshape(t, FOX_WIDTH)
    y_pool, y_fox, mix = _branches_mix(y2, o2, w_pool_out3, w_fox_out3, gu, b_gate)
    x1 = _mm(mix, w_out, res=x2, name="mix_out")
    hx = _rms_fwd(x1, g_x, "norm_x")
    mem_n = _rms_fwd(mem2, g_mem, "norm_mem")
    qx = _mm(hx, w_xq, out_dtype=BF16, name="x_q")
    kv = _mm(mem_n, w_xkv, out_dtype=BF16, name="x_kv")
    qx3, kv3 = qx.reshape(nb, s, X_WIDTH), kv.reshape(nb, n_mem, 2 * X_WIDTH)
    ox = _xattn_fwd(qx3, kv3).reshape(t, X_WIDTH)
    x2_ = _mm(ox, w_xo3, b_stack=True, res=x1, name="x_out")
    hf = _rms_fwd(x2_, g_ffn, "norm_ffn")
    ffn_gate, ffn_up, act = _ffn_in(hf, w_ffn_in3)
    x3 = _mm(act, w_ffn_out, res=x2_, name="ffn_out")

    dx3, dx3_b, dg_final, loss_part = _final_loss(x3, tgt2, g_final)
    dw_ffn_out = _mm(act, dx3_b, ta=True, bm=1408, bn=512, bk=2048, name="d_w_ffn_out")
    dffn = _ffn_act_bwd(dx3_b, w_ffn_out.T, ffn_gate, ffn_up)
    dw_ffn_in = _mm(hf, dffn, ta=True, bm=512, bn=1408, bk=2048, out_stack=True, name="d_w_ffn_in")
    core = lax.axis_index("c").astype(jnp.int32).reshape(1)
    ffn_group = ["w_ffn_in", "w_ffn_out"]
    mid_group = ["w_pool_out", "w_fox_out", "w_out", "w_xq", "w_xkv", "w_xo"]
    grad_stacks = {"w_ffn_in": dw_ffn_in, "w_ffn_out": _stack_of(dw_ffn_out, 0)}

    def presum(group, theirs):
        return [_sum_halves(grad_stacks[n], t_, core, "sum_halves_" + n) for n, t_ in zip(group, theirs)]

    dhf, theirs = _mm(dffn, _stack_t(w_ffn_in3), bk=2816, name="d_hf",
                      ex=_swap_exchange([grad_stacks[n] for n in ffn_group]))
    chip_sums = dict(zip(ffn_group, presum(ffn_group, theirs)))
    dx2, dx2_b, dg_ffn = _rms_bwd(dhf, x2_, g_ffn, dx3, "norm_ffn_bwd")

    dw_xo = _mm(ox, dx2_b, ta=True, bn=256, out_stack=True, name="d_w_xo")
    dox = _mm(dx2_b, _stack_t(w_xo3), out_dtype=BF16, name="d_ox").reshape(nb, s, X_WIDTH)
    dqx, dkv = _xattn_bwd(qx3, kv3, dox)
    dqx2, dkv2 = dqx.reshape(t, X_WIDTH), dkv.reshape(nb * n_mem, 2 * X_WIDTH)
    dw_xkv = _mm(mem_n, dkv2, ta=True, name="d_w_xkv")
    dmem_n = _mm(dkv2, w_xkv.T, name="d_mem_n")
    dg_mem = _rms_bwd(dmem_n, mem2, g_mem, None, "norm_mem_bwd")
    dw_xq = _mm(hx, dqx2, ta=True, name="d_w_xq")
    dhx = _mm(dqx2, w_xq.T, name="d_hx")
    dx1, dx1_b, dg_x = _rms_bwd(dhx, x1, g_x, dx2, "norm_x_bwd")

    dw_out = _mm(mix, dx1_b, ta=True, name="d_w_out")
    dmix = _mm(dx1_b, w_out.T, name="d_mix")
    dyp, dyf, dgu, db_gate = _mix_bwd(gu, b_gate, y_pool, y_fox, dmix)
    dw_pool_out = _mm(y2, dyp, ta=True, bn=256, out_stack=True, name="d_w_pool_out")
    dw_fox_out = _mm(o2, dyf, ta=True, bn=256, out_stack=True, name="d_w_fox_out")
    dy = _mm(dyp, _stack_t(w_pool_out3), name="d_y").reshape(nb, s, POOL_WIDTH)
    do = _mm(dyf, _stack_t(w_fox_out3), name="d_o").reshape(nb, s, FOX_WIDTH)
    dgu3, dpool_w, dpool_scale = _pool_bwd(gu3, dy, pool_w, pool_scale, dgu.reshape(nb, s, GU_COLS))
    grad_stacks.update({"w_pool_out": dw_pool_out, "w_fox_out": dw_fox_out, "w_out": _stack_of(dw_out, 0),
                        "w_xq": _stack_of(dw_xq, 0), "w_xkv": _stack_of(dw_xkv, 0), "w_xo": dw_xo})
    dgu2 = dgu3.reshape(t, GU_COLS)
    dw_gu_t, theirs = _mm(dgu2, h, ta=True, name="d_w_gates_pool",
                          ex=_swap_exchange([grad_stacks[n] for n in mid_group]))
    chip_sums.update(zip(mid_group, presum(mid_group, theirs)))
    early = ffn_group + mid_group
    (dq3, dkv3, dc_q, dc_row), early_slots = _fox_bwd(qkv3, c_exp, c_row, lse, o, do,
                                                      ex=_chips_exchange([chip_sums[n] for n in early]))
    slots = dict(zip(early, early_slots))
    dc = dc_row.reshape(nb, FOX_HEADS, s).transpose(0, 2, 1) + dc_q.transpose(0, 2, 1, 3).reshape(nb, s, FOX_HEADS)
    dc = jnp.pad(dc, ((0, 0), (0, 0), (0, LANES - FOX_HEADS)))
    df, db_f = _forget_bwd(dc, f_pad.reshape(nb, s, LANES), b_f_pad)
    dq2, dkv2, df2 = dq3.reshape(t, FOX_WIDTH), dkv3.reshape(t, 2 * FOX_WIDTH), df.reshape(t, LANES)
    dw_q_t = _mm(dq2, h, ta=True, name="d_w_q")
    dw_kv_t = _mm(dkv2, h, ta=True, name="d_w_kv")
    dw_f_t = _mm(df2, h, ta=True, name="d_w_forget")
    dw_k_t, dw_v_t = _unpair_rows(dw_kv_t)
    dw_in_t = jnp.concatenate([dw_gu_t[GATE_WIDTH:], dw_q_t, dw_k_t, dw_v_t, dw_f_t[:FOX_HEADS],
                               dw_gu_t[:GATE_WIDTH]])
    grad_stacks["w_in"] = dw_in_t.reshape(N_CHIPS, IN_COLS // N_CHIPS, D_MODEL)
    chip_sums["w_in"], = presum(["w_in"], _run_exchange(_swap_exchange([grad_stacks["w_in"]]), "swap_halves_w_in"))
    dh, (slots["w_in"],) = _input_grad([dgu2, dq2, dkv2, df2],
                                       [w_gu_t, w_qkv_t[:FOX_WIDTH], w_qkv_t[FOX_WIDTH:], w_f_t],
                                       _chips_exchange([chip_sums["w_in"]]))

    place = jnp.stack([lax.axis_index("c"), 2 * lax.axis_index("x") + lax.axis_index("y")]).astype(jnp.int32)
    halves = [_sum_chips(slots[n], chip_sums[n], place, _by_rows(local[n].shape[0]), "sum_chips_" + n) for n in names]
    (dx, _, dg_mix), reduced = _rms_bwd(dh, x2, g_mix, dx1, "norm_mix_bwd", ex=_join_exchange(halves))

    small_grads = {"norm_mix_g": dg_mix, "b_forget": db_f[:, :FOX_HEADS], "b_gate": db_gate, "pool_w": dpool_w,
                   "pool_scale": dpool_scale, "norm_x_g": dg_x, "norm_mem_g": dg_mem, "norm_ffn_g": dg_ffn,
                   "norm_final_g": dg_final}
    def flat2d(a):
        return a.reshape(-1, a.shape[-1])

    small_names = [n for n, _ in SMALL]
    own = [flat2d(small_grads[n]) for n in small_names]
    small_gather = _small_exchange(own + [loss_part])

    def tiles_of(a):
        return a.transpose(2, 0, 1)

    def block_of(a3):
        return a3.transpose(1, 2, 0)

    grads, deltas, new_m, new_v = {}, {}, {}, {}
    gathered = None
    for n, g_ in zip(names, reduced):
        if n == "w_in":
            g_ = lax.optimization_barrier(g_.reshape(IN_COLS // N_CHIPS, 1, D_MODEL))
            (d_, m_, v_), gathered = _adamw(tiles_of(weights[n]), g_, tiles_of(moments_m[n]), tiles_of(moments_v[n]),
                                            "adamw_" + n, ex=small_gather)
            back = block_of
        else:
            d_, m_, v_ = _adamw(local[n], g_, shard2d(moments_m[n], n), shard2d(moments_v[n], n), "adamw_" + n)
            back = functools.partial(unshard, n=n)
        grads[n], deltas[n], new_m[n], new_v[n] = (back(a) for a in (g_, d_, m_, v_))

    device = (4 * lax.axis_index("x") + 2 * lax.axis_index("y") + lax.axis_index("c")).astype(jnp.int32).reshape(1)
    sg, sd, sm, sv, loss_sum = _adamw_small(
        gathered[:-1], own, [flat2d(weights[n]) for n in small_names], [flat2d(moments_m[n]) for n in small_names],
        [flat2d(moments_v[n]) for n in small_names], gathered[-1], loss_part, device)
    for n, g_, d_, m_, v_ in zip(small_names, sg, sd, sm, sv):
        grads[n], deltas[n], new_m[n], new_v[n] = (a.reshape(weights[n].shape) for a in (g_, d_, m_, v_))
    return loss_sum[0, 0], dx.reshape(nb, s, d), grads, deltas, new_m, new_v


def kernel(x, mem, norm_mix_g, w_in, b_forget, b_gate, pool_w, pool_scale, w_pool_out, w_fox_out, w_out, norm_x_g, norm_mem_g, w_xq, w_xkv, w_xo, norm_ffn_g, w_ffn_in, w_ffn_out, norm_final_g, loss_target, m_norm_mix_g, m_w_in, m_b_forget, m_b_gate, m_pool_w, m_pool_scale, m_w_pool_out, m_w_fox_out, m_w_out, m_norm_x_g, m_norm_mem_g, m_w_xq, m_w_xkv, m_w_xo, m_norm_ffn_g, m_w_ffn_in, m_w_ffn_out, m_norm_final_g, v_norm_mix_g, v_w_in, v_b_forget, v_b_gate, v_pool_w, v_pool_scale, v_w_pool_out, v_w_fox_out, v_w_out, v_norm_x_g, v_norm_mem_g, v_w_xq, v_w_xkv, v_w_xo, v_norm_ffn_g, v_w_ffn_in, v_w_ffn_out, v_norm_final_g):
    given = dict(locals())
    weights = {n: given[n] for n in WEIGHT_ORDER}
    moments_m = {n: given["m_" + n] for n in WEIGHT_ORDER}
    moments_v = {n: given["v_" + n] for n in WEIGHT_ORDER}
    loss, grad_x, grads, deltas, new_m, new_v = _step(x, mem, loss_target, weights, moments_m, moments_v)
    return (loss, grad_x, *[grads[n] for n in WEIGHT_ORDER], *[deltas[n] for n in WEIGHT_ORDER],
            *[new_m[n] for n in WEIGHT_ORDER], *[new_v[n] for n in WEIGHT_ORDER])
```

```python
import functools
import math

import jax
import jax.numpy as jnp
from jax import lax
from jax.experimental import pallas as pl
from jax.experimental.pallas import tpu as pltpu

F32 = jnp.float32
BF16 = jnp.bfloat16
MESH = pl.DeviceIdType.MESH

D_MODEL = 1024
EPS = 1e-6
POOL_WINDOWS = (2, 4, 8, 16)
POOL_WIDTH = 512
POOL_GC = 128
FOX_HEADS = 8
FOX_DH = 64
FOX_WIDTH = 512
X_HEADS = 4
X_DH = 128
X_WIDTH = 512
D_FF = 2816
IN_COLS = 4104
GATE_WIDTH = 2048
ADAM_LR = 0.001
ADAM_B1 = 0.9
ADAM_B2 = 0.999
ADAM_EPS = 1e-08
ADAM_WD = 0.01
ADAM_STEP = 10

N_CHIPS = 4
N_DEV = 8
LANES = 128
VMEM_LIMIT_BYTES = 56 * 1024 * 1024
NEG_INF = -1e30
ATT_BLOCK = 512

SHARDED = (
    ("w_in", (1024, IN_COLS), 1),
    ("w_pool_out", (POOL_WIDTH, 1024), 1),
    ("w_fox_out", (FOX_WIDTH, 1024), 1),
    ("w_out", (1024, 1024), 0),
    ("w_xq", (1024, X_WIDTH), 0),
    ("w_xkv", (1024, 2 * X_WIDTH), 0),
    ("w_xo", (X_WIDTH, 1024), 1),
    ("w_ffn_in", (1024, 2 * D_FF), 1),
    ("w_ffn_out", (D_FF, 1024), 0),
)
SMALL = (
    ("norm_mix_g", (1, 1024)),
    ("b_forget", (1, 8)),
    ("b_gate", (1, 2048)),
    ("pool_w", (1, 4, 128, 128)),
    ("pool_scale", (1, 512)),
    ("norm_x_g", (1, 1024)),
    ("norm_mem_g", (1, 1024)),
    ("norm_ffn_g", (1, 1024)),
    ("norm_final_g", (1024,)),
)
WEIGHT_ORDER = ("norm_mix_g", "w_in", "b_forget", "b_gate", "pool_w", "pool_scale", "w_pool_out", "w_fox_out", "w_out",
                "norm_x_g", "norm_mem_g", "w_xq", "w_xkv", "w_xo", "norm_ffn_g", "w_ffn_in", "w_ffn_out", "norm_final_g")


def _cparams(sem=None):
    return pltpu.CompilerParams(dimension_semantics=sem, vmem_limit_bytes=VMEM_LIMIT_BYTES)


def _block(dim, pref, unit):
    if dim <= pref:
        return dim
    best = None
    for b in range(unit, pref + 1, unit):
        if dim % b == 0:
            best = b
    assert best is not None, (dim, pref, unit)
    return best


def _rows_block(rows, cols, unit=16, elems=1 << 19):
    return _block(rows, max(unit, elems // cols // unit * unit), unit)


def _my_place():
    return lax.axis_index("x"), lax.axis_index("y"), lax.axis_index("c")


def _other_chips(x, y):
    return [(1 - x, y), (x, 1 - y), (1 - x, 1 - y)]


def _chip(place):
    return 2 * place[0] + place[1]


ANY = pl.BlockSpec(memory_space=pl.ANY)


def _by_rows(rows):
    return rows % 32 == 0


def _half_shape(rows, cols):
    return (rows // 2, cols) if _by_rows(rows) else (rows, cols // 2)


def _core_half(ref, core, lead=()):
    rows, cols = ref.shape[-2:]
    if _by_rows(rows):
        return ref.at[(*lead, pl.ds(core * (rows // 2), rows // 2), slice(None))]
    return ref.at[(*lead, slice(None), pl.ds(core * (cols // 2), cols // 2))]


class _Exchange:
    def __init__(self, arrays, out_shapes, n_sems, start, finish, in_place=False):
        self.arrays, self.out_shapes, self.n_sems, self.start, self.finish = arrays, out_shapes, n_sems, start, finish
        self.in_place = in_place

    def scratch(self):
        return [pltpu.SemaphoreType.DMA((self.n_sems,)), pltpu.SemaphoreType.DMA((self.n_sems,))]

    def aliases(self, first_in, first_out):
        return {first_in + k: first_out + k for k in range(len(self.arrays))} if self.in_place else {}


def _run_exchange(ex, name):
    n = len(ex.arrays)

    def body(*refs):
        ins, outs, sems = refs[:n], refs[n:2 * n], refs[2 * n:]
        ex.start(ins, outs, *sems)
        ex.finish(ins, outs, *sems)

    return pl.pallas_call(
        body, name=name, out_shape=ex.out_shapes, in_specs=[ANY] * n, out_specs=[ANY] * n, scratch_shapes=ex.scratch(),
        input_output_aliases=ex.aliases(0, 0),
    )(*ex.arrays)


def _hosted_call(body, ex, *, name, grid, in_specs, out_specs, out_shape, args, scratch=()):
    n_in, n_out, n_scr = len(args), len(out_shape), len(scratch)
    if ex is None:
        outs = pl.pallas_call(
            body, name=name, grid=grid, out_shape=out_shape, in_specs=in_specs, out_specs=out_specs,
            scratch_shapes=list(scratch), compiler_params=_cparams(("arbitrary",) * len(grid)))(*args)
        return outs, None
    nc = len(ex.arrays)

    def full_body(*refs):
        ins, cins = refs[:n_in], refs[n_in:n_in + nc]
        outs, couts = refs[n_in + nc:n_in + nc + n_out], refs[n_in + nc + n_out:n_in + 2 * nc + n_out]
        rest = refs[n_in + 2 * nc + n_out:]
        scr, sems = rest[:n_scr], rest[n_scr:]
        first = functools.reduce(jnp.logical_and, [pl.program_id(a) == 0 for a in range(len(grid))])
        last = functools.reduce(jnp.logical_and, [pl.program_id(a) == grid[a] - 1 for a in range(len(grid))])

        @pl.when(first)
        def _():
            ex.start(cins, couts, *sems)

        body(*ins, *outs, *scr)

        @pl.when(last)
        def _():
            ex.finish(cins, couts, *sems)

    outs = pl.pallas_call(
        full_body, name=name, grid=grid, out_shape=list(out_shape) + list(ex.out_shapes),
        in_specs=list(in_specs) + [ANY] * nc, out_specs=list(out_specs) + [ANY] * nc,
        scratch_shapes=list(scratch) + ex.scratch(), input_output_aliases=ex.aliases(n_in, n_out),
        compiler_params=_cparams(("arbitrary",) * len(grid)))(*args, *ex.arrays)
    return outs[:n_out], outs[n_out:]


def _gather_exchange(shards):
    n = len(shards)

    def copies(ins, outs, send_sems, recv_sems):
        x, y, c = _my_place()

        def half(k, chip, core):
            return _core_half(outs[k], core, lead=(_chip(chip),))

        def copy(k, slot, chip, core, to, src=None):
            return pltpu.make_async_remote_copy(
                src_ref=half(k, chip, core) if src is None else src, dst_ref=half(k, chip, core),
                send_sem=send_sems.at[6 * k + slot], recv_sem=recv_sems.at[6 * k + slot],
                device_id=to, device_id_type=MESH)

        return (x, y, c), copy

    def first_copies(ins, outs, send_sems, recv_sems):
        (x, y, c), copy = copies(ins, outs, send_sems, recv_sems)
        out = []
        for j, chip in enumerate(_other_chips(x, y)):
            for k in range(n):
                out.append(copy(k, j, (x, y), c, (*chip, c), src=_core_half(ins[k], c)))
        return out

    def start(ins, outs, send_sems, recv_sems):
        for cp in first_copies(ins, outs, send_sems, recv_sems):
            cp.start()

    def finish(ins, outs, send_sems, recv_sems):
        (x, y, c), copy = copies(ins, outs, send_sems, recv_sems)
        chips = _other_chips(x, y)
        passed = []
        for j, chip in enumerate(chips):
            for k in range(n):
                copy(k, j, chip, c, (x, y, c)).wait_recv()
                passed.append(copy(k, 3 + j, chip, c, (x, y, 1 - c)))
                passed[-1].start()
        for j, chip in enumerate(chips):
            for k in range(n):
                copy(k, 3 + j, chip, 1 - c, (x, y, c)).wait_recv()
        for cp in first_copies(ins, outs, send_sems, recv_sems) + passed:
            cp.wait_send()

    return _Exchange(list(shards), [jax.ShapeDtypeStruct((N_CHIPS,) + s.shape, s.dtype) for s in shards], 6 * n,
                     start, finish)


def _place_own(stacks, shards):
    me = 2 * lax.axis_index("x") + lax.axis_index("y")
    return [lax.dynamic_update_slice(others, mine[None], (me, 0, 0)) for others, mine in zip(stacks, shards)]


def _swap_exchange(grads):
    n = len(grads)

    def copies(ins, outs, send_sems, recv_sems):
        x, y, c = _my_place()
        return [pltpu.make_async_remote_copy(
            src_ref=_core_half(ins[k], 1 - c, lead=(slice(None),)), dst_ref=outs[k],
            send_sem=send_sems.at[k], recv_sem=recv_sems.at[k], device_id=(x, y, 1 - c), device_id_type=MESH)
            for k in range(n)]

    def start(ins, outs, send_sems, recv_sems):
        for cp in copies(ins, outs, send_sems, recv_sems):
            cp.start()

    def finish(ins, outs, send_sems, recv_sems):
        for cp in copies(ins, outs, send_sems, recv_sems):
            cp.wait()

    return _Exchange(list(grads), [jax.ShapeDtypeStruct((N_CHIPS,) + _half_shape(*g.shape[1:]), g.dtype) for g in grads],
                     n, start, finish)


def _chips_exchange(sums):
    n = len(sums)

    def sends(ins, outs, send_sems, recv_sems):
        x, y, c = _my_place()
        return [pltpu.make_async_remote_copy(
            src_ref=ins[k].at[_chip(chip)], dst_ref=outs[k].at[_chip((x, y))],
            send_sem=send_sems.at[3 * k + j], recv_sem=recv_sems.at[3 * k + j],
            device_id=(*chip, c), device_id_type=MESH)
            for j, chip in enumerate(_other_chips(x, y)) for k in range(n)]

    def start(ins, outs, send_sems, recv_sems):
        for cp in sends(ins, outs, send_sems, recv_sems):
            cp.start()

    def finish(ins, outs, send_sems, recv_sems):
        x, y, c = _my_place()
        for j, chip in enumerate(_other_chips(x, y)):
            for k in range(n):
                slot = outs[k].at[_chip(chip)]
                pltpu.make_async_remote_copy(
                    src_ref=slot, dst_ref=slot, send_sem=send_sems.at[3 * k + j], recv_sem=recv_sems.at[3 * k + j],
                    device_id=(x, y, c), device_id_type=MESH).wait_recv()
        for cp in sends(ins, outs, send_sems, recv_sems):
            cp.wait_send()

    return _Exchange(list(sums), [jax.ShapeDtypeStruct(s.shape, s.dtype) for s in sums], 3 * n, start, finish)


def _join_exchange(shards):
    n = len(shards)

    def sends(ins, outs, send_sems, recv_sems):
        x, y, c = _my_place()
        return [pltpu.make_async_remote_copy(
            src_ref=_core_half(ins[k], c), dst_ref=_core_half(outs[k], c),
            send_sem=send_sems.at[k], recv_sem=recv_sems.at[k], device_id=(x, y, 1 - c), device_id_type=MESH)
            for k in range(n)]

    def start(ins, outs, send_sems, recv_sems):
        for cp in sends(ins, outs, send_sems, recv_sems):
            cp.start()

    def finish(ins, outs, send_sems, recv_sems):
        x, y, c = _my_place()
        for k in range(n):
            theirs = _core_half(outs[k], 1 - c)
            pltpu.make_async_remote_copy(
                src_ref=theirs, dst_ref=theirs, send_sem=send_sems.at[k], recv_sem=recv_sems.at[k],
                device_id=(x, y, c), device_id_type=MESH).wait_recv()
        for cp in sends(ins, outs, send_sems, recv_sems):
            cp.wait_send()

    return _Exchange(list(shards), [jax.ShapeDtypeStruct(s.shape, s.dtype) for s in shards], n, start, finish,
                     in_place=True)


def _small_exchange(blocks):
    n = len(blocks)

    def copies(ins, outs, send_sems, recv_sems):
        x, y, c = _my_place()

        def copy(k, j, whose, to, src=None):
            slot = outs[k].at[4 * whose[0] + 2 * whose[1] + whose[2]]
            return pltpu.make_async_remote_copy(
                src_ref=slot if src is None else src, dst_ref=slot,
                send_sem=send_sems.at[7 * k + j], recv_sem=recv_sems.at[7 * k + j], device_id=to, device_id_type=MESH)

        return (x, y, c), copy

    def first_copies(ins, outs, send_sems, recv_sems):
        (x, y, c), copy = copies(ins, outs, send_sems, recv_sems)
        out = []
        for k in range(n):
            out.append(copy(k, 0, (x, y, c), (x, y, 1 - c), src=ins[k]))
            out += [copy(k, 1 + j, (x, y, c), (*chip, c), src=ins[k]) for j, chip in enumerate(_other_chips(x, y))]
        return out

    def start(ins, outs, send_sems, recv_sems):
        for cp in first_copies(ins, outs, send_sems, recv_sems):
            cp.start()

    def finish(ins, outs, send_sems, recv_sems):
        (x, y, c), copy = copies(ins, outs, send_sems, recv_sems)
        chips = _other_chips(x, y)
        passed = []
        for j, chip in enumerate(chips):
            for k in range(n):
                copy(k, 1 + j, (*chip, c), (x, y, c)).wait_recv()
                passed.append(copy(k, 4 + j, (*chip, c), (x, y, 1 - c)))
                passed[-1].start()
        for k in range(n):
            copy(k, 0, (x, y, 1 - c), (x, y, c)).wait_recv()
        for j, chip in enumerate(chips):
            for k in range(n):
                copy(k, 4 + j, (*chip, 1 - c), (x, y, c)).wait_recv()
        for cp in first_copies(ins, outs, send_sems, recv_sems) + passed:
            cp.wait_send()

    return _Exchange(list(blocks), [jax.ShapeDtypeStruct((N_DEV,) + blk.shape, blk.dtype) for blk in blocks], 7 * n,
                     start, finish)


def _sum_halves(grads, theirs, core, name):
    _, h, cols = theirs.shape
    by_rows = _by_rows(grads.shape[1])
    br = _rows_block(h, cols) if by_rows else h
    nb = h // br

    def body(core_ref, a_ref, b_ref, o_ref):
        o_ref[...] = (a_ref[...] + b_ref[...]).astype(BF16)

    if by_rows:
        mine = pl.BlockSpec((1, br, cols), lambda j, i, core_ref: (j, core_ref[0] * nb + i, 0))
    else:
        mine = pl.BlockSpec((1, br, cols), lambda j, i, core_ref: (j, i, core_ref[0]))
    return pl.pallas_call(
        body, name=name,
        out_shape=jax.ShapeDtypeStruct(theirs.shape, BF16),
        grid_spec=pltpu.PrefetchScalarGridSpec(
            num_scalar_prefetch=1, grid=(N_CHIPS, nb),
            in_specs=[mine, pl.BlockSpec((1, br, cols), lambda j, i, core_ref: (j, i, 0))],
            out_specs=pl.BlockSpec((1, br, cols), lambda j, i, core_ref: (j, i, 0))),
        compiler_params=_cparams(("parallel", "parallel")),
    )(core, grads, theirs)


def _sum_chips(slots, sums, place, by_rows, name):
    _, h, cols = slots.shape
    br = _rows_block(h, cols) if by_rows else h
    nb = h // br

    def body(place_ref, s_ref, own_ref, o_ref):
        me = place_ref[1]
        acc = None
        for k in range(N_CHIPS):
            term = jnp.where(me == k, own_ref[k], s_ref[k]).astype(F32)
            acc = term if acc is None else acc + term
        o_ref[...] = acc

    stack = pl.BlockSpec((N_CHIPS, br, cols), lambda i, place_ref: (0, i, 0))
    if by_rows:
        out_shape, out_map = (2 * h, cols), lambda i, place_ref: (place_ref[0] * nb + i, 0)
    else:
        out_shape, out_map = (h, 2 * cols), lambda i, place_ref: (i, place_ref[0])
    return pl.pallas_call(
        body, name=name,
        out_shape=jax.ShapeDtypeStruct(out_shape, F32),
        grid_spec=pltpu.PrefetchScalarGridSpec(
            num_scalar_prefetch=1, grid=(nb,), in_specs=[stack, stack],
            out_specs=pl.BlockSpec((br, cols), out_map)),
        compiler_params=_cparams(("parallel",)),
    )(place, slots, sums)


def _adamw_math(w, g, m, v):
    m = ADAM_B1 * m + (1.0 - ADAM_B1) * g
    v = ADAM_B2 * v + (1.0 - ADAM_B2) * (g * g)
    m_hat = m / (1.0 - ADAM_B1 ** ADAM_STEP)
    v_hat = v / (1.0 - ADAM_B2 ** ADAM_STEP)
    delta = -ADAM_LR * (m_hat / (jnp.sqrt(v_hat) + ADAM_EPS) + ADAM_WD * w)
    return delta, m, v


def _adamw(w, g, m, v, name, ex=None):
    def body(w_ref, g_ref, m_ref, v_ref, d_ref, nm_ref, nv_ref):
        d, nm, nv = _adamw_math(w_ref[...], g_ref[...], m_ref[...], v_ref[...])
        d_ref[...] = d
        nm_ref[...] = nm
        nv_ref[...] = nv

    if w.ndim == 3:
        rows = w.shape[0]
        br = max(b for b in range(1, 65) if rows % b == 0)
        spec, steps = pl.BlockSpec((br,) + w.shape[1:], lambda i: (i, 0, 0)), rows // br
    else:
        rows, cols = w.shape
        br = _rows_block(rows, cols, unit=8)
        spec, steps = pl.BlockSpec((br, cols), lambda i: (i, 0)), rows // br
    shape = jax.ShapeDtypeStruct(w.shape, F32)
    outs, moved = _hosted_call(
        body, ex, name=name, out_shape=(shape, shape, shape), grid=(steps,),
        in_specs=[spec] * 4, out_specs=(spec, spec, spec), args=(w, g, m, v))
    return tuple(outs) if ex is None else (tuple(outs), moved)


def _adamw_small(parts, own, ws, ms, vs, loss_parts, loss_own, device):
    n = len(ws)

    def total(device_ref, parts_ref, own_ref):
        acc = None
        for dev in range(N_DEV):
            term = jnp.where(device_ref[0] == dev, own_ref[...], parts_ref[dev])
            acc = term if acc is None else acc + term
        return acc

    def body(device_ref, *refs):
        ins, outs = refs[:5 * n + 2], refs[5 * n + 2:]
        for k in range(n):
            g = total(device_ref, ins[k], ins[n + k])
            d, nm, nv = _adamw_math(ins[2 * n + k][...], g, ins[3 * n + k][...], ins[4 * n + k][...])
            for o_ref, val in zip(outs[k::n][:4], (g, d, nm, nv)):
                o_ref[...] = val
        outs[4 * n][...] = total(device_ref, ins[5 * n], ins[5 * n + 1])

    args = list(parts) + list(own) + list(ws) + list(ms) + list(vs) + [loss_parts, loss_own]
    whole = lambda a: pl.BlockSpec(a.shape, lambda i, device_ref, nd=a.ndim: (0,) * nd)
    shapes = [jax.ShapeDtypeStruct(w.shape, F32) for w in ws] * 4 + [jax.ShapeDtypeStruct(loss_own.shape, F32)]
    outs = pl.pallas_call(
        body, name="adamw_small", out_shape=shapes,
        grid_spec=pltpu.PrefetchScalarGridSpec(
            num_scalar_prefetch=1, grid=(1,), in_specs=[whole(a) for a in args], out_specs=[whole(a) for a in shapes]),
        compiler_params=_cparams(("arbitrary",)),
    )(device, *args)
    return outs[:n], outs[n:2 * n], outs[2 * n:3 * n], outs[3 * n:4 * n], outs[4 * n]


def _mm(a, b, *, name, ta=False, out_dtype=F32, res=None, bm=1024, bn=1024, bk=4096, b_stack=False, out_stack=False,
        ex=None):
    if ta:
        kdim, m = a.shape
    else:
        m, kdim = a.shape
    if b_stack:
        _, kb, chunk = b.shape
        n = N_CHIPS * chunk
    else:
        kb, n = b.shape
        chunk = n // N_CHIPS if out_stack else n
    assert kdim == kb, (a.shape, b.shape, ta)
    bm = _block(m, bm, LANES if ta else 16)
    bn = _block(chunk, bn, LANES)
    bk = _block(kdim, bk, LANES)
    nk = kdim // bk
    per_chunk = chunk // bn
    dims = (((0 if ta else 1,), (0,)), ((), ()))

    def body(*refs):
        refs = list(refs)
        a_ref, b_ref = refs[:2]
        r_ref = refs[2] if res is not None else None
        o_ref = refs[3] if res is not None else refs[2]
        part = lax.dot_general(a_ref[...].astype(BF16), b_ref[...].astype(BF16), dims, preferred_element_type=F32)

        def finish(r):
            if r_ref is not None:
                r = r + r_ref[...]
            o_ref[...] = r.astype(out_dtype)

        if nk == 1:
            finish(part)
        else:
            acc_ref = refs[-1]
            k = pl.program_id(2)

            @pl.when(k == 0)
            def _():
                acc_ref[...] = part

            @pl.when(k > 0)
            def _():
                acc_ref[...] += part

            @pl.when(k == nk - 1)
            def _():
                finish(acc_ref[...])

    a_spec = pl.BlockSpec((bk, bm), lambda i, j, k: (k, i)) if ta else pl.BlockSpec((bm, bk), lambda i, j, k: (i, k))
    if b_stack:
        b_spec = pl.BlockSpec((None, bk, bn), lambda i, j, k: (j // per_chunk, k, j % per_chunk))
    else:
        b_spec = pl.BlockSpec((bk, bn), lambda i, j, k: (k, j))
    r_spec = pl.BlockSpec((bm, bn), lambda i, j, k: (i, j))
    if out_stack:
        o_spec = pl.BlockSpec((None, bm, bn), lambda i, j, k: (j // per_chunk, i, j % per_chunk))
        o_shape = (N_CHIPS, m, chunk)
    else:
        o_spec, o_shape = r_spec, (m, n)
    in_specs = [a_spec, b_spec] + ([r_spec] if res is not None else [])
    args = (a, b) + ((res,) if res is not None else ())
    (out,), moved = _hosted_call(
        body, ex, name=name, out_shape=(jax.ShapeDtypeStruct(o_shape, out_dtype),),
        grid=(m // bm, n // bn, nk), in_specs=in_specs, out_specs=(o_spec,),
        scratch=[pltpu.VMEM((bm, bn), F32)] if nk > 1 else [], args=args)
    return out if ex is None else (out, moved)


def _rms_fwd(x, g, name, ex=None):
    t, d = x.shape
    bt = _block(t, 512, 16)

    def body(x_ref, g_ref, h_ref):
        xv = x_ref[...]
        r = lax.rsqrt(jnp.mean(xv * xv, axis=-1, keepdims=True) + EPS)
        h_ref[...] = (xv * r * g_ref[...]).astype(BF16)

    (out,), moved = _hosted_call(
        body, ex, name=name, out_shape=(jax.ShapeDtypeStruct((t, d), BF16),), grid=(t // bt,),
        in_specs=[pl.BlockSpec((bt, d), lambda i: (i, 0)), pl.BlockSpec((1, d), lambda i: (0, 0))],
        out_specs=(pl.BlockSpec((bt, d), lambda i: (i, 0)),), args=(x, g))
    return out if ex is None else (out, moved)


def _rms_bwd(dh, x, g, dres, name, ex=None):
    t, d = x.shape
    bt = _block(t, 256, 16)
    want_dx = dres is not None

    def body(*refs):
        if want_dx:
            dh_ref, x_ref, g_ref, dres_ref, dx_ref, dxb_ref, dg_ref = refs
        else:
            dh_ref, x_ref, g_ref, dg_ref = refs
        xv = x_ref[...]
        r = lax.rsqrt(jnp.mean(xv * xv, axis=-1, keepdims=True) + EPS)
        xhat = xv * r
        dhv = dh_ref[...]

        @pl.when(pl.program_id(0) == 0)
        def _():
            dg_ref[...] = jnp.zeros_like(dg_ref)

        dg_ref[...] += jnp.sum(dhv * xhat, axis=0, keepdims=True)
        if want_dx:
            dxhat = dhv * g_ref[...]
            dx = dres_ref[...] + r * (dxhat - xhat * jnp.mean(dxhat * xhat, axis=-1, keepdims=True))
            dx_ref[...] = dx
            dxb_ref[...] = dx.astype(BF16)

    row = pl.BlockSpec((bt, d), lambda i: (i, 0))
    vec = pl.BlockSpec((1, d), lambda i: (0, 0))
    if want_dx:
        outs, moved = _hosted_call(
            body, ex, name=name, grid=(t // bt,),
            out_shape=(jax.ShapeDtypeStruct((t, d), F32), jax.ShapeDtypeStruct((t, d), BF16),
                       jax.ShapeDtypeStruct((1, d), F32)),
            in_specs=[row, row, vec, row], out_specs=(row, row, vec), args=(dh, x, g, dres))
        return tuple(outs) if ex is None else (tuple(outs), moved)
    return pl.pallas_call(
        body, name=name, grid=(t // bt,), out_shape=jax.ShapeDtypeStruct((1, d), F32),
        in_specs=[row, row, vec], out_specs=vec,
        compiler_params=_cparams(("arbitrary",)),
    )(dh, x, g)


def _final_loss(x, target, g):
    t, d = x.shape
    bt = _block(t, 256, 16)

    def body(x_ref, t_ref, g_ref, dx_ref, dxb_ref, dg_ref, loss_ref):
        xv = x_ref[...]
        gv = g_ref[...]
        r = lax.rsqrt(jnp.mean(xv * xv, axis=-1, keepdims=True) + EPS)
        xhat = xv * r
        err = xhat * gv - t_ref[...]

        @pl.when(pl.program_id(0) == 0)
        def _():
            dg_ref[...] = jnp.zeros_like(dg_ref)
            loss_ref[...] = jnp.zeros_like(loss_ref)

        loss_ref[...] += 0.5 * jnp.sum(jnp.mean(err * err, axis=-1, keepdims=True), axis=0, keepdims=True)
        dy = err * (1.0 / d)
        dg_ref[...] += jnp.sum(dy * xhat, axis=0, keepdims=True)
        dxhat = dy * gv
        dx = r * (dxhat - xhat * jnp.mean(dxhat * xhat, axis=-1, keepdims=True))
        dx_ref[...] = dx
        dxb_ref[...] = dx.astype(BF16)

    row = pl.BlockSpec((bt, d), lambda i: (i, 0))
    vec = pl.BlockSpec((1, d), lambda i: (0, 0))
    return pl.pallas_call(
        body, name="final_loss", grid=(t // bt,),
        out_shape=(jax.ShapeDtypeStruct((t, d), F32), jax.ShapeDtypeStruct((t, d), BF16),
                   jax.ShapeDtypeStruct((1, d), F32), jax.ShapeDtypeStruct((1, LANES), F32)),
        in_specs=[row, row, vec], out_specs=(row, row, vec, pl.BlockSpec((1, LANES), lambda i: (0, 0))),
        compiler_params=_cparams(("arbitrary",)),
    )(x, target, g)


def _mm_res_norm(a, b, res, g, name):
    t, k = a.shape
    d = b.shape[1]
    bm = _block(t, 512, 16)

    def body(a_ref, b_ref, r_ref, g_ref, x_ref, h_ref):
        xv = jnp.dot(a_ref[...], b_ref[...], preferred_element_type=F32) + r_ref[...]
        x_ref[...] = xv
        r = lax.rsqrt(jnp.mean(xv * xv, axis=-1, keepdims=True) + EPS)
        h_ref[...] = (xv * r * g_ref[...]).astype(BF16)

    row = pl.BlockSpec((bm, d), lambda i: (i, 0))
    return pl.pallas_call(
        body, name=name, grid=(t // bm,),
        out_shape=(jax.ShapeDtypeStruct((t, d), F32), jax.ShapeDtypeStruct((t, d), BF16)),
        in_specs=[pl.BlockSpec((bm, k), lambda i: (i, 0)), pl.BlockSpec((k, d), lambda i: (0, 0)), row,
                  pl.BlockSpec((1, d), lambda i: (0, 0))],
        out_specs=(row, row), compiler_params=_cparams(("parallel",)),
    )(a, b, res, g)


def _mm_norm_bwd(a, b, x, g, dres, name):
    t, k = a.shape
    d = b.shape[1]
    bm = _block(t, 512, 16)

    def body(a_ref, b_ref, x_ref, g_ref, dres_ref, dx_ref, dxb_ref, dg_ref):
        @pl.when(pl.program_id(0) == 0)
        def _():
            dg_ref[...] = jnp.zeros_like(dg_ref)

        dhv = jnp.dot(a_ref[...], b_ref[...], preferred_element_type=F32)
        xv = x_ref[...]
        r = lax.rsqrt(jnp.mean(xv * xv, axis=-1, keepdims=True) + EPS)
        xhat = xv * r
        dg_ref[...] += jnp.sum(dhv * xhat, axis=0, keepdims=True)
        dxhat = dhv * g_ref[...]
        dx = dres_ref[...] + r * (dxhat - xhat * jnp.mean(dxhat * xhat, axis=-1, keepdims=True))
        dx_ref[...] = dx
        dxb_ref[...] = dx.astype(BF16)

    row = pl.BlockSpec((bm, d), lambda i: (i, 0))
    vec = pl.BlockSpec((1, d), lambda i: (0, 0))
    return pl.pallas_call(
        body, name=name, grid=(t // bm,),
        out_shape=(jax.ShapeDtypeStruct((t, d), F32), jax.ShapeDtypeStruct((t, d), BF16), jax.ShapeDtypeStruct((1, d), F32)),
        in_specs=[pl.BlockSpec((bm, k), lambda i: (i, 0)), pl.BlockSpec((k, d), lambda i: (0, 0)), row, vec, row],
        out_specs=(row, row, vec), compiler_params=_cparams(("arbitrary",)),
    )(a, b, x, g, dres)


GU_COLS = GATE_WIDTH + POOL_WIDTH
U_BLK = GATE_WIDTH // POOL_WIDTH


def _shift_down(a, k, row):
    return jnp.where(row >= k, pltpu.roll(a, k, 0), 0.0)


def _shift_up(a, k, row):
    n = a.shape[0]
    return jnp.where(row < n - k, pltpu.roll(a, n - k, 0), 0.0)


def _window_delta(u, w, row):
    s, k = u, 1
    while k < w:
        s = s + _shift_down(s, k, row)
        k *= 2
    cnt = jnp.minimum(row + 1, w).astype(F32)
    return s / cnt - u, cnt


def _pool_fwd(gu, pool_w, pool_scale):
    b, s, _ = gu.shape

    def body(u_ref, pw_ref, sc_ref, y_ref):
        row = lax.broadcasted_iota(jnp.int32, (s, POOL_GC), 0)
        for g, w in enumerate(POOL_WINDOWS):
            cols = slice(g * POOL_GC, (g + 1) * POOL_GC)
            d, _ = _window_delta(u_ref[0, :, cols].astype(F32), w, row)
            z = jnp.dot(d.astype(BF16), pw_ref[g].astype(BF16), preferred_element_type=F32)
            y_ref[0, :, cols] = (z * sc_ref[:, cols]).astype(BF16)

    return pl.pallas_call(
        body, name="pool_fwd", out_shape=jax.ShapeDtypeStruct((b, s, POOL_WIDTH), BF16), grid=(b,),
        in_specs=[pl.BlockSpec((1, s, POOL_WIDTH), lambda i: (i, 0, U_BLK)),
                  pl.BlockSpec((4, POOL_GC, POOL_GC), lambda i: (0, 0, 0)),
                  pl.BlockSpec((1, POOL_WIDTH), lambda i: (0, 0))],
        out_specs=pl.BlockSpec((1, s, POOL_WIDTH), lambda i: (i, 0, 0)),
        compiler_params=_cparams(("parallel",)),
    )(gu, pool_w, pool_scale)


def _pool_bwd(gu, dy, pool_w, pool_scale, dgu):
    b, s, _ = gu.shape

    def body(u_ref, dy_ref, pw_ref, sc_ref, dgu_in, du_ref, dpw_ref, dsc_ref):
        del dgu_in

        @pl.when(pl.program_id(0) == 0)
        def _():
            dpw_ref[...] = jnp.zeros_like(dpw_ref)
            dsc_ref[...] = jnp.zeros_like(dsc_ref)

        row = lax.broadcasted_iota(jnp.int32, (s, POOL_GC), 0)
        for g, w in enumerate(POOL_WINDOWS):
            cols = slice(g * POOL_GC, (g + 1) * POOL_GC)
            d, cnt = _window_delta(u_ref[0, :, cols].astype(F32), w, row)
            db = d.astype(BF16)
            pw = pw_ref[g].astype(BF16)
            z = jnp.dot(db, pw, preferred_element_type=F32)
            dyv = dy_ref[0, :, cols]
            dsc_ref[:, cols] += jnp.sum(dyv * z, axis=0, keepdims=True)
            dz = (dyv * sc_ref[:, cols]).astype(BF16)
            dpw_ref[g] += lax.dot_general(db, dz, (((0,), (0,)), ((), ())), preferred_element_type=F32)
            dd = lax.dot_general(dz, pw, (((1,), (1,)), ((), ())), preferred_element_type=F32)
            acc, k = dd / cnt, 1
            while k < w:
                acc = acc + _shift_up(acc, k, row)
                k *= 2
            du_ref[0, :, cols] = (acc - dd).astype(BF16)

    return pl.pallas_call(
        body, name="pool_bwd", grid=(b,),
        out_shape=(jax.ShapeDtypeStruct((b, s, GU_COLS), BF16), jax.ShapeDtypeStruct((4, POOL_GC, POOL_GC), F32),
                   jax.ShapeDtypeStruct((1, POOL_WIDTH), F32)),
        in_specs=[pl.BlockSpec((1, s, POOL_WIDTH), lambda i: (i, 0, U_BLK)),
                  pl.BlockSpec((1, s, POOL_WIDTH), lambda i: (i, 0, 0)),
                  pl.BlockSpec((4, POOL_GC, POOL_GC), lambda i: (0, 0, 0)),
                  pl.BlockSpec((1, POOL_WIDTH), lambda i: (0, 0)), ANY],
        out_specs=(pl.BlockSpec((1, s, POOL_WIDTH), lambda i: (i, 0, U_BLK)),
                   pl.BlockSpec((4, POOL_GC, POOL_GC), lambda i: (0, 0, 0)),
                   pl.BlockSpec((1, POOL_WIDTH), lambda i: (0, 0))),
        input_output_aliases={4: 0},
        compiler_params=_cparams(("arbitrary",)),
    )(gu, dy, pool_w, pool_scale, dgu)


def _forget_cumsum(f, bias, name):
    b, s, c = f.shape

    def body(f_ref, b_ref, c_ref):
        row = lax.broadcasted_iota(jnp.int32, (s, LANES), 0)
        z = f_ref[0] + b_ref[...]
        acc = jnp.minimum(z, 0.0) - jnp.log(1.0 + jnp.exp(-jnp.abs(z)))
        k = 1
        while k < s:
            acc = acc + _shift_down(acc, k, row)
            k *= 2
        c_ref[0] = acc

    return pl.pallas_call(
        body, name=name, out_shape=jax.ShapeDtypeStruct((b, s, c), F32), grid=(b, c // LANES),
        in_specs=[pl.BlockSpec((1, s, LANES), lambda i, j: (i, 0, j)), pl.BlockSpec((1, LANES), lambda i, j: (0, j))],
        out_specs=pl.BlockSpec((1, s, LANES), lambda i, j: (i, 0, j)),
        compiler_params=_cparams(("parallel", "parallel")),
    )(f, bias)


def _forget_bwd(dc, f, bias):
    b, s, _ = f.shape

    def body(dc_ref, f_ref, b_ref, df_ref, db_ref):
        @pl.when(pl.program_id(0) == 0)
        def _():
            db_ref[...] = jnp.zeros_like(db_ref)

        row = lax.broadcasted_iota(jnp.int32, (s, LANES), 0)
        acc, k = dc_ref[0], 1
        while k < s:
            acc = acc + _shift_up(acc, k, row)
            k *= 2
        z = f_ref[0] + b_ref[...]
        df = acc / (1.0 + jnp.exp(z))
        db_ref[...] += jnp.sum(df, axis=0, keepdims=True)
        df_ref[0] = df.astype(BF16)

    blk = pl.BlockSpec((1, s, LANES), lambda i: (i, 0, 0))
    vec = pl.BlockSpec((1, LANES), lambda i: (0, 0))
    return pl.pallas_call(
        body, name="forget_bwd", grid=(b,),
        out_shape=(jax.ShapeDtypeStruct((b, s, LANES), BF16), jax.ShapeDtypeStruct((1, LANES), F32)),
        in_specs=[blk, blk, vec], out_specs=(blk, vec),
        compiler_params=_cparams(("arbitrary",)),
    )(dc, f, bias)


KV_BLK0 = 2
PAIRS = FOX_HEADS // 2
FOX_SCALE = FOX_DH ** -0.5
NT_DIMS = (((1,), (1,)), ((), ()))
TN_DIMS = (((0,), (0,)), ((), ()))


def _stack_heads(v):
    head = lax.broadcasted_iota(jnp.int32, v.shape, 1) // FOX_DH
    zero = jnp.zeros_like(v)
    return jnp.concatenate([jnp.where(head == 0, v, zero), jnp.where(head == 1, v, zero)], axis=0)


def _stack_cols(v):
    return jnp.concatenate([v[:, 0:1], v[:, FOX_DH:FOX_DH + 1]], axis=0)


def _unstack(t, blk):
    head = lax.broadcasted_iota(jnp.int32, (blk, LANES), 1) // FOX_DH
    return jnp.where(head == 0, t[:blk], t[blk:])


def _fox_scores(q_all, kblk, row_bias, cr_ref, kb, masked, blk):
    top = lax.broadcasted_iota(jnp.int32, (2 * blk, 1), 0) < blk
    s = lax.dot_general(q_all, kblk, NT_DIMS, preferred_element_type=F32)
    s = s + (row_bias - jnp.where(top, cr_ref[0, 0, kb], cr_ref[0, 1, kb]))
    if masked:
        r = lax.broadcasted_iota(jnp.int32, (2 * blk, blk), 0)
        keep = jnp.where(r >= blk, r - blk, r) >= lax.broadcasted_iota(jnp.int32, (2 * blk, blk), 1)
        s = jnp.where(keep, s, NEG_INF)
    return s


def _fox_fwd(qkv, c_exp, c_row, ex=None):
    b, s, _ = qkv.shape
    blk = min(ATT_BLOCK, s)
    nq = s // blk

    def body(q_ref, kv_ref, cc_ref, cr_ref, o_ref, ob_ref, lse_ref):
        qi = pl.program_id(2)
        q_all = _stack_heads(q_ref[0] * FOX_SCALE)
        cq = _stack_cols(cc_ref[0])

        def step(kb, carry, masked):
            m, l, acc = carry
            rows = pl.ds(pl.multiple_of(kb * blk, blk), blk)
            sc = _fox_scores(q_all, kv_ref[0, rows, :LANES], cq, cr_ref, kb, masked, blk)
            m_new = jnp.maximum(m, jnp.max(sc, axis=-1, keepdims=True))
            p = jnp.exp(sc - m_new)
            alpha = jnp.exp(m - m_new)
            l = alpha * l + jnp.sum(p, axis=-1, keepdims=True)
            acc = alpha * acc + jnp.dot(p.astype(BF16), kv_ref[0, rows, LANES:], preferred_element_type=F32)
            return m_new, l, acc

        init = (jnp.full((2 * blk, 1), NEG_INF, F32), jnp.zeros((2 * blk, 1), F32), jnp.zeros((2 * blk, LANES), F32))
        m, l, acc = step(qi, lax.fori_loop(0, qi, functools.partial(step, masked=False), init), True)
        o = _unstack(acc / l, blk)
        o_ref[0] = o
        ob_ref[0] = o.astype(BF16)
        lse_ref[0] = _unstack(jnp.broadcast_to(m + jnp.log(l), (2 * blk, LANES)), blk)

    tile = pl.BlockSpec((1, blk, LANES), lambda i, h, q: (i, q, h))
    kvspec = pl.BlockSpec((1, s, 2 * LANES), lambda i, h, q: (i, 0, KV_BLK0 + h))
    shape = jax.ShapeDtypeStruct((b, s, FOX_WIDTH), F32)
    return _hosted_call(
        body, ex, name="fox_fwd", out_shape=(shape, jax.ShapeDtypeStruct((b, s, FOX_WIDTH), BF16), shape),
        grid=(b, PAIRS, nq),
        in_specs=[tile, kvspec, tile, pl.BlockSpec((1, 2, nq, 1, blk), lambda i, h, q: (i, h, 0, 0, 0))],
        out_specs=(tile, tile, tile), args=(qkv, qkv, c_exp, c_row))


def _fox_bwd(qkv, c_exp, c_row, lse, o, do, ex=None):
    b, s, _ = qkv.shape
    blk = min(ATT_BLOCK, s)
    nq = s // blk

    def body(q_ref, kv_ref, cc_ref, cr_ref, lse_ref, o_ref, do_ref, dq_ref, dkv_ref, dcq_ref, dc_ref, dk_acc, dv_acc):
        qi = pl.program_id(2)

        @pl.when(qi == 0)
        def _():
            dk_acc[...] = jnp.zeros_like(dk_acc)
            dv_acc[...] = jnp.zeros_like(dv_acc)
            dc_ref[...] = jnp.zeros_like(dc_ref)

        q_all = _stack_heads(q_ref[0] * FOX_SCALE)
        dov = do_ref[0]
        do_all = _stack_heads(dov.astype(BF16))
        delta = jnp.sum(_stack_heads(dov * o_ref[0]), axis=-1, keepdims=True)
        bias = _stack_cols(cc_ref[0]) - _stack_cols(lse_ref[0])

        def step(kb, carry, masked):
            acc, dcq = carry
            rows = pl.ds(pl.multiple_of(kb * blk, blk), blk)
            kblk = kv_ref[0, rows, :LANES]
            p = jnp.exp(_fox_scores(q_all, kblk, bias, cr_ref, kb, masked, blk))
            dp = lax.dot_general(do_all, kv_ref[0, rows, LANES:], NT_DIMS, preferred_element_type=F32)
            ds = p * (dp - delta)
            dsb = ds.astype(BF16)
            dv_acc[rows, :] += lax.dot_general(p.astype(BF16), do_all, TN_DIMS, preferred_element_type=F32)
            dk_acc[rows, :] += lax.dot_general(dsb, q_all, TN_DIMS, preferred_element_type=F32)
            dc_ref[0, 0, kb] -= jnp.sum(ds[:blk], axis=0, keepdims=True)
            dc_ref[0, 1, kb] -= jnp.sum(ds[blk:], axis=0, keepdims=True)
            acc = acc + jnp.dot(dsb, kblk, preferred_element_type=F32)
            return acc, dcq + jnp.sum(ds, axis=-1, keepdims=True)

        init = (jnp.zeros((2 * blk, LANES), F32), jnp.zeros((2 * blk, 1), F32))
        acc, dcq = step(qi, lax.fori_loop(0, qi, functools.partial(step, masked=False), init), True)
        dq_ref[0] = (_unstack(acc, blk) * FOX_SCALE).astype(BF16)
        dcq_ref[0, 0] = jnp.where(lax.broadcasted_iota(jnp.int32, (blk, 2), 1) == 0, dcq[:blk], dcq[blk:])

        @pl.when(qi == nq - 1)
        def _():
            dkv_ref[0, :, :LANES] = dk_acc[...].astype(BF16)
            dkv_ref[0, :, LANES:] = dv_acc[...].astype(BF16)

    tile = pl.BlockSpec((1, blk, LANES), lambda i, h, q: (i, q, h))
    kvspec = pl.BlockSpec((1, s, 2 * LANES), lambda i, h, q: (i, 0, KV_BLK0 + h))
    crow = pl.BlockSpec((1, 2, nq, 1, blk), lambda i, h, q: (i, h, 0, 0, 0))
    return _hosted_call(
        body, ex, name="fox_bwd", grid=(b, PAIRS, nq),
        out_shape=(jax.ShapeDtypeStruct((b, s, FOX_WIDTH), BF16), jax.ShapeDtypeStruct((b, s, 2 * FOX_WIDTH), BF16),
                   jax.ShapeDtypeStruct((b, PAIRS, s, 2), F32), jax.ShapeDtypeStruct(c_row.shape, F32)),
        in_specs=[tile, kvspec, tile, crow, tile, tile, tile],
        out_specs=(tile, pl.BlockSpec((1, s, 2 * LANES), lambda i, h, q: (i, 0, h)),
                   pl.BlockSpec((1, 1, blk, 2), lambda i, h, q: (i, h, q, 0)), crow),
        scratch=[pltpu.VMEM((s, LANES), F32), pltpu.VMEM((s, LANES), F32)],
        args=(qkv, qkv, c_exp, c_row, lse, o, do))


def _sigmoid(z):
    return 1.0 / (1.0 + jnp.exp(-z))


def _branches_mix(y, o, w_pool3, w_fox3, gu, b_gate):
    t = y.shape[0]
    chunk = w_pool3.shape[2]
    per_branch = D_MODEL // chunk
    bm = _block(t, 1024, 16)

    def body(y_ref, o_ref, wp_ref, wf_ref, gp_ref, gf_ref, bp_ref, bf_ref, yp_ref, yf_ref, mix_ref):
        yp = jnp.dot(y_ref[...], wp_ref[...], preferred_element_type=F32).astype(BF16)
        yf = jnp.dot(o_ref[...], wf_ref[...], preferred_element_type=F32).astype(BF16)
        yp_ref[...] = yp
        yf_ref[...] = yf
        gp = _sigmoid(gp_ref[...].astype(F32) + bp_ref[...])
        gf = _sigmoid(gf_ref[...].astype(F32) + bf_ref[...])
        mix_ref[...] = (gp * yp.astype(F32) + gf * yf.astype(F32)).astype(BF16)

    rows = pl.BlockSpec((bm, y.shape[1]), lambda i, j: (i, 0))
    weight = pl.BlockSpec((None, y.shape[1], chunk), lambda i, j: (j, 0, 0))
    tile = lambda base: pl.BlockSpec((bm, chunk), lambda i, j: (i, base + j))
    vec = lambda base: pl.BlockSpec((1, chunk), lambda i, j: (0, base + j))
    shape = jax.ShapeDtypeStruct((t, D_MODEL), BF16)
    return pl.pallas_call(
        body, name="branches_mix", out_shape=(shape, shape, shape), grid=(t // bm, per_branch),
        in_specs=[rows, rows, weight, weight, tile(0), tile(per_branch), vec(0), vec(per_branch)],
        out_specs=(tile(0), tile(0), tile(0)),
        compiler_params=_cparams(("parallel", "arbitrary")),
    )(y, o, w_pool3, w_fox3, gu, gu, b_gate, b_gate)


def _mix_bwd(gu, b_gate, y_pool, y_fox, dx, w_out_t):
    t = gu.shape[0]
    bt = _block(t, 256, 16)

    def body(gp_ref, gf_ref, bp_ref, bf_ref, yp_ref, yf_ref, dx_ref, w_ref, dyp_ref, dyf_ref, dgl_ref, db_ref):
        @pl.when(pl.program_id(0) == 0)
        def _():
            db_ref[...] = jnp.zeros_like(db_ref)

        dm = jnp.dot(dx_ref[...], w_ref[...], preferred_element_type=F32)
        gp = _sigmoid(gp_ref[...].astype(F32) + bp_ref[...])
        gf = _sigmoid(gf_ref[...].astype(F32) + bf_ref[...])
        dyp_ref[...] = (dm * gp).astype(BF16)
        dyf_ref[...] = (dm * gf).astype(BF16)
        dlp = dm * yp_ref[...].astype(F32) * gp * (1.0 - gp)
        dlf = dm * yf_ref[...].astype(F32) * gf * (1.0 - gf)
        dgl_ref[:, :D_MODEL] = dlp.astype(BF16)
        dgl_ref[:, D_MODEL:] = dlf.astype(BF16)
        db_ref[:, :D_MODEL] += jnp.sum(dlp, axis=0, keepdims=True)
        db_ref[:, D_MODEL:] += jnp.sum(dlf, axis=0, keepdims=True)

    col = lambda j: pl.BlockSpec((bt, D_MODEL), lambda i: (i, j))
    vec = lambda j: pl.BlockSpec((1, D_MODEL), lambda i: (0, j))
    wide = pl.BlockSpec((bt, GATE_WIDTH), lambda i: (i, 0))
    return pl.pallas_call(
        body, name="mix_bwd", grid=(t // bt,),
        out_shape=(jax.ShapeDtypeStruct((t, D_MODEL), BF16), jax.ShapeDtypeStruct((t, D_MODEL), BF16),
                   jax.ShapeDtypeStruct((t, GU_COLS), BF16), jax.ShapeDtypeStruct((1, GATE_WIDTH), F32)),
        in_specs=[col(0), col(1), vec(0), vec(1), col(0), col(0), col(0),
                  pl.BlockSpec(w_out_t.shape, lambda i: (0, 0))],
        out_specs=(col(0), col(0), wide, pl.BlockSpec((1, GATE_WIDTH), lambda i: (0, 0))),
        compiler_params=_cparams(("arbitrary",)),
    )(gu, gu, b_gate, b_gate, y_pool, y_fox, dx, w_out_t)


X_SCALE = X_DH ** -0.5


def _xattn_probs(qh, kh):
    s = lax.dot_general(qh, kh, NT_DIMS, preferred_element_type=F32) * X_SCALE
    e = jnp.exp(s - jnp.max(s, axis=-1, keepdims=True))
    return e / jnp.sum(e, axis=-1, keepdims=True)


def _xattn_fwd(q, kv):
    b, s, _ = q.shape
    m = kv.shape[1]
    bq = _block(s, 512, 16)

    def body(q_ref, kv_ref, o_ref):
        for h in range(X_HEADS):
            cols = slice(h * X_DH, (h + 1) * X_DH)
            p = _xattn_probs(q_ref[0, :, cols], kv_ref[0, :, cols])
            vh = kv_ref[0, :, X_WIDTH + h * X_DH:X_WIDTH + (h + 1) * X_DH]
            o_ref[0, :, cols] = jnp.dot(p.astype(BF16), vh, preferred_element_type=F32).astype(BF16)

    return pl.pallas_call(
        body, name="xattn_fwd", out_shape=jax.ShapeDtypeStruct((b, s, X_WIDTH), BF16), grid=(b, s // bq),
        in_specs=[pl.BlockSpec((1, bq, X_WIDTH), lambda i, j: (i, j, 0)),
                  pl.BlockSpec((1, m, 2 * X_WIDTH), lambda i, j: (i, 0, 0))],
        out_specs=pl.BlockSpec((1, bq, X_WIDTH), lambda i, j: (i, j, 0)),
        compiler_params=_cparams(("parallel", "parallel")),
    )(q, kv)


def _xattn_bwd(q, kv, do):
    b, s, _ = q.shape
    m = kv.shape[1]
    bq = _block(s, 512, 16)

    def body(q_ref, kv_ref, do_ref, dq_ref, dkv_ref):
        @pl.when(pl.program_id(1) == 0)
        def _():
            dkv_ref[...] = jnp.zeros_like(dkv_ref)

        for h in range(X_HEADS):
            cols = slice(h * X_DH, (h + 1) * X_DH)
            vcols = slice(X_WIDTH + h * X_DH, X_WIDTH + (h + 1) * X_DH)
            qh, kh, vh, doh = q_ref[0, :, cols], kv_ref[0, :, cols], kv_ref[0, :, vcols], do_ref[0, :, cols]
            p = _xattn_probs(qh, kh)
            dkv_ref[0, :, vcols] += lax.dot_general(p.astype(BF16), doh, TN_DIMS, preferred_element_type=F32)
            dp = lax.dot_general(doh, vh, NT_DIMS, preferred_element_type=F32)
            ds = (p * (dp - jnp.sum(p * dp, axis=-1, keepdims=True)) * X_SCALE).astype(BF16)
            dq_ref[0, :, cols] = jnp.dot(ds, kh, preferred_element_type=F32).astype(BF16)
            dkv_ref[0, :, cols] += lax.dot_general(ds, qh, TN_DIMS, preferred_element_type=F32)

    tile = pl.BlockSpec((1, bq, X_WIDTH), lambda i, j: (i, j, 0))
    mem = pl.BlockSpec((1, m, 2 * X_WIDTH), lambda i, j: (i, 0, 0))
    return pl.pallas_call(
        body, name="xattn_bwd", grid=(b, s // bq),
        out_shape=(jax.ShapeDtypeStruct((b, s, X_WIDTH), BF16), jax.ShapeDtypeStruct((b, m, 2 * X_WIDTH), F32)),
        in_specs=[tile, mem, tile], out_specs=(tile, mem),
        compiler_params=_cparams(("parallel", "arbitrary")),
    )(q, kv, do)


def _ffn_in(hf, w3):
    t, d = hf.shape
    chunk = w3.shape[2]
    half = N_CHIPS // 2
    bm = _block(t, 1024, 16)

    def body(a_ref, wg_ref, wu_ref, gt_ref, up_ref, act_ref):
        a = a_ref[...]
        gt = jnp.dot(a, wg_ref[...], preferred_element_type=F32).astype(BF16)
        up = jnp.dot(a, wu_ref[...], preferred_element_type=F32).astype(BF16)
        gt_ref[...] = gt
        up_ref[...] = up
        g32 = gt.astype(F32)
        act_ref[...] = (g32 * _sigmoid(g32) * up.astype(F32)).astype(BF16)

    tile = pl.BlockSpec((bm, chunk), lambda i, j: (i, j))
    shape = jax.ShapeDtypeStruct((t, half * chunk), BF16)
    return pl.pallas_call(
        body, name="ffn_in", out_shape=(shape, shape, shape), grid=(t // bm, half),
        in_specs=[pl.BlockSpec((bm, d), lambda i, j: (i, 0)),
                  pl.BlockSpec((None, d, chunk), lambda i, j: (j, 0, 0)),
                  pl.BlockSpec((None, d, chunk), lambda i, j: (j + half, 0, 0))],
        out_specs=(tile, tile, tile),
        compiler_params=_cparams(("parallel", "arbitrary")),
    )(hf, w3, w3)


def _ffn_act_bwd(dx, w_out_t, gate, up):
    t, d = dx.shape
    bt = _block(t, 256, 16)

    def body(dx_ref, w_ref, gt_ref, up_ref, o_ref):
        da = jnp.dot(dx_ref[...], w_ref[...], preferred_element_type=F32).astype(BF16).astype(F32)
        gt = gt_ref[...].astype(F32)
        sg = _sigmoid(gt)
        silu = gt * sg
        o_ref[:, :D_FF] = (da * up_ref[...].astype(F32) * (sg + silu * (1.0 - sg))).astype(BF16)
        o_ref[:, D_FF:] = (da * silu).astype(BF16)

    col = pl.BlockSpec((bt, D_FF), lambda i: (i, 0))
    return pl.pallas_call(
        body, name="ffn_act_bwd", out_shape=jax.ShapeDtypeStruct((t, 2 * D_FF), BF16), grid=(t // bt,),
        in_specs=[pl.BlockSpec((bt, d), lambda i: (i, 0)), pl.BlockSpec((d, D_FF), lambda i: (0, 0)), col, col],
        out_specs=pl.BlockSpec((bt, 2 * D_FF), lambda i: (i, 0)),
        compiler_params=_cparams(("parallel",)),
    )(dx, w_out_t, gate, up)


def _stack_of(w, axis):
    r, c = w.shape
    if axis == 0:
        return w.reshape(N_CHIPS, r // N_CHIPS, c)
    return w.reshape(r, N_CHIPS, c // N_CHIPS).transpose(1, 0, 2)


def _stack_t(w3):
    n, r, c = w3.shape
    return w3.transpose(0, 2, 1).reshape(n * c, r)


def _pair_rows(k, v):
    c = k.shape[1]
    return jnp.stack([k.reshape(PAIRS, LANES, c), v.reshape(PAIRS, LANES, c)], axis=1).reshape(2 * FOX_WIDTH, c)


def _unpair_rows(kv):
    c = kv.shape[1]
    kv = kv.reshape(PAIRS, 2, LANES, c)
    return kv[:, 0].reshape(FOX_WIDTH, c), kv[:, 1].reshape(FOX_WIDTH, c)


def _input_grad(parts, weights_t, ex):
    t = parts[0].shape[0]
    d = weights_t[0].shape[1]
    bm = _block(t, 512, 16)
    n = len(parts)

    def body(*refs):
        acc = None
        for a_ref, b_ref in zip(refs[:n], refs[n:2 * n]):
            term = jnp.dot(a_ref[...], b_ref[...], preferred_element_type=F32)
            acc = term if acc is None else acc + term
        refs[2 * n][...] = acc

    (out,), moved = _hosted_call(
        body, ex, name="d_h", grid=(t // bm,), out_shape=(jax.ShapeDtypeStruct((t, d), F32),),
        in_specs=[pl.BlockSpec((bm, p.shape[1]), lambda i: (i, 0)) for p in parts]
        + [pl.BlockSpec(w.shape, lambda i: (0, 0)) for w in weights_t],
        out_specs=(pl.BlockSpec((bm, d), lambda i: (i, 0)),), args=tuple(parts) + tuple(weights_t))
    return out, moved


def _step(x, mem, loss_target, weights, moments_m, moments_v):
    nb, s, d = x.shape
    n_mem = mem.shape[1]
    t = nb * s
    blk = min(ATT_BLOCK, s)
    x2 = x.reshape(t, d)
    mem2 = mem.reshape(nb * n_mem, d)
    tgt2 = loss_target.reshape(t, d)

    def shard2d(a, n):
        a = a.reshape(a.shape[1:])
        return a.T if n == "w_in" else a

    def unshard(a, n):
        return (a.T if n == "w_in" else a)[None]

    local = {n: shard2d(weights[n], n) for n, _, _ in SHARDED}

    names = [n for n, _, _ in SHARDED]
    last = ["w_ffn_out"]
    later = [n for n in names if n != "w_in" and n not in last]
    local_b = {n: local[n].astype(BF16) for n in names}
    g_mix = weights["norm_mix_g"]
    h, w_in_others = _rms_fwd(x2, g_mix, "norm_mix", ex=_gather_exchange([local_b["w_in"]]))
    w_in_stack, = _place_own(w_in_others, [local_b["w_in"]])

    def w_in_rows(lo, hi):
        per = IN_COLS // N_CHIPS
        parts = [w_in_stack[j, max(lo, j * per) - j * per:min(hi, (j + 1) * per) - j * per]
                 for j in range(N_CHIPS) if max(lo, j * per) < min(hi, (j + 1) * per)]
        return parts[0] if len(parts) == 1 else jnp.concatenate(parts)

    w_gu_t = jnp.concatenate([w_in_rows(2056, IN_COLS), w_in_rows(0, 512)])
    w_qkv_t = jnp.concatenate([w_in_rows(512, 1024), _pair_rows(w_in_rows(1024, 1536), w_in_rows(1536, 2048))])
    w_f_t = jnp.pad(w_in_rows(2048, 2056), ((0, LANES - FOX_HEADS), (0, 0)))
    w_gu, w_qkv, w_f = w_gu_t.T, w_qkv_t.T, w_f_t.T
    w_f_exp = jnp.repeat(w_f[:, :FOX_HEADS], FOX_DH, axis=1)

    g_mix, g_x, g_mem, g_ffn = (weights[n] for n in ("norm_mix_g", "norm_x_g", "norm_mem_g", "norm_ffn_g"))
    g_final = weights["norm_final_g"].reshape(1, d)
    pool_w = weights["pool_w"].reshape(4, POOL_GC, POOL_GC)
    pool_scale, b_gate = weights["pool_scale"], weights["b_gate"]
    b_f_pad = jnp.pad(weights["b_forget"], ((0, 0), (0, LANES - FOX_HEADS)))
    b_f_exp = jnp.repeat(weights["b_forget"], FOX_DH, axis=1)

    gu, last_others = _mm(h, w_gu, out_dtype=BF16, bn=512, name="in_proj_gates_pool",
                          ex=_gather_exchange([local_b[n] for n in last]))
    qkv = _mm(h, w_qkv, out_dtype=BF16, bn=512, name="in_proj_qkv")
    f_pad = _mm(h, w_f, name="in_proj_forget")
    gu3, qkv3 = gu.reshape(nb, s, GU_COLS), qkv.reshape(nb, s, 3 * FOX_WIDTH)
    y = _pool_fwd(gu3, pool_w, pool_scale)
    f_exp = _mm(h, w_f_exp, name="in_proj_forget_lanes").reshape(nb, s, FOX_WIDTH)
    c_exp = _forget_cumsum(f_exp, b_f_exp, "forget_cumsum_lanes")
    c_pad = _forget_cumsum(f_pad.reshape(nb, s, LANES), b_f_pad, "forget_cumsum")
    c_row = c_pad[:, :, :FOX_HEADS].transpose(0, 2, 1).reshape(nb, FOX_HEADS, s // blk, 1, blk)
    (o, o_b, lse), gathered = _fox_fwd(qkv3, c_exp, c_row, ex=_gather_exchange([local_b[n] for n in later]))
    stacks = dict(zip(later, _place_own(gathered, [local_b[n] for n in later])))
    stacks.update(zip(last, _place_own(last_others, [local_b[n] for n in last])))
    w_pool_out3, w_fox_out3, w_xo3, w_ffn_in3 = (stacks[n] for n in ("w_pool_out", "w_fox_out", "w_xo", "w_ffn_in"))
    w_out, w_xq, w_xkv, w_ffn_out = (stacks[n].reshape(-1, stacks[n].shape[2])
                                     for n in ("w_out", "w_xq", "w_xkv", "w_ffn_out"))
    y2, o2 = y.reshape(t, POOL_WIDTH), o_b.reshape(t, FOX_WIDTH)
    y_pool, y_fox, mix = _branches_mix(y2, o2, w_pool_out3, w_fox_out3, gu, b_gate)
    x1, hx = _mm_res_norm(mix, w_out, x2, g_x, "mix_out_norm_x")
    mem_n = _rms_fwd(mem2, g_mem, "norm_mem")
    qx = _mm(hx, w_xq, out_dtype=BF16, name="x_q")
    kv = _mm(mem_n, w_xkv, out_dtype=BF16, name="x_kv")
    qx3, kv3 = qx.reshape(nb, s, X_WIDTH), kv.reshape(nb, n_mem, 2 * X_WIDTH)
    ox = _xattn_fwd(qx3, kv3).reshape(t, X_WIDTH)
    w_xo = w_xo3.transpose(1, 0, 2).reshape(X_WIDTH, D_MODEL)
    x2_, hf = _mm_res_norm(ox, w_xo, x1, g_ffn, "x_out_norm_ffn")
    ffn_gate, ffn_up, act = _ffn_in(hf, w_ffn_in3)
    x3 = _mm(act, w_ffn_out, res=x2_, name="ffn_out")

    dx3, dx3_b, dg_final, loss_part = _final_loss(x3, tgt2, g_final)
    dw_ffn_out = _mm(act, dx3_b, ta=True, bm=1408, bn=512, bk=2048, name="d_w_ffn_out")
    dffn = _ffn_act_bwd(dx3_b, w_ffn_out.T, ffn_gate, ffn_up)
    dw_ffn_in = _mm(hf, dffn, ta=True, bm=512, bn=1408, bk=2048, out_stack=True, name="d_w_ffn_in")
    core = lax.axis_index("c").astype(jnp.int32).reshape(1)
    ffn_group = ["w_ffn_in", "w_ffn_out"]
    mid_group = ["w_pool_out", "w_fox_out", "w_out", "w_xq", "w_xkv", "w_xo"]
    grad_stacks = {"w_ffn_in": dw_ffn_in, "w_ffn_out": _stack_of(dw_ffn_out, 0)}

    def presum(group, theirs):
        return [_sum_halves(grad_stacks[n], t_, core, "sum_halves_" + n) for n, t_ in zip(group, theirs)]

    dhf, theirs = _mm(dffn, _stack_t(w_ffn_in3), bk=2816, name="d_hf",
                      ex=_swap_exchange([grad_stacks[n] for n in ffn_group]))
    chip_sums = dict(zip(ffn_group, presum(ffn_group, theirs)))
    dx2, dx2_b, dg_ffn = _rms_bwd(dhf, x2_, g_ffn, dx3, "norm_ffn_bwd")

    dw_xo = _mm(ox, dx2_b, ta=True, bn=256, out_stack=True, name="d_w_xo")
    dox = _mm(dx2_b, _stack_t(w_xo3), out_dtype=BF16, name="d_ox").reshape(nb, s, X_WIDTH)
    dqx, dkv = _xattn_bwd(qx3, kv3, dox)
    dqx2, dkv2 = dqx.reshape(t, X_WIDTH), dkv.reshape(nb * n_mem, 2 * X_WIDTH)
    dw_xkv = _mm(mem_n, dkv2, ta=True, name="d_w_xkv")
    dmem_n = _mm(dkv2, w_xkv.T, name="d_mem_n")
    dg_mem = _rms_bwd(dmem_n, mem2, g_mem, None, "norm_mem_bwd")
    dw_xq = _mm(hx, dqx2, ta=True, name="d_w_xq")
    dx1, dx1_b, dg_x = _mm_norm_bwd(dqx2, w_xq.T, x1, g_x, dx2, "d_hx_norm_x_bwd")

    dw_out = _mm(mix, dx1_b, ta=True, name="d_w_out")
    dyp, dyf, dgu, db_gate = _mix_bwd(gu, b_gate, y_pool, y_fox, dx1_b, w_out.T)
    dw_pool_out = _mm(y2, dyp, ta=True, bn=256, out_stack=True, name="d_w_pool_out")
    dw_fox_out = _mm(o2, dyf, ta=True, bn=256, out_stack=True, name="d_w_fox_out")
    dy = _mm(dyp, _stack_t(w_pool_out3), name="d_y").reshape(nb, s, POOL_WIDTH)
    do = _mm(dyf, _stack_t(w_fox_out3), name="d_o").reshape(nb, s, FOX_WIDTH)
    dgu3, dpool_w, dpool_scale = _pool_bwd(gu3, dy, pool_w, pool_scale, dgu.reshape(nb, s, GU_COLS))
    grad_stacks.update({"w_pool_out": dw_pool_out, "w_fox_out": dw_fox_out, "w_out": _stack_of(dw_out, 0),
                        "w_xq": _stack_of(dw_xq, 0), "w_xkv": _stack_of(dw_xkv, 0), "w_xo": dw_xo})
    dgu2 = dgu3.reshape(t, GU_COLS)
    dw_gu_t, theirs = _mm(dgu2, h, ta=True, name="d_w_gates_pool",
                          ex=_swap_exchange([grad_stacks[n] for n in mid_group]))
    chip_sums.update(zip(mid_group, presum(mid_group, theirs)))
    early = ffn_group + mid_group
    (dq3, dkv3, dc_q, dc_row), early_slots = _fox_bwd(qkv3, c_exp, c_row, lse, o, do,
                                                      ex=_chips_exchange([chip_sums[n] for n in early]))
    slots = dict(zip(early, early_slots))
    dc = dc_row.reshape(nb, FOX_HEADS, s).transpose(0, 2, 1) + dc_q.transpose(0, 2, 1, 3).reshape(nb, s, FOX_HEADS)
    dc = jnp.pad(dc, ((0, 0), (0, 0), (0, LANES - FOX_HEADS)))
    df, db_f = _forget_bwd(dc, f_pad.reshape(nb, s, LANES), b_f_pad)
    dq2, dkv2, df2 = dq3.reshape(t, FOX_WIDTH), dkv3.reshape(t, 2 * FOX_WIDTH), df.reshape(t, LANES)
    dw_q_t = _mm(dq2, h, ta=True, name="d_w_q")
    dw_kv_t = _mm(dkv2, h, ta=True, name="d_w_kv")
    dw_f_t = _mm(df2, h, ta=True, name="d_w_forget")
    dw_k_t, dw_v_t = _unpair_rows(dw_kv_t)
    dw_in_t = jnp.concatenate([dw_gu_t[GATE_WIDTH:], dw_q_t, dw_k_t, dw_v_t, dw_f_t[:FOX_HEADS],
                               dw_gu_t[:GATE_WIDTH]])
    grad_stacks["w_in"] = dw_in_t.reshape(N_CHIPS, IN_COLS // N_CHIPS, D_MODEL)
    chip_sums["w_in"], = presum(["w_in"], _run_exchange(_swap_exchange([grad_stacks["w_in"]]), "swap_halves_w_in"))
    dh, (slots["w_in"],) = _input_grad([dgu2, dq2, dkv2, df2],
                                       [w_gu_t, w_qkv_t[:FOX_WIDTH], w_qkv_t[FOX_WIDTH:], w_f_t],
                                       _chips_exchange([chip_sums["w_in"]]))

    place = jnp.stack([lax.axis_index("c"), 2 * lax.axis_index("x") + lax.axis_index("y")]).astype(jnp.int32)
    halves = [_sum_chips(slots[n], chip_sums[n], place, _by_rows(local[n].shape[0]), "sum_chips_" + n) for n in names]
    (dx, _, dg_mix), reduced = _rms_bwd(dh, x2, g_mix, dx1, "norm_mix_bwd", ex=_join_exchange(halves))

    small_grads = {"norm_mix_g": dg_mix, "b_forget": db_f[:, :FOX_HEADS], "b_gate": db_gate, "pool_w": dpool_w,
                   "pool_scale": dpool_scale, "norm_x_g": dg_x, "norm_mem_g": dg_mem, "norm_ffn_g": dg_ffn,
                   "norm_final_g": dg_final}
    def flat2d(a):
        return a.reshape(-1, a.shape[-1])

    small_names = [n for n, _ in SMALL]
    own = [flat2d(small_grads[n]) for n in small_names]
    small_gather = _small_exchange(own + [loss_part])

    def tiles_of(a):
        return a.transpose(2, 0, 1)

    def block_of(a3):
        return a3.transpose(1, 2, 0)

    grads, deltas, new_m, new_v = {}, {}, {}, {}
    gathered = None
    for n, g_ in zip(names, reduced):
        if n == "w_in":
            g_ = lax.optimization_barrier(g_.reshape(IN_COLS // N_CHIPS, 1, D_MODEL))
            (d_, m_, v_), gathered = _adamw(tiles_of(weights[n]), g_, tiles_of(moments_m[n]), tiles_of(moments_v[n]),
                                            "adamw_" + n, ex=small_gather)
            back = block_of
        else:
            d_, m_, v_ = _adamw(local[n], g_, shard2d(moments_m[n], n), shard2d(moments_v[n], n), "adamw_" + n)
            back = functools.partial(unshard, n=n)
        grads[n], deltas[n], new_m[n], new_v[n] = (back(a) for a in (g_, d_, m_, v_))

    device = (4 * lax.axis_index("x") + 2 * lax.axis_index("y") + lax.axis_index("c")).astype(jnp.int32).reshape(1)
    sg, sd, sm, sv, loss_sum = _adamw_small(
        gathered[:-1], own, [flat2d(weights[n]) for n in small_names], [flat2d(moments_m[n]) for n in small_names],
        [flat2d(moments_v[n]) for n in small_names], gathered[-1], loss_part, device)
    for n, g_, d_, m_, v_ in zip(small_names, sg, sd, sm, sv):
        grads[n], deltas[n], new_m[n], new_v[n] = (a.reshape(weights[n].shape) for a in (g_, d_, m_, v_))
    return loss_sum[0, 0], dx.reshape(nb, s, d), grads, deltas, new_m, new_v


def kernel(x, mem, norm_mix_g, w_in, b_forget, b_gate, pool_w, pool_scale, w_pool_out, w_fox_out, w_out, norm_x_g, norm_mem_g, w_xq, w_xkv, w_xo, norm_ffn_g, w_ffn_in, w_ffn_out, norm_final_g, loss_target, m_norm_mix_g, m_w_in, m_b_forget, m_b_gate, m_pool_w, m_pool_scale, m_w_pool_out, m_w_fox_out, m_w_out, m_norm_x_g, m_norm_mem_g, m_w_xq, m_w_xkv, m_w_xo, m_norm_ffn_g, m_w_ffn_in, m_w_ffn_out, m_norm_final_g, v_norm_mix_g, v_w_in, v_b_forget, v_b_gate, v_pool_w, v_pool_scale, v_w_pool_out, v_w_fox_out, v_w_out, v_norm_x_g, v_norm_mem_g, v_w_xq, v_w_xkv, v_w_xo, v_norm_ffn_g, v_w_ffn_in, v_w_ffn_out, v_norm_final_g):
    given = dict(locals())
    weights = {n: given[n] for n in WEIGHT_ORDER}
    moments_m = {n: given["m_" + n] for n in WEIGHT_ORDER}
    moments_v = {n: given["v_" + n] for n in WEIGHT_ORDER}
    loss, grad_x, grads, deltas, new_m, new_v = _step(x, mem, loss_target, weights, moments_m, moments_v)
    return (loss, grad_x, *[grads[n] for n in WEIGHT_ORDER], *[deltas[n] for n in WEIGHT_ORDER],
            *[new_m[n] for n in WEIGHT_ORDER], *[new_v[n] for n in WEIGHT_ORDER])
```

```python
import functools
import math

import jax
import jax.numpy as jnp
from jax import lax
from jax.experimental import pallas as pl
from jax.experimental.pallas import tpu as pltpu

F32 = jnp.float32
BF16 = jnp.bfloat16
MESH = pl.DeviceIdType.MESH

D_MODEL = 1024
EPS = 1e-6
POOL_WINDOWS = (2, 4, 8, 16)
POOL_WIDTH = 512
POOL_GC = 128
FOX_HEADS = 8
FOX_DH = 64
FOX_WIDTH = 512
X_HEADS = 4
X_DH = 128
X_WIDTH = 512
D_FF = 2816
IN_COLS = 4104
GATE_WIDTH = 2048
ADAM_LR = 0.001
ADAM_B1 = 0.9
ADAM_B2 = 0.999
ADAM_EPS = 1e-08
ADAM_WD = 0.01
ADAM_STEP = 10

N_CHIPS = 4
N_DEV = 8
LANES = 128
VMEM_LIMIT_BYTES = 56 * 1024 * 1024
NEG_INF = -1e30
ATT_BLOCK = 512

SHARDED = (
    ("w_in", (1024, IN_COLS), 1),
    ("w_pool_out", (POOL_WIDTH, 1024), 1),
    ("w_fox_out", (FOX_WIDTH, 1024), 1),
    ("w_out", (1024, 1024), 0),
    ("w_xq", (1024, X_WIDTH), 0),
    ("w_xkv", (1024, 2 * X_WIDTH), 0),
    ("w_xo", (X_WIDTH, 1024), 1),
    ("w_ffn_in", (1024, 2 * D_FF), 1),
    ("w_ffn_out", (D_FF, 1024), 0),
)
SMALL = (
    ("norm_mix_g", (1, 1024)),
    ("b_forget", (1, 8)),
    ("b_gate", (1, 2048)),
    ("pool_w", (1, 4, 128, 128)),
    ("pool_scale", (1, 512)),
    ("norm_x_g", (1, 1024)),
    ("norm_mem_g", (1, 1024)),
    ("norm_ffn_g", (1, 1024)),
    ("norm_final_g", (1024,)),
)
WEIGHT_ORDER = ("norm_mix_g", "w_in", "b_forget", "b_gate", "pool_w", "pool_scale", "w_pool_out", "w_fox_out", "w_out",
                "norm_x_g", "norm_mem_g", "w_xq", "w_xkv", "w_xo", "norm_ffn_g", "w_ffn_in", "w_ffn_out", "norm_final_g")


def _cparams(sem=None):
    return pltpu.CompilerParams(dimension_semantics=sem, vmem_limit_bytes=VMEM_LIMIT_BYTES)


def _block(dim, pref, unit):
    if dim <= pref:
        return dim
    best = None
    for b in range(unit, pref + 1, unit):
        if dim % b == 0:
            best = b
    assert best is not None, (dim, pref, unit)
    return best


def _rows_block(rows, cols, unit=16, elems=1 << 19):
    return _block(rows, max(unit, elems // cols // unit * unit), unit)


def _my_place():
    return lax.axis_index("x"), lax.axis_index("y"), lax.axis_index("c")


def _other_chips(x, y):
    return [(1 - x, y), (x, 1 - y), (1 - x, 1 - y)]


def _chip(place):
    return 2 * place[0] + place[1]


ANY = pl.BlockSpec(memory_space=pl.ANY)


def _by_rows(rows):
    return rows % 32 == 0


def _half_shape(rows, cols):
    return (rows // 2, cols) if _by_rows(rows) else (rows, cols // 2)


def _core_half(ref, core, lead=()):
    rows, cols = ref.shape[-2:]
    if _by_rows(rows):
        return ref.at[(*lead, pl.ds(core * (rows // 2), rows // 2), slice(None))]
    return ref.at[(*lead, slice(None), pl.ds(core * (cols // 2), cols // 2))]


class _Exchange:
    def __init__(self, arrays, out_shapes, n_sems, start, finish, in_place=False):
        self.arrays, self.out_shapes, self.n_sems, self.start, self.finish = arrays, out_shapes, n_sems, start, finish
        self.in_place = in_place

    def scratch(self):
        return [pltpu.SemaphoreType.DMA((self.n_sems,)), pltpu.SemaphoreType.DMA((self.n_sems,))]

    def aliases(self, first_in, first_out):
        return {first_in + k: first_out + k for k in range(len(self.arrays))} if self.in_place else {}


def _run_exchange(ex, name):
    n = len(ex.arrays)

    def body(*refs):
        ins, outs, sems = refs[:n], refs[n:2 * n], refs[2 * n:]
        ex.start(ins, outs, *sems)
        ex.finish(ins, outs, *sems)

    return pl.pallas_call(
        body, name=name, out_shape=ex.out_shapes, in_specs=[ANY] * n, out_specs=[ANY] * n, scratch_shapes=ex.scratch(),
        input_output_aliases=ex.aliases(0, 0),
    )(*ex.arrays)


def _hosted_call(body, ex, *, name, grid, in_specs, out_specs, out_shape, args, scratch=()):
    n_in, n_out, n_scr = len(args), len(out_shape), len(scratch)
    if ex is None:
        outs = pl.pallas_call(
            body, name=name, grid=grid, out_shape=out_shape, in_specs=in_specs, out_specs=out_specs,
            scratch_shapes=list(scratch), compiler_params=_cparams(("arbitrary",) * len(grid)))(*args)
        return outs, None
    nc = len(ex.arrays)

    def full_body(*refs):
        ins, cins = refs[:n_in], refs[n_in:n_in + nc]
        outs, couts = refs[n_in + nc:n_in + nc + n_out], refs[n_in + nc + n_out:n_in + 2 * nc + n_out]
        rest = refs[n_in + 2 * nc + n_out:]
        scr, sems = rest[:n_scr], rest[n_scr:]
        first = functools.reduce(jnp.logical_and, [pl.program_id(a) == 0 for a in range(len(grid))])
        last = functools.reduce(jnp.logical_and, [pl.program_id(a) == grid[a] - 1 for a in range(len(grid))])

        @pl.when(first)
        def _():
            ex.start(cins, couts, *sems)

        body(*ins, *outs, *scr)

        @pl.when(last)
        def _():
            ex.finish(cins, couts, *sems)

    outs = pl.pallas_call(
        full_body, name=name, grid=grid, out_shape=list(out_shape) + list(ex.out_shapes),
        in_specs=list(in_specs) + [ANY] * nc, out_specs=list(out_specs) + [ANY] * nc,
        scratch_shapes=list(scratch) + ex.scratch(), input_output_aliases=ex.aliases(n_in, n_out),
        compiler_params=_cparams(("arbitrary",) * len(grid)))(*args, *ex.arrays)
    return outs[:n_out], outs[n_out:]


def _gather_exchange(shards):
    n = len(shards)

    def copies(ins, outs, send_sems, recv_sems):
        x, y, c = _my_place()

        def half(k, chip, core):
            return _core_half(outs[k], core, lead=(_chip(chip),))

        def copy(k, slot, chip, core, to, src=None):
            return pltpu.make_async_remote_copy(
                src_ref=half(k, chip, core) if src is None else src, dst_ref=half(k, chip, core),
                send_sem=send_sems.at[6 * k + slot], recv_sem=recv_sems.at[6 * k + slot],
                device_id=to, device_id_type=MESH)

        return (x, y, c), copy

    def first_copies(ins, outs, send_sems, recv_sems):
        (x, y, c), copy = copies(ins, outs, send_sems, recv_sems)
        out = []
        for j, chip in enumerate(_other_chips(x, y)):
            for k in range(n):
                out.append(copy(k, j, (x, y), c, (*chip, c), src=_core_half(ins[k], c)))
        return out

    def start(ins, outs, send_sems, recv_sems):
        for cp in first_copies(ins, outs, send_sems, recv_sems):
            cp.start()

    def finish(ins, outs, send_sems, recv_sems):
        (x, y, c), copy = copies(ins, outs, send_sems, recv_sems)
        chips = _other_chips(x, y)
        passed = []
        for j, chip in enumerate(chips):
            for k in range(n):
                copy(k, j, chip, c, (x, y, c)).wait_recv()
                passed.append(copy(k, 3 + j, chip, c, (x, y, 1 - c)))
                passed[-1].start()
        for j, chip in enumerate(chips):
            for k in range(n):
                copy(k, 3 + j, chip, 1 - c, (x, y, c)).wait_recv()
        for cp in first_copies(ins, outs, send_sems, recv_sems) + passed:
            cp.wait_send()

    return _Exchange(list(shards), [jax.ShapeDtypeStruct((N_CHIPS,) + s.shape, s.dtype) for s in shards], 6 * n,
                     start, finish)


def _place_own(stacks, shards):
    me = 2 * lax.axis_index("x") + lax.axis_index("y")
    return [lax.dynamic_update_slice(others, mine[None], (me, 0, 0)) for others, mine in zip(stacks, shards)]


def _swap_exchange(grads):
    n = len(grads)

    def copies(ins, outs, send_sems, recv_sems):
        x, y, c = _my_place()
        return [pltpu.make_async_remote_copy(
            src_ref=_core_half(ins[k], 1 - c, lead=(slice(None),)), dst_ref=outs[k],
            send_sem=send_sems.at[k], recv_sem=recv_sems.at[k], device_id=(x, y, 1 - c), device_id_type=MESH)
            for k in range(n)]

    def start(ins, outs, send_sems, recv_sems):
        for cp in copies(ins, outs, send_sems, recv_sems):
            cp.start()

    def finish(ins, outs, send_sems, recv_sems):
        for cp in copies(ins, outs, send_sems, recv_sems):
            cp.wait()

    return _Exchange(list(grads), [jax.ShapeDtypeStruct((N_CHIPS,) + _half_shape(*g.shape[1:]), g.dtype) for g in grads],
                     n, start, finish)


def _chips_exchange(sums):
    n = len(sums)

    def sends(ins, outs, send_sems, recv_sems):
        x, y, c = _my_place()
        return [pltpu.make_async_remote_copy(
            src_ref=ins[k].at[_chip(chip)], dst_ref=outs[k].at[_chip((x, y))],
            send_sem=send_sems.at[3 * k + j], recv_sem=recv_sems.at[3 * k + j],
            device_id=(*chip, c), device_id_type=MESH)
            for j, chip in enumerate(_other_chips(x, y)) for k in range(n)]

    def start(ins, outs, send_sems, recv_sems):
        for cp in sends(ins, outs, send_sems, recv_sems):
            cp.start()

    def finish(ins, outs, send_sems, recv_sems):
        x, y, c = _my_place()
        for j, chip in enumerate(_other_chips(x, y)):
            for k in range(n):
                slot = outs[k].at[_chip(chip)]
                pltpu.make_async_remote_copy(
                    src_ref=slot, dst_ref=slot, send_sem=send_sems.at[3 * k + j], recv_sem=recv_sems.at[3 * k + j],
                    device_id=(x, y, c), device_id_type=MESH).wait_recv()
        for cp in sends(ins, outs, send_sems, recv_sems):
            cp.wait_send()

    return _Exchange(list(sums), [jax.ShapeDtypeStruct(s.shape, s.dtype) for s in sums], 3 * n, start, finish)


def _join_exchange(shards):
    n = len(shards)

    def sends(ins, outs, send_sems, recv_sems):
        x, y, c = _my_place()
        return [pltpu.make_async_remote_copy(
            src_ref=_core_half(ins[k], c), dst_ref=_core_half(outs[k], c),
            send_sem=send_sems.at[k], recv_sem=recv_sems.at[k], device_id=(x, y, 1 - c), device_id_type=MESH)
            for k in range(n)]

    def start(ins, outs, send_sems, recv_sems):
        for cp in sends(ins, outs, send_sems, recv_sems):
            cp.start()

    def finish(ins, outs, send_sems, recv_sems):
        x, y, c = _my_place()
        for k in range(n):
            theirs = _core_half(outs[k], 1 - c)
            pltpu.make_async_remote_copy(
                src_ref=theirs, dst_ref=theirs, send_sem=send_sems.at[k], recv_sem=recv_sems.at[k],
                device_id=(x, y, c), device_id_type=MESH).wait_recv()
        for cp in sends(ins, outs, send_sems, recv_sems):
            cp.wait_send()

    return _Exchange(list(shards), [jax.ShapeDtypeStruct(s.shape, s.dtype) for s in shards], n, start, finish,
                     in_place=True)


def _small_exchange(blocks):
    n = len(blocks)

    def copies(ins, outs, send_sems, recv_sems):
        x, y, c = _my_place()

        def copy(k, j, whose, to, src=None):
            slot = outs[k].at[4 * whose[0] + 2 * whose[1] + whose[2]]
            return pltpu.make_async_remote_copy(
                src_ref=slot if src is None else src, dst_ref=slot,
                send_sem=send_sems.at[7 * k + j], recv_sem=recv_sems.at[7 * k + j], device_id=to, device_id_type=MESH)

        return (x, y, c), copy

    def first_copies(ins, outs, send_sems, recv_sems):
        (x, y, c), copy = copies(ins, outs, send_sems, recv_sems)
        out = []
        for k in range(n):
            out.append(copy(k, 0, (x, y, c), (x, y, 1 - c), src=ins[k]))
            out += [copy(k, 1 + j, (x, y, c), (*chip, c), src=ins[k]) for j, chip in enumerate(_other_chips(x, y))]
        return out

    def start(ins, outs, send_sems, recv_sems):
        for cp in first_copies(ins, outs, send_sems, recv_sems):
            cp.start()

    def finish(ins, outs, send_sems, recv_sems):
        (x, y, c), copy = copies(ins, outs, send_sems, recv_sems)
        chips = _other_chips(x, y)
        passed = []
        for j, chip in enumerate(chips):
            for k in range(n):
                copy(k, 1 + j, (*chip, c), (x, y, c)).wait_recv()
                passed.append(copy(k, 4 + j, (*chip, c), (x, y, 1 - c)))
                passed[-1].start()
        for k in range(n):
            copy(k, 0, (x, y, 1 - c), (x, y, c)).wait_recv()
        for j, chip in enumerate(chips):
            for k in range(n):
                copy(k, 4 + j, (*chip, 1 - c), (x, y, c)).wait_recv()
        for cp in first_copies(ins, outs, send_sems, recv_sems) + passed:
            cp.wait_send()

    return _Exchange(list(blocks), [jax.ShapeDtypeStruct((N_DEV,) + blk.shape, blk.dtype) for blk in blocks], 7 * n,
                     start, finish)


def _sum_halves(grads, theirs, core, name):
    _, h, cols = theirs.shape
    by_rows = _by_rows(grads.shape[1])
    br = _rows_block(h, cols) if by_rows else h
    nb = h // br

    def body(core_ref, a_ref, b_ref, o_ref):
        o_ref[...] = (a_ref[...] + b_ref[...]).astype(BF16)

    if by_rows:
        mine = pl.BlockSpec((1, br, cols), lambda j, i, core_ref: (j, core_ref[0] * nb + i, 0))
    else:
        mine = pl.BlockSpec((1, br, cols), lambda j, i, core_ref: (j, i, core_ref[0]))
    return pl.pallas_call(
        body, name=name,
        out_shape=jax.ShapeDtypeStruct(theirs.shape, BF16),
        grid_spec=pltpu.PrefetchScalarGridSpec(
            num_scalar_prefetch=1, grid=(N_CHIPS, nb),
            in_specs=[mine, pl.BlockSpec((1, br, cols), lambda j, i, core_ref: (j, i, 0))],
            out_specs=pl.BlockSpec((1, br, cols), lambda j, i, core_ref: (j, i, 0))),
        compiler_params=_cparams(("parallel", "parallel")),
    )(core, grads, theirs)


def _sum_chips(slots, sums, place, by_rows, name):
    _, h, cols = slots.shape
    br = _rows_block(h, cols) if by_rows else h
    nb = h // br

    def body(place_ref, s_ref, own_ref, o_ref):
        me = place_ref[1]
        acc = None
        for k in range(N_CHIPS):
            term = jnp.where(me == k, own_ref[k], s_ref[k]).astype(F32)
            acc = term if acc is None else acc + term
        o_ref[...] = acc

    stack = pl.BlockSpec((N_CHIPS, br, cols), lambda i, place_ref: (0, i, 0))
    if by_rows:
        out_shape, out_map = (2 * h, cols), lambda i, place_ref: (place_ref[0] * nb + i, 0)
    else:
        out_shape, out_map = (h, 2 * cols), lambda i, place_ref: (i, place_ref[0])
    return pl.pallas_call(
        body, name=name,
        out_shape=jax.ShapeDtypeStruct(out_shape, F32),
        grid_spec=pltpu.PrefetchScalarGridSpec(
            num_scalar_prefetch=1, grid=(nb,), in_specs=[stack, stack],
            out_specs=pl.BlockSpec((br, cols), out_map)),
        compiler_params=_cparams(("parallel",)),
    )(place, slots, sums)


def _adamw_math(w, g, m, v):
    m = ADAM_B1 * m + (1.0 - ADAM_B1) * g
    v = ADAM_B2 * v + (1.0 - ADAM_B2) * (g * g)
    m_hat = m / (1.0 - ADAM_B1 ** ADAM_STEP)
    v_hat = v / (1.0 - ADAM_B2 ** ADAM_STEP)
    delta = -ADAM_LR * (m_hat / (jnp.sqrt(v_hat) + ADAM_EPS) + ADAM_WD * w)
    return delta, m, v


def _adamw(w, g, m, v, name, ex=None):
    def body(w_ref, g_ref, m_ref, v_ref, d_ref, nm_ref, nv_ref):
        d, nm, nv = _adamw_math(w_ref[...], g_ref[...], m_ref[...], v_ref[...])
        d_ref[...] = d
        nm_ref[...] = nm
        nv_ref[...] = nv

    if w.ndim == 3:
        rows = w.shape[0]
        br = max(b for b in range(1, 65) if rows % b == 0)
        spec, steps = pl.BlockSpec((br,) + w.shape[1:], lambda i: (i, 0, 0)), rows // br
    else:
        rows, cols = w.shape
        br = _rows_block(rows, cols, unit=8)
        spec, steps = pl.BlockSpec((br, cols), lambda i: (i, 0)), rows // br
    shape = jax.ShapeDtypeStruct(w.shape, F32)
    outs, moved = _hosted_call(
        body, ex, name=name, out_shape=(shape, shape, shape), grid=(steps,),
        in_specs=[spec] * 4, out_specs=(spec, spec, spec), args=(w, g, m, v))
    return tuple(outs) if ex is None else (tuple(outs), moved)


def _adamw_small(parts, own, ws, ms, vs, loss_parts, loss_own, device):
    n = len(ws)

    def total(device_ref, parts_ref, own_ref):
        acc = None
        for dev in range(N_DEV):
            term = jnp.where(device_ref[0] == dev, own_ref[...], parts_ref[dev])
            acc = term if acc is None else acc + term
        return acc

    def body(device_ref, *refs):
        ins, outs = refs[:5 * n + 2], refs[5 * n + 2:]
        for k in range(n):
            g = total(device_ref, ins[k], ins[n + k])
            d, nm, nv = _adamw_math(ins[2 * n + k][...], g, ins[3 * n + k][...], ins[4 * n + k][...])
            for o_ref, val in zip(outs[k::n][:4], (g, d, nm, nv)):
                o_ref[...] = val
        outs[4 * n][...] = total(device_ref, ins[5 * n], ins[5 * n + 1])

    args = list(parts) + list(own) + list(ws) + list(ms) + list(vs) + [loss_parts, loss_own]
    whole = lambda a: pl.BlockSpec(a.shape, lambda i, device_ref, nd=a.ndim: (0,) * nd)
    shapes = [jax.ShapeDtypeStruct(w.shape, F32) for w in ws] * 4 + [jax.ShapeDtypeStruct(loss_own.shape, F32)]
    outs = pl.pallas_call(
        body, name="adamw_small", out_shape=shapes,
        grid_spec=pltpu.PrefetchScalarGridSpec(
            num_scalar_prefetch=1, grid=(1,), in_specs=[whole(a) for a in args], out_specs=[whole(a) for a in shapes]),
        compiler_params=_cparams(("arbitrary",)),
    )(device, *args)
    return outs[:n], outs[n:2 * n], outs[2 * n:3 * n], outs[3 * n:4 * n], outs[4 * n]


def _mm(a, b, *, name, ta=False, out_dtype=F32, res=None, bm=1024, bn=1024, bk=4096, b_stack=False, out_stack=False,
        ex=None):
    if ta:
        kdim, m = a.shape
    else:
        m, kdim = a.shape
    if b_stack:
        _, kb, chunk = b.shape
        n = N_CHIPS * chunk
    else:
        kb, n = b.shape
        chunk = n // N_CHIPS if out_stack else n
    assert kdim == kb, (a.shape, b.shape, ta)
    bm = _block(m, bm, LANES if ta else 16)
    bn = _block(chunk, bn, LANES)
    bk = _block(kdim, bk, LANES)
    nk = kdim // bk
    per_chunk = chunk // bn
    dims = (((0 if ta else 1,), (0,)), ((), ()))

    def body(*refs):
        refs = list(refs)
        a_ref, b_ref = refs[:2]
        r_ref = refs[2] if res is not None else None
        o_ref = refs[3] if res is not None else refs[2]
        part = lax.dot_general(a_ref[...].astype(BF16), b_ref[...].astype(BF16), dims, preferred_element_type=F32)

        def finish(r):
            if r_ref is not None:
                r = r + r_ref[...]
            o_ref[...] = r.astype(out_dtype)

        if nk == 1:
            finish(part)
        else:
            acc_ref = refs[-1]
            k = pl.program_id(2)

            @pl.when(k == 0)
            def _():
                acc_ref[...] = part

            @pl.when(k > 0)
            def _():
                acc_ref[...] += part

            @pl.when(k == nk - 1)
            def _():
                finish(acc_ref[...])

    a_spec = pl.BlockSpec((bk, bm), lambda i, j, k: (k, i)) if ta else pl.BlockSpec((bm, bk), lambda i, j, k: (i, k))
    if b_stack:
        b_spec = pl.BlockSpec((None, bk, bn), lambda i, j, k: (j // per_chunk, k, j % per_chunk))
    else:
        b_spec = pl.BlockSpec((bk, bn), lambda i, j, k: (k, j))
    r_spec = pl.BlockSpec((bm, bn), lambda i, j, k: (i, j))
    if out_stack:
        o_spec = pl.BlockSpec((None, bm, bn), lambda i, j, k: (j // per_chunk, i, j % per_chunk))
        o_shape = (N_CHIPS, m, chunk)
    else:
        o_spec, o_shape = r_spec, (m, n)
    in_specs = [a_spec, b_spec] + ([r_spec] if res is not None else [])
    args = (a, b) + ((res,) if res is not None else ())
    (out,), moved = _hosted_call(
        body, ex, name=name, out_shape=(jax.ShapeDtypeStruct(o_shape, out_dtype),),
        grid=(m // bm, n // bn, nk), in_specs=in_specs, out_specs=(o_spec,),
        scratch=[pltpu.VMEM((bm, bn), F32)] if nk > 1 else [], args=args)
    return out if ex is None else (out, moved)


def _rms_fwd(x, g, name, ex=None):
    t, d = x.shape
    bt = _block(t, 512, 16)

    def body(x_ref, g_ref, h_ref):
        xv = x_ref[...]
        r = lax.rsqrt(jnp.mean(xv * xv, axis=-1, keepdims=True) + EPS)
        h_ref[...] = (xv * r * g_ref[...]).astype(BF16)

    (out,), moved = _hosted_call(
        body, ex, name=name, out_shape=(jax.ShapeDtypeStruct((t, d), BF16),), grid=(t // bt,),
        in_specs=[pl.BlockSpec((bt, d), lambda i: (i, 0)), pl.BlockSpec((1, d), lambda i: (0, 0))],
        out_specs=(pl.BlockSpec((bt, d), lambda i: (i, 0)),), args=(x, g))
    return out if ex is None else (out, moved)


def _rms_bwd(dh, x, g, dres, name, ex=None):
    t, d = x.shape
    bt = _block(t, 256, 16)
    want_dx = dres is not None

    def body(*refs):
        if want_dx:
            dh_ref, x_ref, g_ref, dres_ref, dx_ref, dxb_ref, dg_ref = refs
        else:
            dh_ref, x_ref, g_ref, dg_ref = refs
        xv = x_ref[...]
        r = lax.rsqrt(jnp.mean(xv * xv, axis=-1, keepdims=True) + EPS)
        xhat = xv * r
        dhv = dh_ref[...]

        @pl.when(pl.program_id(0) == 0)
        def _():
            dg_ref[...] = jnp.zeros_like(dg_ref)

        dg_ref[...] += jnp.sum(dhv * xhat, axis=0, keepdims=True)
        if want_dx:
            dxhat = dhv * g_ref[...]
            dx = dres_ref[...] + r * (dxhat - xhat * jnp.mean(dxhat * xhat, axis=-1, keepdims=True))
            dx_ref[...] = dx
            dxb_ref[...] = dx.astype(BF16)

    row = pl.BlockSpec((bt, d), lambda i: (i, 0))
    vec = pl.BlockSpec((1, d), lambda i: (0, 0))
    if want_dx:
        outs, moved = _hosted_call(
            body, ex, name=name, grid=(t // bt,),
            out_shape=(jax.ShapeDtypeStruct((t, d), F32), jax.ShapeDtypeStruct((t, d), BF16),
                       jax.ShapeDtypeStruct((1, d), F32)),
            in_specs=[row, row, vec, row], out_specs=(row, row, vec), args=(dh, x, g, dres))
        return tuple(outs) if ex is None else (tuple(outs), moved)
    return pl.pallas_call(
        body, name=name, grid=(t // bt,), out_shape=jax.ShapeDtypeStruct((1, d), F32),
        in_specs=[row, row, vec], out_specs=vec,
        compiler_params=_cparams(("arbitrary",)),
    )(dh, x, g)


def _mm_res_norm(a, b, res, g, name):
    t, k = a.shape
    d = b.shape[1]
    bm = _block(t, 512, 16)

    def body(a_ref, b_ref, r_ref, g_ref, x_ref, h_ref):
        xv = jnp.dot(a_ref[...], b_ref[...], preferred_element_type=F32) + r_ref[...]
        x_ref[...] = xv
        r = lax.rsqrt(jnp.mean(xv * xv, axis=-1, keepdims=True) + EPS)
        h_ref[...] = (xv * r * g_ref[...]).astype(BF16)

    row = pl.BlockSpec((bm, d), lambda i: (i, 0))
    return pl.pallas_call(
        body, name=name, grid=(t // bm,),
        out_shape=(jax.ShapeDtypeStruct((t, d), F32), jax.ShapeDtypeStruct((t, d), BF16)),
        in_specs=[pl.BlockSpec((bm, k), lambda i: (i, 0)), pl.BlockSpec((k, d), lambda i: (0, 0)), row,
                  pl.BlockSpec((1, d), lambda i: (0, 0))],
        out_specs=(row, row), compiler_params=_cparams(("parallel",)),
    )(a, b, res, g)


def _ffn_out_loss(act, w, res, target, g):
    t, k = act.shape
    d = w.shape[1]
    bm = _block(t, 512, 16)

    def body(a_ref, w_ref, r_ref, t_ref, g_ref, dx_ref, dxb_ref, dg_ref, loss_ref):
        xv = jnp.dot(a_ref[...], w_ref[...], preferred_element_type=F32) + r_ref[...]
        gv = g_ref[...]
        r = lax.rsqrt(jnp.mean(xv * xv, axis=-1, keepdims=True) + EPS)
        xhat = xv * r
        err = xhat * gv - t_ref[...]

        @pl.when(pl.program_id(0) == 0)
        def _():
            dg_ref[...] = jnp.zeros_like(dg_ref)
            loss_ref[...] = jnp.zeros_like(loss_ref)

        loss_ref[...] += 0.5 * jnp.sum(jnp.mean(err * err, axis=-1, keepdims=True), axis=0, keepdims=True)
        dy = err * (1.0 / d)
        dg_ref[...] += jnp.sum(dy * xhat, axis=0, keepdims=True)
        dxhat = dy * gv
        dx = r * (dxhat - xhat * jnp.mean(dxhat * xhat, axis=-1, keepdims=True))
        dx_ref[...] = dx
        dxb_ref[...] = dx.astype(BF16)

    row = pl.BlockSpec((bm, d), lambda i: (i, 0))
    vec = pl.BlockSpec((1, d), lambda i: (0, 0))
    return pl.pallas_call(
        body, name="ffn_out_loss", grid=(t // bm,),
        out_shape=(jax.ShapeDtypeStruct((t, d), F32), jax.ShapeDtypeStruct((t, d), BF16),
                   jax.ShapeDtypeStruct((1, d), F32), jax.ShapeDtypeStruct((1, LANES), F32)),
        in_specs=[pl.BlockSpec((bm, k), lambda i: (i, 0)), pl.BlockSpec((k, d), lambda i: (0, 0)), row, row, vec],
        out_specs=(row, row, vec, pl.BlockSpec((1, LANES), lambda i: (0, 0))),
        compiler_params=_cparams(("arbitrary",)),
    )(act, w, res, target, g)


def _mm_norm_bwd(a, b, x, g, dres, name, ex=None):
    t, k = a.shape
    d = b.shape[1]
    bm = _block(t, 512 if k <= 2048 else 256, 16)

    def body(a_ref, b_ref, x_ref, g_ref, dres_ref, dx_ref, dxb_ref, dg_ref):
        @pl.when(pl.program_id(0) == 0)
        def _():
            dg_ref[...] = jnp.zeros_like(dg_ref)

        dhv = jnp.dot(a_ref[...], b_ref[...], preferred_element_type=F32)
        xv = x_ref[...]
        r = lax.rsqrt(jnp.mean(xv * xv, axis=-1, keepdims=True) + EPS)
        xhat = xv * r
        dg_ref[...] += jnp.sum(dhv * xhat, axis=0, keepdims=True)
        dxhat = dhv * g_ref[...]
        dx = dres_ref[...] + r * (dxhat - xhat * jnp.mean(dxhat * xhat, axis=-1, keepdims=True))
        dx_ref[...] = dx
        dxb_ref[...] = dx.astype(BF16)

    row = pl.BlockSpec((bm, d), lambda i: (i, 0))
    vec = pl.BlockSpec((1, d), lambda i: (0, 0))
    outs, moved = _hosted_call(
        body, ex, name=name, grid=(t // bm,),
        out_shape=(jax.ShapeDtypeStruct((t, d), F32), jax.ShapeDtypeStruct((t, d), BF16), jax.ShapeDtypeStruct((1, d), F32)),
        in_specs=[pl.BlockSpec((bm, k), lambda i: (i, 0)), pl.BlockSpec((k, d), lambda i: (0, 0)), row, vec, row],
        out_specs=(row, row, vec), args=(a, b, x, g, dres))
    return tuple(outs) if ex is None else (tuple(outs), moved)


GU_COLS = GATE_WIDTH + POOL_WIDTH
U_BLK = GATE_WIDTH // POOL_WIDTH


def _shift_down(a, k, row):
    return jnp.where(row >= k, pltpu.roll(a, k, 0), 0.0)


def _shift_up(a, k, row):
    n = a.shape[0]
    return jnp.where(row < n - k, pltpu.roll(a, n - k, 0), 0.0)


def _window_delta(u, w, row):
    s, k = u, 1
    while k < w:
        s = s + _shift_down(s, k, row)
        k *= 2
    cnt = jnp.minimum(row + 1, w).astype(F32)
    return s / cnt - u, cnt


def _pool_fwd(gu, pool_w, pool_scale):
    b, s, _ = gu.shape

    def body(u_ref, pw_ref, sc_ref, y_ref):
        row = lax.broadcasted_iota(jnp.int32, (s, POOL_GC), 0)
        for g, w in enumerate(POOL_WINDOWS):
            cols = slice(g * POOL_GC, (g + 1) * POOL_GC)
            d, _ = _window_delta(u_ref[0, :, cols].astype(F32), w, row)
            z = jnp.dot(d.astype(BF16), pw_ref[g].astype(BF16), preferred_element_type=F32)
            y_ref[0, :, cols] = (z * sc_ref[:, cols]).astype(BF16)

    return pl.pallas_call(
        body, name="pool_fwd", out_shape=jax.ShapeDtypeStruct((b, s, POOL_WIDTH), BF16), grid=(b,),
        in_specs=[pl.BlockSpec((1, s, POOL_WIDTH), lambda i: (i, 0, U_BLK)),
                  pl.BlockSpec((4, POOL_GC, POOL_GC), lambda i: (0, 0, 0)),
                  pl.BlockSpec((1, POOL_WIDTH), lambda i: (0, 0))],
        out_specs=pl.BlockSpec((1, s, POOL_WIDTH), lambda i: (i, 0, 0)),
        compiler_params=_cparams(("parallel",)),
    )(gu, pool_w, pool_scale)


def _pool_bwd(gu, dy, pool_w, pool_scale, dgu):
    b, s, _ = gu.shape

    def body(u_ref, dy_ref, pw_ref, sc_ref, dgu_in, du_ref, dpw_ref, dsc_ref):
        del dgu_in

        @pl.when(pl.program_id(0) == 0)
        def _():
            dpw_ref[...] = jnp.zeros_like(dpw_ref)
            dsc_ref[...] = jnp.zeros_like(dsc_ref)

        row = lax.broadcasted_iota(jnp.int32, (s, POOL_GC), 0)
        for g, w in enumerate(POOL_WINDOWS):
            cols = slice(g * POOL_GC, (g + 1) * POOL_GC)
            d, cnt = _window_delta(u_ref[0, :, cols].astype(F32), w, row)
            db = d.astype(BF16)
            pw = pw_ref[g].astype(BF16)
            z = jnp.dot(db, pw, preferred_element_type=F32)
            dyv = dy_ref[0, :, cols]
            dsc_ref[:, cols] += jnp.sum(dyv * z, axis=0, keepdims=True)
            dz = (dyv * sc_ref[:, cols]).astype(BF16)
            dpw_ref[g] += lax.dot_general(db, dz, (((0,), (0,)), ((), ())), preferred_element_type=F32)
            dd = lax.dot_general(dz, pw, (((1,), (1,)), ((), ())), preferred_element_type=F32)
            acc, k = dd / cnt, 1
            while k < w:
                acc = acc + _shift_up(acc, k, row)
                k *= 2
            du_ref[0, :, cols] = (acc - dd).astype(BF16)

    return pl.pallas_call(
        body, name="pool_bwd", grid=(b,),
        out_shape=(jax.ShapeDtypeStruct((b, s, GU_COLS), BF16), jax.ShapeDtypeStruct((4, POOL_GC, POOL_GC), F32),
                   jax.ShapeDtypeStruct((1, POOL_WIDTH), F32)),
        in_specs=[pl.BlockSpec((1, s, POOL_WIDTH), lambda i: (i, 0, U_BLK)),
                  pl.BlockSpec((1, s, POOL_WIDTH), lambda i: (i, 0, 0)),
                  pl.BlockSpec((4, POOL_GC, POOL_GC), lambda i: (0, 0, 0)),
                  pl.BlockSpec((1, POOL_WIDTH), lambda i: (0, 0)), ANY],
        out_specs=(pl.BlockSpec((1, s, POOL_WIDTH), lambda i: (i, 0, U_BLK)),
                   pl.BlockSpec((4, POOL_GC, POOL_GC), lambda i: (0, 0, 0)),
                   pl.BlockSpec((1, POOL_WIDTH), lambda i: (0, 0))),
        input_output_aliases={4: 0},
        compiler_params=_cparams(("arbitrary",)),
    )(gu, dy, pool_w, pool_scale, dgu)


def _forget_cumsum(f, bias, name):
    b, s, c = f.shape

    def body(f_ref, b_ref, c_ref):
        row = lax.broadcasted_iota(jnp.int32, (s, LANES), 0)
        z = f_ref[0] + b_ref[...]
        acc = jnp.minimum(z, 0.0) - jnp.log(1.0 + jnp.exp(-jnp.abs(z)))
        k = 1
        while k < s:
            acc = acc + _shift_down(acc, k, row)
            k *= 2
        c_ref[0] = acc

    return pl.pallas_call(
        body, name=name, out_shape=jax.ShapeDtypeStruct((b, s, c), F32), grid=(b, c // LANES),
        in_specs=[pl.BlockSpec((1, s, LANES), lambda i, j: (i, 0, j)), pl.BlockSpec((1, LANES), lambda i, j: (0, j))],
        out_specs=pl.BlockSpec((1, s, LANES), lambda i, j: (i, 0, j)),
        compiler_params=_cparams(("parallel", "parallel")),
    )(f, bias)


def _forget_bwd(dc, f, bias):
    b, s, _ = f.shape

    def body(dc_ref, f_ref, b_ref, df_ref, db_ref):
        @pl.when(pl.program_id(0) == 0)
        def _():
            db_ref[...] = jnp.zeros_like(db_ref)

        row = lax.broadcasted_iota(jnp.int32, (s, LANES), 0)
        acc, k = dc_ref[0], 1
        while k < s:
            acc = acc + _shift_up(acc, k, row)
            k *= 2
        z = f_ref[0] + b_ref[...]
        df = acc / (1.0 + jnp.exp(z))
        db_ref[...] += jnp.sum(df, axis=0, keepdims=True)
        df_ref[0] = df.astype(BF16)

    blk = pl.BlockSpec((1, s, LANES), lambda i: (i, 0, 0))
    vec = pl.BlockSpec((1, LANES), lambda i: (0, 0))
    return pl.pallas_call(
        body, name="forget_bwd", grid=(b,),
        out_shape=(jax.ShapeDtypeStruct((b, s, LANES), BF16), jax.ShapeDtypeStruct((1, LANES), F32)),
        in_specs=[blk, blk, vec], out_specs=(blk, vec),
        compiler_params=_cparams(("arbitrary",)),
    )(dc, f, bias)


KV_BLK0 = 2
PAIRS = FOX_HEADS // 2
FOX_SCALE = FOX_DH ** -0.5
NT_DIMS = (((1,), (1,)), ((), ()))
TN_DIMS = (((0,), (0,)), ((), ()))


def _stack_heads(v):
    head = lax.broadcasted_iota(jnp.int32, v.shape, 1) // FOX_DH
    zero = jnp.zeros_like(v)
    return jnp.concatenate([jnp.where(head == 0, v, zero), jnp.where(head == 1, v, zero)], axis=0)


def _stack_cols(v):
    return jnp.concatenate([v[:, 0:1], v[:, FOX_DH:FOX_DH + 1]], axis=0)


def _unstack(t, blk):
    head = lax.broadcasted_iota(jnp.int32, (blk, LANES), 1) // FOX_DH
    return jnp.where(head == 0, t[:blk], t[blk:])


def _fox_scores(q_all, kblk, row_bias, cr_ref, kb, masked, blk):
    top = lax.broadcasted_iota(jnp.int32, (2 * blk, 1), 0) < blk
    s = lax.dot_general(q_all, kblk, NT_DIMS, preferred_element_type=F32)
    s = s + (row_bias - jnp.where(top, cr_ref[0, 0, kb], cr_ref[0, 1, kb]))
    if masked:
        r = lax.broadcasted_iota(jnp.int32, (2 * blk, blk), 0)
        keep = jnp.where(r >= blk, r - blk, r) >= lax.broadcasted_iota(jnp.int32, (2 * blk, blk), 1)
        s = jnp.where(keep, s, NEG_INF)
    return s


def _fox_fwd(qkv, c_exp, c_row, ex=None):
    b, s, _ = qkv.shape
    blk = min(ATT_BLOCK, s)
    nq = s // blk

    def body(q_ref, kv_ref, cc_ref, cr_ref, o_ref, ob_ref, lse_ref):
        qi = pl.program_id(2)
        q_all = _stack_heads(q_ref[0] * FOX_SCALE)
        cq = _stack_cols(cc_ref[0])

        def step(kb, carry, masked):
            m, l, acc = carry
            rows = pl.ds(pl.multiple_of(kb * blk, blk), blk)
            sc = _fox_scores(q_all, kv_ref[0, rows, :LANES], cq, cr_ref, kb, masked, blk)
            m_new = jnp.maximum(m, jnp.max(sc, axis=-1, keepdims=True))
            p = jnp.exp(sc - m_new)
            alpha = jnp.exp(m - m_new)
            l = alpha * l + jnp.sum(p, axis=-1, keepdims=True)
            acc = alpha * acc + jnp.dot(p.astype(BF16), kv_ref[0, rows, LANES:], preferred_element_type=F32)
            return m_new, l, acc

        init = (jnp.full((2 * blk, 1), NEG_INF, F32), jnp.zeros((2 * blk, 1), F32), jnp.zeros((2 * blk, LANES), F32))
        m, l, acc = step(qi, lax.fori_loop(0, qi, functools.partial(step, masked=False), init), True)
        o = _unstack(acc / l, blk)
        o_ref[0] = o
        ob_ref[0] = o.astype(BF16)
        lse_ref[0] = _unstack(jnp.broadcast_to(m + jnp.log(l), (2 * blk, LANES)), blk)

    tile = pl.BlockSpec((1, blk, LANES), lambda i, h, q: (i, q, h))
    kvspec = pl.BlockSpec((1, s, 2 * LANES), lambda i, h, q: (i, 0, KV_BLK0 + h))
    shape = jax.ShapeDtypeStruct((b, s, FOX_WIDTH), F32)
    return _hosted_call(
        body, ex, name="fox_fwd", out_shape=(shape, jax.ShapeDtypeStruct((b, s, FOX_WIDTH), BF16), shape),
        grid=(b, PAIRS, nq),
        in_specs=[tile, kvspec, tile, pl.BlockSpec((1, 2, nq, 1, blk), lambda i, h, q: (i, h, 0, 0, 0))],
        out_specs=(tile, tile, tile), args=(qkv, qkv, c_exp, c_row))


def _fox_bwd(qkv, c_exp, c_row, lse, o, do, ex=None):
    b, s, _ = qkv.shape
    blk = min(ATT_BLOCK, s)
    nq = s // blk

    def body(q_ref, kv_ref, cc_ref, cr_ref, lse_ref, o_ref, do_ref, dq_ref, dkv_ref, dcq_ref, dc_ref, dk_acc, dv_acc):
        qi = pl.program_id(2)

        @pl.when(qi == 0)
        def _():
            dk_acc[...] = jnp.zeros_like(dk_acc)
            dv_acc[...] = jnp.zeros_like(dv_acc)
            dc_ref[...] = jnp.zeros_like(dc_ref)

        q_all = _stack_heads(q_ref[0] * FOX_SCALE)
        dov = do_ref[0]
        do_all = _stack_heads(dov.astype(BF16))
        delta = jnp.sum(_stack_heads(dov * o_ref[0]), axis=-1, keepdims=True)
        bias = _stack_cols(cc_ref[0]) - _stack_cols(lse_ref[0])

        def step(kb, carry, masked):
            acc, dcq = carry
            rows = pl.ds(pl.multiple_of(kb * blk, blk), blk)
            kblk = kv_ref[0, rows, :LANES]
            p = jnp.exp(_fox_scores(q_all, kblk, bias, cr_ref, kb, masked, blk))
            dp = lax.dot_general(do_all, kv_ref[0, rows, LANES:], NT_DIMS, preferred_element_type=F32)
            ds = p * (dp - delta)
            dsb = ds.astype(BF16)
            dv_acc[rows, :] += lax.dot_general(p.astype(BF16), do_all, TN_DIMS, preferred_element_type=F32)
            dk_acc[rows, :] += lax.dot_general(dsb, q_all, TN_DIMS, preferred_element_type=F32)
            dc_ref[0, 0, kb] -= jnp.sum(ds[:blk], axis=0, keepdims=True)
            dc_ref[0, 1, kb] -= jnp.sum(ds[blk:], axis=0, keepdims=True)
            acc = acc + jnp.dot(dsb, kblk, preferred_element_type=F32)
            return acc, dcq + jnp.sum(ds, axis=-1, keepdims=True)

        init = (jnp.zeros((2 * blk, LANES), F32), jnp.zeros((2 * blk, 1), F32))
        acc, dcq = step(qi, lax.fori_loop(0, qi, functools.partial(step, masked=False), init), True)
        dq_ref[0] = (_unstack(acc, blk) * FOX_SCALE).astype(BF16)
        dcq_ref[0, 0] = jnp.where(lax.broadcasted_iota(jnp.int32, (blk, 2), 1) == 0, dcq[:blk], dcq[blk:])

        @pl.when(qi == nq - 1)
        def _():
            dkv_ref[0, :, :LANES] = dk_acc[...].astype(BF16)
            dkv_ref[0, :, LANES:] = dv_acc[...].astype(BF16)

    tile = pl.BlockSpec((1, blk, LANES), lambda i, h, q: (i, q, h))
    kvspec = pl.BlockSpec((1, s, 2 * LANES), lambda i, h, q: (i, 0, KV_BLK0 + h))
    crow = pl.BlockSpec((1, 2, nq, 1, blk), lambda i, h, q: (i, h, 0, 0, 0))
    return _hosted_call(
        body, ex, name="fox_bwd", grid=(b, PAIRS, nq),
        out_shape=(jax.ShapeDtypeStruct((b, s, FOX_WIDTH), BF16), jax.ShapeDtypeStruct((b, s, 2 * FOX_WIDTH), BF16),
                   jax.ShapeDtypeStruct((b, PAIRS, s, 2), F32), jax.ShapeDtypeStruct(c_row.shape, F32)),
        in_specs=[tile, kvspec, tile, crow, tile, tile, tile],
        out_specs=(tile, pl.BlockSpec((1, s, 2 * LANES), lambda i, h, q: (i, 0, h)),
                   pl.BlockSpec((1, 1, blk, 2), lambda i, h, q: (i, h, q, 0)), crow),
        scratch=[pltpu.VMEM((s, LANES), F32), pltpu.VMEM((s, LANES), F32)],
        args=(qkv, qkv, c_exp, c_row, lse, o, do))


def _sigmoid(z):
    return 1.0 / (1.0 + jnp.exp(-z))


def _branches_mix(y, o, w_pool3, w_fox3, gu, b_gate):
    t = y.shape[0]
    chunk = w_pool3.shape[2]
    per_branch = D_MODEL // chunk
    bm = _block(t, 1024, 16)

    def body(y_ref, o_ref, wp_ref, wf_ref, gp_ref, gf_ref, bp_ref, bf_ref, yp_ref, yf_ref, mix_ref):
        yp = jnp.dot(y_ref[...], wp_ref[...], preferred_element_type=F32).astype(BF16)
        yf = jnp.dot(o_ref[...], wf_ref[...], preferred_element_type=F32).astype(BF16)
        yp_ref[...] = yp
        yf_ref[...] = yf
        gp = _sigmoid(gp_ref[...].astype(F32) + bp_ref[...])
        gf = _sigmoid(gf_ref[...].astype(F32) + bf_ref[...])
        mix_ref[...] = (gp * yp.astype(F32) + gf * yf.astype(F32)).astype(BF16)

    rows = pl.BlockSpec((bm, y.shape[1]), lambda i, j: (i, 0))
    weight = pl.BlockSpec((None, y.shape[1], chunk), lambda i, j: (j, 0, 0))
    tile = lambda base: pl.BlockSpec((bm, chunk), lambda i, j: (i, base + j))
    vec = lambda base: pl.BlockSpec((1, chunk), lambda i, j: (0, base + j))
    shape = jax.ShapeDtypeStruct((t, D_MODEL), BF16)
    return pl.pallas_call(
        body, name="branches_mix", out_shape=(shape, shape, shape), grid=(t // bm, per_branch),
        in_specs=[rows, rows, weight, weight, tile(0), tile(per_branch), vec(0), vec(per_branch)],
        out_specs=(tile(0), tile(0), tile(0)),
        compiler_params=_cparams(("parallel", "arbitrary")),
    )(y, o, w_pool3, w_fox3, gu, gu, b_gate, b_gate)


def _mix_bwd(gu, b_gate, y_pool, y_fox, dx, w_out_t):
    t = gu.shape[0]
    bt = _block(t, 256, 16)

    def body(gp_ref, gf_ref, bp_ref, bf_ref, yp_ref, yf_ref, dx_ref, w_ref, dyp_ref, dyf_ref, dgl_ref, db_ref):
        @pl.when(pl.program_id(0) == 0)
        def _():
            db_ref[...] = jnp.zeros_like(db_ref)

        dm = jnp.dot(dx_ref[...], w_ref[...], preferred_element_type=F32)
        gp = _sigmoid(gp_ref[...].astype(F32) + bp_ref[...])
        gf = _sigmoid(gf_ref[...].astype(F32) + bf_ref[...])
        dyp_ref[...] = (dm * gp).astype(BF16)
        dyf_ref[...] = (dm * gf).astype(BF16)
        dlp = dm * yp_ref[...].astype(F32) * gp * (1.0 - gp)
        dlf = dm * yf_ref[...].astype(F32) * gf * (1.0 - gf)
        dgl_ref[:, :D_MODEL] = dlp.astype(BF16)
        dgl_ref[:, D_MODEL:] = dlf.astype(BF16)
        db_ref[:, :D_MODEL] += jnp.sum(dlp, axis=0, keepdims=True)
        db_ref[:, D_MODEL:] += jnp.sum(dlf, axis=0, keepdims=True)

    col = lambda j: pl.BlockSpec((bt, D_MODEL), lambda i: (i, j))
    vec = lambda j: pl.BlockSpec((1, D_MODEL), lambda i: (0, j))
    wide = pl.BlockSpec((bt, GATE_WIDTH), lambda i: (i, 0))
    return pl.pallas_call(
        body, name="mix_bwd", grid=(t // bt,),
        out_shape=(jax.ShapeDtypeStruct((t, D_MODEL), BF16), jax.ShapeDtypeStruct((t, D_MODEL), BF16),
                   jax.ShapeDtypeStruct((t, GU_COLS), BF16), jax.ShapeDtypeStruct((1, GATE_WIDTH), F32)),
        in_specs=[col(0), col(1), vec(0), vec(1), col(0), col(0), col(0),
                  pl.BlockSpec(w_out_t.shape, lambda i: (0, 0))],
        out_specs=(col(0), col(0), wide, pl.BlockSpec((1, GATE_WIDTH), lambda i: (0, 0))),
        compiler_params=_cparams(("arbitrary",)),
    )(gu, gu, b_gate, b_gate, y_pool, y_fox, dx, w_out_t)


X_SCALE = X_DH ** -0.5


def _xattn_probs(qh, kh):
    s = lax.dot_general(qh, kh, NT_DIMS, preferred_element_type=F32) * X_SCALE
    e = jnp.exp(s - jnp.max(s, axis=-1, keepdims=True))
    return e / jnp.sum(e, axis=-1, keepdims=True)


def _xattn_fwd(q, kv):
    b, s, _ = q.shape
    m = kv.shape[1]
    bq = _block(s, 512, 16)

    def body(q_ref, kv_ref, o_ref):
        for h in range(X_HEADS):
            cols = slice(h * X_DH, (h + 1) * X_DH)
            p = _xattn_probs(q_ref[0, :, cols], kv_ref[0, :, cols])
            vh = kv_ref[0, :, X_WIDTH + h * X_DH:X_WIDTH + (h + 1) * X_DH]
            o_ref[0, :, cols] = jnp.dot(p.astype(BF16), vh, preferred_element_type=F32).astype(BF16)

    return pl.pallas_call(
        body, name="xattn_fwd", out_shape=jax.ShapeDtypeStruct((b, s, X_WIDTH), BF16), grid=(b, s // bq),
        in_specs=[pl.BlockSpec((1, bq, X_WIDTH), lambda i, j: (i, j, 0)),
                  pl.BlockSpec((1, m, 2 * X_WIDTH), lambda i, j: (i, 0, 0))],
        out_specs=pl.BlockSpec((1, bq, X_WIDTH), lambda i, j: (i, j, 0)),
        compiler_params=_cparams(("parallel", "parallel")),
    )(q, kv)


def _xattn_bwd(q, kv, do):
    b, s, _ = q.shape
    m = kv.shape[1]
    bq = _block(s, 512, 16)

    def body(q_ref, kv_ref, do_ref, dq_ref, dkv_ref):
        @pl.when(pl.program_id(1) == 0)
        def _():
            dkv_ref[...] = jnp.zeros_like(dkv_ref)

        for h in range(X_HEADS):
            cols = slice(h * X_DH, (h + 1) * X_DH)
            vcols = slice(X_WIDTH + h * X_DH, X_WIDTH + (h + 1) * X_DH)
            qh, kh, vh, doh = q_ref[0, :, cols], kv_ref[0, :, cols], kv_ref[0, :, vcols], do_ref[0, :, cols]
            p = _xattn_probs(qh, kh)
            dkv_ref[0, :, vcols] += lax.dot_general(p.astype(BF16), doh, TN_DIMS, preferred_element_type=F32)
            dp = lax.dot_general(doh, vh, NT_DIMS, preferred_element_type=F32)
            ds = (p * (dp - jnp.sum(p * dp, axis=-1, keepdims=True)) * X_SCALE).astype(BF16)
            dq_ref[0, :, cols] = jnp.dot(ds, kh, preferred_element_type=F32).astype(BF16)
            dkv_ref[0, :, cols] += lax.dot_general(ds, qh, TN_DIMS, preferred_element_type=F32)

    tile = pl.BlockSpec((1, bq, X_WIDTH), lambda i, j: (i, j, 0))
    mem = pl.BlockSpec((1, m, 2 * X_WIDTH), lambda i, j: (i, 0, 0))
    return pl.pallas_call(
        body, name="xattn_bwd", grid=(b, s // bq),
        out_shape=(jax.ShapeDtypeStruct((b, s, X_WIDTH), BF16), jax.ShapeDtypeStruct((b, m, 2 * X_WIDTH), F32)),
        in_specs=[tile, mem, tile], out_specs=(tile, mem),
        compiler_params=_cparams(("parallel", "arbitrary")),
    )(q, kv, do)


def _ffn_in(hf, w3):
    t, d = hf.shape
    chunk = w3.shape[2]
    half = N_CHIPS // 2
    bm = _block(t, 1024, 16)

    def body(a_ref, wg_ref, wu_ref, gt_ref, up_ref, act_ref):
        a = a_ref[...]
        gt = jnp.dot(a, wg_ref[...], preferred_element_type=F32).astype(BF16)
        up = jnp.dot(a, wu_ref[...], preferred_element_type=F32).astype(BF16)
        gt_ref[...] = gt
        up_ref[...] = up
        g32 = gt.astype(F32)
        act_ref[...] = (g32 * _sigmoid(g32) * up.astype(F32)).astype(BF16)

    tile = pl.BlockSpec((bm, chunk), lambda i, j: (i, j))
    shape = jax.ShapeDtypeStruct((t, half * chunk), BF16)
    return pl.pallas_call(
        body, name="ffn_in", out_shape=(shape, shape, shape), grid=(t // bm, half),
        in_specs=[pl.BlockSpec((bm, d), lambda i, j: (i, 0)),
                  pl.BlockSpec((None, d, chunk), lambda i, j: (j, 0, 0)),
                  pl.BlockSpec((None, d, chunk), lambda i, j: (j + half, 0, 0))],
        out_specs=(tile, tile, tile),
        compiler_params=_cparams(("parallel", "arbitrary")),
    )(hf, w3, w3)


def _ffn_act_bwd(dx, w_out_t, gate, up):
    t, d = dx.shape
    bt = _block(t, 256, 16)

    def body(dx_ref, w_ref, gt_ref, up_ref, o_ref):
        da = jnp.dot(dx_ref[...], w_ref[...], preferred_element_type=F32).astype(BF16).astype(F32)
        gt = gt_ref[...].astype(F32)
        sg = _sigmoid(gt)
        silu = gt * sg
        o_ref[:, :D_FF] = (da * up_ref[...].astype(F32) * (sg + silu * (1.0 - sg))).astype(BF16)
        o_ref[:, D_FF:] = (da * silu).astype(BF16)

    col = pl.BlockSpec((bt, D_FF), lambda i: (i, 0))
    return pl.pallas_call(
        body, name="ffn_act_bwd", out_shape=jax.ShapeDtypeStruct((t, 2 * D_FF), BF16), grid=(t // bt,),
        in_specs=[pl.BlockSpec((bt, d), lambda i: (i, 0)), pl.BlockSpec((d, D_FF), lambda i: (0, 0)), col, col],
        out_specs=pl.BlockSpec((bt, 2 * D_FF), lambda i: (i, 0)),
        compiler_params=_cparams(("parallel",)),
    )(dx, w_out_t, gate, up)


def _stack_of(w, axis):
    r, c = w.shape
    if axis == 0:
        return w.reshape(N_CHIPS, r // N_CHIPS, c)
    return w.reshape(r, N_CHIPS, c // N_CHIPS).transpose(1, 0, 2)


def _stack_t(w3):
    n, r, c = w3.shape
    return w3.transpose(0, 2, 1).reshape(n * c, r)


def _pair_rows(k, v):
    c = k.shape[1]
    return jnp.stack([k.reshape(PAIRS, LANES, c), v.reshape(PAIRS, LANES, c)], axis=1).reshape(2 * FOX_WIDTH, c)


def _unpair_rows(kv):
    c = kv.shape[1]
    kv = kv.reshape(PAIRS, 2, LANES, c)
    return kv[:, 0].reshape(FOX_WIDTH, c), kv[:, 1].reshape(FOX_WIDTH, c)


def _input_grad(parts, weights_t, ex):
    t = parts[0].shape[0]
    d = weights_t[0].shape[1]
    bm = _block(t, 512, 16)
    n = len(parts)

    def body(*refs):
        acc = None
        for a_ref, b_ref in zip(refs[:n], refs[n:2 * n]):
            term = jnp.dot(a_ref[...], b_ref[...], preferred_element_type=F32)
            acc = term if acc is None else acc + term
        refs[2 * n][...] = acc

    (out,), moved = _hosted_call(
        body, ex, name="d_h", grid=(t // bm,), out_shape=(jax.ShapeDtypeStruct((t, d), F32),),
        in_specs=[pl.BlockSpec((bm, p.shape[1]), lambda i: (i, 0)) for p in parts]
        + [pl.BlockSpec(w.shape, lambda i: (0, 0)) for w in weights_t],
        out_specs=(pl.BlockSpec((bm, d), lambda i: (i, 0)),), args=tuple(parts) + tuple(weights_t))
    return out, moved


def _step(x, mem, loss_target, weights, moments_m, moments_v):
    nb, s, d = x.shape
    n_mem = mem.shape[1]
    t = nb * s
    blk = min(ATT_BLOCK, s)
    x2 = x.reshape(t, d)
    mem2 = mem.reshape(nb * n_mem, d)
    tgt2 = loss_target.reshape(t, d)

    def shard2d(a, n):
        a = a.reshape(a.shape[1:])
        return a.T if n == "w_in" else a

    def unshard(a, n):
        return (a.T if n == "w_in" else a)[None]

    local = {n: shard2d(weights[n], n) for n, _, _ in SHARDED}

    names = [n for n, _, _ in SHARDED]
    last = ["w_ffn_out"]
    later = [n for n in names if n != "w_in" and n not in last]
    local_b = {n: local[n].astype(BF16) for n in names}
    g_mix = weights["norm_mix_g"]
    h, w_in_others = _rms_fwd(x2, g_mix, "norm_mix", ex=_gather_exchange([local_b["w_in"]]))
    w_in_stack, = _place_own(w_in_others, [local_b["w_in"]])

    def w_in_rows(lo, hi):
        per = IN_COLS // N_CHIPS
        parts = [w_in_stack[j, max(lo, j * per) - j * per:min(hi, (j + 1) * per) - j * per]
                 for j in range(N_CHIPS) if max(lo, j * per) < min(hi, (j + 1) * per)]
        return parts[0] if len(parts) == 1 else jnp.concatenate(parts)

    w_gu_t = jnp.concatenate([w_in_rows(2056, IN_COLS), w_in_rows(0, 512)])
    w_qkv_t = jnp.concatenate([w_in_rows(512, 1024), _pair_rows(w_in_rows(1024, 1536), w_in_rows(1536, 2048))])
    w_f_t = jnp.pad(w_in_rows(2048, 2056), ((0, LANES - FOX_HEADS), (0, 0)))
    w_gu, w_qkv, w_f = w_gu_t.T, w_qkv_t.T, w_f_t.T
    w_f_exp = jnp.repeat(w_f[:, :FOX_HEADS], FOX_DH, axis=1)

    g_mix, g_x, g_mem, g_ffn = (weights[n] for n in ("norm_mix_g", "norm_x_g", "norm_mem_g", "norm_ffn_g"))
    g_final = weights["norm_final_g"].reshape(1, d)
    pool_w = weights["pool_w"].reshape(4, POOL_GC, POOL_GC)
    pool_scale, b_gate = weights["pool_scale"], weights["b_gate"]
    b_f_pad = jnp.pad(weights["b_forget"], ((0, 0), (0, LANES - FOX_HEADS)))
    b_f_exp = jnp.repeat(weights["b_forget"], FOX_DH, axis=1)

    gu, last_others = _mm(h, w_gu, out_dtype=BF16, bn=512, name="in_proj_gates_pool",
                          ex=_gather_exchange([local_b[n] for n in last]))
    qkv = _mm(h, w_qkv, out_dtype=BF16, bn=512, name="in_proj_qkv")
    f_pad = _mm(h, w_f, name="in_proj_forget")
    gu3, qkv3 = gu.reshape(nb, s, GU_COLS), qkv.reshape(nb, s, 3 * FOX_WIDTH)
    y = _pool_fwd(gu3, pool_w, pool_scale)
    f_exp = _mm(h, w_f_exp, name="in_proj_forget_lanes").reshape(nb, s, FOX_WIDTH)
    c_exp = _forget_cumsum(f_exp, b_f_exp, "forget_cumsum_lanes")
    c_pad = _forget_cumsum(f_pad.reshape(nb, s, LANES), b_f_pad, "forget_cumsum")
    c_row = c_pad[:, :, :FOX_HEADS].transpose(0, 2, 1).reshape(nb, FOX_HEADS, s // blk, 1, blk)
    (o, o_b, lse), gathered = _fox_fwd(qkv3, c_exp, c_row, ex=_gather_exchange([local_b[n] for n in later]))
    stacks = dict(zip(later, _place_own(gathered, [local_b[n] for n in later])))
    stacks.update(zip(last, _place_own(last_others, [local_b[n] for n in last])))
    w_pool_out3, w_fox_out3, w_xo3, w_ffn_in3 = (stacks[n] for n in ("w_pool_out", "w_fox_out", "w_xo", "w_ffn_in"))
    w_out, w_xq, w_xkv, w_ffn_out = (stacks[n].reshape(-1, stacks[n].shape[2])
                                     for n in ("w_out", "w_xq", "w_xkv", "w_ffn_out"))
    y2, o2 = y.reshape(t, POOL_WIDTH), o_b.reshape(t, FOX_WIDTH)
    y_pool, y_fox, mix = _branches_mix(y2, o2, w_pool_out3, w_fox_out3, gu, b_gate)
    x1, hx = _mm_res_norm(mix, w_out, x2, g_x, "mix_out_norm_x")
    mem_n = _rms_fwd(mem2, g_mem, "norm_mem")
    qx = _mm(hx, w_xq, out_dtype=BF16, name="x_q")
    kv = _mm(mem_n, w_xkv, out_dtype=BF16, name="x_kv")
    qx3, kv3 = qx.reshape(nb, s, X_WIDTH), kv.reshape(nb, n_mem, 2 * X_WIDTH)
    ox = _xattn_fwd(qx3, kv3).reshape(t, X_WIDTH)
    w_xo = w_xo3.transpose(1, 0, 2).reshape(X_WIDTH, D_MODEL)
    x2_, hf = _mm_res_norm(ox, w_xo, x1, g_ffn, "x_out_norm_ffn")
    ffn_gate, ffn_up, act = _ffn_in(hf, w_ffn_in3)

    dx3, dx3_b, dg_final, loss_part = _ffn_out_loss(act, w_ffn_out, x2_, tgt2, g_final)
    dw_ffn_out = _mm(act, dx3_b, ta=True, bm=1408, bn=512, bk=2048, name="d_w_ffn_out")
    dffn = _ffn_act_bwd(dx3_b, w_ffn_out.T, ffn_gate, ffn_up)
    dw_ffn_in = _mm(hf, dffn, ta=True, bm=512, bn=1408, bk=2048, out_stack=True, name="d_w_ffn_in")
    core = lax.axis_index("c").astype(jnp.int32).reshape(1)
    ffn_group = ["w_ffn_in", "w_ffn_out"]
    mid_group = ["w_pool_out", "w_fox_out", "w_out", "w_xq", "w_xkv", "w_xo"]
    grad_stacks = {"w_ffn_in": dw_ffn_in, "w_ffn_out": _stack_of(dw_ffn_out, 0)}

    def presum(group, theirs):
        return [_sum_halves(grad_stacks[n], t_, core, "sum_halves_" + n) for n, t_ in zip(group, theirs)]

    (dx2, dx2_b, dg_ffn), theirs = _mm_norm_bwd(dffn, _stack_t(w_ffn_in3), x2_, g_ffn, dx3, "d_hf_norm_ffn_bwd",
                                                ex=_swap_exchange([grad_stacks[n] for n in ffn_group]))
    chip_sums = dict(zip(ffn_group, presum(ffn_group, theirs)))

    dw_xo = _mm(ox, dx2_b, ta=True, bn=256, out_stack=True, name="d_w_xo")
    dox = _mm(dx2_b, _stack_t(w_xo3), out_dtype=BF16, name="d_ox").reshape(nb, s, X_WIDTH)
    dqx, dkv = _xattn_bwd(qx3, kv3, dox)
    dqx2, dkv2 = dqx.reshape(t, X_WIDTH), dkv.reshape(nb * n_mem, 2 * X_WIDTH)
    dw_xkv = _mm(mem_n, dkv2, ta=True, name="d_w_xkv")
    dmem_n = _mm(dkv2, w_xkv.T, name="d_mem_n")
    dg_mem = _rms_bwd(dmem_n, mem2, g_mem, None, "norm_mem_bwd")
    dw_xq = _mm(hx, dqx2, ta=True, name="d_w_xq")
    dx1, dx1_b, dg_x = _mm_norm_bwd(dqx2, w_xq.T, x1, g_x, dx2, "d_hx_norm_x_bwd")

    dw_out = _mm(mix, dx1_b, ta=True, name="d_w_out")
    dyp, dyf, dgu, db_gate = _mix_bwd(gu, b_gate, y_pool, y_fox, dx1_b, w_out.T)
    dw_pool_out = _mm(y2, dyp, ta=True, bn=256, out_stack=True, name="d_w_pool_out")
    dw_fox_out = _mm(o2, dyf, ta=True, bn=256, out_stack=True, name="d_w_fox_out")
    dy = _mm(dyp, _stack_t(w_pool_out3), name="d_y").reshape(nb, s, POOL_WIDTH)
    do = _mm(dyf, _stack_t(w_fox_out3), name="d_o").reshape(nb, s, FOX_WIDTH)
    dgu3, dpool_w, dpool_scale = _pool_bwd(gu3, dy, pool_w, pool_scale, dgu.reshape(nb, s, GU_COLS))
    grad_stacks.update({"w_pool_out": dw_pool_out, "w_fox_out": dw_fox_out, "w_out": _stack_of(dw_out, 0),
                        "w_xq": _stack_of(dw_xq, 0), "w_xkv": _stack_of(dw_xkv, 0), "w_xo": dw_xo})
    dgu2 = dgu3.reshape(t, GU_COLS)
    dw_gu_t, theirs = _mm(dgu2, h, ta=True, name="d_w_gates_pool",
                          ex=_swap_exchange([grad_stacks[n] for n in mid_group]))
    chip_sums.update(zip(mid_group, presum(mid_group, theirs)))
    early = ffn_group + mid_group
    (dq3, dkv3, dc_q, dc_row), early_slots = _fox_bwd(qkv3, c_exp, c_row, lse, o, do,
                                                      ex=_chips_exchange([chip_sums[n] for n in early]))
    slots = dict(zip(early, early_slots))
    dc = dc_row.reshape(nb, FOX_HEADS, s).transpose(0, 2, 1) + dc_q.transpose(0, 2, 1, 3).reshape(nb, s, FOX_HEADS)
    dc = jnp.pad(dc, ((0, 0), (0, 0), (0, LANES - FOX_HEADS)))
    df, db_f = _forget_bwd(dc, f_pad.reshape(nb, s, LANES), b_f_pad)
    dq2, dkv2, df2 = dq3.reshape(t, FOX_WIDTH), dkv3.reshape(t, 2 * FOX_WIDTH), df.reshape(t, LANES)
    dw_q_t = _mm(dq2, h, ta=True, name="d_w_q")
    dw_kv_t = _mm(dkv2, h, ta=True, name="d_w_kv")
    dw_f_t = _mm(df2, h, ta=True, name="d_w_forget")
    dw_k_t, dw_v_t = _unpair_rows(dw_kv_t)
    dw_in_t = jnp.concatenate([dw_gu_t[GATE_WIDTH:], dw_q_t, dw_k_t, dw_v_t, dw_f_t[:FOX_HEADS],
                               dw_gu_t[:GATE_WIDTH]])
    grad_stacks["w_in"] = dw_in_t.reshape(N_CHIPS, IN_COLS // N_CHIPS, D_MODEL)
    chip_sums["w_in"], = presum(["w_in"], _run_exchange(_swap_exchange([grad_stacks["w_in"]]), "swap_halves_w_in"))
    dh, (slots["w_in"],) = _input_grad([dgu2, dq2, dkv2, df2],
                                       [w_gu_t, w_qkv_t[:FOX_WIDTH], w_qkv_t[FOX_WIDTH:], w_f_t],
                                       _chips_exchange([chip_sums["w_in"]]))

    place = jnp.stack([lax.axis_index("c"), 2 * lax.axis_index("x") + lax.axis_index("y")]).astype(jnp.int32)
    halves = [_sum_chips(slots[n], chip_sums[n], place, _by_rows(local[n].shape[0]), "sum_chips_" + n) for n in names]
    (dx, _, dg_mix), reduced = _rms_bwd(dh, x2, g_mix, dx1, "norm_mix_bwd", ex=_join_exchange(halves))

    small_grads = {"norm_mix_g": dg_mix, "b_forget": db_f[:, :FOX_HEADS], "b_gate": db_gate, "pool_w": dpool_w,
                   "pool_scale": dpool_scale, "norm_x_g": dg_x, "norm_mem_g": dg_mem, "norm_ffn_g": dg_ffn,
                   "norm_final_g": dg_final}
    def flat2d(a):
        return a.reshape(-1, a.shape[-1])

    small_names = [n for n, _ in SMALL]
    own = [flat2d(small_grads[n]) for n in small_names]
    small_gather = _small_exchange(own + [loss_part])

    def tiles_of(a):
        return a.transpose(2, 0, 1)

    def block_of(a3):
        return a3.transpose(1, 2, 0)

    grads, deltas, new_m, new_v = {}, {}, {}, {}
    gathered = None
    for n, g_ in zip(names, reduced):
        if n == "w_in":
            g_ = lax.optimization_barrier(g_.reshape(IN_COLS // N_CHIPS, 1, D_MODEL))
            (d_, m_, v_), gathered = _adamw(tiles_of(weights[n]), g_, tiles_of(moments_m[n]), tiles_of(moments_v[n]),
                                            "adamw_" + n, ex=small_gather)
            back = block_of
        else:
            d_, m_, v_ = _adamw(local[n], g_, shard2d(moments_m[n], n), shard2d(moments_v[n], n), "adamw_" + n)
            back = functools.partial(unshard, n=n)
        grads[n], deltas[n], new_m[n], new_v[n] = (back(a) for a in (g_, d_, m_, v_))

    device = (4 * lax.axis_index("x") + 2 * lax.axis_index("y") + lax.axis_index("c")).astype(jnp.int32).reshape(1)
    sg, sd, sm, sv, loss_sum = _adamw_small(
        gathered[:-1], own, [flat2d(weights[n]) for n in small_names], [flat2d(moments_m[n]) for n in small_names],
        [flat2d(moments_v[n]) for n in small_names], gathered[-1], loss_part, device)
    for n, g_, d_, m_, v_ in zip(small_names, sg, sd, sm, sv):
        grads[n], deltas[n], new_m[n], new_v[n] = (a.reshape(weights[n].shape) for a in (g_, d_, m_, v_))
    return loss_sum[0, 0], dx.reshape(nb, s, d), grads, deltas, new_m, new_v


def kernel(x, mem, norm_mix_g, w_in, b_forget, b_gate, pool_w, pool_scale, w_pool_out, w_fox_out, w_out, norm_x_g, norm_mem_g, w_xq, w_xkv, w_xo, norm_ffn_g, w_ffn_in, w_ffn_out, norm_final_g, loss_target, m_norm_mix_g, m_w_in, m_b_forget, m_b_gate, m_pool_w, m_pool_scale, m_w_pool_out, m_w_fox_out, m_w_out, m_norm_x_g, m_norm_mem_g, m_w_xq, m_w_xkv, m_w_xo, m_norm_ffn_g, m_w_ffn_in, m_w_ffn_out, m_norm_final_g, v_norm_mix_g, v_w_in, v_b_forget, v_b_gate, v_pool_w, v_pool_scale, v_w_pool_out, v_w_fox_out, v_w_out, v_norm_x_g, v_norm_mem_g, v_w_xq, v_w_xkv, v_w_xo, v_norm_ffn_g, v_w_ffn_in, v_w_ffn_out, v_norm_final_g):
    given = dict(locals())
    weights = {n: given[n] for n in WEIGHT_ORDER}
    moments_m = {n: given["m_" + n] for n in WEIGHT_ORDER}
    moments_v = {n: given["v_" + n] for n in WEIGHT_ORDER}
    loss, grad_x, grads, deltas, new_m, new_v = _step(x, mem, loss_target, weights, moments_m, moments_v)
    return (loss, grad_x, *[grads[n] for n in WEIGHT_ORDER], *[deltas[n] for n in WEIGHT_ORDER],
            *[new_m[n] for n in WEIGHT_ORDER], *[new_v[n] for n in WEIGHT_ORDER])
```

```python
import functools
import math

import jax
import jax.numpy as jnp
from jax import lax
from jax.experimental import pallas as pl
from jax.experimental.pallas import tpu as pltpu

F32 = jnp.float32
BF16 = jnp.bfloat16
MESH = pl.DeviceIdType.MESH

D_MODEL = 1024
EPS = 1e-6
POOL_WINDOWS = (2, 4, 8, 16)
POOL_WIDTH = 512
POOL_GC = 128
FOX_HEADS = 8
FOX_DH = 64
FOX_WIDTH = 512
X_HEADS = 4
X_DH = 128
X_WIDTH = 512
D_FF = 2816
IN_COLS = 4104
GATE_WIDTH = 2048
ADAM_LR = 0.001
ADAM_B1 = 0.9
ADAM_B2 = 0.999
ADAM_EPS = 1e-08
ADAM_WD = 0.01
ADAM_STEP = 10

N_CHIPS = 4
N_DEV = 8
LANES = 128
VMEM_LIMIT_BYTES = 56 * 1024 * 1024
NEG_INF = -1e30
ATT_BLOCK = 512

SHARDED = (
    ("w_in", (1024, IN_COLS), 1),
    ("w_pool_out", (POOL_WIDTH, 1024), 1),
    ("w_fox_out", (FOX_WIDTH, 1024), 1),
    ("w_out", (1024, 1024), 0),
    ("w_xq", (1024, X_WIDTH), 0),
    ("w_xkv", (1024, 2 * X_WIDTH), 0),
    ("w_xo", (X_WIDTH, 1024), 1),
    ("w_ffn_in", (1024, 2 * D_FF), 1),
    ("w_ffn_out", (D_FF, 1024), 0),
)
SMALL = (
    ("norm_mix_g", (1, 1024)),
    ("b_forget", (1, 8)),
    ("b_gate", (1, 2048)),
    ("pool_w", (1, 4, 128, 128)),
    ("pool_scale", (1, 512)),
    ("norm_x_g", (1, 1024)),
    ("norm_mem_g", (1, 1024)),
    ("norm_ffn_g", (1, 1024)),
    ("norm_final_g", (1024,)),
)
WEIGHT_ORDER = ("norm_mix_g", "w_in", "b_forget", "b_gate", "pool_w", "pool_scale", "w_pool_out", "w_fox_out", "w_out",
                "norm_x_g", "norm_mem_g", "w_xq", "w_xkv", "w_xo", "norm_ffn_g", "w_ffn_in", "w_ffn_out", "norm_final_g")


def _cparams(sem=None):
    return pltpu.CompilerParams(dimension_semantics=sem, vmem_limit_bytes=VMEM_LIMIT_BYTES)


def _block(dim, pref, unit):
    if dim <= pref:
        return dim
    best = None
    for b in range(unit, pref + 1, unit):
        if dim % b == 0:
            best = b
    assert best is not None, (dim, pref, unit)
    return best


def _rows_block(rows, cols, unit=16, elems=1 << 19):
    return _block(rows, max(unit, elems // cols // unit * unit), unit)


def _my_place():
    return lax.axis_index("x"), lax.axis_index("y"), lax.axis_index("c")


def _other_chips(x, y):
    return [(1 - x, y), (x, 1 - y), (1 - x, 1 - y)]


def _chip(place):
    return 2 * place[0] + place[1]


ANY = pl.BlockSpec(memory_space=pl.ANY)


def _by_rows(rows):
    return rows % 32 == 0


def _half_shape(rows, cols):
    return (rows // 2, cols) if _by_rows(rows) else (rows, cols // 2)


def _core_half(ref, core, lead=()):
    rows, cols = ref.shape[-2:]
    if _by_rows(rows):
        return ref.at[(*lead, pl.ds(core * (rows // 2), rows // 2), slice(None))]
    return ref.at[(*lead, slice(None), pl.ds(core * (cols // 2), cols // 2))]


class _Exchange:
    def __init__(self, arrays, out_shapes, n_sems, start, finish, in_place=False):
        self.arrays, self.out_shapes, self.n_sems, self.start, self.finish = arrays, out_shapes, n_sems, start, finish
        self.in_place = in_place

    def scratch(self):
        return [pltpu.SemaphoreType.DMA((self.n_sems,)), pltpu.SemaphoreType.DMA((self.n_sems,))]

    def aliases(self, first_in, first_out):
        return {first_in + k: first_out + k for k in range(len(self.arrays))} if self.in_place else {}


def _run_exchange(ex, name):
    n = len(ex.arrays)

    def body(*refs):
        ins, outs, sems = refs[:n], refs[n:2 * n], refs[2 * n:]
        ex.start(ins, outs, *sems)
        ex.finish(ins, outs, *sems)

    return pl.pallas_call(
        body, name=name, out_shape=ex.out_shapes, in_specs=[ANY] * n, out_specs=[ANY] * n, scratch_shapes=ex.scratch(),
        input_output_aliases=ex.aliases(0, 0),
    )(*ex.arrays)


def _hosted_call(body, ex, *, name, grid, in_specs, out_specs, out_shape, args, scratch=()):
    n_in, n_out, n_scr = len(args), len(out_shape), len(scratch)
    if ex is None:
        outs = pl.pallas_call(
            body, name=name, grid=grid, out_shape=out_shape, in_specs=in_specs, out_specs=out_specs,
            scratch_shapes=list(scratch), compiler_params=_cparams(("arbitrary",) * len(grid)))(*args)
        return outs, None
    nc = len(ex.arrays)

    def full_body(*refs):
        ins, cins = refs[:n_in], refs[n_in:n_in + nc]
        outs, couts = refs[n_in + nc:n_in + nc + n_out], refs[n_in + nc + n_out:n_in + 2 * nc + n_out]
        rest = refs[n_in + 2 * nc + n_out:]
        scr, sems = rest[:n_scr], rest[n_scr:]
        first = functools.reduce(jnp.logical_and, [pl.program_id(a) == 0 for a in range(len(grid))])
        last = functools.reduce(jnp.logical_and, [pl.program_id(a) == grid[a] - 1 for a in range(len(grid))])

        @pl.when(first)
        def _():
            ex.start(cins, couts, *sems)

        body(*ins, *outs, *scr)

        @pl.when(last)
        def _():
            ex.finish(cins, couts, *sems)

    outs = pl.pallas_call(
        full_body, name=name, grid=grid, out_shape=list(out_shape) + list(ex.out_shapes),
        in_specs=list(in_specs) + [ANY] * nc, out_specs=list(out_specs) + [ANY] * nc,
        scratch_shapes=list(scratch) + ex.scratch(), input_output_aliases=ex.aliases(n_in, n_out),
        compiler_params=_cparams(("arbitrary",) * len(grid)))(*args, *ex.arrays)
    return outs[:n_out], outs[n_out:]


def _gather_exchange(shards):
    n = len(shards)

    def copies(ins, outs, send_sems, recv_sems):
        x, y, c = _my_place()

        def half(k, chip, core):
            return _core_half(outs[k], core, lead=(_chip(chip),))

        def copy(k, slot, chip, core, to, src=None):
            return pltpu.make_async_remote_copy(
                src_ref=half(k, chip, core) if src is None else src, dst_ref=half(k, chip, core),
                send_sem=send_sems.at[6 * k + slot], recv_sem=recv_sems.at[6 * k + slot],
                device_id=to, device_id_type=MESH)

        return (x, y, c), copy

    def first_copies(ins, outs, send_sems, recv_sems):
        (x, y, c), copy = copies(ins, outs, send_sems, recv_sems)
        out = []
        for j, chip in enumerate(_other_chips(x, y)):
            for k in range(n):
                out.append(copy(k, j, (x, y), c, (*chip, c), src=_core_half(ins[k], c)))
        return out

    def start(ins, outs, send_sems, recv_sems):
        for cp in first_copies(ins, outs, send_sems, recv_sems):
            cp.start()

    def finish(ins, outs, send_sems, recv_sems):
        (x, y, c), copy = copies(ins, outs, send_sems, recv_sems)
        chips = _other_chips(x, y)
        passed = []
        for j, chip in enumerate(chips):
            for k in range(n):
                copy(k, j, chip, c, (x, y, c)).wait_recv()
                passed.append(copy(k, 3 + j, chip, c, (x, y, 1 - c)))
                passed[-1].start()
        for j, chip in enumerate(chips):
            for k in range(n):
                copy(k, 3 + j, chip, 1 - c, (x, y, c)).wait_recv()
        for cp in first_copies(ins, outs, send_sems, recv_sems) + passed:
            cp.wait_send()

    return _Exchange(list(shards), [jax.ShapeDtypeStruct((N_CHIPS,) + s.shape, s.dtype) for s in shards], 6 * n,
                     start, finish)


def _place_own(stacks, shards):
    me = 2 * lax.axis_index("x") + lax.axis_index("y")
    return [lax.dynamic_update_slice(others, mine[None], (me, 0, 0)) for others, mine in zip(stacks, shards)]


def _swap_exchange(grads):
    n = len(grads)

    def copies(ins, outs, send_sems, recv_sems):
        x, y, c = _my_place()
        return [pltpu.make_async_remote_copy(
            src_ref=_core_half(ins[k], 1 - c, lead=(slice(None),)), dst_ref=outs[k],
            send_sem=send_sems.at[k], recv_sem=recv_sems.at[k], device_id=(x, y, 1 - c), device_id_type=MESH)
            for k in range(n)]

    def start(ins, outs, send_sems, recv_sems):
        for cp in copies(ins, outs, send_sems, recv_sems):
            cp.start()

    def finish(ins, outs, send_sems, recv_sems):
        for cp in copies(ins, outs, send_sems, recv_sems):
            cp.wait()

    return _Exchange(list(grads), [jax.ShapeDtypeStruct((N_CHIPS,) + _half_shape(*g.shape[1:]), g.dtype) for g in grads],
                     n, start, finish)


def _chips_exchange(sums):
    n = len(sums)

    def sends(ins, outs, send_sems, recv_sems):
        x, y, c = _my_place()
        return [pltpu.make_async_remote_copy(
            src_ref=ins[k].at[_chip(chip)], dst_ref=outs[k].at[_chip((x, y))],
            send_sem=send_sems.at[3 * k + j], recv_sem=recv_sems.at[3 * k + j],
            device_id=(*chip, c), device_id_type=MESH)
            for j, chip in enumerate(_other_chips(x, y)) for k in range(n)]

    def start(ins, outs, send_sems, recv_sems):
        for cp in sends(ins, outs, send_sems, recv_sems):
            cp.start()

    def finish(ins, outs, send_sems, recv_sems):
        x, y, c = _my_place()
        for j, chip in enumerate(_other_chips(x, y)):
            for k in range(n):
                slot = outs[k].at[_chip(chip)]
                pltpu.make_async_remote_copy(
                    src_ref=slot, dst_ref=slot, send_sem=send_sems.at[3 * k + j], recv_sem=recv_sems.at[3 * k + j],
                    device_id=(x, y, c), device_id_type=MESH).wait_recv()
        for cp in sends(ins, outs, send_sems, recv_sems):
            cp.wait_send()

    return _Exchange(list(sums), [jax.ShapeDtypeStruct(s.shape, s.dtype) for s in sums], 3 * n, start, finish)


def _join_exchange(shards):
    n = len(shards)

    def sends(ins, outs, send_sems, recv_sems):
        x, y, c = _my_place()
        return [pltpu.make_async_remote_copy(
            src_ref=_core_half(ins[k], c), dst_ref=_core_half(outs[k], c),
            send_sem=send_sems.at[k], recv_sem=recv_sems.at[k], device_id=(x, y, 1 - c), device_id_type=MESH)
            for k in range(n)]

    def start(ins, outs, send_sems, recv_sems):
        for cp in sends(ins, outs, send_sems, recv_sems):
            cp.start()

    def finish(ins, outs, send_sems, recv_sems):
        x, y, c = _my_place()
        for k in range(n):
            theirs = _core_half(outs[k], 1 - c)
            pltpu.make_async_remote_copy(
                src_ref=theirs, dst_ref=theirs, send_sem=send_sems.at[k], recv_sem=recv_sems.at[k],
                device_id=(x, y, c), device_id_type=MESH).wait_recv()
        for cp in sends(ins, outs, send_sems, recv_sems):
            cp.wait_send()

    return _Exchange(list(shards), [jax.ShapeDtypeStruct(s.shape, s.dtype) for s in shards], n, start, finish,
                     in_place=True)


def _small_exchange(blocks):
    n = len(blocks)

    def copies(ins, outs, send_sems, recv_sems):
        x, y, c = _my_place()

        def copy(k, j, whose, to, src=None):
            slot = outs[k].at[4 * whose[0] + 2 * whose[1] + whose[2]]
            return pltpu.make_async_remote_copy(
                src_ref=slot if src is None else src, dst_ref=slot,
                send_sem=send_sems.at[7 * k + j], recv_sem=recv_sems.at[7 * k + j], device_id=to, device_id_type=MESH)

        return (x, y, c), copy

    def first_copies(ins, outs, send_sems, recv_sems):
        (x, y, c), copy = copies(ins, outs, send_sems, recv_sems)
        out = []
        for k in range(n):
            out.append(copy(k, 0, (x, y, c), (x, y, 1 - c), src=ins[k]))
            out += [copy(k, 1 + j, (x, y, c), (*chip, c), src=ins[k]) for j, chip in enumerate(_other_chips(x, y))]
        return out

    def start(ins, outs, send_sems, recv_sems):
        for cp in first_copies(ins, outs, send_sems, recv_sems):
            cp.start()

    def finish(ins, outs, send_sems, recv_sems):
        (x, y, c), copy = copies(ins, outs, send_sems, recv_sems)
        chips = _other_chips(x, y)
        passed = []
        for j, chip in enumerate(chips):
            for k in range(n):
                copy(k, 1 + j, (*chip, c), (x, y, c)).wait_recv()
                passed.append(copy(k, 4 + j, (*chip, c), (x, y, 1 - c)))
                passed[-1].start()
        for k in range(n):
            copy(k, 0, (x, y, 1 - c), (x, y, c)).wait_recv()
        for j, chip in enumerate(chips):
            for k in range(n):
                copy(k, 4 + j, (*chip, 1 - c), (x, y, c)).wait_recv()
        for cp in first_copies(ins, outs, send_sems, recv_sems) + passed:
            cp.wait_send()

    return _Exchange(list(blocks), [jax.ShapeDtypeStruct((N_DEV,) + blk.shape, blk.dtype) for blk in blocks], 7 * n,
                     start, finish)


def _sum_halves(grads, theirs, core, name):
    _, h, cols = theirs.shape
    by_rows = _by_rows(grads.shape[1])
    br = _rows_block(h, cols) if by_rows else h
    nb = h // br

    def body(core_ref, a_ref, b_ref, o_ref):
        o_ref[...] = (a_ref[...] + b_ref[...]).astype(BF16)

    if by_rows:
        mine = pl.BlockSpec((1, br, cols), lambda j, i, core_ref: (j, core_ref[0] * nb + i, 0))
    else:
        mine = pl.BlockSpec((1, br, cols), lambda j, i, core_ref: (j, i, core_ref[0]))
    return pl.pallas_call(
        body, name=name,
        out_shape=jax.ShapeDtypeStruct(theirs.shape, BF16),
        grid_spec=pltpu.PrefetchScalarGridSpec(
            num_scalar_prefetch=1, grid=(N_CHIPS, nb),
            in_specs=[mine, pl.BlockSpec((1, br, cols), lambda j, i, core_ref: (j, i, 0))],
            out_specs=pl.BlockSpec((1, br, cols), lambda j, i, core_ref: (j, i, 0))),
        compiler_params=_cparams(("parallel", "parallel")),
    )(core, grads, theirs)


def _sum_chips(slots, sums, place, by_rows, name):
    _, h, cols = slots.shape
    br = _rows_block(h, cols) if by_rows else h
    nb = h // br

    def body(place_ref, s_ref, own_ref, o_ref):
        me = place_ref[1]
        acc = None
        for k in range(N_CHIPS):
            term = jnp.where(me == k, own_ref[k], s_ref[k]).astype(F32)
            acc = term if acc is None else acc + term
        o_ref[...] = acc

    stack = pl.BlockSpec((N_CHIPS, br, cols), lambda i, place_ref: (0, i, 0))
    if by_rows:
        out_shape, out_map = (2 * h, cols), lambda i, place_ref: (place_ref[0] * nb + i, 0)
    else:
        out_shape, out_map = (h, 2 * cols), lambda i, place_ref: (i, place_ref[0])
    return pl.pallas_call(
        body, name=name,
        out_shape=jax.ShapeDtypeStruct(out_shape, F32),
        grid_spec=pltpu.PrefetchScalarGridSpec(
            num_scalar_prefetch=1, grid=(nb,), in_specs=[stack, stack],
            out_specs=pl.BlockSpec((br, cols), out_map)),
        compiler_params=_cparams(("parallel",)),
    )(place, slots, sums)


def _adamw_math(w, g, m, v):
    m = ADAM_B1 * m + (1.0 - ADAM_B1) * g
    v = ADAM_B2 * v + (1.0 - ADAM_B2) * (g * g)
    m_hat = m / (1.0 - ADAM_B1 ** ADAM_STEP)
    v_hat = v / (1.0 - ADAM_B2 ** ADAM_STEP)
    delta = -ADAM_LR * (m_hat / (jnp.sqrt(v_hat) + ADAM_EPS) + ADAM_WD * w)
    return delta, m, v


def _adamw(w, g, m, v, name, ex=None):
    def body(w_ref, g_ref, m_ref, v_ref, d_ref, nm_ref, nv_ref):
        d, nm, nv = _adamw_math(w_ref[...], g_ref[...], m_ref[...], v_ref[...])
        d_ref[...] = d
        nm_ref[...] = nm
        nv_ref[...] = nv

    if w.ndim == 3:
        rows = w.shape[0]
        br = max(b for b in range(1, 65) if rows % b == 0)
        spec, steps = pl.BlockSpec((br,) + w.shape[1:], lambda i: (i, 0, 0)), rows // br
    else:
        rows, cols = w.shape
        br = _rows_block(rows, cols, unit=8)
        spec, steps = pl.BlockSpec((br, cols), lambda i: (i, 0)), rows // br
    shape = jax.ShapeDtypeStruct(w.shape, F32)
    outs, moved = _hosted_call(
        body, ex, name=name, out_shape=(shape, shape, shape), grid=(steps,),
        in_specs=[spec] * 4, out_specs=(spec, spec, spec), args=(w, g, m, v))
    return tuple(outs) if ex is None else (tuple(outs), moved)


def _adamw_small(parts, own, ws, ms, vs, loss_parts, loss_own, device):
    n = len(ws)

    def total(device_ref, parts_ref, own_ref):
        acc = None
        for dev in range(N_DEV):
            term = jnp.where(device_ref[0] == dev, own_ref[...], parts_ref[dev])
            acc = term if acc is None else acc + term
        return acc

    def body(device_ref, *refs):
        ins, outs = refs[:5 * n + 2], refs[5 * n + 2:]
        for k in range(n):
            g = total(device_ref, ins[k], ins[n + k])
            d, nm, nv = _adamw_math(ins[2 * n + k][...], g, ins[3 * n + k][...], ins[4 * n + k][...])
            for o_ref, val in zip(outs[k::n][:4], (g, d, nm, nv)):
                o_ref[...] = val
        outs[4 * n][...] = total(device_ref, ins[5 * n], ins[5 * n + 1])

    args = list(parts) + list(own) + list(ws) + list(ms) + list(vs) + [loss_parts, loss_own]
    whole = lambda a: pl.BlockSpec(a.shape, lambda i, device_ref, nd=a.ndim: (0,) * nd)
    shapes = [jax.ShapeDtypeStruct(w.shape, F32) for w in ws] * 4 + [jax.ShapeDtypeStruct(loss_own.shape, F32)]
    outs = pl.pallas_call(
        body, name="adamw_small", out_shape=shapes,
        grid_spec=pltpu.PrefetchScalarGridSpec(
            num_scalar_prefetch=1, grid=(1,), in_specs=[whole(a) for a in args], out_specs=[whole(a) for a in shapes]),
        compiler_params=_cparams(("arbitrary",)),
    )(device, *args)
    return outs[:n], outs[n:2 * n], outs[2 * n:3 * n], outs[3 * n:4 * n], outs[4 * n]


def _mm(a, b, *, name, ta=False, out_dtype=F32, res=None, bm=1024, bn=1024, bk=4096, b_stack=False, out_stack=False,
        ex=None):
    if ta:
        kdim, m = a.shape
    else:
        m, kdim = a.shape
    if b_stack:
        _, kb, chunk = b.shape
        n = N_CHIPS * chunk
    else:
        kb, n = b.shape
        chunk = n // N_CHIPS if out_stack else n
    assert kdim == kb, (a.shape, b.shape, ta)
    bm = _block(m, bm, LANES if ta else 16)
    bn = _block(chunk, bn, LANES)
    bk = _block(kdim, bk, LANES)
    nk = kdim // bk
    per_chunk = chunk // bn
    dims = (((0 if ta else 1,), (0,)), ((), ()))

    def body(*refs):
        refs = list(refs)
        a_ref, b_ref = refs[:2]
        r_ref = refs[2] if res is not None else None
        o_ref = refs[3] if res is not None else refs[2]
        part = lax.dot_general(a_ref[...].astype(BF16), b_ref[...].astype(BF16), dims, preferred_element_type=F32)

        def finish(r):
            if r_ref is not None:
                r = r + r_ref[...]
            o_ref[...] = r.astype(out_dtype)

        if nk == 1:
            finish(part)
        else:
            acc_ref = refs[-1]
            k = pl.program_id(2)

            @pl.when(k == 0)
            def _():
                acc_ref[...] = part

            @pl.when(k > 0)
            def _():
                acc_ref[...] += part

            @pl.when(k == nk - 1)
            def _():
                finish(acc_ref[...])

    a_spec = pl.BlockSpec((bk, bm), lambda i, j, k: (k, i)) if ta else pl.BlockSpec((bm, bk), lambda i, j, k: (i, k))
    if b_stack:
        b_spec = pl.BlockSpec((None, bk, bn), lambda i, j, k: (j // per_chunk, k, j % per_chunk))
    else:
        b_spec = pl.BlockSpec((bk, bn), lambda i, j, k: (k, j))
    r_spec = pl.BlockSpec((bm, bn), lambda i, j, k: (i, j))
    if out_stack:
        o_spec = pl.BlockSpec((None, bm, bn), lambda i, j, k: (j // per_chunk, i, j % per_chunk))
        o_shape = (N_CHIPS, m, chunk)
    else:
        o_spec, o_shape = r_spec, (m, n)
    in_specs = [a_spec, b_spec] + ([r_spec] if res is not None else [])
    args = (a, b) + ((res,) if res is not None else ())
    (out,), moved = _hosted_call(
        body, ex, name=name, out_shape=(jax.ShapeDtypeStruct(o_shape, out_dtype),),
        grid=(m // bm, n // bn, nk), in_specs=in_specs, out_specs=(o_spec,),
        scratch=[pltpu.VMEM((bm, bn), F32)] if nk > 1 else [], args=args)
    return out if ex is None else (out, moved)


def _rms_fwd(x, g, name, ex=None):
    t, d = x.shape
    bt = _block(t, 512, 16)

    def body(x_ref, g_ref, h_ref):
        xv = x_ref[...]
        r = lax.rsqrt(jnp.mean(xv * xv, axis=-1, keepdims=True) + EPS)
        h_ref[...] = (xv * r * g_ref[...]).astype(BF16)

    (out,), moved = _hosted_call(
        body, ex, name=name, out_shape=(jax.ShapeDtypeStruct((t, d), BF16),), grid=(t // bt,),
        in_specs=[pl.BlockSpec((bt, d), lambda i: (i, 0)), pl.BlockSpec((1, d), lambda i: (0, 0))],
        out_specs=(pl.BlockSpec((bt, d), lambda i: (i, 0)),), args=(x, g))
    return out if ex is None else (out, moved)


def _rms_bwd(dh, x, g, dres, name, ex=None):
    t, d = x.shape
    bt = _block(t, 256, 16)
    want_dx = dres is not None

    def body(*refs):
        if want_dx:
            dh_ref, x_ref, g_ref, dres_ref, dx_ref, dxb_ref, dg_ref = refs
        else:
            dh_ref, x_ref, g_ref, dg_ref = refs
        xv = x_ref[...]
        r = lax.rsqrt(jnp.mean(xv * xv, axis=-1, keepdims=True) + EPS)
        xhat = xv * r
        dhv = dh_ref[...]

        @pl.when(pl.program_id(0) == 0)
        def _():
            dg_ref[...] = jnp.zeros_like(dg_ref)

        dg_ref[...] += jnp.sum(dhv * xhat, axis=0, keepdims=True)
        if want_dx:
            dxhat = dhv * g_ref[...]
            dx = dres_ref[...] + r * (dxhat - xhat * jnp.mean(dxhat * xhat, axis=-1, keepdims=True))
            dx_ref[...] = dx
            dxb_ref[...] = dx.astype(BF16)

    row = pl.BlockSpec((bt, d), lambda i: (i, 0))
    vec = pl.BlockSpec((1, d), lambda i: (0, 0))
    if want_dx:
        outs, moved = _hosted_call(
            body, ex, name=name, grid=(t // bt,),
            out_shape=(jax.ShapeDtypeStruct((t, d), F32), jax.ShapeDtypeStruct((t, d), BF16),
                       jax.ShapeDtypeStruct((1, d), F32)),
            in_specs=[row, row, vec, row], out_specs=(row, row, vec), args=(dh, x, g, dres))
        return tuple(outs) if ex is None else (tuple(outs), moved)
    return pl.pallas_call(
        body, name=name, grid=(t // bt,), out_shape=jax.ShapeDtypeStruct((1, d), F32),
        in_specs=[row, row, vec], out_specs=vec,
        compiler_params=_cparams(("arbitrary",)),
    )(dh, x, g)


def _in_proj_attn(h, w_qkv, w_f_lanes, w_f):
    t, d = h.shape
    bm = _block(t, 512, 16)

    def body(h_ref, wq_ref, wl_ref, wf_ref, qkv_ref, fl_ref, f_ref):
        hv = h_ref[...]
        qkv_ref[...] = jnp.dot(hv, wq_ref[...], preferred_element_type=F32).astype(BF16)
        fl_ref[...] = jnp.dot(hv, wl_ref[...], preferred_element_type=F32)
        f_ref[...] = jnp.dot(hv, wf_ref[...], preferred_element_type=F32)

    whole = lambda w: pl.BlockSpec(w.shape, lambda i: (0, 0))
    rows = lambda n: pl.BlockSpec((bm, n), lambda i: (i, 0))
    return pl.pallas_call(
        body, name="in_proj_attn", grid=(t // bm,),
        out_shape=(jax.ShapeDtypeStruct((t, w_qkv.shape[1]), BF16), jax.ShapeDtypeStruct((t, w_f_lanes.shape[1]), F32),
                   jax.ShapeDtypeStruct((t, w_f.shape[1]), F32)),
        in_specs=[rows(d), whole(w_qkv), whole(w_f_lanes), whole(w_f)],
        out_specs=(rows(w_qkv.shape[1]), rows(w_f_lanes.shape[1]), rows(w_f.shape[1])),
        compiler_params=_cparams(("parallel",)),
    )(h, w_qkv, w_f_lanes, w_f)


def _mm_res_norm(a, b, res, g, name):
    t, k = a.shape
    d = b.shape[1]
    bm = _block(t, 512, 16)

    def body(a_ref, b_ref, r_ref, g_ref, x_ref, h_ref):
        xv = jnp.dot(a_ref[...], b_ref[...], preferred_element_type=F32) + r_ref[...]
        x_ref[...] = xv
        r = lax.rsqrt(jnp.mean(xv * xv, axis=-1, keepdims=True) + EPS)
        h_ref[...] = (xv * r * g_ref[...]).astype(BF16)

    row = pl.BlockSpec((bm, d), lambda i: (i, 0))
    return pl.pallas_call(
        body, name=name, grid=(t // bm,),
        out_shape=(jax.ShapeDtypeStruct((t, d), F32), jax.ShapeDtypeStruct((t, d), BF16)),
        in_specs=[pl.BlockSpec((bm, k), lambda i: (i, 0)), pl.BlockSpec((k, d), lambda i: (0, 0)), row,
                  pl.BlockSpec((1, d), lambda i: (0, 0))],
        out_specs=(row, row), compiler_params=_cparams(("parallel",)),
    )(a, b, res, g)


def _ffn_out_loss(act, w, res, target, g):
    t, k = act.shape
    d = w.shape[1]
    bm = _block(t, 512, 16)

    def body(a_ref, w_ref, r_ref, t_ref, g_ref, dx_ref, dxb_ref, dg_ref, loss_ref):
        xv = jnp.dot(a_ref[...], w_ref[...], preferred_element_type=F32) + r_ref[...]
        gv = g_ref[...]
        r = lax.rsqrt(jnp.mean(xv * xv, axis=-1, keepdims=True) + EPS)
        xhat = xv * r
        err = xhat * gv - t_ref[...]

        @pl.when(pl.program_id(0) == 0)
        def _():
            dg_ref[...] = jnp.zeros_like(dg_ref)
            loss_ref[...] = jnp.zeros_like(loss_ref)

        loss_ref[...] += 0.5 * jnp.sum(jnp.mean(err * err, axis=-1, keepdims=True), axis=0, keepdims=True)
        dy = err * (1.0 / d)
        dg_ref[...] += jnp.sum(dy * xhat, axis=0, keepdims=True)
        dxhat = dy * gv
        dx = r * (dxhat - xhat * jnp.mean(dxhat * xhat, axis=-1, keepdims=True))
        dx_ref[...] = dx
        dxb_ref[...] = dx.astype(BF16)

    row = pl.BlockSpec((bm, d), lambda i: (i, 0))
    vec = pl.BlockSpec((1, d), lambda i: (0, 0))
    return pl.pallas_call(
        body, name="ffn_out_loss", grid=(t // bm,),
        out_shape=(jax.ShapeDtypeStruct((t, d), F32), jax.ShapeDtypeStruct((t, d), BF16),
                   jax.ShapeDtypeStruct((1, d), F32), jax.ShapeDtypeStruct((1, LANES), F32)),
        in_specs=[pl.BlockSpec((bm, k), lambda i: (i, 0)), pl.BlockSpec((k, d), lambda i: (0, 0)), row, row, vec],
        out_specs=(row, row, vec, pl.BlockSpec((1, LANES), lambda i: (0, 0))),
        compiler_params=_cparams(("arbitrary",)),
    )(act, w, res, target, g)


def _mm_norm_bwd(a, b, x, g, dres, name, ex=None):
    t, k = a.shape
    d = b.shape[1]
    bm = _block(t, 512 if k <= 2048 else 256, 16)

    def body(a_ref, b_ref, x_ref, g_ref, dres_ref, dx_ref, dxb_ref, dg_ref):
        @pl.when(pl.program_id(0) == 0)
        def _():
            dg_ref[...] = jnp.zeros_like(dg_ref)

        dhv = jnp.dot(a_ref[...], b_ref[...], preferred_element_type=F32)
        xv = x_ref[...]
        r = lax.rsqrt(jnp.mean(xv * xv, axis=-1, keepdims=True) + EPS)
        xhat = xv * r
        dg_ref[...] += jnp.sum(dhv * xhat, axis=0, keepdims=True)
        dxhat = dhv * g_ref[...]
        dx = dres_ref[...] + r * (dxhat - xhat * jnp.mean(dxhat * xhat, axis=-1, keepdims=True))
        dx_ref[...] = dx
        dxb_ref[...] = dx.astype(BF16)

    row = pl.BlockSpec((bm, d), lambda i: (i, 0))
    vec = pl.BlockSpec((1, d), lambda i: (0, 0))
    outs, moved = _hosted_call(
        body, ex, name=name, grid=(t // bm,),
        out_shape=(jax.ShapeDtypeStruct((t, d), F32), jax.ShapeDtypeStruct((t, d), BF16), jax.ShapeDtypeStruct((1, d), F32)),
        in_specs=[pl.BlockSpec((bm, k), lambda i: (i, 0)), pl.BlockSpec((k, d), lambda i: (0, 0)), row, vec, row],
        out_specs=(row, row, vec), args=(a, b, x, g, dres))
    return tuple(outs) if ex is None else (tuple(outs), moved)


GU_COLS = GATE_WIDTH + POOL_WIDTH
U_BLK = GATE_WIDTH // POOL_WIDTH


def _shift_down(a, k, row):
    return jnp.where(row >= k, pltpu.roll(a, k, 0), 0.0)


def _shift_up(a, k, row):
    n = a.shape[0]
    return jnp.where(row < n - k, pltpu.roll(a, n - k, 0), 0.0)


def _window_delta(u, w, row):
    s, k = u, 1
    while k < w:
        s = s + _shift_down(s, k, row)
        k *= 2
    cnt = jnp.minimum(row + 1, w).astype(F32)
    return s / cnt - u, cnt


def _pool_fwd(gu, pool_w, pool_scale):
    b, s, _ = gu.shape

    def body(u_ref, pw_ref, sc_ref, y_ref):
        row = lax.broadcasted_iota(jnp.int32, (s, POOL_GC), 0)
        for g, w in enumerate(POOL_WINDOWS):
            cols = slice(g * POOL_GC, (g + 1) * POOL_GC)
            d, _ = _window_delta(u_ref[0, :, cols].astype(F32), w, row)
            z = jnp.dot(d.astype(BF16), pw_ref[g].astype(BF16), preferred_element_type=F32)
            y_ref[0, :, cols] = (z * sc_ref[:, cols]).astype(BF16)

    return pl.pallas_call(
        body, name="pool_fwd", out_shape=jax.ShapeDtypeStruct((b, s, POOL_WIDTH), BF16), grid=(b,),
        in_specs=[pl.BlockSpec((1, s, POOL_WIDTH), lambda i: (i, 0, U_BLK)),
                  pl.BlockSpec((4, POOL_GC, POOL_GC), lambda i: (0, 0, 0)),
                  pl.BlockSpec((1, POOL_WIDTH), lambda i: (0, 0))],
        out_specs=pl.BlockSpec((1, s, POOL_WIDTH), lambda i: (i, 0, 0)),
        compiler_params=_cparams(("parallel",)),
    )(gu, pool_w, pool_scale)


def _pool_bwd(gu, dyp, w_out_t, pool_w, pool_scale, dgu):
    b, s, _ = gu.shape

    def body(u_ref, dyp_ref, w_ref, pw_ref, sc_ref, dgu_in, du_ref, dpw_ref, dsc_ref):
        del dgu_in

        @pl.when(pl.program_id(0) == 0)
        def _():
            dpw_ref[...] = jnp.zeros_like(dpw_ref)
            dsc_ref[...] = jnp.zeros_like(dsc_ref)

        row = lax.broadcasted_iota(jnp.int32, (s, POOL_GC), 0)
        for g, w in enumerate(POOL_WINDOWS):
            cols = slice(g * POOL_GC, (g + 1) * POOL_GC)
            d, cnt = _window_delta(u_ref[0, :, cols].astype(F32), w, row)
            db = d.astype(BF16)
            pw = pw_ref[g].astype(BF16)
            z = jnp.dot(db, pw, preferred_element_type=F32)
            dyv = jnp.dot(dyp_ref[0], w_ref[:, cols], preferred_element_type=F32)
            dsc_ref[:, cols] += jnp.sum(dyv * z, axis=0, keepdims=True)
            dz = (dyv * sc_ref[:, cols]).astype(BF16)
            dpw_ref[g] += lax.dot_general(db, dz, (((0,), (0,)), ((), ())), preferred_element_type=F32)
            dd = lax.dot_general(dz, pw, (((1,), (1,)), ((), ())), preferred_element_type=F32)
            acc, k = dd / cnt, 1
            while k < w:
                acc = acc + _shift_up(acc, k, row)
                k *= 2
            du_ref[0, :, cols] = (acc - dd).astype(BF16)

    return pl.pallas_call(
        body, name="pool_bwd", grid=(b,),
        out_shape=(jax.ShapeDtypeStruct((b, s, GU_COLS), BF16), jax.ShapeDtypeStruct((4, POOL_GC, POOL_GC), F32),
                   jax.ShapeDtypeStruct((1, POOL_WIDTH), F32)),
        in_specs=[pl.BlockSpec((1, s, POOL_WIDTH), lambda i: (i, 0, U_BLK)),
                  pl.BlockSpec((1, s, dyp.shape[2]), lambda i: (i, 0, 0)),
                  pl.BlockSpec(w_out_t.shape, lambda i: (0, 0)),
                  pl.BlockSpec((4, POOL_GC, POOL_GC), lambda i: (0, 0, 0)),
                  pl.BlockSpec((1, POOL_WIDTH), lambda i: (0, 0)), ANY],
        out_specs=(pl.BlockSpec((1, s, POOL_WIDTH), lambda i: (i, 0, U_BLK)),
                   pl.BlockSpec((4, POOL_GC, POOL_GC), lambda i: (0, 0, 0)),
                   pl.BlockSpec((1, POOL_WIDTH), lambda i: (0, 0))),
        input_output_aliases={5: 0},
        compiler_params=_cparams(("arbitrary",)),
    )(gu, dyp, w_out_t, pool_w, pool_scale, dgu)


def _forget_cumsum(f, bias, name):
    b, s, c = f.shape

    def body(f_ref, b_ref, c_ref):
        row = lax.broadcasted_iota(jnp.int32, (s, LANES), 0)
        z = f_ref[0] + b_ref[...]
        acc = jnp.minimum(z, 0.0) - jnp.log(1.0 + jnp.exp(-jnp.abs(z)))
        k = 1
        while k < s:
            acc = acc + _shift_down(acc, k, row)
            k *= 2
        c_ref[0] = acc

    return pl.pallas_call(
        body, name=name, out_shape=jax.ShapeDtypeStruct((b, s, c), F32), grid=(b, c // LANES),
        in_specs=[pl.BlockSpec((1, s, LANES), lambda i, j: (i, 0, j)), pl.BlockSpec((1, LANES), lambda i, j: (0, j))],
        out_specs=pl.BlockSpec((1, s, LANES), lambda i, j: (i, 0, j)),
        compiler_params=_cparams(("parallel", "parallel")),
    )(f, bias)


def _forget_bwd(dc, f, bias):
    b, s, _ = f.shape

    def body(dc_ref, f_ref, b_ref, df_ref, db_ref):
        @pl.when(pl.program_id(0) == 0)
        def _():
            db_ref[...] = jnp.zeros_like(db_ref)

        row = lax.broadcasted_iota(jnp.int32, (s, LANES), 0)
        acc, k = dc_ref[0], 1
        while k < s:
            acc = acc + _shift_up(acc, k, row)
            k *= 2
        z = f_ref[0] + b_ref[...]
        df = acc / (1.0 + jnp.exp(z))
        db_ref[...] += jnp.sum(df, axis=0, keepdims=True)
        df_ref[0] = df.astype(BF16)

    blk = pl.BlockSpec((1, s, LANES), lambda i: (i, 0, 0))
    vec = pl.BlockSpec((1, LANES), lambda i: (0, 0))
    return pl.pallas_call(
        body, name="forget_bwd", grid=(b,),
        out_shape=(jax.ShapeDtypeStruct((b, s, LANES), BF16), jax.ShapeDtypeStruct((1, LANES), F32)),
        in_specs=[blk, blk, vec], out_specs=(blk, vec),
        compiler_params=_cparams(("arbitrary",)),
    )(dc, f, bias)


KV_BLK0 = 2
PAIRS = FOX_HEADS // 2
FOX_SCALE = FOX_DH ** -0.5
NT_DIMS = (((1,), (1,)), ((), ()))
TN_DIMS = (((0,), (0,)), ((), ()))


def _stack_heads(v):
    head = lax.broadcasted_iota(jnp.int32, v.shape, 1) // FOX_DH
    zero = jnp.zeros_like(v)
    return jnp.concatenate([jnp.where(head == 0, v, zero), jnp.where(head == 1, v, zero)], axis=0)


def _stack_cols(v):
    return jnp.concatenate([v[:, 0:1], v[:, FOX_DH:FOX_DH + 1]], axis=0)


def _unstack(t, blk):
    head = lax.broadcasted_iota(jnp.int32, (blk, LANES), 1) // FOX_DH
    return jnp.where(head == 0, t[:blk], t[blk:])


def _fox_scores(q_all, kblk, row_bias, cr_ref, kb, masked, blk):
    top = lax.broadcasted_iota(jnp.int32, (2 * blk, 1), 0) < blk
    s = lax.dot_general(q_all, kblk, NT_DIMS, preferred_element_type=F32)
    s = s + (row_bias - jnp.where(top, cr_ref[0, 0, kb], cr_ref[0, 1, kb]))
    if masked:
        r = lax.broadcasted_iota(jnp.int32, (2 * blk, blk), 0)
        keep = jnp.where(r >= blk, r - blk, r) >= lax.broadcasted_iota(jnp.int32, (2 * blk, blk), 1)
        s = jnp.where(keep, s, NEG_INF)
    return s


def _fox_fwd(qkv, c_exp, c_row, ex=None):
    b, s, _ = qkv.shape
    blk = min(ATT_BLOCK, s)
    nq = s // blk

    def body(q_ref, kv_ref, cc_ref, cr_ref, o_ref, ob_ref, lse_ref):
        qi = pl.program_id(2)
        q_all = _stack_heads(q_ref[0] * FOX_SCALE)
        cq = _stack_cols(cc_ref[0])

        def step(kb, carry, masked):
            m, l, acc = carry
            rows = pl.ds(pl.multiple_of(kb * blk, blk), blk)
            sc = _fox_scores(q_all, kv_ref[0, rows, :LANES], cq, cr_ref, kb, masked, blk)
            m_new = jnp.maximum(m, jnp.max(sc, axis=-1, keepdims=True))
            p = jnp.exp(sc - m_new)
            alpha = jnp.exp(m - m_new)
            l = alpha * l + jnp.sum(p, axis=-1, keepdims=True)
            acc = alpha * acc + jnp.dot(p.astype(BF16), kv_ref[0, rows, LANES:], preferred_element_type=F32)
            return m_new, l, acc

        init = (jnp.full((2 * blk, 1), NEG_INF, F32), jnp.zeros((2 * blk, 1), F32), jnp.zeros((2 * blk, LANES), F32))
        m, l, acc = step(qi, lax.fori_loop(0, qi, functools.partial(step, masked=False), init), True)
        o = _unstack(acc / l, blk)
        o_ref[0] = o
        ob_ref[0] = o.astype(BF16)
        lse_ref[0] = _unstack(jnp.broadcast_to(m + jnp.log(l), (2 * blk, LANES)), blk)

    tile = pl.BlockSpec((1, blk, LANES), lambda i, h, q: (i, q, h))
    kvspec = pl.BlockSpec((1, s, 2 * LANES), lambda i, h, q: (i, 0, KV_BLK0 + h))
    shape = jax.ShapeDtypeStruct((b, s, FOX_WIDTH), F32)
    return _hosted_call(
        body, ex, name="fox_fwd", out_shape=(shape, jax.ShapeDtypeStruct((b, s, FOX_WIDTH), BF16), shape),
        grid=(b, PAIRS, nq),
        in_specs=[tile, kvspec, tile, pl.BlockSpec((1, 2, nq, 1, blk), lambda i, h, q: (i, h, 0, 0, 0))],
        out_specs=(tile, tile, tile), args=(qkv, qkv, c_exp, c_row))


def _fox_bwd(qkv, c_exp, c_row, lse, o, do, ex=None):
    b, s, _ = qkv.shape
    blk = min(ATT_BLOCK, s)
    nq = s // blk

    def body(q_ref, kv_ref, cc_ref, cr_ref, lse_ref, o_ref, do_ref, dq_ref, dkv_ref, dcq_ref, dc_ref, dk_acc, dv_acc):
        qi = pl.program_id(2)

        @pl.when(qi == 0)
        def _():
            dk_acc[...] = jnp.zeros_like(dk_acc)
            dv_acc[...] = jnp.zeros_like(dv_acc)
            dc_ref[...] = jnp.zeros_like(dc_ref)

        q_all = _stack_heads(q_ref[0] * FOX_SCALE)
        dov = do_ref[0]
        do_all = _stack_heads(dov.astype(BF16))
        delta = jnp.sum(_stack_heads(dov * o_ref[0]), axis=-1, keepdims=True)
        bias = _stack_cols(cc_ref[0]) - _stack_cols(lse_ref[0])

        def step(kb, carry, masked):
            acc, dcq = carry
            rows = pl.ds(pl.multiple_of(kb * blk, blk), blk)
            kblk = kv_ref[0, rows, :LANES]
            p = jnp.exp(_fox_scores(q_all, kblk, bias, cr_ref, kb, masked, blk))
            dp = lax.dot_general(do_all, kv_ref[0, rows, LANES:], NT_DIMS, preferred_element_type=F32)
            ds = p * (dp - delta)
            dsb = ds.astype(BF16)
            dv_acc[rows, :] += lax.dot_general(p.astype(BF16), do_all, TN_DIMS, preferred_element_type=F32)
            dk_acc[rows, :] += lax.dot_general(dsb, q_all, TN_DIMS, preferred_element_type=F32)
            dc_ref[0, 0, kb] -= jnp.sum(ds[:blk], axis=0, keepdims=True)
            dc_ref[0, 1, kb] -= jnp.sum(ds[blk:], axis=0, keepdims=True)
            acc = acc + jnp.dot(dsb, kblk, preferred_element_type=F32)
            return acc, dcq + jnp.sum(ds, axis=-1, keepdims=True)

        init = (jnp.zeros((2 * blk, LANES), F32), jnp.zeros((2 * blk, 1), F32))
        acc, dcq = step(qi, lax.fori_loop(0, qi, functools.partial(step, masked=False), init), True)
        dq_ref[0] = (_unstack(acc, blk) * FOX_SCALE).astype(BF16)
        dcq_ref[0, 0] = jnp.where(lax.broadcasted_iota(jnp.int32, (blk, 2), 1) == 0, dcq[:blk], dcq[blk:])

        @pl.when(qi == nq - 1)
        def _():
            dkv_ref[0, :, :LANES] = dk_acc[...].astype(BF16)
            dkv_ref[0, :, LANES:] = dv_acc[...].astype(BF16)

    tile = pl.BlockSpec((1, blk, LANES), lambda i, h, q: (i, q, h))
    kvspec = pl.BlockSpec((1, s, 2 * LANES), lambda i, h, q: (i, 0, KV_BLK0 + h))
    crow = pl.BlockSpec((1, 2, nq, 1, blk), lambda i, h, q: (i, h, 0, 0, 0))
    return _hosted_call(
        body, ex, name="fox_bwd", grid=(b, PAIRS, nq),
        out_shape=(jax.ShapeDtypeStruct((b, s, FOX_WIDTH), BF16), jax.ShapeDtypeStruct((b, s, 2 * FOX_WIDTH), BF16),
                   jax.ShapeDtypeStruct((b, PAIRS, s, 2), F32), jax.ShapeDtypeStruct(c_row.shape, F32)),
        in_specs=[tile, kvspec, tile, crow, tile, tile, tile],
        out_specs=(tile, pl.BlockSpec((1, s, 2 * LANES), lambda i, h, q: (i, 0, h)),
                   pl.BlockSpec((1, 1, blk, 2), lambda i, h, q: (i, h, q, 0)), crow),
        scratch=[pltpu.VMEM((s, LANES), F32), pltpu.VMEM((s, LANES), F32)],
        args=(qkv, qkv, c_exp, c_row, lse, o, do))


def _sigmoid(z):
    return 1.0 / (1.0 + jnp.exp(-z))


def _branches_mix(y, o, w_pool3, w_fox3, gu, b_gate):
    t = y.shape[0]
    chunk = w_pool3.shape[2]
    per_branch = D_MODEL // chunk
    bm = _block(t, 1024, 16)

    def body(y_ref, o_ref, wp_ref, wf_ref, gp_ref, gf_ref, bp_ref, bf_ref, yp_ref, yf_ref, mix_ref):
        yp = jnp.dot(y_ref[...], wp_ref[...], preferred_element_type=F32).astype(BF16)
        yf = jnp.dot(o_ref[...], wf_ref[...], preferred_element_type=F32).astype(BF16)
        yp_ref[...] = yp
        yf_ref[...] = yf
        gp = _sigmoid(gp_ref[...].astype(F32) + bp_ref[...])
        gf = _sigmoid(gf_ref[...].astype(F32) + bf_ref[...])
        mix_ref[...] = (gp * yp.astype(F32) + gf * yf.astype(F32)).astype(BF16)

    rows = pl.BlockSpec((bm, y.shape[1]), lambda i, j: (i, 0))
    weight = pl.BlockSpec((None, y.shape[1], chunk), lambda i, j: (j, 0, 0))
    tile = lambda base: pl.BlockSpec((bm, chunk), lambda i, j: (i, base + j))
    vec = lambda base: pl.BlockSpec((1, chunk), lambda i, j: (0, base + j))
    shape = jax.ShapeDtypeStruct((t, D_MODEL), BF16)
    return pl.pallas_call(
        body, name="branches_mix", out_shape=(shape, shape, shape), grid=(t // bm, per_branch),
        in_specs=[rows, rows, weight, weight, tile(0), tile(per_branch), vec(0), vec(per_branch)],
        out_specs=(tile(0), tile(0), tile(0)),
        compiler_params=_cparams(("parallel", "arbitrary")),
    )(y, o, w_pool3, w_fox3, gu, gu, b_gate, b_gate)


def _mix_bwd(gu, b_gate, y_pool, y_fox, dx, w_out_t):
    t = gu.shape[0]
    bt = _block(t, 256, 16)

    def body(gp_ref, gf_ref, bp_ref, bf_ref, yp_ref, yf_ref, dx_ref, w_ref, dyp_ref, dyf_ref, dgl_ref, db_ref):
        @pl.when(pl.program_id(0) == 0)
        def _():
            db_ref[...] = jnp.zeros_like(db_ref)

        dm = jnp.dot(dx_ref[...], w_ref[...], preferred_element_type=F32)
        gp = _sigmoid(gp_ref[...].astype(F32) + bp_ref[...])
        gf = _sigmoid(gf_ref[...].astype(F32) + bf_ref[...])
        dyp_ref[...] = (dm * gp).astype(BF16)
        dyf_ref[...] = (dm * gf).astype(BF16)
        dlp = dm * yp_ref[...].astype(F32) * gp * (1.0 - gp)
        dlf = dm * yf_ref[...].astype(F32) * gf * (1.0 - gf)
        dgl_ref[:, :D_MODEL] = dlp.astype(BF16)
        dgl_ref[:, D_MODEL:] = dlf.astype(BF16)
        db_ref[:, :D_MODEL] += jnp.sum(dlp, axis=0, keepdims=True)
        db_ref[:, D_MODEL:] += jnp.sum(dlf, axis=0, keepdims=True)

    col = lambda j: pl.BlockSpec((bt, D_MODEL), lambda i: (i, j))
    vec = lambda j: pl.BlockSpec((1, D_MODEL), lambda i: (0, j))
    wide = pl.BlockSpec((bt, GATE_WIDTH), lambda i: (i, 0))
    return pl.pallas_call(
        body, name="mix_bwd", grid=(t // bt,),
        out_shape=(jax.ShapeDtypeStruct((t, D_MODEL), BF16), jax.ShapeDtypeStruct((t, D_MODEL), BF16),
                   jax.ShapeDtypeStruct((t, GU_COLS), BF16), jax.ShapeDtypeStruct((1, GATE_WIDTH), F32)),
        in_specs=[col(0), col(1), vec(0), vec(1), col(0), col(0), col(0),
                  pl.BlockSpec(w_out_t.shape, lambda i: (0, 0))],
        out_specs=(col(0), col(0), wide, pl.BlockSpec((1, GATE_WIDTH), lambda i: (0, 0))),
        compiler_params=_cparams(("arbitrary",)),
    )(gu, gu, b_gate, b_gate, y_pool, y_fox, dx, w_out_t)


X_SCALE = X_DH ** -0.5


def _xattn_probs(qh, kh):
    s = lax.dot_general(qh, kh, NT_DIMS, preferred_element_type=F32) * X_SCALE
    e = jnp.exp(s - jnp.max(s, axis=-1, keepdims=True))
    return e / jnp.sum(e, axis=-1, keepdims=True)


def _xattn_fwd(q, kv):
    b, s, _ = q.shape
    m = kv.shape[1]
    bq = _block(s, 512, 16)

    def body(q_ref, kv_ref, o_ref):
        for h in range(X_HEADS):
            cols = slice(h * X_DH, (h + 1) * X_DH)
            p = _xattn_probs(q_ref[0, :, cols], kv_ref[0, :, cols])
            vh = kv_ref[0, :, X_WIDTH + h * X_DH:X_WIDTH + (h + 1) * X_DH]
            o_ref[0, :, cols] = jnp.dot(p.astype(BF16), vh, preferred_element_type=F32).astype(BF16)

    return pl.pallas_call(
        body, name="xattn_fwd", out_shape=jax.ShapeDtypeStruct((b, s, X_WIDTH), BF16), grid=(b, s // bq),
        in_specs=[pl.BlockSpec((1, bq, X_WIDTH), lambda i, j: (i, j, 0)),
                  pl.BlockSpec((1, m, 2 * X_WIDTH), lambda i, j: (i, 0, 0))],
        out_specs=pl.BlockSpec((1, bq, X_WIDTH), lambda i, j: (i, j, 0)),
        compiler_params=_cparams(("parallel", "parallel")),
    )(q, kv)


def _xattn_bwd(q, kv, dx, w_o_t):
    b, s, _ = q.shape
    m = kv.shape[1]
    bq = _block(s, 512, 16)

    def body(q_ref, kv_ref, dx_ref, w_ref, dq_ref, dkv_ref):
        @pl.when(pl.program_id(1) == 0)
        def _():
            dkv_ref[...] = jnp.zeros_like(dkv_ref)

        do = jnp.dot(dx_ref[0], w_ref[...], preferred_element_type=F32).astype(BF16)
        for h in range(X_HEADS):
            cols = slice(h * X_DH, (h + 1) * X_DH)
            vcols = slice(X_WIDTH + h * X_DH, X_WIDTH + (h + 1) * X_DH)
            qh, kh, vh, doh = q_ref[0, :, cols], kv_ref[0, :, cols], kv_ref[0, :, vcols], do[:, cols]
            p = _xattn_probs(qh, kh)
            dkv_ref[0, :, vcols] += lax.dot_general(p.astype(BF16), doh, TN_DIMS, preferred_element_type=F32)
            dp = lax.dot_general(doh, vh, NT_DIMS, preferred_element_type=F32)
            ds = (p * (dp - jnp.sum(p * dp, axis=-1, keepdims=True)) * X_SCALE).astype(BF16)
            dq_ref[0, :, cols] = jnp.dot(ds, kh, preferred_element_type=F32).astype(BF16)
            dkv_ref[0, :, cols] += lax.dot_general(ds, qh, TN_DIMS, preferred_element_type=F32)

    tile = pl.BlockSpec((1, bq, X_WIDTH), lambda i, j: (i, j, 0))
    mem = pl.BlockSpec((1, m, 2 * X_WIDTH), lambda i, j: (i, 0, 0))
    return pl.pallas_call(
        body, name="xattn_bwd", grid=(b, s // bq),
        out_shape=(jax.ShapeDtypeStruct((b, s, X_WIDTH), BF16), jax.ShapeDtypeStruct((b, m, 2 * X_WIDTH), F32)),
        in_specs=[tile, mem, pl.BlockSpec((1, bq, dx.shape[2]), lambda i, j: (i, j, 0)),
                  pl.BlockSpec(w_o_t.shape, lambda i, j: (0, 0))],
        out_specs=(tile, mem),
        compiler_params=_cparams(("parallel", "arbitrary")),
    )(q, kv, dx, w_o_t)


def _ffn_in(hf, w3):
    t, d = hf.shape
    chunk = w3.shape[2]
    half = N_CHIPS // 2
    bm = _block(t, 1024, 16)

    def body(a_ref, wg_ref, wu_ref, gt_ref, up_ref, act_ref):
        a = a_ref[...]
        gt = jnp.dot(a, wg_ref[...], preferred_element_type=F32).astype(BF16)
        up = jnp.dot(a, wu_ref[...], preferred_element_type=F32).astype(BF16)
        gt_ref[...] = gt
        up_ref[...] = up
        g32 = gt.astype(F32)
        act_ref[...] = (g32 * _sigmoid(g32) * up.astype(F32)).astype(BF16)

    tile = pl.BlockSpec((bm, chunk), lambda i, j: (i, j))
    shape = jax.ShapeDtypeStruct((t, half * chunk), BF16)
    return pl.pallas_call(
        body, name="ffn_in", out_shape=(shape, shape, shape), grid=(t // bm, half),
        in_specs=[pl.BlockSpec((bm, d), lambda i, j: (i, 0)),
                  pl.BlockSpec((None, d, chunk), lambda i, j: (j, 0, 0)),
                  pl.BlockSpec((None, d, chunk), lambda i, j: (j + half, 0, 0))],
        out_specs=(tile, tile, tile),
        compiler_params=_cparams(("parallel", "arbitrary")),
    )(hf, w3, w3)


def _ffn_act_bwd(dx, w_out_t, gate, up):
    t, d = dx.shape
    bt = _block(t, 256, 16)

    def body(dx_ref, w_ref, gt_ref, up_ref, o_ref):
        da = jnp.dot(dx_ref[...], w_ref[...], preferred_element_type=F32).astype(BF16).astype(F32)
        gt = gt_ref[...].astype(F32)
        sg = _sigmoid(gt)
        silu = gt * sg
        o_ref[:, :D_FF] = (da * up_ref[...].astype(F32) * (sg + silu * (1.0 - sg))).astype(BF16)
        o_ref[:, D_FF:] = (da * silu).astype(BF16)

    col = pl.BlockSpec((bt, D_FF), lambda i: (i, 0))
    return pl.pallas_call(
        body, name="ffn_act_bwd", out_shape=jax.ShapeDtypeStruct((t, 2 * D_FF), BF16), grid=(t // bt,),
        in_specs=[pl.BlockSpec((bt, d), lambda i: (i, 0)), pl.BlockSpec((d, D_FF), lambda i: (0, 0)), col, col],
        out_specs=pl.BlockSpec((bt, 2 * D_FF), lambda i: (i, 0)),
        compiler_params=_cparams(("parallel",)),
    )(dx, w_out_t, gate, up)


def _stack_of(w, axis):
    r, c = w.shape
    if axis == 0:
        return w.reshape(N_CHIPS, r // N_CHIPS, c)
    return w.reshape(r, N_CHIPS, c // N_CHIPS).transpose(1, 0, 2)


def _stack_t(w3):
    n, r, c = w3.shape
    return w3.transpose(0, 2, 1).reshape(n * c, r)


def _pair_rows(k, v):
    c = k.shape[1]
    return jnp.stack([k.reshape(PAIRS, LANES, c), v.reshape(PAIRS, LANES, c)], axis=1).reshape(2 * FOX_WIDTH, c)


def _unpair_rows(kv):
    c = kv.shape[1]
    kv = kv.reshape(PAIRS, 2, LANES, c)
    return kv[:, 0].reshape(FOX_WIDTH, c), kv[:, 1].reshape(FOX_WIDTH, c)


def _input_grad(parts, weights_t, ex):
    t = parts[0].shape[0]
    d = weights_t[0].shape[1]
    bm = _block(t, 512, 16)
    n = len(parts)

    def body(*refs):
        acc = None
        for a_ref, b_ref in zip(refs[:n], refs[n:2 * n]):
            term = jnp.dot(a_ref[...], b_ref[...], preferred_element_type=F32)
            acc = term if acc is None else acc + term
        refs[2 * n][...] = acc

    (out,), moved = _hosted_call(
        body, ex, name="d_h", grid=(t // bm,), out_shape=(jax.ShapeDtypeStruct((t, d), F32),),
        in_specs=[pl.BlockSpec((bm, p.shape[1]), lambda i: (i, 0)) for p in parts]
        + [pl.BlockSpec(w.shape, lambda i: (0, 0)) for w in weights_t],
        out_specs=(pl.BlockSpec((bm, d), lambda i: (i, 0)),), args=tuple(parts) + tuple(weights_t))
    return out, moved


def _step(x, mem, loss_target, weights, moments_m, moments_v):
    nb, s, d = x.shape
    n_mem = mem.shape[1]
    t = nb * s
    blk = min(ATT_BLOCK, s)
    x2 = x.reshape(t, d)
    mem2 = mem.reshape(nb * n_mem, d)
    tgt2 = loss_target.reshape(t, d)

    def shard2d(a, n):
        a = a.reshape(a.shape[1:])
        return a.T if n == "w_in" else a

    def unshard(a, n):
        return (a.T if n == "w_in" else a)[None]

    local = {n: shard2d(weights[n], n) for n, _, _ in SHARDED}

    names = [n for n, _, _ in SHARDED]
    last = ["w_ffn_out"]
    later = [n for n in names if n != "w_in" and n not in last]
    local_b = {n: local[n].astype(BF16) for n in names}
    g_mix = weights["norm_mix_g"]
    h, w_in_others = _rms_fwd(x2, g_mix, "norm_mix", ex=_gather_exchange([local_b["w_in"]]))
    w_in_stack, = _place_own(w_in_others, [local_b["w_in"]])

    def w_in_rows(lo, hi):
        per = IN_COLS // N_CHIPS
        parts = [w_in_stack[j, max(lo, j * per) - j * per:min(hi, (j + 1) * per) - j * per]
                 for j in range(N_CHIPS) if max(lo, j * per) < min(hi, (j + 1) * per)]
        return parts[0] if len(parts) == 1 else jnp.concatenate(parts)

    w_gu_t = jnp.concatenate([w_in_rows(2056, IN_COLS), w_in_rows(0, 512)])
    w_qkv_t = jnp.concatenate([w_in_rows(512, 1024), _pair_rows(w_in_rows(1024, 1536), w_in_rows(1536, 2048))])
    w_f_t = jnp.pad(w_in_rows(2048, 2056), ((0, LANES - FOX_HEADS), (0, 0)))
    w_gu, w_qkv, w_f = w_gu_t.T, w_qkv_t.T, w_f_t.T
    w_f_exp = jnp.repeat(w_f[:, :FOX_HEADS], FOX_DH, axis=1)

    g_mix, g_x, g_mem, g_ffn = (weights[n] for n in ("norm_mix_g", "norm_x_g", "norm_mem_g", "norm_ffn_g"))
    g_final = weights["norm_final_g"].reshape(1, d)
    pool_w = weights["pool_w"].reshape(4, POOL_GC, POOL_GC)
    pool_scale, b_gate = weights["pool_scale"], weights["b_gate"]
    b_f_pad = jnp.pad(weights["b_forget"], ((0, 0), (0, LANES - FOX_HEADS)))
    b_f_exp = jnp.repeat(weights["b_forget"], FOX_DH, axis=1)

    gu, last_others = _mm(h, w_gu, out_dtype=BF16, bn=512, name="in_proj_gates_pool",
                          ex=_gather_exchange([local_b[n] for n in last]))
    qkv, f_exp, f_pad = _in_proj_attn(h, w_qkv, w_f_exp, w_f)
    gu3, qkv3 = gu.reshape(nb, s, GU_COLS), qkv.reshape(nb, s, 3 * FOX_WIDTH)
    y = _pool_fwd(gu3, pool_w, pool_scale)
    c_exp = _forget_cumsum(f_exp.reshape(nb, s, FOX_WIDTH), b_f_exp, "forget_cumsum_lanes")
    c_pad = _forget_cumsum(f_pad.reshape(nb, s, LANES), b_f_pad, "forget_cumsum")
    c_row = c_pad[:, :, :FOX_HEADS].transpose(0, 2, 1).reshape(nb, FOX_HEADS, s // blk, 1, blk)
    (o, o_b, lse), gathered = _fox_fwd(qkv3, c_exp, c_row, ex=_gather_exchange([local_b[n] for n in later]))
    stacks = dict(zip(later, _place_own(gathered, [local_b[n] for n in later])))
    stacks.update(zip(last, _place_own(last_others, [local_b[n] for n in last])))
    w_pool_out3, w_fox_out3, w_xo3, w_ffn_in3 = (stacks[n] for n in ("w_pool_out", "w_fox_out", "w_xo", "w_ffn_in"))
    w_out, w_xq, w_xkv, w_ffn_out = (stacks[n].reshape(-1, stacks[n].shape[2])
                                     for n in ("w_out", "w_xq", "w_xkv", "w_ffn_out"))
    y2, o2 = y.reshape(t, POOL_WIDTH), o_b.reshape(t, FOX_WIDTH)
    y_pool, y_fox, mix = _branches_mix(y2, o2, w_pool_out3, w_fox_out3, gu, b_gate)
    x1, hx = _mm_res_norm(mix, w_out, x2, g_x, "mix_out_norm_x")
    mem_n = _rms_fwd(mem2, g_mem, "norm_mem")
    qx = _mm(hx, w_xq, out_dtype=BF16, name="x_q")
    kv = _mm(mem_n, w_xkv, out_dtype=BF16, name="x_kv")
    qx3, kv3 = qx.reshape(nb, s, X_WIDTH), kv.reshape(nb, n_mem, 2 * X_WIDTH)
    ox = _xattn_fwd(qx3, kv3).reshape(t, X_WIDTH)
    w_xo = w_xo3.transpose(1, 0, 2).reshape(X_WIDTH, D_MODEL)
    x2_, hf = _mm_res_norm(ox, w_xo, x1, g_ffn, "x_out_norm_ffn")
    ffn_gate, ffn_up, act = _ffn_in(hf, w_ffn_in3)

    dx3, dx3_b, dg_final, loss_part = _ffn_out_loss(act, w_ffn_out, x2_, tgt2, g_final)
    dw_ffn_out = _mm(act, dx3_b, ta=True, bm=1408, bn=512, bk=2048, name="d_w_ffn_out")
    dffn = _ffn_act_bwd(dx3_b, w_ffn_out.T, ffn_gate, ffn_up)
    dw_ffn_in = _mm(hf, dffn, ta=True, bm=512, bn=1408, bk=2048, out_stack=True, name="d_w_ffn_in")
    core = lax.axis_index("c").astype(jnp.int32).reshape(1)
    ffn_group = ["w_ffn_in", "w_ffn_out"]
    mid_group = ["w_pool_out", "w_fox_out", "w_out", "w_xq", "w_xkv", "w_xo"]
    grad_stacks = {"w_ffn_in": dw_ffn_in, "w_ffn_out": _stack_of(dw_ffn_out, 0)}

    def presum(group, theirs):
        return [_sum_halves(grad_stacks[n], t_, core, "sum_halves_" + n) for n, t_ in zip(group, theirs)]

    (dx2, dx2_b, dg_ffn), theirs = _mm_norm_bwd(dffn, _stack_t(w_ffn_in3), x2_, g_ffn, dx3, "d_hf_norm_ffn_bwd",
                                                ex=_swap_exchange([grad_stacks[n] for n in ffn_group]))
    chip_sums = dict(zip(ffn_group, presum(ffn_group, theirs)))

    dw_xo = _mm(ox, dx2_b, ta=True, bn=256, out_stack=True, name="d_w_xo")
    dqx, dkv = _xattn_bwd(qx3, kv3, dx2_b.reshape(nb, s, d), _stack_t(w_xo3))
    dqx2, dkv2 = dqx.reshape(t, X_WIDTH), dkv.reshape(nb * n_mem, 2 * X_WIDTH)
    dw_xkv = _mm(mem_n, dkv2, ta=True, name="d_w_xkv")
    dmem_n = _mm(dkv2, w_xkv.T, name="d_mem_n")
    dg_mem = _rms_bwd(dmem_n, mem2, g_mem, None, "norm_mem_bwd")
    dw_xq = _mm(hx, dqx2, ta=True, name="d_w_xq")
    dx1, dx1_b, dg_x = _mm_norm_bwd(dqx2, w_xq.T, x1, g_x, dx2, "d_hx_norm_x_bwd")

    dw_out = _mm(mix, dx1_b, ta=True, name="d_w_out")
    dyp, dyf, dgu, db_gate = _mix_bwd(gu, b_gate, y_pool, y_fox, dx1_b, w_out.T)
    dw_pool_out = _mm(y2, dyp, ta=True, bn=256, out_stack=True, name="d_w_pool_out")
    dw_fox_out = _mm(o2, dyf, ta=True, bn=256, out_stack=True, name="d_w_fox_out")
    do = _mm(dyf, _stack_t(w_fox_out3), name="d_o").reshape(nb, s, FOX_WIDTH)
    dgu3, dpool_w, dpool_scale = _pool_bwd(gu3, dyp.reshape(nb, s, d), _stack_t(w_pool_out3), pool_w, pool_scale,
                                           dgu.reshape(nb, s, GU_COLS))
    grad_stacks.update({"w_pool_out": dw_pool_out, "w_fox_out": dw_fox_out, "w_out": _stack_of(dw_out, 0),
                        "w_xq": _stack_of(dw_xq, 0), "w_xkv": _stack_of(dw_xkv, 0), "w_xo": dw_xo})
    dgu2 = dgu3.reshape(t, GU_COLS)
    dw_gu_t, theirs = _mm(dgu2, h, ta=True, name="d_w_gates_pool",
                          ex=_swap_exchange([grad_stacks[n] for n in mid_group]))
    chip_sums.update(zip(mid_group, presum(mid_group, theirs)))
    early = ffn_group + mid_group
    (dq3, dkv3, dc_q, dc_row), early_slots = _fox_bwd(qkv3, c_exp, c_row, lse, o, do,
                                                      ex=_chips_exchange([chip_sums[n] for n in early]))
    slots = dict(zip(early, early_slots))
    dc = dc_row.reshape(nb, FOX_HEADS, s).transpose(0, 2, 1) + dc_q.transpose(0, 2, 1, 3).reshape(nb, s, FOX_HEADS)
    dc = jnp.pad(dc, ((0, 0), (0, 0), (0, LANES - FOX_HEADS)))
    df, db_f = _forget_bwd(dc, f_pad.reshape(nb, s, LANES), b_f_pad)
    dq2, dkv2, df2 = dq3.reshape(t, FOX_WIDTH), dkv3.reshape(t, 2 * FOX_WIDTH), df.reshape(t, LANES)
    dw_q_t = _mm(dq2, h, ta=True, name="d_w_q")
    dw_kv_t = _mm(dkv2, h, ta=True, name="d_w_kv")
    dw_f_t = _mm(df2, h, ta=True, name="d_w_forget")
    dw_k_t, dw_v_t = _unpair_rows(dw_kv_t)
    dw_in_t = jnp.concatenate([dw_gu_t[GATE_WIDTH:], dw_q_t, dw_k_t, dw_v_t, dw_f_t[:FOX_HEADS],
                               dw_gu_t[:GATE_WIDTH]])
    grad_stacks["w_in"] = dw_in_t.reshape(N_CHIPS, IN_COLS // N_CHIPS, D_MODEL)
    chip_sums["w_in"], = presum(["w_in"], _run_exchange(_swap_exchange([grad_stacks["w_in"]]), "swap_halves_w_in"))
    dh, (slots["w_in"],) = _input_grad([dgu2, dq2, dkv2, df2],
                                       [w_gu_t, w_qkv_t[:FOX_WIDTH], w_qkv_t[FOX_WIDTH:], w_f_t],
                                       _chips_exchange([chip_sums["w_in"]]))

    place = jnp.stack([lax.axis_index("c"), 2 * lax.axis_index("x") + lax.axis_index("y")]).astype(jnp.int32)
    halves = [_sum_chips(slots[n], chip_sums[n], place, _by_rows(local[n].shape[0]), "sum_chips_" + n) for n in names]
    (dx, _, dg_mix), reduced = _rms_bwd(dh, x2, g_mix, dx1, "norm_mix_bwd", ex=_join_exchange(halves))

    small_grads = {"norm_mix_g": dg_mix, "b_forget": db_f[:, :FOX_HEADS], "b_gate": db_gate, "pool_w": dpool_w,
                   "pool_scale": dpool_scale, "norm_x_g": dg_x, "norm_mem_g": dg_mem, "norm_ffn_g": dg_ffn,
                   "norm_final_g": dg_final}
    def flat2d(a):
        return a.reshape(-1, a.shape[-1])

    small_names = [n for n, _ in SMALL]
    own = [flat2d(small_grads[n]) for n in small_names]
    small_gather = _small_exchange(own + [loss_part])

    def tiles_of(a):
        return a.transpose(2, 0, 1)

    def block_of(a3):
        return a3.transpose(1, 2, 0)

    grads, deltas, new_m, new_v = {}, {}, {}, {}
    gathered = None
    for n, g_ in zip(names, reduced):
        if n == "w_in":
            g_ = lax.optimization_barrier(g_.reshape(IN_COLS // N_CHIPS, 1, D_MODEL))
            (d_, m_, v_), gathered = _adamw(tiles_of(weights[n]), g_, tiles_of(moments_m[n]), tiles_of(moments_v[n]),
                                            "adamw_" + n, ex=small_gather)
            back = block_of
        else:
            d_, m_, v_ = _adamw(local[n], g_, shard2d(moments_m[n], n), shard2d(moments_v[n], n), "adamw_" + n)
            back = functools.partial(unshard, n=n)
        grads[n], deltas[n], new_m[n], new_v[n] = (back(a) for a in (g_, d_, m_, v_))

    device = (4 * lax.axis_index("x") + 2 * lax.axis_index("y") + lax.axis_index("c")).astype(jnp.int32).reshape(1)
    sg, sd, sm, sv, loss_sum = _adamw_small(
        gathered[:-1], own, [flat2d(weights[n]) for n in small_names], [flat2d(moments_m[n]) for n in small_names],
        [flat2d(moments_v[n]) for n in small_names], gathered[-1], loss_part, device)
    for n, g_, d_, m_, v_ in zip(small_names, sg, sd, sm, sv):
        grads[n], deltas[n], new_m[n], new_v[n] = (a.reshape(weights[n].shape) for a in (g_, d_, m_, v_))
    return loss_sum[0, 0], dx.reshape(nb, s, d), grads, deltas, new_m, new_v


def kernel(x, mem, norm_mix_g, w_in, b_forget, b_gate, pool_w, pool_scale, w_pool_out, w_fox_out, w_out, norm_x_g, norm_mem_g, w_xq, w_xkv, w_xo, norm_ffn_g, w_ffn_in, w_ffn_out, norm_final_g, loss_target, m_norm_mix_g, m_w_in, m_b_forget, m_b_gate, m_pool_w, m_pool_scale, m_w_pool_out, m_w_fox_out, m_w_out, m_norm_x_g, m_norm_mem_g, m_w_xq, m_w_xkv, m_w_xo, m_norm_ffn_g, m_w_ffn_in, m_w_ffn_out, m_norm_final_g, v_norm_mix_g, v_w_in, v_b_forget, v_b_gate, v_pool_w, v_pool_scale, v_w_pool_out, v_w_fox_out, v_w_out, v_norm_x_g, v_norm_mem_g, v_w_xq, v_w_xkv, v_w_xo, v_norm_ffn_g, v_w_ffn_in, v_w_ffn_out, v_norm_final_g):
    given = dict(locals())
    weights = {n: given[n] for n in WEIGHT_ORDER}
    moments_m = {n: given["m_" + n] for n in WEIGHT_ORDER}
    moments_v = {n: given["v_" + n] for n in WEIGHT_ORDER}
    loss, grad_x, grads, deltas, new_m, new_v = _step(x, mem, loss_target, weights, moments_m, moments_v)
    return (loss, grad_x, *[grads[n] for n in WEIGHT_ORDER], *[deltas[n] for n in WEIGHT_ORDER],
            *[new_m[n] for n in WEIGHT_ORDER], *[new_v[n] for n in WEIGHT_ORDER])
```

```python
import functools
import math

import jax
import jax.numpy as jnp
from jax import lax
from jax.experimental import pallas as pl
from jax.experimental.pallas import tpu as pltpu

F32 = jnp.float32
BF16 = jnp.bfloat16
MESH = pl.DeviceIdType.MESH

D_MODEL = 1024
EPS = 1e-6
POOL_WINDOWS = (2, 4, 8, 16)
POOL_WIDTH = 512
POOL_GC = 128
FOX_HEADS = 8
FOX_DH = 64
FOX_WIDTH = 512
X_HEADS = 4
X_DH = 128
X_WIDTH = 512
D_FF = 2816
IN_COLS = 4104
GATE_WIDTH = 2048
ADAM_LR = 0.001
ADAM_B1 = 0.9
ADAM_B2 = 0.999
ADAM_EPS = 1e-08
ADAM_WD = 0.01
ADAM_STEP = 10

N_CHIPS = 4
N_DEV = 8
LANES = 128
VMEM_LIMIT_BYTES = 56 * 1024 * 1024
NEG_INF = -1e30
ATT_BLOCK = 512

SHARDED = (
    ("w_in", (1024, IN_COLS), 1),
    ("w_pool_out", (POOL_WIDTH, 1024), 1),
    ("w_fox_out", (FOX_WIDTH, 1024), 1),
    ("w_out", (1024, 1024), 0),
    ("w_xq", (1024, X_WIDTH), 0),
    ("w_xkv", (1024, 2 * X_WIDTH), 0),
    ("w_xo", (X_WIDTH, 1024), 1),
    ("w_ffn_in", (1024, 2 * D_FF), 1),
    ("w_ffn_out", (D_FF, 1024), 0),
)
SMALL = (
    ("norm_mix_g", (1, 1024)),
    ("b_forget", (1, 8)),
    ("b_gate", (1, 2048)),
    ("pool_w", (1, 4, 128, 128)),
    ("pool_scale", (1, 512)),
    ("norm_x_g", (1, 1024)),
    ("norm_mem_g", (1, 1024)),
    ("norm_ffn_g", (1, 1024)),
    ("norm_final_g", (1024,)),
)
WEIGHT_ORDER = ("norm_mix_g", "w_in", "b_forget", "b_gate", "pool_w", "pool_scale", "w_pool_out", "w_fox_out", "w_out",
                "norm_x_g", "norm_mem_g", "w_xq", "w_xkv", "w_xo", "norm_ffn_g", "w_ffn_in", "w_ffn_out", "norm_final_g")


def _cparams(sem=None):
    return pltpu.CompilerParams(dimension_semantics=sem, vmem_limit_bytes=VMEM_LIMIT_BYTES)


def _block(dim, pref, unit):
    if dim <= pref:
        return dim
    best = None
    for b in range(unit, pref + 1, unit):
        if dim % b == 0:
            best = b
    assert best is not None, (dim, pref, unit)
    return best


def _rows_block(rows, cols, unit=16, elems=1 << 19):
    return _block(rows, max(unit, elems // cols // unit * unit), unit)


def _my_place():
    return lax.axis_index("x"), lax.axis_index("y"), lax.axis_index("c")


def _other_chips(x, y):
    return [(1 - x, y), (x, 1 - y), (1 - x, 1 - y)]


def _chip(place):
    return 2 * place[0] + place[1]


ANY = pl.BlockSpec(memory_space=pl.ANY)


def _by_rows(rows):
    return rows % 32 == 0


def _half_shape(rows, cols):
    return (rows // 2, cols) if _by_rows(rows) else (rows, cols // 2)


def _core_half(ref, core, lead=()):
    rows, cols = ref.shape[-2:]
    if _by_rows(rows):
        return ref.at[(*lead, pl.ds(core * (rows // 2), rows // 2), slice(None))]
    return ref.at[(*lead, slice(None), pl.ds(core * (cols // 2), cols // 2))]


class _Exchange:
    def __init__(self, arrays, out_shapes, n_sems, start, finish, in_place=False):
        self.arrays, self.out_shapes, self.n_sems, self.start, self.finish = arrays, out_shapes, n_sems, start, finish
        self.in_place = in_place

    def scratch(self):
        return [pltpu.SemaphoreType.DMA((self.n_sems,)), pltpu.SemaphoreType.DMA((self.n_sems,))]

    def aliases(self, first_in, first_out):
        return {first_in + k: first_out + k for k in range(len(self.arrays))} if self.in_place else {}


def _run_exchange(ex, name):
    n = len(ex.arrays)

    def body(*refs):
        ins, outs, sems = refs[:n], refs[n:2 * n], refs[2 * n:]
        ex.start(ins, outs, *sems)
        ex.finish(ins, outs, *sems)

    return pl.pallas_call(
        body, name=name, out_shape=ex.out_shapes, in_specs=[ANY] * n, out_specs=[ANY] * n, scratch_shapes=ex.scratch(),
        input_output_aliases=ex.aliases(0, 0),
    )(*ex.arrays)


def _hosted_call(body, ex, *, name, grid, in_specs, out_specs, out_shape, args, scratch=()):
    n_in, n_out, n_scr = len(args), len(out_shape), len(scratch)
    if ex is None:
        outs = pl.pallas_call(
            body, name=name, grid=grid, out_shape=out_shape, in_specs=in_specs, out_specs=out_specs,
            scratch_shapes=list(scratch), compiler_params=_cparams(("arbitrary",) * len(grid)))(*args)
        return outs, None
    nc = len(ex.arrays)

    def full_body(*refs):
        ins, cins = refs[:n_in], refs[n_in:n_in + nc]
        outs, couts = refs[n_in + nc:n_in + nc + n_out], refs[n_in + nc + n_out:n_in + 2 * nc + n_out]
        rest = refs[n_in + 2 * nc + n_out:]
        scr, sems = rest[:n_scr], rest[n_scr:]
        first = functools.reduce(jnp.logical_and, [pl.program_id(a) == 0 for a in range(len(grid))])
        last = functools.reduce(jnp.logical_and, [pl.program_id(a) == grid[a] - 1 for a in range(len(grid))])

        @pl.when(first)
        def _():
            ex.start(cins, couts, *sems)

        body(*ins, *outs, *scr)

        @pl.when(last)
        def _():
            ex.finish(cins, couts, *sems)

    outs = pl.pallas_call(
        full_body, name=name, grid=grid, out_shape=list(out_shape) + list(ex.out_shapes),
        in_specs=list(in_specs) + [ANY] * nc, out_specs=list(out_specs) + [ANY] * nc,
        scratch_shapes=list(scratch) + ex.scratch(), input_output_aliases=ex.aliases(n_in, n_out),
        compiler_params=_cparams(("arbitrary",) * len(grid)))(*args, *ex.arrays)
    return outs[:n_out], outs[n_out:]


def _gather_exchange(shards):
    n = len(shards)

    def copies(ins, outs, send_sems, recv_sems):
        x, y, c = _my_place()

        def half(k, chip, core):
            return _core_half(outs[k], core, lead=(_chip(chip),))

        def copy(k, slot, chip, core, to, src=None):
            return pltpu.make_async_remote_copy(
                src_ref=half(k, chip, core) if src is None else src, dst_ref=half(k, chip, core),
                send_sem=send_sems.at[6 * k + slot], recv_sem=recv_sems.at[6 * k + slot],
                device_id=to, device_id_type=MESH)

        return (x, y, c), copy

    def first_copies(ins, outs, send_sems, recv_sems):
        (x, y, c), copy = copies(ins, outs, send_sems, recv_sems)
        out = []
        for j, chip in enumerate(_other_chips(x, y)):
            for k in range(n):
                out.append(copy(k, j, (x, y), c, (*chip, c), src=_core_half(ins[k], c)))
        return out

    def start(ins, outs, send_sems, recv_sems):
        for cp in first_copies(ins, outs, send_sems, recv_sems):
            cp.start()

    def finish(ins, outs, send_sems, recv_sems):
        (x, y, c), copy = copies(ins, outs, send_sems, recv_sems)
        chips = _other_chips(x, y)
        passed = []
        for j, chip in enumerate(chips):
            for k in range(n):
                copy(k, j, chip, c, (x, y, c)).wait_recv()
                passed.append(copy(k, 3 + j, chip, c, (x, y, 1 - c)))
                passed[-1].start()
        for j, chip in enumerate(chips):
            for k in range(n):
                copy(k, 3 + j, chip, 1 - c, (x, y, c)).wait_recv()
        for cp in first_copies(ins, outs, send_sems, recv_sems) + passed:
            cp.wait_send()

    return _Exchange(list(shards), [jax.ShapeDtypeStruct((N_CHIPS,) + s.shape, s.dtype) for s in shards], 6 * n,
                     start, finish)


def _place_own(stacks, shards):
    me = 2 * lax.axis_index("x") + lax.axis_index("y")
    return [lax.dynamic_update_slice(others, mine[None], (me, 0, 0)) for others, mine in zip(stacks, shards)]


def _swap_exchange(grads):
    n = len(grads)

    def copies(ins, outs, send_sems, recv_sems):
        x, y, c = _my_place()
        return [pltpu.make_async_remote_copy(
            src_ref=_core_half(ins[k], 1 - c, lead=(slice(None),)), dst_ref=outs[k],
            send_sem=send_sems.at[k], recv_sem=recv_sems.at[k], device_id=(x, y, 1 - c), device_id_type=MESH)
            for k in range(n)]

    def start(ins, outs, send_sems, recv_sems):
        for cp in copies(ins, outs, send_sems, recv_sems):
            cp.start()

    def finish(ins, outs, send_sems, recv_sems):
        for cp in copies(ins, outs, send_sems, recv_sems):
            cp.wait()

    return _Exchange(list(grads), [jax.ShapeDtypeStruct((N_CHIPS,) + _half_shape(*g.shape[1:]), g.dtype) for g in grads],
                     n, start, finish)


def _chips_exchange(sums):
    n = len(sums)

    def sends(ins, outs, send_sems, recv_sems):
        x, y, c = _my_place()
        return [pltpu.make_async_remote_copy(
            src_ref=ins[k].at[_chip(chip)], dst_ref=outs[k].at[_chip((x, y))],
            send_sem=send_sems.at[3 * k + j], recv_sem=recv_sems.at[3 * k + j],
            device_id=(*chip, c), device_id_type=MESH)
            for j, chip in enumerate(_other_chips(x, y)) for k in range(n)]

    def start(ins, outs, send_sems, recv_sems):
        for cp in sends(ins, outs, send_sems, recv_sems):
            cp.start()

    def finish(ins, outs, send_sems, recv_sems):
        x, y, c = _my_place()
        for j, chip in enumerate(_other_chips(x, y)):
            for k in range(n):
                slot = outs[k].at[_chip(chip)]
                pltpu.make_async_remote_copy(
                    src_ref=slot, dst_ref=slot, send_sem=send_sems.at[3 * k + j], recv_sem=recv_sems.at[3 * k + j],
                    device_id=(x, y, c), device_id_type=MESH).wait_recv()
        for cp in sends(ins, outs, send_sems, recv_sems):
            cp.wait_send()

    return _Exchange(list(sums), [jax.ShapeDtypeStruct(s.shape, s.dtype) for s in sums], 3 * n, start, finish)


def _join_exchange(shards):
    n = len(shards)

    def sends(ins, outs, send_sems, recv_sems):
        x, y, c = _my_place()
        return [pltpu.make_async_remote_copy(
            src_ref=_core_half(ins[k], c), dst_ref=_core_half(outs[k], c),
            send_sem=send_sems.at[k], recv_sem=recv_sems.at[k], device_id=(x, y, 1 - c), device_id_type=MESH)
            for k in range(n)]

    def start(ins, outs, send_sems, recv_sems):
        for cp in sends(ins, outs, send_sems, recv_sems):
            cp.start()

    def finish(ins, outs, send_sems, recv_sems):
        x, y, c = _my_place()
        for k in range(n):
            theirs = _core_half(outs[k], 1 - c)
            pltpu.make_async_remote_copy(
                src_ref=theirs, dst_ref=theirs, send_sem=send_sems.at[k], recv_sem=recv_sems.at[k],
                device_id=(x, y, c), device_id_type=MESH).wait_recv()
        for cp in sends(ins, outs, send_sems, recv_sems):
            cp.wait_send()

    return _Exchange(list(shards), [jax.ShapeDtypeStruct(s.shape, s.dtype) for s in shards], n, start, finish,
                     in_place=True)


def _small_exchange(blocks):
    n = len(blocks)

    def copies(ins, outs, send_sems, recv_sems):
        x, y, c = _my_place()

        def copy(k, j, whose, to, src=None):
            slot = outs[k].at[4 * whose[0] + 2 * whose[1] + whose[2]]
            return pltpu.make_async_remote_copy(
                src_ref=slot if src is None else src, dst_ref=slot,
                send_sem=send_sems.at[7 * k + j], recv_sem=recv_sems.at[7 * k + j], device_id=to, device_id_type=MESH)

        return (x, y, c), copy

    def first_copies(ins, outs, send_sems, recv_sems):
        (x, y, c), copy = copies(ins, outs, send_sems, recv_sems)
        out = []
        for k in range(n):
            out.append(copy(k, 0, (x, y, c), (x, y, 1 - c), src=ins[k]))
            out += [copy(k, 1 + j, (x, y, c), (*chip, c), src=ins[k]) for j, chip in enumerate(_other_chips(x, y))]
        return out

    def start(ins, outs, send_sems, recv_sems):
        for cp in first_copies(ins, outs, send_sems, recv_sems):
            cp.start()

    def finish(ins, outs, send_sems, recv_sems):
        (x, y, c), copy = copies(ins, outs, send_sems, recv_sems)
        chips = _other_chips(x, y)
        passed = []
        for j, chip in enumerate(chips):
            for k in range(n):
                copy(k, 1 + j, (*chip, c), (x, y, c)).wait_recv()
                passed.append(copy(k, 4 + j, (*chip, c), (x, y, 1 - c)))
                passed[-1].start()
        for k in range(n):
            copy(k, 0, (x, y, 1 - c), (x, y, c)).wait_recv()
        for j, chip in enumerate(chips):
            for k in range(n):
                copy(k, 4 + j, (*chip, 1 - c), (x, y, c)).wait_recv()
        for cp in first_copies(ins, outs, send_sems, recv_sems) + passed:
            cp.wait_send()

    return _Exchange(list(blocks), [jax.ShapeDtypeStruct((N_DEV,) + blk.shape, blk.dtype) for blk in blocks], 7 * n,
                     start, finish)


def _sum_halves(grads, theirs, core, name):
    _, h, cols = theirs.shape
    by_rows = _by_rows(grads.shape[1])
    br = _rows_block(h, cols) if by_rows else h
    nb = h // br

    def body(core_ref, a_ref, b_ref, o_ref):
        o_ref[...] = (a_ref[...] + b_ref[...]).astype(BF16)

    if by_rows:
        mine = pl.BlockSpec((1, br, cols), lambda j, i, core_ref: (j, core_ref[0] * nb + i, 0))
    else:
        mine = pl.BlockSpec((1, br, cols), lambda j, i, core_ref: (j, i, core_ref[0]))
    return pl.pallas_call(
        body, name=name,
        out_shape=jax.ShapeDtypeStruct(theirs.shape, BF16),
        grid_spec=pltpu.PrefetchScalarGridSpec(
            num_scalar_prefetch=1, grid=(N_CHIPS, nb),
            in_specs=[mine, pl.BlockSpec((1, br, cols), lambda j, i, core_ref: (j, i, 0))],
            out_specs=pl.BlockSpec((1, br, cols), lambda j, i, core_ref: (j, i, 0))),
        compiler_params=_cparams(("parallel", "parallel")),
    )(core, grads, theirs)


def _sum_chips(slots, sums, place, by_rows, name):
    _, h, cols = slots.shape
    br = _rows_block(h, cols) if by_rows else h
    nb = h // br

    def body(place_ref, s_ref, own_ref, o_ref):
        me = place_ref[1]
        acc = None
        for k in range(N_CHIPS):
            term = jnp.where(me == k, own_ref[k], s_ref[k]).astype(F32)
            acc = term if acc is None else acc + term
        o_ref[...] = acc

    stack = pl.BlockSpec((N_CHIPS, br, cols), lambda i, place_ref: (0, i, 0))
    if by_rows:
        out_shape, out_map = (2 * h, cols), lambda i, place_ref: (place_ref[0] * nb + i, 0)
    else:
        out_shape, out_map = (h, 2 * cols), lambda i, place_ref: (i, place_ref[0])
    return pl.pallas_call(
        body, name=name,
        out_shape=jax.ShapeDtypeStruct(out_shape, F32),
        grid_spec=pltpu.PrefetchScalarGridSpec(
            num_scalar_prefetch=1, grid=(nb,), in_specs=[stack, stack],
            out_specs=pl.BlockSpec((br, cols), out_map)),
        compiler_params=_cparams(("parallel",)),
    )(place, slots, sums)


def _adamw_math(w, g, m, v):
    m = ADAM_B1 * m + (1.0 - ADAM_B1) * g
    v = ADAM_B2 * v + (1.0 - ADAM_B2) * (g * g)
    m_hat = m / (1.0 - ADAM_B1 ** ADAM_STEP)
    v_hat = v / (1.0 - ADAM_B2 ** ADAM_STEP)
    delta = -ADAM_LR * (m_hat / (jnp.sqrt(v_hat) + ADAM_EPS) + ADAM_WD * w)
    return delta, m, v


def _adamw(w, g, m, v, name, ex=None):
    def body(w_ref, g_ref, m_ref, v_ref, d_ref, nm_ref, nv_ref):
        d, nm, nv = _adamw_math(w_ref[...], g_ref[...], m_ref[...], v_ref[...])
        d_ref[...] = d
        nm_ref[...] = nm
        nv_ref[...] = nv

    if w.ndim == 3:
        rows = w.shape[0]
        br = max(b for b in range(1, 65) if rows % b == 0)
        spec, steps = pl.BlockSpec((br,) + w.shape[1:], lambda i: (i, 0, 0)), rows // br
    else:
        rows, cols = w.shape
        br = _rows_block(rows, cols, unit=8)
        spec, steps = pl.BlockSpec((br, cols), lambda i: (i, 0)), rows // br
    shape = jax.ShapeDtypeStruct(w.shape, F32)
    outs, moved = _hosted_call(
        body, ex, name=name, out_shape=(shape, shape, shape), grid=(steps,),
        in_specs=[spec] * 4, out_specs=(spec, spec, spec), args=(w, g, m, v))
    return tuple(outs) if ex is None else (tuple(outs), moved)


def _adamw_small(parts, own, ws, ms, vs, loss_parts, loss_own, device):
    n = len(ws)

    def total(device_ref, parts_ref, own_ref):
        acc = None
        for dev in range(N_DEV):
            term = jnp.where(device_ref[0] == dev, own_ref[...], parts_ref[dev])
            acc = term if acc is None else acc + term
        return acc

    def body(device_ref, *refs):
        ins, outs = refs[:5 * n + 2], refs[5 * n + 2:]
        for k in range(n):
            g = total(device_ref, ins[k], ins[n + k])
            d, nm, nv = _adamw_math(ins[2 * n + k][...], g, ins[3 * n + k][...], ins[4 * n + k][...])
            for o_ref, val in zip(outs[k::n][:4], (g, d, nm, nv)):
                o_ref[...] = val
        outs[4 * n][...] = total(device_ref, ins[5 * n], ins[5 * n + 1])

    args = list(parts) + list(own) + list(ws) + list(ms) + list(vs) + [loss_parts, loss_own]
    whole = lambda a: pl.BlockSpec(a.shape, lambda i, device_ref, nd=a.ndim: (0,) * nd)
    shapes = [jax.ShapeDtypeStruct(w.shape, F32) for w in ws] * 4 + [jax.ShapeDtypeStruct(loss_own.shape, F32)]
    outs = pl.pallas_call(
        body, name="adamw_small", out_shape=shapes,
        grid_spec=pltpu.PrefetchScalarGridSpec(
            num_scalar_prefetch=1, grid=(1,), in_specs=[whole(a) for a in args], out_specs=[whole(a) for a in shapes]),
        compiler_params=_cparams(("arbitrary",)),
    )(device, *args)
    return outs[:n], outs[n:2 * n], outs[2 * n:3 * n], outs[3 * n:4 * n], outs[4 * n]


def _mm(a, b, *, name, ta=False, out_dtype=F32, res=None, bm=1024, bn=1024, bk=4096, b_stack=False, out_stack=False,
        ex=None):
    if ta:
        kdim, m = a.shape
    else:
        m, kdim = a.shape
    if b_stack:
        _, kb, chunk = b.shape
        n = N_CHIPS * chunk
    else:
        kb, n = b.shape
        chunk = n // N_CHIPS if out_stack else n
    assert kdim == kb, (a.shape, b.shape, ta)
    bm = _block(m, bm, LANES if ta else 16)
    bn = _block(chunk, bn, LANES)
    bk = _block(kdim, bk, LANES)
    nk = kdim // bk
    per_chunk = chunk // bn
    dims = (((0 if ta else 1,), (0,)), ((), ()))

    def body(*refs):
        refs = list(refs)
        a_ref, b_ref = refs[:2]
        r_ref = refs[2] if res is not None else None
        o_ref = refs[3] if res is not None else refs[2]
        part = lax.dot_general(a_ref[...].astype(BF16), b_ref[...].astype(BF16), dims, preferred_element_type=F32)

        def finish(r):
            if r_ref is not None:
                r = r + r_ref[...]
            o_ref[...] = r.astype(out_dtype)

        if nk == 1:
            finish(part)
        else:
            acc_ref = refs[-1]
            k = pl.program_id(2)

            @pl.when(k == 0)
            def _():
                acc_ref[...] = part

            @pl.when(k > 0)
            def _():
                acc_ref[...] += part

            @pl.when(k == nk - 1)
            def _():
                finish(acc_ref[...])

    a_spec = pl.BlockSpec((bk, bm), lambda i, j, k: (k, i)) if ta else pl.BlockSpec((bm, bk), lambda i, j, k: (i, k))
    if b_stack:
        b_spec = pl.BlockSpec((None, bk, bn), lambda i, j, k: (j // per_chunk, k, j % per_chunk))
    else:
        b_spec = pl.BlockSpec((bk, bn), lambda i, j, k: (k, j))
    r_spec = pl.BlockSpec((bm, bn), lambda i, j, k: (i, j))
    if out_stack:
        o_spec = pl.BlockSpec((None, bm, bn), lambda i, j, k: (j // per_chunk, i, j % per_chunk))
        o_shape = (N_CHIPS, m, chunk)
    else:
        o_spec, o_shape = r_spec, (m, n)
    in_specs = [a_spec, b_spec] + ([r_spec] if res is not None else [])
    args = (a, b) + ((res,) if res is not None else ())
    (out,), moved = _hosted_call(
        body, ex, name=name, out_shape=(jax.ShapeDtypeStruct(o_shape, out_dtype),),
        grid=(m // bm, n // bn, nk), in_specs=in_specs, out_specs=(o_spec,),
        scratch=[pltpu.VMEM((bm, bn), F32)] if nk > 1 else [], args=args)
    return out if ex is None else (out, moved)


def _rms_fwd(x, g, name, ex=None):
    t, d = x.shape
    bt = _block(t, 512, 16)

    def body(x_ref, g_ref, h_ref):
        xv = x_ref[...]
        r = lax.rsqrt(jnp.mean(xv * xv, axis=-1, keepdims=True) + EPS)
        h_ref[...] = (xv * r * g_ref[...]).astype(BF16)

    (out,), moved = _hosted_call(
        body, ex, name=name, out_shape=(jax.ShapeDtypeStruct((t, d), BF16),), grid=(t // bt,),
        in_specs=[pl.BlockSpec((bt, d), lambda i: (i, 0)), pl.BlockSpec((1, d), lambda i: (0, 0))],
        out_specs=(pl.BlockSpec((bt, d), lambda i: (i, 0)),), args=(x, g))
    return out if ex is None else (out, moved)


def _rms_bwd(dh, x, g, dres, name, ex=None):
    t, d = x.shape
    bt = _block(t, 256, 16)
    want_dx = dres is not None

    def body(*refs):
        if want_dx:
            dh_ref, x_ref, g_ref, dres_ref, dx_ref, dxb_ref, dg_ref = refs
        else:
            dh_ref, x_ref, g_ref, dg_ref = refs
        xv = x_ref[...]
        r = lax.rsqrt(jnp.mean(xv * xv, axis=-1, keepdims=True) + EPS)
        xhat = xv * r
        dhv = dh_ref[...]

        @pl.when(pl.program_id(0) == 0)
        def _():
            dg_ref[...] = jnp.zeros_like(dg_ref)

        dg_ref[...] += jnp.sum(dhv * xhat, axis=0, keepdims=True)
        if want_dx:
            dxhat = dhv * g_ref[...]
            dx = dres_ref[...] + r * (dxhat - xhat * jnp.mean(dxhat * xhat, axis=-1, keepdims=True))
            dx_ref[...] = dx
            dxb_ref[...] = dx.astype(BF16)

    row = pl.BlockSpec((bt, d), lambda i: (i, 0))
    vec = pl.BlockSpec((1, d), lambda i: (0, 0))
    if want_dx:
        outs, moved = _hosted_call(
            body, ex, name=name, grid=(t // bt,),
            out_shape=(jax.ShapeDtypeStruct((t, d), F32), jax.ShapeDtypeStruct((t, d), BF16),
                       jax.ShapeDtypeStruct((1, d), F32)),
            in_specs=[row, row, vec, row], out_specs=(row, row, vec), args=(dh, x, g, dres))
        return tuple(outs) if ex is None else (tuple(outs), moved)
    return pl.pallas_call(
        body, name=name, grid=(t // bt,), out_shape=jax.ShapeDtypeStruct((1, d), F32),
        in_specs=[row, row, vec], out_specs=vec,
        compiler_params=_cparams(("arbitrary",)),
    )(dh, x, g)


def _in_proj_attn(h, w_qkv, w_f_lanes, w_f):
    t, d = h.shape
    bm = _block(t, 512, 16)

    def body(h_ref, wq_ref, wl_ref, wf_ref, qkv_ref, fl_ref, f_ref):
        hv = h_ref[...]
        qkv_ref[...] = jnp.dot(hv, wq_ref[...], preferred_element_type=F32).astype(BF16)
        fl_ref[...] = jnp.dot(hv, wl_ref[...], preferred_element_type=F32)
        f_ref[...] = jnp.dot(hv, wf_ref[...], preferred_element_type=F32)

    whole = lambda w: pl.BlockSpec(w.shape, lambda i: (0, 0))
    rows = lambda n: pl.BlockSpec((bm, n), lambda i: (i, 0))
    return pl.pallas_call(
        body, name="in_proj_attn", grid=(t // bm,),
        out_shape=(jax.ShapeDtypeStruct((t, w_qkv.shape[1]), BF16), jax.ShapeDtypeStruct((t, w_f_lanes.shape[1]), F32),
                   jax.ShapeDtypeStruct((t, w_f.shape[1]), F32)),
        in_specs=[rows(d), whole(w_qkv), whole(w_f_lanes), whole(w_f)],
        out_specs=(rows(w_qkv.shape[1]), rows(w_f_lanes.shape[1]), rows(w_f.shape[1])),
        compiler_params=_cparams(("parallel",)),
    )(h, w_qkv, w_f_lanes, w_f)


def _mm_res_norm(a, b, res, g, name):
    t, k = a.shape
    d = b.shape[1]
    bm = _block(t, 512, 16)

    def body(a_ref, b_ref, r_ref, g_ref, x_ref, h_ref):
        xv = jnp.dot(a_ref[...], b_ref[...], preferred_element_type=F32) + r_ref[...]
        x_ref[...] = xv
        r = lax.rsqrt(jnp.mean(xv * xv, axis=-1, keepdims=True) + EPS)
        h_ref[...] = (xv * r * g_ref[...]).astype(BF16)

    row = pl.BlockSpec((bm, d), lambda i: (i, 0))
    return pl.pallas_call(
        body, name=name, grid=(t // bm,),
        out_shape=(jax.ShapeDtypeStruct((t, d), F32), jax.ShapeDtypeStruct((t, d), BF16)),
        in_specs=[pl.BlockSpec((bm, k), lambda i: (i, 0)), pl.BlockSpec((k, d), lambda i: (0, 0)), row,
                  pl.BlockSpec((1, d), lambda i: (0, 0))],
        out_specs=(row, row), compiler_params=_cparams(("parallel",)),
    )(a, b, res, g)


def _ffn_out_loss(act, w, res, target, g):
    t, k = act.shape
    d = w.shape[1]
    bm = _block(t, 512, 16)

    def body(a_ref, w_ref, r_ref, t_ref, g_ref, dx_ref, dxb_ref, dg_ref, loss_ref):
        xv = jnp.dot(a_ref[...], w_ref[...], preferred_element_type=F32) + r_ref[...]
        gv = g_ref[...]
        r = lax.rsqrt(jnp.mean(xv * xv, axis=-1, keepdims=True) + EPS)
        xhat = xv * r
        err = xhat * gv - t_ref[...]

        @pl.when(pl.program_id(0) == 0)
        def _():
            dg_ref[...] = jnp.zeros_like(dg_ref)
            loss_ref[...] = jnp.zeros_like(loss_ref)

        loss_ref[...] += 0.5 * jnp.sum(jnp.mean(err * err, axis=-1, keepdims=True), axis=0, keepdims=True)
        dy = err * (1.0 / d)
        dg_ref[...] += jnp.sum(dy * xhat, axis=0, keepdims=True)
        dxhat = dy * gv
        dx = r * (dxhat - xhat * jnp.mean(dxhat * xhat, axis=-1, keepdims=True))
        dx_ref[...] = dx
        dxb_ref[...] = dx.astype(BF16)

    row = pl.BlockSpec((bm, d), lambda i: (i, 0))
    vec = pl.BlockSpec((1, d), lambda i: (0, 0))
    return pl.pallas_call(
        body, name="ffn_out_loss", grid=(t // bm,),
        out_shape=(jax.ShapeDtypeStruct((t, d), F32), jax.ShapeDtypeStruct((t, d), BF16),
                   jax.ShapeDtypeStruct((1, d), F32), jax.ShapeDtypeStruct((1, LANES), F32)),
        in_specs=[pl.BlockSpec((bm, k), lambda i: (i, 0)), pl.BlockSpec((k, d), lambda i: (0, 0)), row, row, vec],
        out_specs=(row, row, vec, pl.BlockSpec((1, LANES), lambda i: (0, 0))),
        compiler_params=_cparams(("arbitrary",)),
    )(act, w, res, target, g)


def _mm_norm_bwd(a, b, x, g, dres, name, ex=None):
    t, k = a.shape
    d = b.shape[1]
    bm = _block(t, 512 if k <= 2048 else 256, 16)

    def body(a_ref, b_ref, x_ref, g_ref, dres_ref, dx_ref, dxb_ref, dg_ref):
        @pl.when(pl.program_id(0) == 0)
        def _():
            dg_ref[...] = jnp.zeros_like(dg_ref)

        dhv = jnp.dot(a_ref[...], b_ref[...], preferred_element_type=F32)
        xv = x_ref[...]
        r = lax.rsqrt(jnp.mean(xv * xv, axis=-1, keepdims=True) + EPS)
        xhat = xv * r
        dg_ref[...] += jnp.sum(dhv * xhat, axis=0, keepdims=True)
        dxhat = dhv * g_ref[...]
        dx = dres_ref[...] + r * (dxhat - xhat * jnp.mean(dxhat * xhat, axis=-1, keepdims=True))
        dx_ref[...] = dx
        dxb_ref[...] = dx.astype(BF16)

    row = pl.BlockSpec((bm, d), lambda i: (i, 0))
    vec = pl.BlockSpec((1, d), lambda i: (0, 0))
    outs, moved = _hosted_call(
        body, ex, name=name, grid=(t // bm,),
        out_shape=(jax.ShapeDtypeStruct((t, d), F32), jax.ShapeDtypeStruct((t, d), BF16), jax.ShapeDtypeStruct((1, d), F32)),
        in_specs=[pl.BlockSpec((bm, k), lambda i: (i, 0)), pl.BlockSpec((k, d), lambda i: (0, 0)), row, vec, row],
        out_specs=(row, row, vec), args=(a, b, x, g, dres))
    return tuple(outs) if ex is None else (tuple(outs), moved)


GU_COLS = GATE_WIDTH + POOL_WIDTH
U_BLK = GATE_WIDTH // POOL_WIDTH


def _shift_down(a, k, row):
    return jnp.where(row >= k, pltpu.roll(a, k, 0), 0.0)


def _shift_up(a, k, row):
    n = a.shape[0]
    return jnp.where(row < n - k, pltpu.roll(a, n - k, 0), 0.0)


def _window_delta(u, w, row):
    s, k = u, 1
    while k < w:
        s = s + _shift_down(s, k, row)
        k *= 2
    cnt = jnp.minimum(row + 1, w).astype(F32)
    return s / cnt - u, cnt


def _pool_fwd(gu, pool_w, pool_scale):
    b, s, _ = gu.shape

    def body(u_ref, pw_ref, sc_ref, y_ref):
        row = lax.broadcasted_iota(jnp.int32, (s, POOL_GC), 0)
        for g, w in enumerate(POOL_WINDOWS):
            cols = slice(g * POOL_GC, (g + 1) * POOL_GC)
            d, _ = _window_delta(u_ref[0, :, cols].astype(F32), w, row)
            z = jnp.dot(d.astype(BF16), pw_ref[g].astype(BF16), preferred_element_type=F32)
            y_ref[0, :, cols] = (z * sc_ref[:, cols]).astype(BF16)

    return pl.pallas_call(
        body, name="pool_fwd", out_shape=jax.ShapeDtypeStruct((b, s, POOL_WIDTH), BF16), grid=(b,),
        in_specs=[pl.BlockSpec((1, s, POOL_WIDTH), lambda i: (i, 0, U_BLK)),
                  pl.BlockSpec((4, POOL_GC, POOL_GC), lambda i: (0, 0, 0)),
                  pl.BlockSpec((1, POOL_WIDTH), lambda i: (0, 0))],
        out_specs=pl.BlockSpec((1, s, POOL_WIDTH), lambda i: (i, 0, 0)),
        compiler_params=_cparams(("parallel",)),
    )(gu, pool_w, pool_scale)


def _pool_bwd(gu, dyp, w_out_t, pool_w, pool_scale, dgu):
    b, s, _ = gu.shape

    def body(u_ref, dyp_ref, w_ref, pw_ref, sc_ref, dgu_in, du_ref, dpw_ref, dsc_ref):
        del dgu_in

        @pl.when(pl.program_id(0) == 0)
        def _():
            dpw_ref[...] = jnp.zeros_like(dpw_ref)
            dsc_ref[...] = jnp.zeros_like(dsc_ref)

        row = lax.broadcasted_iota(jnp.int32, (s, POOL_GC), 0)
        for g, w in enumerate(POOL_WINDOWS):
            cols = slice(g * POOL_GC, (g + 1) * POOL_GC)
            d, cnt = _window_delta(u_ref[0, :, cols].astype(F32), w, row)
            db = d.astype(BF16)
            pw = pw_ref[g].astype(BF16)
            z = jnp.dot(db, pw, preferred_element_type=F32)
            dyv = jnp.dot(dyp_ref[0], w_ref[:, cols], preferred_element_type=F32)
            dsc_ref[:, cols] += jnp.sum(dyv * z, axis=0, keepdims=True)
            dz = (dyv * sc_ref[:, cols]).astype(BF16)
            dpw_ref[g] += lax.dot_general(db, dz, (((0,), (0,)), ((), ())), preferred_element_type=F32)
            dd = lax.dot_general(dz, pw, (((1,), (1,)), ((), ())), preferred_element_type=F32)
            acc, k = dd / cnt, 1
            while k < w:
                acc = acc + _shift_up(acc, k, row)
                k *= 2
            du_ref[0, :, cols] = (acc - dd).astype(BF16)

    return pl.pallas_call(
        body, name="pool_bwd", grid=(b,),
        out_shape=(jax.ShapeDtypeStruct((b, s, GU_COLS), BF16), jax.ShapeDtypeStruct((4, POOL_GC, POOL_GC), F32),
                   jax.ShapeDtypeStruct((1, POOL_WIDTH), F32)),
        in_specs=[pl.BlockSpec((1, s, POOL_WIDTH), lambda i: (i, 0, U_BLK)),
                  pl.BlockSpec((1, s, dyp.shape[2]), lambda i: (i, 0, 0)),
                  pl.BlockSpec(w_out_t.shape, lambda i: (0, 0)),
                  pl.BlockSpec((4, POOL_GC, POOL_GC), lambda i: (0, 0, 0)),
                  pl.BlockSpec((1, POOL_WIDTH), lambda i: (0, 0)), ANY],
        out_specs=(pl.BlockSpec((1, s, POOL_WIDTH), lambda i: (i, 0, U_BLK)),
                   pl.BlockSpec((4, POOL_GC, POOL_GC), lambda i: (0, 0, 0)),
                   pl.BlockSpec((1, POOL_WIDTH), lambda i: (0, 0))),
        input_output_aliases={5: 0},
        compiler_params=_cparams(("arbitrary",)),
    )(gu, dyp, w_out_t, pool_w, pool_scale, dgu)


def _forget_cumsum(f, bias, name):
    b, s, c = f.shape

    def body(f_ref, b_ref, c_ref):
        row = lax.broadcasted_iota(jnp.int32, (s, LANES), 0)
        z = f_ref[0] + b_ref[...]
        acc = jnp.minimum(z, 0.0) - jnp.log(1.0 + jnp.exp(-jnp.abs(z)))
        k = 1
        while k < s:
            acc = acc + _shift_down(acc, k, row)
            k *= 2
        c_ref[0] = acc

    return pl.pallas_call(
        body, name=name, out_shape=jax.ShapeDtypeStruct((b, s, c), F32), grid=(b, c // LANES),
        in_specs=[pl.BlockSpec((1, s, LANES), lambda i, j: (i, 0, j)), pl.BlockSpec((1, LANES), lambda i, j: (0, j))],
        out_specs=pl.BlockSpec((1, s, LANES), lambda i, j: (i, 0, j)),
        compiler_params=_cparams(("parallel", "parallel")),
    )(f, bias)


def _forget_bwd(dc, f, bias):
    b, s, _ = f.shape

    def body(dc_ref, f_ref, b_ref, df_ref, db_ref):
        @pl.when(pl.program_id(0) == 0)
        def _():
            db_ref[...] = jnp.zeros_like(db_ref)

        row = lax.broadcasted_iota(jnp.int32, (s, LANES), 0)
        acc, k = dc_ref[0], 1
        while k < s:
            acc = acc + _shift_up(acc, k, row)
            k *= 2
        z = f_ref[0] + b_ref[...]
        df = acc / (1.0 + jnp.exp(z))
        db_ref[...] += jnp.sum(df, axis=0, keepdims=True)
        df_ref[0] = df.astype(BF16)

    blk = pl.BlockSpec((1, s, LANES), lambda i: (i, 0, 0))
    vec = pl.BlockSpec((1, LANES), lambda i: (0, 0))
    return pl.pallas_call(
        body, name="forget_bwd", grid=(b,),
        out_shape=(jax.ShapeDtypeStruct((b, s, LANES), BF16), jax.ShapeDtypeStruct((1, LANES), F32)),
        in_specs=[blk, blk, vec], out_specs=(blk, vec),
        compiler_params=_cparams(("arbitrary",)),
    )(dc, f, bias)


KV_BLK0 = 2
PAIRS = FOX_HEADS // 2
FOX_SCALE = FOX_DH ** -0.5
NT_DIMS = (((1,), (1,)), ((), ()))
TN_DIMS = (((0,), (0,)), ((), ()))


def _stack_heads(v):
    head = lax.broadcasted_iota(jnp.int32, v.shape, 1) // FOX_DH
    zero = jnp.zeros_like(v)
    return jnp.concatenate([jnp.where(head == 0, v, zero), jnp.where(head == 1, v, zero)], axis=0)


def _stack_cols(v):
    return jnp.concatenate([v[:, 0:1], v[:, FOX_DH:FOX_DH + 1]], axis=0)


def _unstack(t, blk):
    head = lax.broadcasted_iota(jnp.int32, (blk, LANES), 1) // FOX_DH
    return jnp.where(head == 0, t[:blk], t[blk:])


def _fox_scores(q_all, kblk, row_bias, cr_ref, kb, masked, blk):
    top = lax.broadcasted_iota(jnp.int32, (2 * blk, 1), 0) < blk
    s = lax.dot_general(q_all, kblk, NT_DIMS, preferred_element_type=F32)
    s = s + (row_bias - jnp.where(top, cr_ref[0, 0, kb], cr_ref[0, 1, kb]))
    if masked:
        r = lax.broadcasted_iota(jnp.int32, (2 * blk, blk), 0)
        keep = jnp.where(r >= blk, r - blk, r) >= lax.broadcasted_iota(jnp.int32, (2 * blk, blk), 1)
        s = jnp.where(keep, s, NEG_INF)
    return s


def _fox_fwd(qkv, c_exp, c_row, ex=None):
    b, s, _ = qkv.shape
    blk = min(ATT_BLOCK, s)
    nq = s // blk

    def body(q_ref, kv_ref, cc_ref, cr_ref, o_ref, ob_ref, lse_ref):
        qi = pl.program_id(2)
        q_all = _stack_heads(q_ref[0] * FOX_SCALE)
        cq = _stack_cols(cc_ref[0])

        def step(kb, carry, masked):
            m, l, acc = carry
            rows = pl.ds(pl.multiple_of(kb * blk, blk), blk)
            sc = _fox_scores(q_all, kv_ref[0, rows, :LANES], cq, cr_ref, kb, masked, blk)
            m_new = jnp.maximum(m, jnp.max(sc, axis=-1, keepdims=True))
            p = jnp.exp(sc - m_new)
            alpha = jnp.exp(m - m_new)
            l = alpha * l + jnp.sum(p, axis=-1, keepdims=True)
            acc = alpha * acc + jnp.dot(p.astype(BF16), kv_ref[0, rows, LANES:], preferred_element_type=F32)
            return m_new, l, acc

        init = (jnp.full((2 * blk, 1), NEG_INF, F32), jnp.zeros((2 * blk, 1), F32), jnp.zeros((2 * blk, LANES), F32))
        m, l, acc = step(qi, lax.fori_loop(0, qi, functools.partial(step, masked=False), init), True)
        o = _unstack(acc / l, blk)
        o_ref[0] = o
        ob_ref[0] = o.astype(BF16)
        lse_ref[0] = _unstack(jnp.broadcast_to(m + jnp.log(l), (2 * blk, LANES)), blk)

    tile = pl.BlockSpec((1, blk, LANES), lambda i, h, q: (i, q, h))
    kvspec = pl.BlockSpec((1, s, 2 * LANES), lambda i, h, q: (i, 0, KV_BLK0 + h))
    shape = jax.ShapeDtypeStruct((b, s, FOX_WIDTH), F32)
    return _hosted_call(
        body, ex, name="fox_fwd", out_shape=(shape, jax.ShapeDtypeStruct((b, s, FOX_WIDTH), BF16), shape),
        grid=(b, PAIRS, nq),
        in_specs=[tile, kvspec, tile, pl.BlockSpec((1, 2, nq, 1, blk), lambda i, h, q: (i, h, 0, 0, 0))],
        out_specs=(tile, tile, tile), args=(qkv, qkv, c_exp, c_row))


def _fox_bwd(qkv, c_exp, c_row, lse, o, do, ex=None):
    b, s, _ = qkv.shape
    blk = min(ATT_BLOCK, s)
    nq = s // blk

    def body(q_ref, kv_ref, cc_ref, cr_ref, lse_ref, o_ref, do_ref, dq_ref, dkv_ref, dcq_ref, dc_ref, dk_acc, dv_acc):
        qi = pl.program_id(2)

        @pl.when(qi == 0)
        def _():
            dk_acc[...] = jnp.zeros_like(dk_acc)
            dv_acc[...] = jnp.zeros_like(dv_acc)
            dc_ref[...] = jnp.zeros_like(dc_ref)

        q_all = _stack_heads(q_ref[0] * FOX_SCALE)
        dov = do_ref[0]
        do_all = _stack_heads(dov.astype(BF16))
        delta = jnp.sum(_stack_heads(dov * o_ref[0]), axis=-1, keepdims=True)
        bias = _stack_cols(cc_ref[0]) - _stack_cols(lse_ref[0])

        def step(kb, carry, masked):
            acc, dcq = carry
            rows = pl.ds(pl.multiple_of(kb * blk, blk), blk)
            kblk = kv_ref[0, rows, :LANES]
            p = jnp.exp(_fox_scores(q_all, kblk, bias, cr_ref, kb, masked, blk))
            dp = lax.dot_general(do_all, kv_ref[0, rows, LANES:], NT_DIMS, preferred_element_type=F32)
            ds = p * (dp - delta)
            dsb = ds.astype(BF16)
            dv_acc[rows, :] += lax.dot_general(p.astype(BF16), do_all, TN_DIMS, preferred_element_type=F32)
            dk_acc[rows, :] += lax.dot_general(dsb, q_all, TN_DIMS, preferred_element_type=F32)
            dc_ref[0, 0, kb] -= jnp.sum(ds[:blk], axis=0, keepdims=True)
            dc_ref[0, 1, kb] -= jnp.sum(ds[blk:], axis=0, keepdims=True)
            acc = acc + jnp.dot(dsb, kblk, preferred_element_type=F32)
            return acc, dcq + jnp.sum(ds, axis=-1, keepdims=True)

        init = (jnp.zeros((2 * blk, LANES), F32), jnp.zeros((2 * blk, 1), F32))
        acc, dcq = step(qi, lax.fori_loop(0, qi, functools.partial(step, masked=False), init), True)
        dq_ref[0] = (_unstack(acc, blk) * FOX_SCALE).astype(BF16)
        dcq_ref[0, 0] = jnp.where(lax.broadcasted_iota(jnp.int32, (blk, 2), 1) == 0, dcq[:blk], dcq[blk:])

        @pl.when(qi == nq - 1)
        def _():
            dkv_ref[0, :, :LANES] = dk_acc[...].astype(BF16)
            dkv_ref[0, :, LANES:] = dv_acc[...].astype(BF16)

    tile = pl.BlockSpec((1, blk, LANES), lambda i, h, q: (i, q, h))
    kvspec = pl.BlockSpec((1, s, 2 * LANES), lambda i, h, q: (i, 0, KV_BLK0 + h))
    crow = pl.BlockSpec((1, 2, nq, 1, blk), lambda i, h, q: (i, h, 0, 0, 0))
    return _hosted_call(
        body, ex, name="fox_bwd", grid=(b, PAIRS, nq),
        out_shape=(jax.ShapeDtypeStruct((b, s, FOX_WIDTH), BF16), jax.ShapeDtypeStruct((b, s, 2 * FOX_WIDTH), BF16),
                   jax.ShapeDtypeStruct((b, PAIRS, s, 2), F32), jax.ShapeDtypeStruct(c_row.shape, F32)),
        in_specs=[tile, kvspec, tile, crow, tile, tile, tile],
        out_specs=(tile, pl.BlockSpec((1, s, 2 * LANES), lambda i, h, q: (i, 0, h)),
                   pl.BlockSpec((1, 1, blk, 2), lambda i, h, q: (i, h, q, 0)), crow),
        scratch=[pltpu.VMEM((s, LANES), F32), pltpu.VMEM((s, LANES), F32)],
        args=(qkv, qkv, c_exp, c_row, lse, o, do))


def _sigmoid(z):
    return 1.0 / (1.0 + jnp.exp(-z))


def _branches_mix(y, o, w_pool3, w_fox3, gu, b_gate):
    t = y.shape[0]
    chunk = w_pool3.shape[2]
    per_branch = D_MODEL // chunk
    bm = _block(t, 1024, 16)

    def body(y_ref, o_ref, wp_ref, wf_ref, gp_ref, gf_ref, bp_ref, bf_ref, yp_ref, yf_ref, mix_ref):
        yp = jnp.dot(y_ref[...], wp_ref[...], preferred_element_type=F32).astype(BF16)
        yf = jnp.dot(o_ref[...], wf_ref[...], preferred_element_type=F32).astype(BF16)
        yp_ref[...] = yp
        yf_ref[...] = yf
        gp = _sigmoid(gp_ref[...].astype(F32) + bp_ref[...])
        gf = _sigmoid(gf_ref[...].astype(F32) + bf_ref[...])
        mix_ref[...] = (gp * yp.astype(F32) + gf * yf.astype(F32)).astype(BF16)

    rows = pl.BlockSpec((bm, y.shape[1]), lambda i, j: (i, 0))
    weight = pl.BlockSpec((None, y.shape[1], chunk), lambda i, j: (j, 0, 0))
    tile = lambda base: pl.BlockSpec((bm, chunk), lambda i, j: (i, base + j))
    vec = lambda base: pl.BlockSpec((1, chunk), lambda i, j: (0, base + j))
    shape = jax.ShapeDtypeStruct((t, D_MODEL), BF16)
    return pl.pallas_call(
        body, name="branches_mix", out_shape=(shape, shape, shape), grid=(t // bm, per_branch),
        in_specs=[rows, rows, weight, weight, tile(0), tile(per_branch), vec(0), vec(per_branch)],
        out_specs=(tile(0), tile(0), tile(0)),
        compiler_params=_cparams(("parallel", "arbitrary")),
    )(y, o, w_pool3, w_fox3, gu, gu, b_gate, b_gate)


def _mix_bwd(gu, b_gate, y_pool, y_fox, dx, w_out_t):
    t = gu.shape[0]
    bt = _block(t, 256, 16)

    def body(gp_ref, gf_ref, bp_ref, bf_ref, yp_ref, yf_ref, dx_ref, w_ref, dyp_ref, dyf_ref, dgl_ref, db_ref):
        @pl.when(pl.program_id(0) == 0)
        def _():
            db_ref[...] = jnp.zeros_like(db_ref)

        dm = jnp.dot(dx_ref[...], w_ref[...], preferred_element_type=F32)
        gp = _sigmoid(gp_ref[...].astype(F32) + bp_ref[...])
        gf = _sigmoid(gf_ref[...].astype(F32) + bf_ref[...])
        dyp_ref[...] = (dm * gp).astype(BF16)
        dyf_ref[...] = (dm * gf).astype(BF16)
        dlp = dm * yp_ref[...].astype(F32) * gp * (1.0 - gp)
        dlf = dm * yf_ref[...].astype(F32) * gf * (1.0 - gf)
        dgl_ref[:, :D_MODEL] = dlp.astype(BF16)
        dgl_ref[:, D_MODEL:] = dlf.astype(BF16)
        db_ref[:, :D_MODEL] += jnp.sum(dlp, axis=0, keepdims=True)
        db_ref[:, D_MODEL:] += jnp.sum(dlf, axis=0, keepdims=True)

    col = lambda j: pl.BlockSpec((bt, D_MODEL), lambda i: (i, j))
    vec = lambda j: pl.BlockSpec((1, D_MODEL), lambda i: (0, j))
    wide = pl.BlockSpec((bt, GATE_WIDTH), lambda i: (i, 0))
    return pl.pallas_call(
        body, name="mix_bwd", grid=(t // bt,),
        out_shape=(jax.ShapeDtypeStruct((t, D_MODEL), BF16), jax.ShapeDtypeStruct((t, D_MODEL), BF16),
                   jax.ShapeDtypeStruct((t, GU_COLS), BF16), jax.ShapeDtypeStruct((1, GATE_WIDTH), F32)),
        in_specs=[col(0), col(1), vec(0), vec(1), col(0), col(0), col(0),
                  pl.BlockSpec(w_out_t.shape, lambda i: (0, 0))],
        out_specs=(col(0), col(0), wide, pl.BlockSpec((1, GATE_WIDTH), lambda i: (0, 0))),
        compiler_params=_cparams(("arbitrary",)),
    )(gu, gu, b_gate, b_gate, y_pool, y_fox, dx, w_out_t)


X_SCALE = X_DH ** -0.5


def _xattn_probs(qh, kh):
    s = lax.dot_general(qh, kh, NT_DIMS, preferred_element_type=F32) * X_SCALE
    e = jnp.exp(s - jnp.max(s, axis=-1, keepdims=True))
    return e / jnp.sum(e, axis=-1, keepdims=True)


def _xattn_fwd(q, kv):
    b, s, _ = q.shape
    m = kv.shape[1]
    bq = _block(s, 512, 16)

    def body(q_ref, kv_ref, o_ref):
        for h in range(X_HEADS):
            cols = slice(h * X_DH, (h + 1) * X_DH)
            p = _xattn_probs(q_ref[0, :, cols], kv_ref[0, :, cols])
            vh = kv_ref[0, :, X_WIDTH + h * X_DH:X_WIDTH + (h + 1) * X_DH]
            o_ref[0, :, cols] = jnp.dot(p.astype(BF16), vh, preferred_element_type=F32).astype(BF16)

    return pl.pallas_call(
        body, name="xattn_fwd", out_shape=jax.ShapeDtypeStruct((b, s, X_WIDTH), BF16), grid=(b, s // bq),
        in_specs=[pl.BlockSpec((1, bq, X_WIDTH), lambda i, j: (i, j, 0)),
                  pl.BlockSpec((1, m, 2 * X_WIDTH), lambda i, j: (i, 0, 0))],
        out_specs=pl.BlockSpec((1, bq, X_WIDTH), lambda i, j: (i, j, 0)),
        compiler_params=_cparams(("parallel", "parallel")),
    )(q, kv)


def _xattn_bwd(q, kv, dx, w_o_t):
    b, s, _ = q.shape
    m = kv.shape[1]
    bq = _block(s, 512, 16)

    def body(q_ref, kv_ref, dx_ref, w_ref, dq_ref, dkv_ref):
        @pl.when(pl.program_id(1) == 0)
        def _():
            dkv_ref[...] = jnp.zeros_like(dkv_ref)

        do = jnp.dot(dx_ref[0], w_ref[...], preferred_element_type=F32).astype(BF16)
        for h in range(X_HEADS):
            cols = slice(h * X_DH, (h + 1) * X_DH)
            vcols = slice(X_WIDTH + h * X_DH, X_WIDTH + (h + 1) * X_DH)
            qh, kh, vh, doh = q_ref[0, :, cols], kv_ref[0, :, cols], kv_ref[0, :, vcols], do[:, cols]
            p = _xattn_probs(qh, kh)
            dkv_ref[0, :, vcols] += lax.dot_general(p.astype(BF16), doh, TN_DIMS, preferred_element_type=F32)
            dp = lax.dot_general(doh, vh, NT_DIMS, preferred_element_type=F32)
            ds = (p * (dp - jnp.sum(p * dp, axis=-1, keepdims=True)) * X_SCALE).astype(BF16)
            dq_ref[0, :, cols] = jnp.dot(ds, kh, preferred_element_type=F32).astype(BF16)
            dkv_ref[0, :, cols] += lax.dot_general(ds, qh, TN_DIMS, preferred_element_type=F32)

    tile = pl.BlockSpec((1, bq, X_WIDTH), lambda i, j: (i, j, 0))
    mem = pl.BlockSpec((1, m, 2 * X_WIDTH), lambda i, j: (i, 0, 0))
    return pl.pallas_call(
        body, name="xattn_bwd", grid=(b, s // bq),
        out_shape=(jax.ShapeDtypeStruct((b, s, X_WIDTH), BF16), jax.ShapeDtypeStruct((b, m, 2 * X_WIDTH), F32)),
        in_specs=[tile, mem, pl.BlockSpec((1, bq, dx.shape[2]), lambda i, j: (i, j, 0)),
                  pl.BlockSpec(w_o_t.shape, lambda i, j: (0, 0))],
        out_specs=(tile, mem),
        compiler_params=_cparams(("parallel", "arbitrary")),
    )(q, kv, dx, w_o_t)


def _ffn_in(hf, w3):
    t, d = hf.shape
    chunk = w3.shape[2]
    half = N_CHIPS // 2
    bm = _block(t, 1024, 16)

    def body(a_ref, wg_ref, wu_ref, gt_ref, up_ref, act_ref):
        a = a_ref[...]
        gt = jnp.dot(a, wg_ref[...], preferred_element_type=F32).astype(BF16)
        up = jnp.dot(a, wu_ref[...], preferred_element_type=F32).astype(BF16)
        gt_ref[...] = gt
        up_ref[...] = up
        g32 = gt.astype(F32)
        act_ref[...] = (g32 * _sigmoid(g32) * up.astype(F32)).astype(BF16)

    tile = pl.BlockSpec((bm, chunk), lambda i, j: (i, j))
    shape = jax.ShapeDtypeStruct((t, half * chunk), BF16)
    return pl.pallas_call(
        body, name="ffn_in", out_shape=(shape, shape, shape), grid=(t // bm, half),
        in_specs=[pl.BlockSpec((bm, d), lambda i, j: (i, 0)),
                  pl.BlockSpec((None, d, chunk), lambda i, j: (j, 0, 0)),
                  pl.BlockSpec((None, d, chunk), lambda i, j: (j + half, 0, 0))],
        out_specs=(tile, tile, tile),
        compiler_params=_cparams(("parallel", "arbitrary")),
    )(hf, w3, w3)


def _ffn_act_bwd(dx, w_out_t, gate, up):
    t, d = dx.shape
    bt = _block(t, 256, 16)

    def body(dx_ref, w_ref, gt_ref, up_ref, o_ref):
        da = jnp.dot(dx_ref[...], w_ref[...], preferred_element_type=F32).astype(BF16).astype(F32)
        gt = gt_ref[...].astype(F32)
        sg = _sigmoid(gt)
        silu = gt * sg
        o_ref[:, :D_FF] = (da * up_ref[...].astype(F32) * (sg + silu * (1.0 - sg))).astype(BF16)
        o_ref[:, D_FF:] = (da * silu).astype(BF16)

    col = pl.BlockSpec((bt, D_FF), lambda i: (i, 0))
    return pl.pallas_call(
        body, name="ffn_act_bwd", out_shape=jax.ShapeDtypeStruct((t, 2 * D_FF), BF16), grid=(t // bt,),
        in_specs=[pl.BlockSpec((bt, d), lambda i: (i, 0)), pl.BlockSpec((d, D_FF), lambda i: (0, 0)), col, col],
        out_specs=pl.BlockSpec((bt, 2 * D_FF), lambda i: (i, 0)),
        compiler_params=_cparams(("parallel",)),
    )(dx, w_out_t, gate, up)


def _stack_of(w, axis):
    r, c = w.shape
    if axis == 0:
        return w.reshape(N_CHIPS, r // N_CHIPS, c)
    return w.reshape(r, N_CHIPS, c // N_CHIPS).transpose(1, 0, 2)


def _stack_t(w3):
    n, r, c = w3.shape
    return w3.transpose(0, 2, 1).reshape(n * c, r)


def _pair_rows(k, v):
    c = k.shape[1]
    return jnp.stack([k.reshape(PAIRS, LANES, c), v.reshape(PAIRS, LANES, c)], axis=1).reshape(2 * FOX_WIDTH, c)


def _unpair_rows(kv):
    c = kv.shape[1]
    kv = kv.reshape(PAIRS, 2, LANES, c)
    return kv[:, 0].reshape(FOX_WIDTH, c), kv[:, 1].reshape(FOX_WIDTH, c)


def _input_grad(parts, weights_t, ex):
    t = parts[0].shape[0]
    d = weights_t[0].shape[1]
    bm = _block(t, 512, 16)
    n = len(parts)

    def body(*refs):
        acc = None
        for a_ref, b_ref in zip(refs[:n], refs[n:2 * n]):
            term = jnp.dot(a_ref[...], b_ref[...], preferred_element_type=F32)
            acc = term if acc is None else acc + term
        refs[2 * n][...] = acc

    (out,), moved = _hosted_call(
        body, ex, name="d_h", grid=(t // bm,), out_shape=(jax.ShapeDtypeStruct((t, d), F32),),
        in_specs=[pl.BlockSpec((bm, p.shape[1]), lambda i: (i, 0)) for p in parts]
        + [pl.BlockSpec(w.shape, lambda i: (0, 0)) for w in weights_t],
        out_specs=(pl.BlockSpec((bm, d), lambda i: (i, 0)),), args=tuple(parts) + tuple(weights_t))
    return out, moved


def _step(x, mem, loss_target, weights, moments_m, moments_v):
    nb, s, d = x.shape
    n_mem = mem.shape[1]
    t = nb * s
    blk = min(ATT_BLOCK, s)
    x2 = x.reshape(t, d)
    mem2 = mem.reshape(nb * n_mem, d)
    tgt2 = loss_target.reshape(t, d)

    def shard2d(a, n):
        a = a.reshape(a.shape[1:])
        return a.T if n == "w_in" else a

    def unshard(a, n):
        return (a.T if n == "w_in" else a)[None]

    local = {n: shard2d(weights[n], n) for n, _, _ in SHARDED}

    names = [n for n, _, _ in SHARDED]
    last = ["w_ffn_out"]
    later = [n for n in names if n != "w_in" and n not in last]
    local_b = {n: local[n].astype(BF16) for n in names}
    g_mix = weights["norm_mix_g"]
    h, w_in_others = _rms_fwd(x2, g_mix, "norm_mix", ex=_gather_exchange([local_b["w_in"]]))
    w_in_stack, = _place_own(w_in_others, [local_b["w_in"]])

    def w_in_rows(lo, hi):
        per = IN_COLS // N_CHIPS
        parts = [w_in_stack[j, max(lo, j * per) - j * per:min(hi, (j + 1) * per) - j * per]
                 for j in range(N_CHIPS) if max(lo, j * per) < min(hi, (j + 1) * per)]
        return parts[0] if len(parts) == 1 else jnp.concatenate(parts)

    w_gu_t = jnp.concatenate([w_in_rows(2056, IN_COLS), w_in_rows(0, 512)])
    w_qkv_t = jnp.concatenate([w_in_rows(512, 1024), _pair_rows(w_in_rows(1024, 1536), w_in_rows(1536, 2048))])
    w_f_t = jnp.pad(w_in_rows(2048, 2056), ((0, LANES - FOX_HEADS), (0, 0)))
    w_gu, w_qkv, w_f = w_gu_t.T, w_qkv_t.T, w_f_t.T
    w_f_exp = jnp.repeat(w_f[:, :FOX_HEADS], FOX_DH, axis=1)

    g_mix, g_x, g_mem, g_ffn = (weights[n] for n in ("norm_mix_g", "norm_x_g", "norm_mem_g", "norm_ffn_g"))
    g_final = weights["norm_final_g"].reshape(1, d)
    pool_w = weights["pool_w"].reshape(4, POOL_GC, POOL_GC)
    pool_scale, b_gate = weights["pool_scale"], weights["b_gate"]
    b_f_pad = jnp.pad(weights["b_forget"], ((0, 0), (0, LANES - FOX_HEADS)))
    b_f_exp = jnp.repeat(weights["b_forget"], FOX_DH, axis=1)

    gu, last_others = _mm(h, w_gu, out_dtype=BF16, bn=512, name="in_proj_gates_pool",
                          ex=_gather_exchange([local_b[n] for n in last]))
    qkv, f_exp, f_pad = _in_proj_attn(h, w_qkv, w_f_exp, w_f)
    gu3, qkv3 = gu.reshape(nb, s, GU_COLS), qkv.reshape(nb, s, 3 * FOX_WIDTH)
    y = _pool_fwd(gu3, pool_w, pool_scale)
    c_exp = _forget_cumsum(f_exp.reshape(nb, s, FOX_WIDTH), b_f_exp, "forget_cumsum_lanes")
    c_pad = _forget_cumsum(f_pad.reshape(nb, s, LANES), b_f_pad, "forget_cumsum")
    c_row = c_pad[:, :, :FOX_HEADS].transpose(0, 2, 1).reshape(nb, FOX_HEADS, s // blk, 1, blk)
    (o, o_b, lse), gathered = _fox_fwd(qkv3, c_exp, c_row, ex=_gather_exchange([local_b[n] for n in later]))
    stacks = dict(zip(later, _place_own(gathered, [local_b[n] for n in later])))
    stacks.update(zip(last, _place_own(last_others, [local_b[n] for n in last])))
    w_pool_out3, w_fox_out3, w_xo3, w_ffn_in3 = (stacks[n] for n in ("w_pool_out", "w_fox_out", "w_xo", "w_ffn_in"))
    w_out, w_xq, w_xkv, w_ffn_out = (stacks[n].reshape(-1, stacks[n].shape[2])
                                     for n in ("w_out", "w_xq", "w_xkv", "w_ffn_out"))
    y2, o2 = y.reshape(t, POOL_WIDTH), o_b.reshape(t, FOX_WIDTH)
    y_pool, y_fox, mix = _branches_mix(y2, o2, w_pool_out3, w_fox_out3, gu, b_gate)
    x1, hx = _mm_res_norm(mix, w_out, x2, g_x, "mix_out_norm_x")
    mem_n = _rms_fwd(mem2, g_mem, "norm_mem")
    qx = _mm(hx, w_xq, out_dtype=BF16, name="x_q")
    kv = _mm(mem_n, w_xkv, out_dtype=BF16, name="x_kv")
    qx3, kv3 = qx.reshape(nb, s, X_WIDTH), kv.reshape(nb, n_mem, 2 * X_WIDTH)
    ox = _xattn_fwd(qx3, kv3).reshape(t, X_WIDTH)
    w_xo = w_xo3.transpose(1, 0, 2).reshape(X_WIDTH, D_MODEL)
    x2_, hf = _mm_res_norm(ox, w_xo, x1, g_ffn, "x_out_norm_ffn")
    ffn_gate, ffn_up, act = _ffn_in(hf, w_ffn_in3)

    dx3, dx3_b, dg_final, loss_part = _ffn_out_loss(act, w_ffn_out, x2_, tgt2, g_final)
    dw_ffn_out = _mm(act, dx3_b, ta=True, bm=1408, bn=1024, bk=2048, name="d_w_ffn_out")
    dffn = _ffn_act_bwd(dx3_b, w_ffn_out.T, ffn_gate, ffn_up)
    dw_ffn_in = _mm(hf, dffn, ta=True, bm=1024, bn=1408, bk=2048, out_stack=True, name="d_w_ffn_in")
    core = lax.axis_index("c").astype(jnp.int32).reshape(1)
    ffn_group = ["w_ffn_in", "w_ffn_out"]
    mid_group = ["w_pool_out", "w_fox_out", "w_out", "w_xq", "w_xkv", "w_xo"]
    grad_stacks = {"w_ffn_in": dw_ffn_in, "w_ffn_out": _stack_of(dw_ffn_out, 0)}

    def presum(group, theirs):
        return [_sum_halves(grad_stacks[n], t_, core, "sum_halves_" + n) for n, t_ in zip(group, theirs)]

    (dx2, dx2_b, dg_ffn), theirs = _mm_norm_bwd(dffn, _stack_t(w_ffn_in3), x2_, g_ffn, dx3, "d_hf_norm_ffn_bwd",
                                                ex=_swap_exchange([grad_stacks[n] for n in ffn_group]))
    chip_sums = dict(zip(ffn_group, presum(ffn_group, theirs)))

    dw_xo = _mm(ox, dx2_b, ta=True, bn=256, out_stack=True, name="d_w_xo")
    dqx, dkv = _xattn_bwd(qx3, kv3, dx2_b.reshape(nb, s, d), _stack_t(w_xo3))
    dqx2, dkv2 = dqx.reshape(t, X_WIDTH), dkv.reshape(nb * n_mem, 2 * X_WIDTH)
    dw_xkv = _mm(mem_n, dkv2, ta=True, name="d_w_xkv")
    dmem_n = _mm(dkv2, w_xkv.T, name="d_mem_n")
    dg_mem = _rms_bwd(dmem_n, mem2, g_mem, None, "norm_mem_bwd")
    dw_xq = _mm(hx, dqx2, ta=True, name="d_w_xq")
    dx1, dx1_b, dg_x = _mm_norm_bwd(dqx2, w_xq.T, x1, g_x, dx2, "d_hx_norm_x_bwd")

    dw_out = _mm(mix, dx1_b, ta=True, name="d_w_out")
    dyp, dyf, dgu, db_gate = _mix_bwd(gu, b_gate, y_pool, y_fox, dx1_b, w_out.T)
    dw_pool_out = _mm(y2, dyp, ta=True, bn=256, out_stack=True, name="d_w_pool_out")
    dw_fox_out = _mm(o2, dyf, ta=True, bn=256, out_stack=True, name="d_w_fox_out")
    do = _mm(dyf, _stack_t(w_fox_out3), name="d_o").reshape(nb, s, FOX_WIDTH)
    dgu3, dpool_w, dpool_scale = _pool_bwd(gu3, dyp.reshape(nb, s, d), _stack_t(w_pool_out3), pool_w, pool_scale,
                                           dgu.reshape(nb, s, GU_COLS))
    grad_stacks.update({"w_pool_out": dw_pool_out, "w_fox_out": dw_fox_out, "w_out": _stack_of(dw_out, 0),
                        "w_xq": _stack_of(dw_xq, 0), "w_xkv": _stack_of(dw_xkv, 0), "w_xo": dw_xo})
    dgu2 = dgu3.reshape(t, GU_COLS)
    dw_gu_t, theirs = _mm(dgu2, h, ta=True, name="d_w_gates_pool",
                          ex=_swap_exchange([grad_stacks[n] for n in mid_group]))
    chip_sums.update(zip(mid_group, presum(mid_group, theirs)))
    early = ffn_group + mid_group
    (dq3, dkv3, dc_q, dc_row), early_slots = _fox_bwd(qkv3, c_exp, c_row, lse, o, do,
                                                      ex=_chips_exchange([chip_sums[n] for n in early]))
    slots = dict(zip(early, early_slots))
    dc = dc_row.reshape(nb, FOX_HEADS, s).transpose(0, 2, 1) + dc_q.transpose(0, 2, 1, 3).reshape(nb, s, FOX_HEADS)
    dc = jnp.pad(dc, ((0, 0), (0, 0), (0, LANES - FOX_HEADS)))
    df, db_f = _forget_bwd(dc, f_pad.reshape(nb, s, LANES), b_f_pad)
    dq2, dkv2, df2 = dq3.reshape(t, FOX_WIDTH), dkv3.reshape(t, 2 * FOX_WIDTH), df.reshape(t, LANES)
    dw_q_t = _mm(dq2, h, ta=True, name="d_w_q")
    dw_kv_t = _mm(dkv2, h, ta=True, name="d_w_kv")
    dw_f_t = _mm(df2, h, ta=True, name="d_w_forget")
    dw_k_t, dw_v_t = _unpair_rows(dw_kv_t)
    dw_in_t = jnp.concatenate([dw_gu_t[GATE_WIDTH:], dw_q_t, dw_k_t, dw_v_t, dw_f_t[:FOX_HEADS],
                               dw_gu_t[:GATE_WIDTH]])
    grad_stacks["w_in"] = dw_in_t.reshape(N_CHIPS, IN_COLS // N_CHIPS, D_MODEL)
    chip_sums["w_in"], = presum(["w_in"], _run_exchange(_swap_exchange([grad_stacks["w_in"]]), "swap_halves_w_in"))
    dh, (slots["w_in"],) = _input_grad([dgu2, dq2, dkv2, df2],
                                       [w_gu_t, w_qkv_t[:FOX_WIDTH], w_qkv_t[FOX_WIDTH:], w_f_t],
                                       _chips_exchange([chip_sums["w_in"]]))

    place = jnp.stack([lax.axis_index("c"), 2 * lax.axis_index("x") + lax.axis_index("y")]).astype(jnp.int32)
    halves = [_sum_chips(slots[n], chip_sums[n], place, _by_rows(local[n].shape[0]), "sum_chips_" + n) for n in names]
    (dx, _, dg_mix), reduced = _rms_bwd(dh, x2, g_mix, dx1, "norm_mix_bwd", ex=_join_exchange(halves))

    small_grads = {"norm_mix_g": dg_mix, "b_forget": db_f[:, :FOX_HEADS], "b_gate": db_gate, "pool_w": dpool_w,
                   "pool_scale": dpool_scale, "norm_x_g": dg_x, "norm_mem_g": dg_mem, "norm_ffn_g": dg_ffn,
                   "norm_final_g": dg_final}
    def flat2d(a):
        return a.reshape(-1, a.shape[-1])

    small_names = [n for n, _ in SMALL]
    own = [flat2d(small_grads[n]) for n in small_names]
    small_gather = _small_exchange(own + [loss_part])

    def tiles_of(a):
        return a.transpose(2, 0, 1)

    def block_of(a3):
        return a3.transpose(1, 2, 0)

    grads, deltas, new_m, new_v = {}, {}, {}, {}
    gathered = None
    for n, g_ in zip(names, reduced):
        if n == "w_in":
            g_ = lax.optimization_barrier(g_.reshape(IN_COLS // N_CHIPS, 1, D_MODEL))
            (d_, m_, v_), gathered = _adamw(tiles_of(weights[n]), g_, tiles_of(moments_m[n]), tiles_of(moments_v[n]),
                                            "adamw_" + n, ex=small_gather)
            back = block_of
        else:
            d_, m_, v_ = _adamw(local[n], g_, shard2d(moments_m[n], n), shard2d(moments_v[n], n), "adamw_" + n)
            back = functools.partial(unshard, n=n)
        grads[n], deltas[n], new_m[n], new_v[n] = (back(a) for a in (g_, d_, m_, v_))

    device = (4 * lax.axis_index("x") + 2 * lax.axis_index("y") + lax.axis_index("c")).astype(jnp.int32).reshape(1)
    sg, sd, sm, sv, loss_sum = _adamw_small(
        gathered[:-1], own, [flat2d(weights[n]) for n in small_names], [flat2d(moments_m[n]) for n in small_names],
        [flat2d(moments_v[n]) for n in small_names], gathered[-1], loss_part, device)
    for n, g_, d_, m_, v_ in zip(small_names, sg, sd, sm, sv):
        grads[n], deltas[n], new_m[n], new_v[n] = (a.reshape(weights[n].shape) for a in (g_, d_, m_, v_))
    return loss_sum[0, 0], dx.reshape(nb, s, d), grads, deltas, new_m, new_v


def kernel(x, mem, norm_mix_g, w_in, b_forget, b_gate, pool_w, pool_scale, w_pool_out, w_fox_out, w_out, norm_x_g, norm_mem_g, w_xq, w_xkv, w_xo, norm_ffn_g, w_ffn_in, w_ffn_out, norm_final_g, loss_target, m_norm_mix_g, m_w_in, m_b_forget, m_b_gate, m_pool_w, m_pool_scale, m_w_pool_out, m_w_fox_out, m_w_out, m_norm_x_g, m_norm_mem_g, m_w_xq, m_w_xkv, m_w_xo, m_norm_ffn_g, m_w_ffn_in, m_w_ffn_out, m_norm_final_g, v_norm_mix_g, v_w_in, v_b_forget, v_b_gate, v_pool_w, v_pool_scale, v_w_pool_out, v_w_fox_out, v_w_out, v_norm_x_g, v_norm_mem_g, v_w_xq, v_w_xkv, v_w_xo, v_norm_ffn_g, v_w_ffn_in, v_w_ffn_out, v_norm_final_g):
    given = dict(locals())
    weights = {n: given[n] for n in WEIGHT_ORDER}
    moments_m = {n: given["m_" + n] for n in WEIGHT_ORDER}
    moments_v = {n: given["v_" + n] for n in WEIGHT_ORDER}
    loss, grad_x, grads, deltas, new_m, new_v = _step(x, mem, loss_target, weights, moments_m, moments_v)
    return (loss, grad_x, *[grads[n] for n in WEIGHT_ORDER], *[deltas[n] for n in WEIGHT_ORDER],
            *[new_m[n] for n in WEIGHT_ORDER], *[new_v[n] for n in WEIGHT_ORDER])
```

```python
import functools
import math

import jax
import jax.numpy as jnp
from jax import lax
from jax.experimental import pallas as pl
from jax.experimental.pallas import tpu as pltpu

F32 = jnp.float32
BF16 = jnp.bfloat16
MESH = pl.DeviceIdType.MESH

D_MODEL = 1024
EPS = 1e-6
POOL_WINDOWS = (2, 4, 8, 16)
POOL_WIDTH = 512
POOL_GC = 128
FOX_HEADS = 8
FOX_DH = 64
FOX_WIDTH = 512
X_HEADS = 4
X_DH = 128
X_WIDTH = 512
D_FF = 2816
IN_COLS = 4104
GATE_WIDTH = 2048
ADAM_LR = 0.001
ADAM_B1 = 0.9
ADAM_B2 = 0.999
ADAM_EPS = 1e-08
ADAM_WD = 0.01
ADAM_STEP = 10

N_CHIPS = 4
N_DEV = 8
LANES = 128
VMEM_LIMIT_BYTES = 56 * 1024 * 1024
NEG_INF = -1e30
ATT_BLOCK = 256

SHARDED = (
    ("w_in", (1024, IN_COLS), 1),
    ("w_pool_out", (POOL_WIDTH, 1024), 1),
    ("w_fox_out", (FOX_WIDTH, 1024), 1),
    ("w_out", (1024, 1024), 0),
    ("w_xq", (1024, X_WIDTH), 0),
    ("w_xkv", (1024, 2 * X_WIDTH), 0),
    ("w_xo", (X_WIDTH, 1024), 1),
    ("w_ffn_in", (1024, 2 * D_FF), 1),
    ("w_ffn_out", (D_FF, 1024), 0),
)
SMALL = (
    ("norm_mix_g", (1, 1024)),
    ("b_forget", (1, 8)),
    ("b_gate", (1, 2048)),
    ("pool_w", (1, 4, 128, 128)),
    ("pool_scale", (1, 512)),
    ("norm_x_g", (1, 1024)),
    ("norm_mem_g", (1, 1024)),
    ("norm_ffn_g", (1, 1024)),
    ("norm_final_g", (1024,)),
)
WEIGHT_ORDER = ("norm_mix_g", "w_in", "b_forget", "b_gate", "pool_w", "pool_scale", "w_pool_out", "w_fox_out", "w_out",
                "norm_x_g", "norm_mem_g", "w_xq", "w_xkv", "w_xo", "norm_ffn_g", "w_ffn_in", "w_ffn_out", "norm_final_g")


def _cparams(sem=None):
    return pltpu.CompilerParams(dimension_semantics=sem, vmem_limit_bytes=VMEM_LIMIT_BYTES)


def _block(dim, pref, unit):
    if dim <= pref:
        return dim
    best = None
    for b in range(unit, pref + 1, unit):
        if dim % b == 0:
            best = b
    assert best is not None, (dim, pref, unit)
    return best


def _rows_block(rows, cols, unit=16, elems=1 << 19):
    return _block(rows, max(unit, elems // cols // unit * unit), unit)


def _my_place():
    return lax.axis_index("x"), lax.axis_index("y"), lax.axis_index("c")


def _other_chips(x, y):
    return [(1 - x, y), (x, 1 - y), (1 - x, 1 - y)]


def _chip(place):
    return 2 * place[0] + place[1]


ANY = pl.BlockSpec(memory_space=pl.ANY)


def _by_rows(rows):
    return rows % 32 == 0


def _half_shape(rows, cols):
    return (rows // 2, cols) if _by_rows(rows) else (rows, cols // 2)


def _core_half(ref, core, lead=()):
    rows, cols = ref.shape[-2:]
    if _by_rows(rows):
        return ref.at[(*lead, pl.ds(core * (rows // 2), rows // 2), slice(None))]
    return ref.at[(*lead, slice(None), pl.ds(core * (cols // 2), cols // 2))]


class _Exchange:
    def __init__(self, arrays, out_shapes, n_sems, start, finish, in_place=False):
        self.arrays, self.out_shapes, self.n_sems, self.start, self.finish = arrays, out_shapes, n_sems, start, finish
        self.in_place = in_place

    def scratch(self):
        return [pltpu.SemaphoreType.DMA((self.n_sems,)), pltpu.SemaphoreType.DMA((self.n_sems,))]

    def aliases(self, first_in, first_out):
        return {first_in + k: first_out + k for k in range(len(self.arrays))} if self.in_place else {}


def _run_exchange(ex, name):
    n = len(ex.arrays)

    def body(*refs):
        ins, outs, sems = refs[:n], refs[n:2 * n], refs[2 * n:]
        ex.start(ins, outs, *sems)
        ex.finish(ins, outs, *sems)

    return pl.pallas_call(
        body, name=name, out_shape=ex.out_shapes, in_specs=[ANY] * n, out_specs=[ANY] * n, scratch_shapes=ex.scratch(),
        input_output_aliases=ex.aliases(0, 0),
    )(*ex.arrays)


def _hosted_call(body, ex, *, name, grid, in_specs, out_specs, out_shape, args, scratch=()):
    n_in, n_out, n_scr = len(args), len(out_shape), len(scratch)
    if ex is None:
        outs = pl.pallas_call(
            body, name=name, grid=grid, out_shape=out_shape, in_specs=in_specs, out_specs=out_specs,
            scratch_shapes=list(scratch), compiler_params=_cparams(("arbitrary",) * len(grid)))(*args)
        return outs, None
    nc = len(ex.arrays)

    def full_body(*refs):
        ins, cins = refs[:n_in], refs[n_in:n_in + nc]
        outs, couts = refs[n_in + nc:n_in + nc + n_out], refs[n_in + nc + n_out:n_in + 2 * nc + n_out]
        rest = refs[n_in + 2 * nc + n_out:]
        scr, sems = rest[:n_scr], rest[n_scr:]
        first = functools.reduce(jnp.logical_and, [pl.program_id(a) == 0 for a in range(len(grid))])
        last = functools.reduce(jnp.logical_and, [pl.program_id(a) == grid[a] - 1 for a in range(len(grid))])

        @pl.when(first)
        def _():
            ex.start(cins, couts, *sems)

        body(*ins, *outs, *scr)

        @pl.when(last)
        def _():
            ex.finish(cins, couts, *sems)

    outs = pl.pallas_call(
        full_body, name=name, grid=grid, out_shape=list(out_shape) + list(ex.out_shapes),
        in_specs=list(in_specs) + [ANY] * nc, out_specs=list(out_specs) + [ANY] * nc,
        scratch_shapes=list(scratch) + ex.scratch(), input_output_aliases=ex.aliases(n_in, n_out),
        compiler_params=_cparams(("arbitrary",) * len(grid)))(*args, *ex.arrays)
    return outs[:n_out], outs[n_out:]


def _gather_exchange(shards):
    n = len(shards)

    def copies(ins, outs, send_sems, recv_sems):
        x, y, c = _my_place()

        def half(k, chip, core):
            return _core_half(outs[k], core, lead=(_chip(chip),))

        def copy(k, slot, chip, core, to, src=None):
            return pltpu.make_async_remote_copy(
                src_ref=half(k, chip, core) if src is None else src, dst_ref=half(k, chip, core),
                send_sem=send_sems.at[6 * k + slot], recv_sem=recv_sems.at[6 * k + slot],
                device_id=to, device_id_type=MESH)

        return (x, y, c), copy

    def first_copies(ins, outs, send_sems, recv_sems):
        (x, y, c), copy = copies(ins, outs, send_sems, recv_sems)
        out = []
        for j, chip in enumerate(_other_chips(x, y)):
            for k in range(n):
                out.append(copy(k, j, (x, y), c, (*chip, c), src=_core_half(ins[k], c)))
        return out

    def start(ins, outs, send_sems, recv_sems):
        for cp in first_copies(ins, outs, send_sems, recv_sems):
            cp.start()

    def finish(ins, outs, send_sems, recv_sems):
        (x, y, c), copy = copies(ins, outs, send_sems, recv_sems)
        chips = _other_chips(x, y)
        passed = []
        for j, chip in enumerate(chips):
            for k in range(n):
                copy(k, j, chip, c, (x, y, c)).wait_recv()
                passed.append(copy(k, 3 + j, chip, c, (x, y, 1 - c)))
                passed[-1].start()
        for j, chip in enumerate(chips):
            for k in range(n):
                copy(k, 3 + j, chip, 1 - c, (x, y, c)).wait_recv()
        for cp in first_copies(ins, outs, send_sems, recv_sems) + passed:
            cp.wait_send()

    return _Exchange(list(shards), [jax.ShapeDtypeStruct((N_CHIPS,) + s.shape, s.dtype) for s in shards], 6 * n,
                     start, finish)


def _place_own(stacks, shards):
    me = 2 * lax.axis_index("x") + lax.axis_index("y")
    return [lax.dynamic_update_slice(others, mine[None], (me, 0, 0)) for others, mine in zip(stacks, shards)]


def _swap_exchange(grads):
    n = len(grads)

    def copies(ins, outs, send_sems, recv_sems):
        x, y, c = _my_place()
        return [pltpu.make_async_remote_copy(
            src_ref=_core_half(ins[k], 1 - c, lead=(slice(None),)), dst_ref=outs[k],
            send_sem=send_sems.at[k], recv_sem=recv_sems.at[k], device_id=(x, y, 1 - c), device_id_type=MESH)
            for k in range(n)]

    def start(ins, outs, send_sems, recv_sems):
        for cp in copies(ins, outs, send_sems, recv_sems):
            cp.start()

    def finish(ins, outs, send_sems, recv_sems):
        for cp in copies(ins, outs, send_sems, recv_sems):
            cp.wait()

    return _Exchange(list(grads), [jax.ShapeDtypeStruct((N_CHIPS,) + _half_shape(*g.shape[1:]), g.dtype) for g in grads],
                     n, start, finish)


def _chips_exchange(sums):
    n = len(sums)

    def sends(ins, outs, send_sems, recv_sems):
        x, y, c = _my_place()
        return [pltpu.make_async_remote_copy(
            src_ref=ins[k].at[_chip(chip)], dst_ref=outs[k].at[_chip((x, y))],
            send_sem=send_sems.at[3 * k + j], recv_sem=recv_sems.at[3 * k + j],
            device_id=(*chip, c), device_id_type=MESH)
            for j, chip in enumerate(_other_chips(x, y)) for k in range(n)]

    def start(ins, outs, send_sems, recv_sems):
        for cp in sends(ins, outs, send_sems, recv_sems):
            cp.start()

    def finish(ins, outs, send_sems, recv_sems):
        x, y, c = _my_place()
        for j, chip in enumerate(_other_chips(x, y)):
            for k in range(n):
                slot = outs[k].at[_chip(chip)]
                pltpu.make_async_remote_copy(
                    src_ref=slot, dst_ref=slot, send_sem=send_sems.at[3 * k + j], recv_sem=recv_sems.at[3 * k + j],
                    device_id=(x, y, c), device_id_type=MESH).wait_recv()
        for cp in sends(ins, outs, send_sems, recv_sems):
            cp.wait_send()

    return _Exchange(list(sums), [jax.ShapeDtypeStruct(s.shape, s.dtype) for s in sums], 3 * n, start, finish)


def _join_exchange(shards):
    n = len(shards)

    def sends(ins, outs, send_sems, recv_sems):
        x, y, c = _my_place()
        return [pltpu.make_async_remote_copy(
            src_ref=_core_half(ins[k], c), dst_ref=_core_half(outs[k], c),
            send_sem=send_sems.at[k], recv_sem=recv_sems.at[k], device_id=(x, y, 1 - c), device_id_type=MESH)
            for k in range(n)]

    def start(ins, outs, send_sems, recv_sems):
        for cp in sends(ins, outs, send_sems, recv_sems):
            cp.start()

    def finish(ins, outs, send_sems, recv_sems):
        x, y, c = _my_place()
        for k in range(n):
            theirs = _core_half(outs[k], 1 - c)
            pltpu.make_async_remote_copy(
                src_ref=theirs, dst_ref=theirs, send_sem=send_sems.at[k], recv_sem=recv_sems.at[k],
                device_id=(x, y, c), device_id_type=MESH).wait_recv()
        for cp in sends(ins, outs, send_sems, recv_sems):
            cp.wait_send()

    return _Exchange(list(shards), [jax.ShapeDtypeStruct(s.shape, s.dtype) for s in shards], n, start, finish,
                     in_place=True)


def _small_exchange(blocks):
    n = len(blocks)

    def copies(ins, outs, send_sems, recv_sems):
        x, y, c = _my_place()

        def copy(k, j, whose, to, src=None):
            slot = outs[k].at[4 * whose[0] + 2 * whose[1] + whose[2]]
            return pltpu.make_async_remote_copy(
                src_ref=slot if src is None else src, dst_ref=slot,
                send_sem=send_sems.at[7 * k + j], recv_sem=recv_sems.at[7 * k + j], device_id=to, device_id_type=MESH)

        return (x, y, c), copy

    def first_copies(ins, outs, send_sems, recv_sems):
        (x, y, c), copy = copies(ins, outs, send_sems, recv_sems)
        out = []
        for k in range(n):
            out.append(copy(k, 0, (x, y, c), (x, y, 1 - c), src=ins[k]))
            out += [copy(k, 1 + j, (x, y, c), (*chip, c), src=ins[k]) for j, chip in enumerate(_other_chips(x, y))]
        return out

    def start(ins, outs, send_sems, recv_sems):
        for cp in first_copies(ins, outs, send_sems, recv_sems):
            cp.start()

    def finish(ins, outs, send_sems, recv_sems):
        (x, y, c), copy = copies(ins, outs, send_sems, recv_sems)
        chips = _other_chips(x, y)
        passed = []
        for j, chip in enumerate(chips):
            for k in range(n):
                copy(k, 1 + j, (*chip, c), (x, y, c)).wait_recv()
                passed.append(copy(k, 4 + j, (*chip, c), (x, y, 1 - c)))
                passed[-1].start()
        for k in range(n):
            copy(k, 0, (x, y, 1 - c), (x, y, c)).wait_recv()
        for j, chip in enumerate(chips):
            for k in range(n):
                copy(k, 4 + j, (*chip, 1 - c), (x, y, c)).wait_recv()
        for cp in first_copies(ins, outs, send_sems, recv_sems) + passed:
            cp.wait_send()

    return _Exchange(list(blocks), [jax.ShapeDtypeStruct((N_DEV,) + blk.shape, blk.dtype) for blk in blocks], 7 * n,
                     start, finish)


def _sum_halves(grads, theirs, core, name):
    _, h, cols = theirs.shape
    by_rows = _by_rows(grads.shape[1])
    br = _rows_block(h, cols) if by_rows else h
    nb = h // br

    def body(core_ref, a_ref, b_ref, o_ref):
        o_ref[...] = (a_ref[...] + b_ref[...]).astype(BF16)

    if by_rows:
        mine = pl.BlockSpec((1, br, cols), lambda j, i, core_ref: (j, core_ref[0] * nb + i, 0))
    else:
        mine = pl.BlockSpec((1, br, cols), lambda j, i, core_ref: (j, i, core_ref[0]))
    return pl.pallas_call(
        body, name=name,
        out_shape=jax.ShapeDtypeStruct(theirs.shape, BF16),
        grid_spec=pltpu.PrefetchScalarGridSpec(
            num_scalar_prefetch=1, grid=(N_CHIPS, nb),
            in_specs=[mine, pl.BlockSpec((1, br, cols), lambda j, i, core_ref: (j, i, 0))],
            out_specs=pl.BlockSpec((1, br, cols), lambda j, i, core_ref: (j, i, 0))),
        compiler_params=_cparams(("parallel", "parallel")),
    )(core, grads, theirs)


def _sum_chips(slots, sums, place, by_rows, name):
    _, h, cols = slots.shape
    br = _rows_block(h, cols) if by_rows else h
    nb = h // br

    def body(place_ref, s_ref, own_ref, o_ref):
        me = place_ref[1]
        acc = None
        for k in range(N_CHIPS):
            term = jnp.where(me == k, own_ref[k], s_ref[k]).astype(F32)
            acc = term if acc is None else acc + term
        o_ref[...] = acc

    stack = pl.BlockSpec((N_CHIPS, br, cols), lambda i, place_ref: (0, i, 0))
    if by_rows:
        out_shape, out_map = (2 * h, cols), lambda i, place_ref: (place_ref[0] * nb + i, 0)
    else:
        out_shape, out_map = (h, 2 * cols), lambda i, place_ref: (i, place_ref[0])
    return pl.pallas_call(
        body, name=name,
        out_shape=jax.ShapeDtypeStruct(out_shape, F32),
        grid_spec=pltpu.PrefetchScalarGridSpec(
            num_scalar_prefetch=1, grid=(nb,), in_specs=[stack, stack],
            out_specs=pl.BlockSpec((br, cols), out_map)),
        compiler_params=_cparams(("parallel",)),
    )(place, slots, sums)


def _adamw_math(w, g, m, v):
    m = ADAM_B1 * m + (1.0 - ADAM_B1) * g
    v = ADAM_B2 * v + (1.0 - ADAM_B2) * (g * g)
    m_hat = m / (1.0 - ADAM_B1 ** ADAM_STEP)
    v_hat = v / (1.0 - ADAM_B2 ** ADAM_STEP)
    delta = -ADAM_LR * (m_hat / (jnp.sqrt(v_hat) + ADAM_EPS) + ADAM_WD * w)
    return delta, m, v


def _adamw(w, g, m, v, name, ex=None):
    def body(w_ref, g_ref, m_ref, v_ref, d_ref, nm_ref, nv_ref):
        d, nm, nv = _adamw_math(w_ref[...], g_ref[...], m_ref[...], v_ref[...])
        d_ref[...] = d
        nm_ref[...] = nm
        nv_ref[...] = nv

    if w.ndim == 3:
        rows = w.shape[0]
        br = max(b for b in range(1, 65) if rows % b == 0)
        spec, steps = pl.BlockSpec((br,) + w.shape[1:], lambda i: (i, 0, 0)), rows // br
    else:
        rows, cols = w.shape
        br = _rows_block(rows, cols, unit=8)
        spec, steps = pl.BlockSpec((br, cols), lambda i: (i, 0)), rows // br
    shape = jax.ShapeDtypeStruct(w.shape, F32)
    outs, moved = _hosted_call(
        body, ex, name=name, out_shape=(shape, shape, shape), grid=(steps,),
        in_specs=[spec] * 4, out_specs=(spec, spec, spec), args=(w, g, m, v))
    return tuple(outs) if ex is None else (tuple(outs), moved)


def _adamw_small(parts, own, ws, ms, vs, loss_parts, loss_own, device):
    n = len(ws)

    def total(device_ref, parts_ref, own_ref):
        acc = None
        for dev in range(N_DEV):
            term = jnp.where(device_ref[0] == dev, own_ref[...], parts_ref[dev])
            acc = term if acc is None else acc + term
        return acc

    def body(device_ref, *refs):
        ins, outs = refs[:5 * n + 2], refs[5 * n + 2:]
        for k in range(n):
            g = total(device_ref, ins[k], ins[n + k])
            d, nm, nv = _adamw_math(ins[2 * n + k][...], g, ins[3 * n + k][...], ins[4 * n + k][...])
            for o_ref, val in zip(outs[k::n][:4], (g, d, nm, nv)):
                o_ref[...] = val
        outs[4 * n][...] = total(device_ref, ins[5 * n], ins[5 * n + 1])

    args = list(parts) + list(own) + list(ws) + list(ms) + list(vs) + [loss_parts, loss_own]
    whole = lambda a: pl.BlockSpec(a.shape, lambda i, device_ref, nd=a.ndim: (0,) * nd)
    shapes = [jax.ShapeDtypeStruct(w.shape, F32) for w in ws] * 4 + [jax.ShapeDtypeStruct(loss_own.shape, F32)]
    outs = pl.pallas_call(
        body, name="adamw_small", out_shape=shapes,
        grid_spec=pltpu.PrefetchScalarGridSpec(
            num_scalar_prefetch=1, grid=(1,), in_specs=[whole(a) for a in args], out_specs=[whole(a) for a in shapes]),
        compiler_params=_cparams(("arbitrary",)),
    )(device, *args)
    return outs[:n], outs[n:2 * n], outs[2 * n:3 * n], outs[3 * n:4 * n], outs[4 * n]


def _mm(a, b, *, name, ta=False, out_dtype=F32, res=None, bm=1024, bn=1024, bk=4096, b_stack=False, out_stack=False,
        ex=None):
    if ta:
        kdim, m = a.shape
    else:
        m, kdim = a.shape
    if b_stack:
        _, kb, chunk = b.shape
        n = N_CHIPS * chunk
    else:
        kb, n = b.shape
        chunk = n // N_CHIPS if out_stack else n
    assert kdim == kb, (a.shape, b.shape, ta)
    bm = _block(m, bm, LANES if ta else 16)
    bn = _block(chunk, bn, LANES)
    bk = _block(kdim, bk, LANES)
    nk = kdim // bk
    per_chunk = chunk // bn
    dims = (((0 if ta else 1,), (0,)), ((), ()))

    def body(*refs):
        refs = list(refs)
        a_ref, b_ref = refs[:2]
        r_ref = refs[2] if res is not None else None
        o_ref = refs[3] if res is not None else refs[2]
        part = lax.dot_general(a_ref[...].astype(BF16), b_ref[...].astype(BF16), dims, preferred_element_type=F32)

        def finish(r):
            if r_ref is not None:
                r = r + r_ref[...]
            o_ref[...] = r.astype(out_dtype)

        if nk == 1:
            finish(part)
        else:
            acc_ref = refs[-1]
            k = pl.program_id(2)

            @pl.when(k == 0)
            def _():
                acc_ref[...] = part

            @pl.when(k > 0)
            def _():
                acc_ref[...] += part

            @pl.when(k == nk - 1)
            def _():
                finish(acc_ref[...])

    a_spec = pl.BlockSpec((bk, bm), lambda i, j, k: (k, i)) if ta else pl.BlockSpec((bm, bk), lambda i, j, k: (i, k))
    if b_stack:
        b_spec = pl.BlockSpec((None, bk, bn), lambda i, j, k: (j // per_chunk, k, j % per_chunk))
    else:
        b_spec = pl.BlockSpec((bk, bn), lambda i, j, k: (k, j))
    r_spec = pl.BlockSpec((bm, bn), lambda i, j, k: (i, j))
    if out_stack:
        o_spec = pl.BlockSpec((None, bm, bn), lambda i, j, k: (j // per_chunk, i, j % per_chunk))
        o_shape = (N_CHIPS, m, chunk)
    else:
        o_spec, o_shape = r_spec, (m, n)
    in_specs = [a_spec, b_spec] + ([r_spec] if res is not None else [])
    args = (a, b) + ((res,) if res is not None else ())
    (out,), moved = _hosted_call(
        body, ex, name=name, out_shape=(jax.ShapeDtypeStruct(o_shape, out_dtype),),
        grid=(m // bm, n // bn, nk), in_specs=in_specs, out_specs=(o_spec,),
        scratch=[pltpu.VMEM((bm, bn), F32)] if nk > 1 else [], args=args)
    return out if ex is None else (out, moved)


def _rms_fwd(x, g, name, ex=None):
    t, d = x.shape
    bt = _block(t, 512, 16)

    def body(x_ref, g_ref, h_ref):
        xv = x_ref[...]
        r = lax.rsqrt(jnp.mean(xv * xv, axis=-1, keepdims=True) + EPS)
        h_ref[...] = (xv * r * g_ref[...]).astype(BF16)

    (out,), moved = _hosted_call(
        body, ex, name=name, out_shape=(jax.ShapeDtypeStruct((t, d), BF16),), grid=(t // bt,),
        in_specs=[pl.BlockSpec((bt, d), lambda i: (i, 0)), pl.BlockSpec((1, d), lambda i: (0, 0))],
        out_specs=(pl.BlockSpec((bt, d), lambda i: (i, 0)),), args=(x, g))
    return out if ex is None else (out, moved)


def _rms_bwd(dh, x, g, dres, name, ex=None):
    t, d = x.shape
    bt = _block(t, 256, 16)
    want_dx = dres is not None

    def body(*refs):
        if want_dx:
            dh_ref, x_ref, g_ref, dres_ref, dx_ref, dxb_ref, dg_ref = refs
        else:
            dh_ref, x_ref, g_ref, dg_ref = refs
        xv = x_ref[...]
        r = lax.rsqrt(jnp.mean(xv * xv, axis=-1, keepdims=True) + EPS)
        xhat = xv * r
        dhv = dh_ref[...]

        @pl.when(pl.program_id(0) == 0)
        def _():
            dg_ref[...] = jnp.zeros_like(dg_ref)

        dg_ref[...] += jnp.sum(dhv * xhat, axis=0, keepdims=True)
        if want_dx:
            dxhat = dhv * g_ref[...]
            dx = dres_ref[...] + r * (dxhat - xhat * jnp.mean(dxhat * xhat, axis=-1, keepdims=True))
            dx_ref[...] = dx
            dxb_ref[...] = dx.astype(BF16)

    row = pl.BlockSpec((bt, d), lambda i: (i, 0))
    vec = pl.BlockSpec((1, d), lambda i: (0, 0))
    if want_dx:
        outs, moved = _hosted_call(
            body, ex, name=name, grid=(t // bt,),
            out_shape=(jax.ShapeDtypeStruct((t, d), F32), jax.ShapeDtypeStruct((t, d), BF16),
                       jax.ShapeDtypeStruct((1, d), F32)),
            in_specs=[row, row, vec, row], out_specs=(row, row, vec), args=(dh, x, g, dres))
        return tuple(outs) if ex is None else (tuple(outs), moved)
    return pl.pallas_call(
        body, name=name, grid=(t // bt,), out_shape=jax.ShapeDtypeStruct((1, d), F32),
        in_specs=[row, row, vec], out_specs=vec,
        compiler_params=_cparams(("arbitrary",)),
    )(dh, x, g)


def _in_proj_attn(h, w_qkv, w_f_lanes, w_f):
    t, d = h.shape
    bm = _block(t, 512, 16)

    def body(h_ref, wq_ref, wl_ref, wf_ref, qkv_ref, fl_ref, f_ref):
        hv = h_ref[...]
        qkv_ref[...] = jnp.dot(hv, wq_ref[...], preferred_element_type=F32).astype(BF16)
        fl_ref[...] = jnp.dot(hv, wl_ref[...], preferred_element_type=F32)
        f_ref[...] = jnp.dot(hv, wf_ref[...], preferred_element_type=F32)

    whole = lambda w: pl.BlockSpec(w.shape, lambda i: (0, 0))
    rows = lambda n: pl.BlockSpec((bm, n), lambda i: (i, 0))
    return pl.pallas_call(
        body, name="in_proj_attn", grid=(t // bm,),
        out_shape=(jax.ShapeDtypeStruct((t, w_qkv.shape[1]), BF16), jax.ShapeDtypeStruct((t, w_f_lanes.shape[1]), F32),
                   jax.ShapeDtypeStruct((t, w_f.shape[1]), F32)),
        in_specs=[rows(d), whole(w_qkv), whole(w_f_lanes), whole(w_f)],
        out_specs=(rows(w_qkv.shape[1]), rows(w_f_lanes.shape[1]), rows(w_f.shape[1])),
        compiler_params=_cparams(("parallel",)),
    )(h, w_qkv, w_f_lanes, w_f)


def _mm_res_norm(a, b, res, g, name):
    t, k = a.shape
    d = b.shape[1]
    bm = _block(t, 512, 16)

    def body(a_ref, b_ref, r_ref, g_ref, x_ref, h_ref):
        xv = jnp.dot(a_ref[...], b_ref[...], preferred_element_type=F32) + r_ref[...]
        x_ref[...] = xv
        r = lax.rsqrt(jnp.mean(xv * xv, axis=-1, keepdims=True) + EPS)
        h_ref[...] = (xv * r * g_ref[...]).astype(BF16)

    row = pl.BlockSpec((bm, d), lambda i: (i, 0))
    return pl.pallas_call(
        body, name=name, grid=(t // bm,),
        out_shape=(jax.ShapeDtypeStruct((t, d), F32), jax.ShapeDtypeStruct((t, d), BF16)),
        in_specs=[pl.BlockSpec((bm, k), lambda i: (i, 0)), pl.BlockSpec((k, d), lambda i: (0, 0)), row,
                  pl.BlockSpec((1, d), lambda i: (0, 0))],
        out_specs=(row, row), compiler_params=_cparams(("parallel",)),
    )(a, b, res, g)


def _ffn_out_loss(act, w, res, target, g):
    t, k = act.shape
    d = w.shape[1]
    bm = _block(t, 512, 16)

    def body(a_ref, w_ref, r_ref, t_ref, g_ref, dx_ref, dxb_ref, dg_ref, loss_ref):
        xv = jnp.dot(a_ref[...], w_ref[...], preferred_element_type=F32) + r_ref[...]
        gv = g_ref[...]
        r = lax.rsqrt(jnp.mean(xv * xv, axis=-1, keepdims=True) + EPS)
        xhat = xv * r
        err = xhat * gv - t_ref[...]

        @pl.when(pl.program_id(0) == 0)
        def _():
            dg_ref[...] = jnp.zeros_like(dg_ref)
            loss_ref[...] = jnp.zeros_like(loss_ref)

        loss_ref[...] += 0.5 * jnp.sum(jnp.mean(err * err, axis=-1, keepdims=True), axis=0, keepdims=True)
        dy = err * (1.0 / d)
        dg_ref[...] += jnp.sum(dy * xhat, axis=0, keepdims=True)
        dxhat = dy * gv
        dx = r * (dxhat - xhat * jnp.mean(dxhat * xhat, axis=-1, keepdims=True))
        dx_ref[...] = dx
        dxb_ref[...] = dx.astype(BF16)

    row = pl.BlockSpec((bm, d), lambda i: (i, 0))
    vec = pl.BlockSpec((1, d), lambda i: (0, 0))
    return pl.pallas_call(
        body, name="ffn_out_loss", grid=(t // bm,),
        out_shape=(jax.ShapeDtypeStruct((t, d), F32), jax.ShapeDtypeStruct((t, d), BF16),
                   jax.ShapeDtypeStruct((1, d), F32), jax.ShapeDtypeStruct((1, LANES), F32)),
        in_specs=[pl.BlockSpec((bm, k), lambda i: (i, 0)), pl.BlockSpec((k, d), lambda i: (0, 0)), row, row, vec],
        out_specs=(row, row, vec, pl.BlockSpec((1, LANES), lambda i: (0, 0))),
        compiler_params=_cparams(("arbitrary",)),
    )(act, w, res, target, g)


def _mm_norm_bwd(a, b, x, g, dres, name, ex=None):
    t, k = a.shape
    d = b.shape[1]
    bm = _block(t, 512 if k <= 2048 else 256, 16)

    def body(a_ref, b_ref, x_ref, g_ref, dres_ref, dx_ref, dxb_ref, dg_ref):
        @pl.when(pl.program_id(0) == 0)
        def _():
            dg_ref[...] = jnp.zeros_like(dg_ref)

        dhv = jnp.dot(a_ref[...], b_ref[...], preferred_element_type=F32)
        xv = x_ref[...]
        r = lax.rsqrt(jnp.mean(xv * xv, axis=-1, keepdims=True) + EPS)
        xhat = xv * r
        dg_ref[...] += jnp.sum(dhv * xhat, axis=0, keepdims=True)
        dxhat = dhv * g_ref[...]
        dx = dres_ref[...] + r * (dxhat - xhat * jnp.mean(dxhat * xhat, axis=-1, keepdims=True))
        dx_ref[...] = dx
        dxb_ref[...] = dx.astype(BF16)

    row = pl.BlockSpec((bm, d), lambda i: (i, 0))
    vec = pl.BlockSpec((1, d), lambda i: (0, 0))
    outs, moved = _hosted_call(
        body, ex, name=name, grid=(t // bm,),
        out_shape=(jax.ShapeDtypeStruct((t, d), F32), jax.ShapeDtypeStruct((t, d), BF16), jax.ShapeDtypeStruct((1, d), F32)),
        in_specs=[pl.BlockSpec((bm, k), lambda i: (i, 0)), pl.BlockSpec((k, d), lambda i: (0, 0)), row, vec, row],
        out_specs=(row, row, vec), args=(a, b, x, g, dres))
    return tuple(outs) if ex is None else (tuple(outs), moved)


GU_COLS = GATE_WIDTH + POOL_WIDTH
U_BLK = GATE_WIDTH // POOL_WIDTH


def _shift_down(a, k, row):
    return jnp.where(row >= k, pltpu.roll(a, k, 0), 0.0)


def _shift_up(a, k, row):
    n = a.shape[0]
    return jnp.where(row < n - k, pltpu.roll(a, n - k, 0), 0.0)


def _window_delta(u, w, row):
    s, k = u, 1
    while k < w:
        s = s + _shift_down(s, k, row)
        k *= 2
    cnt = jnp.minimum(row + 1, w).astype(F32)
    return s / cnt - u, cnt


def _pool_fwd(gu, pool_w, pool_scale):
    b, s, _ = gu.shape

    def body(u_ref, pw_ref, sc_ref, y_ref):
        row = lax.broadcasted_iota(jnp.int32, (s, POOL_GC), 0)
        for g, w in enumerate(POOL_WINDOWS):
            cols = slice(g * POOL_GC, (g + 1) * POOL_GC)
            d, _ = _window_delta(u_ref[0, :, cols].astype(F32), w, row)
            z = jnp.dot(d.astype(BF16), pw_ref[g].astype(BF16), preferred_element_type=F32)
            y_ref[0, :, cols] = (z * sc_ref[:, cols]).astype(BF16)

    return pl.pallas_call(
        body, name="pool_fwd", out_shape=jax.ShapeDtypeStruct((b, s, POOL_WIDTH), BF16), grid=(b,),
        in_specs=[pl.BlockSpec((1, s, POOL_WIDTH), lambda i: (i, 0, U_BLK)),
                  pl.BlockSpec((4, POOL_GC, POOL_GC), lambda i: (0, 0, 0)),
                  pl.BlockSpec((1, POOL_WIDTH), lambda i: (0, 0))],
        out_specs=pl.BlockSpec((1, s, POOL_WIDTH), lambda i: (i, 0, 0)),
        compiler_params=_cparams(("parallel",)),
    )(gu, pool_w, pool_scale)


def _pool_bwd(gu, dyp, w_out_t, pool_w, pool_scale, dgu):
    b, s, _ = gu.shape

    def body(u_ref, dyp_ref, w_ref, pw_ref, sc_ref, dgu_in, du_ref, dpw_ref, dsc_ref):
        del dgu_in

        @pl.when(pl.program_id(0) == 0)
        def _():
            dpw_ref[...] = jnp.zeros_like(dpw_ref)
            dsc_ref[...] = jnp.zeros_like(dsc_ref)

        row = lax.broadcasted_iota(jnp.int32, (s, POOL_GC), 0)
        for g, w in enumerate(POOL_WINDOWS):
            cols = slice(g * POOL_GC, (g + 1) * POOL_GC)
            d, cnt = _window_delta(u_ref[0, :, cols].astype(F32), w, row)
            db = d.astype(BF16)
            pw = pw_ref[g].astype(BF16)
            z = jnp.dot(db, pw, preferred_element_type=F32)
            dyv = jnp.dot(dyp_ref[0], w_ref[:, cols], preferred_element_type=F32)
            dsc_ref[:, cols] += jnp.sum(dyv * z, axis=0, keepdims=True)
            dz = (dyv * sc_ref[:, cols]).astype(BF16)
            dpw_ref[g] += lax.dot_general(db, dz, (((0,), (0,)), ((), ())), preferred_element_type=F32)
            dd = lax.dot_general(dz, pw, (((1,), (1,)), ((), ())), preferred_element_type=F32)
            acc, k = dd / cnt, 1
            while k < w:
                acc = acc + _shift_up(acc, k, row)
                k *= 2
            du_ref[0, :, cols] = (acc - dd).astype(BF16)

    return pl.pallas_call(
        body, name="pool_bwd", grid=(b,),
        out_shape=(jax.ShapeDtypeStruct((b, s, GU_COLS), BF16), jax.ShapeDtypeStruct((4, POOL_GC, POOL_GC), F32),
                   jax.ShapeDtypeStruct((1, POOL_WIDTH), F32)),
        in_specs=[pl.BlockSpec((1, s, POOL_WIDTH), lambda i: (i, 0, U_BLK)),
                  pl.BlockSpec((1, s, dyp.shape[2]), lambda i: (i, 0, 0)),
                  pl.BlockSpec(w_out_t.shape, lambda i: (0, 0)),
                  pl.BlockSpec((4, POOL_GC, POOL_GC), lambda i: (0, 0, 0)),
                  pl.BlockSpec((1, POOL_WIDTH), lambda i: (0, 0)), ANY],
        out_specs=(pl.BlockSpec((1, s, POOL_WIDTH), lambda i: (i, 0, U_BLK)),
                   pl.BlockSpec((4, POOL_GC, POOL_GC), lambda i: (0, 0, 0)),
                   pl.BlockSpec((1, POOL_WIDTH), lambda i: (0, 0))),
        input_output_aliases={5: 0},
        compiler_params=_cparams(("arbitrary",)),
    )(gu, dyp, w_out_t, pool_w, pool_scale, dgu)


def _forget_cumsum(f, bias, name):
    b, s, c = f.shape

    def body(f_ref, b_ref, c_ref):
        row = lax.broadcasted_iota(jnp.int32, (s, LANES), 0)
        z = f_ref[0] + b_ref[...]
        acc = jnp.minimum(z, 0.0) - jnp.log(1.0 + jnp.exp(-jnp.abs(z)))
        k = 1
        while k < s:
            acc = acc + _shift_down(acc, k, row)
            k *= 2
        c_ref[0] = acc

    return pl.pallas_call(
        body, name=name, out_shape=jax.ShapeDtypeStruct((b, s, c), F32), grid=(b, c // LANES),
        in_specs=[pl.BlockSpec((1, s, LANES), lambda i, j: (i, 0, j)), pl.BlockSpec((1, LANES), lambda i, j: (0, j))],
        out_specs=pl.BlockSpec((1, s, LANES), lambda i, j: (i, 0, j)),
        compiler_params=_cparams(("parallel", "parallel")),
    )(f, bias)


def _forget_bwd(dc, f, bias):
    b, s, _ = f.shape

    def body(dc_ref, f_ref, b_ref, df_ref, db_ref):
        @pl.when(pl.program_id(0) == 0)
        def _():
            db_ref[...] = jnp.zeros_like(db_ref)

        row = lax.broadcasted_iota(jnp.int32, (s, LANES), 0)
        acc, k = dc_ref[0], 1
        while k < s:
            acc = acc + _shift_up(acc, k, row)
            k *= 2
        z = f_ref[0] + b_ref[...]
        df = acc / (1.0 + jnp.exp(z))
        db_ref[...] += jnp.sum(df, axis=0, keepdims=True)
        df_ref[0] = df.astype(BF16)

    blk = pl.BlockSpec((1, s, LANES), lambda i: (i, 0, 0))
    vec = pl.BlockSpec((1, LANES), lambda i: (0, 0))
    return pl.pallas_call(
        body, name="forget_bwd", grid=(b,),
        out_shape=(jax.ShapeDtypeStruct((b, s, LANES), BF16), jax.ShapeDtypeStruct((1, LANES), F32)),
        in_specs=[blk, blk, vec], out_specs=(blk, vec),
        compiler_params=_cparams(("arbitrary",)),
    )(dc, f, bias)


KV_BLK0 = 2
PAIRS = FOX_HEADS // 2
FOX_SCALE = FOX_DH ** -0.5
NT_DIMS = (((1,), (1,)), ((), ()))
TN_DIMS = (((0,), (0,)), ((), ()))


def _stack_heads(v):
    head = lax.broadcasted_iota(jnp.int32, v.shape, 1) // FOX_DH
    zero = jnp.zeros_like(v)
    return jnp.concatenate([jnp.where(head == 0, v, zero), jnp.where(head == 1, v, zero)], axis=0)


def _stack_cols(v):
    return jnp.concatenate([v[:, 0:1], v[:, FOX_DH:FOX_DH + 1]], axis=0)


def _unstack(t, blk):
    head = lax.broadcasted_iota(jnp.int32, (blk, LANES), 1) // FOX_DH
    return jnp.where(head == 0, t[:blk], t[blk:])


def _fox_scores(q_all, kblk, row_bias, cr_ref, kb, masked, blk):
    top = lax.broadcasted_iota(jnp.int32, (2 * blk, 1), 0) < blk
    s = lax.dot_general(q_all, kblk, NT_DIMS, preferred_element_type=F32)
    s = s + (row_bias - jnp.where(top, cr_ref[0, 0, kb], cr_ref[0, 1, kb]))
    if masked:
        r = lax.broadcasted_iota(jnp.int32, (2 * blk, blk), 0)
        keep = jnp.where(r >= blk, r - blk, r) >= lax.broadcasted_iota(jnp.int32, (2 * blk, blk), 1)
        s = jnp.where(keep, s, NEG_INF)
    return s


def _fox_fwd(qkv, c_exp, c_row, ex=None):
    b, s, _ = qkv.shape
    blk = min(ATT_BLOCK, s)
    nq = s // blk

    def body(q_ref, kv_ref, cc_ref, cr_ref, o_ref, ob_ref, lse_ref):
        qi = pl.program_id(2)
        q_all = _stack_heads(q_ref[0] * FOX_SCALE)
        cq = _stack_cols(cc_ref[0])

        def step(kb, carry, masked):
            m, l, acc = carry
            rows = pl.ds(pl.multiple_of(kb * blk, blk), blk)
            sc = _fox_scores(q_all, kv_ref[0, rows, :LANES], cq, cr_ref, kb, masked, blk)
            m_new = jnp.maximum(m, jnp.max(sc, axis=-1, keepdims=True))
            p = jnp.exp(sc - m_new)
            alpha = jnp.exp(m - m_new)
            l = alpha * l + jnp.sum(p, axis=-1, keepdims=True)
            acc = alpha * acc + jnp.dot(p.astype(BF16), kv_ref[0, rows, LANES:], preferred_element_type=F32)
            return m_new, l, acc

        init = (jnp.full((2 * blk, 1), NEG_INF, F32), jnp.zeros((2 * blk, 1), F32), jnp.zeros((2 * blk, LANES), F32))
        m, l, acc = step(qi, lax.fori_loop(0, qi, functools.partial(step, masked=False), init), True)
        o = _unstack(acc / l, blk)
        o_ref[0] = o
        ob_ref[0] = o.astype(BF16)
        lse_ref[0] = _unstack(jnp.broadcast_to(m + jnp.log(l), (2 * blk, LANES)), blk)

    tile = pl.BlockSpec((1, blk, LANES), lambda i, h, q: (i, q, h))
    kvspec = pl.BlockSpec((1, s, 2 * LANES), lambda i, h, q: (i, 0, KV_BLK0 + h))
    shape = jax.ShapeDtypeStruct((b, s, FOX_WIDTH), F32)
    return _hosted_call(
        body, ex, name="fox_fwd", out_shape=(shape, jax.ShapeDtypeStruct((b, s, FOX_WIDTH), BF16), shape),
        grid=(b, PAIRS, nq),
        in_specs=[tile, kvspec, tile, pl.BlockSpec((1, 2, nq, 1, blk), lambda i, h, q: (i, h, 0, 0, 0))],
        out_specs=(tile, tile, tile), args=(qkv, qkv, c_exp, c_row))


def _fox_bwd(qkv, c_exp, c_row, lse, o, do, ex=None):
    b, s, _ = qkv.shape
    blk = min(ATT_BLOCK, s)
    nq = s // blk

    def body(q_ref, kv_ref, cc_ref, cr_ref, lse_ref, o_ref, do_ref, dq_ref, dkv_ref, dcq_ref, dc_ref, dk_acc, dv_acc):
        qi = pl.program_id(2)

        @pl.when(qi == 0)
        def _():
            dk_acc[...] = jnp.zeros_like(dk_acc)
            dv_acc[...] = jnp.zeros_like(dv_acc)
            dc_ref[...] = jnp.zeros_like(dc_ref)

        q_all = _stack_heads(q_ref[0] * FOX_SCALE)
        dov = do_ref[0]
        do_all = _stack_heads(dov.astype(BF16))
        delta = jnp.sum(_stack_heads(dov * o_ref[0]), axis=-1, keepdims=True)
        bias = _stack_cols(cc_ref[0]) - _stack_cols(lse_ref[0])

        def step(kb, carry, masked):
            acc, dcq = carry
            rows = pl.ds(pl.multiple_of(kb * blk, blk), blk)
            kblk = kv_ref[0, rows, :LANES]
            p = jnp.exp(_fox_scores(q_all, kblk, bias, cr_ref, kb, masked, blk))
            dp = lax.dot_general(do_all, kv_ref[0, rows, LANES:], NT_DIMS, preferred_element_type=F32)
            ds = p * (dp - delta)
            dsb = ds.astype(BF16)
            dv_acc[rows, :] += lax.dot_general(p.astype(BF16), do_all, TN_DIMS, preferred_element_type=F32)
            dk_acc[rows, :] += lax.dot_general(dsb, q_all, TN_DIMS, preferred_element_type=F32)
            dc_ref[0, 0, kb] -= jnp.sum(ds[:blk], axis=0, keepdims=True)
            dc_ref[0, 1, kb] -= jnp.sum(ds[blk:], axis=0, keepdims=True)
            acc = acc + jnp.dot(dsb, kblk, preferred_element_type=F32)
            return acc, dcq + jnp.sum(ds, axis=-1, keepdims=True)

        init = (jnp.zeros((2 * blk, LANES), F32), jnp.zeros((2 * blk, 1), F32))
        acc, dcq = step(qi, lax.fori_loop(0, qi, functools.partial(step, masked=False), init), True)
        dq_ref[0] = (_unstack(acc, blk) * FOX_SCALE).astype(BF16)
        dcq_ref[0, 0] = jnp.where(lax.broadcasted_iota(jnp.int32, (blk, 2), 1) == 0, dcq[:blk], dcq[blk:])

        @pl.when(qi == nq - 1)
        def _():
            dkv_ref[0, :, :LANES] = dk_acc[...].astype(BF16)
            dkv_ref[0, :, LANES:] = dv_acc[...].astype(BF16)

    tile = pl.BlockSpec((1, blk, LANES), lambda i, h, q: (i, q, h))
    kvspec = pl.BlockSpec((1, s, 2 * LANES), lambda i, h, q: (i, 0, KV_BLK0 + h))
    crow = pl.BlockSpec((1, 2, nq, 1, blk), lambda i, h, q: (i, h, 0, 0, 0))
    return _hosted_call(
        body, ex, name="fox_bwd", grid=(b, PAIRS, nq),
        out_shape=(jax.ShapeDtypeStruct((b, s, FOX_WIDTH), BF16), jax.ShapeDtypeStruct((b, s, 2 * FOX_WIDTH), BF16),
                   jax.ShapeDtypeStruct((b, PAIRS, s, 2), F32), jax.ShapeDtypeStruct(c_row.shape, F32)),
        in_specs=[tile, kvspec, tile, crow, tile, tile, tile],
        out_specs=(tile, pl.BlockSpec((1, s, 2 * LANES), lambda i, h, q: (i, 0, h)),
                   pl.BlockSpec((1, 1, blk, 2), lambda i, h, q: (i, h, q, 0)), crow),
        scratch=[pltpu.VMEM((s, LANES), F32), pltpu.VMEM((s, LANES), F32)],
        args=(qkv, qkv, c_exp, c_row, lse, o, do))


def _sigmoid(z):
    return 1.0 / (1.0 + jnp.exp(-z))


def _branches_mix(y, o, w_pool3, w_fox3, gu, b_gate):
    t = y.shape[0]
    chunk = w_pool3.shape[2]
    per_branch = D_MODEL // chunk
    bm = _block(t, 1024, 16)

    def body(y_ref, o_ref, wp_ref, wf_ref, gp_ref, gf_ref, bp_ref, bf_ref, yp_ref, yf_ref, mix_ref):
        yp = jnp.dot(y_ref[...], wp_ref[...], preferred_element_type=F32).astype(BF16)
        yf = jnp.dot(o_ref[...], wf_ref[...], preferred_element_type=F32).astype(BF16)
        yp_ref[...] = yp
        yf_ref[...] = yf
        gp = _sigmoid(gp_ref[...].astype(F32) + bp_ref[...])
        gf = _sigmoid(gf_ref[...].astype(F32) + bf_ref[...])
        mix_ref[...] = (gp * yp.astype(F32) + gf * yf.astype(F32)).astype(BF16)

    rows = pl.BlockSpec((bm, y.shape[1]), lambda i, j: (i, 0))
    weight = pl.BlockSpec((None, y.shape[1], chunk), lambda i, j: (j, 0, 0))
    tile = lambda base: pl.BlockSpec((bm, chunk), lambda i, j: (i, base + j))
    vec = lambda base: pl.BlockSpec((1, chunk), lambda i, j: (0, base + j))
    shape = jax.ShapeDtypeStruct((t, D_MODEL), BF16)
    return pl.pallas_call(
        body, name="branches_mix", out_shape=(shape, shape, shape), grid=(t // bm, per_branch),
        in_specs=[rows, rows, weight, weight, tile(0), tile(per_branch), vec(0), vec(per_branch)],
        out_specs=(tile(0), tile(0), tile(0)),
        compiler_params=_cparams(("parallel", "arbitrary")),
    )(y, o, w_pool3, w_fox3, gu, gu, b_gate, b_gate)


def _mix_bwd(gu, b_gate, y_pool, y_fox, dx, w_out_t):
    t = gu.shape[0]
    bt = _block(t, 256, 16)

    def body(gp_ref, gf_ref, bp_ref, bf_ref, yp_ref, yf_ref, dx_ref, w_ref, dyp_ref, dyf_ref, dgl_ref, db_ref):
        @pl.when(pl.program_id(0) == 0)
        def _():
            db_ref[...] = jnp.zeros_like(db_ref)

        dm = jnp.dot(dx_ref[...], w_ref[...], preferred_element_type=F32)
        gp = _sigmoid(gp_ref[...].astype(F32) + bp_ref[...])
        gf = _sigmoid(gf_ref[...].astype(F32) + bf_ref[...])
        dyp_ref[...] = (dm * gp).astype(BF16)
        dyf_ref[...] = (dm * gf).astype(BF16)
        dlp = dm * yp_ref[...].astype(F32) * gp * (1.0 - gp)
        dlf = dm * yf_ref[...].astype(F32) * gf * (1.0 - gf)
        dgl_ref[:, :D_MODEL] = dlp.astype(BF16)
        dgl_ref[:, D_MODEL:] = dlf.astype(BF16)
        db_ref[:, :D_MODEL] += jnp.sum(dlp, axis=0, keepdims=True)
        db_ref[:, D_MODEL:] += jnp.sum(dlf, axis=0, keepdims=True)

    col = lambda j: pl.BlockSpec((bt, D_MODEL), lambda i: (i, j))
    vec = lambda j: pl.BlockSpec((1, D_MODEL), lambda i: (0, j))
    wide = pl.BlockSpec((bt, GATE_WIDTH), lambda i: (i, 0))
    return pl.pallas_call(
        body, name="mix_bwd", grid=(t // bt,),
        out_shape=(jax.ShapeDtypeStruct((t, D_MODEL), BF16), jax.ShapeDtypeStruct((t, D_MODEL), BF16),
                   jax.ShapeDtypeStruct((t, GU_COLS), BF16), jax.ShapeDtypeStruct((1, GATE_WIDTH), F32)),
        in_specs=[col(0), col(1), vec(0), vec(1), col(0), col(0), col(0),
                  pl.BlockSpec(w_out_t.shape, lambda i: (0, 0))],
        out_specs=(col(0), col(0), wide, pl.BlockSpec((1, GATE_WIDTH), lambda i: (0, 0))),
        compiler_params=_cparams(("arbitrary",)),
    )(gu, gu, b_gate, b_gate, y_pool, y_fox, dx, w_out_t)


X_SCALE = X_DH ** -0.5


def _xattn_probs(qh, kh):
    s = lax.dot_general(qh, kh, NT_DIMS, preferred_element_type=F32) * X_SCALE
    e = jnp.exp(s - jnp.max(s, axis=-1, keepdims=True))
    return e / jnp.sum(e, axis=-1, keepdims=True)


def _xattn_fwd(q, kv):
    b, s, _ = q.shape
    m = kv.shape[1]
    bq = _block(s, 512, 16)

    def body(q_ref, kv_ref, o_ref):
        for h in range(X_HEADS):
            cols = slice(h * X_DH, (h + 1) * X_DH)
            p = _xattn_probs(q_ref[0, :, cols], kv_ref[0, :, cols])
            vh = kv_ref[0, :, X_WIDTH + h * X_DH:X_WIDTH + (h + 1) * X_DH]
            o_ref[0, :, cols] = jnp.dot(p.astype(BF16), vh, preferred_element_type=F32).astype(BF16)

    return pl.pallas_call(
        body, name="xattn_fwd", out_shape=jax.ShapeDtypeStruct((b, s, X_WIDTH), BF16), grid=(b, s // bq),
        in_specs=[pl.BlockSpec((1, bq, X_WIDTH), lambda i, j: (i, j, 0)),
                  pl.BlockSpec((1, m, 2 * X_WIDTH), lambda i, j: (i, 0, 0))],
        out_specs=pl.BlockSpec((1, bq, X_WIDTH), lambda i, j: (i, j, 0)),
        compiler_params=_cparams(("parallel", "parallel")),
    )(q, kv)


def _xattn_bwd(q, kv, dx, w_o_t):
    b, s, _ = q.shape
    m = kv.shape[1]
    bq = _block(s, 512, 16)

    def body(q_ref, kv_ref, dx_ref, w_ref, dq_ref, dkv_ref):
        @pl.when(pl.program_id(1) == 0)
        def _():
            dkv_ref[...] = jnp.zeros_like(dkv_ref)

        do = jnp.dot(dx_ref[0], w_ref[...], preferred_element_type=F32).astype(BF16)
        for h in range(X_HEADS):
            cols = slice(h * X_DH, (h + 1) * X_DH)
            vcols = slice(X_WIDTH + h * X_DH, X_WIDTH + (h + 1) * X_DH)
            qh, kh, vh, doh = q_ref[0, :, cols], kv_ref[0, :, cols], kv_ref[0, :, vcols], do[:, cols]
            p = _xattn_probs(qh, kh)
            dkv_ref[0, :, vcols] += lax.dot_general(p.astype(BF16), doh, TN_DIMS, preferred_element_type=F32)
            dp = lax.dot_general(doh, vh, NT_DIMS, preferred_element_type=F32)
            ds = (p * (dp - jnp.sum(p * dp, axis=-1, keepdims=True)) * X_SCALE).astype(BF16)
            dq_ref[0, :, cols] = jnp.dot(ds, kh, preferred_element_type=F32).astype(BF16)
            dkv_ref[0, :, cols] += lax.dot_general(ds, qh, TN_DIMS, preferred_element_type=F32)

    tile = pl.BlockSpec((1, bq, X_WIDTH), lambda i, j: (i, j, 0))
    mem = pl.BlockSpec((1, m, 2 * X_WIDTH), lambda i, j: (i, 0, 0))
    return pl.pallas_call(
        body, name="xattn_bwd", grid=(b, s // bq),
        out_shape=(jax.ShapeDtypeStruct((b, s, X_WIDTH), BF16), jax.ShapeDtypeStruct((b, m, 2 * X_WIDTH), F32)),
        in_specs=[tile, mem, pl.BlockSpec((1, bq, dx.shape[2]), lambda i, j: (i, j, 0)),
                  pl.BlockSpec(w_o_t.shape, lambda i, j: (0, 0))],
        out_specs=(tile, mem),
        compiler_params=_cparams(("parallel", "arbitrary")),
    )(q, kv, dx, w_o_t)


def _ffn_in(hf, w3):
    t, d = hf.shape
    chunk = w3.shape[2]
    half = N_CHIPS // 2
    bm = _block(t, 1024, 16)

    def body(a_ref, wg_ref, wu_ref, gt_ref, up_ref, act_ref):
        a = a_ref[...]
        gt = jnp.dot(a, wg_ref[...], preferred_element_type=F32).astype(BF16)
        up = jnp.dot(a, wu_ref[...], preferred_element_type=F32).astype(BF16)
        gt_ref[...] = gt
        up_ref[...] = up
        g32 = gt.astype(F32)
        act_ref[...] = (g32 * _sigmoid(g32) * up.astype(F32)).astype(BF16)

    tile = pl.BlockSpec((bm, chunk), lambda i, j: (i, j))
    shape = jax.ShapeDtypeStruct((t, half * chunk), BF16)
    return pl.pallas_call(
        body, name="ffn_in", out_shape=(shape, shape, shape), grid=(t // bm, half),
        in_specs=[pl.BlockSpec((bm, d), lambda i, j: (i, 0)),
                  pl.BlockSpec((None, d, chunk), lambda i, j: (j, 0, 0)),
                  pl.BlockSpec((None, d, chunk), lambda i, j: (j + half, 0, 0))],
        out_specs=(tile, tile, tile),
        compiler_params=_cparams(("parallel", "arbitrary")),
    )(hf, w3, w3)


def _ffn_act_bwd(dx, w_out_t, gate, up):
    t, d = dx.shape
    bt = _block(t, 256, 16)

    def body(dx_ref, w_ref, gt_ref, up_ref, o_ref):
        da = jnp.dot(dx_ref[...], w_ref[...], preferred_element_type=F32).astype(BF16).astype(F32)
        gt = gt_ref[...].astype(F32)
        sg = _sigmoid(gt)
        silu = gt * sg
        o_ref[:, :D_FF] = (da * up_ref[...].astype(F32) * (sg + silu * (1.0 - sg))).astype(BF16)
        o_ref[:, D_FF:] = (da * silu).astype(BF16)

    col = pl.BlockSpec((bt, D_FF), lambda i: (i, 0))
    return pl.pallas_call(
        body, name="ffn_act_bwd", out_shape=jax.ShapeDtypeStruct((t, 2 * D_FF), BF16), grid=(t // bt,),
        in_specs=[pl.BlockSpec((bt, d), lambda i: (i, 0)), pl.BlockSpec((d, D_FF), lambda i: (0, 0)), col, col],
        out_specs=pl.BlockSpec((bt, 2 * D_FF), lambda i: (i, 0)),
        compiler_params=_cparams(("parallel",)),
    )(dx, w_out_t, gate, up)


def _stack_of(w, axis):
    r, c = w.shape
    if axis == 0:
        return w.reshape(N_CHIPS, r // N_CHIPS, c)
    return w.reshape(r, N_CHIPS, c // N_CHIPS).transpose(1, 0, 2)


def _stack_t(w3):
    n, r, c = w3.shape
    return w3.transpose(0, 2, 1).reshape(n * c, r)


def _pair_rows(k, v):
    c = k.shape[1]
    return jnp.stack([k.reshape(PAIRS, LANES, c), v.reshape(PAIRS, LANES, c)], axis=1).reshape(2 * FOX_WIDTH, c)


def _unpair_rows(kv):
    c = kv.shape[1]
    kv = kv.reshape(PAIRS, 2, LANES, c)
    return kv[:, 0].reshape(FOX_WIDTH, c), kv[:, 1].reshape(FOX_WIDTH, c)


def _input_grad(parts, weights_t, ex):
    t = parts[0].shape[0]
    d = weights_t[0].shape[1]
    bm = _block(t, 512, 16)
    n = len(parts)

    def body(*refs):
        acc = None
        for a_ref, b_ref in zip(refs[:n], refs[n:2 * n]):
            term = jnp.dot(a_ref[...], b_ref[...], preferred_element_type=F32)
            acc = term if acc is None else acc + term
        refs[2 * n][...] = acc

    (out,), moved = _hosted_call(
        body, ex, name="d_h", grid=(t // bm,), out_shape=(jax.ShapeDtypeStruct((t, d), F32),),
        in_specs=[pl.BlockSpec((bm, p.shape[1]), lambda i: (i, 0)) for p in parts]
        + [pl.BlockSpec(w.shape, lambda i: (0, 0)) for w in weights_t],
        out_specs=(pl.BlockSpec((bm, d), lambda i: (i, 0)),), args=tuple(parts) + tuple(weights_t))
    return out, moved


def _step(x, mem, loss_target, weights, moments_m, moments_v):
    nb, s, d = x.shape
    n_mem = mem.shape[1]
    t = nb * s
    blk = min(ATT_BLOCK, s)
    x2 = x.reshape(t, d)
    mem2 = mem.reshape(nb * n_mem, d)
    tgt2 = loss_target.reshape(t, d)

    def shard2d(a, n):
        a = a.reshape(a.shape[1:])
        return a.T if n == "w_in" else a

    def unshard(a, n):
        return (a.T if n == "w_in" else a)[None]

    local = {n: shard2d(weights[n], n) for n, _, _ in SHARDED}

    names = [n for n, _, _ in SHARDED]
    last = ["w_ffn_out"]
    later = [n for n in names if n != "w_in" and n not in last]
    local_b = {n: local[n].astype(BF16) for n in names}
    g_mix = weights["norm_mix_g"]
    h, w_in_others = _rms_fwd(x2, g_mix, "norm_mix", ex=_gather_exchange([local_b["w_in"]]))
    w_in_stack, = _place_own(w_in_others, [local_b["w_in"]])

    def w_in_rows(lo, hi):
        per = IN_COLS // N_CHIPS
        parts = [w_in_stack[j, max(lo, j * per) - j * per:min(hi, (j + 1) * per) - j * per]
                 for j in range(N_CHIPS) if max(lo, j * per) < min(hi, (j + 1) * per)]
        return parts[0] if len(parts) == 1 else jnp.concatenate(parts)

    w_gu_t = jnp.concatenate([w_in_rows(2056, IN_COLS), w_in_rows(0, 512)])
    w_qkv_t = jnp.concatenate([w_in_rows(512, 1024), _pair_rows(w_in_rows(1024, 1536), w_in_rows(1536, 2048))])
    w_f_t = jnp.pad(w_in_rows(2048, 2056), ((0, LANES - FOX_HEADS), (0, 0)))
    w_gu, w_qkv, w_f = w_gu_t.T, w_qkv_t.T, w_f_t.T
    w_f_exp = jnp.repeat(w_f[:, :FOX_HEADS], FOX_DH, axis=1)

    g_mix, g_x, g_mem, g_ffn = (weights[n] for n in ("norm_mix_g", "norm_x_g", "norm_mem_g", "norm_ffn_g"))
    g_final = weights["norm_final_g"].reshape(1, d)
    pool_w = weights["pool_w"].reshape(4, POOL_GC, POOL_GC)
    pool_scale, b_gate = weights["pool_scale"], weights["b_gate"]
    b_f_pad = jnp.pad(weights["b_forget"], ((0, 0), (0, LANES - FOX_HEADS)))
    b_f_exp = jnp.repeat(weights["b_forget"], FOX_DH, axis=1)

    gu, last_others = _mm(h, w_gu, out_dtype=BF16, bn=512, name="in_proj_gates_pool",
                          ex=_gather_exchange([local_b[n] for n in last]))
    qkv, f_exp, f_pad = _in_proj_attn(h, w_qkv, w_f_exp, w_f)
    gu3, qkv3 = gu.reshape(nb, s, GU_COLS), qkv.reshape(nb, s, 3 * FOX_WIDTH)
    y = _pool_fwd(gu3, pool_w, pool_scale)
    c_exp = _forget_cumsum(f_exp.reshape(nb, s, FOX_WIDTH), b_f_exp, "forget_cumsum_lanes")
    c_pad = _forget_cumsum(f_pad.reshape(nb, s, LANES), b_f_pad, "forget_cumsum")
    c_row = c_pad[:, :, :FOX_HEADS].transpose(0, 2, 1).reshape(nb, FOX_HEADS, s // blk, 1, blk)
    (o, o_b, lse), gathered = _fox_fwd(qkv3, c_exp, c_row, ex=_gather_exchange([local_b[n] for n in later]))
    stacks = dict(zip(later, _place_own(gathered, [local_b[n] for n in later])))
    stacks.update(zip(last, _place_own(last_others, [local_b[n] for n in last])))
    w_pool_out3, w_fox_out3, w_xo3, w_ffn_in3 = (stacks[n] for n in ("w_pool_out", "w_fox_out", "w_xo", "w_ffn_in"))
    w_out, w_xq, w_xkv, w_ffn_out = (stacks[n].reshape(-1, stacks[n].shape[2])
                                     for n in ("w_out", "w_xq", "w_xkv", "w_ffn_out"))
    y2, o2 = y.reshape(t, POOL_WIDTH), o_b.reshape(t, FOX_WIDTH)
    y_pool, y_fox, mix = _branches_mix(y2, o2, w_pool_out3, w_fox_out3, gu, b_gate)
    x1, hx = _mm_res_norm(mix, w_out, x2, g_x, "mix_out_norm_x")
    mem_n = _rms_fwd(mem2, g_mem, "norm_mem")
    qx = _mm(hx, w_xq, out_dtype=BF16, name="x_q")
    kv = _mm(mem_n, w_xkv, out_dtype=BF16, name="x_kv")
    qx3, kv3 = qx.reshape(nb, s, X_WIDTH), kv.reshape(nb, n_mem, 2 * X_WIDTH)
    ox = _xattn_fwd(qx3, kv3).reshape(t, X_WIDTH)
    w_xo = w_xo3.transpose(1, 0, 2).reshape(X_WIDTH, D_MODEL)
    x2_, hf = _mm_res_norm(ox, w_xo, x1, g_ffn, "x_out_norm_ffn")
    ffn_gate, ffn_up, act = _ffn_in(hf, w_ffn_in3)

    dx3, dx3_b, dg_final, loss_part = _ffn_out_loss(act, w_ffn_out, x2_, tgt2, g_final)
    dw_ffn_out = _mm(act, dx3_b, ta=True, bm=1408, bn=1024, bk=2048, name="d_w_ffn_out")
    dffn = _ffn_act_bwd(dx3_b, w_ffn_out.T, ffn_gate, ffn_up)
    dw_ffn_in = _mm(hf, dffn, ta=True, bm=1024, bn=1408, bk=2048, out_stack=True, name="d_w_ffn_in")
    core = lax.axis_index("c").astype(jnp.int32).reshape(1)
    ffn_group = ["w_ffn_in", "w_ffn_out"]
    mid_group = ["w_pool_out", "w_fox_out", "w_out", "w_xq", "w_xkv", "w_xo"]
    grad_stacks = {"w_ffn_in": dw_ffn_in, "w_ffn_out": _stack_of(dw_ffn_out, 0)}

    def presum(group, theirs):
        return [_sum_halves(grad_stacks[n], t_, core, "sum_halves_" + n) for n, t_ in zip(group, theirs)]

    (dx2, dx2_b, dg_ffn), theirs = _mm_norm_bwd(dffn, _stack_t(w_ffn_in3), x2_, g_ffn, dx3, "d_hf_norm_ffn_bwd",
                                                ex=_swap_exchange([grad_stacks[n] for n in ffn_group]))
    chip_sums = dict(zip(ffn_group, presum(ffn_group, theirs)))

    dw_xo = _mm(ox, dx2_b, ta=True, bn=256, out_stack=True, name="d_w_xo")
    dqx, dkv = _xattn_bwd(qx3, kv3, dx2_b.reshape(nb, s, d), _stack_t(w_xo3))
    dqx2, dkv2 = dqx.reshape(t, X_WIDTH), dkv.reshape(nb * n_mem, 2 * X_WIDTH)
    dw_xkv = _mm(mem_n, dkv2, ta=True, name="d_w_xkv")
    dmem_n = _mm(dkv2, w_xkv.T, name="d_mem_n")
    dg_mem = _rms_bwd(dmem_n, mem2, g_mem, None, "norm_mem_bwd")
    dw_xq = _mm(hx, dqx2, ta=True, name="d_w_xq")
    dx1, dx1_b, dg_x = _mm_norm_bwd(dqx2, w_xq.T, x1, g_x, dx2, "d_hx_norm_x_bwd")

    dw_out = _mm(mix, dx1_b, ta=True, name="d_w_out")
    dyp, dyf, dgu, db_gate = _mix_bwd(gu, b_gate, y_pool, y_fox, dx1_b, w_out.T)
    dw_pool_out = _mm(y2, dyp, ta=True, bn=256, out_stack=True, name="d_w_pool_out")
    dw_fox_out = _mm(o2, dyf, ta=True, bn=256, out_stack=True, name="d_w_fox_out")
    do = _mm(dyf, _stack_t(w_fox_out3), name="d_o").reshape(nb, s, FOX_WIDTH)
    dgu3, dpool_w, dpool_scale = _pool_bwd(gu3, dyp.reshape(nb, s, d), _stack_t(w_pool_out3), pool_w, pool_scale,
                                           dgu.reshape(nb, s, GU_COLS))
    grad_stacks.update({"w_pool_out": dw_pool_out, "w_fox_out": dw_fox_out, "w_out": _stack_of(dw_out, 0),
                        "w_xq": _stack_of(dw_xq, 0), "w_xkv": _stack_of(dw_xkv, 0), "w_xo": dw_xo})
    dgu2 = dgu3.reshape(t, GU_COLS)
    dw_gu_t, theirs = _mm(dgu2, h, ta=True, name="d_w_gates_pool",
                          ex=_swap_exchange([grad_stacks[n] for n in mid_group]))
    chip_sums.update(zip(mid_group, presum(mid_group, theirs)))
    early = ffn_group + mid_group
    (dq3, dkv3, dc_q, dc_row), early_slots = _fox_bwd(qkv3, c_exp, c_row, lse, o, do,
                                                      ex=_chips_exchange([chip_sums[n] for n in early]))
    slots = dict(zip(early, early_slots))
    dc = dc_row.reshape(nb, FOX_HEADS, s).transpose(0, 2, 1) + dc_q.transpose(0, 2, 1, 3).reshape(nb, s, FOX_HEADS)
    dc = jnp.pad(dc, ((0, 0), (0, 0), (0, LANES - FOX_HEADS)))
    df, db_f = _forget_bwd(dc, f_pad.reshape(nb, s, LANES), b_f_pad)
    dq2, dkv2, df2 = dq3.reshape(t, FOX_WIDTH), dkv3.reshape(t, 2 * FOX_WIDTH), df.reshape(t, LANES)
    dw_q_t = _mm(dq2, h, ta=True, name="d_w_q")
    dw_kv_t = _mm(dkv2, h, ta=True, name="d_w_kv")
    dw_f_t = _mm(df2, h, ta=True, name="d_w_forget")
    dw_k_t, dw_v_t = _unpair_rows(dw_kv_t)
    dw_in_t = jnp.concatenate([dw_gu_t[GATE_WIDTH:], dw_q_t, dw_k_t, dw_v_t, dw_f_t[:FOX_HEADS],
                               dw_gu_t[:GATE_WIDTH]])
    grad_stacks["w_in"] = dw_in_t.reshape(N_CHIPS, IN_COLS // N_CHIPS, D_MODEL)
    chip_sums["w_in"], = presum(["w_in"], _run_exchange(_swap_exchange([grad_stacks["w_in"]]), "swap_halves_w_in"))
    dh, (slots["w_in"],) = _input_grad([dgu2, dq2, dkv2, df2],
                                       [w_gu_t, w_qkv_t[:FOX_WIDTH], w_qkv_t[FOX_WIDTH:], w_f_t],
                                       _chips_exchange([chip_sums["w_in"]]))

    place = jnp.stack([lax.axis_index("c"), 2 * lax.axis_index("x") + lax.axis_index("y")]).astype(jnp.int32)
    halves = [_sum_chips(slots[n], chip_sums[n], place, _by_rows(local[n].shape[0]), "sum_chips_" + n) for n in names]
    (dx, _, dg_mix), reduced = _rms_bwd(dh, x2, g_mix, dx1, "norm_mix_bwd", ex=_join_exchange(halves))

    small_grads = {"norm_mix_g": dg_mix, "b_forget": db_f[:, :FOX_HEADS], "b_gate": db_gate, "pool_w": dpool_w,
                   "pool_scale": dpool_scale, "norm_x_g": dg_x, "norm_mem_g": dg_mem, "norm_ffn_g": dg_ffn,
                   "norm_final_g": dg_final}
    def flat2d(a):
        return a.reshape(-1, a.shape[-1])

    small_names = [n for n, _ in SMALL]
    own = [flat2d(small_grads[n]) for n in small_names]
    small_gather = _small_exchange(own + [loss_part])

    def tiles_of(a):
        return a.transpose(2, 0, 1)

    def block_of(a3):
        return a3.transpose(1, 2, 0)

    grads, deltas, new_m, new_v = {}, {}, {}, {}
    gathered = None
    for n, g_ in zip(names, reduced):
        if n == "w_in":
            g_ = lax.optimization_barrier(g_.reshape(IN_COLS // N_CHIPS, 1, D_MODEL))
            (d_, m_, v_), gathered = _adamw(tiles_of(weights[n]), g_, tiles_of(moments_m[n]), tiles_of(moments_v[n]),
                                            "adamw_" + n, ex=small_gather)
            back = block_of
        else:
            d_, m_, v_ = _adamw(local[n], g_, shard2d(moments_m[n], n), shard2d(moments_v[n], n), "adamw_" + n)
            back = functools.partial(unshard, n=n)
        grads[n], deltas[n], new_m[n], new_v[n] = (back(a) for a in (g_, d_, m_, v_))

    device = (4 * lax.axis_index("x") + 2 * lax.axis_index("y") + lax.axis_index("c")).astype(jnp.int32).reshape(1)
    sg, sd, sm, sv, loss_sum = _adamw_small(
        gathered[:-1], own, [flat2d(weights[n]) for n in small_names], [flat2d(moments_m[n]) for n in small_names],
        [flat2d(moments_v[n]) for n in small_names], gathered[-1], loss_part, device)
    for n, g_, d_, m_, v_ in zip(small_names, sg, sd, sm, sv):
        grads[n], deltas[n], new_m[n], new_v[n] = (a.reshape(weights[n].shape) for a in (g_, d_, m_, v_))
    return loss_sum[0, 0], dx.reshape(nb, s, d), grads, deltas, new_m, new_v


def kernel(x, mem, norm_mix_g, w_in, b_forget, b_gate, pool_w, pool_scale, w_pool_out, w_fox_out, w_out, norm_x_g, norm_mem_g, w_xq, w_xkv, w_xo, norm_ffn_g, w_ffn_in, w_ffn_out, norm_final_g, loss_target, m_norm_mix_g, m_w_in, m_b_forget, m_b_gate, m_pool_w, m_pool_scale, m_w_pool_out, m_w_fox_out, m_w_out, m_norm_x_g, m_norm_mem_g, m_w_xq, m_w_xkv, m_w_xo, m_norm_ffn_g, m_w_ffn_in, m_w_ffn_out, m_norm_final_g, v_norm_mix_g, v_w_in, v_b_forget, v_b_gate, v_pool_w, v_pool_scale, v_w_pool_out, v_w_fox_out, v_w_out, v_norm_x_g, v_norm_mem_g, v_w_xq, v_w_xkv, v_w_xo, v_norm_ffn_g, v_w_ffn_in, v_w_ffn_out, v_norm_final_g):
    given = dict(locals())
    weights = {n: given[n] for n in WEIGHT_ORDER}
    moments_m = {n: given["m_" + n] for n in WEIGHT_ORDER}
    moments_v = {n: given["v_" + n] for n in WEIGHT_ORDER}
    loss, grad_x, grads, deltas, new_m, new_v = _step(x, mem, loss_target, weights, moments_m, moments_v)
    return (loss, grad_x, *[grads[n] for n in WEIGHT_ORDER], *[deltas[n] for n in WEIGHT_ORDER],
            *[new_m[n] for n in WEIGHT_ORDER], *[new_v[n] for n in WEIGHT_ORDER])
```

```python
import functools
import math

import jax
import jax.numpy as jnp
from jax import lax
from jax.experimental import pallas as pl
from jax.experimental.pallas import tpu as pltpu

F32 = jnp.float32
BF16 = jnp.bfloat16
MESH = pl.DeviceIdType.MESH

D_MODEL = 1024
EPS = 1e-6
POOL_WINDOWS = (2, 4, 8, 16)
POOL_WIDTH = 512
POOL_GC = 128
FOX_HEADS = 8
FOX_DH = 64
FOX_WIDTH = 512
X_HEADS = 4
X_DH = 128
X_WIDTH = 512
D_FF = 2816
IN_COLS = 4104
GATE_WIDTH = 2048
ADAM_LR = 0.001
ADAM_B1 = 0.9
ADAM_B2 = 0.999
ADAM_EPS = 1e-08
ADAM_WD = 0.01
ADAM_STEP = 10

N_CHIPS = 4
N_DEV = 8
LANES = 128
VMEM_LIMIT_BYTES = 56 * 1024 * 1024
NEG_INF = -1e30
ATT_BLOCK = 1024

SHARDED = (
    ("w_in", (1024, IN_COLS), 1),
    ("w_pool_out", (POOL_WIDTH, 1024), 1),
    ("w_fox_out", (FOX_WIDTH, 1024), 1),
    ("w_out", (1024, 1024), 0),
    ("w_xq", (1024, X_WIDTH), 0),
    ("w_xkv", (1024, 2 * X_WIDTH), 0),
    ("w_xo", (X_WIDTH, 1024), 1),
    ("w_ffn_in", (1024, 2 * D_FF), 1),
    ("w_ffn_out", (D_FF, 1024), 0),
)
SMALL = (
    ("norm_mix_g", (1, 1024)),
    ("b_forget", (1, 8)),
    ("b_gate", (1, 2048)),
    ("pool_w", (1, 4, 128, 128)),
    ("pool_scale", (1, 512)),
    ("norm_x_g", (1, 1024)),
    ("norm_mem_g", (1, 1024)),
    ("norm_ffn_g", (1, 1024)),
    ("norm_final_g", (1024,)),
)
WEIGHT_ORDER = ("norm_mix_g", "w_in", "b_forget", "b_gate", "pool_w", "pool_scale", "w_pool_out", "w_fox_out", "w_out",
                "norm_x_g", "norm_mem_g", "w_xq", "w_xkv", "w_xo", "norm_ffn_g", "w_ffn_in", "w_ffn_out", "norm_final_g")


def _cparams(sem=None):
    return pltpu.CompilerParams(dimension_semantics=sem, vmem_limit_bytes=VMEM_LIMIT_BYTES)


def _block(dim, pref, unit):
    if dim <= pref:
        return dim
    best = None
    for b in range(unit, pref + 1, unit):
        if dim % b == 0:
            best = b
    assert best is not None, (dim, pref, unit)
    return best


def _rows_block(rows, cols, unit=16, elems=1 << 19):
    return _block(rows, max(unit, elems // cols // unit * unit), unit)


def _my_place():
    return lax.axis_index("x"), lax.axis_index("y"), lax.axis_index("c")


def _other_chips(x, y):
    return [(1 - x, y), (x, 1 - y), (1 - x, 1 - y)]


def _chip(place):
    return 2 * place[0] + place[1]


ANY = pl.BlockSpec(memory_space=pl.ANY)


def _by_rows(rows):
    return rows % 32 == 0


def _half_shape(rows, cols):
    return (rows // 2, cols) if _by_rows(rows) else (rows, cols // 2)


def _core_half(ref, core, lead=()):
    rows, cols = ref.shape[-2:]
    if _by_rows(rows):
        return ref.at[(*lead, pl.ds(core * (rows // 2), rows // 2), slice(None))]
    return ref.at[(*lead, slice(None), pl.ds(core * (cols // 2), cols // 2))]


class _Exchange:
    def __init__(self, arrays, out_shapes, n_sems, start, finish, in_place=False):
        self.arrays, self.out_shapes, self.n_sems, self.start, self.finish = arrays, out_shapes, n_sems, start, finish
        self.in_place = in_place

    def scratch(self):
        return [pltpu.SemaphoreType.DMA((self.n_sems,)), pltpu.SemaphoreType.DMA((self.n_sems,))]

    def aliases(self, first_in, first_out):
        return {first_in + k: first_out + k for k in range(len(self.arrays))} if self.in_place else {}


def _run_exchange(ex, name):
    n = len(ex.arrays)

    def body(*refs):
        ins, outs, sems = refs[:n], refs[n:2 * n], refs[2 * n:]
        ex.start(ins, outs, *sems)
        ex.finish(ins, outs, *sems)

    return pl.pallas_call(
        body, name=name, out_shape=ex.out_shapes, in_specs=[ANY] * n, out_specs=[ANY] * n, scratch_shapes=ex.scratch(),
        input_output_aliases=ex.aliases(0, 0),
    )(*ex.arrays)


def _hosted_call(body, ex, *, name, grid, in_specs, out_specs, out_shape, args, scratch=()):
    n_in, n_out, n_scr = len(args), len(out_shape), len(scratch)
    if ex is None:
        outs = pl.pallas_call(
            body, name=name, grid=grid, out_shape=out_shape, in_specs=in_specs, out_specs=out_specs,
            scratch_shapes=list(scratch), compiler_params=_cparams(("arbitrary",) * len(grid)))(*args)
        return outs, None
    nc = len(ex.arrays)

    def full_body(*refs):
        ins, cins = refs[:n_in], refs[n_in:n_in + nc]
        outs, couts = refs[n_in + nc:n_in + nc + n_out], refs[n_in + nc + n_out:n_in + 2 * nc + n_out]
        rest = refs[n_in + 2 * nc + n_out:]
        scr, sems = rest[:n_scr], rest[n_scr:]
        first = functools.reduce(jnp.logical_and, [pl.program_id(a) == 0 for a in range(len(grid))])
        last = functools.reduce(jnp.logical_and, [pl.program_id(a) == grid[a] - 1 for a in range(len(grid))])

        @pl.when(first)
        def _():
            ex.start(cins, couts, *sems)

        body(*ins, *outs, *scr)

        @pl.when(last)
        def _():
            ex.finish(cins, couts, *sems)

    outs = pl.pallas_call(
        full_body, name=name, grid=grid, out_shape=list(out_shape) + list(ex.out_shapes),
        in_specs=list(in_specs) + [ANY] * nc, out_specs=list(out_specs) + [ANY] * nc,
        scratch_shapes=list(scratch) + ex.scratch(), input_output_aliases=ex.aliases(n_in, n_out),
        compiler_params=_cparams(("arbitrary",) * len(grid)))(*args, *ex.arrays)
    return outs[:n_out], outs[n_out:]


def _gather_exchange(shards):
    n = len(shards)

    def copies(ins, outs, send_sems, recv_sems):
        x, y, c = _my_place()

        def half(k, chip, core):
            return _core_half(outs[k], core, lead=(_chip(chip),))

        def copy(k, slot, chip, core, to, src=None):
            return pltpu.make_async_remote_copy(
                src_ref=half(k, chip, core) if src is None else src, dst_ref=half(k, chip, core),
                send_sem=send_sems.at[6 * k + slot], recv_sem=recv_sems.at[6 * k + slot],
                device_id=to, device_id_type=MESH)

        return (x, y, c), copy

    def first_copies(ins, outs, send_sems, recv_sems):
        (x, y, c), copy = copies(ins, outs, send_sems, recv_sems)
        out = []
        for j, chip in enumerate(_other_chips(x, y)):
            for k in range(n):
                out.append(copy(k, j, (x, y), c, (*chip, c), src=_core_half(ins[k], c)))
        return out

    def start(ins, outs, send_sems, recv_sems):
        for cp in first_copies(ins, outs, send_sems, recv_sems):
            cp.start()

    def finish(ins, outs, send_sems, recv_sems):
        (x, y, c), copy = copies(ins, outs, send_sems, recv_sems)
        chips = _other_chips(x, y)
        passed = []
        for j, chip in enumerate(chips):
            for k in range(n):
                copy(k, j, chip, c, (x, y, c)).wait_recv()
                passed.append(copy(k, 3 + j, chip, c, (x, y, 1 - c)))
                passed[-1].start()
        for j, chip in enumerate(chips):
            for k in range(n):
                copy(k, 3 + j, chip, 1 - c, (x, y, c)).wait_recv()
        for cp in first_copies(ins, outs, send_sems, recv_sems) + passed:
            cp.wait_send()

    return _Exchange(list(shards), [jax.ShapeDtypeStruct((N_CHIPS,) + s.shape, s.dtype) for s in shards], 6 * n,
                     start, finish)


def _place_own(stacks, shards):
    me = 2 * lax.axis_index("x") + lax.axis_index("y")
    return [lax.dynamic_update_slice(others, mine[None], (me, 0, 0)) for others, mine in zip(stacks, shards)]


def _swap_exchange(grads):
    n = len(grads)

    def copies(ins, outs, send_sems, recv_sems):
        x, y, c = _my_place()
        return [pltpu.make_async_remote_copy(
            src_ref=_core_half(ins[k], 1 - c, lead=(slice(None),)), dst_ref=outs[k],
            send_sem=send_sems.at[k], recv_sem=recv_sems.at[k], device_id=(x, y, 1 - c), device_id_type=MESH)
            for k in range(n)]

    def start(ins, outs, send_sems, recv_sems):
        for cp in copies(ins, outs, send_sems, recv_sems):
            cp.start()

    def finish(ins, outs, send_sems, recv_sems):
        for cp in copies(ins, outs, send_sems, recv_sems):
            cp.wait()

    return _Exchange(list(grads), [jax.ShapeDtypeStruct((N_CHIPS,) + _half_shape(*g.shape[1:]), g.dtype) for g in grads],
                     n, start, finish)


def _chips_exchange(sums):
    n = len(sums)

    def sends(ins, outs, send_sems, recv_sems):
        x, y, c = _my_place()
        return [pltpu.make_async_remote_copy(
            src_ref=ins[k].at[_chip(chip)], dst_ref=outs[k].at[_chip((x, y))],
            send_sem=send_sems.at[3 * k + j], recv_sem=recv_sems.at[3 * k + j],
            device_id=(*chip, c), device_id_type=MESH)
            for j, chip in enumerate(_other_chips(x, y)) for k in range(n)]

    def start(ins, outs, send_sems, recv_sems):
        for cp in sends(ins, outs, send_sems, recv_sems):
            cp.start()

    def finish(ins, outs, send_sems, recv_sems):
        x, y, c = _my_place()
        for j, chip in enumerate(_other_chips(x, y)):
            for k in range(n):
                slot = outs[k].at[_chip(chip)]
                pltpu.make_async_remote_copy(
                    src_ref=slot, dst_ref=slot, send_sem=send_sems.at[3 * k + j], recv_sem=recv_sems.at[3 * k + j],
                    device_id=(x, y, c), device_id_type=MESH).wait_recv()
        for cp in sends(ins, outs, send_sems, recv_sems):
            cp.wait_send()

    return _Exchange(list(sums), [jax.ShapeDtypeStruct(s.shape, s.dtype) for s in sums], 3 * n, start, finish)


def _join_exchange(shards):
    n = len(shards)

    def sends(ins, outs, send_sems, recv_sems):
        x, y, c = _my_place()
        return [pltpu.make_async_remote_copy(
            src_ref=_core_half(ins[k], c), dst_ref=_core_half(outs[k], c),
            send_sem=send_sems.at[k], recv_sem=recv_sems.at[k], device_id=(x, y, 1 - c), device_id_type=MESH)
            for k in range(n)]

    def start(ins, outs, send_sems, recv_sems):
        for cp in sends(ins, outs, send_sems, recv_sems):
            cp.start()

    def finish(ins, outs, send_sems, recv_sems):
        x, y, c = _my_place()
        for k in range(n):
            theirs = _core_half(outs[k], 1 - c)
            pltpu.make_async_remote_copy(
                src_ref=theirs, dst_ref=theirs, send_sem=send_sems.at[k], recv_sem=recv_sems.at[k],
                device_id=(x, y, c), device_id_type=MESH).wait_recv()
        for cp in sends(ins, outs, send_sems, recv_sems):
            cp.wait_send()

    return _Exchange(list(shards), [jax.ShapeDtypeStruct(s.shape, s.dtype) for s in shards], n, start, finish,
                     in_place=True)


def _small_exchange(blocks):
    n = len(blocks)

    def copies(ins, outs, send_sems, recv_sems):
        x, y, c = _my_place()

        def copy(k, j, whose, to, src=None):
            slot = outs[k].at[4 * whose[0] + 2 * whose[1] + whose[2]]
            return pltpu.make_async_remote_copy(
                src_ref=slot if src is None else src, dst_ref=slot,
                send_sem=send_sems.at[7 * k + j], recv_sem=recv_sems.at[7 * k + j], device_id=to, device_id_type=MESH)

        return (x, y, c), copy

    def first_copies(ins, outs, send_sems, recv_sems):
        (x, y, c), copy = copies(ins, outs, send_sems, recv_sems)
        out = []
        for k in range(n):
            out.append(copy(k, 0, (x, y, c), (x, y, 1 - c), src=ins[k]))
            out += [copy(k, 1 + j, (x, y, c), (*chip, c), src=ins[k]) for j, chip in enumerate(_other_chips(x, y))]
        return out

    def start(ins, outs, send_sems, recv_sems):
        for cp in first_copies(ins, outs, send_sems, recv_sems):
            cp.start()

    def finish(ins, outs, send_sems, recv_sems):
        (x, y, c), copy = copies(ins, outs, send_sems, recv_sems)
        chips = _other_chips(x, y)
        passed = []
        for j, chip in enumerate(chips):
            for k in range(n):
                copy(k, 1 + j, (*chip, c), (x, y, c)).wait_recv()
                passed.append(copy(k, 4 + j, (*chip, c), (x, y, 1 - c)))
                passed[-1].start()
        for k in range(n):
            copy(k, 0, (x, y, 1 - c), (x, y, c)).wait_recv()
        for j, chip in enumerate(chips):
            for k in range(n):
                copy(k, 4 + j, (*chip, 1 - c), (x, y, c)).wait_recv()
        for cp in first_copies(ins, outs, send_sems, recv_sems) + passed:
            cp.wait_send()

    return _Exchange(list(blocks), [jax.ShapeDtypeStruct((N_DEV,) + blk.shape, blk.dtype) for blk in blocks], 7 * n,
                     start, finish)


def _sum_halves(grads, theirs, core, name):
    _, h, cols = theirs.shape
    by_rows = _by_rows(grads.shape[1])
    br = _rows_block(h, cols) if by_rows else h
    nb = h // br

    def body(core_ref, a_ref, b_ref, o_ref):
        o_ref[...] = (a_ref[...] + b_ref[...]).astype(BF16)

    if by_rows:
        mine = pl.BlockSpec((1, br, cols), lambda j, i, core_ref: (j, core_ref[0] * nb + i, 0))
    else:
        mine = pl.BlockSpec((1, br, cols), lambda j, i, core_ref: (j, i, core_ref[0]))
    return pl.pallas_call(
        body, name=name,
        out_shape=jax.ShapeDtypeStruct(theirs.shape, BF16),
        grid_spec=pltpu.PrefetchScalarGridSpec(
            num_scalar_prefetch=1, grid=(N_CHIPS, nb),
            in_specs=[mine, pl.BlockSpec((1, br, cols), lambda j, i, core_ref: (j, i, 0))],
            out_specs=pl.BlockSpec((1, br, cols), lambda j, i, core_ref: (j, i, 0))),
        compiler_params=_cparams(("parallel", "parallel")),
    )(core, grads, theirs)


def _sum_chips(slots, sums, place, by_rows, name):
    _, h, cols = slots.shape
    br = _rows_block(h, cols) if by_rows else h
    nb = h // br

    def body(place_ref, s_ref, own_ref, o_ref):
        me = place_ref[1]
        acc = None
        for k in range(N_CHIPS):
            term = jnp.where(me == k, own_ref[k], s_ref[k]).astype(F32)
            acc = term if acc is None else acc + term
        o_ref[...] = acc

    stack = pl.BlockSpec((N_CHIPS, br, cols), lambda i, place_ref: (0, i, 0))
    if by_rows:
        out_shape, out_map = (2 * h, cols), lambda i, place_ref: (place_ref[0] * nb + i, 0)
    else:
        out_shape, out_map = (h, 2 * cols), lambda i, place_ref: (i, place_ref[0])
    return pl.pallas_call(
        body, name=name,
        out_shape=jax.ShapeDtypeStruct(out_shape, F32),
        grid_spec=pltpu.PrefetchScalarGridSpec(
            num_scalar_prefetch=1, grid=(nb,), in_specs=[stack, stack],
            out_specs=pl.BlockSpec((br, cols), out_map)),
        compiler_params=_cparams(("parallel",)),
    )(place, slots, sums)


def _adamw_math(w, g, m, v):
    m = ADAM_B1 * m + (1.0 - ADAM_B1) * g
    v = ADAM_B2 * v + (1.0 - ADAM_B2) * (g * g)
    m_hat = m / (1.0 - ADAM_B1 ** ADAM_STEP)
    v_hat = v / (1.0 - ADAM_B2 ** ADAM_STEP)
    delta = -ADAM_LR * (m_hat / (jnp.sqrt(v_hat) + ADAM_EPS) + ADAM_WD * w)
    return delta, m, v


def _adamw(w, g, m, v, name, ex=None):
    def body(w_ref, g_ref, m_ref, v_ref, d_ref, nm_ref, nv_ref):
        d, nm, nv = _adamw_math(w_ref[...], g_ref[...], m_ref[...], v_ref[...])
        d_ref[...] = d
        nm_ref[...] = nm
        nv_ref[...] = nv

    if w.ndim == 3:
        rows = w.shape[0]
        br = max(b for b in range(1, 65) if rows % b == 0)
        spec, steps = pl.BlockSpec((br,) + w.shape[1:], lambda i: (i, 0, 0)), rows // br
    else:
        rows, cols = w.shape
        br = _rows_block(rows, cols, unit=8)
        spec, steps = pl.BlockSpec((br, cols), lambda i: (i, 0)), rows // br
    shape = jax.ShapeDtypeStruct(w.shape, F32)
    outs, moved = _hosted_call(
        body, ex, name=name, out_shape=(shape, shape, shape), grid=(steps,),
        in_specs=[spec] * 4, out_specs=(spec, spec, spec), args=(w, g, m, v))
    return tuple(outs) if ex is None else (tuple(outs), moved)


def _adamw_small(parts, own, ws, ms, vs, loss_parts, loss_own, device):
    n = len(ws)

    def total(device_ref, parts_ref, own_ref):
        acc = None
        for dev in range(N_DEV):
            term = jnp.where(device_ref[0] == dev, own_ref[...], parts_ref[dev])
            acc = term if acc is None else acc + term
        return acc

    def body(device_ref, *refs):
        ins, outs = refs[:5 * n + 2], refs[5 * n + 2:]
        for k in range(n):
            g = total(device_ref, ins[k], ins[n + k])
            d, nm, nv = _adamw_math(ins[2 * n + k][...], g, ins[3 * n + k][...], ins[4 * n + k][...])
            for o_ref, val in zip(outs[k::n][:4], (g, d, nm, nv)):
                o_ref[...] = val
        outs[4 * n][...] = total(device_ref, ins[5 * n], ins[5 * n + 1])

    args = list(parts) + list(own) + list(ws) + list(ms) + list(vs) + [loss_parts, loss_own]
    whole = lambda a: pl.BlockSpec(a.shape, lambda i, device_ref, nd=a.ndim: (0,) * nd)
    shapes = [jax.ShapeDtypeStruct(w.shape, F32) for w in ws] * 4 + [jax.ShapeDtypeStruct(loss_own.shape, F32)]
    outs = pl.pallas_call(
        body, name="adamw_small", out_shape=shapes,
        grid_spec=pltpu.PrefetchScalarGridSpec(
            num_scalar_prefetch=1, grid=(1,), in_specs=[whole(a) for a in args], out_specs=[whole(a) for a in shapes]),
        compiler_params=_cparams(("arbitrary",)),
    )(device, *args)
    return outs[:n], outs[n:2 * n], outs[2 * n:3 * n], outs[3 * n:4 * n], outs[4 * n]


def _mm(a, b, *, name, ta=False, out_dtype=F32, res=None, bm=1024, bn=1024, bk=4096, b_stack=False, out_stack=False,
        ex=None):
    if ta:
        kdim, m = a.shape
    else:
        m, kdim = a.shape
    if b_stack:
        _, kb, chunk = b.shape
        n = N_CHIPS * chunk
    else:
        kb, n = b.shape
        chunk = n // N_CHIPS if out_stack else n
    assert kdim == kb, (a.shape, b.shape, ta)
    bm = _block(m, bm, LANES if ta else 16)
    bn = _block(chunk, bn, LANES)
    bk = _block(kdim, bk, LANES)
    nk = kdim // bk
    per_chunk = chunk // bn
    dims = (((0 if ta else 1,), (0,)), ((), ()))

    def body(*refs):
        refs = list(refs)
        a_ref, b_ref = refs[:2]
        r_ref = refs[2] if res is not None else None
        o_ref = refs[3] if res is not None else refs[2]
        part = lax.dot_general(a_ref[...].astype(BF16), b_ref[...].astype(BF16), dims, preferred_element_type=F32)

        def finish(r):
            if r_ref is not None:
                r = r + r_ref[...]
            o_ref[...] = r.astype(out_dtype)

        if nk == 1:
            finish(part)
        else:
            acc_ref = refs[-1]
            k = pl.program_id(2)

            @pl.when(k == 0)
            def _():
                acc_ref[...] = part

            @pl.when(k > 0)
            def _():
                acc_ref[...] += part

            @pl.when(k == nk - 1)
            def _():
                finish(acc_ref[...])

    a_spec = pl.BlockSpec((bk, bm), lambda i, j, k: (k, i)) if ta else pl.BlockSpec((bm, bk), lambda i, j, k: (i, k))
    if b_stack:
        b_spec = pl.BlockSpec((None, bk, bn), lambda i, j, k: (j // per_chunk, k, j % per_chunk))
    else:
        b_spec = pl.BlockSpec((bk, bn), lambda i, j, k: (k, j))
    r_spec = pl.BlockSpec((bm, bn), lambda i, j, k: (i, j))
    if out_stack:
        o_spec = pl.BlockSpec((None, bm, bn), lambda i, j, k: (j // per_chunk, i, j % per_chunk))
        o_shape = (N_CHIPS, m, chunk)
    else:
        o_spec, o_shape = r_spec, (m, n)
    in_specs = [a_spec, b_spec] + ([r_spec] if res is not None else [])
    args = (a, b) + ((res,) if res is not None else ())
    (out,), moved = _hosted_call(
        body, ex, name=name, out_shape=(jax.ShapeDtypeStruct(o_shape, out_dtype),),
        grid=(m // bm, n // bn, nk), in_specs=in_specs, out_specs=(o_spec,),
        scratch=[pltpu.VMEM((bm, bn), F32)] if nk > 1 else [], args=args)
    return out if ex is None else (out, moved)


def _rms_fwd(x, g, name, ex=None):
    t, d = x.shape
    bt = _block(t, 512, 16)

    def body(x_ref, g_ref, h_ref):
        xv = x_ref[...]
        r = lax.rsqrt(jnp.mean(xv * xv, axis=-1, keepdims=True) + EPS)
        h_ref[...] = (xv * r * g_ref[...]).astype(BF16)

    (out,), moved = _hosted_call(
        body, ex, name=name, out_shape=(jax.ShapeDtypeStruct((t, d), BF16),), grid=(t // bt,),
        in_specs=[pl.BlockSpec((bt, d), lambda i: (i, 0)), pl.BlockSpec((1, d), lambda i: (0, 0))],
        out_specs=(pl.BlockSpec((bt, d), lambda i: (i, 0)),), args=(x, g))
    return out if ex is None else (out, moved)


def _rms_bwd(dh, x, g, dres, name, ex=None):
    t, d = x.shape
    bt = _block(t, 256, 16)
    want_dx = dres is not None

    def body(*refs):
        if want_dx:
            dh_ref, x_ref, g_ref, dres_ref, dx_ref, dxb_ref, dg_ref = refs
        else:
            dh_ref, x_ref, g_ref, dg_ref = refs
        xv = x_ref[...]
        r = lax.rsqrt(jnp.mean(xv * xv, axis=-1, keepdims=True) + EPS)
        xhat = xv * r
        dhv = dh_ref[...]

        @pl.when(pl.program_id(0) == 0)
        def _():
            dg_ref[...] = jnp.zeros_like(dg_ref)

        dg_ref[...] += jnp.sum(dhv * xhat, axis=0, keepdims=True)
        if want_dx:
            dxhat = dhv * g_ref[...]
            dx = dres_ref[...] + r * (dxhat - xhat * jnp.mean(dxhat * xhat, axis=-1, keepdims=True))
            dx_ref[...] = dx
            dxb_ref[...] = dx.astype(BF16)

    row = pl.BlockSpec((bt, d), lambda i: (i, 0))
    vec = pl.BlockSpec((1, d), lambda i: (0, 0))
    if want_dx:
        outs, moved = _hosted_call(
            body, ex, name=name, grid=(t // bt,),
            out_shape=(jax.ShapeDtypeStruct((t, d), F32), jax.ShapeDtypeStruct((t, d), BF16),
                       jax.ShapeDtypeStruct((1, d), F32)),
            in_specs=[row, row, vec, row], out_specs=(row, row, vec), args=(dh, x, g, dres))
        return tuple(outs) if ex is None else (tuple(outs), moved)
    return pl.pallas_call(
        body, name=name, grid=(t // bt,), out_shape=jax.ShapeDtypeStruct((1, d), F32),
        in_specs=[row, row, vec], out_specs=vec,
        compiler_params=_cparams(("arbitrary",)),
    )(dh, x, g)


def _in_proj_attn(h, w_qkv, w_f_lanes, w_f):
    t, d = h.shape
    bm = _block(t, 512, 16)

    def body(h_ref, wq_ref, wl_ref, wf_ref, qkv_ref, fl_ref, f_ref):
        hv = h_ref[...]
        qkv_ref[...] = jnp.dot(hv, wq_ref[...], preferred_element_type=F32).astype(BF16)
        fl_ref[...] = jnp.dot(hv, wl_ref[...], preferred_element_type=F32)
        f_ref[...] = jnp.dot(hv, wf_ref[...], preferred_element_type=F32)

    whole = lambda w: pl.BlockSpec(w.shape, lambda i: (0, 0))
    rows = lambda n: pl.BlockSpec((bm, n), lambda i: (i, 0))
    return pl.pallas_call(
        body, name="in_proj_attn", grid=(t // bm,),
        out_shape=(jax.ShapeDtypeStruct((t, w_qkv.shape[1]), BF16), jax.ShapeDtypeStruct((t, w_f_lanes.shape[1]), F32),
                   jax.ShapeDtypeStruct((t, w_f.shape[1]), F32)),
        in_specs=[rows(d), whole(w_qkv), whole(w_f_lanes), whole(w_f)],
        out_specs=(rows(w_qkv.shape[1]), rows(w_f_lanes.shape[1]), rows(w_f.shape[1])),
        compiler_params=_cparams(("parallel",)),
    )(h, w_qkv, w_f_lanes, w_f)


def _mm_res_norm(a, b, res, g, name):
    t, k = a.shape
    d = b.shape[1]
    bm = _block(t, 512, 16)

    def body(a_ref, b_ref, r_ref, g_ref, x_ref, h_ref):
        xv = jnp.dot(a_ref[...], b_ref[...], preferred_element_type=F32) + r_ref[...]
        x_ref[...] = xv
        r = lax.rsqrt(jnp.mean(xv * xv, axis=-1, keepdims=True) + EPS)
        h_ref[...] = (xv * r * g_ref[...]).astype(BF16)

    row = pl.BlockSpec((bm, d), lambda i: (i, 0))
    return pl.pallas_call(
        body, name=name, grid=(t // bm,),
        out_shape=(jax.ShapeDtypeStruct((t, d), F32), jax.ShapeDtypeStruct((t, d), BF16)),
        in_specs=[pl.BlockSpec((bm, k), lambda i: (i, 0)), pl.BlockSpec((k, d), lambda i: (0, 0)), row,
                  pl.BlockSpec((1, d), lambda i: (0, 0))],
        out_specs=(row, row), compiler_params=_cparams(("parallel",)),
    )(a, b, res, g)


def _ffn_out_loss(act, w, res, target, g):
    t, k = act.shape
    d = w.shape[1]
    bm = _block(t, 512, 16)

    def body(a_ref, w_ref, r_ref, t_ref, g_ref, dx_ref, dxb_ref, dg_ref, loss_ref):
        xv = jnp.dot(a_ref[...], w_ref[...], preferred_element_type=F32) + r_ref[...]
        gv = g_ref[...]
        r = lax.rsqrt(jnp.mean(xv * xv, axis=-1, keepdims=True) + EPS)
        xhat = xv * r
        err = xhat * gv - t_ref[...]

        @pl.when(pl.program_id(0) == 0)
        def _():
            dg_ref[...] = jnp.zeros_like(dg_ref)
            loss_ref[...] = jnp.zeros_like(loss_ref)

        loss_ref[...] += 0.5 * jnp.sum(jnp.mean(err * err, axis=-1, keepdims=True), axis=0, keepdims=True)
        dy = err * (1.0 / d)
        dg_ref[...] += jnp.sum(dy * xhat, axis=0, keepdims=True)
        dxhat = dy * gv
        dx = r * (dxhat - xhat * jnp.mean(dxhat * xhat, axis=-1, keepdims=True))
        dx_ref[...] = dx
        dxb_ref[...] = dx.astype(BF16)

    row = pl.BlockSpec((bm, d), lambda i: (i, 0))
    vec = pl.BlockSpec((1, d), lambda i: (0, 0))
    return pl.pallas_call(
        body, name="ffn_out_loss", grid=(t // bm,),
        out_shape=(jax.ShapeDtypeStruct((t, d), F32), jax.ShapeDtypeStruct((t, d), BF16),
                   jax.ShapeDtypeStruct((1, d), F32), jax.ShapeDtypeStruct((1, LANES), F32)),
        in_specs=[pl.BlockSpec((bm, k), lambda i: (i, 0)), pl.BlockSpec((k, d), lambda i: (0, 0)), row, row, vec],
        out_specs=(row, row, vec, pl.BlockSpec((1, LANES), lambda i: (0, 0))),
        compiler_params=_cparams(("arbitrary",)),
    )(act, w, res, target, g)


def _mm_norm_bwd(a, b, x, g, dres, name, ex=None):
    t, k = a.shape
    d = b.shape[1]
    bm = _block(t, 512 if k <= 2048 else 256, 16)

    def body(a_ref, b_ref, x_ref, g_ref, dres_ref, dx_ref, dxb_ref, dg_ref):
        @pl.when(pl.program_id(0) == 0)
        def _():
            dg_ref[...] = jnp.zeros_like(dg_ref)

        dhv = jnp.dot(a_ref[...], b_ref[...], preferred_element_type=F32)
        xv = x_ref[...]
        r = lax.rsqrt(jnp.mean(xv * xv, axis=-1, keepdims=True) + EPS)
        xhat = xv * r
        dg_ref[...] += jnp.sum(dhv * xhat, axis=0, keepdims=True)
        dxhat = dhv * g_ref[...]
        dx = dres_ref[...] + r * (dxhat - xhat * jnp.mean(dxhat * xhat, axis=-1, keepdims=True))
        dx_ref[...] = dx
        dxb_ref[...] = dx.astype(BF16)

    row = pl.BlockSpec((bm, d), lambda i: (i, 0))
    vec = pl.BlockSpec((1, d), lambda i: (0, 0))
    outs, moved = _hosted_call(
        body, ex, name=name, grid=(t // bm,),
        out_shape=(jax.ShapeDtypeStruct((t, d), F32), jax.ShapeDtypeStruct((t, d), BF16), jax.ShapeDtypeStruct((1, d), F32)),
        in_specs=[pl.BlockSpec((bm, k), lambda i: (i, 0)), pl.BlockSpec((k, d), lambda i: (0, 0)), row, vec, row],
        out_specs=(row, row, vec), args=(a, b, x, g, dres))
    return tuple(outs) if ex is None else (tuple(outs), moved)


GU_COLS = GATE_WIDTH + POOL_WIDTH
U_BLK = GATE_WIDTH // POOL_WIDTH


def _shift_down(a, k, row):
    return jnp.where(row >= k, pltpu.roll(a, k, 0), 0.0)


def _shift_up(a, k, row):
    n = a.shape[0]
    return jnp.where(row < n - k, pltpu.roll(a, n - k, 0), 0.0)


def _window_delta(u, w, row):
    s, k = u, 1
    while k < w:
        s = s + _shift_down(s, k, row)
        k *= 2
    cnt = jnp.minimum(row + 1, w).astype(F32)
    return s / cnt - u, cnt


def _pool_fwd(gu, pool_w, pool_scale):
    b, s, _ = gu.shape

    def body(u_ref, pw_ref, sc_ref, y_ref):
        row = lax.broadcasted_iota(jnp.int32, (s, POOL_GC), 0)
        for g, w in enumerate(POOL_WINDOWS):
            cols = slice(g * POOL_GC, (g + 1) * POOL_GC)
            d, _ = _window_delta(u_ref[0, :, cols].astype(F32), w, row)
            z = jnp.dot(d.astype(BF16), pw_ref[g].astype(BF16), preferred_element_type=F32)
            y_ref[0, :, cols] = (z * sc_ref[:, cols]).astype(BF16)

    return pl.pallas_call(
        body, name="pool_fwd", out_shape=jax.ShapeDtypeStruct((b, s, POOL_WIDTH), BF16), grid=(b,),
        in_specs=[pl.BlockSpec((1, s, POOL_WIDTH), lambda i: (i, 0, U_BLK)),
                  pl.BlockSpec((4, POOL_GC, POOL_GC), lambda i: (0, 0, 0)),
                  pl.BlockSpec((1, POOL_WIDTH), lambda i: (0, 0))],
        out_specs=pl.BlockSpec((1, s, POOL_WIDTH), lambda i: (i, 0, 0)),
        compiler_params=_cparams(("parallel",)),
    )(gu, pool_w, pool_scale)


def _pool_bwd(gu, dyp, w_out_t, pool_w, pool_scale, dgu):
    b, s, _ = gu.shape

    def body(u_ref, dyp_ref, w_ref, pw_ref, sc_ref, dgu_in, du_ref, dpw_ref, dsc_ref):
        del dgu_in

        @pl.when(pl.program_id(0) == 0)
        def _():
            dpw_ref[...] = jnp.zeros_like(dpw_ref)
            dsc_ref[...] = jnp.zeros_like(dsc_ref)

        row = lax.broadcasted_iota(jnp.int32, (s, POOL_GC), 0)
        for g, w in enumerate(POOL_WINDOWS):
            cols = slice(g * POOL_GC, (g + 1) * POOL_GC)
            d, cnt = _window_delta(u_ref[0, :, cols].astype(F32), w, row)
            db = d.astype(BF16)
            pw = pw_ref[g].astype(BF16)
            z = jnp.dot(db, pw, preferred_element_type=F32)
            dyv = jnp.dot(dyp_ref[0], w_ref[:, cols], preferred_element_type=F32)
            dsc_ref[:, cols] += jnp.sum(dyv * z, axis=0, keepdims=True)
            dz = (dyv * sc_ref[:, cols]).astype(BF16)
            dpw_ref[g] += lax.dot_general(db, dz, (((0,), (0,)), ((), ())), preferred_element_type=F32)
            dd = lax.dot_general(dz, pw, (((1,), (1,)), ((), ())), preferred_element_type=F32)
            acc, k = dd / cnt, 1
            while k < w:
                acc = acc + _shift_up(acc, k, row)
                k *= 2
            du_ref[0, :, cols] = (acc - dd).astype(BF16)

    return pl.pallas_call(
        body, name="pool_bwd", grid=(b,),
        out_shape=(jax.ShapeDtypeStruct((b, s, GU_COLS), BF16), jax.ShapeDtypeStruct((4, POOL_GC, POOL_GC), F32),
                   jax.ShapeDtypeStruct((1, POOL_WIDTH), F32)),
        in_specs=[pl.BlockSpec((1, s, POOL_WIDTH), lambda i: (i, 0, U_BLK)),
                  pl.BlockSpec((1, s, dyp.shape[2]), lambda i: (i, 0, 0)),
                  pl.BlockSpec(w_out_t.shape, lambda i: (0, 0)),
                  pl.BlockSpec((4, POOL_GC, POOL_GC), lambda i: (0, 0, 0)),
                  pl.BlockSpec((1, POOL_WIDTH), lambda i: (0, 0)), ANY],
        out_specs=(pl.BlockSpec((1, s, POOL_WIDTH), lambda i: (i, 0, U_BLK)),
                   pl.BlockSpec((4, POOL_GC, POOL_GC), lambda i: (0, 0, 0)),
                   pl.BlockSpec((1, POOL_WIDTH), lambda i: (0, 0))),
        input_output_aliases={5: 0},
        compiler_params=_cparams(("arbitrary",)),
    )(gu, dyp, w_out_t, pool_w, pool_scale, dgu)


def _forget_cumsum(f, bias, name):
    b, s, c = f.shape

    def body(f_ref, b_ref, c_ref):
        row = lax.broadcasted_iota(jnp.int32, (s, LANES), 0)
        z = f_ref[0] + b_ref[...]
        acc = jnp.minimum(z, 0.0) - jnp.log(1.0 + jnp.exp(-jnp.abs(z)))
        k = 1
        while k < s:
            acc = acc + _shift_down(acc, k, row)
            k *= 2
        c_ref[0] = acc

    return pl.pallas_call(
        body, name=name, out_shape=jax.ShapeDtypeStruct((b, s, c), F32), grid=(b, c // LANES),
        in_specs=[pl.BlockSpec((1, s, LANES), lambda i, j: (i, 0, j)), pl.BlockSpec((1, LANES), lambda i, j: (0, j))],
        out_specs=pl.BlockSpec((1, s, LANES), lambda i, j: (i, 0, j)),
        compiler_params=_cparams(("parallel", "parallel")),
    )(f, bias)


def _forget_bwd(dc, f, bias):
    b, s, _ = f.shape

    def body(dc_ref, f_ref, b_ref, df_ref, db_ref):
        @pl.when(pl.program_id(0) == 0)
        def _():
            db_ref[...] = jnp.zeros_like(db_ref)

        row = lax.broadcasted_iota(jnp.int32, (s, LANES), 0)
        acc, k = dc_ref[0], 1
        while k < s:
            acc = acc + _shift_up(acc, k, row)
            k *= 2
        z = f_ref[0] + b_ref[...]
        df = acc / (1.0 + jnp.exp(z))
        db_ref[...] += jnp.sum(df, axis=0, keepdims=True)
        df_ref[0] = df.astype(BF16)

    blk = pl.BlockSpec((1, s, LANES), lambda i: (i, 0, 0))
    vec = pl.BlockSpec((1, LANES), lambda i: (0, 0))
    return pl.pallas_call(
        body, name="forget_bwd", grid=(b,),
        out_shape=(jax.ShapeDtypeStruct((b, s, LANES), BF16), jax.ShapeDtypeStruct((1, LANES), F32)),
        in_specs=[blk, blk, vec], out_specs=(blk, vec),
        compiler_params=_cparams(("arbitrary",)),
    )(dc, f, bias)


KV_BLK0 = 2
PAIRS = FOX_HEADS // 2
FOX_SCALE = FOX_DH ** -0.5
NT_DIMS = (((1,), (1,)), ((), ()))
TN_DIMS = (((0,), (0,)), ((), ()))


def _stack_heads(v):
    head = lax.broadcasted_iota(jnp.int32, v.shape, 1) // FOX_DH
    zero = jnp.zeros_like(v)
    return jnp.concatenate([jnp.where(head == 0, v, zero), jnp.where(head == 1, v, zero)], axis=0)


def _stack_cols(v):
    return jnp.concatenate([v[:, 0:1], v[:, FOX_DH:FOX_DH + 1]], axis=0)


def _unstack(t, blk):
    head = lax.broadcasted_iota(jnp.int32, (blk, LANES), 1) // FOX_DH
    return jnp.where(head == 0, t[:blk], t[blk:])


def _fox_scores(q_all, kblk, row_bias, cr_ref, kb, masked, blk):
    top = lax.broadcasted_iota(jnp.int32, (2 * blk, 1), 0) < blk
    s = lax.dot_general(q_all, kblk, NT_DIMS, preferred_element_type=F32)
    s = s + (row_bias - jnp.where(top, cr_ref[0, 0, kb], cr_ref[0, 1, kb]))
    if masked:
        r = lax.broadcasted_iota(jnp.int32, (2 * blk, blk), 0)
        keep = jnp.where(r >= blk, r - blk, r) >= lax.broadcasted_iota(jnp.int32, (2 * blk, blk), 1)
        s = jnp.where(keep, s, NEG_INF)
    return s


def _fox_fwd(qkv, c_exp, c_row, ex=None):
    b, s, _ = qkv.shape
    blk = min(ATT_BLOCK, s)
    nq = s // blk

    def body(q_ref, kv_ref, cc_ref, cr_ref, o_ref, ob_ref, lse_ref):
        qi = pl.program_id(2)
        q_all = _stack_heads(q_ref[0] * FOX_SCALE)
        cq = _stack_cols(cc_ref[0])

        def step(kb, carry, masked):
            m, l, acc = carry
            rows = pl.ds(pl.multiple_of(kb * blk, blk), blk)
            sc = _fox_scores(q_all, kv_ref[0, rows, :LANES], cq, cr_ref, kb, masked, blk)
            m_new = jnp.maximum(m, jnp.max(sc, axis=-1, keepdims=True))
            p = jnp.exp(sc - m_new)
            alpha = jnp.exp(m - m_new)
            l = alpha * l + jnp.sum(p, axis=-1, keepdims=True)
            acc = alpha * acc + jnp.dot(p.astype(BF16), kv_ref[0, rows, LANES:], preferred_element_type=F32)
            return m_new, l, acc

        init = (jnp.full((2 * blk, 1), NEG_INF, F32), jnp.zeros((2 * blk, 1), F32), jnp.zeros((2 * blk, LANES), F32))
        m, l, acc = step(qi, lax.fori_loop(0, qi, functools.partial(step, masked=False), init), True)
        o = _unstack(acc / l, blk)
        o_ref[0] = o
        ob_ref[0] = o.astype(BF16)
        lse_ref[0] = _unstack(jnp.broadcast_to(m + jnp.log(l), (2 * blk, LANES)), blk)

    tile = pl.BlockSpec((1, blk, LANES), lambda i, h, q: (i, q, h))
    kvspec = pl.BlockSpec((1, s, 2 * LANES), lambda i, h, q: (i, 0, KV_BLK0 + h))
    shape = jax.ShapeDtypeStruct((b, s, FOX_WIDTH), F32)
    return _hosted_call(
        body, ex, name="fox_fwd", out_shape=(shape, jax.ShapeDtypeStruct((b, s, FOX_WIDTH), BF16), shape),
        grid=(b, PAIRS, nq),
        in_specs=[tile, kvspec, tile, pl.BlockSpec((1, 2, nq, 1, blk), lambda i, h, q: (i, h, 0, 0, 0))],
        out_specs=(tile, tile, tile), args=(qkv, qkv, c_exp, c_row))


def _fox_bwd(qkv, c_exp, c_row, lse, o, do, ex=None):
    b, s, _ = qkv.shape
    blk = min(ATT_BLOCK, s)
    nq = s // blk

    def body(q_ref, kv_ref, cc_ref, cr_ref, lse_ref, o_ref, do_ref, dq_ref, dkv_ref, dcq_ref, dc_ref, dk_acc, dv_acc):
        qi = pl.program_id(2)

        @pl.when(qi == 0)
        def _():
            dk_acc[...] = jnp.zeros_like(dk_acc)
            dv_acc[...] = jnp.zeros_like(dv_acc)
            dc_ref[...] = jnp.zeros_like(dc_ref)

        q_all = _stack_heads(q_ref[0] * FOX_SCALE)
        dov = do_ref[0]
        do_all = _stack_heads(dov.astype(BF16))
        delta = jnp.sum(_stack_heads(dov * o_ref[0]), axis=-1, keepdims=True)
        bias = _stack_cols(cc_ref[0]) - _stack_cols(lse_ref[0])

        def step(kb, carry, masked):
            acc, dcq = carry
            rows = pl.ds(pl.multiple_of(kb * blk, blk), blk)
            kblk = kv_ref[0, rows, :LANES]
            p = jnp.exp(_fox_scores(q_all, kblk, bias, cr_ref, kb, masked, blk))
            dp = lax.dot_general(do_all, kv_ref[0, rows, LANES:], NT_DIMS, preferred_element_type=F32)
            ds = p * (dp - delta)
            dsb = ds.astype(BF16)
            dv_acc[rows, :] += lax.dot_general(p.astype(BF16), do_all, TN_DIMS, preferred_element_type=F32)
            dk_acc[rows, :] += lax.dot_general(dsb, q_all, TN_DIMS, preferred_element_type=F32)
            dc_ref[0, 0, kb] -= jnp.sum(ds[:blk], axis=0, keepdims=True)
            dc_ref[0, 1, kb] -= jnp.sum(ds[blk:], axis=0, keepdims=True)
            acc = acc + jnp.dot(dsb, kblk, preferred_element_type=F32)
            return acc, dcq + jnp.sum(ds, axis=-1, keepdims=True)

        init = (jnp.zeros((2 * blk, LANES), F32), jnp.zeros((2 * blk, 1), F32))
        acc, dcq = step(qi, lax.fori_loop(0, qi, functools.partial(step, masked=False), init), True)
        dq_ref[0] = (_unstack(acc, blk) * FOX_SCALE).astype(BF16)
        dcq_ref[0, 0] = jnp.where(lax.broadcasted_iota(jnp.int32, (blk, 2), 1) == 0, dcq[:blk], dcq[blk:])

        @pl.when(qi == nq - 1)
        def _():
            dkv_ref[0, :, :LANES] = dk_acc[...].astype(BF16)
            dkv_ref[0, :, LANES:] = dv_acc[...].astype(BF16)

    tile = pl.BlockSpec((1, blk, LANES), lambda i, h, q: (i, q, h))
    kvspec = pl.BlockSpec((1, s, 2 * LANES), lambda i, h, q: (i, 0, KV_BLK0 + h))
    crow = pl.BlockSpec((1, 2, nq, 1, blk), lambda i, h, q: (i, h, 0, 0, 0))
    return _hosted_call(
        body, ex, name="fox_bwd", grid=(b, PAIRS, nq),
        out_shape=(jax.ShapeDtypeStruct((b, s, FOX_WIDTH), BF16), jax.ShapeDtypeStruct((b, s, 2 * FOX_WIDTH), BF16),
                   jax.ShapeDtypeStruct((b, PAIRS, s, 2), F32), jax.ShapeDtypeStruct(c_row.shape, F32)),
        in_specs=[tile, kvspec, tile, crow, tile, tile, tile],
        out_specs=(tile, pl.BlockSpec((1, s, 2 * LANES), lambda i, h, q: (i, 0, h)),
                   pl.BlockSpec((1, 1, blk, 2), lambda i, h, q: (i, h, q, 0)), crow),
        scratch=[pltpu.VMEM((s, LANES), F32), pltpu.VMEM((s, LANES), F32)],
        args=(qkv, qkv, c_exp, c_row, lse, o, do))


def _sigmoid(z):
    return 1.0 / (1.0 + jnp.exp(-z))


def _branches_mix(y, o, w_pool3, w_fox3, gu, b_gate):
    t = y.shape[0]
    chunk = w_pool3.shape[2]
    per_branch = D_MODEL // chunk
    bm = _block(t, 1024, 16)

    def body(y_ref, o_ref, wp_ref, wf_ref, gp_ref, gf_ref, bp_ref, bf_ref, yp_ref, yf_ref, mix_ref):
        yp = jnp.dot(y_ref[...], wp_ref[...], preferred_element_type=F32).astype(BF16)
        yf = jnp.dot(o_ref[...], wf_ref[...], preferred_element_type=F32).astype(BF16)
        yp_ref[...] = yp
        yf_ref[...] = yf
        gp = _sigmoid(gp_ref[...].astype(F32) + bp_ref[...])
        gf = _sigmoid(gf_ref[...].astype(F32) + bf_ref[...])
        mix_ref[...] = (gp * yp.astype(F32) + gf * yf.astype(F32)).astype(BF16)

    rows = pl.BlockSpec((bm, y.shape[1]), lambda i, j: (i, 0))
    weight = pl.BlockSpec((None, y.shape[1], chunk), lambda i, j: (j, 0, 0))
    tile = lambda base: pl.BlockSpec((bm, chunk), lambda i, j: (i, base + j))
    vec = lambda base: pl.BlockSpec((1, chunk), lambda i, j: (0, base + j))
    shape = jax.ShapeDtypeStruct((t, D_MODEL), BF16)
    return pl.pallas_call(
        body, name="branches_mix", out_shape=(shape, shape, shape), grid=(t // bm, per_branch),
        in_specs=[rows, rows, weight, weight, tile(0), tile(per_branch), vec(0), vec(per_branch)],
        out_specs=(tile(0), tile(0), tile(0)),
        compiler_params=_cparams(("parallel", "arbitrary")),
    )(y, o, w_pool3, w_fox3, gu, gu, b_gate, b_gate)


def _mix_bwd(gu, b_gate, y_pool, y_fox, dx, w_out_t):
    t = gu.shape[0]
    bt = _block(t, 256, 16)

    def body(gp_ref, gf_ref, bp_ref, bf_ref, yp_ref, yf_ref, dx_ref, w_ref, dyp_ref, dyf_ref, dgl_ref, db_ref):
        @pl.when(pl.program_id(0) == 0)
        def _():
            db_ref[...] = jnp.zeros_like(db_ref)

        dm = jnp.dot(dx_ref[...], w_ref[...], preferred_element_type=F32)
        gp = _sigmoid(gp_ref[...].astype(F32) + bp_ref[...])
        gf = _sigmoid(gf_ref[...].astype(F32) + bf_ref[...])
        dyp_ref[...] = (dm * gp).astype(BF16)
        dyf_ref[...] = (dm * gf).astype(BF16)
        dlp = dm * yp_ref[...].astype(F32) * gp * (1.0 - gp)
        dlf = dm * yf_ref[...].astype(F32) * gf * (1.0 - gf)
        dgl_ref[:, :D_MODEL] = dlp.astype(BF16)
        dgl_ref[:, D_MODEL:] = dlf.astype(BF16)
        db_ref[:, :D_MODEL] += jnp.sum(dlp, axis=0, keepdims=True)
        db_ref[:, D_MODEL:] += jnp.sum(dlf, axis=0, keepdims=True)

    col = lambda j: pl.BlockSpec((bt, D_MODEL), lambda i: (i, j))
    vec = lambda j: pl.BlockSpec((1, D_MODEL), lambda i: (0, j))
    wide = pl.BlockSpec((bt, GATE_WIDTH), lambda i: (i, 0))
    return pl.pallas_call(
        body, name="mix_bwd", grid=(t // bt,),
        out_shape=(jax.ShapeDtypeStruct((t, D_MODEL), BF16), jax.ShapeDtypeStruct((t, D_MODEL), BF16),
                   jax.ShapeDtypeStruct((t, GU_COLS), BF16), jax.ShapeDtypeStruct((1, GATE_WIDTH), F32)),
        in_specs=[col(0), col(1), vec(0), vec(1), col(0), col(0), col(0),
                  pl.BlockSpec(w_out_t.shape, lambda i: (0, 0))],
        out_specs=(col(0), col(0), wide, pl.BlockSpec((1, GATE_WIDTH), lambda i: (0, 0))),
        compiler_params=_cparams(("arbitrary",)),
    )(gu, gu, b_gate, b_gate, y_pool, y_fox, dx, w_out_t)


X_SCALE = X_DH ** -0.5


def _xattn_probs(qh, kh):
    s = lax.dot_general(qh, kh, NT_DIMS, preferred_element_type=F32) * X_SCALE
    e = jnp.exp(s - jnp.max(s, axis=-1, keepdims=True))
    return e / jnp.sum(e, axis=-1, keepdims=True)


def _xattn_fwd(q, kv):
    b, s, _ = q.shape
    m = kv.shape[1]
    bq = _block(s, 512, 16)

    def body(q_ref, kv_ref, o_ref):
        for h in range(X_HEADS):
            cols = slice(h * X_DH, (h + 1) * X_DH)
            p = _xattn_probs(q_ref[0, :, cols], kv_ref[0, :, cols])
            vh = kv_ref[0, :, X_WIDTH + h * X_DH:X_WIDTH + (h + 1) * X_DH]
            o_ref[0, :, cols] = jnp.dot(p.astype(BF16), vh, preferred_element_type=F32).astype(BF16)

    return pl.pallas_call(
        body, name="xattn_fwd", out_shape=jax.ShapeDtypeStruct((b, s, X_WIDTH), BF16), grid=(b, s // bq),
        in_specs=[pl.BlockSpec((1, bq, X_WIDTH), lambda i, j: (i, j, 0)),
                  pl.BlockSpec((1, m, 2 * X_WIDTH), lambda i, j: (i, 0, 0))],
        out_specs=pl.BlockSpec((1, bq, X_WIDTH), lambda i, j: (i, j, 0)),
        compiler_params=_cparams(("parallel", "parallel")),
    )(q, kv)


def _xattn_bwd(q, kv, dx, w_o_t):
    b, s, _ = q.shape
    m = kv.shape[1]
    bq = _block(s, 512, 16)

    def body(q_ref, kv_ref, dx_ref, w_ref, dq_ref, dkv_ref):
        @pl.when(pl.program_id(1) == 0)
        def _():
            dkv_ref[...] = jnp.zeros_like(dkv_ref)

        do = jnp.dot(dx_ref[0], w_ref[...], preferred_element_type=F32).astype(BF16)
        for h in range(X_HEADS):
            cols = slice(h * X_DH, (h + 1) * X_DH)
            vcols = slice(X_WIDTH + h * X_DH, X_WIDTH + (h + 1) * X_DH)
            qh, kh, vh, doh = q_ref[0, :, cols], kv_ref[0, :, cols], kv_ref[0, :, vcols], do[:, cols]
            p = _xattn_probs(qh, kh)
            dkv_ref[0, :, vcols] += lax.dot_general(p.astype(BF16), doh, TN_DIMS, preferred_element_type=F32)
            dp = lax.dot_general(doh, vh, NT_DIMS, preferred_element_type=F32)
            ds = (p * (dp - jnp.sum(p * dp, axis=-1, keepdims=True)) * X_SCALE).astype(BF16)
            dq_ref[0, :, cols] = jnp.dot(ds, kh, preferred_element_type=F32).astype(BF16)
            dkv_ref[0, :, cols] += lax.dot_general(ds, qh, TN_DIMS, preferred_element_type=F32)

    tile = pl.BlockSpec((1, bq, X_WIDTH), lambda i, j: (i, j, 0))
    mem = pl.BlockSpec((1, m, 2 * X_WIDTH), lambda i, j: (i, 0, 0))
    return pl.pallas_call(
        body, name="xattn_bwd", grid=(b, s // bq),
        out_shape=(jax.ShapeDtypeStruct((b, s, X_WIDTH), BF16), jax.ShapeDtypeStruct((b, m, 2 * X_WIDTH), F32)),
        in_specs=[tile, mem, pl.BlockSpec((1, bq, dx.shape[2]), lambda i, j: (i, j, 0)),
                  pl.BlockSpec(w_o_t.shape, lambda i, j: (0, 0))],
        out_specs=(tile, mem),
        compiler_params=_cparams(("parallel", "arbitrary")),
    )(q, kv, dx, w_o_t)


def _ffn_in(hf, w3):
    t, d = hf.shape
    chunk = w3.shape[2]
    half = N_CHIPS // 2
    bm = _block(t, 1024, 16)

    def body(a_ref, wg_ref, wu_ref, gt_ref, up_ref, act_ref):
        a = a_ref[...]
        gt = jnp.dot(a, wg_ref[...], preferred_element_type=F32).astype(BF16)
        up = jnp.dot(a, wu_ref[...], preferred_element_type=F32).astype(BF16)
        gt_ref[...] = gt
        up_ref[...] = up
        g32 = gt.astype(F32)
        act_ref[...] = (g32 * _sigmoid(g32) * up.astype(F32)).astype(BF16)

    tile = pl.BlockSpec((bm, chunk), lambda i, j: (i, j))
    shape = jax.ShapeDtypeStruct((t, half * chunk), BF16)
    return pl.pallas_call(
        body, name="ffn_in", out_shape=(shape, shape, shape), grid=(t // bm, half),
        in_specs=[pl.BlockSpec((bm, d), lambda i, j: (i, 0)),
                  pl.BlockSpec((None, d, chunk), lambda i, j: (j, 0, 0)),
                  pl.BlockSpec((None, d, chunk), lambda i, j: (j + half, 0, 0))],
        out_specs=(tile, tile, tile),
        compiler_params=_cparams(("parallel", "arbitrary")),
    )(hf, w3, w3)


def _ffn_act_bwd(dx, w_out_t, gate, up):
    t, d = dx.shape
    bt = _block(t, 256, 16)

    def body(dx_ref, w_ref, gt_ref, up_ref, o_ref):
        da = jnp.dot(dx_ref[...], w_ref[...], preferred_element_type=F32).astype(BF16).astype(F32)
        gt = gt_ref[...].astype(F32)
        sg = _sigmoid(gt)
        silu = gt * sg
        o_ref[:, :D_FF] = (da * up_ref[...].astype(F32) * (sg + silu * (1.0 - sg))).astype(BF16)
        o_ref[:, D_FF:] = (da * silu).astype(BF16)

    col = pl.BlockSpec((bt, D_FF), lambda i: (i, 0))
    return pl.pallas_call(
        body, name="ffn_act_bwd", out_shape=jax.ShapeDtypeStruct((t, 2 * D_FF), BF16), grid=(t // bt,),
        in_specs=[pl.BlockSpec((bt, d), lambda i: (i, 0)), pl.BlockSpec((d, D_FF), lambda i: (0, 0)), col, col],
        out_specs=pl.BlockSpec((bt, 2 * D_FF), lambda i: (i, 0)),
        compiler_params=_cparams(("parallel",)),
    )(dx, w_out_t, gate, up)


def _stack_of(w, axis):
    r, c = w.shape
    if axis == 0:
        return w.reshape(N_CHIPS, r // N_CHIPS, c)
    return w.reshape(r, N_CHIPS, c // N_CHIPS).transpose(1, 0, 2)


def _stack_t(w3):
    n, r, c = w3.shape
    return w3.transpose(0, 2, 1).reshape(n * c, r)


def _pair_rows(k, v):
    c = k.shape[1]
    return jnp.stack([k.reshape(PAIRS, LANES, c), v.reshape(PAIRS, LANES, c)], axis=1).reshape(2 * FOX_WIDTH, c)


def _unpair_rows(kv):
    c = kv.shape[1]
    kv = kv.reshape(PAIRS, 2, LANES, c)
    return kv[:, 0].reshape(FOX_WIDTH, c), kv[:, 1].reshape(FOX_WIDTH, c)


def _input_grad(parts, weights_t, ex):
    t = parts[0].shape[0]
    d = weights_t[0].shape[1]
    bm = _block(t, 512, 16)
    n = len(parts)

    def body(*refs):
        acc = None
        for a_ref, b_ref in zip(refs[:n], refs[n:2 * n]):
            term = jnp.dot(a_ref[...], b_ref[...], preferred_element_type=F32)
            acc = term if acc is None else acc + term
        refs[2 * n][...] = acc

    (out,), moved = _hosted_call(
        body, ex, name="d_h", grid=(t // bm,), out_shape=(jax.ShapeDtypeStruct((t, d), F32),),
        in_specs=[pl.BlockSpec((bm, p.shape[1]), lambda i: (i, 0)) for p in parts]
        + [pl.BlockSpec(w.shape, lambda i: (0, 0)) for w in weights_t],
        out_specs=(pl.BlockSpec((bm, d), lambda i: (i, 0)),), args=tuple(parts) + tuple(weights_t))
    return out, moved


def _step(x, mem, loss_target, weights, moments_m, moments_v):
    nb, s, d = x.shape
    n_mem = mem.shape[1]
    t = nb * s
    blk = min(ATT_BLOCK, s)
    x2 = x.reshape(t, d)
    mem2 = mem.reshape(nb * n_mem, d)
    tgt2 = loss_target.reshape(t, d)

    def shard2d(a, n):
        a = a.reshape(a.shape[1:])
        return a.T if n == "w_in" else a

    def unshard(a, n):
        return (a.T if n == "w_in" else a)[None]

    local = {n: shard2d(weights[n], n) for n, _, _ in SHARDED}

    names = [n for n, _, _ in SHARDED]
    last = ["w_ffn_out"]
    later = [n for n in names if n != "w_in" and n not in last]
    local_b = {n: local[n].astype(BF16) for n in names}
    g_mix = weights["norm_mix_g"]
    h, w_in_others = _rms_fwd(x2, g_mix, "norm_mix", ex=_gather_exchange([local_b["w_in"]]))
    w_in_stack, = _place_own(w_in_others, [local_b["w_in"]])

    def w_in_rows(lo, hi):
        per = IN_COLS // N_CHIPS
        parts = [w_in_stack[j, max(lo, j * per) - j * per:min(hi, (j + 1) * per) - j * per]
                 for j in range(N_CHIPS) if max(lo, j * per) < min(hi, (j + 1) * per)]
        return parts[0] if len(parts) == 1 else jnp.concatenate(parts)

    w_gu_t = jnp.concatenate([w_in_rows(2056, IN_COLS), w_in_rows(0, 512)])
    w_qkv_t = jnp.concatenate([w_in_rows(512, 1024), _pair_rows(w_in_rows(1024, 1536), w_in_rows(1536, 2048))])
    w_f_t = jnp.pad(w_in_rows(2048, 2056), ((0, LANES - FOX_HEADS), (0, 0)))
    w_gu, w_qkv, w_f = w_gu_t.T, w_qkv_t.T, w_f_t.T
    w_f_exp = jnp.repeat(w_f[:, :FOX_HEADS], FOX_DH, axis=1)

    g_mix, g_x, g_mem, g_ffn = (weights[n] for n in ("norm_mix_g", "norm_x_g", "norm_mem_g", "norm_ffn_g"))
    g_final = weights["norm_final_g"].reshape(1, d)
    pool_w = weights["pool_w"].reshape(4, POOL_GC, POOL_GC)
    pool_scale, b_gate = weights["pool_scale"], weights["b_gate"]
    b_f_pad = jnp.pad(weights["b_forget"], ((0, 0), (0, LANES - FOX_HEADS)))
    b_f_exp = jnp.repeat(weights["b_forget"], FOX_DH, axis=1)

    gu, last_others = _mm(h, w_gu, out_dtype=BF16, bn=512, name="in_proj_gates_pool",
                          ex=_gather_exchange([local_b[n] for n in last]))
    qkv, f_exp, f_pad = _in_proj_attn(h, w_qkv, w_f_exp, w_f)
    gu3, qkv3 = gu.reshape(nb, s, GU_COLS), qkv.reshape(nb, s, 3 * FOX_WIDTH)
    y = _pool_fwd(gu3, pool_w, pool_scale)
    c_exp = _forget_cumsum(f_exp.reshape(nb, s, FOX_WIDTH), b_f_exp, "forget_cumsum_lanes")
    c_pad = _forget_cumsum(f_pad.reshape(nb, s, LANES), b_f_pad, "forget_cumsum")
    c_row = c_pad[:, :, :FOX_HEADS].transpose(0, 2, 1).reshape(nb, FOX_HEADS, s // blk, 1, blk)
    (o, o_b, lse), gathered = _fox_fwd(qkv3, c_exp, c_row, ex=_gather_exchange([local_b[n] for n in later]))
    stacks = dict(zip(later, _place_own(gathered, [local_b[n] for n in later])))
    stacks.update(zip(last, _place_own(last_others, [local_b[n] for n in last])))
    w_pool_out3, w_fox_out3, w_xo3, w_ffn_in3 = (stacks[n] for n in ("w_pool_out", "w_fox_out", "w_xo", "w_ffn_in"))
    w_out, w_xq, w_xkv, w_ffn_out = (stacks[n].reshape(-1, stacks[n].shape[2])
                                     for n in ("w_out", "w_xq", "w_xkv", "w_ffn_out"))
    y2, o2 = y.reshape(t, POOL_WIDTH), o_b.reshape(t, FOX_WIDTH)
    y_pool, y_fox, mix = _branches_mix(y2, o2, w_pool_out3, w_fox_out3, gu, b_gate)
    x1, hx = _mm_res_norm(mix, w_out, x2, g_x, "mix_out_norm_x")
    mem_n = _rms_fwd(mem2, g_mem, "norm_mem")
    qx = _mm(hx, w_xq, out_dtype=BF16, name="x_q")
    kv = _mm(mem_n, w_xkv, out_dtype=BF16, name="x_kv")
    qx3, kv3 = qx.reshape(nb, s, X_WIDTH), kv.reshape(nb, n_mem, 2 * X_WIDTH)
    ox = _xattn_fwd(qx3, kv3).reshape(t, X_WIDTH)
    w_xo = w_xo3.transpose(1, 0, 2).reshape(X_WIDTH, D_MODEL)
    x2_, hf = _mm_res_norm(ox, w_xo, x1, g_ffn, "x_out_norm_ffn")
    ffn_gate, ffn_up, act = _ffn_in(hf, w_ffn_in3)

    dx3, dx3_b, dg_final, loss_part = _ffn_out_loss(act, w_ffn_out, x2_, tgt2, g_final)
    dw_ffn_out = _mm(act, dx3_b, ta=True, bm=1408, bn=1024, bk=2048, name="d_w_ffn_out")
    dffn = _ffn_act_bwd(dx3_b, w_ffn_out.T, ffn_gate, ffn_up)
    dw_ffn_in = _mm(hf, dffn, ta=True, bm=1024, bn=1408, bk=2048, out_stack=True, name="d_w_ffn_in")
    core = lax.axis_index("c").astype(jnp.int32).reshape(1)
    ffn_group = ["w_ffn_in", "w_ffn_out"]
    mid_group = ["w_pool_out", "w_fox_out", "w_out", "w_xq", "w_xkv", "w_xo"]
    grad_stacks = {"w_ffn_in": dw_ffn_in, "w_ffn_out": _stack_of(dw_ffn_out, 0)}

    def presum(group, theirs):
        return [_sum_halves(grad_stacks[n], t_, core, "sum_halves_" + n) for n, t_ in zip(group, theirs)]

    (dx2, dx2_b, dg_ffn), theirs = _mm_norm_bwd(dffn, _stack_t(w_ffn_in3), x2_, g_ffn, dx3, "d_hf_norm_ffn_bwd",
                                                ex=_swap_exchange([grad_stacks[n] for n in ffn_group]))
    chip_sums = dict(zip(ffn_group, presum(ffn_group, theirs)))

    dw_xo = _mm(ox, dx2_b, ta=True, bn=256, out_stack=True, name="d_w_xo")
    dqx, dkv = _xattn_bwd(qx3, kv3, dx2_b.reshape(nb, s, d), _stack_t(w_xo3))
    dqx2, dkv2 = dqx.reshape(t, X_WIDTH), dkv.reshape(nb * n_mem, 2 * X_WIDTH)
    dw_xkv = _mm(mem_n, dkv2, ta=True, name="d_w_xkv")
    dmem_n = _mm(dkv2, w_xkv.T, name="d_mem_n")
    dg_mem = _rms_bwd(dmem_n, mem2, g_mem, None, "norm_mem_bwd")
    dw_xq = _mm(hx, dqx2, ta=True, name="d_w_xq")
    dx1, dx1_b, dg_x = _mm_norm_bwd(dqx2, w_xq.T, x1, g_x, dx2, "d_hx_norm_x_bwd")

    dw_out = _mm(mix, dx1_b, ta=True, name="d_w_out")
    dyp, dyf, dgu, db_gate = _mix_bwd(gu, b_gate, y_pool, y_fox, dx1_b, w_out.T)
    dw_pool_out = _mm(y2, dyp, ta=True, bn=256, out_stack=True, name="d_w_pool_out")
    dw_fox_out = _mm(o2, dyf, ta=True, bn=256, out_stack=True, name="d_w_fox_out")
    do = _mm(dyf, _stack_t(w_fox_out3), name="d_o").reshape(nb, s, FOX_WIDTH)
    dgu3, dpool_w, dpool_scale = _pool_bwd(gu3, dyp.reshape(nb, s, d), _stack_t(w_pool_out3), pool_w, pool_scale,
                                           dgu.reshape(nb, s, GU_COLS))
    grad_stacks.update({"w_pool_out": dw_pool_out, "w_fox_out": dw_fox_out, "w_out": _stack_of(dw_out, 0),
                        "w_xq": _stack_of(dw_xq, 0), "w_xkv": _stack_of(dw_xkv, 0), "w_xo": dw_xo})
    dgu2 = dgu3.reshape(t, GU_COLS)
    dw_gu_t, theirs = _mm(dgu2, h, ta=True, name="d_w_gates_pool",
                          ex=_swap_exchange([grad_stacks[n] for n in mid_group]))
    chip_sums.update(zip(mid_group, presum(mid_group, theirs)))
    early = ffn_group + mid_group
    (dq3, dkv3, dc_q, dc_row), early_slots = _fox_bwd(qkv3, c_exp, c_row, lse, o, do,
                                                      ex=_chips_exchange([chip_sums[n] for n in early]))
    slots = dict(zip(early, early_slots))
    dc = dc_row.reshape(nb, FOX_HEADS, s).transpose(0, 2, 1) + dc_q.transpose(0, 2, 1, 3).reshape(nb, s, FOX_HEADS)
    dc = jnp.pad(dc, ((0, 0), (0, 0), (0, LANES - FOX_HEADS)))
    df, db_f = _forget_bwd(dc, f_pad.reshape(nb, s, LANES), b_f_pad)
    dq2, dkv2, df2 = dq3.reshape(t, FOX_WIDTH), dkv3.reshape(t, 2 * FOX_WIDTH), df.reshape(t, LANES)
    dw_q_t = _mm(dq2, h, ta=True, name="d_w_q")
    dw_kv_t = _mm(dkv2, h, ta=True, name="d_w_kv")
    dw_f_t = _mm(df2, h, ta=True, name="d_w_forget")
    dw_k_t, dw_v_t = _unpair_rows(dw_kv_t)
    dw_in_t = jnp.concatenate([dw_gu_t[GATE_WIDTH:], dw_q_t, dw_k_t, dw_v_t, dw_f_t[:FOX_HEADS],
                               dw_gu_t[:GATE_WIDTH]])
    grad_stacks["w_in"] = dw_in_t.reshape(N_CHIPS, IN_COLS // N_CHIPS, D_MODEL)
    chip_sums["w_in"], = presum(["w_in"], _run_exchange(_swap_exchange([grad_stacks["w_in"]]), "swap_halves_w_in"))
    dh, (slots["w_in"],) = _input_grad([dgu2, dq2, dkv2, df2],
                                       [w_gu_t, w_qkv_t[:FOX_WIDTH], w_qkv_t[FOX_WIDTH:], w_f_t],
                                       _chips_exchange([chip_sums["w_in"]]))

    place = jnp.stack([lax.axis_index("c"), 2 * lax.axis_index("x") + lax.axis_index("y")]).astype(jnp.int32)
    halves = [_sum_chips(slots[n], chip_sums[n], place, _by_rows(local[n].shape[0]), "sum_chips_" + n) for n in names]
    (dx, _, dg_mix), reduced = _rms_bwd(dh, x2, g_mix, dx1, "norm_mix_bwd", ex=_join_exchange(halves))

    small_grads = {"norm_mix_g": dg_mix, "b_forget": db_f[:, :FOX_HEADS], "b_gate": db_gate, "pool_w": dpool_w,
                   "pool_scale": dpool_scale, "norm_x_g": dg_x, "norm_mem_g": dg_mem, "norm_ffn_g": dg_ffn,
                   "norm_final_g": dg_final}
    def flat2d(a):
        return a.reshape(-1, a.shape[-1])

    small_names = [n for n, _ in SMALL]
    own = [flat2d(small_grads[n]) for n in small_names]
    small_gather = _small_exchange(own + [loss_part])

    def tiles_of(a):
        return a.transpose(2, 0, 1)

    def block_of(a3):
        return a3.transpose(1, 2, 0)

    grads, deltas, new_m, new_v = {}, {}, {}, {}
    gathered = None
    for n, g_ in zip(names, reduced):
        if n == "w_in":
            g_ = lax.optimization_barrier(g_.reshape(IN_COLS // N_CHIPS, 1, D_MODEL))
            (d_, m_, v_), gathered = _adamw(tiles_of(weights[n]), g_, tiles_of(moments_m[n]), tiles_of(moments_v[n]),
                                            "adamw_" + n, ex=small_gather)
            back = block_of
        else:
            d_, m_, v_ = _adamw(local[n], g_, shard2d(moments_m[n], n), shard2d(moments_v[n], n), "adamw_" + n)
            back = functools.partial(unshard, n=n)
        grads[n], deltas[n], new_m[n], new_v[n] = (back(a) for a in (g_, d_, m_, v_))

    device = (4 * lax.axis_index("x") + 2 * lax.axis_index("y") + lax.axis_index("c")).astype(jnp.int32).reshape(1)
    sg, sd, sm, sv, loss_sum = _adamw_small(
        gathered[:-1], own, [flat2d(weights[n]) for n in small_names], [flat2d(moments_m[n]) for n in small_names],
        [flat2d(moments_v[n]) for n in small_names], gathered[-1], loss_part, device)
    for n, g_, d_, m_, v_ in zip(small_names, sg, sd, sm, sv):
        grads[n], deltas[n], new_m[n], new_v[n] = (a.reshape(weights[n].shape) for a in (g_, d_, m_, v_))
    return loss_sum[0, 0], dx.reshape(nb, s, d), grads, deltas, new_m, new_v


def kernel(x, mem, norm_mix_g, w_in, b_forget, b_gate, pool_w, pool_scale, w_pool_out, w_fox_out, w_out, norm_x_g, norm_mem_g, w_xq, w_xkv, w_xo, norm_ffn_g, w_ffn_in, w_ffn_out, norm_final_g, loss_target, m_norm_mix_g, m_w_in, m_b_forget, m_b_gate, m_pool_w, m_pool_scale, m_w_pool_out, m_w_fox_out, m_w_out, m_norm_x_g, m_norm_mem_g, m_w_xq, m_w_xkv, m_w_xo, m_norm_ffn_g, m_w_ffn_in, m_w_ffn_out, m_norm_final_g, v_norm_mix_g, v_w_in, v_b_forget, v_b_gate, v_pool_w, v_pool_scale, v_w_pool_out, v_w_fox_out, v_w_out, v_norm_x_g, v_norm_mem_g, v_w_xq, v_w_xkv, v_w_xo, v_norm_ffn_g, v_w_ffn_in, v_w_ffn_out, v_norm_final_g):
    given = dict(locals())
    weights = {n: given[n] for n in WEIGHT_ORDER}
    moments_m = {n: given["m_" + n] for n in WEIGHT_ORDER}
    moments_v = {n: given["v_" + n] for n in WEIGHT_ORDER}
    loss, grad_x, grads, deltas, new_m, new_v = _step(x, mem, loss_target, weights, moments_m, moments_v)
    return (loss, grad_x, *[grads[n] for n in WEIGHT_ORDER], *[deltas[n] for n in WEIGHT_ORDER],
            *[new_m[n] for n in WEIGHT_ORDER], *[new_v[n] for n in WEIGHT_ORDER])
```

```python
import functools
import math

import jax
import jax.numpy as jnp
from jax import lax
from jax.experimental import pallas as pl
from jax.experimental.pallas import tpu as pltpu

F32 = jnp.float32
BF16 = jnp.bfloat16
MESH = pl.DeviceIdType.MESH

D_MODEL = 1024
EPS = 1e-6
POOL_WINDOWS = (2, 4, 8, 16)
POOL_WIDTH = 512
POOL_GC = 128
FOX_HEADS = 8
FOX_DH = 64
FOX_WIDTH = 512
X_HEADS = 4
X_DH = 128
X_WIDTH = 512
D_FF = 2816
IN_COLS = 4104
GATE_WIDTH = 2048
ADAM_LR = 0.001
ADAM_B1 = 0.9
ADAM_B2 = 0.999
ADAM_EPS = 1e-08
ADAM_WD = 0.01
ADAM_STEP = 10

N_CHIPS = 4
N_DEV = 8
LANES = 128
VMEM_LIMIT_BYTES = 56 * 1024 * 1024
NEG_INF = -1e30
ATT_BLOCK = 512

SHARDED = (
    ("w_in", (1024, IN_COLS), 1),
    ("w_pool_out", (POOL_WIDTH, 1024), 1),
    ("w_fox_out", (FOX_WIDTH, 1024), 1),
    ("w_out", (1024, 1024), 0),
    ("w_xq", (1024, X_WIDTH), 0),
    ("w_xkv", (1024, 2 * X_WIDTH), 0),
    ("w_xo", (X_WIDTH, 1024), 1),
    ("w_ffn_in", (1024, 2 * D_FF), 1),
    ("w_ffn_out", (D_FF, 1024), 0),
)
SMALL = (
    ("norm_mix_g", (1, 1024)),
    ("b_forget", (1, 8)),
    ("b_gate", (1, 2048)),
    ("pool_w", (1, 4, 128, 128)),
    ("pool_scale", (1, 512)),
    ("norm_x_g", (1, 1024)),
    ("norm_mem_g", (1, 1024)),
    ("norm_ffn_g", (1, 1024)),
    ("norm_final_g", (1024,)),
)
WEIGHT_ORDER = ("norm_mix_g", "w_in", "b_forget", "b_gate", "pool_w", "pool_scale", "w_pool_out", "w_fox_out", "w_out",
                "norm_x_g", "norm_mem_g", "w_xq", "w_xkv", "w_xo", "norm_ffn_g", "w_ffn_in", "w_ffn_out", "norm_final_g")


def _cparams(sem=None):
    return pltpu.CompilerParams(dimension_semantics=sem, vmem_limit_bytes=VMEM_LIMIT_BYTES)


def _block(dim, pref, unit):
    if dim <= pref:
        return dim
    best = None
    for b in range(unit, pref + 1, unit):
        if dim % b == 0:
            best = b
    assert best is not None, (dim, pref, unit)
    return best


def _rows_block(rows, cols, unit=16, elems=1 << 19):
    return _block(rows, max(unit, elems // cols // unit * unit), unit)


def _my_place():
    return lax.axis_index("x"), lax.axis_index("y"), lax.axis_index("c")


def _other_chips(x, y):
    return [(1 - x, y), (x, 1 - y), (1 - x, 1 - y)]


def _chip(place):
    return 2 * place[0] + place[1]


ANY = pl.BlockSpec(memory_space=pl.ANY)


def _by_rows(rows):
    return rows % 32 == 0


def _half_shape(rows, cols):
    return (rows // 2, cols) if _by_rows(rows) else (rows, cols // 2)


def _core_half(ref, core, lead=()):
    rows, cols = ref.shape[-2:]
    if _by_rows(rows):
        return ref.at[(*lead, pl.ds(core * (rows // 2), rows // 2), slice(None))]
    return ref.at[(*lead, slice(None), pl.ds(core * (cols // 2), cols // 2))]


class _Exchange:
    def __init__(self, arrays, out_shapes, n_sems, start, finish, in_place=False):
        self.arrays, self.out_shapes, self.n_sems, self.start, self.finish = arrays, out_shapes, n_sems, start, finish
        self.in_place = in_place

    def scratch(self):
        return [pltpu.SemaphoreType.DMA((self.n_sems,)), pltpu.SemaphoreType.DMA((self.n_sems,))]

    def aliases(self, first_in, first_out):
        return {first_in + k: first_out + k for k in range(len(self.arrays))} if self.in_place else {}


def _run_exchange(ex, name):
    n = len(ex.arrays)

    def body(*refs):
        ins, outs, sems = refs[:n], refs[n:2 * n], refs[2 * n:]
        ex.start(ins, outs, *sems)
        ex.finish(ins, outs, *sems)

    return pl.pallas_call(
        body, name=name, out_shape=ex.out_shapes, in_specs=[ANY] * n, out_specs=[ANY] * n, scratch_shapes=ex.scratch(),
        input_output_aliases=ex.aliases(0, 0),
    )(*ex.arrays)


def _hosted_call(body, ex, *, name, grid, in_specs, out_specs, out_shape, args, scratch=()):
    n_in, n_out, n_scr = len(args), len(out_shape), len(scratch)
    if ex is None:
        outs = pl.pallas_call(
            body, name=name, grid=grid, out_shape=out_shape, in_specs=in_specs, out_specs=out_specs,
            scratch_shapes=list(scratch), compiler_params=_cparams(("arbitrary",) * len(grid)))(*args)
        return outs, None
    nc = len(ex.arrays)

    def full_body(*refs):
        ins, cins = refs[:n_in], refs[n_in:n_in + nc]
        outs, couts = refs[n_in + nc:n_in + nc + n_out], refs[n_in + nc + n_out:n_in + 2 * nc + n_out]
        rest = refs[n_in + 2 * nc + n_out:]
        scr, sems = rest[:n_scr], rest[n_scr:]
        first = functools.reduce(jnp.logical_and, [pl.program_id(a) == 0 for a in range(len(grid))])
        last = functools.reduce(jnp.logical_and, [pl.program_id(a) == grid[a] - 1 for a in range(len(grid))])

        @pl.when(first)
        def _():
            ex.start(cins, couts, *sems)

        body(*ins, *outs, *scr)

        @pl.when(last)
        def _():
            ex.finish(cins, couts, *sems)

    outs = pl.pallas_call(
        full_body, name=name, grid=grid, out_shape=list(out_shape) + list(ex.out_shapes),
        in_specs=list(in_specs) + [ANY] * nc, out_specs=list(out_specs) + [ANY] * nc,
        scratch_shapes=list(scratch) + ex.scratch(), input_output_aliases=ex.aliases(n_in, n_out),
        compiler_params=_cparams(("arbitrary",) * len(grid)))(*args, *ex.arrays)
    return outs[:n_out], outs[n_out:]


def _gather_exchange(shards):
    n = len(shards)

    def copies(ins, outs, send_sems, recv_sems):
        x, y, c = _my_place()

        def half(k, chip, core):
            return _core_half(outs[k], core, lead=(_chip(chip),))

        def copy(k, slot, chip, core, to, src=None):
            return pltpu.make_async_remote_copy(
                src_ref=half(k, chip, core) if src is None else src, dst_ref=half(k, chip, core),
                send_sem=send_sems.at[6 * k + slot], recv_sem=recv_sems.at[6 * k + slot],
                device_id=to, device_id_type=MESH)

        return (x, y, c), copy

    def first_copies(ins, outs, send_sems, recv_sems):
        (x, y, c), copy = copies(ins, outs, send_sems, recv_sems)
        out = []
        for j, chip in enumerate(_other_chips(x, y)):
            for k in range(n):
                out.append(copy(k, j, (x, y), c, (*chip, c), src=_core_half(ins[k], c)))
        return out

    def start(ins, outs, send_sems, recv_sems):
        for cp in first_copies(ins, outs, send_sems, recv_sems):
            cp.start()

    def finish(ins, outs, send_sems, recv_sems):
        (x, y, c), copy = copies(ins, outs, send_sems, recv_sems)
        chips = _other_chips(x, y)
        passed = []
        for j, chip in enumerate(chips):
            for k in range(n):
                copy(k, j, chip, c, (x, y, c)).wait_recv()
                passed.append(copy(k, 3 + j, chip, c, (x, y, 1 - c)))
                passed[-1].start()
        for j, chip in enumerate(chips):
            for k in range(n):
                copy(k, 3 + j, chip, 1 - c, (x, y, c)).wait_recv()
        for cp in first_copies(ins, outs, send_sems, recv_sems) + passed:
            cp.wait_send()

    return _Exchange(list(shards), [jax.ShapeDtypeStruct((N_CHIPS,) + s.shape, s.dtype) for s in shards], 6 * n,
                     start, finish)


def _place_own(stacks, shards):
    me = 2 * lax.axis_index("x") + lax.axis_index("y")
    return [lax.dynamic_update_slice(others, mine[None], (me, 0, 0)) for others, mine in zip(stacks, shards)]


def _swap_exchange(grads):
    n = len(grads)

    def copies(ins, outs, send_sems, recv_sems):
        x, y, c = _my_place()
        return [pltpu.make_async_remote_copy(
            src_ref=_core_half(ins[k], 1 - c, lead=(slice(None),)), dst_ref=outs[k],
            send_sem=send_sems.at[k], recv_sem=recv_sems.at[k], device_id=(x, y, 1 - c), device_id_type=MESH)
            for k in range(n)]

    def start(ins, outs, send_sems, recv_sems):
        for cp in copies(ins, outs, send_sems, recv_sems):
            cp.start()

    def finish(ins, outs, send_sems, recv_sems):
        for cp in copies(ins, outs, send_sems, recv_sems):
            cp.wait()

    return _Exchange(list(grads), [jax.ShapeDtypeStruct((N_CHIPS,) + _half_shape(*g.shape[1:]), g.dtype) for g in grads],
                     n, start, finish)


def _chips_exchange(sums):
    n = len(sums)

    def sends(ins, outs, send_sems, recv_sems):
        x, y, c = _my_place()
        return [pltpu.make_async_remote_copy(
            src_ref=ins[k].at[_chip(chip)], dst_ref=outs[k].at[_chip((x, y))],
            send_sem=send_sems.at[3 * k + j], recv_sem=recv_sems.at[3 * k + j],
            device_id=(*chip, c), device_id_type=MESH)
            for j, chip in enumerate(_other_chips(x, y)) for k in range(n)]

    def start(ins, outs, send_sems, recv_sems):
        for cp in sends(ins, outs, send_sems, recv_sems):
            cp.start()

    def finish(ins, outs, send_sems, recv_sems):
        x, y, c = _my_place()
        for j, chip in enumerate(_other_chips(x, y)):
            for k in range(n):
                slot = outs[k].at[_chip(chip)]
                pltpu.make_async_remote_copy(
                    src_ref=slot, dst_ref=slot, send_sem=send_sems.at[3 * k + j], recv_sem=recv_sems.at[3 * k + j],
                    device_id=(x, y, c), device_id_type=MESH).wait_recv()
        for cp in sends(ins, outs, send_sems, recv_sems):
            cp.wait_send()

    return _Exchange(list(sums), [jax.ShapeDtypeStruct(s.shape, s.dtype) for s in sums], 3 * n, start, finish)


def _join_exchange(shards):
    n = len(shards)

    def sends(ins, outs, send_sems, recv_sems):
        x, y, c = _my_place()
        return [pltpu.make_async_remote_copy(
            src_ref=_core_half(ins[k], c), dst_ref=_core_half(outs[k], c),
            send_sem=send_sems.at[k], recv_sem=recv_sems.at[k], device_id=(x, y, 1 - c), device_id_type=MESH)
            for k in range(n)]

    def start(ins, outs, send_sems, recv_sems):
        for cp in sends(ins, outs, send_sems, recv_sems):
            cp.start()

    def finish(ins, outs, send_sems, recv_sems):
        x, y, c = _my_place()
        for k in range(n):
            theirs = _core_half(outs[k], 1 - c)
            pltpu.make_async_remote_copy(
                src_ref=theirs, dst_ref=theirs, send_sem=send_sems.at[k], recv_sem=recv_sems.at[k],
                device_id=(x, y, c), device_id_type=MESH).wait_recv()
        for cp in sends(ins, outs, send_sems, recv_sems):
            cp.wait_send()

    return _Exchange(list(shards), [jax.ShapeDtypeStruct(s.shape, s.dtype) for s in shards], n, start, finish,
                     in_place=True)


def _small_exchange(blocks):
    n = len(blocks)

    def copies(ins, outs, send_sems, recv_sems):
        x, y, c = _my_place()

        def copy(k, j, whose, to, src=None):
            slot = outs[k].at[4 * whose[0] + 2 * whose[1] + whose[2]]
            return pltpu.make_async_remote_copy(
                src_ref=slot if src is None else src, dst_ref=slot,
                send_sem=send_sems.at[7 * k + j], recv_sem=recv_sems.at[7 * k + j], device_id=to, device_id_type=MESH)

        return (x, y, c), copy

    def first_copies(ins, outs, send_sems, recv_sems):
        (x, y, c), copy = copies(ins, outs, send_sems, recv_sems)
        out = []
        for k in range(n):
            out.append(copy(k, 0, (x, y, c), (x, y, 1 - c), src=ins[k]))
            out += [copy(k, 1 + j, (x, y, c), (*chip, c), src=ins[k]) for j, chip in enumerate(_other_chips(x, y))]
        return out

    def start(ins, outs, send_sems, recv_sems):
        for cp in first_copies(ins, outs, send_sems, recv_sems):
            cp.start()

    def finish(ins, outs, send_sems, recv_sems):
        (x, y, c), copy = copies(ins, outs, send_sems, recv_sems)
        chips = _other_chips(x, y)
        passed = []
        for j, chip in enumerate(chips):
            for k in range(n):
                copy(k, 1 + j, (*chip, c), (x, y, c)).wait_recv()
                passed.append(copy(k, 4 + j, (*chip, c), (x, y, 1 - c)))
                passed[-1].start()
        for k in range(n):
            copy(k, 0, (x, y, 1 - c), (x, y, c)).wait_recv()
        for j, chip in enumerate(chips):
            for k in range(n):
                copy(k, 4 + j, (*chip, 1 - c), (x, y, c)).wait_recv()
        for cp in first_copies(ins, outs, send_sems, recv_sems) + passed:
            cp.wait_send()

    return _Exchange(list(blocks), [jax.ShapeDtypeStruct((N_DEV,) + blk.shape, blk.dtype) for blk in blocks], 7 * n,
                     start, finish)


def _sum_halves(grads, theirs, core, name):
    _, h, cols = theirs.shape
    by_rows = _by_rows(grads.shape[1])
    br = _rows_block(h, cols) if by_rows else h
    nb = h // br

    def body(core_ref, a_ref, b_ref, o_ref):
        o_ref[...] = (a_ref[...] + b_ref[...]).astype(BF16)

    if by_rows:
        mine = pl.BlockSpec((1, br, cols), lambda j, i, core_ref: (j, core_ref[0] * nb + i, 0))
    else:
        mine = pl.BlockSpec((1, br, cols), lambda j, i, core_ref: (j, i, core_ref[0]))
    return pl.pallas_call(
        body, name=name,
        out_shape=jax.ShapeDtypeStruct(theirs.shape, BF16),
        grid_spec=pltpu.PrefetchScalarGridSpec(
            num_scalar_prefetch=1, grid=(N_CHIPS, nb),
            in_specs=[mine, pl.BlockSpec((1, br, cols), lambda j, i, core_ref: (j, i, 0))],
            out_specs=pl.BlockSpec((1, br, cols), lambda j, i, core_ref: (j, i, 0))),
        compiler_params=_cparams(("parallel", "parallel")),
    )(core, grads, theirs)


def _sum_chips(slots, sums, place, by_rows, name):
    _, h, cols = slots.shape
    br = _rows_block(h, cols) if by_rows else h
    nb = h // br

    def body(place_ref, s_ref, own_ref, o_ref):
        me = place_ref[1]
        acc = None
        for k in range(N_CHIPS):
            term = jnp.where(me == k, own_ref[k], s_ref[k]).astype(F32)
            acc = term if acc is None else acc + term
        o_ref[...] = acc

    stack = pl.BlockSpec((N_CHIPS, br, cols), lambda i, place_ref: (0, i, 0))
    if by_rows:
        out_shape, out_map = (2 * h, cols), lambda i, place_ref: (place_ref[0] * nb + i, 0)
    else:
        out_shape, out_map = (h, 2 * cols), lambda i, place_ref: (i, place_ref[0])
    return pl.pallas_call(
        body, name=name,
        out_shape=jax.ShapeDtypeStruct(out_shape, F32),
        grid_spec=pltpu.PrefetchScalarGridSpec(
            num_scalar_prefetch=1, grid=(nb,), in_specs=[stack, stack],
            out_specs=pl.BlockSpec((br, cols), out_map)),
        compiler_params=_cparams(("parallel",)),
    )(place, slots, sums)


def _adamw_math(w, g, m, v):
    m = ADAM_B1 * m + (1.0 - ADAM_B1) * g
    v = ADAM_B2 * v + (1.0 - ADAM_B2) * (g * g)
    m_hat = m / (1.0 - ADAM_B1 ** ADAM_STEP)
    v_hat = v / (1.0 - ADAM_B2 ** ADAM_STEP)
    delta = -ADAM_LR * (m_hat / (jnp.sqrt(v_hat) + ADAM_EPS) + ADAM_WD * w)
    return delta, m, v


def _adamw(w, g, m, v, name, ex=None):
    def body(w_ref, g_ref, m_ref, v_ref, d_ref, nm_ref, nv_ref):
        d, nm, nv = _adamw_math(w_ref[...], g_ref[...], m_ref[...], v_ref[...])
        d_ref[...] = d
        nm_ref[...] = nm
        nv_ref[...] = nv

    if w.ndim == 3:
        rows = w.shape[0]
        br = max(b for b in range(1, 65) if rows % b == 0)
        spec, steps = pl.BlockSpec((br,) + w.shape[1:], lambda i: (i, 0, 0)), rows // br
    else:
        rows, cols = w.shape
        br = _rows_block(rows, cols, unit=8)
        spec, steps = pl.BlockSpec((br, cols), lambda i: (i, 0)), rows // br
    shape = jax.ShapeDtypeStruct(w.shape, F32)
    outs, moved = _hosted_call(
        body, ex, name=name, out_shape=(shape, shape, shape), grid=(steps,),
        in_specs=[spec] * 4, out_specs=(spec, spec, spec), args=(w, g, m, v))
    return tuple(outs) if ex is None else (tuple(outs), moved)


def _adamw_small(parts, own, ws, ms, vs, loss_parts, loss_own, device):
    n = len(ws)

    def total(device_ref, parts_ref, own_ref):
        acc = None
        for dev in range(N_DEV):
            term = jnp.where(device_ref[0] == dev, own_ref[...], parts_ref[dev])
            acc = term if acc is None else acc + term
        return acc

    def body(device_ref, *refs):
        ins, outs = refs[:5 * n + 2], refs[5 * n + 2:]
        for k in range(n):
            g = total(device_ref, ins[k], ins[n + k])
            d, nm, nv = _adamw_math(ins[2 * n + k][...], g, ins[3 * n + k][...], ins[4 * n + k][...])
            for o_ref, val in zip(outs[k::n][:4], (g, d, nm, nv)):
                o_ref[...] = val
        outs[4 * n][...] = total(device_ref, ins[5 * n], ins[5 * n + 1])

    args = list(parts) + list(own) + list(ws) + list(ms) + list(vs) + [loss_parts, loss_own]
    whole = lambda a: pl.BlockSpec(a.shape, lambda i, device_ref, nd=a.ndim: (0,) * nd)
    shapes = [jax.ShapeDtypeStruct(w.shape, F32) for w in ws] * 4 + [jax.ShapeDtypeStruct(loss_own.shape, F32)]
    outs = pl.pallas_call(
        body, name="adamw_small", out_shape=shapes,
        grid_spec=pltpu.PrefetchScalarGridSpec(
            num_scalar_prefetch=1, grid=(1,), in_specs=[whole(a) for a in args], out_specs=[whole(a) for a in shapes]),
        compiler_params=_cparams(("arbitrary",)),
    )(device, *args)
    return outs[:n], outs[n:2 * n], outs[2 * n:3 * n], outs[3 * n:4 * n], outs[4 * n]


def _mm(a, b, *, name, ta=False, out_dtype=F32, res=None, bm=1024, bn=1024, bk=4096, b_stack=False, out_stack=False,
        ex=None):
    if ta:
        kdim, m = a.shape
    else:
        m, kdim = a.shape
    if b_stack:
        _, kb, chunk = b.shape
        n = N_CHIPS * chunk
    else:
        kb, n = b.shape
        chunk = n // N_CHIPS if out_stack else n
    assert kdim == kb, (a.shape, b.shape, ta)
    bm = _block(m, bm, LANES if ta else 16)
    bn = _block(chunk, bn, LANES)
    bk = _block(kdim, bk, LANES)
    nk = kdim // bk
    per_chunk = chunk // bn
    dims = (((0 if ta else 1,), (0,)), ((), ()))

    def body(*refs):
        refs = list(refs)
        a_ref, b_ref = refs[:2]
        r_ref = refs[2] if res is not None else None
        o_ref = refs[3] if res is not None else refs[2]
        part = lax.dot_general(a_ref[...].astype(BF16), b_ref[...].astype(BF16), dims, preferred_element_type=F32)

        def finish(r):
            if r_ref is not None:
                r = r + r_ref[...]
            o_ref[...] = r.astype(out_dtype)

        if nk == 1:
            finish(part)
        else:
            acc_ref = refs[-1]
            k = pl.program_id(2)

            @pl.when(k == 0)
            def _():
                acc_ref[...] = part

            @pl.when(k > 0)
            def _():
                acc_ref[...] += part

            @pl.when(k == nk - 1)
            def _():
                finish(acc_ref[...])

    a_spec = pl.BlockSpec((bk, bm), lambda i, j, k: (k, i)) if ta else pl.BlockSpec((bm, bk), lambda i, j, k: (i, k))
    if b_stack:
        b_spec = pl.BlockSpec((None, bk, bn), lambda i, j, k: (j // per_chunk, k, j % per_chunk))
    else:
        b_spec = pl.BlockSpec((bk, bn), lambda i, j, k: (k, j))
    r_spec = pl.BlockSpec((bm, bn), lambda i, j, k: (i, j))
    if out_stack:
        o_spec = pl.BlockSpec((None, bm, bn), lambda i, j, k: (j // per_chunk, i, j % per_chunk))
        o_shape = (N_CHIPS, m, chunk)
    else:
        o_spec, o_shape = r_spec, (m, n)
    in_specs = [a_spec, b_spec] + ([r_spec] if res is not None else [])
    args = (a, b) + ((res,) if res is not None else ())
    (out,), moved = _hosted_call(
        body, ex, name=name, out_shape=(jax.ShapeDtypeStruct(o_shape, out_dtype),),
        grid=(m // bm, n // bn, nk), in_specs=in_specs, out_specs=(o_spec,),
        scratch=[pltpu.VMEM((bm, bn), F32)] if nk > 1 else [], args=args)
    return out if ex is None else (out, moved)


def _rms_fwd(x, g, name, ex=None):
    t, d = x.shape
    bt = _block(t, 512, 16)

    def body(x_ref, g_ref, h_ref):
        xv = x_ref[...]
        r = lax.rsqrt(jnp.mean(xv * xv, axis=-1, keepdims=True) + EPS)
        h_ref[...] = (xv * r * g_ref[...]).astype(BF16)

    (out,), moved = _hosted_call(
        body, ex, name=name, out_shape=(jax.ShapeDtypeStruct((t, d), BF16),), grid=(t // bt,),
        in_specs=[pl.BlockSpec((bt, d), lambda i: (i, 0)), pl.BlockSpec((1, d), lambda i: (0, 0))],
        out_specs=(pl.BlockSpec((bt, d), lambda i: (i, 0)),), args=(x, g))
    return out if ex is None else (out, moved)


def _rms_bwd(dh, x, g, dres, name, ex=None):
    t, d = x.shape
    bt = _block(t, 256, 16)
    want_dx = dres is not None

    def body(*refs):
        if want_dx:
            dh_ref, x_ref, g_ref, dres_ref, dx_ref, dxb_ref, dg_ref = refs
        else:
            dh_ref, x_ref, g_ref, dg_ref = refs
        xv = x_ref[...]
        r = lax.rsqrt(jnp.mean(xv * xv, axis=-1, keepdims=True) + EPS)
        xhat = xv * r
        dhv = dh_ref[...]

        @pl.when(pl.program_id(0) == 0)
        def _():
            dg_ref[...] = jnp.zeros_like(dg_ref)

        dg_ref[...] += jnp.sum(dhv * xhat, axis=0, keepdims=True)
        if want_dx:
            dxhat = dhv * g_ref[...]
            dx = dres_ref[...] + r * (dxhat - xhat * jnp.mean(dxhat * xhat, axis=-1, keepdims=True))
            dx_ref[...] = dx
            dxb_ref[...] = dx.astype(BF16)

    row = pl.BlockSpec((bt, d), lambda i: (i, 0))
    vec = pl.BlockSpec((1, d), lambda i: (0, 0))
    if want_dx:
        outs, moved = _hosted_call(
            body, ex, name=name, grid=(t // bt,),
            out_shape=(jax.ShapeDtypeStruct((t, d), F32), jax.ShapeDtypeStruct((t, d), BF16),
                       jax.ShapeDtypeStruct((1, d), F32)),
            in_specs=[row, row, vec, row], out_specs=(row, row, vec), args=(dh, x, g, dres))
        return tuple(outs) if ex is None else (tuple(outs), moved)
    return pl.pallas_call(
        body, name=name, grid=(t // bt,), out_shape=jax.ShapeDtypeStruct((1, d), F32),
        in_specs=[row, row, vec], out_specs=vec,
        compiler_params=_cparams(("arbitrary",)),
    )(dh, x, g)


def _in_proj_attn(h, w_qkv, w_f_lanes, w_f):
    t, d = h.shape
    bm = _block(t, 512, 16)

    def body(h_ref, wq_ref, wl_ref, wf_ref, qkv_ref, fl_ref, f_ref):
        hv = h_ref[...]
        qkv_ref[...] = jnp.dot(hv, wq_ref[...], preferred_element_type=F32).astype(BF16)
        fl_ref[...] = jnp.dot(hv, wl_ref[...], preferred_element_type=F32)
        f_ref[...] = jnp.dot(hv, wf_ref[...], preferred_element_type=F32)

    whole = lambda w: pl.BlockSpec(w.shape, lambda i: (0, 0))
    rows = lambda n: pl.BlockSpec((bm, n), lambda i: (i, 0))
    return pl.pallas_call(
        body, name="in_proj_attn", grid=(t // bm,),
        out_shape=(jax.ShapeDtypeStruct((t, w_qkv.shape[1]), BF16), jax.ShapeDtypeStruct((t, w_f_lanes.shape[1]), F32),
                   jax.ShapeDtypeStruct((t, w_f.shape[1]), F32)),
        in_specs=[rows(d), whole(w_qkv), whole(w_f_lanes), whole(w_f)],
        out_specs=(rows(w_qkv.shape[1]), rows(w_f_lanes.shape[1]), rows(w_f.shape[1])),
        compiler_params=_cparams(("parallel",)),
    )(h, w_qkv, w_f_lanes, w_f)


def _mm_res_norm(a, b, res, g, name):
    t, k = a.shape
    d = b.shape[1]
    bm = _block(t, 512, 16)

    def body(a_ref, b_ref, r_ref, g_ref, x_ref, h_ref):
        xv = jnp.dot(a_ref[...], b_ref[...], preferred_element_type=F32) + r_ref[...]
        x_ref[...] = xv
        r = lax.rsqrt(jnp.mean(xv * xv, axis=-1, keepdims=True) + EPS)
        h_ref[...] = (xv * r * g_ref[...]).astype(BF16)

    row = pl.BlockSpec((bm, d), lambda i: (i, 0))
    return pl.pallas_call(
        body, name=name, grid=(t // bm,),
        out_shape=(jax.ShapeDtypeStruct((t, d), F32), jax.ShapeDtypeStruct((t, d), BF16)),
        in_specs=[pl.BlockSpec((bm, k), lambda i: (i, 0)), pl.BlockSpec((k, d), lambda i: (0, 0)), row,
                  pl.BlockSpec((1, d), lambda i: (0, 0))],
        out_specs=(row, row), compiler_params=_cparams(("parallel",)),
    )(a, b, res, g)


def _ffn_out_loss(act, w, res, target, g):
    t, k = act.shape
    d = w.shape[1]
    bm = _block(t, 512, 16)

    def body(a_ref, w_ref, r_ref, t_ref, g_ref, dx_ref, dxb_ref, dg_ref, loss_ref):
        xv = jnp.dot(a_ref[...], w_ref[...], preferred_element_type=F32) + r_ref[...]
        gv = g_ref[...]
        r = lax.rsqrt(jnp.mean(xv * xv, axis=-1, keepdims=True) + EPS)
        xhat = xv * r
        err = xhat * gv - t_ref[...]

        @pl.when(pl.program_id(0) == 0)
        def _():
            dg_ref[...] = jnp.zeros_like(dg_ref)
            loss_ref[...] = jnp.zeros_like(loss_ref)

        loss_ref[...] += 0.5 * jnp.sum(jnp.mean(err * err, axis=-1, keepdims=True), axis=0, keepdims=True)
        dy = err * (1.0 / d)
        dg_ref[...] += jnp.sum(dy * xhat, axis=0, keepdims=True)
        dxhat = dy * gv
        dx = r * (dxhat - xhat * jnp.mean(dxhat * xhat, axis=-1, keepdims=True))
        dx_ref[...] = dx
        dxb_ref[...] = dx.astype(BF16)

    row = pl.BlockSpec((bm, d), lambda i: (i, 0))
    vec = pl.BlockSpec((1, d), lambda i: (0, 0))
    return pl.pallas_call(
        body, name="ffn_out_loss", grid=(t // bm,),
        out_shape=(jax.ShapeDtypeStruct((t, d), F32), jax.ShapeDtypeStruct((t, d), BF16),
                   jax.ShapeDtypeStruct((1, d), F32), jax.ShapeDtypeStruct((1, LANES), F32)),
        in_specs=[pl.BlockSpec((bm, k), lambda i: (i, 0)), pl.BlockSpec((k, d), lambda i: (0, 0)), row, row, vec],
        out_specs=(row, row, vec, pl.BlockSpec((1, LANES), lambda i: (0, 0))),
        compiler_params=_cparams(("arbitrary",)),
    )(act, w, res, target, g)


def _mm_norm_bwd(a, b, x, g, dres, name, ex=None):
    t, k = a.shape
    d = b.shape[1]
    bm = _block(t, 512 if k <= 2048 else 256, 16)

    def body(a_ref, b_ref, x_ref, g_ref, dres_ref, dx_ref, dxb_ref, dg_ref):
        @pl.when(pl.program_id(0) == 0)
        def _():
            dg_ref[...] = jnp.zeros_like(dg_ref)

        dhv = jnp.dot(a_ref[...], b_ref[...], preferred_element_type=F32)
        xv = x_ref[...]
        r = lax.rsqrt(jnp.mean(xv * xv, axis=-1, keepdims=True) + EPS)
        xhat = xv * r
        dg_ref[...] += jnp.sum(dhv * xhat, axis=0, keepdims=True)
        dxhat = dhv * g_ref[...]
        dx = dres_ref[...] + r * (dxhat - xhat * jnp.mean(dxhat * xhat, axis=-1, keepdims=True))
        dx_ref[...] = dx
        dxb_ref[...] = dx.astype(BF16)

    row = pl.BlockSpec((bm, d), lambda i: (i, 0))
    vec = pl.BlockSpec((1, d), lambda i: (0, 0))
    outs, moved = _hosted_call(
        body, ex, name=name, grid=(t // bm,),
        out_shape=(jax.ShapeDtypeStruct((t, d), F32), jax.ShapeDtypeStruct((t, d), BF16), jax.ShapeDtypeStruct((1, d), F32)),
        in_specs=[pl.BlockSpec((bm, k), lambda i: (i, 0)), pl.BlockSpec((k, d), lambda i: (0, 0)), row, vec, row],
        out_specs=(row, row, vec), args=(a, b, x, g, dres))
    return tuple(outs) if ex is None else (tuple(outs), moved)


GU_COLS = GATE_WIDTH + POOL_WIDTH
U_BLK = GATE_WIDTH // POOL_WIDTH


def _shift_down(a, k, row):
    return jnp.where(row >= k, pltpu.roll(a, k, 0), 0.0)


def _shift_up(a, k, row):
    n = a.shape[0]
    return jnp.where(row < n - k, pltpu.roll(a, n - k, 0), 0.0)


def _window_delta(u, w, row):
    s, k = u, 1
    while k < w:
        s = s + _shift_down(s, k, row)
        k *= 2
    cnt = jnp.minimum(row + 1, w).astype(F32)
    return s / cnt - u, cnt


def _pool_fwd(gu, pool_w, pool_scale):
    b, s, _ = gu.shape

    def body(u_ref, pw_ref, sc_ref, y_ref):
        row = lax.broadcasted_iota(jnp.int32, (s, POOL_GC), 0)
        for g, w in enumerate(POOL_WINDOWS):
            cols = slice(g * POOL_GC, (g + 1) * POOL_GC)
            d, _ = _window_delta(u_ref[0, :, cols].astype(F32), w, row)
            z = jnp.dot(d.astype(BF16), pw_ref[g].astype(BF16), preferred_element_type=F32)
            y_ref[0, :, cols] = (z * sc_ref[:, cols]).astype(BF16)

    return pl.pallas_call(
        body, name="pool_fwd", out_shape=jax.ShapeDtypeStruct((b, s, POOL_WIDTH), BF16), grid=(b,),
        in_specs=[pl.BlockSpec((1, s, POOL_WIDTH), lambda i: (i, 0, U_BLK)),
                  pl.BlockSpec((4, POOL_GC, POOL_GC), lambda i: (0, 0, 0)),
                  pl.BlockSpec((1, POOL_WIDTH), lambda i: (0, 0))],
        out_specs=pl.BlockSpec((1, s, POOL_WIDTH), lambda i: (i, 0, 0)),
        compiler_params=_cparams(("parallel",)),
    )(gu, pool_w, pool_scale)


def _pool_bwd(gu, dyp, w_out_t, pool_w, pool_scale, dgu):
    b, s, _ = gu.shape

    def body(u_ref, dyp_ref, w_ref, pw_ref, sc_ref, dgu_in, du_ref, dpw_ref, dsc_ref):
        del dgu_in

        @pl.when(pl.program_id(0) == 0)
        def _():
            dpw_ref[...] = jnp.zeros_like(dpw_ref)
            dsc_ref[...] = jnp.zeros_like(dsc_ref)

        row = lax.broadcasted_iota(jnp.int32, (s, POOL_GC), 0)
        for g, w in enumerate(POOL_WINDOWS):
            cols = slice(g * POOL_GC, (g + 1) * POOL_GC)
            d, cnt = _window_delta(u_ref[0, :, cols].astype(F32), w, row)
            db = d.astype(BF16)
            pw = pw_ref[g].astype(BF16)
            z = jnp.dot(db, pw, preferred_element_type=F32)
            dyv = jnp.dot(dyp_ref[0], w_ref[:, cols], preferred_element_type=F32)
            dsc_ref[:, cols] += jnp.sum(dyv * z, axis=0, keepdims=True)
            dz = (dyv * sc_ref[:, cols]).astype(BF16)
            dpw_ref[g] += lax.dot_general(db, dz, (((0,), (0,)), ((), ())), preferred_element_type=F32)
            dd = lax.dot_general(dz, pw, (((1,), (1,)), ((), ())), preferred_element_type=F32)
            acc, k = dd / cnt, 1
            while k < w:
                acc = acc + _shift_up(acc, k, row)
                k *= 2
            du_ref[0, :, cols] = (acc - dd).astype(BF16)

    return pl.pallas_call(
        body, name="pool_bwd", grid=(b,),
        out_shape=(jax.ShapeDtypeStruct((b, s, GU_COLS), BF16), jax.ShapeDtypeStruct((4, POOL_GC, POOL_GC), F32),
                   jax.ShapeDtypeStruct((1, POOL_WIDTH), F32)),
        in_specs=[pl.BlockSpec((1, s, POOL_WIDTH), lambda i: (i, 0, U_BLK)),
                  pl.BlockSpec((1, s, dyp.shape[2]), lambda i: (i, 0, 0)),
                  pl.BlockSpec(w_out_t.shape, lambda i: (0, 0)),
                  pl.BlockSpec((4, POOL_GC, POOL_GC), lambda i: (0, 0, 0)),
                  pl.BlockSpec((1, POOL_WIDTH), lambda i: (0, 0)), ANY],
        out_specs=(pl.BlockSpec((1, s, POOL_WIDTH), lambda i: (i, 0, U_BLK)),
                   pl.BlockSpec((4, POOL_GC, POOL_GC), lambda i: (0, 0, 0)),
                   pl.BlockSpec((1, POOL_WIDTH), lambda i: (0, 0))),
        input_output_aliases={5: 0},
        compiler_params=_cparams(("arbitrary",)),
    )(gu, dyp, w_out_t, pool_w, pool_scale, dgu)


def _forget_cumsum(f, bias, name):
    b, s, c = f.shape

    def body(f_ref, b_ref, c_ref):
        row = lax.broadcasted_iota(jnp.int32, (s, LANES), 0)
        z = f_ref[0] + b_ref[...]
        acc = jnp.minimum(z, 0.0) - jnp.log(1.0 + jnp.exp(-jnp.abs(z)))
        k = 1
        while k < s:
            acc = acc + _shift_down(acc, k, row)
            k *= 2
        c_ref[0] = acc

    return pl.pallas_call(
        body, name=name, out_shape=jax.ShapeDtypeStruct((b, s, c), F32), grid=(b, c // LANES),
        in_specs=[pl.BlockSpec((1, s, LANES), lambda i, j: (i, 0, j)), pl.BlockSpec((1, LANES), lambda i, j: (0, j))],
        out_specs=pl.BlockSpec((1, s, LANES), lambda i, j: (i, 0, j)),
        compiler_params=_cparams(("parallel", "parallel")),
    )(f, bias)


def _forget_bwd(dc, f, bias):
    b, s, _ = f.shape

    def body(dc_ref, f_ref, b_ref, df_ref, db_ref):
        @pl.when(pl.program_id(0) == 0)
        def _():
            db_ref[...] = jnp.zeros_like(db_ref)

        row = lax.broadcasted_iota(jnp.int32, (s, LANES), 0)
        acc, k = dc_ref[0], 1
        while k < s:
            acc = acc + _shift_up(acc, k, row)
            k *= 2
        z = f_ref[0] + b_ref[...]
        df = acc / (1.0 + jnp.exp(z))
        db_ref[...] += jnp.sum(df, axis=0, keepdims=True)
        df_ref[0] = df.astype(BF16)

    blk = pl.BlockSpec((1, s, LANES), lambda i: (i, 0, 0))
    vec = pl.BlockSpec((1, LANES), lambda i: (0, 0))
    return pl.pallas_call(
        body, name="forget_bwd", grid=(b,),
        out_shape=(jax.ShapeDtypeStruct((b, s, LANES), BF16), jax.ShapeDtypeStruct((1, LANES), F32)),
        in_specs=[blk, blk, vec], out_specs=(blk, vec),
        compiler_params=_cparams(("arbitrary",)),
    )(dc, f, bias)


KV_BLK0 = 2
PAIRS = FOX_HEADS // 2
FOX_SCALE = FOX_DH ** -0.5
NT_DIMS = (((1,), (1,)), ((), ()))
TN_DIMS = (((0,), (0,)), ((), ()))


def _stack_heads(v):
    head = lax.broadcasted_iota(jnp.int32, v.shape, 1) // FOX_DH
    zero = jnp.zeros_like(v)
    return jnp.concatenate([jnp.where(head == 0, v, zero), jnp.where(head == 1, v, zero)], axis=0)


def _stack_cols(v):
    return jnp.concatenate([v[:, 0:1], v[:, FOX_DH:FOX_DH + 1]], axis=0)


def _unstack(t, blk):
    head = lax.broadcasted_iota(jnp.int32, (blk, LANES), 1) // FOX_DH
    return jnp.where(head == 0, t[:blk], t[blk:])


def _fox_scores(q_all, kblk, row_bias, cr_ref, kb, masked, blk):
    top = lax.broadcasted_iota(jnp.int32, (2 * blk, 1), 0) < blk
    s = lax.dot_general(q_all, kblk, NT_DIMS, preferred_element_type=F32)
    s = s + (row_bias - jnp.where(top, cr_ref[0, 0, kb], cr_ref[0, 1, kb]))
    if masked:
        r = lax.broadcasted_iota(jnp.int32, (2 * blk, blk), 0)
        keep = jnp.where(r >= blk, r - blk, r) >= lax.broadcasted_iota(jnp.int32, (2 * blk, blk), 1)
        s = jnp.where(keep, s, NEG_INF)
    return s


def _fox_fwd(qkv, c_exp, c_row, ex=None):
    b, s, _ = qkv.shape
    blk = min(ATT_BLOCK, s)
    nq = s // blk

    def body(q_ref, kv_ref, cc_ref, cr_ref, o_ref, ob_ref, lse_ref):
        qi = pl.program_id(2)
        q_all = _stack_heads(q_ref[0] * FOX_SCALE)
        cq = _stack_cols(cc_ref[0])

        def step(kb, carry, masked):
            m, l, acc = carry
            rows = pl.ds(pl.multiple_of(kb * blk, blk), blk)
            sc = _fox_scores(q_all, kv_ref[0, rows, :LANES], cq, cr_ref, kb, masked, blk)
            m_new = jnp.maximum(m, jnp.max(sc, axis=-1, keepdims=True))
            p = jnp.exp(sc - m_new)
            alpha = jnp.exp(m - m_new)
            l = alpha * l + jnp.sum(p, axis=-1, keepdims=True)
            acc = alpha * acc + jnp.dot(p.astype(BF16), kv_ref[0, rows, LANES:], preferred_element_type=F32)
            return m_new, l, acc

        init = (jnp.full((2 * blk, 1), NEG_INF, F32), jnp.zeros((2 * blk, 1), F32), jnp.zeros((2 * blk, LANES), F32))
        m, l, acc = step(qi, lax.fori_loop(0, qi, functools.partial(step, masked=False), init), True)
        o = _unstack(acc / l, blk)
        o_ref[0] = o
        ob_ref[0] = o.astype(BF16)
        lse_ref[0] = _unstack(jnp.broadcast_to(m + jnp.log(l), (2 * blk, LANES)), blk)

    tile = pl.BlockSpec((1, blk, LANES), lambda i, h, q: (i, q, h))
    kvspec = pl.BlockSpec((1, s, 2 * LANES), lambda i, h, q: (i, 0, KV_BLK0 + h))
    shape = jax.ShapeDtypeStruct((b, s, FOX_WIDTH), F32)
    return _hosted_call(
        body, ex, name="fox_fwd", out_shape=(shape, jax.ShapeDtypeStruct((b, s, FOX_WIDTH), BF16), shape),
        grid=(b, PAIRS, nq),
        in_specs=[tile, kvspec, tile, pl.BlockSpec((1, 2, nq, 1, blk), lambda i, h, q: (i, h, 0, 0, 0))],
        out_specs=(tile, tile, tile), args=(qkv, qkv, c_exp, c_row))


def _fox_bwd(qkv, c_exp, c_row, lse, o, dy, w_out_t, ex=None):
    b, s, _ = qkv.shape
    blk = min(ATT_BLOCK, s)
    nq = s // blk

    def body(q_ref, kv_ref, cc_ref, cr_ref, lse_ref, o_ref, dy_ref, w_ref, dq_ref, dkv_ref, dcq_ref, dc_ref, dk_acc, dv_acc):
        qi = pl.program_id(2)

        @pl.when(qi == 0)
        def _():
            dk_acc[...] = jnp.zeros_like(dk_acc)
            dv_acc[...] = jnp.zeros_like(dv_acc)
            dc_ref[...] = jnp.zeros_like(dc_ref)

        q_all = _stack_heads(q_ref[0] * FOX_SCALE)
        dov = jnp.dot(dy_ref[0], w_ref[...], preferred_element_type=F32)
        do_all = _stack_heads(dov.astype(BF16))
        delta = jnp.sum(_stack_heads(dov * o_ref[0]), axis=-1, keepdims=True)
        bias = _stack_cols(cc_ref[0]) - _stack_cols(lse_ref[0])

        def step(kb, carry, masked):
            acc, dcq = carry
            rows = pl.ds(pl.multiple_of(kb * blk, blk), blk)
            kblk = kv_ref[0, rows, :LANES]
            p = jnp.exp(_fox_scores(q_all, kblk, bias, cr_ref, kb, masked, blk))
            dp = lax.dot_general(do_all, kv_ref[0, rows, LANES:], NT_DIMS, preferred_element_type=F32)
            ds = p * (dp - delta)
            dsb = ds.astype(BF16)
            dv_acc[rows, :] += lax.dot_general(p.astype(BF16), do_all, TN_DIMS, preferred_element_type=F32)
            dk_acc[rows, :] += lax.dot_general(dsb, q_all, TN_DIMS, preferred_element_type=F32)
            dc_ref[0, 0, kb] -= jnp.sum(ds[:blk], axis=0, keepdims=True)
            dc_ref[0, 1, kb] -= jnp.sum(ds[blk:], axis=0, keepdims=True)
            acc = acc + jnp.dot(dsb, kblk, preferred_element_type=F32)
            return acc, dcq + jnp.sum(ds, axis=-1, keepdims=True)

        init = (jnp.zeros((2 * blk, LANES), F32), jnp.zeros((2 * blk, 1), F32))
        acc, dcq = step(qi, lax.fori_loop(0, qi, functools.partial(step, masked=False), init), True)
        dq_ref[0] = (_unstack(acc, blk) * FOX_SCALE).astype(BF16)
        dcq_ref[0, 0] = jnp.where(lax.broadcasted_iota(jnp.int32, (blk, 2), 1) == 0, dcq[:blk], dcq[blk:])

        @pl.when(qi == nq - 1)
        def _():
            dkv_ref[0, :, :LANES] = dk_acc[...].astype(BF16)
            dkv_ref[0, :, LANES:] = dv_acc[...].astype(BF16)

    tile = pl.BlockSpec((1, blk, LANES), lambda i, h, q: (i, q, h))
    kvspec = pl.BlockSpec((1, s, 2 * LANES), lambda i, h, q: (i, 0, KV_BLK0 + h))
    crow = pl.BlockSpec((1, 2, nq, 1, blk), lambda i, h, q: (i, h, 0, 0, 0))
    return _hosted_call(
        body, ex, name="fox_bwd", grid=(b, PAIRS, nq),
        out_shape=(jax.ShapeDtypeStruct((b, s, FOX_WIDTH), BF16), jax.ShapeDtypeStruct((b, s, 2 * FOX_WIDTH), BF16),
                   jax.ShapeDtypeStruct((b, PAIRS, s, 2), F32), jax.ShapeDtypeStruct(c_row.shape, F32)),
        in_specs=[tile, kvspec, tile, crow, tile, tile,
                  pl.BlockSpec((1, blk, dy.shape[2]), lambda i, h, q: (i, q, 0)),
                  pl.BlockSpec((w_out_t.shape[0], LANES), lambda i, h, q: (0, h))],
        out_specs=(tile, pl.BlockSpec((1, s, 2 * LANES), lambda i, h, q: (i, 0, h)),
                   pl.BlockSpec((1, 1, blk, 2), lambda i, h, q: (i, h, q, 0)), crow),
        scratch=[pltpu.VMEM((s, LANES), F32), pltpu.VMEM((s, LANES), F32)],
        args=(qkv, qkv, c_exp, c_row, lse, o, dy, w_out_t))


def _sigmoid(z):
    return 1.0 / (1.0 + jnp.exp(-z))


def _branches_mix(y, o, w_pool3, w_fox3, gu, b_gate):
    t = y.shape[0]
    chunk = w_pool3.shape[2]
    per_branch = D_MODEL // chunk
    bm = _block(t, 1024, 16)

    def body(y_ref, o_ref, wp_ref, wf_ref, gp_ref, gf_ref, bp_ref, bf_ref, yp_ref, yf_ref, mix_ref):
        yp = jnp.dot(y_ref[...], wp_ref[...], preferred_element_type=F32).astype(BF16)
        yf = jnp.dot(o_ref[...], wf_ref[...], preferred_element_type=F32).astype(BF16)
        yp_ref[...] = yp
        yf_ref[...] = yf
        gp = _sigmoid(gp_ref[...].astype(F32) + bp_ref[...])
        gf = _sigmoid(gf_ref[...].astype(F32) + bf_ref[...])
        mix_ref[...] = (gp * yp.astype(F32) + gf * yf.astype(F32)).astype(BF16)

    rows = pl.BlockSpec((bm, y.shape[1]), lambda i, j: (i, 0))
    weight = pl.BlockSpec((None, y.shape[1], chunk), lambda i, j: (j, 0, 0))
    tile = lambda base: pl.BlockSpec((bm, chunk), lambda i, j: (i, base + j))
    vec = lambda base: pl.BlockSpec((1, chunk), lambda i, j: (0, base + j))
    shape = jax.ShapeDtypeStruct((t, D_MODEL), BF16)
    return pl.pallas_call(
        body, name="branches_mix", out_shape=(shape, shape, shape), grid=(t // bm, per_branch),
        in_specs=[rows, rows, weight, weight, tile(0), tile(per_branch), vec(0), vec(per_branch)],
        out_specs=(tile(0), tile(0), tile(0)),
        compiler_params=_cparams(("parallel", "arbitrary")),
    )(y, o, w_pool3, w_fox3, gu, gu, b_gate, b_gate)


def _mix_bwd(gu, b_gate, y_pool, y_fox, dx, w_out_t):
    t = gu.shape[0]
    bt = _block(t, 256, 16)

    def body(gp_ref, gf_ref, bp_ref, bf_ref, yp_ref, yf_ref, dx_ref, w_ref, dyp_ref, dyf_ref, dgl_ref, db_ref):
        @pl.when(pl.program_id(0) == 0)
        def _():
            db_ref[...] = jnp.zeros_like(db_ref)

        dm = jnp.dot(dx_ref[...], w_ref[...], preferred_element_type=F32)
        gp = _sigmoid(gp_ref[...].astype(F32) + bp_ref[...])
        gf = _sigmoid(gf_ref[...].astype(F32) + bf_ref[...])
        dyp_ref[...] = (dm * gp).astype(BF16)
        dyf_ref[...] = (dm * gf).astype(BF16)
        dlp = dm * yp_ref[...].astype(F32) * gp * (1.0 - gp)
        dlf = dm * yf_ref[...].astype(F32) * gf * (1.0 - gf)
        dgl_ref[:, :D_MODEL] = dlp.astype(BF16)
        dgl_ref[:, D_MODEL:] = dlf.astype(BF16)
        db_ref[:, :D_MODEL] += jnp.sum(dlp, axis=0, keepdims=True)
        db_ref[:, D_MODEL:] += jnp.sum(dlf, axis=0, keepdims=True)

    col = lambda j: pl.BlockSpec((bt, D_MODEL), lambda i: (i, j))
    vec = lambda j: pl.BlockSpec((1, D_MODEL), lambda i: (0, j))
    wide = pl.BlockSpec((bt, GATE_WIDTH), lambda i: (i, 0))
    return pl.pallas_call(
        body, name="mix_bwd", grid=(t // bt,),
        out_shape=(jax.ShapeDtypeStruct((t, D_MODEL), BF16), jax.ShapeDtypeStruct((t, D_MODEL), BF16),
                   jax.ShapeDtypeStruct((t, GU_COLS), BF16), jax.ShapeDtypeStruct((1, GATE_WIDTH), F32)),
        in_specs=[col(0), col(1), vec(0), vec(1), col(0), col(0), col(0),
                  pl.BlockSpec(w_out_t.shape, lambda i: (0, 0))],
        out_specs=(col(0), col(0), wide, pl.BlockSpec((1, GATE_WIDTH), lambda i: (0, 0))),
        compiler_params=_cparams(("arbitrary",)),
    )(gu, gu, b_gate, b_gate, y_pool, y_fox, dx, w_out_t)


X_SCALE = X_DH ** -0.5


def _xattn_probs(qh, kh):
    s = lax.dot_general(qh, kh, NT_DIMS, preferred_element_type=F32) * X_SCALE
    e = jnp.exp(s - jnp.max(s, axis=-1, keepdims=True))
    return e / jnp.sum(e, axis=-1, keepdims=True)


def _xattn_fwd(q, kv):
    b, s, _ = q.shape
    m = kv.shape[1]
    bq = _block(s, 512, 16)

    def body(q_ref, kv_ref, o_ref):
        for h in range(X_HEADS):
            cols = slice(h * X_DH, (h + 1) * X_DH)
            p = _xattn_probs(q_ref[0, :, cols], kv_ref[0, :, cols])
            vh = kv_ref[0, :, X_WIDTH + h * X_DH:X_WIDTH + (h + 1) * X_DH]
            o_ref[0, :, cols] = jnp.dot(p.astype(BF16), vh, preferred_element_type=F32).astype(BF16)

    return pl.pallas_call(
        body, name="xattn_fwd", out_shape=jax.ShapeDtypeStruct((b, s, X_WIDTH), BF16), grid=(b, s // bq),
        in_specs=[pl.BlockSpec((1, bq, X_WIDTH), lambda i, j: (i, j, 0)),
                  pl.BlockSpec((1, m, 2 * X_WIDTH), lambda i, j: (i, 0, 0))],
        out_specs=pl.BlockSpec((1, bq, X_WIDTH), lambda i, j: (i, j, 0)),
        compiler_params=_cparams(("parallel", "parallel")),
    )(q, kv)


def _xattn_bwd(q, kv, dx, w_o_t):
    b, s, _ = q.shape
    m = kv.shape[1]
    bq = _block(s, 512, 16)

    def body(q_ref, kv_ref, dx_ref, w_ref, dq_ref, dkv_ref):
        @pl.when(pl.program_id(1) == 0)
        def _():
            dkv_ref[...] = jnp.zeros_like(dkv_ref)

        do = jnp.dot(dx_ref[0], w_ref[...], preferred_element_type=F32).astype(BF16)
        for h in range(X_HEADS):
            cols = slice(h * X_DH, (h + 1) * X_DH)
            vcols = slice(X_WIDTH + h * X_DH, X_WIDTH + (h + 1) * X_DH)
            qh, kh, vh, doh = q_ref[0, :, cols], kv_ref[0, :, cols], kv_ref[0, :, vcols], do[:, cols]
            p = _xattn_probs(qh, kh)
            dkv_ref[0, :, vcols] += lax.dot_general(p.astype(BF16), doh, TN_DIMS, preferred_element_type=F32)
            dp = lax.dot_general(doh, vh, NT_DIMS, preferred_element_type=F32)
            ds = (p * (dp - jnp.sum(p * dp, axis=-1, keepdims=True)) * X_SCALE).astype(BF16)
            dq_ref[0, :, cols] = jnp.dot(ds, kh, preferred_element_type=F32).astype(BF16)
            dkv_ref[0, :, cols] += lax.dot_general(ds, qh, TN_DIMS, preferred_element_type=F32)

    tile = pl.BlockSpec((1, bq, X_WIDTH), lambda i, j: (i, j, 0))
    mem = pl.BlockSpec((1, m, 2 * X_WIDTH), lambda i, j: (i, 0, 0))
    return pl.pallas_call(
        body, name="xattn_bwd", grid=(b, s // bq),
        out_shape=(jax.ShapeDtypeStruct((b, s, X_WIDTH), BF16), jax.ShapeDtypeStruct((b, m, 2 * X_WIDTH), F32)),
        in_specs=[tile, mem, pl.BlockSpec((1, bq, dx.shape[2]), lambda i, j: (i, j, 0)),
                  pl.BlockSpec(w_o_t.shape, lambda i, j: (0, 0))],
        out_specs=(tile, mem),
        compiler_params=_cparams(("parallel", "arbitrary")),
    )(q, kv, dx, w_o_t)


def _ffn_in(hf, w3):
    t, d = hf.shape
    chunk = w3.shape[2]
    half = N_CHIPS // 2
    bm = _block(t, 1024, 16)

    def body(a_ref, wg_ref, wu_ref, gt_ref, up_ref, act_ref):
        a = a_ref[...]
        gt = jnp.dot(a, wg_ref[...], preferred_element_type=F32).astype(BF16)
        up = jnp.dot(a, wu_ref[...], preferred_element_type=F32).astype(BF16)
        gt_ref[...] = gt
        up_ref[...] = up
        g32 = gt.astype(F32)
        act_ref[...] = (g32 * _sigmoid(g32) * up.astype(F32)).astype(BF16)

    tile = pl.BlockSpec((bm, chunk), lambda i, j: (i, j))
    shape = jax.ShapeDtypeStruct((t, half * chunk), BF16)
    return pl.pallas_call(
        body, name="ffn_in", out_shape=(shape, shape, shape), grid=(t // bm, half),
        in_specs=[pl.BlockSpec((bm, d), lambda i, j: (i, 0)),
                  pl.BlockSpec((None, d, chunk), lambda i, j: (j, 0, 0)),
                  pl.BlockSpec((None, d, chunk), lambda i, j: (j + half, 0, 0))],
        out_specs=(tile, tile, tile),
        compiler_params=_cparams(("parallel", "arbitrary")),
    )(hf, w3, w3)


def _ffn_act_bwd(dx, w_out_t, gate, up):
    t, d = dx.shape
    bt = _block(t, 256, 16)

    def body(dx_ref, w_ref, gt_ref, up_ref, o_ref):
        da = jnp.dot(dx_ref[...], w_ref[...], preferred_element_type=F32).astype(BF16).astype(F32)
        gt = gt_ref[...].astype(F32)
        sg = _sigmoid(gt)
        silu = gt * sg
        o_ref[:, :D_FF] = (da * up_ref[...].astype(F32) * (sg + silu * (1.0 - sg))).astype(BF16)
        o_ref[:, D_FF:] = (da * silu).astype(BF16)

    col = pl.BlockSpec((bt, D_FF), lambda i: (i, 0))
    return pl.pallas_call(
        body, name="ffn_act_bwd", out_shape=jax.ShapeDtypeStruct((t, 2 * D_FF), BF16), grid=(t // bt,),
        in_specs=[pl.BlockSpec((bt, d), lambda i: (i, 0)), pl.BlockSpec((d, D_FF), lambda i: (0, 0)), col, col],
        out_specs=pl.BlockSpec((bt, 2 * D_FF), lambda i: (i, 0)),
        compiler_params=_cparams(("parallel",)),
    )(dx, w_out_t, gate, up)


def _stack_of(w, axis):
    r, c = w.shape
    if axis == 0:
        return w.reshape(N_CHIPS, r // N_CHIPS, c)
    return w.reshape(r, N_CHIPS, c // N_CHIPS).transpose(1, 0, 2)


def _stack_t(w3):
    n, r, c = w3.shape
    return w3.transpose(0, 2, 1).reshape(n * c, r)


def _pair_rows(k, v):
    c = k.shape[1]
    return jnp.stack([k.reshape(PAIRS, LANES, c), v.reshape(PAIRS, LANES, c)], axis=1).reshape(2 * FOX_WIDTH, c)


def _unpair_rows(kv):
    c = kv.shape[1]
    kv = kv.reshape(PAIRS, 2, LANES, c)
    return kv[:, 0].reshape(FOX_WIDTH, c), kv[:, 1].reshape(FOX_WIDTH, c)


def _input_grad(parts, weights_t, ex):
    t = parts[0].shape[0]
    d = weights_t[0].shape[1]
    bm = _block(t, 512, 16)
    n = len(parts)

    def body(*refs):
        acc = None
        for a_ref, b_ref in zip(refs[:n], refs[n:2 * n]):
            term = jnp.dot(a_ref[...], b_ref[...], preferred_element_type=F32)
            acc = term if acc is None else acc + term
        refs[2 * n][...] = acc

    (out,), moved = _hosted_call(
        body, ex, name="d_h", grid=(t // bm,), out_shape=(jax.ShapeDtypeStruct((t, d), F32),),
        in_specs=[pl.BlockSpec((bm, p.shape[1]), lambda i: (i, 0)) for p in parts]
        + [pl.BlockSpec(w.shape, lambda i: (0, 0)) for w in weights_t],
        out_specs=(pl.BlockSpec((bm, d), lambda i: (i, 0)),), args=tuple(parts) + tuple(weights_t))
    return out, moved


def _step(x, mem, loss_target, weights, moments_m, moments_v):
    nb, s, d = x.shape
    n_mem = mem.shape[1]
    t = nb * s
    blk = min(ATT_BLOCK, s)
    x2 = x.reshape(t, d)
    mem2 = mem.reshape(nb * n_mem, d)
    tgt2 = loss_target.reshape(t, d)

    def shard2d(a, n):
        a = a.reshape(a.shape[1:])
        return a.T if n == "w_in" else a

    def unshard(a, n):
        return (a.T if n == "w_in" else a)[None]

    local = {n: shard2d(weights[n], n) for n, _, _ in SHARDED}

    names = [n for n, _, _ in SHARDED]
    last = ["w_ffn_out"]
    later = [n for n in names if n != "w_in" and n not in last]
    local_b = {n: local[n].astype(BF16) for n in names}
    g_mix = weights["norm_mix_g"]
    h, w_in_others = _rms_fwd(x2, g_mix, "norm_mix", ex=_gather_exchange([local_b["w_in"]]))
    w_in_stack, = _place_own(w_in_others, [local_b["w_in"]])

    def w_in_rows(lo, hi):
        per = IN_COLS // N_CHIPS
        parts = [w_in_stack[j, max(lo, j * per) - j * per:min(hi, (j + 1) * per) - j * per]
                 for j in range(N_CHIPS) if max(lo, j * per) < min(hi, (j + 1) * per)]
        return parts[0] if len(parts) == 1 else jnp.concatenate(parts)

    w_gu_t = jnp.concatenate([w_in_rows(2056, IN_COLS), w_in_rows(0, 512)])
    w_qkv_t = jnp.concatenate([w_in_rows(512, 1024), _pair_rows(w_in_rows(1024, 1536), w_in_rows(1536, 2048))])
    w_f_t = jnp.pad(w_in_rows(2048, 2056), ((0, LANES - FOX_HEADS), (0, 0)))
    w_gu, w_qkv, w_f = w_gu_t.T, w_qkv_t.T, w_f_t.T
    w_f_exp = jnp.repeat(w_f[:, :FOX_HEADS], FOX_DH, axis=1)

    g_mix, g_x, g_mem, g_ffn = (weights[n] for n in ("norm_mix_g", "norm_x_g", "norm_mem_g", "norm_ffn_g"))
    g_final = weights["norm_final_g"].reshape(1, d)
    pool_w = weights["pool_w"].reshape(4, POOL_GC, POOL_GC)
    pool_scale, b_gate = weights["pool_scale"], weights["b_gate"]
    b_f_pad = jnp.pad(weights["b_forget"], ((0, 0), (0, LANES - FOX_HEADS)))
    b_f_exp = jnp.repeat(weights["b_forget"], FOX_DH, axis=1)

    gu, last_others = _mm(h, w_gu, out_dtype=BF16, bn=512, name="in_proj_gates_pool",
                          ex=_gather_exchange([local_b[n] for n in last]))
    qkv, f_exp, f_pad = _in_proj_attn(h, w_qkv, w_f_exp, w_f)
    gu3, qkv3 = gu.reshape(nb, s, GU_COLS), qkv.reshape(nb, s, 3 * FOX_WIDTH)
    y = _pool_fwd(gu3, pool_w, pool_scale)
    c_exp = _forget_cumsum(f_exp.reshape(nb, s, FOX_WIDTH), b_f_exp, "forget_cumsum_lanes")
    c_pad = _forget_cumsum(f_pad.reshape(nb, s, LANES), b_f_pad, "forget_cumsum")
    c_row = c_pad[:, :, :FOX_HEADS].transpose(0, 2, 1).reshape(nb, FOX_HEADS, s // blk, 1, blk)
    (o, o_b, lse), gathered = _fox_fwd(qkv3, c_exp, c_row, ex=_gather_exchange([local_b[n] for n in later]))
    stacks = dict(zip(later, _place_own(gathered, [local_b[n] for n in later])))
    stacks.update(zip(last, _place_own(last_others, [local_b[n] for n in last])))
    w_pool_out3, w_fox_out3, w_xo3, w_ffn_in3 = (stacks[n] for n in ("w_pool_out", "w_fox_out", "w_xo", "w_ffn_in"))
    w_out, w_xq, w_xkv, w_ffn_out = (stacks[n].reshape(-1, stacks[n].shape[2])
                                     for n in ("w_out", "w_xq", "w_xkv", "w_ffn_out"))
    y2, o2 = y.reshape(t, POOL_WIDTH), o_b.reshape(t, FOX_WIDTH)
    y_pool, y_fox, mix = _branches_mix(y2, o2, w_pool_out3, w_fox_out3, gu, b_gate)
    x1, hx = _mm_res_norm(mix, w_out, x2, g_x, "mix_out_norm_x")
    mem_n = _rms_fwd(mem2, g_mem, "norm_mem")
    qx = _mm(hx, w_xq, out_dtype=BF16, name="x_q")
    kv = _mm(mem_n, w_xkv, out_dtype=BF16, name="x_kv")
    qx3, kv3 = qx.reshape(nb, s, X_WIDTH), kv.reshape(nb, n_mem, 2 * X_WIDTH)
    ox = _xattn_fwd(qx3, kv3).reshape(t, X_WIDTH)
    w_xo = w_xo3.transpose(1, 0, 2).reshape(X_WIDTH, D_MODEL)
    x2_, hf = _mm_res_norm(ox, w_xo, x1, g_ffn, "x_out_norm_ffn")
    ffn_gate, ffn_up, act = _ffn_in(hf, w_ffn_in3)

    dx3, dx3_b, dg_final, loss_part = _ffn_out_loss(act, w_ffn_out, x2_, tgt2, g_final)
    dw_ffn_out = _mm(act, dx3_b, ta=True, bm=1408, bn=1024, bk=2048, name="d_w_ffn_out")
    dffn = _ffn_act_bwd(dx3_b, w_ffn_out.T, ffn_gate, ffn_up)
    dw_ffn_in = _mm(hf, dffn, ta=True, bm=1024, bn=1408, bk=2048, out_stack=True, name="d_w_ffn_in")
    core = lax.axis_index("c").astype(jnp.int32).reshape(1)
    ffn_group = ["w_ffn_in", "w_ffn_out"]
    mid_group = ["w_pool_out", "w_fox_out", "w_out", "w_xq", "w_xkv", "w_xo"]
    grad_stacks = {"w_ffn_in": dw_ffn_in, "w_ffn_out": _stack_of(dw_ffn_out, 0)}

    def presum(group, theirs):
        return [_sum_halves(grad_stacks[n], t_, core, "sum_halves_" + n) for n, t_ in zip(group, theirs)]

    (dx2, dx2_b, dg_ffn), theirs = _mm_norm_bwd(dffn, _stack_t(w_ffn_in3), x2_, g_ffn, dx3, "d_hf_norm_ffn_bwd",
                                                ex=_swap_exchange([grad_stacks[n] for n in ffn_group]))
    chip_sums = dict(zip(ffn_group, presum(ffn_group, theirs)))

    dw_xo = _mm(ox, dx2_b, ta=True, bn=256, out_stack=True, name="d_w_xo")
    dqx, dkv = _xattn_bwd(qx3, kv3, dx2_b.reshape(nb, s, d), _stack_t(w_xo3))
    dqx2, dkv2 = dqx.reshape(t, X_WIDTH), dkv.reshape(nb * n_mem, 2 * X_WIDTH)
    dw_xkv = _mm(mem_n, dkv2, ta=True, name="d_w_xkv")
    dmem_n = _mm(dkv2, w_xkv.T, name="d_mem_n")
    dg_mem = _rms_bwd(dmem_n, mem2, g_mem, None, "norm_mem_bwd")
    dw_xq = _mm(hx, dqx2, ta=True, name="d_w_xq")
    dx1, dx1_b, dg_x = _mm_norm_bwd(dqx2, w_xq.T, x1, g_x, dx2, "d_hx_norm_x_bwd")

    dw_out = _mm(mix, dx1_b, ta=True, name="d_w_out")
    dyp, dyf, dgu, db_gate = _mix_bwd(gu, b_gate, y_pool, y_fox, dx1_b, w_out.T)
    dw_pool_out = _mm(y2, dyp, ta=True, bn=256, out_stack=True, name="d_w_pool_out")
    dw_fox_out = _mm(o2, dyf, ta=True, bn=256, out_stack=True, name="d_w_fox_out")
    dgu3, dpool_w, dpool_scale = _pool_bwd(gu3, dyp.reshape(nb, s, d), _stack_t(w_pool_out3), pool_w, pool_scale,
                                           dgu.reshape(nb, s, GU_COLS))
    grad_stacks.update({"w_pool_out": dw_pool_out, "w_fox_out": dw_fox_out, "w_out": _stack_of(dw_out, 0),
                        "w_xq": _stack_of(dw_xq, 0), "w_xkv": _stack_of(dw_xkv, 0), "w_xo": dw_xo})
    dgu2 = dgu3.reshape(t, GU_COLS)
    dw_gu_t, theirs = _mm(dgu2, h, ta=True, name="d_w_gates_pool",
                          ex=_swap_exchange([grad_stacks[n] for n in mid_group]))
    chip_sums.update(zip(mid_group, presum(mid_group, theirs)))
    early = ffn_group + mid_group
    (dq3, dkv3, dc_q, dc_row), early_slots = _fox_bwd(qkv3, c_exp, c_row, lse, o, dyf.reshape(nb, s, d),
                                                      _stack_t(w_fox_out3),
                                                      ex=_chips_exchange([chip_sums[n] for n in early]))
    slots = dict(zip(early, early_slots))
    dc = dc_row.reshape(nb, FOX_HEADS, s).transpose(0, 2, 1) + dc_q.transpose(0, 2, 1, 3).reshape(nb, s, FOX_HEADS)
    dc = jnp.pad(dc, ((0, 0), (0, 0), (0, LANES - FOX_HEADS)))
    df, db_f = _forget_bwd(dc, f_pad.reshape(nb, s, LANES), b_f_pad)
    dq2, dkv2, df2 = dq3.reshape(t, FOX_WIDTH), dkv3.reshape(t, 2 * FOX_WIDTH), df.reshape(t, LANES)
    dw_q_t = _mm(dq2, h, ta=True, name="d_w_q")
    dw_kv_t = _mm(dkv2, h, ta=True, name="d_w_kv")
    dw_f_t = _mm(df2, h, ta=True, name="d_w_forget")
    dw_k_t, dw_v_t = _unpair_rows(dw_kv_t)
    dw_in_t = jnp.concatenate([dw_gu_t[GATE_WIDTH:], dw_q_t, dw_k_t, dw_v_t, dw_f_t[:FOX_HEADS],
                               dw_gu_t[:GATE_WIDTH]])
    grad_stacks["w_in"] = dw_in_t.reshape(N_CHIPS, IN_COLS // N_CHIPS, D_MODEL)
    chip_sums["w_in"], = presum(["w_in"], _run_exchange(_swap_exchange([grad_stacks["w_in"]]), "swap_halves_w_in"))
    dh, (slots["w_in"],) = _input_grad([dgu2, dq2, dkv2, df2],
                                       [w_gu_t, w_qkv_t[:FOX_WIDTH], w_qkv_t[FOX_WIDTH:], w_f_t],
                                       _chips_exchange([chip_sums["w_in"]]))

    place = jnp.stack([lax.axis_index("c"), 2 * lax.axis_index("x") + lax.axis_index("y")]).astype(jnp.int32)
    halves = [_sum_chips(slots[n], chip_sums[n], place, _by_rows(local[n].shape[0]), "sum_chips_" + n) for n in names]
    (dx, _, dg_mix), reduced = _rms_bwd(dh, x2, g_mix, dx1, "norm_mix_bwd", ex=_join_exchange(halves))

    small_grads = {"norm_mix_g": dg_mix, "b_forget": db_f[:, :FOX_HEADS], "b_gate": db_gate, "pool_w": dpool_w,
                   "pool_scale": dpool_scale, "norm_x_g": dg_x, "norm_mem_g": dg_mem, "norm_ffn_g": dg_ffn,
                   "norm_final_g": dg_final}
    def flat2d(a):
        return a.reshape(-1, a.shape[-1])

    small_names = [n for n, _ in SMALL]
    own = [flat2d(small_grads[n]) for n in small_names]
    small_gather = _small_exchange(own + [loss_part])

    def tiles_of(a):
        return a.transpose(2, 0, 1)

    def block_of(a3):
        return a3.transpose(1, 2, 0)

    grads, deltas, new_m, new_v = {}, {}, {}, {}
    gathered = None
    for n, g_ in zip(names, reduced):
        if n == "w_in":
            g_ = lax.optimization_barrier(g_.reshape(IN_COLS // N_CHIPS, 1, D_MODEL))
            (d_, m_, v_), gathered = _adamw(tiles_of(weights[n]), g_, tiles_of(moments_m[n]), tiles_of(moments_v[n]),
                                            "adamw_" + n, ex=small_gather)
            back = block_of
        else:
            d_, m_, v_ = _adamw(local[n], g_, shard2d(moments_m[n], n), shard2d(moments_v[n], n), "adamw_" + n)
            back = functools.partial(unshard, n=n)
        grads[n], deltas[n], new_m[n], new_v[n] = (back(a) for a in (g_, d_, m_, v_))

    device = (4 * lax.axis_index("x") + 2 * lax.axis_index("y") + lax.axis_index("c")).astype(jnp.int32).reshape(1)
    sg, sd, sm, sv, loss_sum = _adamw_small(
        gathered[:-1], own, [flat2d(weights[n]) for n in small_names], [flat2d(moments_m[n]) for n in small_names],
        [flat2d(moments_v[n]) for n in small_names], gathered[-1], loss_part, device)
    for n, g_, d_, m_, v_ in zip(small_names, sg, sd, sm, sv):
        grads[n], deltas[n], new_m[n], new_v[n] = (a.reshape(weights[n].shape) for a in (g_, d_, m_, v_))
    return loss_sum[0, 0], dx.reshape(nb, s, d), grads, deltas, new_m, new_v


def kernel(x, mem, norm_mix_g, w_in, b_forget, b_gate, pool_w, pool_scale, w_pool_out, w_fox_out, w_out, norm_x_g, norm_mem_g, w_xq, w_xkv, w_xo, norm_ffn_g, w_ffn_in, w_ffn_out, norm_final_g, loss_target, m_norm_mix_g, m_w_in, m_b_forget, m_b_gate, m_pool_w, m_pool_scale, m_w_pool_out, m_w_fox_out, m_w_out, m_norm_x_g, m_norm_mem_g, m_w_xq, m_w_xkv, m_w_xo, m_norm_ffn_g, m_w_ffn_in, m_w_ffn_out, m_norm_final_g, v_norm_mix_g, v_w_in, v_b_forget, v_b_gate, v_pool_w, v_pool_scale, v_w_pool_out, v_w_fox_out, v_w_out, v_norm_x_g, v_norm_mem_g, v_w_xq, v_w_xkv, v_w_xo, v_norm_ffn_g, v_w_ffn_in, v_w_ffn_out, v_norm_final_g):
    given = dict(locals())
    weights = {n: given[n] for n in WEIGHT_ORDER}
    moments_m = {n: given["m_" + n] for n in WEIGHT_ORDER}
    moments_v = {n: given["v_" + n] for n in WEIGHT_ORDER}
    loss, grad_x, grads, deltas, new_m, new_v = _step(x, mem, loss_target, weights, moments_m, moments_v)
    return (loss, grad_x, *[grads[n] for n in WEIGHT_ORDER], *[deltas[n] for n in WEIGHT_ORDER],
            *[new_m[n] for n in WEIGHT_ORDER], *[new_v[n] for n in WEIGHT_ORDER])
```
